```python
import math
import jax
import jax.numpy as jnp
from jax import lax
import numpy as np

D_MODEL = 1024
BATCH = 8
SEQ = 8192
DEPTH = 4

N_MEM = 256
N_MIXERS = 3
EPS = 1e-6

D_MIX = D_MODEL
XA_HEADS = 4
XA_HEAD_DIM = 128
D_XA = XA_HEADS * XA_HEAD_DIM
D_CAT = D_MIX + D_XA
D_GATE = D_CAT

GM_CHUNK = 128
GM_GROUPS = 8
GM_GROUP_DIM = D_MIX // GM_GROUPS

SC_WIDTH = 3

DN_HEADS = 8
DN_HEAD_DIM = D_MIX // DN_HEADS
DN_CONV = 4
DN_CHUNK = 64

N_LAYERS_A = (DEPTH + 2) // N_MIXERS
N_LAYERS_B = (DEPTH + 1) // N_MIXERS
N_LAYERS_C = DEPTH // N_MIXERS

A_IN = 2 * D_MIX + D_XA + D_GATE
B_IN = 3 * D_MIX + D_XA + D_GATE
C_IN = 3 * D_MIX + 2 * DN_HEADS + D_XA + D_GATE

kernel_name = "hybrid_gmlp_shortconv_gdn_memattn_trunk"


def rmsnorm(x, w):
    xf = x.astype(jnp.float32)
    y = xf * lax.rsqrt(jnp.mean(xf * xf, axis=-1, keepdims=True) + EPS)
    return (y * w.astype(jnp.float32)).astype(x.dtype)


def layernorm(x, w, b):
    xf = x.astype(jnp.float32)
    xc = xf - jnp.mean(xf, axis=-1, keepdims=True)
    y = xc * lax.rsqrt(jnp.mean(xc * xc, axis=-1, keepdims=True) + EPS)
    return (y * w.astype(jnp.float32) + b.astype(jnp.float32)).astype(x.dtype)


def l2norm(x):
    xf = x.astype(jnp.float32)
    return xf * lax.rsqrt(jnp.sum(xf * xf, axis=-1, keepdims=True) + EPS)


def causal_depthwise_conv(x, w):
    K = w.shape[0]
    S = x.shape[1]
    xp = jnp.pad(x, ((0, 0), (K - 1, 0), (0, 0)))
    y = xp[:, 0:S] * w[0]
    for k in range(1, K):
        y = y + xp[:, k:k + S] * w[k]
    return y


def memory_attention(q, mem_k, mem_v):
    B, S, _ = q.shape
    qh = q.reshape(B, S, XA_HEADS, XA_HEAD_DIM)
    s = jnp.einsum("bshd,bmhd->bhsm", qh, mem_k).astype(jnp.float32) * (XA_HEAD_DIM ** -0.5)
    p = jax.nn.softmax(s, axis=-1).astype(q.dtype)
    o = jnp.einsum("bhsm,bmhd->bshd", p, mem_v)
    return o.reshape(B, S, D_XA)


def chunked_causal_gmlp(u, v, ln_w, ln_b, w_s, b_s):
    B, S, _ = u.shape
    n_chunks = S // GM_CHUNK
    u = jax.nn.gelu(u)
    v = layernorm(jax.nn.gelu(v), ln_w, ln_b)
    vc = v.reshape(B, n_chunks, GM_CHUNK, GM_GROUPS, GM_GROUP_DIM)
    causal = jnp.tril(jnp.ones((GM_CHUNK, GM_CHUNK), dtype=bool))
    ws = jnp.where(causal[None], w_s, jnp.zeros_like(w_s))
    sp = jnp.einsum("gts,bnsgc->bntgc", ws, vc) + b_s.T[:, :, None]
    return u * sp.reshape(B, S, D_MIX)


def chunk_gated_delta_rule(q, k, v, g, beta):
    B, S, H, D = q.shape
    C = DN_CHUNK
    N = S // C

    def to_chunks(t):
        t = t.reshape((B, N, C, H) + t.shape[3:])
        return jnp.moveaxis(t, 3, 1)

    qc, kc, vc = to_chunks(q), to_chunks(k), to_chunks(v)
    gc, bc = to_chunks(g), to_chunks(beta)
    gcum = jnp.cumsum(gc, axis=-1)
    incl = jnp.tril(jnp.ones((C, C), dtype=bool))
    strict = jnp.tril(jnp.ones((C, C), dtype=bool), k=-1)
    decay = jnp.exp(jnp.where(incl, gcum[..., :, None] - gcum[..., None, :], -jnp.inf))
    kb = kc * bc[..., None]
    a_mat = jnp.where(strict, jnp.einsum("bhnid,bhnjd->bhnij", kb, kc) * decay, 0.0)
    eye = jnp.eye(C, dtype=q.dtype)
    t_mat = lax.linalg.triangular_solve(eye + a_mat, jnp.broadcast_to(eye, a_mat.shape),
                                        left_side=True, lower=True, unit_diagonal=True)
    u_c = jnp.einsum("bhnij,bhnjd->bhnid", t_mat, vc * bc[..., None])
    w_c = jnp.einsum("bhnij,bhnjd->bhnid", t_mat, kb * jnp.exp(gcum)[..., None])
    qk = jnp.einsum("bhnid,bhnjd->bhnij", qc, kc) * decay
    q_dec = qc * jnp.exp(gcum)[..., None]
    k_dec = kc * jnp.exp(gcum[..., -1:] - gcum)[..., None]
    g_last = jnp.exp(gcum[..., -1])
    xs = tuple(jnp.moveaxis(t, 2, 0) for t in (q_dec, qk, u_c, w_c, k_dec, g_last))

    def step(state, inp):
        q_i, qk_i, u_i, w_i, k_i, gl_i = inp
        v_new = u_i - jnp.einsum("bhcd,bhde->bhce", w_i, state)
        o_i = (jnp.einsum("bhcd,bhde->bhce", q_i, state)
               + jnp.einsum("bhij,bhje->bhie", qk_i, v_new))
        state = state * gl_i[..., None, None] + jnp.einsum("bhcd,bhce->bhde", k_i, v_new)
        return state, o_i

    state0 = jnp.zeros((B, H, D, D), dtype=q.dtype)
    _, o = lax.scan(step, state0, xs)
    o = jnp.transpose(o, (1, 0, 3, 2, 4))
    return o.reshape(B, S, H, D)


def gated_deltanet(qkv, a, b, conv_w, a_log, dt_bias, o_norm_w):
    B, S, _ = qkv.shape
    dtype = qkv.dtype
    qkv = jax.nn.silu(causal_depthwise_conv(qkv, conv_w))
    q, k, v = jnp.split(qkv, 3, axis=-1)
    shp = (B, S, DN_HEADS, DN_HEAD_DIM)
    q = l2norm(q.reshape(shp)) * (DN_HEAD_DIM ** -0.5)
    k = l2norm(k.reshape(shp))
    v = v.reshape(shp).astype(jnp.float32)
    beta = jax.nn.sigmoid(b.astype(jnp.float32))
    g = -jnp.exp(a_log.astype(jnp.float32)) * jax.nn.softplus(
        a.astype(jnp.float32) + dt_bias.astype(jnp.float32))
    o = chunk_gated_delta_rule(q, k, v, g, beta)
    o = rmsnorm(o, o_norm_w)
    return o.reshape(B, S, D_MIX).astype(dtype)


def branch_a(h, w_in, ln_w, ln_b, w_s, b_s, mem_k, mem_v):
    u, v, qx, z = jnp.split(h @ w_in, [D_MIX, 2 * D_MIX, 2 * D_MIX + D_XA], axis=-1)
    y = chunked_causal_gmlp(u, v, ln_w, ln_b, w_s, b_s)
    return jnp.concatenate([y, memory_attention(qx, mem_k, mem_v)], axis=-1) * jax.nn.silu(z)


def branch_b(h, w_in, conv_w, mem_k, mem_v):
    bg, cg, hv, qx, z = jnp.split(h @ w_in, [D_MIX, 2 * D_MIX, 3 * D_MIX, 3 * D_MIX + D_XA], axis=-1)
    y = bg * causal_depthwise_conv(cg * hv, conv_w)
    return jnp.concatenate([y, memory_attention(qx, mem_k, mem_v)], axis=-1) * jax.nn.silu(z)


def branch_c(h, w_in, conv_w, a_log, dt_bias, o_norm_w, mem_k, mem_v):
    c0 = 3 * D_MIX
    qkv, a, b, qx, z = jnp.split(
        h @ w_in, [c0, c0 + DN_HEADS, c0 + 2 * DN_HEADS, c0 + 2 * DN_HEADS + D_XA], axis=-1)
    y = gated_deltanet(qkv, a, b, conv_w, a_log, dt_bias, o_norm_w)
    return jnp.concatenate([y, memory_attention(qx, mem_k, mem_v)], axis=-1) * jax.nn.silu(z)


def _fwd_setup_inputs(seed: int = 0) -> dict:
    key = jax.random.key(seed)
    ks = jax.random.split(key, 20)
    f32 = jnp.float32

    def normal(k, shape, scale):
        return jax.random.normal(k, shape, f32) * scale

    def gain(k, shape):
        return 1.0 + 0.02 * jax.random.normal(k, shape, f32)

    a_coef = jax.random.uniform(ks[17], (N_LAYERS_C, DN_HEADS), f32, 1.0, 16.0)
    dt = jnp.exp(jax.random.uniform(ks[18], (N_LAYERS_C, DN_HEADS), f32,
                                    math.log(1e-3), math.log(1e-1)))
    return {
        "x": normal(ks[0], (BATCH, SEQ, D_MODEL), 1.0),
        "mem": normal(ks[1], (BATCH, N_MEM, D_MODEL), 1.0),
        "mem_norm_w": gain(ks[2], (D_MODEL,)),
        "w_mem_kv": normal(ks[3], (D_MODEL, 2 * D_XA), D_MODEL ** -0.5),
        "norm_pre": gain(ks[4], (DEPTH, D_MODEL)),
        "norm_post": gain(ks[5], (DEPTH, D_MODEL)),
        "w_out": normal(ks[6], (DEPTH, D_CAT, D_MODEL), D_CAT ** -0.5),
        "a_w_in": normal(ks[7], (N_LAYERS_A, D_MODEL, A_IN), D_MODEL ** -0.5),
        "a_ln_w": gain(ks[8], (N_LAYERS_A, D_MIX)),
        "a_ln_b": normal(ks[9], (N_LAYERS_A, D_MIX), 0.02),
        "a_w_s": normal(ks[10], (N_LAYERS_A, GM_GROUPS, GM_CHUNK, GM_CHUNK), 0.5 * GM_CHUNK ** -0.5),
        "a_b_s": gain(ks[11], (N_LAYERS_A, GM_GROUPS, GM_CHUNK)),
        "b_w_in": normal(ks[12], (N_LAYERS_B, D_MODEL, B_IN), D_MODEL ** -0.5),
        "b_conv_w": normal(ks[13], (N_LAYERS_B, SC_WIDTH, D_MIX), SC_WIDTH ** -0.5),
        "c_w_in": normal(ks[14], (N_LAYERS_C, D_MODEL, C_IN), D_MODEL ** -0.5),
        "c_conv_w": normal(ks[15], (N_LAYERS_C, DN_CONV, 3 * D_MIX), DN_CONV ** -0.5),
        "c_a_log": jnp.log(a_coef),
        "c_dt_bias": dt + jnp.log(-jnp.expm1(-dt)),
        "c_o_norm_w": gain(ks[16], (N_LAYERS_C, DN_HEAD_DIM)),
    }


def _fwd_reference(x, mem, mem_norm_w, w_mem_kv, norm_pre, norm_post, w_out,
              a_w_in, a_ln_w, a_ln_b, a_w_s, a_b_s,
              b_w_in, b_conv_w,
              c_w_in, c_conv_w, c_a_log, c_dt_bias, c_o_norm_w):
    B = mem.shape[0]
    mem_k, mem_v = jnp.split(rmsnorm(mem, mem_norm_w) @ w_mem_kv, 2, axis=-1)
    mem_k = mem_k.reshape(B, N_MEM, XA_HEADS, XA_HEAD_DIM)
    mem_v = mem_v.reshape(B, N_MEM, XA_HEADS, XA_HEAD_DIM)
    for i in range(DEPTH):
        kind, j = i % N_MIXERS, i // N_MIXERS
        h = rmsnorm(x, norm_pre[i])
        if kind == 0:
            y = branch_a(h, a_w_in[j], a_ln_w[j], a_ln_b[j], a_w_s[j], a_b_s[j], mem_k, mem_v)
        elif kind == 1:
            y = branch_b(h, b_w_in[j], b_conv_w[j], mem_k, mem_v)
        else:
            y = branch_c(h, c_w_in[j], c_conv_w[j], c_a_log[j], c_dt_bias[j], c_o_norm_w[j],
                         mem_k, mem_v)
        x = x + rmsnorm(y @ w_out[i], norm_post[i])
    return x


import jax as _jax
import jax.numpy as _jnp

TWIN_FORMAT = 'train_step'
FWD_PARAMS = ['x', 'mem', 'mem_norm_w', 'w_mem_kv', 'norm_pre', 'norm_post', 'w_out', 'a_w_in', 'a_ln_w', 'a_ln_b', 'a_w_s', 'a_b_s', 'b_w_in', 'b_conv_w', 'c_w_in', 'c_conv_w', 'c_a_log', 'c_dt_bias', 'c_o_norm_w']
TWIN_WEIGHTS = ['mem_norm_w', 'w_mem_kv', 'norm_pre', 'norm_post', 'w_out', 'a_w_in', 'a_ln_w', 'a_ln_b', 'a_w_s', 'a_b_s', 'b_w_in', 'b_conv_w', 'c_w_in', 'c_conv_w', 'c_a_log', 'c_dt_bias', 'c_o_norm_w']
TWIN_DIFF_INPUT = 'x'
TWIN_INPUTS = ['x', 'mem', 'mem_norm_w', 'w_mem_kv', 'norm_pre', 'norm_post', 'w_out', 'a_w_in', 'a_ln_w', 'a_ln_b', 'a_w_s', 'a_b_s', 'b_w_in', 'b_conv_w', 'c_w_in', 'c_conv_w', 'c_a_log', 'c_dt_bias', 'c_o_norm_w', 'loss_target', 'm_mem_norm_w', 'm_w_mem_kv', 'm_norm_pre', 'm_norm_post', 'm_w_out', 'm_a_w_in', 'm_a_ln_w', 'm_a_ln_b', 'm_a_w_s', 'm_a_b_s', 'm_b_w_in', 'm_b_conv_w', 'm_c_w_in', 'm_c_conv_w', 'm_c_a_log', 'm_c_dt_bias', 'm_c_o_norm_w', 'v_mem_norm_w', 'v_w_mem_kv', 'v_norm_pre', 'v_norm_post', 'v_w_out', 'v_a_w_in', 'v_a_ln_w', 'v_a_ln_b', 'v_a_w_s', 'v_a_b_s', 'v_b_w_in', 'v_b_conv_w', 'v_c_w_in', 'v_c_conv_w', 'v_c_a_log', 'v_c_dt_bias', 'v_c_o_norm_w']
TWIN_OUTPUTS = ['loss', 'grad_x', 'grad_mem_norm_w', 'grad_w_mem_kv', 'grad_norm_pre', 'grad_norm_post', 'grad_w_out', 'grad_a_w_in', 'grad_a_ln_w', 'grad_a_ln_b', 'grad_a_w_s', 'grad_a_b_s', 'grad_b_w_in', 'grad_b_conv_w', 'grad_c_w_in', 'grad_c_conv_w', 'grad_c_a_log', 'grad_c_dt_bias', 'grad_c_o_norm_w', 'delta_mem_norm_w', 'delta_w_mem_kv', 'delta_norm_pre', 'delta_norm_post', 'delta_w_out', 'delta_a_w_in', 'delta_a_ln_w', 'delta_a_ln_b', 'delta_a_w_s', 'delta_a_b_s', 'delta_b_w_in', 'delta_b_conv_w', 'delta_c_w_in', 'delta_c_conv_w', 'delta_c_a_log', 'delta_c_dt_bias', 'delta_c_o_norm_w', 'new_m_mem_norm_w', 'new_m_w_mem_kv', 'new_m_norm_pre', 'new_m_norm_post', 'new_m_w_out', 'new_m_a_w_in', 'new_m_a_ln_w', 'new_m_a_ln_b', 'new_m_a_w_s', 'new_m_a_b_s', 'new_m_b_w_in', 'new_m_b_conv_w', 'new_m_c_w_in', 'new_m_c_conv_w', 'new_m_c_a_log', 'new_m_c_dt_bias', 'new_m_c_o_norm_w', 'new_v_mem_norm_w', 'new_v_w_mem_kv', 'new_v_norm_pre', 'new_v_norm_post', 'new_v_w_out', 'new_v_a_w_in', 'new_v_a_ln_w', 'new_v_a_ln_b', 'new_v_a_w_s', 'new_v_a_b_s', 'new_v_b_w_in', 'new_v_b_conv_w', 'new_v_c_w_in', 'new_v_c_conv_w', 'new_v_c_a_log', 'new_v_c_dt_bias', 'new_v_c_o_norm_w']
TWIN_LEAF_KINDS = {'loss': 'loss', 'grad_x': 'grad_x', 'grad_mem_norm_w': 'grad_w', 'grad_w_mem_kv': 'grad_w', 'grad_norm_pre': 'grad_w', 'grad_norm_post': 'grad_w', 'grad_w_out': 'grad_w', 'grad_a_w_in': 'grad_w', 'grad_a_ln_w': 'grad_w', 'grad_a_ln_b': 'grad_w', 'grad_a_w_s': 'grad_w', 'grad_a_b_s': 'grad_w', 'grad_b_w_in': 'grad_w', 'grad_b_conv_w': 'grad_w', 'grad_c_w_in': 'grad_w', 'grad_c_conv_w': 'grad_w', 'grad_c_a_log': 'grad_w', 'grad_c_dt_bias': 'grad_w', 'grad_c_o_norm_w': 'grad_w', 'delta_mem_norm_w': 'delta_w', 'delta_w_mem_kv': 'delta_w', 'delta_norm_pre': 'delta_w', 'delta_norm_post': 'delta_w', 'delta_w_out': 'delta_w', 'delta_a_w_in': 'delta_w', 'delta_a_ln_w': 'delta_w', 'delta_a_ln_b': 'delta_w', 'delta_a_w_s': 'delta_w', 'delta_a_b_s': 'delta_w', 'delta_b_w_in': 'delta_w', 'delta_b_conv_w': 'delta_w', 'delta_c_w_in': 'delta_w', 'delta_c_conv_w': 'delta_w', 'delta_c_a_log': 'delta_w', 'delta_c_dt_bias': 'delta_w', 'delta_c_o_norm_w': 'delta_w', 'new_m_mem_norm_w': 'new_m', 'new_m_w_mem_kv': 'new_m', 'new_m_norm_pre': 'new_m', 'new_m_norm_post': 'new_m', 'new_m_w_out': 'new_m', 'new_m_a_w_in': 'new_m', 'new_m_a_ln_w': 'new_m', 'new_m_a_ln_b': 'new_m', 'new_m_a_w_s': 'new_m', 'new_m_a_b_s': 'new_m', 'new_m_b_w_in': 'new_m', 'new_m_b_conv_w': 'new_m', 'new_m_c_w_in': 'new_m', 'new_m_c_conv_w': 'new_m', 'new_m_c_a_log': 'new_m', 'new_m_c_dt_bias': 'new_m', 'new_m_c_o_norm_w': 'new_m', 'new_v_mem_norm_w': 'new_v', 'new_v_w_mem_kv': 'new_v', 'new_v_norm_pre': 'new_v', 'new_v_norm_post': 'new_v', 'new_v_w_out': 'new_v', 'new_v_a_w_in': 'new_v', 'new_v_a_ln_w': 'new_v', 'new_v_a_ln_b': 'new_v', 'new_v_a_w_s': 'new_v', 'new_v_a_b_s': 'new_v', 'new_v_b_w_in': 'new_v', 'new_v_b_conv_w': 'new_v', 'new_v_c_w_in': 'new_v', 'new_v_c_conv_w': 'new_v', 'new_v_c_a_log': 'new_v', 'new_v_c_dt_bias': 'new_v', 'new_v_c_o_norm_w': 'new_v'}


def _forward(args):
    return _fwd_reference(*[args[k] for k in FWD_PARAMS])


def _output_shape():
    def fwd():
        inp = _fwd_setup_inputs(0)
        return _fwd_reference(*[inp[k] for k in FWD_PARAMS])
    out = _jax.eval_shape(fwd)
    return out.shape, out.dtype

N_MICROBATCH = 1
ADAM_LR = 0.001
ADAM_B1 = 0.9
ADAM_B2 = 0.999
ADAM_EPS = 1e-08
ADAM_WD = 0.01
ADAM_STEP = 10
PER_EXAMPLE_BATCH_AXIS = {'x': 0, 'mem': 0, 'loss_target': 0}
SHARED_INPUTS = []
_WEIGHT_DTYPES = {'mem_norm_w': _jnp.float32, 'w_mem_kv': _jnp.float32, 'norm_pre': _jnp.float32, 'norm_post': _jnp.float32, 'w_out': _jnp.float32, 'a_w_in': _jnp.float32, 'a_ln_w': _jnp.float32, 'a_ln_b': _jnp.float32, 'a_w_s': _jnp.float32, 'a_b_s': _jnp.float32, 'b_w_in': _jnp.float32, 'b_conv_w': _jnp.float32, 'c_w_in': _jnp.float32, 'c_conv_w': _jnp.float32, 'c_a_log': _jnp.float32, 'c_dt_bias': _jnp.float32, 'c_o_norm_w': _jnp.float32}
MOMENT_SCALE = {'mem_norm_w': 3.948890e-01, 'w_mem_kv': 3.652664e-01, 'norm_pre': 2.035146e+00, 'norm_post': 6.394249e+01, 'w_out': 2.192955e+00, 'a_w_in': 1.119660e+00, 'a_ln_w': 4.308472e-01, 'a_ln_b': 4.359211e-01, 'a_w_s': 8.503570e-01, 'a_b_s': 1.131264e+00, 'b_w_in': 9.142600e-01, 'b_conv_w': 1.150951e+00, 'c_w_in': 8.127658e-01, 'c_conv_w': 1.147613e+00, 'c_a_log': 1.567180e+01, 'c_dt_bias': 1.537665e+01, 'c_o_norm_w': 7.324296e+00}


def _to_microbatches(a, axis):
    t = _jnp.moveaxis(a, axis, 0)
    t = t.reshape((N_MICROBATCH, t.shape[0] // N_MICROBATCH) + t.shape[1:])
    return _jnp.moveaxis(t, 1, axis + 1)


def setup_inputs(seed: int = 0) -> dict:
    inp = _fwd_setup_inputs(seed)
    key = _jax.random.fold_in(_jax.random.key(seed), 7919)
    shape, _ = _output_shape()
    out = dict(inp)
    out["loss_target"] = _jax.random.normal(_jax.random.fold_in(key, 0), shape, _jnp.float32)
    for i, name in enumerate(TWIN_WEIGHTS):
        w = inp[name].astype(_jnp.float32)
        if MOMENT_SCALE is None:
            s = _jnp.sqrt(_jnp.mean(_jnp.square(w)) + 1e-30)
        else:
            s = MOMENT_SCALE[name]
        km, kv = _jax.random.split(_jax.random.fold_in(key, i + 1))
        out[name] = w
        out["m_" + name] = s * _jax.random.normal(km, w.shape, _jnp.float32)
        out["v_" + name] = (s * s) * _jax.random.uniform(kv, w.shape, _jnp.float32, 0.5, 1.5)
    if N_MICROBATCH > 1:
        for name, axis in PER_EXAMPLE_BATCH_AXIS.items():
            out[name] = _to_microbatches(out[name], axis)
    return {'x': out['x'], 'mem': out['mem'], 'mem_norm_w': out['mem_norm_w'], 'w_mem_kv': out['w_mem_kv'], 'norm_pre': out['norm_pre'], 'norm_post': out['norm_post'], 'w_out': out['w_out'], 'a_w_in': out['a_w_in'], 'a_ln_w': out['a_ln_w'], 'a_ln_b': out['a_ln_b'], 'a_w_s': out['a_w_s'], 'a_b_s': out['a_b_s'], 'b_w_in': out['b_w_in'], 'b_conv_w': out['b_conv_w'], 'c_w_in': out['c_w_in'], 'c_conv_w': out['c_conv_w'], 'c_a_log': out['c_a_log'], 'c_dt_bias': out['c_dt_bias'], 'c_o_norm_w': out['c_o_norm_w'], 'loss_target': out['loss_target'], 'm_mem_norm_w': out['m_mem_norm_w'], 'm_w_mem_kv': out['m_w_mem_kv'], 'm_norm_pre': out['m_norm_pre'], 'm_norm_post': out['m_norm_post'], 'm_w_out': out['m_w_out'], 'm_a_w_in': out['m_a_w_in'], 'm_a_ln_w': out['m_a_ln_w'], 'm_a_ln_b': out['m_a_ln_b'], 'm_a_w_s': out['m_a_w_s'], 'm_a_b_s': out['m_a_b_s'], 'm_b_w_in': out['m_b_w_in'], 'm_b_conv_w': out['m_b_conv_w'], 'm_c_w_in': out['m_c_w_in'], 'm_c_conv_w': out['m_c_conv_w'], 'm_c_a_log': out['m_c_a_log'], 'm_c_dt_bias': out['m_c_dt_bias'], 'm_c_o_norm_w': out['m_c_o_norm_w'], 'v_mem_norm_w': out['v_mem_norm_w'], 'v_w_mem_kv': out['v_w_mem_kv'], 'v_norm_pre': out['v_norm_pre'], 'v_norm_post': out['v_norm_post'], 'v_w_out': out['v_w_out'], 'v_a_w_in': out['v_a_w_in'], 'v_a_ln_w': out['v_a_ln_w'], 'v_a_ln_b': out['v_a_ln_b'], 'v_a_w_s': out['v_a_w_s'], 'v_a_b_s': out['v_a_b_s'], 'v_b_w_in': out['v_b_w_in'], 'v_b_conv_w': out['v_b_conv_w'], 'v_c_w_in': out['v_c_w_in'], 'v_c_conv_w': out['v_c_conv_w'], 'v_c_a_log': out['v_c_a_log'], 'v_c_dt_bias': out['v_c_dt_bias'], 'v_c_o_norm_w': out['v_c_o_norm_w']}


def _loss(weights, diff, rest, loss_target):
    with _jax.named_scope("forward"):
        args = {**rest, TWIN_DIFF_INPUT: diff, **{k: w.astype(_WEIGHT_DTYPES[k]) for k, w in weights.items()}}
        y = _forward(args)
    with _jax.named_scope("loss_head"):
        err = _jnp.square(y.astype(_jnp.float32) - loss_target)
        return 0.5 * _jnp.sum(_jnp.mean(err, axis=-1)) if err.ndim else 0.5 * err


def _adamw(w, g, m, v):
    m = ADAM_B1 * m + (1.0 - ADAM_B1) * g
    v = ADAM_B2 * v + (1.0 - ADAM_B2) * _jnp.square(g)
    m_hat = m / (1.0 - ADAM_B1 ** ADAM_STEP)
    v_hat = v / (1.0 - ADAM_B2 ** ADAM_STEP)
    delta = -ADAM_LR * (m_hat / (_jnp.sqrt(v_hat) + ADAM_EPS) + ADAM_WD * w)
    return delta, m, v


def reference(x, mem, mem_norm_w, w_mem_kv, norm_pre, norm_post, w_out, a_w_in, a_ln_w, a_ln_b, a_w_s, a_b_s, b_w_in, b_conv_w, c_w_in, c_conv_w, c_a_log, c_dt_bias, c_o_norm_w, loss_target, m_mem_norm_w, m_w_mem_kv, m_norm_pre, m_norm_post, m_w_out, m_a_w_in, m_a_ln_w, m_a_ln_b, m_a_w_s, m_a_b_s, m_b_w_in, m_b_conv_w, m_c_w_in, m_c_conv_w, m_c_a_log, m_c_dt_bias, m_c_o_norm_w, v_mem_norm_w, v_w_mem_kv, v_norm_pre, v_norm_post, v_w_out, v_a_w_in, v_a_ln_w, v_a_ln_b, v_a_w_s, v_a_b_s, v_b_w_in, v_b_conv_w, v_c_w_in, v_c_conv_w, v_c_a_log, v_c_dt_bias, v_c_o_norm_w):
    given = dict(x=x, mem=mem, mem_norm_w=mem_norm_w, w_mem_kv=w_mem_kv, norm_pre=norm_pre, norm_post=norm_post, w_out=w_out, a_w_in=a_w_in, a_ln_w=a_ln_w, a_ln_b=a_ln_b, a_w_s=a_w_s, a_b_s=a_b_s, b_w_in=b_w_in, b_conv_w=b_conv_w, c_w_in=c_w_in, c_conv_w=c_conv_w, c_a_log=c_a_log, c_dt_bias=c_dt_bias, c_o_norm_w=c_o_norm_w, loss_target=loss_target, m_mem_norm_w=m_mem_norm_w, m_w_mem_kv=m_w_mem_kv, m_norm_pre=m_norm_pre, m_norm_post=m_norm_post, m_w_out=m_w_out, m_a_w_in=m_a_w_in, m_a_ln_w=m_a_ln_w, m_a_ln_b=m_a_ln_b, m_a_w_s=m_a_w_s, m_a_b_s=m_a_b_s, m_b_w_in=m_b_w_in, m_b_conv_w=m_b_conv_w, m_c_w_in=m_c_w_in, m_c_conv_w=m_c_conv_w, m_c_a_log=m_c_a_log, m_c_dt_bias=m_c_dt_bias, m_c_o_norm_w=m_c_o_norm_w, v_mem_norm_w=v_mem_norm_w, v_w_mem_kv=v_w_mem_kv, v_norm_pre=v_norm_pre, v_norm_post=v_norm_post, v_w_out=v_w_out, v_a_w_in=v_a_w_in, v_a_ln_w=v_a_ln_w, v_a_ln_b=v_a_ln_b, v_a_w_s=v_a_w_s, v_a_b_s=v_a_b_s, v_b_w_in=v_b_w_in, v_b_conv_w=v_b_conv_w, v_c_w_in=v_c_w_in, v_c_conv_w=v_c_conv_w, v_c_a_log=v_c_a_log, v_c_dt_bias=v_c_dt_bias, v_c_o_norm_w=v_c_o_norm_w)
    weights = {n: given[n] for n in TWIN_WEIGHTS}
    shared = {n: given[n] for n in SHARED_INPUTS}
    per_example = {n: given[n] for n in ['x', 'mem']}
    grad_fn = _jax.value_and_grad(_loss, argnums=(0, 1))

    def one_microbatch(ex, loss_target):
        ex = dict(ex)
        diff = ex.pop(TWIN_DIFF_INPUT)
        return grad_fn(weights, diff, {**shared, **ex}, loss_target)

    if N_MICROBATCH == 1:
        loss, (grad_w, grad_x) = one_microbatch(per_example, given["loss_target"])
    else:
        def body(carry, xs):
            loss_sum, grad_sum = carry
            l_k, (gw_k, gx_k) = one_microbatch(xs[0], xs[1])
            with _jax.named_scope("update"):
                return (loss_sum + l_k, _jax.tree.map(_jnp.add, grad_sum, gw_k)), gx_k

        init = (_jnp.zeros((), _jnp.float32), _jax.tree.map(_jnp.zeros_like, weights))
        (loss, grad_w), grad_x = _jax.lax.scan(body, init, (per_example, given["loss_target"]))
    with _jax.named_scope("update"):
        delta_w, new_m, new_v = {}, {}, {}
        for n in TWIN_WEIGHTS:
            delta_w[n], new_m[n], new_v[n] = _adamw(weights[n], grad_w[n], given["m_" + n], given["v_" + n])
    return (loss, grad_x, *[grad_w[n] for n in TWIN_WEIGHTS], *[delta_w[n] for n in TWIN_WEIGHTS],
            *[new_m[n] for n in TWIN_WEIGHTS], *[new_v[n] for n in TWIN_WEIGHTS])
```

```python
import functools
import math

import jax
import jax.numpy as jnp
from jax import lax
from jax.experimental import pallas as pl
from jax.experimental.pallas import tpu as pltpu

F32 = jnp.float32
MXU = jnp.bfloat16
ACT = jnp.bfloat16

D = 1024
D_XA = 512
D_CAT = 1536
N_MEM = 256
XA_HEADS = 4
HEAD = 128
ZQ = D_CAT + D_XA
EPS = 1e-6
GM_CHUNK = 128
GM_GROUPS = 8
DN_HEADS = 8
DN_CHUNK = 64
N_DEV = 8
HALO = 8
VMEM_LIMIT = 56 * 1024 * 1024
R_SHARD = 3328
R_REPL = 272

ADAM_LR = 0.001
ADAM_B1 = 0.9
ADAM_B2 = 0.999
ADAM_EPS = 1e-08
ADAM_WD = 0.01
ADAM_STEP = 10

NN = ((1,), (0,))
NT = ((1,), (1,))
TN = ((0,), (0,))
MESH = pl.DeviceIdType.MESH


def _pcall(body, **kw):
    return pl.pallas_call(body, **kw)


def _cp(n_axes):
    return pltpu.CompilerParams(dimension_semantics=("arbitrary",) * n_axes, vmem_limit_bytes=VMEM_LIMIT)


def _dot(a, b, dims, prec=None):
    return lax.dot_general(a, b, (dims, ((), ())), preferred_element_type=F32, precision=prec)


def _mdot(a, b, dims):
    return _dot(a.astype(MXU), b.astype(MXU), dims)


def _make_mms(raw):
    @jax.custom_vjp
    def nn(a, b):
        return raw(a, b, NN)

    @jax.custom_vjp
    def nt(a, b):
        return raw(a, b, NT)

    @jax.custom_vjp
    def tn(a, b):
        return raw(a, b, TN)

    nn.defvjp(lambda a, b: (nn(a, b), (a, b)), lambda r, g: (nt(g, r[1]), tn(r[0], g)))
    nt.defvjp(lambda a, b: (nt(a, b), (a, b)), lambda r, g: (nn(g, r[1]), tn(g, r[0])))
    tn.defvjp(lambda a, b: (tn(a, b), (a, b)), lambda r, g: (nt(r[1], g), nn(r[0], g)))
    return nn, nt, tn


_nn, _nt, _tn = _make_mms(_mdot)
_hnn, _hnt, _htn = _make_mms(lambda a, b, dims: _dot(a, b, dims, lax.Precision.HIGHEST))


def _full(shape):
    return pl.BlockSpec(shape, lambda *_: (0,) * len(shape))


def _silu(z):
    return z * jax.nn.sigmoid(z)


def _dsilu(z):
    s = jax.nn.sigmoid(z)
    return s * (1.0 + z * (1.0 - s))


def _onehot_row(n, k):
    return (lax.broadcasted_iota(jnp.int32, (1, n), 1) == k).astype(F32)


def _all_gather(blk, name):
    r, c = blk.shape

    def body(x_ref, out_ref, send_sems, recv_sems, local_sem):
        x, y, cc = lax.axis_index("x"), lax.axis_index("y"), lax.axis_index("c")
        me, sibling = (x, y, cc), (x, y, 1 - cc)
        chips = [(1 - x, y), (x, 1 - y), (1 - x, 1 - y)]

        def rows(px, py, pc):
            return out_ref.at[4 * px + 2 * py + pc]

        def copy(k, block, to, src=None):
            return pltpu.make_async_remote_copy(
                src_ref=rows(*block) if src is None else src, dst_ref=rows(*block),
                send_sem=send_sems.at[k], recv_sem=recv_sems.at[k], device_id=to, device_id_type=MESH)

        mine = pltpu.make_async_copy(x_ref, rows(*me), local_sem)
        mine.start()
        first = [copy(0, me, sibling, src=x_ref)]
        first += [copy(1 + j, me, (*chip, cc), src=x_ref) for j, chip in enumerate(chips)]
        for cp in first:
            cp.start()
        passed = [copy(4 + j, (*chip, cc), sibling) for j, chip in enumerate(chips)]
        for j, chip in enumerate(chips):
            copy(1 + j, (*chip, cc), me).wait_recv()
            passed[j].start()
        copy(0, sibling, me).wait_recv()
        for j, chip in enumerate(chips):
            copy(4 + j, (*chip, 1 - cc), me).wait_recv()
        for cp in first + passed:
            cp.wait_send()
        mine.wait()

    return _pcall(
        body, name=name,
        out_shape=jax.ShapeDtypeStruct((N_DEV, r, c), blk.dtype),
        in_specs=[pl.BlockSpec(memory_space=pl.ANY)],
        out_specs=pl.BlockSpec(memory_space=pl.ANY),
        scratch_shapes=[pltpu.SemaphoreType.DMA((7,)), pltpu.SemaphoreType.DMA((7,)), pltpu.SemaphoreType.DMA],
    )(blk)


def _all_to_all(g, name):
    _, r, c = g.shape

    def body(g_ref, out_ref, send_sems, recv_sems, local_sem):
        x, y, cc = lax.axis_index("x"), lax.axis_index("y"), lax.axis_index("c")
        my_idx = 4 * x + 2 * y + cc
        mine = pltpu.make_async_copy(g_ref.at[my_idx], out_ref.at[my_idx], local_sem)
        mine.start()
        copies = []
        for k in range(1, N_DEV):
            px = 1 - x if k & 4 else x
            py = 1 - y if k & 2 else y
            pc = 1 - cc if k & 1 else cc
            copies.append(pltpu.make_async_remote_copy(
                src_ref=g_ref.at[4 * px + 2 * py + pc], dst_ref=out_ref.at[my_idx],
                send_sem=send_sems.at[k - 1], recv_sem=recv_sems.at[k - 1], device_id=(px, py, pc), device_id_type=MESH))
        for cp in copies:
            cp.start()
        for cp in copies:
            cp.wait_recv()
        for cp in copies:
            cp.wait_send()
        mine.wait()

    return _pcall(
        body, name=name,
        out_shape=jax.ShapeDtypeStruct(g.shape, g.dtype),
        in_specs=[pl.BlockSpec(memory_space=pl.ANY)],
        out_specs=pl.BlockSpec(memory_space=pl.ANY),
        scratch_shapes=[pltpu.SemaphoreType.DMA((7,)), pltpu.SemaphoreType.DMA((7,)), pltpu.SemaphoreType.DMA],
    )(g)


def _reduce_adamw(parts, w, m, v, name):
    r = w.shape[0]
    tr = 128 if r % 128 == 0 else r

    def body(p_ref, w_ref, m_ref, v_ref, g_ref, d_ref, nm_ref, nv_ref):
        g = p_ref[0]
        for s in range(1, N_DEV):
            g = g + p_ref[s]
        nm = ADAM_B1 * m_ref[...] + (1.0 - ADAM_B1) * g
        nv = ADAM_B2 * v_ref[...] + (1.0 - ADAM_B2) * (g * g)
        m_hat = nm / (1.0 - ADAM_B1 ** ADAM_STEP)
        v_hat = nv / (1.0 - ADAM_B2 ** ADAM_STEP)
        g_ref[...] = g
        d_ref[...] = -ADAM_LR * (m_hat / (jnp.sqrt(v_hat) + ADAM_EPS) + ADAM_WD * w_ref[...])
        nm_ref[...] = nm
        nv_ref[...] = nv

    row = pl.BlockSpec((tr, D), lambda i: (i, 0))
    out = jax.ShapeDtypeStruct((r, D), F32)
    return _pcall(
        body, name=name, grid=(r // tr,),
        in_specs=[pl.BlockSpec((N_DEV, tr, D), lambda i: (0, i, 0)), row, row, row],
        out_specs=[row, row, row, row], out_shape=[out, out, out, out], compiler_params=_cp(1),
    )(parts, w, m, v)


def _proj_fwd(x, nw, w, wab, name):
    t, npj = x.shape[0], w.shape[1]
    tm, tn = min(512, t), 1024
    has_ab = wab is not None

    def body(*refs):
        if has_ab:
            x_ref, nw_ref, w_ref, wab_ref, p_ref, h_ref, ab_ref, hs = refs
        else:
            x_ref, nw_ref, w_ref, p_ref, h_ref, hs = refs

        @pl.when(pl.program_id(1) == 0)
        def _():
            xv = x_ref[...]
            hv = xv * lax.rsqrt(jnp.mean(xv * xv, axis=-1, keepdims=True) + EPS) * nw_ref[...]
            hs[...] = hv.astype(MXU)
            h_ref[...] = hv.astype(ACT)
            if has_ab:
                ab_ref[...] = _mdot(hv, wab_ref[...], NN)

        p_ref[...] = _dot(hs[...], w_ref[...], NN).astype(ACT)

    in_specs = [pl.BlockSpec((tm, D), lambda i, j: (i, 0)), _full((1, D)), pl.BlockSpec((D, tn), lambda i, j: (0, j))]
    out_specs = [pl.BlockSpec((tm, tn), lambda i, j: (i, j)), pl.BlockSpec((tm, D), lambda i, j: (i, 0))]
    out_shape = [jax.ShapeDtypeStruct((t, npj), ACT), jax.ShapeDtypeStruct((t, D), ACT)]
    args = [x, nw, w]
    if has_ab:
        in_specs.append(_full((D, HEAD)))
        out_specs.append(pl.BlockSpec((tm, HEAD), lambda i, j: (i, 0)))
        out_shape.append(jax.ShapeDtypeStruct((t, HEAD), F32))
        args.append(wab)
    return _pcall(body, name=name, grid=(t // tm, npj // tn), in_specs=in_specs, out_specs=out_specs,
                  out_shape=out_shape, scratch_shapes=[pltpu.VMEM((tm, D), MXU)], compiler_params=_cp(2))(*args)


def _proj_bwd_x(dps, ws, x, dxn, nw, name):
    t = x.shape[0]
    tm = min(256, t)
    n = len(dps)

    def body(*refs):
        dp_refs, w_refs = refs[:n], refs[n:2 * n]
        x_ref, dxn_ref, nw_ref, dx_ref, dnw_ref = refs[2 * n:]
        dh = _mdot(dp_refs[0][...], w_refs[0][...], NT)
        for k in range(1, n):
            dh = dh + _mdot(dp_refs[k][...], w_refs[k][...], NT)
        xv = x_ref[...]
        r = lax.rsqrt(jnp.mean(xv * xv, axis=-1, keepdims=True) + EPS)

        @pl.when(pl.program_id(0) == 0)
        def _():
            dnw_ref[...] = jnp.zeros_like(dnw_ref)

        dnw_ref[...] += jnp.sum(dh * xv * r, axis=0, keepdims=True)
        dhw = dh * nw_ref[...]
        dx_ref[...] = dxn_ref[...] + r * (dhw - xv * (r * r) * jnp.mean(dhw * xv, axis=-1, keepdims=True))

    row = pl.BlockSpec((tm, D), lambda i: (i, 0))
    in_specs = [pl.BlockSpec((tm, dp.shape[1]), lambda i: (i, 0)) for dp in dps]
    in_specs += [_full(w.shape) for w in ws]
    in_specs += [row, row, _full((1, D))]
    return _pcall(body, name=name, grid=(t // tm,), in_specs=in_specs, out_specs=[row, _full((1, D))],
                  out_shape=[jax.ShapeDtypeStruct((t, D), F32), jax.ShapeDtypeStruct((1, D), F32)],
                  compiler_params=_cp(1))(*dps, *ws, x, dxn, nw)


def _matmul_tn(a, b, name):
    t, m = a.shape
    n = b.shape[1]
    tm, tn = min(512, t), min(1024, n)

    def body(a_ref, b_ref, o_ref):
        @pl.when(pl.program_id(1) == 0)
        def _():
            o_ref[...] = jnp.zeros_like(o_ref)

        o_ref[...] += _mdot(a_ref[...], b_ref[...], TN)

    return _pcall(body, name=name, grid=(n // tn, t // tm),
                  in_specs=[pl.BlockSpec((tm, m), lambda j, i: (i, 0)), pl.BlockSpec((tm, tn), lambda j, i: (i, j))],
                  out_specs=pl.BlockSpec((m, tn), lambda j, i: (0, j)),
                  out_shape=jax.ShapeDtypeStruct((m, n), F32), compiler_params=_cp(2))(a, b)


def _memkv_fwd(mem, mw, wkv):
    def body(mem_ref, mw_ref, w_ref, kv_ref):
        mv = mem_ref[...]
        mn = mv * lax.rsqrt(jnp.mean(mv * mv, axis=-1, keepdims=True) + EPS) * mw_ref[...]
        kv_ref[...] = _mdot(mn, w_ref[...], NN)

    return _pcall(body, name="memkv_fwd", out_shape=jax.ShapeDtypeStruct((N_MEM, 2 * D_XA), F32),
                  compiler_params=pltpu.CompilerParams(vmem_limit_bytes=VMEM_LIMIT))(mem, mw, wkv)


def _memkv_bwd(mem, mw, wkv, dkv):
    def body(mem_ref, mw_ref, w_ref, dkv_ref, dw_ref, dmw_ref):
        mv = mem_ref[...]
        r = lax.rsqrt(jnp.mean(mv * mv, axis=-1, keepdims=True) + EPS)
        mn = mv * r * mw_ref[...]
        dkvv = dkv_ref[...]
        dw_ref[...] = _mdot(mn, dkvv, TN)
        dmn = _mdot(dkvv, w_ref[...], NT)
        dmw_ref[...] = jnp.sum(dmn * mv * r, axis=0, keepdims=True)

    return _pcall(body, name="memkv_bwd",
                  out_shape=[jax.ShapeDtypeStruct((D, 2 * D_XA), F32), jax.ShapeDtypeStruct((1, D), F32)],
                  compiler_params=pltpu.CompilerParams(vmem_limit_bytes=VMEM_LIMIT))(mem, mw, wkv, dkv)


def _attend(q, kv):
    ps, outs = [], []
    for hh in range(XA_HEADS):
        qh = q[:, hh * HEAD:(hh + 1) * HEAD]
        kh = kv[:, hh * HEAD:(hh + 1) * HEAD]
        vh = kv[:, D_XA + hh * HEAD:D_XA + (hh + 1) * HEAD]
        s = _mdot(qh, kh, NT) * (HEAD ** -0.5)
        s = s - jnp.max(s, axis=-1, keepdims=True)
        e = jnp.exp(s)
        p = e / jnp.sum(e, axis=-1, keepdims=True)
        ps.append(p)
        outs.append(_mdot(p, vh, NN))
    return ps, outs


def _tail_fwd(p, ymix, x, kv, wout, npost, name):
    t = x.shape[0]
    tm = min(256, t)

    def body(z_ref, q_ref, y_ref, x_ref, kv_ref, w_ref, np_ref, xn_ref, o_ref):
        _, outs = _attend(q_ref[...].astype(F32), kv_ref[...])
        cat = jnp.concatenate([y_ref[...].astype(F32)] + outs, axis=1)
        g = cat * _silu(z_ref[...].astype(F32))
        o = _mdot(g, w_ref[...], NN)
        o_ref[...] = o
        xn_ref[...] = x_ref[...] + o * lax.rsqrt(jnp.mean(o * o, axis=-1, keepdims=True) + EPS) * np_ref[...]

    row = pl.BlockSpec((tm, D), lambda i: (i, 0))
    return _pcall(
        body, name=name, grid=(t // tm,),
        in_specs=[pl.BlockSpec((tm, D_CAT), lambda i: (i, 0)), pl.BlockSpec((tm, D_XA), lambda i: (i, D_CAT // D_XA)),
                  row, row, _full((N_MEM, 2 * D_XA)), _full((D_CAT, D)), _full((1, D))],
        out_specs=[row, row],
        out_shape=[jax.ShapeDtypeStruct((t, D), F32), jax.ShapeDtypeStruct((t, D), F32)], compiler_params=_cp(1),
    )(p, p, ymix, x, kv, wout, npost)


def _tail_bwd(dxn, o, p, ymix, kv, wout, npost, name):
    t = dxn.shape[0]
    tm = min(256, t)

    def body(dxn_ref, o_ref, z_ref, q_ref, y_ref, kv_ref, w_ref, np_ref, dzq_ref, dy_ref, dw_ref, dnp_ref, dkv_ref):
        @pl.when(pl.program_id(0) == 0)
        def _():
            dw_ref[...] = jnp.zeros_like(dw_ref)
            dnp_ref[...] = jnp.zeros_like(dnp_ref)
            dkv_ref[...] = jnp.zeros_like(dkv_ref)

        q = q_ref[...].astype(F32)
        kvv = kv_ref[...]
        z = z_ref[...].astype(F32)
        ps, outs = _attend(q, kvv)
        cat = jnp.concatenate([y_ref[...].astype(F32)] + outs, axis=1)
        sz = _silu(z)
        g = cat * sz
        ov = o_ref[...]
        dr = dxn_ref[...]
        rr = lax.rsqrt(jnp.mean(ov * ov, axis=-1, keepdims=True) + EPS)
        dnp_ref[...] += jnp.sum(dr * ov * rr, axis=0, keepdims=True)
        dow = dr * np_ref[...]
        do = rr * (dow - ov * (rr * rr) * jnp.mean(dow * ov, axis=-1, keepdims=True))
        dg = _mdot(do, w_ref[...], NT)
        dw_ref[...] += _mdot(g, do, TN)
        dcat = dg * sz
        dzq_ref[:, 0:D_CAT] = (dg * cat * _dsilu(z)).astype(ACT)
        dy_ref[...] = dcat[:, 0:D].astype(ACT)
        for hh in range(XA_HEADS):
            lo = hh * HEAD
            doh = dcat[:, D + lo:D + lo + HEAD]
            qh = q[:, lo:lo + HEAD]
            kh = kvv[:, lo:lo + HEAD]
            vh = kvv[:, D_XA + lo:D_XA + lo + HEAD]
            ph = ps[hh]
            dp = _mdot(doh, vh, NT)
            ds = ph * (dp - jnp.sum(dp * ph, axis=-1, keepdims=True))
            dzq_ref[:, D_CAT + lo:D_CAT + lo + HEAD] = (_mdot(ds, kh, NN) * (HEAD ** -0.5)).astype(ACT)
            dkv_ref[:, lo:lo + HEAD] += _mdot(ds, qh, TN) * (HEAD ** -0.5)
            dkv_ref[:, D_XA + lo:D_XA + lo + HEAD] += _mdot(ph, doh, TN)

    row = pl.BlockSpec((tm, D), lambda i: (i, 0))
    return _pcall(
        body, name=name, grid=(t // tm,),
        in_specs=[row, row, pl.BlockSpec((tm, D_CAT), lambda i: (i, 0)),
                  pl.BlockSpec((tm, D_XA), lambda i: (i, D_CAT // D_XA)), row,
                  _full((N_MEM, 2 * D_XA)), _full((D_CAT, D)), _full((1, D))],
        out_specs=[pl.BlockSpec((tm, ZQ), lambda i: (i, 0)), row, _full((D_CAT, D)), _full((1, D)),
                   _full((N_MEM, 2 * D_XA))],
        out_shape=[jax.ShapeDtypeStruct((t, ZQ), ACT), jax.ShapeDtypeStruct((t, D), ACT),
                   jax.ShapeDtypeStruct((D_CAT, D), F32), jax.ShapeDtypeStruct((1, D), F32),
                   jax.ShapeDtypeStruct((N_MEM, 2 * D_XA), F32)],
        compiler_params=_cp(1),
    )(dxn, o, p, p, ymix, kv, wout, npost)


def _gmlp_chunk(us, vs, lnws, lnbs, wss, bss):
    gv = [jax.nn.gelu(v) for v in vs]
    mean = sum(jnp.sum(v, axis=-1, keepdims=True) for v in gv) / D
    cen = [v - mean for v in gv]
    var = sum(jnp.sum(c * c, axis=-1, keepdims=True) for c in cen) / D
    rstd = lax.rsqrt(var + EPS)
    row = lax.broadcasted_iota(jnp.int32, (GM_CHUNK, GM_CHUNK), 0)
    col = lax.broadcasted_iota(jnp.int32, (GM_CHUNK, GM_CHUNK), 1)
    ys = []
    for g in range(GM_GROUPS):
        vn = cen[g] * rstd * lnws[g] + lnbs[g]
        sp = _nn(jnp.where(row >= col, wss[g], 0.0), vn) + bss[g]
        ys.append(jax.nn.gelu(us[g]) * sp)
    return ys


def _split_cols(v, n, width=HEAD):
    return [v[:, k * width:(k + 1) * width] for k in range(n)]


def _gmlp_operands(u_ref, v_ref, lnw_ref, lnb_ref, ws_ref, bst_ref):
    us = _split_cols(u_ref[...].astype(F32), GM_GROUPS)
    vs = _split_cols(v_ref[...].astype(F32), GM_GROUPS)
    lnws = _split_cols(lnw_ref[...], GM_GROUPS)
    lnbs = _split_cols(lnb_ref[...], GM_GROUPS)
    wss = [ws_ref[g] for g in range(GM_GROUPS)]
    bst = bst_ref[...]
    bss = [jnp.sum(bst * _onehot_row(HEAD, g), axis=1, keepdims=True) for g in range(GM_GROUPS)]
    return us, vs, lnws, lnbs, wss, bss


def _gmlp_specs():
    return [pl.BlockSpec((GM_CHUNK, D), lambda i: (i, ZQ // D)), pl.BlockSpec((GM_CHUNK, D), lambda i: (i, ZQ // D + 1)),
            _full((1, D)), _full((1, D)), _full((GM_GROUPS, GM_CHUNK, GM_CHUNK)), _full((GM_CHUNK, HEAD))]


def _gmlp_fwd(p, lnw, lnb, ws, bst, name):
    t = p.shape[0]

    def body(u_ref, v_ref, lnw_ref, lnb_ref, ws_ref, bst_ref, y_ref):
        ys = _gmlp_chunk(*_gmlp_operands(u_ref, v_ref, lnw_ref, lnb_ref, ws_ref, bst_ref))
        for g in range(GM_GROUPS):
            y_ref[:, g * HEAD:(g + 1) * HEAD] = ys[g].astype(ACT)

    return _pcall(body, name=name, grid=(t // GM_CHUNK,), in_specs=_gmlp_specs(),
                  out_specs=pl.BlockSpec((GM_CHUNK, D), lambda i: (i, 0)),
                  out_shape=jax.ShapeDtypeStruct((t, D), ACT), compiler_params=_cp(1))(p, p, lnw, lnb, ws, bst)


def _gmlp_bwd(p, dy, lnw, lnb, ws, bst, name):
    t = p.shape[0]

    def body(u_ref, v_ref, lnw_ref, lnb_ref, ws_ref, bst_ref, dy_ref, duv_ref, dlnw_ref, dlnb_ref, dws_ref, dbst_ref):
        @pl.when(pl.program_id(0) == 0)
        def _():
            dlnw_ref[...] = jnp.zeros_like(dlnw_ref)
            dlnb_ref[...] = jnp.zeros_like(dlnb_ref)
            dws_ref[...] = jnp.zeros_like(dws_ref)
            dbst_ref[...] = jnp.zeros_like(dbst_ref)

        ops = _gmlp_operands(u_ref, v_ref, lnw_ref, lnb_ref, ws_ref, bst_ref)
        _, vjp = jax.vjp(_gmlp_chunk, *ops)
        dus, dvs, dlnws, dlnbs, dwss, dbss = vjp(_split_cols(dy_ref[...].astype(F32), GM_GROUPS))
        dbst = jnp.zeros((GM_CHUNK, HEAD), F32)
        for g in range(GM_GROUPS):
            lo = g * HEAD
            duv_ref[:, lo:lo + HEAD] = dus[g].astype(ACT)
            duv_ref[:, D + lo:D + lo + HEAD] = dvs[g].astype(ACT)
            dlnw_ref[:, lo:lo + HEAD] += dlnws[g]
            dlnb_ref[:, lo:lo + HEAD] += dlnbs[g]
            dws_ref[g] += dwss[g]
            dbst = dbst + dbss[g] * _onehot_row(HEAD, g)
        dbst_ref[...] += dbst

    return _pcall(
        body, name=name, grid=(t // GM_CHUNK,),
        in_specs=_gmlp_specs() + [pl.BlockSpec((GM_CHUNK, D), lambda i: (i, 0))],
        out_specs=[pl.BlockSpec((GM_CHUNK, 2 * D), lambda i: (i, 0)), _full((1, D)), _full((1, D)),
                   _full((GM_GROUPS, GM_CHUNK, GM_CHUNK)), _full((GM_CHUNK, HEAD))],
        out_shape=[jax.ShapeDtypeStruct((t, 2 * D), ACT), jax.ShapeDtypeStruct((1, D), F32),
                   jax.ShapeDtypeStruct((1, D), F32), jax.ShapeDtypeStruct((GM_GROUPS, GM_CHUNK, GM_CHUNK), F32),
                   jax.ShapeDtypeStruct((GM_CHUNK, HEAD), F32)],
        compiler_params=_cp(1),
    )(p, p, lnw, lnb, ws, bst, dy)


def _prev_halo(tm, col):
    return pl.BlockSpec((HALO, D), lambda i: (jnp.maximum(i * (tm // HALO) - 1, 0), col))


def _next_halo(tm, col, n_tiles):
    return pl.BlockSpec((HALO, D), lambda i: (jnp.minimum(i + 1, n_tiles - 1) * (tm // HALO), col))


def _taps_back(ext, w, width):
    acc = None
    for k in range(width):
        s = width - 1 - k
        term = w[k:k + 1, :] * (pltpu.roll(ext, s, 0) if s else ext)[HALO:, :]
        acc = term if acc is None else acc + term
    return acc


def _taps_fwd(ext, w, width, n):
    rows = ext.shape[0]
    acc = None
    for k in range(width):
        s = width - 1 - k
        term = w[k:k + 1, :] * (pltpu.roll(ext, rows - s, 0) if s else ext)[0:n, :]
        acc = term if acc is None else acc + term
    return acc


def _sconv_fwd(p, cw, name):
    t = p.shape[0]
    tm = min(256, t)

    def body(b_ref, c_ref, h_ref, cp_ref, hp_ref, w_ref, y_ref):
        first = pl.program_id(0) == 0
        prev = jnp.where(first, 0.0, cp_ref[...].astype(F32) * hp_ref[...].astype(F32))
        ext = jnp.concatenate([prev, c_ref[...].astype(F32) * h_ref[...].astype(F32)], axis=0)
        y_ref[...] = (b_ref[...].astype(F32) * _taps_back(ext, w_ref[...], 3)).astype(ACT)

    c0 = ZQ // D
    tile = [pl.BlockSpec((tm, D), lambda i, c=c: (i, c)) for c in (c0, c0 + 1, c0 + 2)]
    return _pcall(body, name=name, grid=(t // tm,),
                  in_specs=tile + [_prev_halo(tm, c0 + 1), _prev_halo(tm, c0 + 2), _full((HALO, D))],
                  out_specs=pl.BlockSpec((tm, D), lambda i: (i, 0)),
                  out_shape=jax.ShapeDtypeStruct((t, D), ACT), compiler_params=_cp(1))(p, p, p, p, p, cw)


def _sconv_bwd(p, dy, cw, name):
    t = p.shape[0]
    tm = min(256, t)
    n_tiles = t // tm

    def body(b_ref, c_ref, h_ref, cp_ref, hp_ref, bn_ref, dy_ref, dyn_ref, w_ref, d_ref, dw_ref):
        i = pl.program_id(0)

        @pl.when(i == 0)
        def _():
            dw_ref[...] = jnp.zeros_like(dw_ref)

        w = w_ref[...]
        bv, cv, hv = b_ref[...].astype(F32), c_ref[...].astype(F32), h_ref[...].astype(F32)
        dyv = dy_ref[...].astype(F32)
        prev = jnp.where(i == 0, 0.0, cp_ref[...].astype(F32) * hp_ref[...].astype(F32))
        ext = jnp.concatenate([prev, cv * hv], axis=0)
        conv = _taps_back(ext, w, 3)
        dconv = dyv * bv
        nxt = jnp.where(i == n_tiles - 1, 0.0, dyn_ref[...].astype(F32) * bn_ref[...].astype(F32))
        dc = _taps_fwd(jnp.concatenate([dconv, nxt], axis=0), w, 3, tm)
        d_ref[:, 0:D] = (dyv * conv).astype(ACT)
        d_ref[:, D:2 * D] = (dc * hv).astype(ACT)
        d_ref[:, 2 * D:3 * D] = (dc * cv).astype(ACT)
        for k in range(3):
            s = 2 - k
            shifted = (pltpu.roll(ext, s, 0) if s else ext)[HALO:, :]
            dw_ref[k:k + 1, :] += jnp.sum(dconv * shifted, axis=0, keepdims=True)

    c0 = ZQ // D
    tile = [pl.BlockSpec((tm, D), lambda i, c=c: (i, c)) for c in (c0, c0 + 1, c0 + 2)]
    return _pcall(
        body, name=name, grid=(n_tiles,),
        in_specs=tile + [_prev_halo(tm, c0 + 1), _prev_halo(tm, c0 + 2), _next_halo(tm, c0, n_tiles),
                         pl.BlockSpec((tm, D), lambda i: (i, 0)), _next_halo(tm, 0, n_tiles), _full((HALO, D))],
        out_specs=[pl.BlockSpec((tm, 3 * D), lambda i: (i, 0)), _full((HALO, D))],
        out_shape=[jax.ShapeDtypeStruct((t, 3 * D), ACT), jax.ShapeDtypeStruct((HALO, D), F32)],
        compiler_params=_cp(1),
    )(p, p, p, p, p, p, dy, dy, cw)


def _l2_heads(s, scale):
    outs, rs = [], []
    for hh in range(DN_HEADS):
        blk = s[:, hh * HEAD:(hh + 1) * HEAD]
        r = lax.rsqrt(jnp.sum(blk * blk, axis=-1, keepdims=True) + EPS)
        outs.append(blk * (r * scale))
        rs.append(r)
    return outs, rs


_QKV_SCALE = (HEAD ** -0.5, 1.0, None)


def _qkv_fwd(p, cw, name):
    t = p.shape[0]
    tm = min(256, t)

    def body(q_ref, k_ref, v_ref, qp_ref, kp_ref, vp_ref, w_ref, o_ref):
        first = pl.program_id(0) == 0
        for part, (ref, pref) in enumerate(((q_ref, qp_ref), (k_ref, kp_ref), (v_ref, vp_ref))):
            prev = jnp.where(first, 0.0, pref[...].astype(F32))
            ext = jnp.concatenate([prev, ref[...].astype(F32)], axis=0)
            s = _silu(_taps_back(ext, w_ref[:, part * D:(part + 1) * D], 4))
            if _QKV_SCALE[part] is None:
                o_ref[:, part * D:(part + 1) * D] = s.astype(ACT)
            else:
                outs, _ = _l2_heads(s, _QKV_SCALE[part])
                for hh in range(DN_HEADS):
                    o_ref[:, part * D + hh * HEAD:part * D + (hh + 1) * HEAD] = outs[hh].astype(ACT)

    c0 = ZQ // D
    tile = [pl.BlockSpec((tm, D), lambda i, c=c: (i, c)) for c in (c0, c0 + 1, c0 + 2)]
    halo = [_prev_halo(tm, c) for c in (c0, c0 + 1, c0 + 2)]
    return _pcall(body, name=name, grid=(t // tm,), in_specs=tile + halo + [_full((HALO, 3 * D))],
                  out_specs=pl.BlockSpec((tm, 3 * D), lambda i: (i, 0)),
                  out_shape=jax.ShapeDtypeStruct((t, 3 * D), ACT), compiler_params=_cp(1))(p, p, p, p, p, p, cw)


def _qkv_bwd(p, dqkv, cw, name):
    t = p.shape[0]
    tm = min(256, t)
    n_tiles = t // tm

    def body(*refs):
        tiles, prevs, nexts = refs[0:3], refs[3:6], refs[6:9]
        d_tiles, d_nexts = refs[9:12], refs[12:15]
        w_ref, o_ref, dw_ref = refs[15:]
        i = pl.program_id(0)

        @pl.when(i == 0)
        def _():
            dw_ref[...] = jnp.zeros_like(dw_ref)

        for part in range(3):
            w = w_ref[:, part * D:(part + 1) * D]
            prev = jnp.where(i == 0, 0.0, prevs[part][...].astype(F32))
            ext = jnp.concatenate([prev, tiles[part][...].astype(F32), nexts[part][...].astype(F32)], axis=0)
            xc = _taps_back(ext, w, 4)
            dout = jnp.concatenate([d_tiles[part][...].astype(F32), d_nexts[part][...].astype(F32)], axis=0)
            s = _silu(xc)
            if _QKV_SCALE[part] is None:
                ds = dout
            else:
                scale = _QKV_SCALE[part]
                pieces = []
                for hh in range(DN_HEADS):
                    blk = s[:, hh * HEAD:(hh + 1) * HEAD]
                    dblk = dout[:, hh * HEAD:(hh + 1) * HEAD]
                    r = lax.rsqrt(jnp.sum(blk * blk, axis=-1, keepdims=True) + EPS)
                    pieces.append(scale * r * (dblk - blk * (r * r) * jnp.sum(dblk * blk, axis=-1, keepdims=True)))
                ds = jnp.concatenate(pieces, axis=1)
            dxc = ds * _dsilu(xc)
            row = lax.broadcasted_iota(jnp.int32, (tm + HALO, 1), 0)
            dxc = jnp.where(jnp.logical_and(i == n_tiles - 1, row >= tm), 0.0, dxc)
            o_ref[:, part * D:(part + 1) * D] = _taps_fwd(dxc, w, 4, tm).astype(ACT)
            for k in range(4):
                sh = 3 - k
                shifted = (pltpu.roll(ext, sh, 0) if sh else ext)[HALO:HALO + tm, :]
                dw_ref[k:k + 1, part * D:(part + 1) * D] += jnp.sum(dxc[0:tm, :] * shifted, axis=0, keepdims=True)

    c0 = ZQ // D
    cols = (c0, c0 + 1, c0 + 2)
    tile = [pl.BlockSpec((tm, D), lambda i, c=c: (i, c)) for c in cols]
    dtile = [pl.BlockSpec((tm, D), lambda i, c=c: (i, c)) for c in range(3)]
    in_specs = (tile + [_prev_halo(tm, c) for c in cols] + [_next_halo(tm, c, n_tiles) for c in cols]
                + dtile + [_next_halo(tm, c, n_tiles) for c in range(3)] + [_full((HALO, 3 * D))])
    return _pcall(
        body, name=name, grid=(n_tiles,), in_specs=in_specs,
        out_specs=[pl.BlockSpec((tm, 3 * D), lambda i: (i, 0)), _full((HALO, 3 * D))],
        out_shape=[jax.ShapeDtypeStruct((t, 3 * D), ACT), jax.ShapeDtypeStruct((HALO, 3 * D), F32)],
        compiler_params=_cp(1),
    )(*([p] * 9), *([dqkv] * 6), cw)


def _tri_masks(n):
    row = lax.broadcasted_iota(jnp.int32, (n, n), 0)
    col = lax.broadcasted_iota(jnp.int32, (n, n), 1)
    return row, col


@jax.custom_vjp
def _unit_lower_inverse(a):
    n = DN_CHUNK
    row, col = _tri_masks(n)
    eye = (row == col).astype(F32)
    same16 = (row // 16) == (col // 16)
    same32 = (row // 32) == (col // 32)
    a0 = jnp.where(same16, a, 0.0)
    x = eye - a0
    pw = a0
    for _ in range(3):
        pw = _hnn(pw, pw)
        x = _hnn(x, eye + pw)
    a1 = jnp.where(jnp.logical_and(same32, jnp.logical_not(same16)), a, 0.0)
    x = x - _hnn(x, _hnn(a1, x))
    a2 = jnp.where(same32, 0.0, a)
    return x - _hnn(x, _hnn(a2, x))


def _uli_fwd(a):
    t = _unit_lower_inverse(a)
    return t, t


def _uli_bwd(t, g):
    return (-_htn(t, _hnt(g, t)),)


_unit_lower_inverse.defvjp(_uli_fwd, _uli_bwd)


def _delta_head(state, q, k, v, a, b, alog, dtb, onw):
    n = DN_CHUNK
    row, col = _tri_masks(n)
    incl = row >= col
    beta = jax.nn.sigmoid(b)
    g = -jnp.exp(alog) * jax.nn.softplus(a + dtb)
    lower = incl.astype(F32)
    g128 = g * jnp.ones((1, HEAD), F32)
    g64 = g * jnp.ones((1, n), F32)
    c128 = _hnn(lower, g128)
    c64 = _hnn(lower, g64)
    r64 = _hnn(jnp.ones((n, n), F32), g64 * (row <= col).astype(F32))
    glast = jnp.sum(g128, axis=0, keepdims=True)
    decay = jnp.exp(jnp.where(incl, c64 - r64, -1e30))
    eg = jnp.exp(c128)
    kb = k * beta
    amat = jnp.where(row > col, _nt(kb, k) * decay, 0.0)
    tmat = _unit_lower_inverse(amat)
    u = _nn(tmat, v * beta)
    w = _nn(tmat, kb * eg)
    qk = _nt(q, k) * decay
    v_new = u - _nn(w, state)
    o = _nn(q * eg, state) + _nn(qk, v_new)
    new_state = state * jnp.exp(glast) + _tn(k * jnp.exp(glast - c128), v_new)
    y = o * lax.rsqrt(jnp.mean(o * o, axis=-1, keepdims=True) + EPS) * onw
    return y, new_state


def _delta_operands(hh, q_ref, k_ref, v_ref, ab, alog, dtb):
    lo = hh * HEAD
    a = jnp.sum(ab * _onehot_row(HEAD, hh), axis=1, keepdims=True)
    b = jnp.sum(ab * _onehot_row(HEAD, DN_HEADS + hh), axis=1, keepdims=True)
    al = jnp.sum(alog * _onehot_row(HEAD, hh), axis=1, keepdims=True)
    dt = jnp.sum(dtb * _onehot_row(HEAD, hh), axis=1, keepdims=True)
    return (q_ref[:, lo:lo + HEAD].astype(F32), k_ref[:, lo:lo + HEAD].astype(F32),
            v_ref[:, lo:lo + HEAD].astype(F32), a, b, al, dt)


def _delta_fwd(qkv, ab, alog, dtb, onw, name):
    t = qkv.shape[0]
    nc = t // DN_CHUNK

    def body(q_ref, k_ref, v_ref, ab_ref, alog_ref, dtb_ref, onw_ref, y_ref, keep_ref, state):
        @pl.when(pl.program_id(0) == 0)
        def _():
            state[...] = jnp.zeros_like(state)

        ab_v, alog_v, dtb_v, onw_v = ab_ref[...], alog_ref[...], dtb_ref[...], onw_ref[...]
        for hh in range(DN_HEADS):
            s0 = state[hh]
            keep_ref[0, hh] = s0
            y, s1 = _delta_head(s0, *_delta_operands(hh, q_ref, k_ref, v_ref, ab_v, alog_v, dtb_v), onw_v)
            state[hh] = s1
            y_ref[:, hh * HEAD:(hh + 1) * HEAD] = y.astype(ACT)

    chunk = [pl.BlockSpec((DN_CHUNK, D), lambda i, c=c: (i, c)) for c in range(3)]
    return _pcall(
        body, name=name, grid=(nc,),
        in_specs=chunk + [pl.BlockSpec((DN_CHUNK, HEAD), lambda i: (i, 0)), _full((1, HEAD)), _full((1, HEAD)),
                          _full((1, HEAD))],
        out_specs=[pl.BlockSpec((DN_CHUNK, D), lambda i: (i, 0)),
                   pl.BlockSpec((1, DN_HEADS, HEAD, HEAD), lambda i: (i, 0, 0, 0))],
        out_shape=[jax.ShapeDtypeStruct((t, D), ACT), jax.ShapeDtypeStruct((nc, DN_HEADS, HEAD, HEAD), F32)],
        scratch_shapes=[pltpu.VMEM((DN_HEADS, HEAD, HEAD), F32)], compiler_params=_cp(1),
    )(qkv, qkv, qkv, ab, alog, dtb, onw)


def _delta_bwd(qkv, ab, alog, dtb, onw, keep, dy, name):
    t = qkv.shape[0]
    nc = t // DN_CHUNK

    def body(q_ref, k_ref, v_ref, ab_ref, alog_ref, dtb_ref, onw_ref, keep_ref, dy_ref,
             dqkv_ref, dab_ref, dalog_ref, ddtb_ref, donw_ref, dstate):
        @pl.when(pl.program_id(0) == 0)
        def _():
            dstate[...] = jnp.zeros_like(dstate)
            dalog_ref[...] = jnp.zeros_like(dalog_ref)
            ddtb_ref[...] = jnp.zeros_like(ddtb_ref)
            donw_ref[...] = jnp.zeros_like(donw_ref)

        ab_v, alog_v, dtb_v, onw_v = ab_ref[...], alog_ref[...], dtb_ref[...], onw_ref[...]
        dab = jnp.zeros((DN_CHUNK, HEAD), F32)
        dal = jnp.zeros((1, HEAD), F32)
        ddt = jnp.zeros((1, HEAD), F32)
        don = jnp.zeros((1, HEAD), F32)
        for hh in range(DN_HEADS):
            lo = hh * HEAD
            ops = _delta_operands(hh, q_ref, k_ref, v_ref, ab_v, alog_v, dtb_v)
            _, vjp = jax.vjp(_delta_head, keep_ref[0, hh], *ops, onw_v)
            ds, dq, dk, dv, da, db, dal_h, ddt_h, don_h = vjp((dy_ref[:, lo:lo + HEAD].astype(F32), dstate[hh]))
            dstate[hh] = ds
            dqkv_ref[:, lo:lo + HEAD] = dq.astype(ACT)
            dqkv_ref[:, D + lo:D + lo + HEAD] = dk.astype(ACT)
            dqkv_ref[:, 2 * D + lo:2 * D + lo + HEAD] = dv.astype(ACT)
            dab = dab + da * _onehot_row(HEAD, hh) + db * _onehot_row(HEAD, DN_HEADS + hh)
            dal = dal + dal_h * _onehot_row(HEAD, hh)
            ddt = ddt + ddt_h * _onehot_row(HEAD, hh)
            don = don + don_h
        dab_ref[...] = dab
        dalog_ref[...] += dal
        ddtb_ref[...] += ddt
        donw_ref[...] += don

    rev = lambda i: nc - 1 - i
    chunk = [pl.BlockSpec((DN_CHUNK, D), lambda i, c=c: (rev(i), c)) for c in range(3)]
    small = jax.ShapeDtypeStruct((1, HEAD), F32)
    return _pcall(
        body, name=name, grid=(nc,),
        in_specs=chunk + [pl.BlockSpec((DN_CHUNK, HEAD), lambda i: (rev(i), 0)), _full((1, HEAD)), _full((1, HEAD)),
                          _full((1, HEAD)), pl.BlockSpec((1, DN_HEADS, HEAD, HEAD), lambda i: (rev(i), 0, 0, 0)),
                          pl.BlockSpec((DN_CHUNK, D), lambda i: (rev(i), 0))],
        out_specs=[pl.BlockSpec((DN_CHUNK, 3 * D), lambda i: (rev(i), 0)),
                   pl.BlockSpec((DN_CHUNK, HEAD), lambda i: (rev(i), 0)), _full((1, HEAD)), _full((1, HEAD)),
                   _full((1, HEAD))],
        out_shape=[jax.ShapeDtypeStruct((t, 3 * D), ACT), jax.ShapeDtypeStruct((t, HEAD), F32), small, small, small],
        scratch_shapes=[pltpu.VMEM((DN_HEADS, HEAD, HEAD), F32)], compiler_params=_cp(1),
    )(qkv, qkv, qkv, ab, alog, dtb, onw, keep, dy)


def _loss_head(y, target):
    t = y.shape[0]
    tm = min(512, t)

    def body(y_ref, t_ref, l_ref, dy_ref):
        @pl.when(pl.program_id(0) == 0)
        def _():
            l_ref[...] = jnp.zeros_like(l_ref)

        diff = y_ref[...] - t_ref[...]
        dy_ref[...] = diff * (1.0 / D)
        l_ref[...] += 0.5 * jnp.sum(jnp.sum(diff * diff, axis=-1, keepdims=True) * (1.0 / D), axis=0, keepdims=True)

    row = pl.BlockSpec((tm, D), lambda i: (i, 0))
    return _pcall(body, name="loss_head", grid=(t // tm,), in_specs=[row, row], out_specs=[_full((8, 128)), row],
                  out_shape=[jax.ShapeDtypeStruct((8, 128), F32), jax.ShapeDtypeStruct((t, D), F32)],
                  compiler_params=_cp(1))(y, target)


_SHARDED = (("w_mem_kv", (128, 1024), 0), ("w_out", (4, 192, 1024), 1), ("a_w_in", (2, 1024, 512), 2),
            ("a_ln_w", (2, 128), 1), ("a_ln_b", (2, 128), 1), ("b_w_in", (1, 1024, 640), 2),
            ("b_conv_w", (1, 3, 128), 2), ("c_w_in", (1, 1024, 642), 2), ("c_conv_w", (1, 4, 384), 2))
_REPLICATED = (("mem_norm_w", (1024,)), ("norm_pre", (4, 1024)), ("norm_post", (4, 1024)),
               ("a_w_s", (2, 8, 128, 128)), ("a_b_s", (2, 8, 128)), ("c_a_log", (1, 8)), ("c_dt_bias", (1, 8)),
               ("c_o_norm_w", (1, 128)))


def _pack_rows(flat_parts, rows, lead=()):
    n_lead = len(lead)
    flat = jnp.concatenate([a.reshape(lead + (-1,)) for a in flat_parts], axis=n_lead)
    pad = rows * D - flat.shape[n_lead]
    flat = jnp.pad(flat, [(0, 0)] * n_lead + [(0, pad)])
    return flat.reshape(lead + (rows, D))


def _unpack_rows(packed, table, lead=()):
    flat = packed.reshape(lead + (-1,))
    out, at = {}, 0
    for name, shape in table:
        size = math.prod(shape)
        out[name] = lax.slice_in_dim(flat, at, at + size, axis=len(lead)).reshape(lead + shape)
        at += size
    return out


def _join_shards(blocks, axis):
    moved = jnp.moveaxis(blocks, 0, axis)
    shape = moved.shape
    return moved.reshape(shape[:axis] + (shape[axis] * shape[axis + 1],) + shape[axis + 2:])


def _split_shards(full, axis):
    shape = full.shape
    split = full.reshape(shape[:axis] + (N_DEV, shape[axis] // N_DEV) + shape[axis + 1:])
    return jnp.moveaxis(split, axis, 0)


_A_COLS = ((2560, 1536), (2048, 512), (0, 2048))
_BC_COLS = ((3584, 1536), (3072, 512), (0, 3072))
_C_COLS = ((3600, 1536), (3088, 512), (0, 3072))


def _reorder_cols(w, cols):
    return jnp.concatenate([w[:, s:s + n] for s, n in cols], axis=1)


def _restore_cols(pieces_in_my_order, cols, extra=()):
    placed = sorted(list(zip([s for s, _ in cols], pieces_in_my_order)) + list(extra), key=lambda sp: sp[0])
    return jnp.concatenate([piece for _, piece in placed], axis=1)


def kernel(x, mem, mem_norm_w, w_mem_kv, norm_pre, norm_post, w_out, a_w_in, a_ln_w, a_ln_b, a_w_s, a_b_s, b_w_in, b_conv_w, c_w_in, c_conv_w, c_a_log, c_dt_bias, c_o_norm_w, loss_target, m_mem_norm_w, m_w_mem_kv, m_norm_pre, m_norm_post, m_w_out, m_a_w_in, m_a_ln_w, m_a_ln_b, m_a_w_s, m_a_b_s, m_b_w_in, m_b_conv_w, m_c_w_in, m_c_conv_w, m_c_a_log, m_c_dt_bias, m_c_o_norm_w, v_mem_norm_w, v_w_mem_kv, v_norm_pre, v_norm_post, v_w_out, v_a_w_in, v_a_ln_w, v_a_ln_b, v_a_w_s, v_a_b_s, v_b_w_in, v_b_conv_w, v_c_w_in, v_c_conv_w, v_c_a_log, v_c_dt_bias, v_c_o_norm_w):
    given = dict(locals())
    x0 = x[0]
    mem0 = mem[0]
    target = loss_target[0]

    shard_tab = tuple((n, s) for n, s, _ in _SHARDED)
    w_pack = _pack_rows([given[n] for n, _, _ in _SHARDED], R_SHARD)
    gathered = _unpack_rows(_all_gather(w_pack, "gather_weights"), shard_tab, lead=(N_DEV,))
    full = {n: _join_shards(gathered[n], ax) for n, _, ax in _SHARDED}

    wkv = full["w_mem_kv"].astype(MXU)
    wouts = [full["w_out"][i].astype(MXU) for i in range(4)]
    w_a = [_reorder_cols(full["a_w_in"][j], _A_COLS).astype(MXU) for j in range(2)]
    w_b = _reorder_cols(full["b_w_in"][0], _BC_COLS).astype(MXU)
    w_c = _reorder_cols(full["c_w_in"][0], _C_COLS).astype(MXU)
    w_cab = jnp.pad(full["c_w_in"][0][:, 3072:3088], ((0, 0), (0, HEAD - 16))).astype(MXU)
    lnw = [full["a_ln_w"][j][None, :] for j in range(2)]
    lnb = [full["a_ln_b"][j][None, :] for j in range(2)]
    ws = [a_w_s[j] for j in range(2)]
    bst = [jnp.pad(a_b_s[j].T, ((0, 0), (0, HEAD - GM_GROUPS))) for j in range(2)]
    cw_b = jnp.pad(full["b_conv_w"][0], ((0, HALO - 3), (0, 0)))
    cw_c = jnp.pad(full["c_conv_w"][0], ((0, HALO - 4), (0, 0)))
    alog = jnp.pad(c_a_log, ((0, 0), (0, HEAD - DN_HEADS)))
    dtb = jnp.pad(c_dt_bias, ((0, 0), (0, HEAD - DN_HEADS)))
    onw = c_o_norm_w
    mw = mem_norm_w[None, :]
    w_in = [w_a[0], w_b, w_c, w_a[1]]

    kv = _memkv_fwd(mem0, mw, wkv)
    xs, saved = [x0], []
    for i in range(4):
        kind = i % 3
        npre, npost = norm_pre[i][None, :], norm_post[i][None, :]
        if kind == 2:
            p, h, ab = _proj_fwd(xs[i], npre, w_in[i], w_cab, f"proj_fwd_{i}")
            qkv = _qkv_fwd(p, cw_c, f"qkv_fwd_{i}")
            ymix, keep = _delta_fwd(qkv, ab, alog, dtb, onw, f"delta_fwd_{i}")
            extra = (qkv, ab, keep)
        else:
            p, h = _proj_fwd(xs[i], npre, w_in[i], None, f"proj_fwd_{i}")
            if kind == 0:
                ymix = _gmlp_fwd(p, lnw[i // 3], lnb[i // 3], ws[i // 3], bst[i // 3], f"gmlp_fwd_{i}")
            else:
                ymix = _sconv_fwd(p, cw_b, f"sconv_fwd_{i}")
            extra = ()
        xn, o = _tail_fwd(p, ymix, xs[i], kv, wouts[i], npost, f"tail_fwd_{i}")
        xs.append(xn)
        saved.append((p, h, ymix, o, extra))

    loss_tile, dx = _loss_head(xs[4], target)
    loss = lax.psum(loss_tile[0, 0], ("x", "y", "c"))

    g = {}
    d_npre, d_npost, d_wout = [None] * 4, [None] * 4, [None] * 4
    d_ws, d_bs, d_lnw, d_lnb, d_wa = [None] * 2, [None] * 2, [None] * 2, [None] * 2, [None] * 2
    dkv = None
    for i in reversed(range(4)):
        kind = i % 3
        p, h, ymix, o, extra = saved[i]
        npre, npost = norm_pre[i][None, :], norm_post[i][None, :]
        dzq, dymix, d_wout[i], d_npost[i], dkv_i = _tail_bwd(dx, o, p, ymix, kv, wouts[i], npost, f"tail_bwd_{i}")
        dkv = dkv_i if dkv is None else dkv + dkv_i
        w_zq, w_mix = w_in[i][:, :ZQ], w_in[i][:, ZQ:]
        dw_zq = _matmul_tn(h, dzq, f"dw_zq_{i}")
        if kind == 0:
            j = i // 3
            dmix, d_lnw[j], d_lnb[j], d_ws[j], dbst = _gmlp_bwd(p, dymix, lnw[j], lnb[j], ws[j], bst[j], f"gmlp_bwd_{i}")
            d_bs[j] = dbst[:, :GM_GROUPS].T
            dw_mix = _matmul_tn(h, dmix, f"dw_mix_{i}")
            d_wa[j] = _restore_cols([dw_zq[:, :D_CAT], dw_zq[:, D_CAT:], dw_mix], _A_COLS)
            dps, wparts = [dzq, dmix], [w_zq, w_mix]
        elif kind == 1:
            dmix, dcw = _sconv_bwd(p, dymix, cw_b, f"sconv_bwd_{i}")
            g["b_conv_w"] = dcw[None, :3]
            dw_mix = _matmul_tn(h, dmix, f"dw_mix_{i}")
            g["b_w_in"] = _restore_cols([dw_zq[:, :D_CAT], dw_zq[:, D_CAT:], dw_mix], _BC_COLS)[None]
            dps, wparts = [dzq, dmix], [w_zq, w_mix]
        else:
            qkv, ab, keep = extra
            dqkv, dab, dalog, ddtb, donw = _delta_bwd(qkv, ab, alog, dtb, onw, keep, dymix, f"delta_bwd_{i}")
            dmix, dcw = _qkv_bwd(p, dqkv, cw_c, f"qkv_bwd_{i}")
            g["c_conv_w"] = dcw[None, :4]
            g["c_a_log"], g["c_dt_bias"], g["c_o_norm_w"] = dalog[:, :DN_HEADS], ddtb[:, :DN_HEADS], donw
            dw_mix = _matmul_tn(h, dmix, f"dw_mix_{i}")
            dw_ab = _matmul_tn(h, dab, f"dw_ab_{i}")
            g["c_w_in"] = _restore_cols([dw_zq[:, :D_CAT], dw_zq[:, D_CAT:], dw_mix], _C_COLS,
                                        extra=[(3072, dw_ab[:, :16])])[None]
            dps, wparts = [dzq, dmix, dab], [w_zq, w_mix, w_cab]
        dx, d_npre[i] = _proj_bwd_x(dps, wparts, xs[i], dx, npre, f"proj_bwd_x_{i}")

    d_wkv, d_mw = _memkv_bwd(mem0, mw, wkv, dkv)
    g["w_mem_kv"], g["mem_norm_w"] = d_wkv, d_mw[0]
    g["norm_pre"] = jnp.concatenate(d_npre, axis=0)
    g["norm_post"] = jnp.concatenate(d_npost, axis=0)
    g["w_out"] = jnp.stack(d_wout)
    g["a_w_in"] = jnp.stack(d_wa)
    g["a_ln_w"] = jnp.concatenate(d_lnw, axis=0)
    g["a_ln_b"] = jnp.concatenate(d_lnb, axis=0)
    g["a_w_s"] = jnp.stack(d_ws)
    g["a_b_s"] = jnp.stack(d_bs)

    g_pack = _pack_rows([_split_shards(g[n], ax) for n, _, ax in _SHARDED], R_SHARD, lead=(N_DEV,))
    landed = _all_to_all(g_pack, "exchange_grads")
    sh = _reduce_adamw(landed, w_pack, _pack_rows([given["m_" + n] for n, _, _ in _SHARDED], R_SHARD),
                       _pack_rows([given["v_" + n] for n, _, _ in _SHARDED], R_SHARD), "adamw_sharded")
    r_pack = _pack_rows([g[n] for n, _ in _REPLICATED], R_REPL)
    r_all = _all_gather(r_pack, "gather_small_grads")
    rp = _reduce_adamw(r_all, _pack_rows([given[n] for n, _ in _REPLICATED], R_REPL),
                       _pack_rows([given["m_" + n] for n, _ in _REPLICATED], R_REPL),
                       _pack_rows([given["v_" + n] for n, _ in _REPLICATED], R_REPL), "adamw_replicated")

    sh = [_unpack_rows(a, shard_tab) for a in sh]
    rp = [_unpack_rows(a, _REPLICATED) for a in rp]
    order = ["mem_norm_w", "w_mem_kv", "norm_pre", "norm_post", "w_out", "a_w_in", "a_ln_w", "a_ln_b", "a_w_s", "a_b_s",
             "b_w_in", "b_conv_w", "c_w_in", "c_conv_w", "c_a_log", "c_dt_bias", "c_o_norm_w"]
    sharded_names = {n for n, _, _ in _SHARDED}
    outs = [loss, dx[None]]
    for kind in range(4):
        outs += [(sh if n in sharded_names else rp)[kind][n] for n in order]
    return tuple(outs)
```

```python
import functools
import math

import jax
import jax.numpy as jnp
from jax import lax
from jax.experimental import pallas as pl
from jax.experimental.pallas import tpu as pltpu

F32 = jnp.float32
MXU = jnp.bfloat16
ACT = jnp.bfloat16

D = 1024
D_XA = 512
D_CAT = 1536
N_MEM = 256
XA_HEADS = 4
HEAD = 128
ZQ = D_CAT + D_XA
EPS = 1e-6
GM_CHUNK = 128
GM_GROUPS = 8
DN_HEADS = 8
DN_CHUNK = 64
N_DEV = 8
HALO = 8
VMEM_LIMIT = 56 * 1024 * 1024
XCH = jnp.bfloat16
R_REPL = 272

ADAM_LR = 0.001
ADAM_B1 = 0.9
ADAM_B2 = 0.999
ADAM_EPS = 1e-08
ADAM_WD = 0.01
ADAM_STEP = 10

NN = ((1,), (0,))
NT = ((1,), (1,))
TN = ((0,), (0,))
MESH = pl.DeviceIdType.MESH


def _pcall(body, **kw):
    return pl.pallas_call(body, **kw)


def _cp(n_axes):
    return pltpu.CompilerParams(dimension_semantics=("arbitrary",) * n_axes, vmem_limit_bytes=VMEM_LIMIT)


def _dot(a, b, dims, prec=None):
    return lax.dot_general(a, b, (dims, ((), ())), preferred_element_type=F32, precision=prec)


def _mdot(a, b, dims):
    return _dot(a.astype(MXU), b.astype(MXU), dims)


def _make_mms(raw):
    @jax.custom_vjp
    def nn(a, b):
        return raw(a, b, NN)

    @jax.custom_vjp
    def nt(a, b):
        return raw(a, b, NT)

    @jax.custom_vjp
    def tn(a, b):
        return raw(a, b, TN)

    nn.defvjp(lambda a, b: (nn(a, b), (a, b)), lambda r, g: (nt(g, r[1]), tn(r[0], g)))
    nt.defvjp(lambda a, b: (nt(a, b), (a, b)), lambda r, g: (nn(g, r[1]), tn(g, r[0])))
    tn.defvjp(lambda a, b: (tn(a, b), (a, b)), lambda r, g: (nt(r[1], g), nn(r[0], g)))
    return nn, nt, tn


_nn, _nt, _tn = _make_mms(_mdot)
_hnn, _hnt, _htn = _make_mms(lambda a, b, dims: _dot(a, b, dims, lax.Precision.HIGHEST))


def _full(shape):
    return pl.BlockSpec(shape, lambda *_: (0,) * len(shape))


def _silu(z):
    return z * jax.nn.sigmoid(z)


def _dsilu(z):
    s = jax.nn.sigmoid(z)
    return s * (1.0 + z * (1.0 - s))


def _onehot_row(n, k):
    return (lax.broadcasted_iota(jnp.int32, (1, n), 1) == k).astype(F32)


_HBM = pl.BlockSpec(memory_space=pl.ANY)


def _all_gather(blks, name):
    n = len(blks)

    def body(*refs):
        x_refs, out_refs = refs[:n], refs[n:2 * n]
        send_sems, recv_sems, local_sems = refs[2 * n:]
        x, y, cc = lax.axis_index("x"), lax.axis_index("y"), lax.axis_index("c")
        me, sibling = (x, y, cc), (x, y, 1 - cc)
        chips = [(1 - x, y), (x, 1 - y), (1 - x, 1 - y)]

        def slot(a, px, py, pc):
            return out_refs[a].at[4 * px + 2 * py + pc]

        def copy(k, a, block, to, src=None):
            return pltpu.make_async_remote_copy(
                src_ref=slot(a, *block) if src is None else src, dst_ref=slot(a, *block),
                send_sem=send_sems.at[k * n + a], recv_sem=recv_sems.at[k * n + a], device_id=to, device_id_type=MESH)

        mine = [pltpu.make_async_copy(x_refs[a], slot(a, *me), local_sems.at[a]) for a in range(n)]
        for cp in mine:
            cp.start()
        first = [copy(0, a, me, sibling, src=x_refs[a]) for a in range(n)]
        first += [copy(1 + j, a, me, (*chip, cc), src=x_refs[a]) for j, chip in enumerate(chips) for a in range(n)]
        for cp in first:
            cp.start()
        passed = []
        for j, chip in enumerate(chips):
            for a in range(n):
                copy(1 + j, a, (*chip, cc), me).wait_recv()
                passed.append(copy(4 + j, a, (*chip, cc), sibling))
                passed[-1].start()
        for a in range(n):
            copy(0, a, sibling, me).wait_recv()
        for j, chip in enumerate(chips):
            for a in range(n):
                copy(4 + j, a, (*chip, 1 - cc), me).wait_recv()
        for cp in first + passed:
            cp.wait_send()
        for cp in mine:
            cp.wait()

    return _pcall(
        body, name=name,
        out_shape=[jax.ShapeDtypeStruct((N_DEV,) + b.shape, b.dtype) for b in blks],
        in_specs=[_HBM] * n, out_specs=[_HBM] * n,
        scratch_shapes=[pltpu.SemaphoreType.DMA((7 * n,)), pltpu.SemaphoreType.DMA((7 * n,)),
                        pltpu.SemaphoreType.DMA((n,))],
    )(*blks)


def _exchange(gs, bcast, name):
    n = len(gs)
    per_peer = n + 1

    def body(*refs):
        g_refs, b_ref = refs[:n], refs[n]
        out_refs, ball_ref = refs[n + 1:2 * n + 1], refs[2 * n + 1]
        send_sems, recv_sems, local_sems = refs[2 * n + 2:]
        x, y, cc = lax.axis_index("x"), lax.axis_index("y"), lax.axis_index("c")
        my_idx = 4 * x + 2 * y + cc
        mine = [pltpu.make_async_copy(g_refs[a].at[my_idx], out_refs[a].at[my_idx], local_sems.at[a]) for a in range(n)]
        mine.append(pltpu.make_async_copy(b_ref, ball_ref.at[my_idx], local_sems.at[n]))
        for cp in mine:
            cp.start()
        copies = []
        for k in range(1, N_DEV):
            px = 1 - x if k & 4 else x
            py = 1 - y if k & 2 else y
            pc = 1 - cc if k & 1 else cc
            base = (k - 1) * per_peer
            for a in range(n):
                copies.append(pltpu.make_async_remote_copy(
                    src_ref=g_refs[a].at[4 * px + 2 * py + pc], dst_ref=out_refs[a].at[my_idx],
                    send_sem=send_sems.at[base + a], recv_sem=recv_sems.at[base + a],
                    device_id=(px, py, pc), device_id_type=MESH))
            copies.append(pltpu.make_async_remote_copy(
                src_ref=b_ref, dst_ref=ball_ref.at[my_idx], send_sem=send_sems.at[base + n],
                recv_sem=recv_sems.at[base + n], device_id=(px, py, pc), device_id_type=MESH))
        for cp in copies:
            cp.start()
        for cp in copies:
            cp.wait_recv()
        for cp in copies:
            cp.wait_send()
        for cp in mine:
            cp.wait()

    out_shape = [jax.ShapeDtypeStruct(g.shape, g.dtype) for g in gs]
    out_shape.append(jax.ShapeDtypeStruct((N_DEV,) + bcast.shape, bcast.dtype))
    outs = _pcall(
        body, name=name, out_shape=out_shape, in_specs=[_HBM] * per_peer, out_specs=[_HBM] * per_peer,
        scratch_shapes=[pltpu.SemaphoreType.DMA((7 * per_peer,)), pltpu.SemaphoreType.DMA((7 * per_peer,)),
                        pltpu.SemaphoreType.DMA((per_peer,))],
    )(*gs, bcast)
    return outs[:n], outs[n]


def _reduce_adamw(parts, w, m, v, name):
    r, c = w.shape
    tr = 128 if r % 128 == 0 else r

    def body(p_ref, w_ref, m_ref, v_ref, g_ref, d_ref, nm_ref, nv_ref):
        g = p_ref[0].astype(F32)
        for s in range(1, N_DEV):
            g = g + p_ref[s].astype(F32)
        nm = ADAM_B1 * m_ref[...] + (1.0 - ADAM_B1) * g
        nv = ADAM_B2 * v_ref[...] + (1.0 - ADAM_B2) * (g * g)
        m_hat = nm / (1.0 - ADAM_B1 ** ADAM_STEP)
        v_hat = nv / (1.0 - ADAM_B2 ** ADAM_STEP)
        g_ref[...] = g
        d_ref[...] = -ADAM_LR * (m_hat / (jnp.sqrt(v_hat) + ADAM_EPS) + ADAM_WD * w_ref[...])
        nm_ref[...] = nm
        nv_ref[...] = nv

    row = pl.BlockSpec((tr, c), lambda i: (i, 0))
    out = jax.ShapeDtypeStruct((r, c), F32)
    return _pcall(
        body, name=name, grid=(r // tr,),
        in_specs=[pl.BlockSpec((N_DEV, tr, c), lambda i: (0, i, 0)), row, row, row],
        out_specs=[row, row, row, row], out_shape=[out, out, out, out], compiler_params=_cp(1),
    )(parts, w, m, v)


def _proj_fwd(x, nw, w, wab, name):
    t, npj = x.shape[0], w.shape[1]
    tm, tn = min(512, t), 1024
    has_ab = wab is not None

    def body(*refs):
        if has_ab:
            x_ref, nw_ref, w_ref, wab_ref, p_ref, h_ref, ab_ref, hs = refs
        else:
            x_ref, nw_ref, w_ref, p_ref, h_ref, hs = refs

        @pl.when(pl.program_id(1) == 0)
        def _():
            xv = x_ref[...]
            hv = xv * lax.rsqrt(jnp.mean(xv * xv, axis=-1, keepdims=True) + EPS) * nw_ref[...]
            hs[...] = hv.astype(MXU)
            h_ref[...] = hv.astype(ACT)
            if has_ab:
                ab_ref[...] = _mdot(hv, wab_ref[...], NN)

        p_ref[...] = _dot(hs[...], w_ref[...], NN).astype(ACT)

    in_specs = [pl.BlockSpec((tm, D), lambda i, j: (i, 0)), _full((1, D)), pl.BlockSpec((D, tn), lambda i, j: (0, j))]
    out_specs = [pl.BlockSpec((tm, tn), lambda i, j: (i, j)), pl.BlockSpec((tm, D), lambda i, j: (i, 0))]
    out_shape = [jax.ShapeDtypeStruct((t, npj), ACT), jax.ShapeDtypeStruct((t, D), ACT)]
    args = [x, nw, w]
    if has_ab:
        in_specs.append(_full((D, HEAD)))
        out_specs.append(pl.BlockSpec((tm, HEAD), lambda i, j: (i, 0)))
        out_shape.append(jax.ShapeDtypeStruct((t, HEAD), F32))
        args.append(wab)
    return _pcall(body, name=name, grid=(t // tm, npj // tn), in_specs=in_specs, out_specs=out_specs,
                  out_shape=out_shape, scratch_shapes=[pltpu.VMEM((tm, D), MXU)], compiler_params=_cp(2))(*args)


def _proj_bwd_x(dps, ws, x, dxn, nw, name):
    t = x.shape[0]
    tm = min(256, t)
    n = len(dps)

    def body(*refs):
        dp_refs, w_refs = refs[:n], refs[n:2 * n]
        x_ref, dxn_ref, nw_ref, dx_ref, dnw_ref = refs[2 * n:]
        dh = _mdot(dp_refs[0][...], w_refs[0][...], NT)
        for k in range(1, n):
            dh = dh + _mdot(dp_refs[k][...], w_refs[k][...], NT)
        xv = x_ref[...]
        r = lax.rsqrt(jnp.mean(xv * xv, axis=-1, keepdims=True) + EPS)

        @pl.when(pl.program_id(0) == 0)
        def _():
            dnw_ref[...] = jnp.zeros_like(dnw_ref)

        dnw_ref[...] += jnp.sum(dh * xv * r, axis=0, keepdims=True)
        dhw = dh * nw_ref[...]
        dx_ref[...] = dxn_ref[...] + r * (dhw - xv * (r * r) * jnp.mean(dhw * xv, axis=-1, keepdims=True))

    row = pl.BlockSpec((tm, D), lambda i: (i, 0))
    in_specs = [pl.BlockSpec((tm, dp.shape[1]), lambda i: (i, 0)) for dp in dps]
    in_specs += [_full(w.shape) for w in ws]
    in_specs += [row, row, _full((1, D))]
    return _pcall(body, name=name, grid=(t // tm,), in_specs=in_specs, out_specs=[row, _full((1, D))],
                  out_shape=[jax.ShapeDtypeStruct((t, D), F32), jax.ShapeDtypeStruct((1, D), F32)],
                  compiler_params=_cp(1))(*dps, *ws, x, dxn, nw)


def _matmul_tn(a, b, name):
    t, m = a.shape
    n = b.shape[1]
    tm, tn = min(512, t), min(1024, n)

    def body(a_ref, b_ref, o_ref):
        @pl.when(pl.program_id(1) == 0)
        def _():
            o_ref[...] = jnp.zeros_like(o_ref)

        o_ref[...] += _mdot(a_ref[...], b_ref[...], TN)

    return _pcall(body, name=name, grid=(n // tn, t // tm),
                  in_specs=[pl.BlockSpec((tm, m), lambda j, i: (i, 0)), pl.BlockSpec((tm, tn), lambda j, i: (i, j))],
                  out_specs=pl.BlockSpec((m, tn), lambda j, i: (0, j)),
                  out_shape=jax.ShapeDtypeStruct((m, n), F32), compiler_params=_cp(2))(a, b)


def _memkv_fwd(mem, mw, wkv):
    def body(mem_ref, mw_ref, w_ref, kv_ref):
        mv = mem_ref[...]
        mn = mv * lax.rsqrt(jnp.mean(mv * mv, axis=-1, keepdims=True) + EPS) * mw_ref[...]
        kv_ref[...] = _mdot(mn, w_ref[...], NN)

    return _pcall(body, name="memkv_fwd", out_shape=jax.ShapeDtypeStruct((N_MEM, 2 * D_XA), F32),
                  compiler_params=pltpu.CompilerParams(vmem_limit_bytes=VMEM_LIMIT))(mem, mw, wkv)


def _memkv_bwd(mem, mw, wkv, dkv):
    def body(mem_ref, mw_ref, w_ref, dkv_ref, dw_ref, dmw_ref):
        mv = mem_ref[...]
        r = lax.rsqrt(jnp.mean(mv * mv, axis=-1, keepdims=True) + EPS)
        mn = mv * r * mw_ref[...]
        dkvv = dkv_ref[...]
        dw_ref[...] = _mdot(mn, dkvv, TN)
        dmn = _mdot(dkvv, w_ref[...], NT)
        dmw_ref[...] = jnp.sum(dmn * mv * r, axis=0, keepdims=True)

    return _pcall(body, name="memkv_bwd",
                  out_shape=[jax.ShapeDtypeStruct((D, 2 * D_XA), F32), jax.ShapeDtypeStruct((1, D), F32)],
                  compiler_params=pltpu.CompilerParams(vmem_limit_bytes=VMEM_LIMIT))(mem, mw, wkv, dkv)


def _attend(q, kv):
    ps, outs = [], []
    for hh in range(XA_HEADS):
        qh = q[:, hh * HEAD:(hh + 1) * HEAD]
        kh = kv[:, hh * HEAD:(hh + 1) * HEAD]
        vh = kv[:, D_XA + hh * HEAD:D_XA + (hh + 1) * HEAD]
        s = _mdot(qh, kh, NT) * (HEAD ** -0.5)
        s = s - jnp.max(s, axis=-1, keepdims=True)
        e = jnp.exp(s)
        p = e / jnp.sum(e, axis=-1, keepdims=True)
        ps.append(p)
        outs.append(_mdot(p, vh, NN))
    return ps, outs


def _tail_fwd(p, ymix, x, kv, wout, npost, name):
    t = x.shape[0]
    tm = min(256, t)

    def body(z_ref, q_ref, y_ref, x_ref, kv_ref, w_ref, np_ref, xn_ref, o_ref):
        _, outs = _attend(q_ref[...].astype(F32), kv_ref[...])
        cat = jnp.concatenate([y_ref[...].astype(F32)] + outs, axis=1)
        g = cat * _silu(z_ref[...].astype(F32))
        o = _mdot(g, w_ref[...], NN)
        o_ref[...] = o
        xn_ref[...] = x_ref[...] + o * lax.rsqrt(jnp.mean(o * o, axis=-1, keepdims=True) + EPS) * np_ref[...]

    row = pl.BlockSpec((tm, D), lambda i: (i, 0))
    return _pcall(
        body, name=name, grid=(t // tm,),
        in_specs=[pl.BlockSpec((tm, D_CAT), lambda i: (i, 0)), pl.BlockSpec((tm, D_XA), lambda i: (i, D_CAT // D_XA)),
                  row, row, _full((N_MEM, 2 * D_XA)), _full((D_CAT, D)), _full((1, D))],
        out_specs=[row, row],
        out_shape=[jax.ShapeDtypeStruct((t, D), F32), jax.ShapeDtypeStruct((t, D), F32)], compiler_params=_cp(1),
    )(p, p, ymix, x, kv, wout, npost)


def _tail_bwd(dxn, o, p, ymix, kv, wout, npost, name):
    t = dxn.shape[0]
    tm = min(256, t)

    def body(dxn_ref, o_ref, z_ref, q_ref, y_ref, kv_ref, w_ref, np_ref, dzq_ref, dy_ref, dw_ref, dnp_ref, dkv_ref):
        @pl.when(pl.program_id(0) == 0)
        def _():
            dw_ref[...] = jnp.zeros_like(dw_ref)
            dnp_ref[...] = jnp.zeros_like(dnp_ref)
            dkv_ref[...] = jnp.zeros_like(dkv_ref)

        q = q_ref[...].astype(F32)
        kvv = kv_ref[...]
        z = z_ref[...].astype(F32)
        ps, outs = _attend(q, kvv)
        cat = jnp.concatenate([y_ref[...].astype(F32)] + outs, axis=1)
        sz = _silu(z)
        g = cat * sz
        ov = o_ref[...]
        dr = dxn_ref[...]
        rr = lax.rsqrt(jnp.mean(ov * ov, axis=-1, keepdims=True) + EPS)
        dnp_ref[...] += jnp.sum(dr * ov * rr, axis=0, keepdims=True)
        dow = dr * np_ref[...]
        do = rr * (dow - ov * (rr * rr) * jnp.mean(dow * ov, axis=-1, keepdims=True))
        dg = _mdot(do, w_ref[...], NT)
        dw_ref[...] += _mdot(g, do, TN)
        dcat = dg * sz
        dzq_ref[:, 0:D_CAT] = (dg * cat * _dsilu(z)).astype(ACT)
        dy_ref[...] = dcat[:, 0:D].astype(ACT)
        for hh in range(XA_HEADS):
            lo = hh * HEAD
            doh = dcat[:, D + lo:D + lo + HEAD]
            qh = q[:, lo:lo + HEAD]
            kh = kvv[:, lo:lo + HEAD]
            vh = kvv[:, D_XA + lo:D_XA + lo + HEAD]
            ph = ps[hh]
            dp = _mdot(doh, vh, NT)
            ds = ph * (dp - jnp.sum(dp * ph, axis=-1, keepdims=True))
            dzq_ref[:, D_CAT + lo:D_CAT + lo + HEAD] = (_mdot(ds, kh, NN) * (HEAD ** -0.5)).astype(ACT)
            dkv_ref[:, lo:lo + HEAD] += _mdot(ds, qh, TN) * (HEAD ** -0.5)
            dkv_ref[:, D_XA + lo:D_XA + lo + HEAD] += _mdot(ph, doh, TN)

    row = pl.BlockSpec((tm, D), lambda i: (i, 0))
    return _pcall(
        body, name=name, grid=(t // tm,),
        in_specs=[row, row, pl.BlockSpec((tm, D_CAT), lambda i: (i, 0)),
                  pl.BlockSpec((tm, D_XA), lambda i: (i, D_CAT // D_XA)), row,
                  _full((N_MEM, 2 * D_XA)), _full((D_CAT, D)), _full((1, D))],
        out_specs=[pl.BlockSpec((tm, ZQ), lambda i: (i, 0)), row, _full((D_CAT, D)), _full((1, D)),
                   _full((N_MEM, 2 * D_XA))],
        out_shape=[jax.ShapeDtypeStruct((t, ZQ), ACT), jax.ShapeDtypeStruct((t, D), ACT),
                   jax.ShapeDtypeStruct((D_CAT, D), F32), jax.ShapeDtypeStruct((1, D), F32),
                   jax.ShapeDtypeStruct((N_MEM, 2 * D_XA), F32)],
        compiler_params=_cp(1),
    )(dxn, o, p, p, ymix, kv, wout, npost)


def _gmlp_chunk(us, vs, lnws, lnbs, wss, bss):
    gv = [jax.nn.gelu(v) for v in vs]
    mean = sum(jnp.sum(v, axis=-1, keepdims=True) for v in gv) / D
    cen = [v - mean for v in gv]
    var = sum(jnp.sum(c * c, axis=-1, keepdims=True) for c in cen) / D
    rstd = lax.rsqrt(var + EPS)
    row = lax.broadcasted_iota(jnp.int32, (GM_CHUNK, GM_CHUNK), 0)
    col = lax.broadcasted_iota(jnp.int32, (GM_CHUNK, GM_CHUNK), 1)
    ys = []
    for g in range(GM_GROUPS):
        vn = cen[g] * rstd * lnws[g] + lnbs[g]
        sp = _nn(jnp.where(row >= col, wss[g], 0.0), vn) + bss[g]
        ys.append(jax.nn.gelu(us[g]) * sp)
    return ys


def _split_cols(v, n, width=HEAD):
    return [v[:, k * width:(k + 1) * width] for k in range(n)]


def _gmlp_operands(u_ref, v_ref, lnw_ref, lnb_ref, ws_ref, bst_ref):
    us = _split_cols(u_ref[...].astype(F32), GM_GROUPS)
    vs = _split_cols(v_ref[...].astype(F32), GM_GROUPS)
    lnws = _split_cols(lnw_ref[...], GM_GROUPS)
    lnbs = _split_cols(lnb_ref[...], GM_GROUPS)
    wss = [ws_ref[g] for g in range(GM_GROUPS)]
    bst = bst_ref[...]
    bss = [jnp.sum(bst * _onehot_row(HEAD, g), axis=1, keepdims=True) for g in range(GM_GROUPS)]
    return us, vs, lnws, lnbs, wss, bss


def _gmlp_specs():
    return [pl.BlockSpec((GM_CHUNK, D), lambda i: (i, ZQ // D)), pl.BlockSpec((GM_CHUNK, D), lambda i: (i, ZQ // D + 1)),
            _full((1, D)), _full((1, D)), _full((GM_GROUPS, GM_CHUNK, GM_CHUNK)), _full((GM_CHUNK, HEAD))]


def _gmlp_fwd(p, lnw, lnb, ws, bst, name):
    t = p.shape[0]

    def body(u_ref, v_ref, lnw_ref, lnb_ref, ws_ref, bst_ref, y_ref):
        ys = _gmlp_chunk(*_gmlp_operands(u_ref, v_ref, lnw_ref, lnb_ref, ws_ref, bst_ref))
        for g in range(GM_GROUPS):
            y_ref[:, g * HEAD:(g + 1) * HEAD] = ys[g].astype(ACT)

    return _pcall(body, name=name, grid=(t // GM_CHUNK,), in_specs=_gmlp_specs(),
                  out_specs=pl.BlockSpec((GM_CHUNK, D), lambda i: (i, 0)),
                  out_shape=jax.ShapeDtypeStruct((t, D), ACT), compiler_params=_cp(1))(p, p, lnw, lnb, ws, bst)


def _gmlp_bwd(p, dy, lnw, lnb, ws, bst, name):
    t = p.shape[0]

    def body(u_ref, v_ref, lnw_ref, lnb_ref, ws_ref, bst_ref, dy_ref, duv_ref, dlnw_ref, dlnb_ref, dws_ref, dbst_ref):
        @pl.when(pl.program_id(0) == 0)
        def _():
            dlnw_ref[...] = jnp.zeros_like(dlnw_ref)
            dlnb_ref[...] = jnp.zeros_like(dlnb_ref)
            dws_ref[...] = jnp.zeros_like(dws_ref)
            dbst_ref[...] = jnp.zeros_like(dbst_ref)

        ops = _gmlp_operands(u_ref, v_ref, lnw_ref, lnb_ref, ws_ref, bst_ref)
        _, vjp = jax.vjp(_gmlp_chunk, *ops)
        dus, dvs, dlnws, dlnbs, dwss, dbss = vjp(_split_cols(dy_ref[...].astype(F32), GM_GROUPS))
        dbst = jnp.zeros((GM_CHUNK, HEAD), F32)
        for g in range(GM_GROUPS):
            lo = g * HEAD
            duv_ref[:, lo:lo + HEAD] = dus[g].astype(ACT)
            duv_ref[:, D + lo:D + lo + HEAD] = dvs[g].astype(ACT)
            dlnw_ref[:, lo:lo + HEAD] += dlnws[g]
            dlnb_ref[:, lo:lo + HEAD] += dlnbs[g]
            dws_ref[g] += dwss[g]
            dbst = dbst + dbss[g] * _onehot_row(HEAD, g)
        dbst_ref[...] += dbst

    return _pcall(
        body, name=name, grid=(t // GM_CHUNK,),
        in_specs=_gmlp_specs() + [pl.BlockSpec((GM_CHUNK, D), lambda i: (i, 0))],
        out_specs=[pl.BlockSpec((GM_CHUNK, 2 * D), lambda i: (i, 0)), _full((1, D)), _full((1, D)),
                   _full((GM_GROUPS, GM_CHUNK, GM_CHUNK)), _full((GM_CHUNK, HEAD))],
        out_shape=[jax.ShapeDtypeStruct((t, 2 * D), ACT), jax.ShapeDtypeStruct((1, D), F32),
                   jax.ShapeDtypeStruct((1, D), F32), jax.ShapeDtypeStruct((GM_GROUPS, GM_CHUNK, GM_CHUNK), F32),
                   jax.ShapeDtypeStruct((GM_CHUNK, HEAD), F32)],
        compiler_params=_cp(1),
    )(p, p, lnw, lnb, ws, bst, dy)


def _prev_halo(tm, col):
    return pl.BlockSpec((HALO, D), lambda i: (jnp.maximum(i * (tm // HALO) - 1, 0), col))


def _next_halo(tm, col, n_tiles):
    return pl.BlockSpec((HALO, D), lambda i: (jnp.minimum(i + 1, n_tiles - 1) * (tm // HALO), col))


def _taps_back(ext, w, width):
    acc = None
    for k in range(width):
        s = width - 1 - k
        term = w[k:k + 1, :] * (pltpu.roll(ext, s, 0) if s else ext)[HALO:, :]
        acc = term if acc is None else acc + term
    return acc


def _taps_fwd(ext, w, width, n):
    rows = ext.shape[0]
    acc = None
    for k in range(width):
        s = width - 1 - k
        term = w[k:k + 1, :] * (pltpu.roll(ext, rows - s, 0) if s else ext)[0:n, :]
        acc = term if acc is None else acc + term
    return acc


def _sconv_fwd(p, cw, name):
    t = p.shape[0]
    tm = min(256, t)

    def body(b_ref, c_ref, h_ref, cp_ref, hp_ref, w_ref, y_ref):
        first = pl.program_id(0) == 0
        prev = jnp.where(first, 0.0, cp_ref[...].astype(F32) * hp_ref[...].astype(F32))
        ext = jnp.concatenate([prev, c_ref[...].astype(F32) * h_ref[...].astype(F32)], axis=0)
        y_ref[...] = (b_ref[...].astype(F32) * _taps_back(ext, w_ref[...], 3)).astype(ACT)

    c0 = ZQ // D
    tile = [pl.BlockSpec((tm, D), lambda i, c=c: (i, c)) for c in (c0, c0 + 1, c0 + 2)]
    return _pcall(body, name=name, grid=(t // tm,),
                  in_specs=tile + [_prev_halo(tm, c0 + 1), _prev_halo(tm, c0 + 2), _full((HALO, D))],
                  out_specs=pl.BlockSpec((tm, D), lambda i: (i, 0)),
                  out_shape=jax.ShapeDtypeStruct((t, D), ACT), compiler_params=_cp(1))(p, p, p, p, p, cw)


def _sconv_bwd(p, dy, cw, name):
    t = p.shape[0]
    tm = min(256, t)
    n_tiles = t // tm

    def body(b_ref, c_ref, h_ref, cp_ref, hp_ref, bn_ref, dy_ref, dyn_ref, w_ref, d_ref, dw_ref):
        i = pl.program_id(0)

        @pl.when(i == 0)
        def _():
            dw_ref[...] = jnp.zeros_like(dw_ref)

        w = w_ref[...]
        bv, cv, hv = b_ref[...].astype(F32), c_ref[...].astype(F32), h_ref[...].astype(F32)
        dyv = dy_ref[...].astype(F32)
        prev = jnp.where(i == 0, 0.0, cp_ref[...].astype(F32) * hp_ref[...].astype(F32))
        ext = jnp.concatenate([prev, cv * hv], axis=0)
        conv = _taps_back(ext, w, 3)
        dconv = dyv * bv
        nxt = jnp.where(i == n_tiles - 1, 0.0, dyn_ref[...].astype(F32) * bn_ref[...].astype(F32))
        dc = _taps_fwd(jnp.concatenate([dconv, nxt], axis=0), w, 3, tm)
        d_ref[:, 0:D] = (dyv * conv).astype(ACT)
        d_ref[:, D:2 * D] = (dc * hv).astype(ACT)
        d_ref[:, 2 * D:3 * D] = (dc * cv).astype(ACT)
        for k in range(3):
            s = 2 - k
            shifted = (pltpu.roll(ext, s, 0) if s else ext)[HALO:, :]
            dw_ref[k:k + 1, :] += jnp.sum(dconv * shifted, axis=0, keepdims=True)

    c0 = ZQ // D
    tile = [pl.BlockSpec((tm, D), lambda i, c=c: (i, c)) for c in (c0, c0 + 1, c0 + 2)]
    return _pcall(
        body, name=name, grid=(n_tiles,),
        in_specs=tile + [_prev_halo(tm, c0 + 1), _prev_halo(tm, c0 + 2), _next_halo(tm, c0, n_tiles),
                         pl.BlockSpec((tm, D), lambda i: (i, 0)), _next_halo(tm, 0, n_tiles), _full((HALO, D))],
        out_specs=[pl.BlockSpec((tm, 3 * D), lambda i: (i, 0)), _full((HALO, D))],
        out_shape=[jax.ShapeDtypeStruct((t, 3 * D), ACT), jax.ShapeDtypeStruct((HALO, D), F32)],
        compiler_params=_cp(1),
    )(p, p, p, p, p, p, dy, dy, cw)


def _l2_heads(s, scale):
    outs, rs = [], []
    for hh in range(DN_HEADS):
        blk = s[:, hh * HEAD:(hh + 1) * HEAD]
        r = lax.rsqrt(jnp.sum(blk * blk, axis=-1, keepdims=True) + EPS)
        outs.append(blk * (r * scale))
        rs.append(r)
    return outs, rs


_QKV_SCALE = (HEAD ** -0.5, 1.0, None)


def _qkv_fwd(p, cw, name):
    t = p.shape[0]
    tm = min(256, t)

    def body(q_ref, k_ref, v_ref, qp_ref, kp_ref, vp_ref, w_ref, o_ref):
        first = pl.program_id(0) == 0
        for part, (ref, pref) in enumerate(((q_ref, qp_ref), (k_ref, kp_ref), (v_ref, vp_ref))):
            prev = jnp.where(first, 0.0, pref[...].astype(F32))
            ext = jnp.concatenate([prev, ref[...].astype(F32)], axis=0)
            s = _silu(_taps_back(ext, w_ref[:, part * D:(part + 1) * D], 4))
            if _QKV_SCALE[part] is None:
                o_ref[:, part * D:(part + 1) * D] = s.astype(ACT)
            else:
                outs, _ = _l2_heads(s, _QKV_SCALE[part])
                for hh in range(DN_HEADS):
                    o_ref[:, part * D + hh * HEAD:part * D + (hh + 1) * HEAD] = outs[hh].astype(ACT)

    c0 = ZQ // D
    tile = [pl.BlockSpec((tm, D), lambda i, c=c: (i, c)) for c in (c0, c0 + 1, c0 + 2)]
    halo = [_prev_halo(tm, c) for c in (c0, c0 + 1, c0 + 2)]
    return _pcall(body, name=name, grid=(t // tm,), in_specs=tile + halo + [_full((HALO, 3 * D))],
                  out_specs=pl.BlockSpec((tm, 3 * D), lambda i: (i, 0)),
                  out_shape=jax.ShapeDtypeStruct((t, 3 * D), ACT), compiler_params=_cp(1))(p, p, p, p, p, p, cw)


def _qkv_bwd(p, dqkv, cw, name):
    t = p.shape[0]
    tm = min(256, t)
    n_tiles = t // tm

    def body(*refs):
        tiles, prevs, nexts = refs[0:3], refs[3:6], refs[6:9]
        d_tiles, d_nexts = refs[9:12], refs[12:15]
        w_ref, o_ref, dw_ref = refs[15:]
        i = pl.program_id(0)

        @pl.when(i == 0)
        def _():
            dw_ref[...] = jnp.zeros_like(dw_ref)

        for part in range(3):
            w = w_ref[:, part * D:(part + 1) * D]
            prev = jnp.where(i == 0, 0.0, prevs[part][...].astype(F32))
            ext = jnp.concatenate([prev, tiles[part][...].astype(F32), nexts[part][...].astype(F32)], axis=0)
            xc = _taps_back(ext, w, 4)
            dout = jnp.concatenate([d_tiles[part][...].astype(F32), d_nexts[part][...].astype(F32)], axis=0)
            s = _silu(xc)
            if _QKV_SCALE[part] is None:
                ds = dout
            else:
                scale = _QKV_SCALE[part]
                pieces = []
                for hh in range(DN_HEADS):
                    blk = s[:, hh * HEAD:(hh + 1) * HEAD]
                    dblk = dout[:, hh * HEAD:(hh + 1) * HEAD]
                    r = lax.rsqrt(jnp.sum(blk * blk, axis=-1, keepdims=True) + EPS)
                    pieces.append(scale * r * (dblk - blk * (r * r) * jnp.sum(dblk * blk, axis=-1, keepdims=True)))
                ds = jnp.concatenate(pieces, axis=1)
            dxc = ds * _dsilu(xc)
            row = lax.broadcasted_iota(jnp.int32, (tm + HALO, 1), 0)
            dxc = jnp.where(jnp.logical_and(i == n_tiles - 1, row >= tm), 0.0, dxc)
            o_ref[:, part * D:(part + 1) * D] = _taps_fwd(dxc, w, 4, tm).astype(ACT)
            for k in range(4):
                sh = 3 - k
                shifted = (pltpu.roll(ext, sh, 0) if sh else ext)[HALO:HALO + tm, :]
                dw_ref[k:k + 1, part * D:(part + 1) * D] += jnp.sum(dxc[0:tm, :] * shifted, axis=0, keepdims=True)

    c0 = ZQ // D
    cols = (c0, c0 + 1, c0 + 2)
    tile = [pl.BlockSpec((tm, D), lambda i, c=c: (i, c)) for c in cols]
    dtile = [pl.BlockSpec((tm, D), lambda i, c=c: (i, c)) for c in range(3)]
    in_specs = (tile + [_prev_halo(tm, c) for c in cols] + [_next_halo(tm, c, n_tiles) for c in cols]
                + dtile + [_next_halo(tm, c, n_tiles) for c in range(3)] + [_full((HALO, 3 * D))])
    return _pcall(
        body, name=name, grid=(n_tiles,), in_specs=in_specs,
        out_specs=[pl.BlockSpec((tm, 3 * D), lambda i: (i, 0)), _full((HALO, 3 * D))],
        out_shape=[jax.ShapeDtypeStruct((t, 3 * D), ACT), jax.ShapeDtypeStruct((HALO, 3 * D), F32)],
        compiler_params=_cp(1),
    )(*([p] * 9), *([dqkv] * 6), cw)


def _tri_masks(n):
    row = lax.broadcasted_iota(jnp.int32, (n, n), 0)
    col = lax.broadcasted_iota(jnp.int32, (n, n), 1)
    return row, col


@jax.custom_vjp
def _unit_lower_inverses(mats):
    n = DN_CHUNK
    row, col = _tri_masks(n)
    eye = (row == col).astype(F32)
    same16 = (row // 16) == (col // 16)
    same32 = (row // 32) == (col // 32)
    pw = [jnp.where(same16, a, 0.0) for a in mats]
    x = [eye - p for p in pw]
    for _ in range(3):
        pw = [_hnn(p, p) for p in pw]
        x = [_hnn(xi, eye + p) for xi, p in zip(x, pw)]
    for keep in (jnp.logical_and(same32, jnp.logical_not(same16)), jnp.logical_not(same32)):
        inner = [_hnn(jnp.where(keep, a, 0.0), xi) for a, xi in zip(mats, x)]
        x = [xi - _hnn(xi, y) for xi, y in zip(x, inner)]
    return tuple(x)


def _uli_fwd(mats):
    t = _unit_lower_inverses(mats)
    return t, t


def _uli_bwd(ts, gs):
    inner = [_hnt(g, t) for g, t in zip(gs, ts)]
    return (tuple(-_htn(t, y) for t, y in zip(ts, inner)),)


_unit_lower_inverses.defvjp(_uli_fwd, _uli_bwd)


def _pick_col(m, k):
    return jnp.sum(m * _onehot_row(m.shape[1], k), axis=1, keepdims=True)


def _pick_row(m, k):
    hot = (lax.broadcasted_iota(jnp.int32, (m.shape[0], 1), 0) == k).astype(F32)
    return jnp.sum(m * hot, axis=0, keepdims=True)


def _delta_chunk(states, qs, ks, vs, ab, alog, dtb, onw):
    n = DN_CHUNK
    heads = range(DN_HEADS)
    row, col = _tri_masks(n)
    incl = row >= col
    lane = lax.broadcasted_iota(jnp.int32, (1, HEAD), 1)
    g_all = jnp.where(lane < DN_HEADS, -jnp.exp(alog) * jax.nn.softplus(ab + dtb), 0.0)
    c_cols = _hnn(incl.astype(F32), g_all)
    c_rows = _htn(g_all, (row <= col).astype(F32))
    g_tot = jnp.sum(g_all, axis=0, keepdims=True)
    beta_all = jax.nn.sigmoid(ab)
    ccol = [_pick_col(c_cols, h) for h in heads]
    crow = [_pick_row(c_rows, h) for h in heads]
    gl = [_pick_col(g_tot, h) for h in heads]
    beta = [_pick_col(beta_all, DN_HEADS + h) for h in heads]
    decay = [jnp.exp(jnp.where(incl, ccol[h] - crow[h], -1e30)) for h in heads]
    eg = [jnp.exp(ccol[h]) for h in heads]
    kb = [ks[h] * beta[h] for h in heads]
    amat = [jnp.where(row > col, _nt(kb[h], ks[h]) * decay[h], 0.0) for h in heads]
    tmat = _unit_lower_inverses(tuple(amat))
    u = [_nn(tmat[h], vs[h] * beta[h]) for h in heads]
    w = [_nn(tmat[h], kb[h] * eg[h]) for h in heads]
    qk = [_nt(qs[h], ks[h]) * decay[h] for h in heads]
    v_new = [u[h] - _nn(w[h], states[h]) for h in heads]
    o = [_nn(qs[h] * eg[h], states[h]) + _nn(qk[h], v_new[h]) for h in heads]
    new_states = [states[h] * jnp.exp(gl[h]) + _tn(ks[h] * jnp.exp(gl[h] - ccol[h]), v_new[h]) for h in heads]
    ys = [o[h] * lax.rsqrt(jnp.mean(o[h] * o[h], axis=-1, keepdims=True) + EPS) * onw for h in heads]
    return ys, new_states


def _head_cols(ref):
    return [ref[:, h * HEAD:(h + 1) * HEAD].astype(F32) for h in range(DN_HEADS)]


def _delta_fwd(qkv, ab, alog, dtb, onw, name):
    t = qkv.shape[0]
    nc = t // DN_CHUNK

    def body(q_ref, k_ref, v_ref, ab_ref, alog_ref, dtb_ref, onw_ref, y_ref, keep_ref, state):
        @pl.when(pl.program_id(0) == 0)
        def _():
            state[...] = jnp.zeros_like(state)

        s0 = [state[hh] for hh in range(DN_HEADS)]
        ys, s1 = _delta_chunk(s0, _head_cols(q_ref), _head_cols(k_ref), _head_cols(v_ref), ab_ref[...],
                              alog_ref[...], dtb_ref[...], onw_ref[...])
        for hh in range(DN_HEADS):
            keep_ref[0, hh] = s0[hh]
            state[hh] = s1[hh]
            y_ref[:, hh * HEAD:(hh + 1) * HEAD] = ys[hh].astype(ACT)

    chunk = [pl.BlockSpec((DN_CHUNK, D), lambda i, c=c: (i, c)) for c in range(3)]
    return _pcall(
        body, name=name, grid=(nc,),
        in_specs=chunk + [pl.BlockSpec((DN_CHUNK, HEAD), lambda i: (i, 0)), _full((1, HEAD)), _full((1, HEAD)),
                          _full((1, HEAD))],
        out_specs=[pl.BlockSpec((DN_CHUNK, D), lambda i: (i, 0)),
                   pl.BlockSpec((1, DN_HEADS, HEAD, HEAD), lambda i: (i, 0, 0, 0))],
        out_shape=[jax.ShapeDtypeStruct((t, D), ACT), jax.ShapeDtypeStruct((nc, DN_HEADS, HEAD, HEAD), F32)],
        scratch_shapes=[pltpu.VMEM((DN_HEADS, HEAD, HEAD), F32)], compiler_params=_cp(1),
    )(qkv, qkv, qkv, ab, alog, dtb, onw)


def _delta_bwd(qkv, ab, alog, dtb, onw, keep, dy, name):
    t = qkv.shape[0]
    nc = t // DN_CHUNK

    def body(q_ref, k_ref, v_ref, ab_ref, alog_ref, dtb_ref, onw_ref, keep_ref, dy_ref,
             dqkv_ref, dab_ref, dalog_ref, ddtb_ref, donw_ref, dstate):
        @pl.when(pl.program_id(0) == 0)
        def _():
            dstate[...] = jnp.zeros_like(dstate)
            dalog_ref[...] = jnp.zeros_like(dalog_ref)
            ddtb_ref[...] = jnp.zeros_like(ddtb_ref)
            donw_ref[...] = jnp.zeros_like(donw_ref)

        s0 = [keep_ref[0, hh] for hh in range(DN_HEADS)]
        _, vjp = jax.vjp(_delta_chunk, s0, _head_cols(q_ref), _head_cols(k_ref), _head_cols(v_ref), ab_ref[...],
                         alog_ref[...], dtb_ref[...], onw_ref[...])
        ds, dq, dk, dv, dab, dal, ddt, don = vjp((_head_cols(dy_ref), [dstate[hh] for hh in range(DN_HEADS)]))
        for hh in range(DN_HEADS):
            lo = hh * HEAD
            dstate[hh] = ds[hh]
            dqkv_ref[:, lo:lo + HEAD] = dq[hh].astype(ACT)
            dqkv_ref[:, D + lo:D + lo + HEAD] = dk[hh].astype(ACT)
            dqkv_ref[:, 2 * D + lo:2 * D + lo + HEAD] = dv[hh].astype(ACT)
        dab_ref[...] = dab
        dalog_ref[...] += dal
        ddtb_ref[...] += ddt
        donw_ref[...] += don

    rev = lambda i: nc - 1 - i
    chunk = [pl.BlockSpec((DN_CHUNK, D), lambda i, c=c: (rev(i), c)) for c in range(3)]
    small = jax.ShapeDtypeStruct((1, HEAD), F32)
    return _pcall(
        body, name=name, grid=(nc,),
        in_specs=chunk + [pl.BlockSpec((DN_CHUNK, HEAD), lambda i: (rev(i), 0)), _full((1, HEAD)), _full((1, HEAD)),
                          _full((1, HEAD)), pl.BlockSpec((1, DN_HEADS, HEAD, HEAD), lambda i: (rev(i), 0, 0, 0)),
                          pl.BlockSpec((DN_CHUNK, D), lambda i: (rev(i), 0))],
        out_specs=[pl.BlockSpec((DN_CHUNK, 3 * D), lambda i: (rev(i), 0)),
                   pl.BlockSpec((DN_CHUNK, HEAD), lambda i: (rev(i), 0)), _full((1, HEAD)), _full((1, HEAD)),
                   _full((1, HEAD))],
        out_shape=[jax.ShapeDtypeStruct((t, 3 * D), ACT), jax.ShapeDtypeStruct((t, HEAD), F32), small, small, small],
        scratch_shapes=[pltpu.VMEM((DN_HEADS, HEAD, HEAD), F32)], compiler_params=_cp(1),
    )(qkv, qkv, qkv, ab, alog, dtb, onw, keep, dy)


def _loss_head(y, target):
    t = y.shape[0]
    tm = min(512, t)

    def body(y_ref, t_ref, l_ref, dy_ref):
        @pl.when(pl.program_id(0) == 0)
        def _():
            l_ref[...] = jnp.zeros_like(l_ref)

        diff = y_ref[...] - t_ref[...]
        dy_ref[...] = diff * (1.0 / D)
        l_ref[...] += 0.5 * jnp.sum(jnp.sum(diff * diff, axis=-1, keepdims=True) * (1.0 / D), axis=0, keepdims=True)

    row = pl.BlockSpec((tm, D), lambda i: (i, 0))
    return _pcall(body, name="loss_head", grid=(t // tm,), in_specs=[row, row], out_specs=[_full((8, 128)), row],
                  out_shape=[jax.ShapeDtypeStruct((8, 128), F32), jax.ShapeDtypeStruct((t, D), F32)],
                  compiler_params=_cp(1))(y, target)


_SMALL_SHARDED = (("a_ln_w", (2, 128), 1), ("a_ln_b", (2, 128), 1), ("b_conv_w", (1, 3, 128), 2),
                  ("c_conv_w", (1, 4, 384), 2))
_REPLICATED = (("mem_norm_w", (1024,)), ("norm_pre", (4, 1024)), ("norm_post", (4, 1024)),
               ("a_w_s", (2, 8, 128, 128)), ("a_b_s", (2, 8, 128)), ("c_a_log", (1, 8)), ("c_dt_bias", (1, 8)),
               ("c_o_norm_w", (1, 128)))


def _big_shards(given, prefix, dtype):
    rows = jnp.concatenate([given[prefix + "w_mem_kv"], given[prefix + "w_out"].reshape(4 * (D_CAT // N_DEV), D)])
    return [rows.astype(dtype), given[prefix + "a_w_in"].reshape(2 * D, 512).astype(dtype),
            given[prefix + "b_w_in"][0].astype(dtype), given[prefix + "c_w_in"][0].astype(dtype)]


def _pack_rows(flat_parts, rows, lead=()):
    n_lead = len(lead)
    flat = jnp.concatenate([a.reshape(lead + (-1,)) for a in flat_parts], axis=n_lead)
    pad = rows * D - flat.shape[n_lead]
    flat = jnp.pad(flat, [(0, 0)] * n_lead + [(0, pad)])
    return flat.reshape(lead + (rows, D))


def _unpack_rows(packed, table, lead=()):
    flat = packed.reshape(lead + (-1,))
    out, at = {}, 0
    for name, shape in table:
        size = math.prod(shape)
        out[name] = lax.slice_in_dim(flat, at, at + size, axis=len(lead)).reshape(lead + shape)
        at += size
    return out


def _join_shards(blocks, axis):
    moved = jnp.moveaxis(blocks, 0, axis)
    shape = moved.shape
    return moved.reshape(shape[:axis] + (shape[axis] * shape[axis + 1],) + shape[axis + 2:])


def _split_shards(full, axis):
    shape = full.shape
    split = full.reshape(shape[:axis] + (N_DEV, shape[axis] // N_DEV) + shape[axis + 1:])
    return jnp.moveaxis(split, axis, 0)


_A_COLS = ((2560, 1536), (2048, 512), (0, 2048))
_BC_COLS = ((3584, 1536), (3072, 512), (0, 3072))
_C_COLS = ((3600, 1536), (3088, 512), (0, 3072))


def _reorder_cols(w, cols):
    return jnp.concatenate([w[:, s:s + n] for s, n in cols], axis=1)


def _restore_cols(pieces_in_my_order, cols, extra=()):
    placed = sorted(list(zip([s for s, _ in cols], pieces_in_my_order)) + list(extra), key=lambda sp: sp[0])
    return jnp.concatenate([piece for _, piece in placed], axis=1)


def kernel(x, mem, mem_norm_w, w_mem_kv, norm_pre, norm_post, w_out, a_w_in, a_ln_w, a_ln_b, a_w_s, a_b_s, b_w_in, b_conv_w, c_w_in, c_conv_w, c_a_log, c_dt_bias, c_o_norm_w, loss_target, m_mem_norm_w, m_w_mem_kv, m_norm_pre, m_norm_post, m_w_out, m_a_w_in, m_a_ln_w, m_a_ln_b, m_a_w_s, m_a_b_s, m_b_w_in, m_b_conv_w, m_c_w_in, m_c_conv_w, m_c_a_log, m_c_dt_bias, m_c_o_norm_w, v_mem_norm_w, v_w_mem_kv, v_norm_pre, v_norm_post, v_w_out, v_a_w_in, v_a_ln_w, v_a_ln_b, v_a_w_s, v_a_b_s, v_b_w_in, v_b_conv_w, v_c_w_in, v_c_conv_w, v_c_a_log, v_c_dt_bias, v_c_o_norm_w):
    given = dict(locals())
    x0 = x[0]
    mem0 = mem[0]
    target = loss_target[0]

    big = _big_shards(given, "", MXU)
    small_tab = tuple((n, s) for n, s, _ in _SMALL_SHARDED)
    small = _pack_rows([given[n] for n, _, _ in _SMALL_SHARDED], HALO)
    g_rows, g_a, g_b, g_c, g_small = _all_gather(big + [small], "gather_weights")
    small_full = _unpack_rows(g_small, small_tab, lead=(N_DEV,))
    full = {n: _join_shards(small_full[n], ax) for n, _, ax in _SMALL_SHARDED}

    wkv = g_rows[:, :128].reshape(D, 2 * D_XA)
    wout_all = jnp.moveaxis(g_rows[:, 128:].reshape(N_DEV, 4, D_CAT // N_DEV, D), 0, 1).reshape(4, D_CAT, D)
    wouts = [wout_all[i] for i in range(4)]
    a_full = _join_shards(g_a.reshape(N_DEV, 2, D, 512), 2)
    w_a = [_reorder_cols(a_full[j], _A_COLS) for j in range(2)]
    w_b = _reorder_cols(_join_shards(g_b, 1), _BC_COLS)
    c_full = jnp.concatenate([g_c[d] for d in range(N_DEV)], axis=1)
    w_c = _reorder_cols(c_full, _C_COLS)
    w_cab = jnp.pad(c_full[:, 3072:3088], ((0, 0), (0, HEAD - 16)))
    lnw = [full["a_ln_w"][j][None, :] for j in range(2)]
    lnb = [full["a_ln_b"][j][None, :] for j in range(2)]
    ws = [a_w_s[j] for j in range(2)]
    bst = [jnp.pad(a_b_s[j].T, ((0, 0), (0, HEAD - GM_GROUPS))) for j in range(2)]
    cw_b = jnp.pad(full["b_conv_w"][0], ((0, HALO - 3), (0, 0)))
    cw_c = jnp.pad(full["c_conv_w"][0], ((0, HALO - 4), (0, 0)))
    alog = jnp.pad(c_a_log, ((0, 0), (0, HEAD - DN_HEADS)))
    dtb = jnp.pad(c_dt_bias, ((0, 0), (0, HEAD - DN_HEADS)))
    onw = c_o_norm_w
    mw = mem_norm_w[None, :]
    w_in = [w_a[0], w_b, w_c, w_a[1]]

    kv = _memkv_fwd(mem0, mw, wkv)
    xs, saved = [x0], []
    for i in range(4):
        kind = i % 3
        npre, npost = norm_pre[i][None, :], norm_post[i][None, :]
        if kind == 2:
            p, h, ab = _proj_fwd(xs[i], npre, w_in[i], w_cab, f"proj_fwd_{i}")
            qkv = _qkv_fwd(p, cw_c, f"qkv_fwd_{i}")
            ymix, keep = _delta_fwd(qkv, ab, alog, dtb, onw, f"delta_fwd_{i}")
            extra = (qkv, ab, keep)
        else:
            p, h = _proj_fwd(xs[i], npre, w_in[i], None, f"proj_fwd_{i}")
            if kind == 0:
                ymix = _gmlp_fwd(p, lnw[i // 3], lnb[i // 3], ws[i // 3], bst[i // 3], f"gmlp_fwd_{i}")
            else:
                ymix = _sconv_fwd(p, cw_b, f"sconv_fwd_{i}")
            extra = ()
        xn, o = _tail_fwd(p, ymix, xs[i], kv, wouts[i], npost, f"tail_fwd_{i}")
        xs.append(xn)
        saved.append((p, h, ymix, o, extra))

    loss_tile, dx = _loss_head(xs[4], target)
    loss = lax.psum(loss_tile[0, 0], ("x", "y", "c"))

    g = {}
    d_npre, d_npost, d_wout = [None] * 4, [None] * 4, [None] * 4
    d_ws, d_bs, d_lnw, d_lnb, d_wa = [None] * 2, [None] * 2, [None] * 2, [None] * 2, [None] * 2
    dkv = None
    for i in reversed(range(4)):
        kind = i % 3
        p, h, ymix, o, extra = saved[i]
        npre, npost = norm_pre[i][None, :], norm_post[i][None, :]
        dzq, dymix, d_wout[i], d_npost[i], dkv_i = _tail_bwd(dx, o, p, ymix, kv, wouts[i], npost, f"tail_bwd_{i}")
        dkv = dkv_i if dkv is None else dkv + dkv_i
        w_zq, w_mix = w_in[i][:, :ZQ], w_in[i][:, ZQ:]
        dw_zq = _matmul_tn(h, dzq, f"dw_zq_{i}")
        if kind == 0:
            j = i // 3
            dmix, d_lnw[j], d_lnb[j], d_ws[j], dbst = _gmlp_bwd(p, dymix, lnw[j], lnb[j], ws[j], bst[j], f"gmlp_bwd_{i}")
            d_bs[j] = dbst[:, :GM_GROUPS].T
            dw_mix = _matmul_tn(h, dmix, f"dw_mix_{i}")
            d_wa[j] = _restore_cols([dw_zq[:, :D_CAT], dw_zq[:, D_CAT:], dw_mix], _A_COLS)
            dps, wparts = [dzq, dmix], [w_zq, w_mix]
        elif kind == 1:
            dmix, dcw = _sconv_bwd(p, dymix, cw_b, f"sconv_bwd_{i}")
            g["b_conv_w"] = dcw[None, :3]
            dw_mix = _matmul_tn(h, dmix, f"dw_mix_{i}")
            g["b_w_in"] = _restore_cols([dw_zq[:, :D_CAT], dw_zq[:, D_CAT:], dw_mix], _BC_COLS)
            dps, wparts = [dzq, dmix], [w_zq, w_mix]
        else:
            qkv, ab, keep = extra
            dqkv, dab, dalog, ddtb, donw = _delta_bwd(qkv, ab, alog, dtb, onw, keep, dymix, f"delta_bwd_{i}")
            dmix, dcw = _qkv_bwd(p, dqkv, cw_c, f"qkv_bwd_{i}")
            g["c_conv_w"] = dcw[None, :4]
            g["c_a_log"], g["c_dt_bias"], g["c_o_norm_w"] = dalog[:, :DN_HEADS], ddtb[:, :DN_HEADS], donw
            dw_mix = _matmul_tn(h, dmix, f"dw_mix_{i}")
            dw_ab = _matmul_tn(h, dab, f"dw_ab_{i}")
            g["c_w_in"] = _restore_cols([dw_zq[:, :D_CAT], dw_zq[:, D_CAT:], dw_mix], _C_COLS,
                                        extra=[(3072, dw_ab[:, :16])])
            dps, wparts = [dzq, dmix, dab], [w_zq, w_mix, w_cab]
        dx, d_npre[i] = _proj_bwd_x(dps, wparts, xs[i], dx, npre, f"proj_bwd_x_{i}")

    d_wkv, d_mw = _memkv_bwd(mem0, mw, wkv, dkv)
    g["w_mem_kv"], g["mem_norm_w"] = d_wkv, d_mw[0]
    g["norm_pre"] = jnp.concatenate(d_npre, axis=0)
    g["norm_post"] = jnp.concatenate(d_npost, axis=0)
    g["w_out"] = jnp.stack(d_wout)
    g["a_w_in"] = jnp.stack(d_wa)
    g["a_ln_w"] = jnp.concatenate(d_lnw, axis=0)
    g["a_ln_b"] = jnp.concatenate(d_lnb, axis=0)
    g["a_w_s"] = jnp.stack(d_ws)
    g["a_b_s"] = jnp.stack(d_bs)

    e_rows = jnp.concatenate([g["w_mem_kv"].reshape(N_DEV, 128, D),
                              _split_shards(g["w_out"], 1).reshape(N_DEV, 4 * (D_CAT // N_DEV), D)], axis=1)
    e_a = _split_shards(g["a_w_in"], 2).reshape(N_DEV, 2 * D, 512)
    e_b = _split_shards(g["b_w_in"], 1)
    e_c = jnp.stack([g["c_w_in"][:, d * 642:(d + 1) * 642] for d in range(N_DEV)])
    e_small = _pack_rows([_split_shards(g[n], ax) for n, _, ax in _SMALL_SHARDED], HALO, lead=(N_DEV,))
    r_pack = _pack_rows([g[n] for n, _ in _REPLICATED], R_REPL)
    landed, r_all = _exchange([e.astype(XCH) for e in (e_rows, e_a, e_b, e_c)] + [e_small], r_pack, "exchange_grads")

    w_big, m_big, v_big = (_big_shards(given, pre, F32) for pre in ("", "m_", "v_"))
    res = [_reduce_adamw(landed[a], w_big[a], m_big[a], v_big[a], f"adamw_{a}") for a in range(4)]
    res_small = _reduce_adamw(landed[4], small, _pack_rows([given["m_" + n] for n, _, _ in _SMALL_SHARDED], HALO),
                              _pack_rows([given["v_" + n] for n, _, _ in _SMALL_SHARDED], HALO), "adamw_small")
    rp = _reduce_adamw(r_all, _pack_rows([given[n] for n, _ in _REPLICATED], R_REPL),
                       _pack_rows([given["m_" + n] for n, _ in _REPLICATED], R_REPL),
                       _pack_rows([given["v_" + n] for n, _ in _REPLICATED], R_REPL), "adamw_replicated")

    order = ["mem_norm_w", "w_mem_kv", "norm_pre", "norm_post", "w_out", "a_w_in", "a_ln_w", "a_ln_b", "a_w_s", "a_b_s",
             "b_w_in", "b_conv_w", "c_w_in", "c_conv_w", "c_a_log", "c_dt_bias", "c_o_norm_w"]
    outs = [loss, dx[None]]
    for kind in range(4):
        got = dict(_unpack_rows(rp[kind], _REPLICATED))
        got.update(_unpack_rows(res_small[kind], small_tab))
        got["w_mem_kv"] = res[0][kind][:128]
        got["w_out"] = res[0][kind][128:].reshape(4, D_CAT // N_DEV, D)
        got["a_w_in"] = res[1][kind].reshape(2, D, 512)
        got["b_w_in"] = res[2][kind][None]
        got["c_w_in"] = res[3][kind][None]
        outs += [got[n] for n in order]
    return tuple(outs)
```

```python
import functools
import math

import jax
import jax.numpy as jnp
from jax import lax
from jax.experimental import pallas as pl
from jax.experimental.pallas import tpu as pltpu

F32 = jnp.float32
MXU = jnp.bfloat16
ACT = jnp.bfloat16

D = 1024
D_XA = 512
D_CAT = 1536
N_MEM = 256
XA_HEADS = 4
HEAD = 128
ZQ = D_CAT + D_XA
EPS = 1e-6
GM_CHUNK = 128
GM_GROUPS = 8
DN_HEADS = 8
DN_CHUNK = 64
N_DEV = 8
HALO = 8
VMEM_LIMIT = 56 * 1024 * 1024
XCH = jnp.bfloat16
R_REPL = 272

ADAM_LR = 0.001
ADAM_B1 = 0.9
ADAM_B2 = 0.999
ADAM_EPS = 1e-08
ADAM_WD = 0.01
ADAM_STEP = 10

NN = ((1,), (0,))
NT = ((1,), (1,))
TN = ((0,), (0,))
MESH = pl.DeviceIdType.MESH


def _pcall(body, **kw):
    return pl.pallas_call(body, **kw)


def _cp(n_axes):
    return pltpu.CompilerParams(dimension_semantics=("arbitrary",) * n_axes, vmem_limit_bytes=VMEM_LIMIT)


def _dot(a, b, dims, prec=None):
    return lax.dot_general(a, b, (dims, ((), ())), preferred_element_type=F32, precision=prec)


def _mdot(a, b, dims):
    return _dot(a.astype(MXU), b.astype(MXU), dims)


def _make_mms(raw):
    @jax.custom_vjp
    def nn(a, b):
        return raw(a, b, NN)

    @jax.custom_vjp
    def nt(a, b):
        return raw(a, b, NT)

    @jax.custom_vjp
    def tn(a, b):
        return raw(a, b, TN)

    nn.defvjp(lambda a, b: (nn(a, b), (a, b)), lambda r, g: (nt(g, r[1]), tn(r[0], g)))
    nt.defvjp(lambda a, b: (nt(a, b), (a, b)), lambda r, g: (nn(g, r[1]), tn(g, r[0])))
    tn.defvjp(lambda a, b: (tn(a, b), (a, b)), lambda r, g: (nt(r[1], g), nn(r[0], g)))
    return nn, nt, tn


_nn, _nt, _tn = _make_mms(_mdot)


def _split_dot(a, b, dims):
    ah = a.astype(jnp.bfloat16)
    bh = b.astype(jnp.bfloat16)
    al = (a - ah.astype(F32)).astype(jnp.bfloat16)
    bl = (b - bh.astype(F32)).astype(jnp.bfloat16)
    return _dot(ah, bh, dims) + (_dot(ah, bl, dims) + _dot(al, bh, dims))


_hnn, _hnt, _htn = _make_mms(_split_dot)


def _full(shape):
    return pl.BlockSpec(shape, lambda *_: (0,) * len(shape))


def _silu(z):
    return z * jax.nn.sigmoid(z)


def _dsilu(z):
    s = jax.nn.sigmoid(z)
    return s * (1.0 + z * (1.0 - s))


def _onehot_row(n, k):
    return (lax.broadcasted_iota(jnp.int32, (1, n), 1) == k).astype(F32)


_HBM = pl.BlockSpec(memory_space=pl.ANY)


def _sem_shapes(n_remote, n_local):
    return [pltpu.SemaphoreType.DMA((n_remote,)), pltpu.SemaphoreType.DMA((n_remote,)),
            pltpu.SemaphoreType.DMA((n_local,))]


def _gather_parts(x_refs, out_refs, send_sems, recv_sems, local_sems):
    n = len(x_refs)
    x, y, cc = lax.axis_index("x"), lax.axis_index("y"), lax.axis_index("c")
    me, sibling = (x, y, cc), (x, y, 1 - cc)
    chips = [(1 - x, y), (x, 1 - y), (1 - x, 1 - y)]

    def slot(a, px, py, pc):
        return out_refs[a].at[4 * px + 2 * py + pc]

    def copy(k, a, block, to, src=None):
        return pltpu.make_async_remote_copy(
            src_ref=slot(a, *block) if src is None else src, dst_ref=slot(a, *block),
            send_sem=send_sems.at[k * n + a], recv_sem=recv_sems.at[k * n + a], device_id=to, device_id_type=MESH)

    mine = [pltpu.make_async_copy(x_refs[a], slot(a, *me), local_sems.at[a]) for a in range(n)]
    first = [copy(0, a, me, sibling, src=x_refs[a]) for a in range(n)]
    first += [copy(1 + j, a, me, (*chip, cc), src=x_refs[a]) for j, chip in enumerate(chips) for a in range(n)]

    def begin():
        for cp in mine + first:
            cp.start()

    def end():
        passed = []
        for j, chip in enumerate(chips):
            for a in range(n):
                copy(1 + j, a, (*chip, cc), me).wait_recv()
                passed.append(copy(4 + j, a, (*chip, cc), sibling))
                passed[-1].start()
        for a in range(n):
            copy(0, a, sibling, me).wait_recv()
        for j, chip in enumerate(chips):
            for a in range(n):
                copy(4 + j, a, (*chip, 1 - cc), me).wait_recv()
        for cp in first + passed:
            cp.wait_send()
        for cp in mine:
            cp.wait()

    return begin, end


def _exchange_parts(g_refs, out_refs, b_ref, ball_ref, send_sems, recv_sems, local_sems):
    n = len(g_refs)
    per_peer = n + (b_ref is not None)
    x, y, cc = lax.axis_index("x"), lax.axis_index("y"), lax.axis_index("c")
    my_idx = 4 * x + 2 * y + cc
    mine = [pltpu.make_async_copy(g_refs[a].at[my_idx], out_refs[a].at[my_idx], local_sems.at[a]) for a in range(n)]
    if b_ref is not None:
        mine.append(pltpu.make_async_copy(b_ref, ball_ref.at[my_idx], local_sems.at[n]))
    copies = []
    for k in range(1, N_DEV):
        px = 1 - x if k & 4 else x
        py = 1 - y if k & 2 else y
        pc = 1 - cc if k & 1 else cc
        base = (k - 1) * per_peer
        for a in range(n):
            copies.append(pltpu.make_async_remote_copy(
                src_ref=g_refs[a].at[4 * px + 2 * py + pc], dst_ref=out_refs[a].at[my_idx],
                send_sem=send_sems.at[base + a], recv_sem=recv_sems.at[base + a],
                device_id=(px, py, pc), device_id_type=MESH))
        if b_ref is not None:
            copies.append(pltpu.make_async_remote_copy(
                src_ref=b_ref, dst_ref=ball_ref.at[my_idx], send_sem=send_sems.at[base + n],
                recv_sem=recv_sems.at[base + n], device_id=(px, py, pc), device_id_type=MESH))

    def begin():
        for cp in mine + copies:
            cp.start()

    def end():
        for cp in copies:
            cp.wait_recv()
        for cp in copies:
            cp.wait_send()
        for cp in mine:
            cp.wait()

    return begin, end


def _all_gather(blks, name):
    n = len(blks)

    def body(*refs):
        begin, end = _gather_parts(refs[:n], refs[n:2 * n], *refs[2 * n:])
        begin()
        end()

    return _pcall(body, name=name, out_shape=[jax.ShapeDtypeStruct((N_DEV,) + b.shape, b.dtype) for b in blks],
                  in_specs=[_HBM] * n, out_specs=[_HBM] * n, scratch_shapes=_sem_shapes(7 * n, n))(*blks)


def _exchange(gs, bcast, name):
    n = len(gs)

    def body(*refs):
        begin, end = _exchange_parts(refs[:n], refs[n + 1:2 * n + 1], refs[n], refs[2 * n + 1], *refs[2 * n + 2:])
        begin()
        end()

    out_shape = [jax.ShapeDtypeStruct(g.shape, g.dtype) for g in gs]
    out_shape.append(jax.ShapeDtypeStruct((N_DEV,) + bcast.shape, bcast.dtype))
    outs = _pcall(body, name=name, out_shape=out_shape, in_specs=[_HBM] * (n + 1), out_specs=[_HBM] * (n + 1),
                  scratch_shapes=_sem_shapes(7 * (n + 1), n + 1))(*gs, bcast)
    return outs[:n], outs[n]


def _reduce_adamw(parts, w, m, v, name):
    r, c = w.shape
    tr = 128 if r % 128 == 0 else r

    def body(p_ref, w_ref, m_ref, v_ref, g_ref, d_ref, nm_ref, nv_ref):
        g = p_ref[0].astype(F32)
        for s in range(1, N_DEV):
            g = g + p_ref[s].astype(F32)
        nm = ADAM_B1 * m_ref[...] + (1.0 - ADAM_B1) * g
        nv = ADAM_B2 * v_ref[...] + (1.0 - ADAM_B2) * (g * g)
        m_hat = nm / (1.0 - ADAM_B1 ** ADAM_STEP)
        v_hat = nv / (1.0 - ADAM_B2 ** ADAM_STEP)
        g_ref[...] = g
        d_ref[...] = -ADAM_LR * (m_hat / (jnp.sqrt(v_hat) + ADAM_EPS) + ADAM_WD * w_ref[...])
        nm_ref[...] = nm
        nv_ref[...] = nv

    row = pl.BlockSpec((tr, c), lambda i: (i, 0))
    out = jax.ShapeDtypeStruct((r, c), F32)
    return _pcall(
        body, name=name, grid=(r // tr,),
        in_specs=[pl.BlockSpec((N_DEV, tr, c), lambda i: (0, i, 0)), row, row, row],
        out_specs=[row, row, row, row], out_shape=[out, out, out, out], compiler_params=_cp(1),
    )(parts, w, m, v)


def _proj_fwd(x, nw, w, wab, name, gather=()):
    t, npj = x.shape[0], w.shape[1]
    tm, tn = min(512, t), 1024
    has_ab = wab is not None
    n_in, n_out, n_g = 3 + has_ab, 2 + has_ab, len(gather)
    n_i, n_j = t // tm, npj // tn

    def body(*refs):
        ins, g_ins = refs[:n_in], refs[n_in:n_in + n_g]
        outs = refs[n_in + n_g:n_in + n_g + n_out]
        g_outs = refs[n_in + n_g + n_out:n_in + 2 * n_g + n_out]
        hs = refs[n_in + 2 * n_g + n_out]
        if has_ab:
            (x_ref, nw_ref, w_ref, wab_ref), (p_ref, h_ref, ab_ref) = ins, outs
        else:
            (x_ref, nw_ref, w_ref), (p_ref, h_ref) = ins, outs
        if n_g:
            begin, end = _gather_parts(g_ins, g_outs, *refs[n_in + 2 * n_g + n_out + 1:])
            pl.when(jnp.logical_and(pl.program_id(0) == 0, pl.program_id(1) == 0))(begin)

        @pl.when(pl.program_id(1) == 0)
        def _():
            xv = x_ref[...]
            hv = xv * lax.rsqrt(jnp.mean(xv * xv, axis=-1, keepdims=True) + EPS) * nw_ref[...]
            hs[...] = hv.astype(MXU)
            h_ref[...] = hv.astype(ACT)
            if has_ab:
                ab_ref[...] = _mdot(hv, wab_ref[...], NN)

        p_ref[...] = _dot(hs[...], w_ref[...], NN).astype(ACT)
        if n_g:
            pl.when(jnp.logical_and(pl.program_id(0) == n_i - 1, pl.program_id(1) == n_j - 1))(end)

    in_specs = [pl.BlockSpec((tm, D), lambda i, j: (i, 0)), _full((1, D)), pl.BlockSpec((D, tn), lambda i, j: (0, j))]
    out_specs = [pl.BlockSpec((tm, tn), lambda i, j: (i, j)), pl.BlockSpec((tm, D), lambda i, j: (i, 0))]
    out_shape = [jax.ShapeDtypeStruct((t, npj), ACT), jax.ShapeDtypeStruct((t, D), ACT)]
    args = [x, nw, w]
    if has_ab:
        in_specs.append(_full((D, HEAD)))
        out_specs.append(pl.BlockSpec((tm, HEAD), lambda i, j: (i, 0)))
        out_shape.append(jax.ShapeDtypeStruct((t, HEAD), F32))
        args.append(wab)
    scratch = [pltpu.VMEM((tm, D), MXU)]
    if n_g:
        in_specs += [_HBM] * n_g
        out_specs += [_HBM] * n_g
        out_shape += [jax.ShapeDtypeStruct((N_DEV,) + b.shape, b.dtype) for b in gather]
        args += list(gather)
        scratch += _sem_shapes(7 * n_g, n_g)
    return _pcall(body, name=name, grid=(n_i, n_j), in_specs=in_specs, out_specs=out_specs,
                  out_shape=out_shape, scratch_shapes=scratch, compiler_params=_cp(2))(*args)


def _proj_bwd_x(dps, ws, x, dxn, nw, name):
    t = x.shape[0]
    tm = min(256, t)
    n = len(dps)

    def body(*refs):
        dp_refs, w_refs = refs[:n], refs[n:2 * n]
        x_ref, dxn_ref, nw_ref, dx_ref, dnw_ref = refs[2 * n:]
        dh = _mdot(dp_refs[0][...], w_refs[0][...], NT)
        for k in range(1, n):
            dh = dh + _mdot(dp_refs[k][...], w_refs[k][...], NT)
        xv = x_ref[...]
        r = lax.rsqrt(jnp.mean(xv * xv, axis=-1, keepdims=True) + EPS)

        @pl.when(pl.program_id(0) == 0)
        def _():
            dnw_ref[...] = jnp.zeros_like(dnw_ref)

        dnw_ref[...] += jnp.sum(dh * xv * r, axis=0, keepdims=True)
        dhw = dh * nw_ref[...]
        dx_ref[...] = dxn_ref[...] + r * (dhw - xv * (r * r) * jnp.mean(dhw * xv, axis=-1, keepdims=True))

    row = pl.BlockSpec((tm, D), lambda i: (i, 0))
    in_specs = [pl.BlockSpec((tm, dp.shape[1]), lambda i: (i, 0)) for dp in dps]
    in_specs += [_full(w.shape) for w in ws]
    in_specs += [row, row, _full((1, D))]
    return _pcall(body, name=name, grid=(t // tm,), in_specs=in_specs, out_specs=[row, _full((1, D))],
                  out_shape=[jax.ShapeDtypeStruct((t, D), F32), jax.ShapeDtypeStruct((1, D), F32)],
                  compiler_params=_cp(1))(*dps, *ws, x, dxn, nw)


def _matmul_tn(a, b, name):
    t, m = a.shape
    n = b.shape[1]
    tm, tn = min(512, t), min(1024, n)

    def body(a_ref, b_ref, o_ref):
        @pl.when(pl.program_id(1) == 0)
        def _():
            o_ref[...] = jnp.zeros_like(o_ref)

        o_ref[...] += _mdot(a_ref[...], b_ref[...], TN)

    return _pcall(body, name=name, grid=(n // tn, t // tm),
                  in_specs=[pl.BlockSpec((tm, m), lambda j, i: (i, 0)), pl.BlockSpec((tm, tn), lambda j, i: (i, j))],
                  out_specs=pl.BlockSpec((m, tn), lambda j, i: (0, j)),
                  out_shape=jax.ShapeDtypeStruct((m, n), F32), compiler_params=_cp(2))(a, b)


def _memkv_fwd(mem, mw, wkv):
    def body(mem_ref, mw_ref, w_ref, kv_ref):
        mv = mem_ref[...]
        mn = mv * lax.rsqrt(jnp.mean(mv * mv, axis=-1, keepdims=True) + EPS) * mw_ref[...]
        kv_ref[...] = _mdot(mn, w_ref[...], NN)

    return _pcall(body, name="memkv_fwd", out_shape=jax.ShapeDtypeStruct((N_MEM, 2 * D_XA), F32),
                  compiler_params=pltpu.CompilerParams(vmem_limit_bytes=VMEM_LIMIT))(mem, mw, wkv)


def _memkv_bwd(mem, mw, wkv, dkv):
    def body(mem_ref, mw_ref, w_ref, dkv_ref, dw_ref, dmw_ref):
        mv = mem_ref[...]
        r = lax.rsqrt(jnp.mean(mv * mv, axis=-1, keepdims=True) + EPS)
        mn = mv * r * mw_ref[...]
        dkvv = dkv_ref[...]
        dw_ref[...] = _mdot(mn, dkvv, TN)
        dmn = _mdot(dkvv, w_ref[...], NT)
        dmw_ref[...] = jnp.sum(dmn * mv * r, axis=0, keepdims=True)

    return _pcall(body, name="memkv_bwd",
                  out_shape=[jax.ShapeDtypeStruct((D, 2 * D_XA), F32), jax.ShapeDtypeStruct((1, D), F32)],
                  compiler_params=pltpu.CompilerParams(vmem_limit_bytes=VMEM_LIMIT))(mem, mw, wkv, dkv)


def _attend(q, kv):
    ps, outs = [], []
    for hh in range(XA_HEADS):
        qh = q[:, hh * HEAD:(hh + 1) * HEAD]
        kh = kv[:, hh * HEAD:(hh + 1) * HEAD]
        vh = kv[:, D_XA + hh * HEAD:D_XA + (hh + 1) * HEAD]
        s = _mdot(qh, kh, NT) * (HEAD ** -0.5)
        s = s - jnp.max(s, axis=-1, keepdims=True)
        e = jnp.exp(s)
        p = e / jnp.sum(e, axis=-1, keepdims=True)
        ps.append(p)
        outs.append(_mdot(p, vh, NN))
    return ps, outs


def _tail_fwd(p, ymix, x, kv, wout, npost, name):
    t = x.shape[0]
    tm = min(256, t)

    def body(z_ref, q_ref, y_ref, x_ref, kv_ref, w_ref, np_ref, xn_ref, o_ref):
        _, outs = _attend(q_ref[...].astype(F32), kv_ref[...])
        cat = jnp.concatenate([y_ref[...].astype(F32)] + outs, axis=1)
        g = cat * _silu(z_ref[...].astype(F32))
        o = _mdot(g, w_ref[...], NN)
        o_ref[...] = o
        xn_ref[...] = x_ref[...] + o * lax.rsqrt(jnp.mean(o * o, axis=-1, keepdims=True) + EPS) * np_ref[...]

    row = pl.BlockSpec((tm, D), lambda i: (i, 0))
    return _pcall(
        body, name=name, grid=(t // tm,),
        in_specs=[pl.BlockSpec((tm, D_CAT), lambda i: (i, 0)), pl.BlockSpec((tm, D_XA), lambda i: (i, D_CAT // D_XA)),
                  row, row, _full((N_MEM, 2 * D_XA)), _full((D_CAT, D)), _full((1, D))],
        out_specs=[row, row],
        out_shape=[jax.ShapeDtypeStruct((t, D), F32), jax.ShapeDtypeStruct((t, D), F32)], compiler_params=_cp(1),
    )(p, p, ymix, x, kv, wout, npost)


def _tail_bwd(dxn, o, p, ymix, kv, wout, npost, name, exchange=()):
    t = dxn.shape[0]
    tm = min(256, t)
    n_x, n_steps = len(exchange), t // tm

    def body(*refs):
        dxn_ref, o_ref, z_ref, q_ref, y_ref, kv_ref, w_ref, np_ref = refs[:8]
        dzq_ref, dy_ref, dw_ref, dnp_ref, dkv_ref = refs[8 + n_x:13 + n_x]
        if n_x:
            begin, end = _exchange_parts(refs[8:8 + n_x], refs[13 + n_x:13 + 2 * n_x], None, None, *refs[13 + 2 * n_x:])
            pl.when(pl.program_id(0) == 0)(begin)

        @pl.when(pl.program_id(0) == 0)
        def _():
            dw_ref[...] = jnp.zeros_like(dw_ref)
            dnp_ref[...] = jnp.zeros_like(dnp_ref)
            dkv_ref[...] = jnp.zeros_like(dkv_ref)

        q = q_ref[...].astype(F32)
        kvv = kv_ref[...]
        z = z_ref[...].astype(F32)
        ps, outs = _attend(q, kvv)
        cat = jnp.concatenate([y_ref[...].astype(F32)] + outs, axis=1)
        sz = _silu(z)
        g = cat * sz
        ov = o_ref[...]
        dr = dxn_ref[...]
        rr = lax.rsqrt(jnp.mean(ov * ov, axis=-1, keepdims=True) + EPS)
        dnp_ref[...] += jnp.sum(dr * ov * rr, axis=0, keepdims=True)
        dow = dr * np_ref[...]
        do = rr * (dow - ov * (rr * rr) * jnp.mean(dow * ov, axis=-1, keepdims=True))
        dg = _mdot(do, w_ref[...], NT)
        dw_ref[...] += _mdot(g, do, TN)
        dcat = dg * sz
        dzq_ref[:, 0:D_CAT] = (dg * cat * _dsilu(z)).astype(ACT)
        dy_ref[...] = dcat[:, 0:D].astype(ACT)
        for hh in range(XA_HEADS):
            lo = hh * HEAD
            doh = dcat[:, D + lo:D + lo + HEAD]
            qh = q[:, lo:lo + HEAD]
            kh = kvv[:, lo:lo + HEAD]
            vh = kvv[:, D_XA + lo:D_XA + lo + HEAD]
            ph = ps[hh]
            dp = _mdot(doh, vh, NT)
            ds = ph * (dp - jnp.sum(dp * ph, axis=-1, keepdims=True))
            dzq_ref[:, D_CAT + lo:D_CAT + lo + HEAD] = (_mdot(ds, kh, NN) * (HEAD ** -0.5)).astype(ACT)
            dkv_ref[:, lo:lo + HEAD] += _mdot(ds, qh, TN) * (HEAD ** -0.5)
            dkv_ref[:, D_XA + lo:D_XA + lo + HEAD] += _mdot(ph, doh, TN)
        if n_x:
            pl.when(pl.program_id(0) == n_steps - 1)(end)

    row = pl.BlockSpec((tm, D), lambda i: (i, 0))
    return _pcall(
        body, name=name, grid=(n_steps,),
        in_specs=[row, row, pl.BlockSpec((tm, D_CAT), lambda i: (i, 0)),
                  pl.BlockSpec((tm, D_XA), lambda i: (i, D_CAT // D_XA)), row,
                  _full((N_MEM, 2 * D_XA)), _full((D_CAT, D)), _full((1, D))] + [_HBM] * n_x,
        out_specs=[pl.BlockSpec((tm, ZQ), lambda i: (i, 0)), row, _full((D_CAT, D)), _full((1, D)),
                   _full((N_MEM, 2 * D_XA))] + [_HBM] * n_x,
        out_shape=[jax.ShapeDtypeStruct((t, ZQ), ACT), jax.ShapeDtypeStruct((t, D), ACT),
                   jax.ShapeDtypeStruct((D_CAT, D), F32), jax.ShapeDtypeStruct((1, D), F32),
                   jax.ShapeDtypeStruct((N_MEM, 2 * D_XA), F32)]
        + [jax.ShapeDtypeStruct(e.shape, e.dtype) for e in exchange],
        scratch_shapes=_sem_shapes(7 * n_x, n_x) if n_x else [], compiler_params=_cp(1),
    )(dxn, o, p, p, ymix, kv, wout, npost, *exchange)


def _gmlp_chunk(us, vs, lnws, lnbs, wss, bss):
    gv = [jax.nn.gelu(v) for v in vs]
    mean = sum(jnp.sum(v, axis=-1, keepdims=True) for v in gv) / D
    cen = [v - mean for v in gv]
    var = sum(jnp.sum(c * c, axis=-1, keepdims=True) for c in cen) / D
    rstd = lax.rsqrt(var + EPS)
    row = lax.broadcasted_iota(jnp.int32, (GM_CHUNK, GM_CHUNK), 0)
    col = lax.broadcasted_iota(jnp.int32, (GM_CHUNK, GM_CHUNK), 1)
    ys = []
    for g in range(GM_GROUPS):
        vn = cen[g] * rstd * lnws[g] + lnbs[g]
        sp = _nn(jnp.where(row >= col, wss[g], 0.0), vn) + bss[g]
        ys.append(jax.nn.gelu(us[g]) * sp)
    return ys


def _split_cols(v, n, width=HEAD):
    return [v[:, k * width:(k + 1) * width] for k in range(n)]


def _gmlp_operands(u_ref, v_ref, lnw_ref, lnb_ref, ws_ref, bst_ref):
    us = _split_cols(u_ref[...].astype(F32), GM_GROUPS)
    vs = _split_cols(v_ref[...].astype(F32), GM_GROUPS)
    lnws = _split_cols(lnw_ref[...], GM_GROUPS)
    lnbs = _split_cols(lnb_ref[...], GM_GROUPS)
    wss = [ws_ref[g] for g in range(GM_GROUPS)]
    bst = bst_ref[...]
    bss = [jnp.sum(bst * _onehot_row(HEAD, g), axis=1, keepdims=True) for g in range(GM_GROUPS)]
    return us, vs, lnws, lnbs, wss, bss


def _gmlp_specs():
    return [pl.BlockSpec((GM_CHUNK, D), lambda i: (i, ZQ // D)), pl.BlockSpec((GM_CHUNK, D), lambda i: (i, ZQ // D + 1)),
            _full((1, D)), _full((1, D)), _full((GM_GROUPS, GM_CHUNK, GM_CHUNK)), _full((GM_CHUNK, HEAD))]


def _gmlp_fwd(p, lnw, lnb, ws, bst, name):
    t = p.shape[0]

    def body(u_ref, v_ref, lnw_ref, lnb_ref, ws_ref, bst_ref, y_ref):
        ys = _gmlp_chunk(*_gmlp_operands(u_ref, v_ref, lnw_ref, lnb_ref, ws_ref, bst_ref))
        for g in range(GM_GROUPS):
            y_ref[:, g * HEAD:(g + 1) * HEAD] = ys[g].astype(ACT)

    return _pcall(body, name=name, grid=(t // GM_CHUNK,), in_specs=_gmlp_specs(),
                  out_specs=pl.BlockSpec((GM_CHUNK, D), lambda i: (i, 0)),
                  out_shape=jax.ShapeDtypeStruct((t, D), ACT), compiler_params=_cp(1))(p, p, lnw, lnb, ws, bst)


def _gmlp_bwd(p, dy, lnw, lnb, ws, bst, name):
    t = p.shape[0]

    def body(u_ref, v_ref, lnw_ref, lnb_ref, ws_ref, bst_ref, dy_ref, duv_ref, dlnw_ref, dlnb_ref, dws_ref, dbst_ref):
        @pl.when(pl.program_id(0) == 0)
        def _():
            dlnw_ref[...] = jnp.zeros_like(dlnw_ref)
            dlnb_ref[...] = jnp.zeros_like(dlnb_ref)
            dws_ref[...] = jnp.zeros_like(dws_ref)
            dbst_ref[...] = jnp.zeros_like(dbst_ref)

        ops = _gmlp_operands(u_ref, v_ref, lnw_ref, lnb_ref, ws_ref, bst_ref)
        _, vjp = jax.vjp(_gmlp_chunk, *ops)
        dus, dvs, dlnws, dlnbs, dwss, dbss = vjp(_split_cols(dy_ref[...].astype(F32), GM_GROUPS))
        dbst = jnp.zeros((GM_CHUNK, HEAD), F32)
        for g in range(GM_GROUPS):
            lo = g * HEAD
            duv_ref[:, lo:lo + HEAD] = dus[g].astype(ACT)
            duv_ref[:, D + lo:D + lo + HEAD] = dvs[g].astype(ACT)
            dlnw_ref[:, lo:lo + HEAD] += dlnws[g]
            dlnb_ref[:, lo:lo + HEAD] += dlnbs[g]
            dws_ref[g] += dwss[g]
            dbst = dbst + dbss[g] * _onehot_row(HEAD, g)
        dbst_ref[...] += dbst

    return _pcall(
        body, name=name, grid=(t // GM_CHUNK,),
        in_specs=_gmlp_specs() + [pl.BlockSpec((GM_CHUNK, D), lambda i: (i, 0))],
        out_specs=[pl.BlockSpec((GM_CHUNK, 2 * D), lambda i: (i, 0)), _full((1, D)), _full((1, D)),
                   _full((GM_GROUPS, GM_CHUNK, GM_CHUNK)), _full((GM_CHUNK, HEAD))],
        out_shape=[jax.ShapeDtypeStruct((t, 2 * D), ACT), jax.ShapeDtypeStruct((1, D), F32),
                   jax.ShapeDtypeStruct((1, D), F32), jax.ShapeDtypeStruct((GM_GROUPS, GM_CHUNK, GM_CHUNK), F32),
                   jax.ShapeDtypeStruct((GM_CHUNK, HEAD), F32)],
        compiler_params=_cp(1),
    )(p, p, lnw, lnb, ws, bst, dy)


def _prev_halo(tm, col):
    return pl.BlockSpec((HALO, D), lambda i: (jnp.maximum(i * (tm // HALO) - 1, 0), col))


def _next_halo(tm, col, n_tiles):
    return pl.BlockSpec((HALO, D), lambda i: (jnp.minimum(i + 1, n_tiles - 1) * (tm // HALO), col))


def _taps_back(ext, w, width):
    acc = None
    for k in range(width):
        s = width - 1 - k
        term = w[k:k + 1, :] * (pltpu.roll(ext, s, 0) if s else ext)[HALO:, :]
        acc = term if acc is None else acc + term
    return acc


def _taps_fwd(ext, w, width, n):
    rows = ext.shape[0]
    acc = None
    for k in range(width):
        s = width - 1 - k
        term = w[k:k + 1, :] * (pltpu.roll(ext, rows - s, 0) if s else ext)[0:n, :]
        acc = term if acc is None else acc + term
    return acc


def _sconv_fwd(p, cw, name):
    t = p.shape[0]
    tm = min(256, t)

    def body(b_ref, c_ref, h_ref, cp_ref, hp_ref, w_ref, y_ref):
        first = pl.program_id(0) == 0
        prev = jnp.where(first, 0.0, cp_ref[...].astype(F32) * hp_ref[...].astype(F32))
        ext = jnp.concatenate([prev, c_ref[...].astype(F32) * h_ref[...].astype(F32)], axis=0)
        y_ref[...] = (b_ref[...].astype(F32) * _taps_back(ext, w_ref[...], 3)).astype(ACT)

    c0 = ZQ // D
    tile = [pl.BlockSpec((tm, D), lambda i, c=c: (i, c)) for c in (c0, c0 + 1, c0 + 2)]
    return _pcall(body, name=name, grid=(t // tm,),
                  in_specs=tile + [_prev_halo(tm, c0 + 1), _prev_halo(tm, c0 + 2), _full((HALO, D))],
                  out_specs=pl.BlockSpec((tm, D), lambda i: (i, 0)),
                  out_shape=jax.ShapeDtypeStruct((t, D), ACT), compiler_params=_cp(1))(p, p, p, p, p, cw)


def _sconv_bwd(p, dy, cw, name):
    t = p.shape[0]
    tm = min(256, t)
    n_tiles = t // tm

    def body(b_ref, c_ref, h_ref, cp_ref, hp_ref, bn_ref, dy_ref, dyn_ref, w_ref, d_ref, dw_ref):
        i = pl.program_id(0)

        @pl.when(i == 0)
        def _():
            dw_ref[...] = jnp.zeros_like(dw_ref)

        w = w_ref[...]
        bv, cv, hv = b_ref[...].astype(F32), c_ref[...].astype(F32), h_ref[...].astype(F32)
        dyv = dy_ref[...].astype(F32)
        prev = jnp.where(i == 0, 0.0, cp_ref[...].astype(F32) * hp_ref[...].astype(F32))
        ext = jnp.concatenate([prev, cv * hv], axis=0)
        conv = _taps_back(ext, w, 3)
        dconv = dyv * bv
        nxt = jnp.where(i == n_tiles - 1, 0.0, dyn_ref[...].astype(F32) * bn_ref[...].astype(F32))
        dc = _taps_fwd(jnp.concatenate([dconv, nxt], axis=0), w, 3, tm)
        d_ref[:, 0:D] = (dyv * conv).astype(ACT)
        d_ref[:, D:2 * D] = (dc * hv).astype(ACT)
        d_ref[:, 2 * D:3 * D] = (dc * cv).astype(ACT)
        for k in range(3):
            s = 2 - k
            shifted = (pltpu.roll(ext, s, 0) if s else ext)[HALO:, :]
            dw_ref[k:k + 1, :] += jnp.sum(dconv * shifted, axis=0, keepdims=True)

    c0 = ZQ // D
    tile = [pl.BlockSpec((tm, D), lambda i, c=c: (i, c)) for c in (c0, c0 + 1, c0 + 2)]
    return _pcall(
        body, name=name, grid=(n_tiles,),
        in_specs=tile + [_prev_halo(tm, c0 + 1), _prev_halo(tm, c0 + 2), _next_halo(tm, c0, n_tiles),
                         pl.BlockSpec((tm, D), lambda i: (i, 0)), _next_halo(tm, 0, n_tiles), _full((HALO, D))],
        out_specs=[pl.BlockSpec((tm, 3 * D), lambda i: (i, 0)), _full((HALO, D))],
        out_shape=[jax.ShapeDtypeStruct((t, 3 * D), ACT), jax.ShapeDtypeStruct((HALO, D), F32)],
        compiler_params=_cp(1),
    )(p, p, p, p, p, p, dy, dy, cw)


def _l2_heads(s, scale):
    outs, rs = [], []
    for hh in range(DN_HEADS):
        blk = s[:, hh * HEAD:(hh + 1) * HEAD]
        r = lax.rsqrt(jnp.sum(blk * blk, axis=-1, keepdims=True) + EPS)
        outs.append(blk * (r * scale))
        rs.append(r)
    return outs, rs


_QKV_SCALE = (HEAD ** -0.5, 1.0, None)


def _qkv_fwd(p, cw, name):
    t = p.shape[0]
    tm = min(256, t)

    def body(q_ref, k_ref, v_ref, qp_ref, kp_ref, vp_ref, w_ref, o_ref):
        first = pl.program_id(0) == 0
        for part, (ref, pref) in enumerate(((q_ref, qp_ref), (k_ref, kp_ref), (v_ref, vp_ref))):
            prev = jnp.where(first, 0.0, pref[...].astype(F32))
            ext = jnp.concatenate([prev, ref[...].astype(F32)], axis=0)
            s = _silu(_taps_back(ext, w_ref[:, part * D:(part + 1) * D], 4))
            if _QKV_SCALE[part] is None:
                o_ref[:, part * D:(part + 1) * D] = s.astype(ACT)
            else:
                outs, _ = _l2_heads(s, _QKV_SCALE[part])
                for hh in range(DN_HEADS):
                    o_ref[:, part * D + hh * HEAD:part * D + (hh + 1) * HEAD] = outs[hh].astype(ACT)

    c0 = ZQ // D
    tile = [pl.BlockSpec((tm, D), lambda i, c=c: (i, c)) for c in (c0, c0 + 1, c0 + 2)]
    halo = [_prev_halo(tm, c) for c in (c0, c0 + 1, c0 + 2)]
    return _pcall(body, name=name, grid=(t // tm,), in_specs=tile + halo + [_full((HALO, 3 * D))],
                  out_specs=pl.BlockSpec((tm, 3 * D), lambda i: (i, 0)),
                  out_shape=jax.ShapeDtypeStruct((t, 3 * D), ACT), compiler_params=_cp(1))(p, p, p, p, p, p, cw)


def _qkv_bwd(p, dqkv, cw, name):
    t = p.shape[0]
    tm = min(256, t)
    n_tiles = t // tm

    def body(*refs):
        tiles, prevs, nexts = refs[0:3], refs[3:6], refs[6:9]
        d_tiles, d_nexts = refs[9:12], refs[12:15]
        w_ref, o_ref, dw_ref = refs[15:]
        i = pl.program_id(0)

        @pl.when(i == 0)
        def _():
            dw_ref[...] = jnp.zeros_like(dw_ref)

        for part in range(3):
            w = w_ref[:, part * D:(part + 1) * D]
            prev = jnp.where(i == 0, 0.0, prevs[part][...].astype(F32))
            ext = jnp.concatenate([prev, tiles[part][...].astype(F32), nexts[part][...].astype(F32)], axis=0)
            xc = _taps_back(ext, w, 4)
            dout = jnp.concatenate([d_tiles[part][...].astype(F32), d_nexts[part][...].astype(F32)], axis=0)
            s = _silu(xc)
            if _QKV_SCALE[part] is None:
                ds = dout
            else:
                scale = _QKV_SCALE[part]
                pieces = []
                for hh in range(DN_HEADS):
                    blk = s[:, hh * HEAD:(hh + 1) * HEAD]
                    dblk = dout[:, hh * HEAD:(hh + 1) * HEAD]
                    r = lax.rsqrt(jnp.sum(blk * blk, axis=-1, keepdims=True) + EPS)
                    pieces.append(scale * r * (dblk - blk * (r * r) * jnp.sum(dblk * blk, axis=-1, keepdims=True)))
                ds = jnp.concatenate(pieces, axis=1)
            dxc = ds * _dsilu(xc)
            row = lax.broadcasted_iota(jnp.int32, (tm + HALO, 1), 0)
            dxc = jnp.where(jnp.logical_and(i == n_tiles - 1, row >= tm), 0.0, dxc)
            o_ref[:, part * D:(part + 1) * D] = _taps_fwd(dxc, w, 4, tm).astype(ACT)
            for k in range(4):
                sh = 3 - k
                shifted = (pltpu.roll(ext, sh, 0) if sh else ext)[HALO:HALO + tm, :]
                dw_ref[k:k + 1, part * D:(part + 1) * D] += jnp.sum(dxc[0:tm, :] * shifted, axis=0, keepdims=True)

    c0 = ZQ // D
    cols = (c0, c0 + 1, c0 + 2)
    tile = [pl.BlockSpec((tm, D), lambda i, c=c: (i, c)) for c in cols]
    dtile = [pl.BlockSpec((tm, D), lambda i, c=c: (i, c)) for c in range(3)]
    in_specs = (tile + [_prev_halo(tm, c) for c in cols] + [_next_halo(tm, c, n_tiles) for c in cols]
                + dtile + [_next_halo(tm, c, n_tiles) for c in range(3)] + [_full((HALO, 3 * D))])
    return _pcall(
        body, name=name, grid=(n_tiles,), in_specs=in_specs,
        out_specs=[pl.BlockSpec((tm, 3 * D), lambda i: (i, 0)), _full((HALO, 3 * D))],
        out_shape=[jax.ShapeDtypeStruct((t, 3 * D), ACT), jax.ShapeDtypeStruct((HALO, 3 * D), F32)],
        compiler_params=_cp(1),
    )(*([p] * 9), *([dqkv] * 6), cw)


def _tri_masks(n):
    row = lax.broadcasted_iota(jnp.int32, (n, n), 0)
    col = lax.broadcasted_iota(jnp.int32, (n, n), 1)
    return row, col


@jax.custom_vjp
def _unit_lower_inverses(mats):
    n = DN_CHUNK
    row, col = _tri_masks(n)
    eye = (row == col).astype(F32)
    same16 = (row // 16) == (col // 16)
    same32 = (row // 32) == (col // 32)
    pw = [jnp.where(same16, a, 0.0) for a in mats]
    x = [eye - p for p in pw]
    for _ in range(3):
        pw = [_hnn(p, p) for p in pw]
        x = [_hnn(xi, eye + p) for xi, p in zip(x, pw)]
    for keep in (jnp.logical_and(same32, jnp.logical_not(same16)), jnp.logical_not(same32)):
        inner = [_hnn(jnp.where(keep, a, 0.0), xi) for a, xi in zip(mats, x)]
        x = [xi - _hnn(xi, y) for xi, y in zip(x, inner)]
    return tuple(x)


def _uli_fwd(mats):
    t = _unit_lower_inverses(mats)
    return t, t


def _uli_bwd(ts, gs):
    inner = [_hnt(g, t) for g, t in zip(gs, ts)]
    return (tuple(-_htn(t, y) for t, y in zip(ts, inner)),)


_unit_lower_inverses.defvjp(_uli_fwd, _uli_bwd)


@jax.custom_vjp
def _known_inverses(mats, ts):
    return ts


_known_inverses.defvjp(lambda mats, ts: (ts, ts),
                       lambda ts, gs: (_uli_bwd(ts, gs)[0], tuple(jnp.zeros_like(t) for t in ts)))


def _pick_col(m, k):
    return jnp.sum(m * _onehot_row(m.shape[1], k), axis=1, keepdims=True)


def _pick_row(m, k):
    hot = (lax.broadcasted_iota(jnp.int32, (m.shape[0], 1), 0) == k).astype(F32)
    return jnp.sum(m * hot, axis=0, keepdims=True)


def _delta_chunk(states, qs, ks, vs, ab, alog, dtb, onw, known_inverses=None):
    n = DN_CHUNK
    heads = range(DN_HEADS)
    row, col = _tri_masks(n)
    incl = row >= col
    lane = lax.broadcasted_iota(jnp.int32, (1, HEAD), 1)
    g_all = jnp.where(lane < DN_HEADS, -jnp.exp(alog) * jax.nn.softplus(ab + dtb), 0.0)
    c_cols = _hnn(incl.astype(F32), g_all)
    c_rows = _htn(g_all, (row <= col).astype(F32))
    g_tot = jnp.sum(g_all, axis=0, keepdims=True)
    beta_all = jax.nn.sigmoid(ab)
    ccol = [_pick_col(c_cols, h) for h in heads]
    crow = [_pick_row(c_rows, h) for h in heads]
    gl = [_pick_col(g_tot, h) for h in heads]
    beta = [_pick_col(beta_all, DN_HEADS + h) for h in heads]
    decay = [jnp.exp(jnp.where(incl, ccol[h] - crow[h], -1e30)) for h in heads]
    eg = [jnp.exp(ccol[h]) for h in heads]
    kb = [ks[h] * beta[h] for h in heads]
    amat = [jnp.where(row > col, _nt(kb[h], ks[h]) * decay[h], 0.0) for h in heads]
    if known_inverses is None:
        tmat = _unit_lower_inverses(tuple(amat))
    else:
        tmat = _known_inverses(tuple(amat), tuple(known_inverses))
    u = [_nn(tmat[h], vs[h] * beta[h]) for h in heads]
    w = [_nn(tmat[h], kb[h] * eg[h]) for h in heads]
    qk = [_nt(qs[h], ks[h]) * decay[h] for h in heads]
    v_new = [u[h] - _nn(w[h], states[h]) for h in heads]
    o = [_nn(qs[h] * eg[h], states[h]) + _nn(qk[h], v_new[h]) for h in heads]
    new_states = [states[h] * jnp.exp(gl[h]) + _tn(ks[h] * jnp.exp(gl[h] - ccol[h]), v_new[h]) for h in heads]
    ys = [o[h] * lax.rsqrt(jnp.mean(o[h] * o[h], axis=-1, keepdims=True) + EPS) * onw for h in heads]
    return (ys, new_states), tmat


def _head_cols(ref):
    return [ref[:, h * HEAD:(h + 1) * HEAD].astype(F32) for h in range(DN_HEADS)]


def _delta_fwd(qkv, ab, alog, dtb, onw, name):
    t = qkv.shape[0]
    nc = t // DN_CHUNK

    def body(q_ref, k_ref, v_ref, ab_ref, alog_ref, dtb_ref, onw_ref, y_ref, keep_ref, inv_ref, state):
        @pl.when(pl.program_id(0) == 0)
        def _():
            state[...] = jnp.zeros_like(state)

        s0 = [state[hh] for hh in range(DN_HEADS)]
        (ys, s1), tmat = _delta_chunk(s0, _head_cols(q_ref), _head_cols(k_ref), _head_cols(v_ref), ab_ref[...],
                                      alog_ref[...], dtb_ref[...], onw_ref[...])
        for hh in range(DN_HEADS):
            keep_ref[0, hh] = s0[hh]
            inv_ref[0, hh] = tmat[hh]
            state[hh] = s1[hh]
            y_ref[:, hh * HEAD:(hh + 1) * HEAD] = ys[hh].astype(ACT)

    chunk = [pl.BlockSpec((DN_CHUNK, D), lambda i, c=c: (i, c)) for c in range(3)]
    return _pcall(
        body, name=name, grid=(nc,),
        in_specs=chunk + [pl.BlockSpec((DN_CHUNK, HEAD), lambda i: (i, 0)), _full((1, HEAD)), _full((1, HEAD)),
                          _full((1, HEAD))],
        out_specs=[pl.BlockSpec((DN_CHUNK, D), lambda i: (i, 0)),
                   pl.BlockSpec((1, DN_HEADS, HEAD, HEAD), lambda i: (i, 0, 0, 0)),
                   pl.BlockSpec((1, DN_HEADS, DN_CHUNK, DN_CHUNK), lambda i: (i, 0, 0, 0))],
        out_shape=[jax.ShapeDtypeStruct((t, D), ACT), jax.ShapeDtypeStruct((nc, DN_HEADS, HEAD, HEAD), F32),
                   jax.ShapeDtypeStruct((nc, DN_HEADS, DN_CHUNK, DN_CHUNK), F32)],
        scratch_shapes=[pltpu.VMEM((DN_HEADS, HEAD, HEAD), F32)], compiler_params=_cp(1),
    )(qkv, qkv, qkv, ab, alog, dtb, onw)


def _delta_bwd(qkv, ab, alog, dtb, onw, keep, inv, dy, name):
    t = qkv.shape[0]
    nc = t // DN_CHUNK

    def body(q_ref, k_ref, v_ref, ab_ref, alog_ref, dtb_ref, onw_ref, keep_ref, inv_ref, dy_ref,
             dqkv_ref, dab_ref, dalog_ref, ddtb_ref, donw_ref, dstate):
        @pl.when(pl.program_id(0) == 0)
        def _():
            dstate[...] = jnp.zeros_like(dstate)
            dalog_ref[...] = jnp.zeros_like(dalog_ref)
            ddtb_ref[...] = jnp.zeros_like(ddtb_ref)
            donw_ref[...] = jnp.zeros_like(donw_ref)

        s0 = [keep_ref[0, hh] for hh in range(DN_HEADS)]
        known = [inv_ref[0, hh] for hh in range(DN_HEADS)]
        _, vjp, _ = jax.vjp(functools.partial(_delta_chunk, known_inverses=known), s0, _head_cols(q_ref),
                            _head_cols(k_ref), _head_cols(v_ref), ab_ref[...], alog_ref[...], dtb_ref[...], onw_ref[...],
                            has_aux=True)
        ds, dq, dk, dv, dab, dal, ddt, don = vjp((_head_cols(dy_ref), [dstate[hh] for hh in range(DN_HEADS)]))
        for hh in range(DN_HEADS):
            lo = hh * HEAD
            dstate[hh] = ds[hh]
            dqkv_ref[:, lo:lo + HEAD] = dq[hh].astype(ACT)
            dqkv_ref[:, D + lo:D + lo + HEAD] = dk[hh].astype(ACT)
            dqkv_ref[:, 2 * D + lo:2 * D + lo + HEAD] = dv[hh].astype(ACT)
        dab_ref[...] = dab
        dalog_ref[...] += dal
        ddtb_ref[...] += ddt
        donw_ref[...] += don

    rev = lambda i: nc - 1 - i
    chunk = [pl.BlockSpec((DN_CHUNK, D), lambda i, c=c: (rev(i), c)) for c in range(3)]
    small = jax.ShapeDtypeStruct((1, HEAD), F32)
    return _pcall(
        body, name=name, grid=(nc,),
        in_specs=chunk + [pl.BlockSpec((DN_CHUNK, HEAD), lambda i: (rev(i), 0)), _full((1, HEAD)), _full((1, HEAD)),
                          _full((1, HEAD)), pl.BlockSpec((1, DN_HEADS, HEAD, HEAD), lambda i: (rev(i), 0, 0, 0)),
                          pl.BlockSpec((1, DN_HEADS, DN_CHUNK, DN_CHUNK), lambda i: (rev(i), 0, 0, 0)),
                          pl.BlockSpec((DN_CHUNK, D), lambda i: (rev(i), 0))],
        out_specs=[pl.BlockSpec((DN_CHUNK, 3 * D), lambda i: (rev(i), 0)),
                   pl.BlockSpec((DN_CHUNK, HEAD), lambda i: (rev(i), 0)), _full((1, HEAD)), _full((1, HEAD)),
                   _full((1, HEAD))],
        out_shape=[jax.ShapeDtypeStruct((t, 3 * D), ACT), jax.ShapeDtypeStruct((t, HEAD), F32), small, small, small],
        scratch_shapes=[pltpu.VMEM((DN_HEADS, HEAD, HEAD), F32)], compiler_params=_cp(1),
    )(qkv, qkv, qkv, ab, alog, dtb, onw, keep, inv, dy)


def _loss_head(y, target):
    t = y.shape[0]
    tm = min(512, t)

    def body(y_ref, t_ref, l_ref, dy_ref):
        @pl.when(pl.program_id(0) == 0)
        def _():
            l_ref[...] = jnp.zeros_like(l_ref)

        diff = y_ref[...] - t_ref[...]
        dy_ref[...] = diff * (1.0 / D)
        l_ref[...] += 0.5 * jnp.sum(jnp.sum(diff * diff, axis=-1, keepdims=True) * (1.0 / D), axis=0, keepdims=True)

    row = pl.BlockSpec((tm, D), lambda i: (i, 0))
    return _pcall(body, name="loss_head", grid=(t // tm,), in_specs=[row, row], out_specs=[_full((8, 128)), row],
                  out_shape=[jax.ShapeDtypeStruct((8, 128), F32), jax.ShapeDtypeStruct((t, D), F32)],
                  compiler_params=_cp(1))(y, target)


_SMALL_SHARDED = (("a_ln_w", (2, 128), 1), ("a_ln_b", (2, 128), 1), ("b_conv_w", (1, 3, 128), 2),
                  ("c_conv_w", (1, 4, 384), 2))
_REPLICATED = (("mem_norm_w", (1024,)), ("norm_pre", (4, 1024)), ("norm_post", (4, 1024)),
               ("a_w_s", (2, 8, 128, 128)), ("a_b_s", (2, 8, 128)), ("c_a_log", (1, 8)), ("c_dt_bias", (1, 8)),
               ("c_o_norm_w", (1, 128)))


def _layer_shards(given, prefix):
    w_out = given[prefix + "w_out"]
    w_ins = [given[prefix + "a_w_in"][0], given[prefix + "b_w_in"][0], given[prefix + "c_w_in"][0],
             given[prefix + "a_w_in"][1]]
    rows = [jnp.concatenate([given[prefix + "w_mem_kv"], w_out[0]])] + [w_out[i] for i in (1, 2, 3)]
    return [[rows[i], w_ins[i]] for i in range(4)]


def _pack_rows(flat_parts, rows, lead=()):
    n_lead = len(lead)
    flat = jnp.concatenate([a.reshape(lead + (-1,)) for a in flat_parts], axis=n_lead)
    pad = rows * D - flat.shape[n_lead]
    flat = jnp.pad(flat, [(0, 0)] * n_lead + [(0, pad)])
    return flat.reshape(lead + (rows, D))


def _unpack_rows(packed, table, lead=()):
    flat = packed.reshape(lead + (-1,))
    out, at = {}, 0
    for name, shape in table:
        size = math.prod(shape)
        out[name] = lax.slice_in_dim(flat, at, at + size, axis=len(lead)).reshape(lead + shape)
        at += size
    return out


def _join_shards(blocks, axis):
    moved = jnp.moveaxis(blocks, 0, axis)
    shape = moved.shape
    return moved.reshape(shape[:axis] + (shape[axis] * shape[axis + 1],) + shape[axis + 2:])


def _split_shards(full, axis):
    shape = full.shape
    split = full.reshape(shape[:axis] + (N_DEV, shape[axis] // N_DEV) + shape[axis + 1:])
    return jnp.moveaxis(split, axis, 0)


_A_COLS = ((2560, 1536), (2048, 512), (0, 2048))
_BC_COLS = ((3584, 1536), (3072, 512), (0, 3072))
_C_COLS = ((3600, 1536), (3088, 512), (0, 3072))


def _reorder_cols(w, cols):
    return jnp.concatenate([w[:, s:s + n] for s, n in cols], axis=1)


def _restore_cols(pieces_in_my_order, cols, extra=()):
    placed = sorted(list(zip([s for s, _ in cols], pieces_in_my_order)) + list(extra), key=lambda sp: sp[0])
    return jnp.concatenate([piece for _, piece in placed], axis=1)


def kernel(x, mem, mem_norm_w, w_mem_kv, norm_pre, norm_post, w_out, a_w_in, a_ln_w, a_ln_b, a_w_s, a_b_s, b_w_in, b_conv_w, c_w_in, c_conv_w, c_a_log, c_dt_bias, c_o_norm_w, loss_target, m_mem_norm_w, m_w_mem_kv, m_norm_pre, m_norm_post, m_w_out, m_a_w_in, m_a_ln_w, m_a_ln_b, m_a_w_s, m_a_b_s, m_b_w_in, m_b_conv_w, m_c_w_in, m_c_conv_w, m_c_a_log, m_c_dt_bias, m_c_o_norm_w, v_mem_norm_w, v_w_mem_kv, v_norm_pre, v_norm_post, v_w_out, v_a_w_in, v_a_ln_w, v_a_ln_b, v_a_w_s, v_a_b_s, v_b_w_in, v_b_conv_w, v_c_w_in, v_c_conv_w, v_c_a_log, v_c_dt_bias, v_c_o_norm_w):
    given = dict(locals())
    x0 = x[0]
    mem0 = mem[0]
    target = loss_target[0]

    w_sh, m_sh, v_sh = (_layer_shards(given, pre) for pre in ("", "m_", "v_"))
    small_tab = tuple((n, s) for n, s, _ in _SMALL_SHARDED)
    small = _pack_rows([given[n] for n, _, _ in _SMALL_SHARDED], HALO)
    g_rows0, g_in0, g_small = _all_gather([w.astype(MXU) for w in w_sh[0]] + [small], "gather_weights")
    small_full = _unpack_rows(g_small, small_tab, lead=(N_DEV,))
    full = {n: _join_shards(small_full[n], ax) for n, _, ax in _SMALL_SHARDED}
    wkv = g_rows0[:, :128].reshape(D, 2 * D_XA)
    wouts = [g_rows0[:, 128:].reshape(D_CAT, D)]
    w_in = [_reorder_cols(_join_shards(g_in0, 1), _A_COLS)]
    w_cab = None
    lnw = [full["a_ln_w"][j][None, :] for j in range(2)]
    lnb = [full["a_ln_b"][j][None, :] for j in range(2)]
    ws = [a_w_s[j] for j in range(2)]
    bst = [jnp.pad(a_b_s[j].T, ((0, 0), (0, HEAD - GM_GROUPS))) for j in range(2)]
    cw_b = jnp.pad(full["b_conv_w"][0], ((0, HALO - 3), (0, 0)))
    cw_c = jnp.pad(full["c_conv_w"][0], ((0, HALO - 4), (0, 0)))
    alog = jnp.pad(c_a_log, ((0, 0), (0, HEAD - DN_HEADS)))
    dtb = jnp.pad(c_dt_bias, ((0, 0), (0, HEAD - DN_HEADS)))
    onw = c_o_norm_w
    mw = mem_norm_w[None, :]

    kv = _memkv_fwd(mem0, mw, wkv)
    xs, saved = [x0], []
    for i in range(4):
        kind = i % 3
        npre, npost = norm_pre[i][None, :], norm_post[i][None, :]
        ahead = [w.astype(MXU) for w in w_sh[i + 1]] if i < 3 else []
        res = _proj_fwd(xs[i], npre, w_in[i], w_cab if kind == 2 else None, f"proj_fwd_{i}", gather=ahead)
        if ahead:
            g_wout, g_in = res[-2:]
            wouts.append(g_wout.reshape(D_CAT, D))
            if i + 1 == 2:
                c_full = jnp.concatenate([g_in[d] for d in range(N_DEV)], axis=1)
                w_in.append(_reorder_cols(c_full, _C_COLS))
                w_cab = jnp.pad(c_full[:, 3072:3088], ((0, 0), (0, HEAD - 16)))
            else:
                w_in.append(_reorder_cols(_join_shards(g_in, 1), _BC_COLS if i + 1 == 1 else _A_COLS))
        if kind == 2:
            p, h, ab = res[:3]
            qkv = _qkv_fwd(p, cw_c, f"qkv_fwd_{i}")
            ymix, keep, inv = _delta_fwd(qkv, ab, alog, dtb, onw, f"delta_fwd_{i}")
            extra = (qkv, ab, keep, inv)
        else:
            p, h = res[:2]
            if kind == 0:
                ymix = _gmlp_fwd(p, lnw[i // 3], lnb[i // 3], ws[i // 3], bst[i // 3], f"gmlp_fwd_{i}")
            else:
                ymix = _sconv_fwd(p, cw_b, f"sconv_fwd_{i}")
            extra = ()
        xn, o = _tail_fwd(p, ymix, xs[i], kv, wouts[i], npost, f"tail_fwd_{i}")
        xs.append(xn)
        saved.append((p, h, ymix, o, extra))

    loss_tile, dx = _loss_head(xs[4], target)
    loss = lax.psum(loss_tile[0, 0], ("x", "y", "c"))

    g = {}
    d_npre, d_npost = [None] * 4, [None] * 4
    d_ws, d_bs, d_lnw, d_lnb = [None] * 2, [None] * 2, [None] * 2, [None] * 2
    dkv = None
    pending, landed = [], [None] * 4
    for i in reversed(range(4)):
        kind = i % 3
        p, h, ymix, o, extra = saved[i]
        npre, npost = norm_pre[i][None, :], norm_post[i][None, :]
        res = _tail_bwd(dx, o, p, ymix, kv, wouts[i], npost, f"tail_bwd_{i}", exchange=pending)
        dzq, dymix, d_wout, d_npost[i], dkv_i = res[:5]
        if pending:
            landed[i + 1] = res[5:]
        dkv = dkv_i if dkv is None else dkv + dkv_i
        w_zq, w_mix = w_in[i][:, :ZQ], w_in[i][:, ZQ:]
        dw_zq = _matmul_tn(h, dzq, f"dw_zq_{i}")
        if kind == 0:
            j = i // 3
            dmix, d_lnw[j], d_lnb[j], d_ws[j], dbst = _gmlp_bwd(p, dymix, lnw[j], lnb[j], ws[j], bst[j], f"gmlp_bwd_{i}")
            d_bs[j] = dbst[:, :GM_GROUPS].T
            dw_mix = _matmul_tn(h, dmix, f"dw_mix_{i}")
            d_win = _restore_cols([dw_zq[:, :D_CAT], dw_zq[:, D_CAT:], dw_mix], _A_COLS)
            dps, wparts = [dzq, dmix], [w_zq, w_mix]
        elif kind == 1:
            dmix, dcw = _sconv_bwd(p, dymix, cw_b, f"sconv_bwd_{i}")
            g["b_conv_w"] = dcw[None, :3]
            dw_mix = _matmul_tn(h, dmix, f"dw_mix_{i}")
            d_win = _restore_cols([dw_zq[:, :D_CAT], dw_zq[:, D_CAT:], dw_mix], _BC_COLS)
            dps, wparts = [dzq, dmix], [w_zq, w_mix]
        else:
            qkv, ab, keep, inv = extra
            dqkv, dab, dalog, ddtb, donw = _delta_bwd(qkv, ab, alog, dtb, onw, keep, inv, dymix, f"delta_bwd_{i}")
            dmix, dcw = _qkv_bwd(p, dqkv, cw_c, f"qkv_bwd_{i}")
            g["c_conv_w"] = dcw[None, :4]
            g["c_a_log"], g["c_dt_bias"], g["c_o_norm_w"] = dalog[:, :DN_HEADS], ddtb[:, :DN_HEADS], donw
            dw_mix = _matmul_tn(h, dmix, f"dw_mix_{i}")
            dw_ab = _matmul_tn(h, dab, f"dw_ab_{i}")
            d_win = _restore_cols([dw_zq[:, :D_CAT], dw_zq[:, D_CAT:], dw_mix], _C_COLS, extra=[(3072, dw_ab[:, :16])])
            dps, wparts = [dzq, dmix, dab], [w_zq, w_mix, w_cab]
        width = d_win.shape[1] // N_DEV
        pending = [d_wout.reshape(N_DEV, D_CAT // N_DEV, D).astype(XCH),
                   jnp.stack([d_win[:, d * width:(d + 1) * width] for d in range(N_DEV)]).astype(XCH)]
        dx, d_npre[i] = _proj_bwd_x(dps, wparts, xs[i], dx, npre, f"proj_bwd_x_{i}")

    d_wkv, d_mw = _memkv_bwd(mem0, mw, wkv, dkv)
    g["mem_norm_w"] = d_mw[0]
    g["norm_pre"] = jnp.concatenate(d_npre, axis=0)
    g["norm_post"] = jnp.concatenate(d_npost, axis=0)
    g["a_ln_w"] = jnp.concatenate(d_lnw, axis=0)
    g["a_ln_b"] = jnp.concatenate(d_lnb, axis=0)
    g["a_w_s"] = jnp.stack(d_ws)
    g["a_b_s"] = jnp.stack(d_bs)

    pending[0] = jnp.concatenate([d_wkv.reshape(N_DEV, 128, D).astype(XCH), pending[0]], axis=1)
    e_small = _pack_rows([_split_shards(g[n], ax) for n, _, ax in _SMALL_SHARDED], HALO, lead=(N_DEV,))
    r_pack = _pack_rows([g[n] for n, _ in _REPLICATED], R_REPL)
    landed0, r_all = _exchange(pending + [e_small], r_pack, "exchange_grads")
    landed[0] = landed0[:2]

    res = [[_reduce_adamw(landed[i][a], w_sh[i][a], m_sh[i][a], v_sh[i][a], f"adamw_{i}_{a}") for a in range(2)]
           for i in range(4)]
    res_small = _reduce_adamw(landed0[2], small, _pack_rows([given["m_" + n] for n, _, _ in _SMALL_SHARDED], HALO),
                              _pack_rows([given["v_" + n] for n, _, _ in _SMALL_SHARDED], HALO), "adamw_small")
    rp = _reduce_adamw(r_all, _pack_rows([given[n] for n, _ in _REPLICATED], R_REPL),
                       _pack_rows([given["m_" + n] for n, _ in _REPLICATED], R_REPL),
                       _pack_rows([given["v_" + n] for n, _ in _REPLICATED], R_REPL), "adamw_replicated")

    order = ["mem_norm_w", "w_mem_kv", "norm_pre", "norm_post", "w_out", "a_w_in", "a_ln_w", "a_ln_b", "a_w_s", "a_b_s",
             "b_w_in", "b_conv_w", "c_w_in", "c_conv_w", "c_a_log", "c_dt_bias", "c_o_norm_w"]
    outs = [loss, dx[None]]
    for kind in range(4):
        got = dict(_unpack_rows(rp[kind], _REPLICATED))
        got.update(_unpack_rows(res_small[kind], small_tab))
        got["w_mem_kv"] = res[0][0][kind][:128]
        got["w_out"] = jnp.stack([res[0][0][kind][128:]] + [res[i][0][kind] for i in (1, 2, 3)])
        got["a_w_in"] = jnp.stack([res[0][1][kind], res[3][1][kind]])
        got["b_w_in"] = res[1][1][kind][None]
        got["c_w_in"] = res[2][1][kind][None]
        outs += [got[n] for n in order]
    return tuple(outs)
```

```python
import functools
import math

import jax
import jax.numpy as jnp
from jax import lax
from jax.experimental import pallas as pl
from jax.experimental.pallas import tpu as pltpu

F32 = jnp.float32
MXU = jnp.bfloat16
ACT = jnp.bfloat16

D = 1024
D_XA = 512
D_CAT = 1536
N_MEM = 256
XA_HEADS = 4
HEAD = 128
ZQ = D_CAT + D_XA
EPS = 1e-6
GM_CHUNK = 128
GM_GROUPS = 8
DN_HEADS = 8
DN_CHUNK = 64
N_DEV = 8
HALO = 8
VMEM_LIMIT = 56 * 1024 * 1024
XCH = jnp.bfloat16
R_REPL = 16

ADAM_LR = 0.001
ADAM_B1 = 0.9
ADAM_B2 = 0.999
ADAM_EPS = 1e-08
ADAM_WD = 0.01
ADAM_STEP = 10

NN = ((1,), (0,))
NT = ((1,), (1,))
TN = ((0,), (0,))
MESH = pl.DeviceIdType.MESH


def _pcall(body, **kw):
    return pl.pallas_call(body, **kw)


def _cp(n_axes):
    return pltpu.CompilerParams(dimension_semantics=("arbitrary",) * n_axes, vmem_limit_bytes=VMEM_LIMIT)


def _dot(a, b, dims, prec=None):
    return lax.dot_general(a, b, (dims, ((), ())), preferred_element_type=F32, precision=prec)


def _mdot(a, b, dims):
    return _dot(a.astype(MXU), b.astype(MXU), dims)


def _make_mms(raw):
    @jax.custom_vjp
    def nn(a, b):
        return raw(a, b, NN)

    @jax.custom_vjp
    def nt(a, b):
        return raw(a, b, NT)

    @jax.custom_vjp
    def tn(a, b):
        return raw(a, b, TN)

    nn.defvjp(lambda a, b: (nn(a, b), (a, b)), lambda r, g: (nt(g, r[1]), tn(r[0], g)))
    nt.defvjp(lambda a, b: (nt(a, b), (a, b)), lambda r, g: (nn(g, r[1]), tn(g, r[0])))
    tn.defvjp(lambda a, b: (tn(a, b), (a, b)), lambda r, g: (nt(r[1], g), nn(r[0], g)))
    return nn, nt, tn


_nn, _nt, _tn = _make_mms(_mdot)


def _split_dot(a, b, dims):
    ah = a.astype(jnp.bfloat16)
    bh = b.astype(jnp.bfloat16)
    al = (a - ah.astype(F32)).astype(jnp.bfloat16)
    bl = (b - bh.astype(F32)).astype(jnp.bfloat16)
    return _dot(ah, bh, dims) + (_dot(ah, bl, dims) + _dot(al, bh, dims))


_hnn, _hnt, _htn = _make_mms(_split_dot)


def _full(shape):
    return pl.BlockSpec(shape, lambda *_: (0,) * len(shape))


def _silu(z):
    return z * jax.nn.sigmoid(z)


def _dsilu(z):
    s = jax.nn.sigmoid(z)
    return s * (1.0 + z * (1.0 - s))


def _onehot_row(n, k):
    return (lax.broadcasted_iota(jnp.int32, (1, n), 1) == k).astype(F32)


_HBM = pl.BlockSpec(memory_space=pl.ANY)


def _sem_shapes(n_remote, n_local):
    return [pltpu.SemaphoreType.DMA((n_remote,)), pltpu.SemaphoreType.DMA((n_remote,)),
            pltpu.SemaphoreType.DMA((n_local,))]


def _gather_parts(x_refs, out_refs, send_sems, recv_sems, local_sems):
    n = len(x_refs)
    x, y, cc = lax.axis_index("x"), lax.axis_index("y"), lax.axis_index("c")
    me, sibling = (x, y, cc), (x, y, 1 - cc)
    chips = [(1 - x, y), (x, 1 - y), (1 - x, 1 - y)]

    def slot(a, px, py, pc):
        return out_refs[a].at[4 * px + 2 * py + pc]

    def copy(k, a, block, to, src=None):
        return pltpu.make_async_remote_copy(
            src_ref=slot(a, *block) if src is None else src, dst_ref=slot(a, *block),
            send_sem=send_sems.at[k * n + a], recv_sem=recv_sems.at[k * n + a], device_id=to, device_id_type=MESH)

    mine = [pltpu.make_async_copy(x_refs[a], slot(a, *me), local_sems.at[a]) for a in range(n)]
    first = [copy(0, a, me, sibling, src=x_refs[a]) for a in range(n)]
    first += [copy(1 + j, a, me, (*chip, cc), src=x_refs[a]) for j, chip in enumerate(chips) for a in range(n)]

    def begin():
        for cp in mine + first:
            cp.start()

    def end():
        passed = []
        for j, chip in enumerate(chips):
            for a in range(n):
                copy(1 + j, a, (*chip, cc), me).wait_recv()
                passed.append(copy(4 + j, a, (*chip, cc), sibling))
                passed[-1].start()
        for a in range(n):
            copy(0, a, sibling, me).wait_recv()
        for j, chip in enumerate(chips):
            for a in range(n):
                copy(4 + j, a, (*chip, 1 - cc), me).wait_recv()
        for cp in first + passed:
            cp.wait_send()
        for cp in mine:
            cp.wait()

    return begin, end


def _exchange_parts(g_refs, out_refs, b_refs, ball_refs, send_sems, recv_sems, local_sems):
    n, nb = len(g_refs), len(b_refs)
    per_peer = n + nb
    x, y, cc = lax.axis_index("x"), lax.axis_index("y"), lax.axis_index("c")
    my_idx = 4 * x + 2 * y + cc
    mine = [pltpu.make_async_copy(g_refs[a].at[my_idx], out_refs[a].at[my_idx], local_sems.at[a]) for a in range(n)]
    mine += [pltpu.make_async_copy(b_refs[a], ball_refs[a].at[my_idx], local_sems.at[n + a]) for a in range(nb)]
    copies = []
    for k in range(1, N_DEV):
        px = 1 - x if k & 4 else x
        py = 1 - y if k & 2 else y
        pc = 1 - cc if k & 1 else cc
        base = (k - 1) * per_peer
        for a in range(n):
            copies.append(pltpu.make_async_remote_copy(
                src_ref=g_refs[a].at[4 * px + 2 * py + pc], dst_ref=out_refs[a].at[my_idx],
                send_sem=send_sems.at[base + a], recv_sem=recv_sems.at[base + a],
                device_id=(px, py, pc), device_id_type=MESH))
        for a in range(nb):
            copies.append(pltpu.make_async_remote_copy(
                src_ref=b_refs[a], dst_ref=ball_refs[a].at[my_idx], send_sem=send_sems.at[base + n + a],
                recv_sem=recv_sems.at[base + n + a], device_id=(px, py, pc), device_id_type=MESH))

    def begin():
        for cp in mine + copies:
            cp.start()

    def end():
        for cp in copies:
            cp.wait_recv()
        for cp in copies:
            cp.wait_send()
        for cp in mine:
            cp.wait()

    return begin, end


def _all_gather(blks, name):
    n = len(blks)

    def body(*refs):
        begin, end = _gather_parts(refs[:n], refs[n:2 * n], *refs[2 * n:])
        begin()
        end()

    return _pcall(body, name=name, out_shape=[jax.ShapeDtypeStruct((N_DEV,) + b.shape, b.dtype) for b in blks],
                  in_specs=[_HBM] * n, out_specs=[_HBM] * n, scratch_shapes=_sem_shapes(7 * n, n))(*blks)


def _reduce_adamw(parts, w, m, v, name):
    r, c = w.shape
    tr = 128 if r % 128 == 0 else r

    def body(p_ref, w_ref, m_ref, v_ref, g_ref, d_ref, nm_ref, nv_ref):
        g = p_ref[0].astype(F32)
        for s in range(1, N_DEV):
            g = g + p_ref[s].astype(F32)
        nm = ADAM_B1 * m_ref[...] + (1.0 - ADAM_B1) * g
        nv = ADAM_B2 * v_ref[...] + (1.0 - ADAM_B2) * (g * g)
        m_hat = nm / (1.0 - ADAM_B1 ** ADAM_STEP)
        v_hat = nv / (1.0 - ADAM_B2 ** ADAM_STEP)
        g_ref[...] = g
        d_ref[...] = -ADAM_LR * (m_hat / (jnp.sqrt(v_hat) + ADAM_EPS) + ADAM_WD * w_ref[...])
        nm_ref[...] = nm
        nv_ref[...] = nv

    row = pl.BlockSpec((tr, c), lambda i: (i, 0))
    out = jax.ShapeDtypeStruct((r, c), F32)
    return _pcall(
        body, name=name, grid=(r // tr,),
        in_specs=[pl.BlockSpec((N_DEV, tr, c), lambda i: (0, i, 0)), row, row, row],
        out_specs=[row, row, row, row], out_shape=[out, out, out, out], compiler_params=_cp(1),
    )(parts, w, m, v)


def _proj_fwd(x, nw, w, wab, name, gather=()):
    t, npj = x.shape[0], w.shape[1]
    tm, tn = min(512, t), 1024
    has_ab = wab is not None
    n_in, n_out, n_g = 3 + has_ab, 2 + has_ab, len(gather)
    n_i = t // tm

    def body(*refs):
        ins, g_ins = refs[:n_in], refs[n_in:n_in + n_g]
        outs = refs[n_in + n_g:n_in + n_g + n_out]
        g_outs = refs[n_in + n_g + n_out:n_in + 2 * n_g + n_out]
        if has_ab:
            (x_ref, nw_ref, w_ref, wab_ref), (p_ref, h_ref, ab_ref) = ins, outs
        else:
            (x_ref, nw_ref, w_ref), (p_ref, h_ref) = ins, outs
        if n_g:
            begin, end = _gather_parts(g_ins, g_outs, *refs[n_in + 2 * n_g + n_out:])
            pl.when(pl.program_id(0) == 0)(begin)

        xv = x_ref[...]
        hv = (xv * lax.rsqrt(jnp.mean(xv * xv, axis=-1, keepdims=True) + EPS) * nw_ref[...]).astype(MXU)
        h_ref[...] = hv.astype(ACT)
        if has_ab:
            ab_ref[...] = _dot(hv, wab_ref[...], NN)
        for j in range(npj // tn):
            p_ref[:, j * tn:(j + 1) * tn] = _dot(hv, w_ref[:, j * tn:(j + 1) * tn], NN).astype(ACT)
        if n_g:
            pl.when(pl.program_id(0) == n_i - 1)(end)

    in_specs = [pl.BlockSpec((tm, D), lambda i: (i, 0)), _full((1, D)), _full((D, npj))]
    out_specs = [pl.BlockSpec((tm, npj), lambda i: (i, 0)), pl.BlockSpec((tm, D), lambda i: (i, 0))]
    out_shape = [jax.ShapeDtypeStruct((t, npj), ACT), jax.ShapeDtypeStruct((t, D), ACT)]
    args = [x, nw, w]
    if has_ab:
        in_specs.append(_full((D, HEAD)))
        out_specs.append(pl.BlockSpec((tm, HEAD), lambda i: (i, 0)))
        out_shape.append(jax.ShapeDtypeStruct((t, HEAD), F32))
        args.append(wab)
    scratch = []
    if n_g:
        in_specs += [_HBM] * n_g
        out_specs += [_HBM] * n_g
        out_shape += [jax.ShapeDtypeStruct((N_DEV,) + b.shape, b.dtype) for b in gather]
        args += list(gather)
        scratch += _sem_shapes(7 * n_g, n_g)
    return _pcall(body, name=name, grid=(n_i,), in_specs=in_specs, out_specs=out_specs,
                  out_shape=out_shape, scratch_shapes=scratch, compiler_params=_cp(1))(*args)


def _proj_bwd_x(dps, ws, x, dxn, nw, name, exchange=(), bcast=()):
    t = x.shape[0]
    tm = min(256, t)
    n, n_x, n_b = len(dps), len(exchange), len(bcast)
    n_c, n_steps = n_x + n_b, t // tm

    def body(*refs):
        dp_refs, w_refs = refs[:n], refs[n:2 * n]
        x_ref, dxn_ref, nw_ref = refs[2 * n:2 * n + 3]
        c_in = refs[2 * n + 3:2 * n + 3 + n_c]
        dx_ref, dnw_ref = refs[2 * n + 3 + n_c:2 * n + 5 + n_c]
        c_out = refs[2 * n + 5 + n_c:2 * n + 5 + 2 * n_c]
        if n_c:
            begin, end = _exchange_parts(c_in[:n_x], c_out[:n_x], c_in[n_x:], c_out[n_x:], *refs[2 * n + 5 + 2 * n_c:])
            pl.when(pl.program_id(0) == 0)(begin)
        dh = _mdot(dp_refs[0][...], w_refs[0][...], NT)
        for k in range(1, n):
            dh = dh + _mdot(dp_refs[k][...], w_refs[k][...], NT)
        xv = x_ref[...]
        r = lax.rsqrt(jnp.mean(xv * xv, axis=-1, keepdims=True) + EPS)

        @pl.when(pl.program_id(0) == 0)
        def _():
            dnw_ref[...] = jnp.zeros_like(dnw_ref)

        dnw_ref[...] += jnp.sum(dh * xv * r, axis=0, keepdims=True)
        dhw = dh * nw_ref[...]
        dx_ref[...] = dxn_ref[...] + r * (dhw - xv * (r * r) * jnp.mean(dhw * xv, axis=-1, keepdims=True))
        if n_c:
            pl.when(pl.program_id(0) == n_steps - 1)(end)

    row = pl.BlockSpec((tm, D), lambda i: (i, 0))
    in_specs = [pl.BlockSpec((tm, dp.shape[1]), lambda i: (i, 0)) for dp in dps]
    in_specs += [_full(w.shape) for w in ws]
    in_specs += [row, row, _full((1, D))] + [_HBM] * n_c
    out_shape = [jax.ShapeDtypeStruct((t, D), F32), jax.ShapeDtypeStruct((1, D), F32)]
    out_shape += [jax.ShapeDtypeStruct(e.shape, e.dtype) for e in exchange]
    out_shape += [jax.ShapeDtypeStruct((N_DEV,) + b.shape, b.dtype) for b in bcast]
    return _pcall(body, name=name, grid=(n_steps,), in_specs=in_specs, out_specs=[row, _full((1, D))] + [_HBM] * n_c,
                  out_shape=out_shape, scratch_shapes=_sem_shapes(7 * n_c, n_c) if n_c else [],
                  compiler_params=_cp(1))(*dps, *ws, x, dxn, nw, *exchange, *bcast)


def _matmul_tn(a, b, name):
    t, m = a.shape
    n = b.shape[1]
    tm, tn = min(512, t), min(1024, n)

    def body(a_ref, b_ref, o_ref):
        @pl.when(pl.program_id(1) == 0)
        def _():
            o_ref[...] = jnp.zeros_like(o_ref)

        o_ref[...] += _mdot(a_ref[...], b_ref[...], TN)

    return _pcall(body, name=name, grid=(n // tn, t // tm),
                  in_specs=[pl.BlockSpec((tm, m), lambda j, i: (i, 0)), pl.BlockSpec((tm, tn), lambda j, i: (i, j))],
                  out_specs=pl.BlockSpec((m, tn), lambda j, i: (0, j)),
                  out_shape=jax.ShapeDtypeStruct((m, n), F32), compiler_params=_cp(2))(a, b)


def _memkv_fwd(mem, mw, wkv):
    def body(mem_ref, mw_ref, w_ref, kv_ref):
        mv = mem_ref[...]
        mn = mv * lax.rsqrt(jnp.mean(mv * mv, axis=-1, keepdims=True) + EPS) * mw_ref[...]
        kv_ref[...] = _mdot(mn, w_ref[...], NN)

    return _pcall(body, name="memkv_fwd", out_shape=jax.ShapeDtypeStruct((N_MEM, 2 * D_XA), F32),
                  compiler_params=pltpu.CompilerParams(vmem_limit_bytes=VMEM_LIMIT))(mem, mw, wkv)


def _memkv_bwd(mem, mw, wkv, dkv):
    def body(mem_ref, mw_ref, w_ref, dkv_ref, dw_ref, dmw_ref):
        mv = mem_ref[...]
        r = lax.rsqrt(jnp.mean(mv * mv, axis=-1, keepdims=True) + EPS)
        mn = mv * r * mw_ref[...]
        dkvv = dkv_ref[...]
        dw_ref[...] = _mdot(mn, dkvv, TN)
        dmn = _mdot(dkvv, w_ref[...], NT)
        dmw_ref[...] = jnp.sum(dmn * mv * r, axis=0, keepdims=True)

    return _pcall(body, name="memkv_bwd",
                  out_shape=[jax.ShapeDtypeStruct((D, 2 * D_XA), F32), jax.ShapeDtypeStruct((1, D), F32)],
                  compiler_params=pltpu.CompilerParams(vmem_limit_bytes=VMEM_LIMIT))(mem, mw, wkv, dkv)


def _attend(q, kv):
    ps, outs = [], []
    for hh in range(XA_HEADS):
        qh = q[:, hh * HEAD:(hh + 1) * HEAD]
        kh = kv[:, hh * HEAD:(hh + 1) * HEAD]
        vh = kv[:, D_XA + hh * HEAD:D_XA + (hh + 1) * HEAD]
        s = _mdot(qh, kh, NT) * (HEAD ** -0.5)
        s = s - jnp.max(s, axis=-1, keepdims=True)
        e = jnp.exp(s)
        p = e / jnp.sum(e, axis=-1, keepdims=True)
        ps.append(p)
        outs.append(_mdot(p, vh, NN))
    return ps, outs


def _tail_fwd(p, ymix, x, kv, wout, npost, name):
    t = x.shape[0]
    tm = min(256, t)

    def body(z_ref, q_ref, y_ref, x_ref, kv_ref, w_ref, np_ref, xn_ref, o_ref):
        _, outs = _attend(q_ref[...].astype(F32), kv_ref[...])
        cat = jnp.concatenate([y_ref[...].astype(F32)] + outs, axis=1)
        g = cat * _silu(z_ref[...].astype(F32))
        o = _mdot(g, w_ref[...], NN)
        o_ref[...] = o
        xn_ref[...] = x_ref[...] + o * lax.rsqrt(jnp.mean(o * o, axis=-1, keepdims=True) + EPS) * np_ref[...]

    row = pl.BlockSpec((tm, D), lambda i: (i, 0))
    return _pcall(
        body, name=name, grid=(t // tm,),
        in_specs=[pl.BlockSpec((tm, D_CAT), lambda i: (i, 0)), pl.BlockSpec((tm, D_XA), lambda i: (i, D_CAT // D_XA)),
                  row, row, _full((N_MEM, 2 * D_XA)), _full((D_CAT, D)), _full((1, D))],
        out_specs=[row, row],
        out_shape=[jax.ShapeDtypeStruct((t, D), F32), jax.ShapeDtypeStruct((t, D), F32)], compiler_params=_cp(1),
    )(p, p, ymix, x, kv, wout, npost)


def _tail_bwd(dxn, o, p, ymix, kv, wout, npost, name, exchange=()):
    t = dxn.shape[0]
    tm = min(256, t)
    n_x, n_steps = len(exchange), t // tm

    def body(*refs):
        dxn_ref, o_ref, z_ref, q_ref, y_ref, kv_ref, w_ref, np_ref = refs[:8]
        dzq_ref, dy_ref, dw_ref, dnp_ref, dkv_ref = refs[8 + n_x:13 + n_x]
        if n_x:
            begin, end = _exchange_parts(refs[8:8 + n_x], refs[13 + n_x:13 + 2 * n_x], (), (), *refs[13 + 2 * n_x:])
            pl.when(pl.program_id(0) == 0)(begin)

        @pl.when(pl.program_id(0) == 0)
        def _():
            dw_ref[...] = jnp.zeros_like(dw_ref)
            dnp_ref[...] = jnp.zeros_like(dnp_ref)
            dkv_ref[...] = jnp.zeros_like(dkv_ref)

        q = q_ref[...].astype(F32)
        kvv = kv_ref[...]
        z = z_ref[...].astype(F32)
        ps, outs = _attend(q, kvv)
        cat = jnp.concatenate([y_ref[...].astype(F32)] + outs, axis=1)
        sz = _silu(z)
        g = cat * sz
        ov = o_ref[...]
        dr = dxn_ref[...]
        rr = lax.rsqrt(jnp.mean(ov * ov, axis=-1, keepdims=True) + EPS)
        dnp_ref[...] += jnp.sum(dr * ov * rr, axis=0, keepdims=True)
        dow = dr * np_ref[...]
        do = rr * (dow - ov * (rr * rr) * jnp.mean(dow * ov, axis=-1, keepdims=True))
        dg = _mdot(do, w_ref[...], NT)
        dw_ref[...] += _mdot(g, do, TN)
        dcat = dg * sz
        dzq_ref[:, 0:D_CAT] = (dg * cat * _dsilu(z)).astype(ACT)
        dy_ref[...] = dcat[:, 0:D].astype(ACT)
        for hh in range(XA_HEADS):
            lo = hh * HEAD
            doh = dcat[:, D + lo:D + lo + HEAD]
            qh = q[:, lo:lo + HEAD]
            kh = kvv[:, lo:lo + HEAD]
            vh = kvv[:, D_XA + lo:D_XA + lo + HEAD]
            ph = ps[hh]
            dp = _mdot(doh, vh, NT)
            ds = ph * (dp - jnp.sum(dp * ph, axis=-1, keepdims=True))
            dzq_ref[:, D_CAT + lo:D_CAT + lo + HEAD] = (_mdot(ds, kh, NN) * (HEAD ** -0.5)).astype(ACT)
            dkv_ref[:, lo:lo + HEAD] += _mdot(ds, qh, TN) * (HEAD ** -0.5)
            dkv_ref[:, D_XA + lo:D_XA + lo + HEAD] += _mdot(ph, doh, TN)
        if n_x:
            pl.when(pl.program_id(0) == n_steps - 1)(end)

    row = pl.BlockSpec((tm, D), lambda i: (i, 0))
    return _pcall(
        body, name=name, grid=(n_steps,),
        in_specs=[row, row, pl.BlockSpec((tm, D_CAT), lambda i: (i, 0)),
                  pl.BlockSpec((tm, D_XA), lambda i: (i, D_CAT // D_XA)), row,
                  _full((N_MEM, 2 * D_XA)), _full((D_CAT, D)), _full((1, D))] + [_HBM] * n_x,
        out_specs=[pl.BlockSpec((tm, ZQ), lambda i: (i, 0)), row, _full((D_CAT, D)), _full((1, D)),
                   _full((N_MEM, 2 * D_XA))] + [_HBM] * n_x,
        out_shape=[jax.ShapeDtypeStruct((t, ZQ), ACT), jax.ShapeDtypeStruct((t, D), ACT),
                   jax.ShapeDtypeStruct((D_CAT, D), F32), jax.ShapeDtypeStruct((1, D), F32),
                   jax.ShapeDtypeStruct((N_MEM, 2 * D_XA), F32)]
        + [jax.ShapeDtypeStruct(e.shape, e.dtype) for e in exchange],
        scratch_shapes=_sem_shapes(7 * n_x, n_x) if n_x else [], compiler_params=_cp(1),
    )(dxn, o, p, p, ymix, kv, wout, npost, *exchange)


def _gmlp_chunk(us, vs, lnws, lnbs, wss, bss):
    gv = [jax.nn.gelu(v) for v in vs]
    mean = sum(jnp.sum(v, axis=-1, keepdims=True) for v in gv) / D
    cen = [v - mean for v in gv]
    var = sum(jnp.sum(c * c, axis=-1, keepdims=True) for c in cen) / D
    rstd = lax.rsqrt(var + EPS)
    row = lax.broadcasted_iota(jnp.int32, (GM_CHUNK, GM_CHUNK), 0)
    col = lax.broadcasted_iota(jnp.int32, (GM_CHUNK, GM_CHUNK), 1)
    ys = []
    for g in range(GM_GROUPS):
        vn = cen[g] * rstd * lnws[g] + lnbs[g]
        sp = _nn(jnp.where(row >= col, wss[g], 0.0), vn) + bss[g]
        ys.append(jax.nn.gelu(us[g]) * sp)
    return ys


def _split_cols(v, n, width=HEAD):
    return [v[:, k * width:(k + 1) * width] for k in range(n)]


def _gmlp_operands(u_ref, v_ref, lnw_ref, lnb_ref, ws_ref, bst_ref):
    us = _split_cols(u_ref[...].astype(F32), GM_GROUPS)
    vs = _split_cols(v_ref[...].astype(F32), GM_GROUPS)
    lnws = _split_cols(lnw_ref[...], GM_GROUPS)
    lnbs = _split_cols(lnb_ref[...], GM_GROUPS)
    wss = [ws_ref[g] for g in range(GM_GROUPS)]
    bst = bst_ref[...]
    bss = [jnp.sum(bst * _onehot_row(HEAD, g), axis=1, keepdims=True) for g in range(GM_GROUPS)]
    return us, vs, lnws, lnbs, wss, bss


def _gmlp_specs():
    return [pl.BlockSpec((GM_CHUNK, D), lambda i: (i, ZQ // D)), pl.BlockSpec((GM_CHUNK, D), lambda i: (i, ZQ // D + 1)),
            _full((1, D)), _full((1, D)), _full((GM_GROUPS, GM_CHUNK, GM_CHUNK)), _full((GM_CHUNK, HEAD))]


def _gmlp_fwd(p, lnw, lnb, ws, bst, name):
    t = p.shape[0]

    def body(u_ref, v_ref, lnw_ref, lnb_ref, ws_ref, bst_ref, y_ref):
        ys = _gmlp_chunk(*_gmlp_operands(u_ref, v_ref, lnw_ref, lnb_ref, ws_ref, bst_ref))
        for g in range(GM_GROUPS):
            y_ref[:, g * HEAD:(g + 1) * HEAD] = ys[g].astype(ACT)

    return _pcall(body, name=name, grid=(t // GM_CHUNK,), in_specs=_gmlp_specs(),
                  out_specs=pl.BlockSpec((GM_CHUNK, D), lambda i: (i, 0)),
                  out_shape=jax.ShapeDtypeStruct((t, D), ACT), compiler_params=_cp(1))(p, p, lnw, lnb, ws, bst)


def _gmlp_bwd(p, dy, lnw, lnb, ws, bst, name):
    t = p.shape[0]

    def body(u_ref, v_ref, lnw_ref, lnb_ref, ws_ref, bst_ref, dy_ref, duv_ref, dlnw_ref, dlnb_ref, dws_ref, dbst_ref):
        @pl.when(pl.program_id(0) == 0)
        def _():
            dlnw_ref[...] = jnp.zeros_like(dlnw_ref)
            dlnb_ref[...] = jnp.zeros_like(dlnb_ref)
            dws_ref[...] = jnp.zeros_like(dws_ref)
            dbst_ref[...] = jnp.zeros_like(dbst_ref)

        ops = _gmlp_operands(u_ref, v_ref, lnw_ref, lnb_ref, ws_ref, bst_ref)
        _, vjp = jax.vjp(_gmlp_chunk, *ops)
        dus, dvs, dlnws, dlnbs, dwss, dbss = vjp(_split_cols(dy_ref[...].astype(F32), GM_GROUPS))
        dbst = jnp.zeros((GM_CHUNK, HEAD), F32)
        for g in range(GM_GROUPS):
            lo = g * HEAD
            duv_ref[:, lo:lo + HEAD] = dus[g].astype(ACT)
            duv_ref[:, D + lo:D + lo + HEAD] = dvs[g].astype(ACT)
            dlnw_ref[:, lo:lo + HEAD] += dlnws[g]
            dlnb_ref[:, lo:lo + HEAD] += dlnbs[g]
            dws_ref[g] += dwss[g]
            dbst = dbst + dbss[g] * _onehot_row(HEAD, g)
        dbst_ref[...] += dbst

    return _pcall(
        body, name=name, grid=(t // GM_CHUNK,),
        in_specs=_gmlp_specs() + [pl.BlockSpec((GM_CHUNK, D), lambda i: (i, 0))],
        out_specs=[pl.BlockSpec((GM_CHUNK, 2 * D), lambda i: (i, 0)), _full((1, D)), _full((1, D)),
                   _full((GM_GROUPS, GM_CHUNK, GM_CHUNK)), _full((GM_CHUNK, HEAD))],
        out_shape=[jax.ShapeDtypeStruct((t, 2 * D), ACT), jax.ShapeDtypeStruct((1, D), F32),
                   jax.ShapeDtypeStruct((1, D), F32), jax.ShapeDtypeStruct((GM_GROUPS, GM_CHUNK, GM_CHUNK), F32),
                   jax.ShapeDtypeStruct((GM_CHUNK, HEAD), F32)],
        compiler_params=_cp(1),
    )(p, p, lnw, lnb, ws, bst, dy)


def _prev_halo(tm, col):
    return pl.BlockSpec((HALO, D), lambda i: (jnp.maximum(i * (tm // HALO) - 1, 0), col))


def _next_halo(tm, col, n_tiles):
    return pl.BlockSpec((HALO, D), lambda i: (jnp.minimum(i + 1, n_tiles - 1) * (tm // HALO), col))


def _taps_back(ext, w, width):
    acc = None
    for k in range(width):
        s = width - 1 - k
        term = w[k:k + 1, :] * (pltpu.roll(ext, s, 0) if s else ext)[HALO:, :]
        acc = term if acc is None else acc + term
    return acc


def _taps_fwd(ext, w, width, n):
    rows = ext.shape[0]
    acc = None
    for k in range(width):
        s = width - 1 - k
        term = w[k:k + 1, :] * (pltpu.roll(ext, rows - s, 0) if s else ext)[0:n, :]
        acc = term if acc is None else acc + term
    return acc


def _sconv_fwd(p, cw, name):
    t = p.shape[0]
    tm = min(256, t)

    def body(b_ref, c_ref, h_ref, cp_ref, hp_ref, w_ref, y_ref):
        first = pl.program_id(0) == 0
        prev = jnp.where(first, 0.0, cp_ref[...].astype(F32) * hp_ref[...].astype(F32))
        ext = jnp.concatenate([prev, c_ref[...].astype(F32) * h_ref[...].astype(F32)], axis=0)
        y_ref[...] = (b_ref[...].astype(F32) * _taps_back(ext, w_ref[...], 3)).astype(ACT)

    c0 = ZQ // D
    tile = [pl.BlockSpec((tm, D), lambda i, c=c: (i, c)) for c in (c0, c0 + 1, c0 + 2)]
    return _pcall(body, name=name, grid=(t // tm,),
                  in_specs=tile + [_prev_halo(tm, c0 + 1), _prev_halo(tm, c0 + 2), _full((HALO, D))],
                  out_specs=pl.BlockSpec((tm, D), lambda i: (i, 0)),
                  out_shape=jax.ShapeDtypeStruct((t, D), ACT), compiler_params=_cp(1))(p, p, p, p, p, cw)


def _sconv_bwd(p, dy, cw, name):
    t = p.shape[0]
    tm = min(256, t)
    n_tiles = t // tm

    def body(b_ref, c_ref, h_ref, cp_ref, hp_ref, bn_ref, dy_ref, dyn_ref, w_ref, d_ref, dw_ref):
        i = pl.program_id(0)

        @pl.when(i == 0)
        def _():
            dw_ref[...] = jnp.zeros_like(dw_ref)

        w = w_ref[...]
        bv, cv, hv = b_ref[...].astype(F32), c_ref[...].astype(F32), h_ref[...].astype(F32)
        dyv = dy_ref[...].astype(F32)
        prev = jnp.where(i == 0, 0.0, cp_ref[...].astype(F32) * hp_ref[...].astype(F32))
        ext = jnp.concatenate([prev, cv * hv], axis=0)
        conv = _taps_back(ext, w, 3)
        dconv = dyv * bv
        nxt = jnp.where(i == n_tiles - 1, 0.0, dyn_ref[...].astype(F32) * bn_ref[...].astype(F32))
        dc = _taps_fwd(jnp.concatenate([dconv, nxt], axis=0), w, 3, tm)
        d_ref[:, 0:D] = (dyv * conv).astype(ACT)
        d_ref[:, D:2 * D] = (dc * hv).astype(ACT)
        d_ref[:, 2 * D:3 * D] = (dc * cv).astype(ACT)
        for k in range(3):
            s = 2 - k
            shifted = (pltpu.roll(ext, s, 0) if s else ext)[HALO:, :]
            dw_ref[k:k + 1, :] += jnp.sum(dconv * shifted, axis=0, keepdims=True)

    c0 = ZQ // D
    tile = [pl.BlockSpec((tm, D), lambda i, c=c: (i, c)) for c in (c0, c0 + 1, c0 + 2)]
    return _pcall(
        body, name=name, grid=(n_tiles,),
        in_specs=tile + [_prev_halo(tm, c0 + 1), _prev_halo(tm, c0 + 2), _next_halo(tm, c0, n_tiles),
                         pl.BlockSpec((tm, D), lambda i: (i, 0)), _next_halo(tm, 0, n_tiles), _full((HALO, D))],
        out_specs=[pl.BlockSpec((tm, 3 * D), lambda i: (i, 0)), _full((HALO, D))],
        out_shape=[jax.ShapeDtypeStruct((t, 3 * D), ACT), jax.ShapeDtypeStruct((HALO, D), F32)],
        compiler_params=_cp(1),
    )(p, p, p, p, p, p, dy, dy, cw)


def _l2_heads(s, scale):
    outs, rs = [], []
    for hh in range(DN_HEADS):
        blk = s[:, hh * HEAD:(hh + 1) * HEAD]
        r = lax.rsqrt(jnp.sum(blk * blk, axis=-1, keepdims=True) + EPS)
        outs.append(blk * (r * scale))
        rs.append(r)
    return outs, rs


_QKV_SCALE = (HEAD ** -0.5, 1.0, None)


def _qkv_fwd(p, cw, name):
    t = p.shape[0]
    tm = min(256, t)

    def body(q_ref, k_ref, v_ref, qp_ref, kp_ref, vp_ref, w_ref, o_ref):
        first = pl.program_id(0) == 0
        for part, (ref, pref) in enumerate(((q_ref, qp_ref), (k_ref, kp_ref), (v_ref, vp_ref))):
            prev = jnp.where(first, 0.0, pref[...].astype(F32))
            ext = jnp.concatenate([prev, ref[...].astype(F32)], axis=0)
            s = _silu(_taps_back(ext, w_ref[:, part * D:(part + 1) * D], 4))
            if _QKV_SCALE[part] is None:
                o_ref[:, part * D:(part + 1) * D] = s.astype(ACT)
            else:
                outs, _ = _l2_heads(s, _QKV_SCALE[part])
                for hh in range(DN_HEADS):
                    o_ref[:, part * D + hh * HEAD:part * D + (hh + 1) * HEAD] = outs[hh].astype(ACT)

    c0 = ZQ // D
    tile = [pl.BlockSpec((tm, D), lambda i, c=c: (i, c)) for c in (c0, c0 + 1, c0 + 2)]
    halo = [_prev_halo(tm, c) for c in (c0, c0 + 1, c0 + 2)]
    return _pcall(body, name=name, grid=(t // tm,), in_specs=tile + halo + [_full((HALO, 3 * D))],
                  out_specs=pl.BlockSpec((tm, 3 * D), lambda i: (i, 0)),
                  out_shape=jax.ShapeDtypeStruct((t, 3 * D), ACT), compiler_params=_cp(1))(p, p, p, p, p, p, cw)


def _qkv_bwd(p, dqkv, cw, name):
    t = p.shape[0]
    tm = min(256, t)
    n_tiles = t // tm

    def body(*refs):
        tiles, prevs, nexts = refs[0:3], refs[3:6], refs[6:9]
        d_tiles, d_nexts = refs[9:12], refs[12:15]
        w_ref, o_ref, dw_ref = refs[15:]
        i = pl.program_id(0)

        @pl.when(i == 0)
        def _():
            dw_ref[...] = jnp.zeros_like(dw_ref)

        for part in range(3):
            w = w_ref[:, part * D:(part + 1) * D]
            prev = jnp.where(i == 0, 0.0, prevs[part][...].astype(F32))
            ext = jnp.concatenate([prev, tiles[part][...].astype(F32), nexts[part][...].astype(F32)], axis=0)
            xc = _taps_back(ext, w, 4)
            dout = jnp.concatenate([d_tiles[part][...].astype(F32), d_nexts[part][...].astype(F32)], axis=0)
            s = _silu(xc)
            if _QKV_SCALE[part] is None:
                ds = dout
            else:
                scale = _QKV_SCALE[part]
                pieces = []
                for hh in range(DN_HEADS):
                    blk = s[:, hh * HEAD:(hh + 1) * HEAD]
                    dblk = dout[:, hh * HEAD:(hh + 1) * HEAD]
                    r = lax.rsqrt(jnp.sum(blk * blk, axis=-1, keepdims=True) + EPS)
                    pieces.append(scale * r * (dblk - blk * (r * r) * jnp.sum(dblk * blk, axis=-1, keepdims=True)))
                ds = jnp.concatenate(pieces, axis=1)
            dxc = ds * _dsilu(xc)
            row = lax.broadcasted_iota(jnp.int32, (tm + HALO, 1), 0)
            dxc = jnp.where(jnp.logical_and(i == n_tiles - 1, row >= tm), 0.0, dxc)
            o_ref[:, part * D:(part + 1) * D] = _taps_fwd(dxc, w, 4, tm).astype(ACT)
            for k in range(4):
                sh = 3 - k
                shifted = (pltpu.roll(ext, sh, 0) if sh else ext)[HALO:HALO + tm, :]
                dw_ref[k:k + 1, part * D:(part + 1) * D] += jnp.sum(dxc[0:tm, :] * shifted, axis=0, keepdims=True)

    c0 = ZQ // D
    cols = (c0, c0 + 1, c0 + 2)
    tile = [pl.BlockSpec((tm, D), lambda i, c=c: (i, c)) for c in cols]
    dtile = [pl.BlockSpec((tm, D), lambda i, c=c: (i, c)) for c in range(3)]
    in_specs = (tile + [_prev_halo(tm, c) for c in cols] + [_next_halo(tm, c, n_tiles) for c in cols]
                + dtile + [_next_halo(tm, c, n_tiles) for c in range(3)] + [_full((HALO, 3 * D))])
    return _pcall(
        body, name=name, grid=(n_tiles,), in_specs=in_specs,
        out_specs=[pl.BlockSpec((tm, 3 * D), lambda i: (i, 0)), _full((HALO, 3 * D))],
        out_shape=[jax.ShapeDtypeStruct((t, 3 * D), ACT), jax.ShapeDtypeStruct((HALO, 3 * D), F32)],
        compiler_params=_cp(1),
    )(*([p] * 9), *([dqkv] * 6), cw)


def _tri_masks(n):
    row = lax.broadcasted_iota(jnp.int32, (n, n), 0)
    col = lax.broadcasted_iota(jnp.int32, (n, n), 1)
    return row, col


@jax.custom_vjp
def _unit_lower_inverses(mats):
    n = DN_CHUNK
    row, col = _tri_masks(n)
    eye = (row == col).astype(F32)
    same16 = (row // 16) == (col // 16)
    same32 = (row // 32) == (col // 32)
    pw = [jnp.where(same16, a, 0.0) for a in mats]
    x = [eye - p for p in pw]
    for _ in range(3):
        pw = [_hnn(p, p) for p in pw]
        x = [_hnn(xi, eye + p) for xi, p in zip(x, pw)]
    for keep in (jnp.logical_and(same32, jnp.logical_not(same16)), jnp.logical_not(same32)):
        inner = [_hnn(jnp.where(keep, a, 0.0), xi) for a, xi in zip(mats, x)]
        x = [xi - _hnn(xi, y) for xi, y in zip(x, inner)]
    return tuple(x)


def _uli_fwd(mats):
    t = _unit_lower_inverses(mats)
    return t, t


def _uli_bwd(ts, gs):
    inner = [_hnt(g, t) for g, t in zip(gs, ts)]
    return (tuple(-_htn(t, y) for t, y in zip(ts, inner)),)


_unit_lower_inverses.defvjp(_uli_fwd, _uli_bwd)


@jax.custom_vjp
def _known_inverses(mats, ts):
    return ts


_known_inverses.defvjp(lambda mats, ts: (ts, ts),
                       lambda ts, gs: (_uli_bwd(ts, gs)[0], tuple(jnp.zeros_like(t) for t in ts)))


def _pick_col(m, k):
    return jnp.sum(m * _onehot_row(m.shape[1], k), axis=1, keepdims=True)


def _pick_row(m, k):
    hot = (lax.broadcasted_iota(jnp.int32, (m.shape[0], 1), 0) == k).astype(F32)
    return jnp.sum(m * hot, axis=0, keepdims=True)


def _delta_chunk(states, qs, ks, vs, ab, alog, dtb, onw, known_inverses=None):
    n = DN_CHUNK
    heads = range(DN_HEADS)
    row, col = _tri_masks(n)
    incl = row >= col
    lane = lax.broadcasted_iota(jnp.int32, (1, HEAD), 1)
    g_all = jnp.where(lane < DN_HEADS, -jnp.exp(alog) * jax.nn.softplus(ab + dtb), 0.0)
    c_cols = _hnn(incl.astype(F32), g_all)
    c_rows = _htn(g_all, (row <= col).astype(F32))
    g_tot = jnp.sum(g_all, axis=0, keepdims=True)
    beta_all = jax.nn.sigmoid(ab)
    ccol = [_pick_col(c_cols, h) for h in heads]
    crow = [_pick_row(c_rows, h) for h in heads]
    gl = [_pick_col(g_tot, h) for h in heads]
    beta = [_pick_col(beta_all, DN_HEADS + h) for h in heads]
    decay = [jnp.exp(jnp.where(incl, ccol[h] - crow[h], -1e30)) for h in heads]
    eg = [jnp.exp(ccol[h]) for h in heads]
    kb = [ks[h] * beta[h] for h in heads]
    amat = [jnp.where(row > col, _nt(kb[h], ks[h]) * decay[h], 0.0) for h in heads]
    if known_inverses is None:
        tmat = _unit_lower_inverses(tuple(amat))
    else:
        tmat = _known_inverses(tuple(amat), tuple(known_inverses))
    u = [_nn(tmat[h], vs[h] * beta[h]) for h in heads]
    w = [_nn(tmat[h], kb[h] * eg[h]) for h in heads]
    qk = [_nt(qs[h], ks[h]) * decay[h] for h in heads]
    v_new = [u[h] - _nn(w[h], states[h]) for h in heads]
    o = [_nn(qs[h] * eg[h], states[h]) + _nn(qk[h], v_new[h]) for h in heads]
    new_states = [states[h] * jnp.exp(gl[h]) + _tn(ks[h] * jnp.exp(gl[h] - ccol[h]), v_new[h]) for h in heads]
    ys = [o[h] * lax.rsqrt(jnp.mean(o[h] * o[h], axis=-1, keepdims=True) + EPS) * onw for h in heads]
    return (ys, new_states), tmat


def _head_cols(ref):
    return [ref[:, h * HEAD:(h + 1) * HEAD].astype(F32) for h in range(DN_HEADS)]


def _delta_fwd(qkv, ab, alog, dtb, onw, name):
    t = qkv.shape[0]
    nc = t // DN_CHUNK

    def body(q_ref, k_ref, v_ref, ab_ref, alog_ref, dtb_ref, onw_ref, y_ref, keep_ref, inv_ref, state):
        @pl.when(pl.program_id(0) == 0)
        def _():
            state[...] = jnp.zeros_like(state)

        s0 = [state[hh] for hh in range(DN_HEADS)]
        (ys, s1), tmat = _delta_chunk(s0, _head_cols(q_ref), _head_cols(k_ref), _head_cols(v_ref), ab_ref[...],
                                      alog_ref[...], dtb_ref[...], onw_ref[...])
        for hh in range(DN_HEADS):
            keep_ref[0, hh] = s0[hh]
            inv_ref[0, hh] = tmat[hh]
            state[hh] = s1[hh]
            y_ref[:, hh * HEAD:(hh + 1) * HEAD] = ys[hh].astype(ACT)

    chunk = [pl.BlockSpec((DN_CHUNK, D), lambda i, c=c: (i, c)) for c in range(3)]
    return _pcall(
        body, name=name, grid=(nc,),
        in_specs=chunk + [pl.BlockSpec((DN_CHUNK, HEAD), lambda i: (i, 0)), _full((1, HEAD)), _full((1, HEAD)),
                          _full((1, HEAD))],
        out_specs=[pl.BlockSpec((DN_CHUNK, D), lambda i: (i, 0)),
                   pl.BlockSpec((1, DN_HEADS, HEAD, HEAD), lambda i: (i, 0, 0, 0)),
                   pl.BlockSpec((1, DN_HEADS, DN_CHUNK, DN_CHUNK), lambda i: (i, 0, 0, 0))],
        out_shape=[jax.ShapeDtypeStruct((t, D), ACT), jax.ShapeDtypeStruct((nc, DN_HEADS, HEAD, HEAD), F32),
                   jax.ShapeDtypeStruct((nc, DN_HEADS, DN_CHUNK, DN_CHUNK), F32)],
        scratch_shapes=[pltpu.VMEM((DN_HEADS, HEAD, HEAD), F32)], compiler_params=_cp(1),
    )(qkv, qkv, qkv, ab, alog, dtb, onw)


def _delta_bwd(qkv, ab, alog, dtb, onw, keep, inv, dy, name):
    t = qkv.shape[0]
    nc = t // DN_CHUNK

    def body(q_ref, k_ref, v_ref, ab_ref, alog_ref, dtb_ref, onw_ref, keep_ref, inv_ref, dy_ref,
             dqkv_ref, dab_ref, dalog_ref, ddtb_ref, donw_ref, dstate):
        @pl.when(pl.program_id(0) == 0)
        def _():
            dstate[...] = jnp.zeros_like(dstate)
            dalog_ref[...] = jnp.zeros_like(dalog_ref)
            ddtb_ref[...] = jnp.zeros_like(ddtb_ref)
            donw_ref[...] = jnp.zeros_like(donw_ref)

        s0 = [keep_ref[0, hh] for hh in range(DN_HEADS)]
        known = [inv_ref[0, hh] for hh in range(DN_HEADS)]
        _, vjp, _ = jax.vjp(functools.partial(_delta_chunk, known_inverses=known), s0, _head_cols(q_ref),
                            _head_cols(k_ref), _head_cols(v_ref), ab_ref[...], alog_ref[...], dtb_ref[...], onw_ref[...],
                            has_aux=True)
        ds, dq, dk, dv, dab, dal, ddt, don = vjp((_head_cols(dy_ref), [dstate[hh] for hh in range(DN_HEADS)]))
        for hh in range(DN_HEADS):
            lo = hh * HEAD
            dstate[hh] = ds[hh]
            dqkv_ref[:, lo:lo + HEAD] = dq[hh].astype(ACT)
            dqkv_ref[:, D + lo:D + lo + HEAD] = dk[hh].astype(ACT)
            dqkv_ref[:, 2 * D + lo:2 * D + lo + HEAD] = dv[hh].astype(ACT)
        dab_ref[...] = dab
        dalog_ref[...] += dal
        ddtb_ref[...] += ddt
        donw_ref[...] += don

    rev = lambda i: nc - 1 - i
    chunk = [pl.BlockSpec((DN_CHUNK, D), lambda i, c=c: (rev(i), c)) for c in range(3)]
    small = jax.ShapeDtypeStruct((1, HEAD), F32)
    return _pcall(
        body, name=name, grid=(nc,),
        in_specs=chunk + [pl.BlockSpec((DN_CHUNK, HEAD), lambda i: (rev(i), 0)), _full((1, HEAD)), _full((1, HEAD)),
                          _full((1, HEAD)), pl.BlockSpec((1, DN_HEADS, HEAD, HEAD), lambda i: (rev(i), 0, 0, 0)),
                          pl.BlockSpec((1, DN_HEADS, DN_CHUNK, DN_CHUNK), lambda i: (rev(i), 0, 0, 0)),
                          pl.BlockSpec((DN_CHUNK, D), lambda i: (rev(i), 0))],
        out_specs=[pl.BlockSpec((DN_CHUNK, 3 * D), lambda i: (rev(i), 0)),
                   pl.BlockSpec((DN_CHUNK, HEAD), lambda i: (rev(i), 0)), _full((1, HEAD)), _full((1, HEAD)),
                   _full((1, HEAD))],
        out_shape=[jax.ShapeDtypeStruct((t, 3 * D), ACT), jax.ShapeDtypeStruct((t, HEAD), F32), small, small, small],
        scratch_shapes=[pltpu.VMEM((DN_HEADS, HEAD, HEAD), F32)], compiler_params=_cp(1),
    )(qkv, qkv, qkv, ab, alog, dtb, onw, keep, inv, dy)


def _loss_head(y, target):
    t = y.shape[0]
    tm = min(512, t)

    def body(y_ref, t_ref, l_ref, dy_ref):
        @pl.when(pl.program_id(0) == 0)
        def _():
            l_ref[...] = jnp.zeros_like(l_ref)

        diff = y_ref[...] - t_ref[...]
        dy_ref[...] = diff * (1.0 / D)
        l_ref[...] += 0.5 * jnp.sum(jnp.sum(diff * diff, axis=-1, keepdims=True) * (1.0 / D), axis=0, keepdims=True)

    row = pl.BlockSpec((tm, D), lambda i: (i, 0))
    return _pcall(body, name="loss_head", grid=(t // tm,), in_specs=[row, row], out_specs=[_full((8, 128)), row],
                  out_shape=[jax.ShapeDtypeStruct((8, 128), F32), jax.ShapeDtypeStruct((t, D), F32)],
                  compiler_params=_cp(1))(y, target)


_SMALL_SHARDED = (("a_ln_w", (2, 128), 1), ("a_ln_b", (2, 128), 1), ("b_conv_w", (1, 3, 128), 2),
                  ("c_conv_w", (1, 4, 384), 2))
_REPLICATED = (("mem_norm_w", (1024,)), ("norm_pre", (4, 1024)), ("norm_post", (4, 1024)),
               ("a_b_s", (2, 8, 128)), ("c_a_log", (1, 8)), ("c_dt_bias", (1, 8)), ("c_o_norm_w", (1, 128)))


def _layer_shards(given, prefix):
    w_out = given[prefix + "w_out"]
    w_ins = [given[prefix + "a_w_in"][0], given[prefix + "b_w_in"][0], given[prefix + "c_w_in"][0],
             given[prefix + "a_w_in"][1]]
    rows = [jnp.concatenate([given[prefix + "w_mem_kv"], w_out[0]])] + [w_out[i] for i in (1, 2, 3)]
    return [[rows[i], w_ins[i]] for i in range(4)]


def _pack_rows(flat_parts, rows, lead=()):
    n_lead = len(lead)
    flat = jnp.concatenate([a.reshape(lead + (-1,)) for a in flat_parts], axis=n_lead)
    pad = rows * D - flat.shape[n_lead]
    flat = jnp.pad(flat, [(0, 0)] * n_lead + [(0, pad)])
    return flat.reshape(lead + (rows, D))


def _unpack_rows(packed, table, lead=()):
    flat = packed.reshape(lead + (-1,))
    out, at = {}, 0
    for name, shape in table:
        size = math.prod(shape)
        out[name] = lax.slice_in_dim(flat, at, at + size, axis=len(lead)).reshape(lead + shape)
        at += size
    return out


def _join_shards(blocks, axis):
    moved = jnp.moveaxis(blocks, 0, axis)
    shape = moved.shape
    return moved.reshape(shape[:axis] + (shape[axis] * shape[axis + 1],) + shape[axis + 2:])


def _split_shards(full, axis):
    shape = full.shape
    split = full.reshape(shape[:axis] + (N_DEV, shape[axis] // N_DEV) + shape[axis + 1:])
    return jnp.moveaxis(split, axis, 0)


_A_COLS = ((2560, 1536), (2048, 512), (0, 2048))
_BC_COLS = ((3584, 1536), (3072, 512), (0, 3072))
_C_COLS = ((3600, 1536), (3088, 512), (0, 3072))


def _reorder_cols(w, cols):
    return jnp.concatenate([w[:, s:s + n] for s, n in cols], axis=1)


def _restore_cols(pieces_in_my_order, cols, extra=()):
    placed = sorted(list(zip([s for s, _ in cols], pieces_in_my_order)) + list(extra), key=lambda sp: sp[0])
    return jnp.concatenate([piece for _, piece in placed], axis=1)


def kernel(x, mem, mem_norm_w, w_mem_kv, norm_pre, norm_post, w_out, a_w_in, a_ln_w, a_ln_b, a_w_s, a_b_s, b_w_in, b_conv_w, c_w_in, c_conv_w, c_a_log, c_dt_bias, c_o_norm_w, loss_target, m_mem_norm_w, m_w_mem_kv, m_norm_pre, m_norm_post, m_w_out, m_a_w_in, m_a_ln_w, m_a_ln_b, m_a_w_s, m_a_b_s, m_b_w_in, m_b_conv_w, m_c_w_in, m_c_conv_w, m_c_a_log, m_c_dt_bias, m_c_o_norm_w, v_mem_norm_w, v_w_mem_kv, v_norm_pre, v_norm_post, v_w_out, v_a_w_in, v_a_ln_w, v_a_ln_b, v_a_w_s, v_a_b_s, v_b_w_in, v_b_conv_w, v_c_w_in, v_c_conv_w, v_c_a_log, v_c_dt_bias, v_c_o_norm_w):
    given = dict(locals())
    x0 = x[0]
    mem0 = mem[0]
    target = loss_target[0]

    w_sh, m_sh, v_sh = (_layer_shards(given, pre) for pre in ("", "m_", "v_"))
    small_tab = tuple((n, s) for n, s, _ in _SMALL_SHARDED)
    small = _pack_rows([given[n] for n, _, _ in _SMALL_SHARDED], HALO)
    g_rows0, g_in0, g_small = _all_gather([w.astype(MXU) for w in w_sh[0]] + [small], "gather_weights")
    small_full = _unpack_rows(g_small, small_tab, lead=(N_DEV,))
    full = {n: _join_shards(small_full[n], ax) for n, _, ax in _SMALL_SHARDED}
    wkv = g_rows0[:, :128].reshape(D, 2 * D_XA)
    wouts = [g_rows0[:, 128:].reshape(D_CAT, D)]
    w_in = [_reorder_cols(_join_shards(g_in0, 1), _A_COLS)]
    w_cab = None
    lnw = [full["a_ln_w"][j][None, :] for j in range(2)]
    lnb = [full["a_ln_b"][j][None, :] for j in range(2)]
    ws = [a_w_s[j] for j in range(2)]
    bst = [jnp.pad(a_b_s[j].T, ((0, 0), (0, HEAD - GM_GROUPS))) for j in range(2)]
    cw_b = jnp.pad(full["b_conv_w"][0], ((0, HALO - 3), (0, 0)))
    cw_c = jnp.pad(full["c_conv_w"][0], ((0, HALO - 4), (0, 0)))
    alog = jnp.pad(c_a_log, ((0, 0), (0, HEAD - DN_HEADS)))
    dtb = jnp.pad(c_dt_bias, ((0, 0), (0, HEAD - DN_HEADS)))
    onw = c_o_norm_w
    mw = mem_norm_w[None, :]

    kv = _memkv_fwd(mem0, mw, wkv)
    xs, saved = [x0], []
    for i in range(4):
        kind = i % 3
        npre, npost = norm_pre[i][None, :], norm_post[i][None, :]
        ahead = [w.astype(MXU) for w in w_sh[i + 1]] if i < 3 else []
        res = _proj_fwd(xs[i], npre, w_in[i], w_cab if kind == 2 else None, f"proj_fwd_{i}", gather=ahead)
        if ahead:
            g_wout, g_in = res[-2:]
            wouts.append(g_wout.reshape(D_CAT, D))
            if i + 1 == 2:
                c_full = jnp.concatenate([g_in[d] for d in range(N_DEV)], axis=1)
                w_in.append(_reorder_cols(c_full, _C_COLS))
                w_cab = jnp.pad(c_full[:, 3072:3088], ((0, 0), (0, HEAD - 16)))
            else:
                w_in.append(_reorder_cols(_join_shards(g_in, 1), _BC_COLS if i + 1 == 1 else _A_COLS))
        if kind == 2:
            p, h, ab = res[:3]
            qkv = _qkv_fwd(p, cw_c, f"qkv_fwd_{i}")
            ymix, keep, inv = _delta_fwd(qkv, ab, alog, dtb, onw, f"delta_fwd_{i}")
            extra = (qkv, ab, keep, inv)
        else:
            p, h = res[:2]
            if kind == 0:
                ymix = _gmlp_fwd(p, lnw[i // 3], lnb[i // 3], ws[i // 3], bst[i // 3], f"gmlp_fwd_{i}")
            else:
                ymix = _sconv_fwd(p, cw_b, f"sconv_fwd_{i}")
            extra = ()
        xn, o = _tail_fwd(p, ymix, xs[i], kv, wouts[i], npost, f"tail_fwd_{i}")
        xs.append(xn)
        saved.append((p, h, ymix, o, extra))

    loss_tile, dx = _loss_head(xs[4], target)
    loss = lax.psum(loss_tile[0, 0], ("x", "y", "c"))

    g = {}
    d_npre, d_npost = [None] * 4, [None] * 4
    d_ws, d_bs, d_lnw, d_lnb = [None] * 2, [None] * 2, [None] * 2, [None] * 2
    dkv = None
    pending, landed = [], [None] * 4
    for i in reversed(range(4)):
        kind = i % 3
        p, h, ymix, o, extra = saved[i]
        npre, npost = norm_pre[i][None, :], norm_post[i][None, :]
        res = _tail_bwd(dx, o, p, ymix, kv, wouts[i], npost, f"tail_bwd_{i}", exchange=pending)
        dzq, dymix, d_wout, d_npost[i], dkv_i = res[:5]
        if pending:
            landed[i + 1] = res[5:]
        dkv = dkv_i if dkv is None else dkv + dkv_i
        w_zq, w_mix = w_in[i][:, :ZQ], w_in[i][:, ZQ:]
        dw_zq = _matmul_tn(h, dzq, f"dw_zq_{i}")
        if kind == 0:
            j = i // 3
            dmix, d_lnw[j], d_lnb[j], d_ws[j], dbst = _gmlp_bwd(p, dymix, lnw[j], lnb[j], ws[j], bst[j], f"gmlp_bwd_{i}")
            d_bs[j] = dbst[:, :GM_GROUPS].T
            dw_mix = _matmul_tn(h, dmix, f"dw_mix_{i}")
            d_win = _restore_cols([dw_zq[:, :D_CAT], dw_zq[:, D_CAT:], dw_mix], _A_COLS)
            dps, wparts = [dzq, dmix], [w_zq, w_mix]
        elif kind == 1:
            dmix, dcw = _sconv_bwd(p, dymix, cw_b, f"sconv_bwd_{i}")
            g["b_conv_w"] = dcw[None, :3]
            dw_mix = _matmul_tn(h, dmix, f"dw_mix_{i}")
            d_win = _restore_cols([dw_zq[:, :D_CAT], dw_zq[:, D_CAT:], dw_mix], _BC_COLS)
            dps, wparts = [dzq, dmix], [w_zq, w_mix]
        else:
            qkv, ab, keep, inv = extra
            dqkv, dab, dalog, ddtb, donw = _delta_bwd(qkv, ab, alog, dtb, onw, keep, inv, dymix, f"delta_bwd_{i}")
            dmix, dcw = _qkv_bwd(p, dqkv, cw_c, f"qkv_bwd_{i}")
            g["c_conv_w"] = dcw[None, :4]
            g["c_a_log"], g["c_dt_bias"], g["c_o_norm_w"] = dalog[:, :DN_HEADS], ddtb[:, :DN_HEADS], donw
            dw_mix = _matmul_tn(h, dmix, f"dw_mix_{i}")
            dw_ab = _matmul_tn(h, dab, f"dw_ab_{i}")
            d_win = _restore_cols([dw_zq[:, :D_CAT], dw_zq[:, D_CAT:], dw_mix], _C_COLS, extra=[(3072, dw_ab[:, :16])])
            dps, wparts = [dzq, dmix, dab], [w_zq, w_mix, w_cab]
        width = d_win.shape[1] // N_DEV
        pending = [d_wout.reshape(N_DEV, D_CAT // N_DEV, D).astype(XCH),
                   jnp.stack([d_win[:, d * width:(d + 1) * width] for d in range(N_DEV)]).astype(XCH)]
        if i > 0:
            dx, d_npre[i] = _proj_bwd_x(dps, wparts, xs[i], dx, npre, f"proj_bwd_x_{i}")

    d_wkv, d_mw = _memkv_bwd(mem0, mw, wkv, dkv)
    g["mem_norm_w"] = d_mw[0]
    g["norm_pre"] = jnp.concatenate([jnp.zeros((1, D), F32)] + d_npre[1:], axis=0)
    g["norm_post"] = jnp.concatenate(d_npost, axis=0)
    g["a_ln_w"] = jnp.concatenate(d_lnw, axis=0)
    g["a_ln_b"] = jnp.concatenate(d_lnb, axis=0)
    g["a_b_s"] = jnp.stack(d_bs)
    pending[0] = jnp.concatenate([d_wkv.reshape(N_DEV, 128, D).astype(XCH), pending[0]], axis=1)
    e_small = _pack_rows([_split_shards(g[n], ax) for n, _, ax in _SMALL_SHARDED], HALO, lead=(N_DEV,))
    d_ws_pack = jnp.stack(d_ws).reshape(2 * GM_GROUPS * GM_CHUNK * GM_CHUNK // D, D).astype(XCH)
    r_pack = _pack_rows([g[n] for n, _ in _REPLICATED], R_REPL)
    dx, d_npre0, l_rows, l_in, l_small, ws_all, r_all = _proj_bwd_x(
        dps, wparts, xs[0], dx, norm_pre[0][None, :], "proj_bwd_x_0", exchange=pending + [e_small],
        bcast=[d_ws_pack, r_pack])
    landed[0] = (l_rows, l_in)
    npre0_all, = _all_gather([jnp.pad(d_npre0, ((0, HALO - 1), (0, 0)))], "gather_norm_pre0")
    r_all = r_all.at[:, 1, :].set(npre0_all[:, 0, :])

    res = [[_reduce_adamw(landed[i][a], w_sh[i][a], m_sh[i][a], v_sh[i][a], f"adamw_{i}_{a}") for a in range(2)]
           for i in range(4)]
    res_small = _reduce_adamw(l_small, small, _pack_rows([given["m_" + n] for n, _, _ in _SMALL_SHARDED], HALO),
                              _pack_rows([given["v_" + n] for n, _, _ in _SMALL_SHARDED], HALO), "adamw_small")
    res_ws = _reduce_adamw(ws_all, *(given[pre + "a_w_s"].reshape(d_ws_pack.shape) for pre in ("", "m_", "v_")),
                           "adamw_a_w_s")
    rp = _reduce_adamw(r_all, _pack_rows([given[n] for n, _ in _REPLICATED], R_REPL),
                       _pack_rows([given["m_" + n] for n, _ in _REPLICATED], R_REPL),
                       _pack_rows([given["v_" + n] for n, _ in _REPLICATED], R_REPL), "adamw_replicated")

    order = ["mem_norm_w", "w_mem_kv", "norm_pre", "norm_post", "w_out", "a_w_in", "a_ln_w", "a_ln_b", "a_w_s", "a_b_s",
             "b_w_in", "b_conv_w", "c_w_in", "c_conv_w", "c_a_log", "c_dt_bias", "c_o_norm_w"]
    outs = [loss, dx[None]]
    for kind in range(4):
        got = dict(_unpack_rows(rp[kind], _REPLICATED))
        got["a_w_s"] = res_ws[kind].reshape(a_w_s.shape)
        got.update(_unpack_rows(res_small[kind], small_tab))
        got["w_mem_kv"] = res[0][0][kind][:128]
        got["w_out"] = jnp.stack([res[0][0][kind][128:]] + [res[i][0][kind] for i in (1, 2, 3)])
        got["a_w_in"] = jnp.stack([res[0][1][kind], res[3][1][kind]])
        got["b_w_in"] = res[1][1][kind][None]
        got["c_w_in"] = res[2][1][kind][None]
        outs += [got[n] for n in order]
    return tuple(outs)
```

```python
import functools
import math

import jax
import jax.numpy as jnp
from jax import lax
from jax.experimental import pallas as pl
from jax.experimental.pallas import tpu as pltpu

F32 = jnp.float32
MXU = jnp.bfloat16
ACT = jnp.bfloat16

D = 1024
D_XA = 512
D_CAT = 1536
N_MEM = 256
XA_HEADS = 4
HEAD = 128
ZQ = D_CAT + D_XA
EPS = 1e-6
GM_CHUNK = 128
GM_CHUNKS_PER_STEP = 4
GM_GROUPS = 8
DN_HEADS = 8
DN_CHUNK = 64
N_DEV = 8
HALO = 8
VMEM_LIMIT = 56 * 1024 * 1024
XCH = jnp.bfloat16
R_REPL = 32
R_SMALL = 16

ADAM_LR = 0.001
ADAM_B1 = 0.9
ADAM_B2 = 0.999
ADAM_EPS = 1e-08
ADAM_WD = 0.01
ADAM_STEP = 10

NN = ((1,), (0,))
NT = ((1,), (1,))
TN = ((0,), (0,))
MESH = pl.DeviceIdType.MESH


def _pcall(body, **kw):
    return pl.pallas_call(body, **kw)


def _cp(n_axes):
    return pltpu.CompilerParams(dimension_semantics=("arbitrary",) * n_axes, vmem_limit_bytes=VMEM_LIMIT)


def _dot(a, b, dims, prec=None):
    return lax.dot_general(a, b, (dims, ((), ())), preferred_element_type=F32, precision=prec)


def _mdot(a, b, dims):
    return _dot(a.astype(MXU), b.astype(MXU), dims)


def _make_mms(raw):
    @jax.custom_vjp
    def nn(a, b):
        return raw(a, b, NN)

    @jax.custom_vjp
    def nt(a, b):
        return raw(a, b, NT)

    @jax.custom_vjp
    def tn(a, b):
        return raw(a, b, TN)

    nn.defvjp(lambda a, b: (nn(a, b), (a, b)), lambda r, g: (nt(g, r[1]), tn(r[0], g)))
    nt.defvjp(lambda a, b: (nt(a, b), (a, b)), lambda r, g: (nn(g, r[1]), tn(g, r[0])))
    tn.defvjp(lambda a, b: (tn(a, b), (a, b)), lambda r, g: (nt(r[1], g), nn(r[0], g)))
    return nn, nt, tn


_nn, _nt, _tn = _make_mms(_mdot)


def _split_dot(a, b, dims):
    ah = a.astype(jnp.bfloat16)
    bh = b.astype(jnp.bfloat16)
    al = (a - ah.astype(F32)).astype(jnp.bfloat16)
    bl = (b - bh.astype(F32)).astype(jnp.bfloat16)
    return _dot(ah, bh, dims) + (_dot(ah, bl, dims) + _dot(al, bh, dims))


_hnn, _hnt, _htn = _make_mms(_split_dot)


def _full(shape):
    return pl.BlockSpec(shape, lambda *_: (0,) * len(shape))


def _silu(z):
    return z * jax.nn.sigmoid(z)


def _dsilu(z):
    s = jax.nn.sigmoid(z)
    return s * (1.0 + z * (1.0 - s))


def _onehot_row(n, k):
    return (lax.broadcasted_iota(jnp.int32, (1, n), 1) == k).astype(F32)


_HBM = pl.BlockSpec(memory_space=pl.ANY)


def _sem_shapes(n_remote, n_local):
    return [pltpu.SemaphoreType.DMA((n_remote,)), pltpu.SemaphoreType.DMA((n_remote,)),
            pltpu.SemaphoreType.DMA((n_local,))]


def _gather_parts(x_refs, out_refs, send_sems, recv_sems, local_sems):
    n = len(x_refs)
    x, y, cc = lax.axis_index("x"), lax.axis_index("y"), lax.axis_index("c")
    me, sibling = (x, y, cc), (x, y, 1 - cc)
    chips = [(1 - x, y), (x, 1 - y), (1 - x, 1 - y)]

    def slot(a, px, py, pc):
        return out_refs[a].at[4 * px + 2 * py + pc]

    def copy(k, a, block, to, src=None):
        return pltpu.make_async_remote_copy(
            src_ref=slot(a, *block) if src is None else src, dst_ref=slot(a, *block),
            send_sem=send_sems.at[k * n + a], recv_sem=recv_sems.at[k * n + a], device_id=to, device_id_type=MESH)

    mine = [pltpu.make_async_copy(x_refs[a], slot(a, *me), local_sems.at[a]) for a in range(n)]
    first = [copy(0, a, me, sibling, src=x_refs[a]) for a in range(n)]
    first += [copy(1 + j, a, me, (*chip, cc), src=x_refs[a]) for j, chip in enumerate(chips) for a in range(n)]

    def begin():
        for cp in mine + first:
            cp.start()

    def end():
        passed = []
        for j, chip in enumerate(chips):
            for a in range(n):
                copy(1 + j, a, (*chip, cc), me).wait_recv()
                passed.append(copy(4 + j, a, (*chip, cc), sibling))
                passed[-1].start()
        for a in range(n):
            copy(0, a, sibling, me).wait_recv()
        for j, chip in enumerate(chips):
            for a in range(n):
                copy(4 + j, a, (*chip, 1 - cc), me).wait_recv()
        for cp in first + passed:
            cp.wait_send()
        for cp in mine:
            cp.wait()

    return begin, end


def _exchange_parts(g_refs, out_refs, b_refs, ball_refs, send_sems, recv_sems, local_sems):
    n, nb = len(g_refs), len(b_refs)
    per_peer = n + nb
    x, y, cc = lax.axis_index("x"), lax.axis_index("y"), lax.axis_index("c")
    my_idx = 4 * x + 2 * y + cc
    mine = [pltpu.make_async_copy(g_refs[a].at[my_idx], out_refs[a].at[my_idx], local_sems.at[a]) for a in range(n)]
    mine += [pltpu.make_async_copy(b_refs[a], ball_refs[a].at[my_idx], local_sems.at[n + a]) for a in range(nb)]
    copies = []
    for k in range(1, N_DEV):
        px = 1 - x if k & 4 else x
        py = 1 - y if k & 2 else y
        pc = 1 - cc if k & 1 else cc
        base = (k - 1) * per_peer
        for a in range(n):
            copies.append(pltpu.make_async_remote_copy(
                src_ref=g_refs[a].at[4 * px + 2 * py + pc], dst_ref=out_refs[a].at[my_idx],
                send_sem=send_sems.at[base + a], recv_sem=recv_sems.at[base + a],
                device_id=(px, py, pc), device_id_type=MESH))
        for a in range(nb):
            copies.append(pltpu.make_async_remote_copy(
                src_ref=b_refs[a], dst_ref=ball_refs[a].at[my_idx], send_sem=send_sems.at[base + n + a],
                recv_sem=recv_sems.at[base + n + a], device_id=(px, py, pc), device_id_type=MESH))

    def begin():
        for cp in mine + copies:
            cp.start()

    def end():
        for cp in copies:
            cp.wait_recv()
        for cp in copies:
            cp.wait_send()
        for cp in mine:
            cp.wait()

    return begin, end


def _all_gather(blks, name):
    n = len(blks)

    def body(*refs):
        begin, end = _gather_parts(refs[:n], refs[n:2 * n], *refs[2 * n:])
        begin()
        end()

    return _pcall(body, name=name, out_shape=[jax.ShapeDtypeStruct((N_DEV,) + b.shape, b.dtype) for b in blks],
                  in_specs=[_HBM] * n, out_specs=[_HBM] * n, scratch_shapes=_sem_shapes(7 * n, n))(*blks)


def _sum_and_adamw(p_ref, w, m, v):
    g = p_ref[0].astype(F32)
    for s in range(1, N_DEV):
        g = g + p_ref[s].astype(F32)
    nm = ADAM_B1 * m + (1.0 - ADAM_B1) * g
    nv = ADAM_B2 * v + (1.0 - ADAM_B2) * (g * g)
    m_hat = nm / (1.0 - ADAM_B1 ** ADAM_STEP)
    v_hat = nv / (1.0 - ADAM_B2 ** ADAM_STEP)
    return g, -ADAM_LR * (m_hat / (jnp.sqrt(v_hat) + ADAM_EPS) + ADAM_WD * w), nm, nv


def _reduce_adamw_vectors(parts, w, m, v, shapes, name):
    table = _row_table(shapes)

    def body(p_ref, w_ref, m_ref, v_ref, *out_refs):
        results = _sum_and_adamw(p_ref, w_ref[...], m_ref[...], v_ref[...])
        for kind, val in enumerate(results):
            for t, (r0, nr, nc) in enumerate(table):
                out_refs[kind * len(table) + t][...] = val[r0:r0 + nr, 0:nc]

    outs = _pcall(body, name=name, out_shape=[jax.ShapeDtypeStruct((nr, nc), F32) for _, nr, nc in table] * 4)(parts, w, m, v)
    return [[outs[kind * len(table) + t].reshape(s) for t, s in enumerate(shapes)] for kind in range(4)]


def _reduce_adamw(parts, w, m, v, name):
    r, c = w.shape
    tr = 128 if r % 128 == 0 else r

    def body(p_ref, w_ref, m_ref, v_ref, g_ref, d_ref, nm_ref, nv_ref):
        g_ref[...], d_ref[...], nm_ref[...], nv_ref[...] = _sum_and_adamw(p_ref, w_ref[...], m_ref[...], v_ref[...])

    row = pl.BlockSpec((tr, c), lambda i: (i, 0))
    out = jax.ShapeDtypeStruct((r, c), F32)
    return _pcall(
        body, name=name, grid=(r // tr,),
        in_specs=[pl.BlockSpec((N_DEV, tr, c), lambda i: (0, i, 0)), row, row, row],
        out_specs=[row, row, row, row], out_shape=[out, out, out, out], compiler_params=_cp(1),
    )(parts, w, m, v)


def _proj_fwd(x, nw, w, wab, name, gather=()):
    t, npj = x.shape[0], w.shape[1]
    tm, tn = min(512, t), 1024
    has_ab = wab is not None
    n_in, n_out, n_g = 3 + has_ab, 2 + has_ab, len(gather)
    n_i = t // tm

    def body(*refs):
        ins, g_ins = refs[:n_in], refs[n_in:n_in + n_g]
        outs = refs[n_in + n_g:n_in + n_g + n_out]
        g_outs = refs[n_in + n_g + n_out:n_in + 2 * n_g + n_out]
        if has_ab:
            (x_ref, nw_ref, w_ref, wab_ref), (p_ref, h_ref, ab_ref) = ins, outs
        else:
            (x_ref, nw_ref, w_ref), (p_ref, h_ref) = ins, outs
        if n_g:
            begin, end = _gather_parts(g_ins, g_outs, *refs[n_in + 2 * n_g + n_out:])
            pl.when(pl.program_id(0) == 0)(begin)

        xv = x_ref[...]
        hv = (xv * lax.rsqrt(jnp.mean(xv * xv, axis=-1, keepdims=True) + EPS) * nw_ref[...]).astype(MXU)
        h_ref[...] = hv.astype(ACT)
        if has_ab:
            ab_ref[...] = _dot(hv, wab_ref[...], NN)
        for j in range(npj // tn):
            p_ref[:, j * tn:(j + 1) * tn] = _dot(hv, w_ref[:, j * tn:(j + 1) * tn], NN).astype(ACT)
        if n_g:
            pl.when(pl.program_id(0) == n_i - 1)(end)

    in_specs = [pl.BlockSpec((tm, D), lambda i: (i, 0)), _full((1, D)), _full((D, npj))]
    out_specs = [pl.BlockSpec((tm, npj), lambda i: (i, 0)), pl.BlockSpec((tm, D), lambda i: (i, 0))]
    out_shape = [jax.ShapeDtypeStruct((t, npj), ACT), jax.ShapeDtypeStruct((t, D), ACT)]
    args = [x, nw, w]
    if has_ab:
        in_specs.append(_full((D, HEAD)))
        out_specs.append(pl.BlockSpec((tm, HEAD), lambda i: (i, 0)))
        out_shape.append(jax.ShapeDtypeStruct((t, HEAD), F32))
        args.append(wab)
    scratch = []
    if n_g:
        in_specs += [_HBM] * n_g
        out_specs += [_HBM] * n_g
        out_shape += [jax.ShapeDtypeStruct((N_DEV,) + b.shape, b.dtype) for b in gather]
        args += list(gather)
        scratch += _sem_shapes(7 * n_g, n_g)
    return _pcall(body, name=name, grid=(n_i,), in_specs=in_specs, out_specs=out_specs,
                  out_shape=out_shape, scratch_shapes=scratch, compiler_params=_cp(1))(*args)


def _proj_bwd_x(dps, ws, x, dxn, nw, name, exchange=(), bcast=()):
    t = x.shape[0]
    tm = min(256, t)
    n, n_x, n_b = len(dps), len(exchange), len(bcast)
    n_c, n_steps = n_x + n_b, t // tm

    def body(*refs):
        dp_refs, w_refs = refs[:n], refs[n:2 * n]
        x_ref, dxn_ref, nw_ref = refs[2 * n:2 * n + 3]
        c_in = refs[2 * n + 3:2 * n + 3 + n_c]
        dx_ref, dnw_ref = refs[2 * n + 3 + n_c:2 * n + 5 + n_c]
        c_out = refs[2 * n + 5 + n_c:2 * n + 5 + 2 * n_c]
        if n_c:
            begin, end = _exchange_parts(c_in[:n_x], c_out[:n_x], c_in[n_x:], c_out[n_x:], *refs[2 * n + 5 + 2 * n_c:])
            pl.when(pl.program_id(0) == 0)(begin)
        dh = _mdot(dp_refs[0][...], w_refs[0][...], NT)
        for k in range(1, n):
            dh = dh + _mdot(dp_refs[k][...], w_refs[k][...], NT)
        xv = x_ref[...]
        r = lax.rsqrt(jnp.mean(xv * xv, axis=-1, keepdims=True) + EPS)

        @pl.when(pl.program_id(0) == 0)
        def _():
            dnw_ref[...] = jnp.zeros_like(dnw_ref)

        dnw_ref[...] += jnp.sum(dh * xv * r, axis=0, keepdims=True)
        dhw = dh * nw_ref[...]
        dx_ref[...] = dxn_ref[...] + r * (dhw - xv * (r * r) * jnp.mean(dhw * xv, axis=-1, keepdims=True))
        if n_c:
            pl.when(pl.program_id(0) == n_steps - 1)(end)

    row = pl.BlockSpec((tm, D), lambda i: (i, 0))
    in_specs = [pl.BlockSpec((tm, dp.shape[1]), lambda i: (i, 0)) for dp in dps]
    in_specs += [_full(w.shape) for w in ws]
    in_specs += [row, row, _full((1, D))] + [_HBM] * n_c
    out_shape = [jax.ShapeDtypeStruct((t, D), F32), jax.ShapeDtypeStruct((1, D), F32)]
    out_shape += [jax.ShapeDtypeStruct(e.shape, e.dtype) for e in exchange]
    out_shape += [jax.ShapeDtypeStruct((N_DEV,) + b.shape, b.dtype) for b in bcast]
    return _pcall(body, name=name, grid=(n_steps,), in_specs=in_specs, out_specs=[row, _full((1, D))] + [_HBM] * n_c,
                  out_shape=out_shape, scratch_shapes=_sem_shapes(7 * n_c, n_c) if n_c else [],
                  compiler_params=_cp(1))(*dps, *ws, x, dxn, nw, *exchange, *bcast)


def _matmul_tn(a, b, name):
    t, m = a.shape
    n = b.shape[1]
    tm, tn = min(1024, t), min(1024, n)

    def body(a_ref, b_ref, o_ref):
        @pl.when(pl.program_id(1) == 0)
        def _():
            o_ref[...] = jnp.zeros_like(o_ref)

        o_ref[...] += _mdot(a_ref[...], b_ref[...], TN)

    return _pcall(body, name=name, grid=(n // tn, t // tm),
                  in_specs=[pl.BlockSpec((tm, m), lambda j, i: (i, 0)), pl.BlockSpec((tm, tn), lambda j, i: (i, j))],
                  out_specs=pl.BlockSpec((m, tn), lambda j, i: (0, j)),
                  out_shape=jax.ShapeDtypeStruct((m, n), F32), compiler_params=_cp(2))(a, b)


def _memkv_fwd(mem, mw, wkv):
    def body(mem_ref, mw_ref, w_ref, kv_ref):
        mv = mem_ref[...]
        mn = mv * lax.rsqrt(jnp.mean(mv * mv, axis=-1, keepdims=True) + EPS) * mw_ref[...]
        kv_ref[...] = _mdot(mn, w_ref[...], NN)

    return _pcall(body, name="memkv_fwd", out_shape=jax.ShapeDtypeStruct((N_MEM, 2 * D_XA), F32),
                  compiler_params=pltpu.CompilerParams(vmem_limit_bytes=VMEM_LIMIT))(mem, mw, wkv)


def _memkv_bwd(mem, mw, wkv, dkv):
    def body(mem_ref, mw_ref, w_ref, dkv_ref, dw_ref, dmw_ref):
        mv = mem_ref[...]
        r = lax.rsqrt(jnp.mean(mv * mv, axis=-1, keepdims=True) + EPS)
        mn = mv * r * mw_ref[...]
        dkvv = dkv_ref[...]
        dw_ref[...] = _mdot(mn, dkvv, TN)
        dmn = _mdot(dkvv, w_ref[...], NT)
        dmw_ref[...] = jnp.sum(dmn * mv * r, axis=0, keepdims=True)

    return _pcall(body, name="memkv_bwd",
                  out_shape=[jax.ShapeDtypeStruct((D, 2 * D_XA), F32), jax.ShapeDtypeStruct((1, D), F32)],
                  compiler_params=pltpu.CompilerParams(vmem_limit_bytes=VMEM_LIMIT))(mem, mw, wkv, dkv)


def _attend(q, kv):
    ps, outs = [], []
    for hh in range(XA_HEADS):
        qh = q[:, hh * HEAD:(hh + 1) * HEAD]
        kh = kv[:, hh * HEAD:(hh + 1) * HEAD]
        vh = kv[:, D_XA + hh * HEAD:D_XA + (hh + 1) * HEAD]
        s = _mdot(qh, kh, NT) * (HEAD ** -0.5)
        s = s - jnp.max(s, axis=-1, keepdims=True)
        e = jnp.exp(s)
        p = e / jnp.sum(e, axis=-1, keepdims=True)
        ps.append(p)
        outs.append(_mdot(p, vh, NN))
    return ps, outs


def _tail_fwd(p, ymix, x, kv, wout, npost, name):
    t = x.shape[0]
    tm = min(512, t)

    def body(z_ref, q_ref, y_ref, x_ref, kv_ref, w_ref, np_ref, xn_ref, o_ref):
        _, outs = _attend(q_ref[...].astype(F32), kv_ref[...])
        cat = jnp.concatenate([y_ref[...].astype(F32)] + outs, axis=1)
        g = cat * _silu(z_ref[...].astype(F32))
        o = _mdot(g, w_ref[...], NN)
        o_ref[...] = o
        xn_ref[...] = x_ref[...] + o * lax.rsqrt(jnp.mean(o * o, axis=-1, keepdims=True) + EPS) * np_ref[...]

    row = pl.BlockSpec((tm, D), lambda i: (i, 0))
    return _pcall(
        body, name=name, grid=(t // tm,),
        in_specs=[pl.BlockSpec((tm, D_CAT), lambda i: (i, 0)), pl.BlockSpec((tm, D_XA), lambda i: (i, D_CAT // D_XA)),
                  row, row, _full((N_MEM, 2 * D_XA)), _full((D_CAT, D)), _full((1, D))],
        out_specs=[row, row],
        out_shape=[jax.ShapeDtypeStruct((t, D), F32), jax.ShapeDtypeStruct((t, D), F32)], compiler_params=_cp(1),
    )(p, p, ymix, x, kv, wout, npost)


def _tail_bwd(dxn, o, p, ymix, kv, wout, npost, name, exchange=()):
    t = dxn.shape[0]
    tm = min(256, t)
    n_x, n_steps = len(exchange), t // tm

    def body(*refs):
        dxn_ref, o_ref, z_ref, q_ref, y_ref, kv_ref, w_ref, np_ref = refs[:8]
        dzq_ref, dy_ref, dw_ref, dnp_ref, dkv_ref = refs[8 + n_x:13 + n_x]
        if n_x:
            begin, end = _exchange_parts(refs[8:8 + n_x], refs[13 + n_x:13 + 2 * n_x], (), (), *refs[13 + 2 * n_x:])
            pl.when(pl.program_id(0) == 0)(begin)

        @pl.when(pl.program_id(0) == 0)
        def _():
            dw_ref[...] = jnp.zeros_like(dw_ref)
            dnp_ref[...] = jnp.zeros_like(dnp_ref)
            dkv_ref[...] = jnp.zeros_like(dkv_ref)

        q = q_ref[...].astype(F32)
        kvv = kv_ref[...]
        z = z_ref[...].astype(F32)
        ps, outs = _attend(q, kvv)
        cat = jnp.concatenate([y_ref[...].astype(F32)] + outs, axis=1)
        sz = _silu(z)
        g = cat * sz
        ov = o_ref[...]
        dr = dxn_ref[...]
        rr = lax.rsqrt(jnp.mean(ov * ov, axis=-1, keepdims=True) + EPS)
        dnp_ref[...] += jnp.sum(dr * ov * rr, axis=0, keepdims=True)
        dow = dr * np_ref[...]
        do = rr * (dow - ov * (rr * rr) * jnp.mean(dow * ov, axis=-1, keepdims=True))
        dg = _mdot(do, w_ref[...], NT)
        dw_ref[...] += _mdot(g, do, TN)
        dcat = dg * sz
        dzq_ref[:, 0:D_CAT] = (dg * cat * _dsilu(z)).astype(ACT)
        dy_ref[...] = dcat[:, 0:D].astype(ACT)
        for hh in range(XA_HEADS):
            lo = hh * HEAD
            doh = dcat[:, D + lo:D + lo + HEAD]
            qh = q[:, lo:lo + HEAD]
            kh = kvv[:, lo:lo + HEAD]
            vh = kvv[:, D_XA + lo:D_XA + lo + HEAD]
            ph = ps[hh]
            dp = _mdot(doh, vh, NT)
            ds = ph * (dp - jnp.sum(dp * ph, axis=-1, keepdims=True))
            dzq_ref[:, D_CAT + lo:D_CAT + lo + HEAD] = (_mdot(ds, kh, NN) * (HEAD ** -0.5)).astype(ACT)
            dkv_ref[:, lo:lo + HEAD] += _mdot(ds, qh, TN) * (HEAD ** -0.5)
            dkv_ref[:, D_XA + lo:D_XA + lo + HEAD] += _mdot(ph, doh, TN)
        if n_x:
            pl.when(pl.program_id(0) == n_steps - 1)(end)

    row = pl.BlockSpec((tm, D), lambda i: (i, 0))
    return _pcall(
        body, name=name, grid=(n_steps,),
        in_specs=[row, row, pl.BlockSpec((tm, D_CAT), lambda i: (i, 0)),
                  pl.BlockSpec((tm, D_XA), lambda i: (i, D_CAT // D_XA)), row,
                  _full((N_MEM, 2 * D_XA)), _full((D_CAT, D)), _full((1, D))] + [_HBM] * n_x,
        out_specs=[pl.BlockSpec((tm, ZQ), lambda i: (i, 0)), row, _full((D_CAT, D)), _full((1, D)),
                   _full((N_MEM, 2 * D_XA))] + [_HBM] * n_x,
        out_shape=[jax.ShapeDtypeStruct((t, ZQ), ACT), jax.ShapeDtypeStruct((t, D), ACT),
                   jax.ShapeDtypeStruct((D_CAT, D), F32), jax.ShapeDtypeStruct((1, D), F32),
                   jax.ShapeDtypeStruct((N_MEM, 2 * D_XA), F32)]
        + [jax.ShapeDtypeStruct(e.shape, e.dtype) for e in exchange],
        scratch_shapes=_sem_shapes(7 * n_x, n_x) if n_x else [], compiler_params=_cp(1),
    )(dxn, o, p, p, ymix, kv, wout, npost, *exchange)


def _gmlp_chunk(us, vs, lnws, lnbs, wss, bss):
    gv = [jax.nn.gelu(v) for v in vs]
    mean = sum(jnp.sum(v, axis=-1, keepdims=True) for v in gv) / D
    cen = [v - mean for v in gv]
    var = sum(jnp.sum(c * c, axis=-1, keepdims=True) for c in cen) / D
    rstd = lax.rsqrt(var + EPS)
    row = lax.broadcasted_iota(jnp.int32, (GM_CHUNK, GM_CHUNK), 0)
    col = lax.broadcasted_iota(jnp.int32, (GM_CHUNK, GM_CHUNK), 1)
    ys = []
    for g in range(GM_GROUPS):
        vn = cen[g] * rstd * lnws[g] + lnbs[g]
        sp = _nn(jnp.where(row >= col, wss[g], 0.0), vn) + bss[g]
        ys.append(jax.nn.gelu(us[g]) * sp)
    return ys


def _split_cols(v, n, width=HEAD):
    return [v[:, k * width:(k + 1) * width] for k in range(n)]


def _gmlp_operands(u_ref, v_ref, lnw_ref, lnb_ref, ws_ref, bst_ref, r0):
    us = _split_cols(u_ref[r0:r0 + GM_CHUNK, :].astype(F32), GM_GROUPS)
    vs = _split_cols(v_ref[r0:r0 + GM_CHUNK, :].astype(F32), GM_GROUPS)
    lnws = _split_cols(lnw_ref[...], GM_GROUPS)
    lnbs = _split_cols(lnb_ref[...], GM_GROUPS)
    wss = [ws_ref[g] for g in range(GM_GROUPS)]
    bst = bst_ref[...]
    bss = [jnp.sum(bst * _onehot_row(HEAD, g), axis=1, keepdims=True) for g in range(GM_GROUPS)]
    return us, vs, lnws, lnbs, wss, bss


def _gmlp_rows(t):
    return min(GM_CHUNKS_PER_STEP, t // GM_CHUNK) * GM_CHUNK


def _gmlp_specs(rows):
    return [pl.BlockSpec((rows, D), lambda i: (i, ZQ // D)), pl.BlockSpec((rows, D), lambda i: (i, ZQ // D + 1)),
            _full((1, D)), _full((1, D)), _full((GM_GROUPS, GM_CHUNK, GM_CHUNK)), _full((GM_CHUNK, HEAD))]


def _gmlp_fwd(p, lnw, lnb, ws, bst, name):
    t = p.shape[0]
    rows = _gmlp_rows(t)

    def body(u_ref, v_ref, lnw_ref, lnb_ref, ws_ref, bst_ref, y_ref):
        for r0 in range(0, rows, GM_CHUNK):
            ys = _gmlp_chunk(*_gmlp_operands(u_ref, v_ref, lnw_ref, lnb_ref, ws_ref, bst_ref, r0))
            for g in range(GM_GROUPS):
                y_ref[r0:r0 + GM_CHUNK, g * HEAD:(g + 1) * HEAD] = ys[g].astype(ACT)

    return _pcall(body, name=name, grid=(t // rows,), in_specs=_gmlp_specs(rows),
                  out_specs=pl.BlockSpec((rows, D), lambda i: (i, 0)),
                  out_shape=jax.ShapeDtypeStruct((t, D), ACT), compiler_params=_cp(1))(p, p, lnw, lnb, ws, bst)


def _gmlp_bwd(p, dy, lnw, lnb, ws, bst, name):
    t = p.shape[0]
    rows = _gmlp_rows(t)

    def body(u_ref, v_ref, lnw_ref, lnb_ref, ws_ref, bst_ref, dy_ref, duv_ref, dlnw_ref, dlnb_ref, dws_ref, dbst_ref):
        @pl.when(pl.program_id(0) == 0)
        def _():
            dlnw_ref[...] = jnp.zeros_like(dlnw_ref)
            dlnb_ref[...] = jnp.zeros_like(dlnb_ref)
            dws_ref[...] = jnp.zeros_like(dws_ref)
            dbst_ref[...] = jnp.zeros_like(dbst_ref)

        for r0 in range(0, rows, GM_CHUNK):
            ops = _gmlp_operands(u_ref, v_ref, lnw_ref, lnb_ref, ws_ref, bst_ref, r0)
            _, vjp = jax.vjp(_gmlp_chunk, *ops)
            dus, dvs, dlnws, dlnbs, dwss, dbss = vjp(_split_cols(dy_ref[r0:r0 + GM_CHUNK, :].astype(F32), GM_GROUPS))
            dbst = jnp.zeros((GM_CHUNK, HEAD), F32)
            for g in range(GM_GROUPS):
                lo = g * HEAD
                duv_ref[r0:r0 + GM_CHUNK, lo:lo + HEAD] = dus[g].astype(ACT)
                duv_ref[r0:r0 + GM_CHUNK, D + lo:D + lo + HEAD] = dvs[g].astype(ACT)
                dlnw_ref[:, lo:lo + HEAD] += dlnws[g]
                dlnb_ref[:, lo:lo + HEAD] += dlnbs[g]
                dws_ref[g] += dwss[g]
                dbst = dbst + dbss[g] * _onehot_row(HEAD, g)
            dbst_ref[...] += dbst

    return _pcall(
        body, name=name, grid=(t // rows,),
        in_specs=_gmlp_specs(rows) + [pl.BlockSpec((rows, D), lambda i: (i, 0))],
        out_specs=[pl.BlockSpec((rows, 2 * D), lambda i: (i, 0)), _full((1, D)), _full((1, D)),
                   _full((GM_GROUPS, GM_CHUNK, GM_CHUNK)), _full((GM_CHUNK, HEAD))],
        out_shape=[jax.ShapeDtypeStruct((t, 2 * D), ACT), jax.ShapeDtypeStruct((1, D), F32),
                   jax.ShapeDtypeStruct((1, D), F32), jax.ShapeDtypeStruct((GM_GROUPS, GM_CHUNK, GM_CHUNK), F32),
                   jax.ShapeDtypeStruct((GM_CHUNK, HEAD), F32)],
        compiler_params=_cp(1),
    )(p, p, lnw, lnb, ws, bst, dy)


def _prev_halo(tm, col):
    return pl.BlockSpec((HALO, D), lambda i: (jnp.maximum(i * (tm // HALO) - 1, 0), col))


def _next_halo(tm, col, n_tiles):
    return pl.BlockSpec((HALO, D), lambda i: (jnp.minimum(i + 1, n_tiles - 1) * (tm // HALO), col))


def _taps_back(ext, w, width):
    acc = None
    for k in range(width):
        s = width - 1 - k
        term = w[k:k + 1, :] * (pltpu.roll(ext, s, 0) if s else ext)[HALO:, :]
        acc = term if acc is None else acc + term
    return acc


def _taps_fwd(ext, w, width, n):
    rows = ext.shape[0]
    acc = None
    for k in range(width):
        s = width - 1 - k
        term = w[k:k + 1, :] * (pltpu.roll(ext, rows - s, 0) if s else ext)[0:n, :]
        acc = term if acc is None else acc + term
    return acc


def _sconv_fwd(p, cw, name):
    t = p.shape[0]
    tm = min(256, t)

    def body(b_ref, c_ref, h_ref, cp_ref, hp_ref, w_ref, y_ref):
        first = pl.program_id(0) == 0
        prev = jnp.where(first, 0.0, cp_ref[...].astype(F32) * hp_ref[...].astype(F32))
        ext = jnp.concatenate([prev, c_ref[...].astype(F32) * h_ref[...].astype(F32)], axis=0)
        y_ref[...] = (b_ref[...].astype(F32) * _taps_back(ext, w_ref[...], 3)).astype(ACT)

    c0 = ZQ // D
    tile = [pl.BlockSpec((tm, D), lambda i, c=c: (i, c)) for c in (c0, c0 + 1, c0 + 2)]
    return _pcall(body, name=name, grid=(t // tm,),
                  in_specs=tile + [_prev_halo(tm, c0 + 1), _prev_halo(tm, c0 + 2), _full((HALO, D))],
                  out_specs=pl.BlockSpec((tm, D), lambda i: (i, 0)),
                  out_shape=jax.ShapeDtypeStruct((t, D), ACT), compiler_params=_cp(1))(p, p, p, p, p, cw)


def _sconv_bwd(p, dy, cw, name):
    t = p.shape[0]
    tm = min(256, t)
    n_tiles = t // tm

    def body(b_ref, c_ref, h_ref, cp_ref, hp_ref, bn_ref, dy_ref, dyn_ref, w_ref, d_ref, dw_ref):
        i = pl.program_id(0)

        @pl.when(i == 0)
        def _():
            dw_ref[...] = jnp.zeros_like(dw_ref)

        w = w_ref[...]
        bv, cv, hv = b_ref[...].astype(F32), c_ref[...].astype(F32), h_ref[...].astype(F32)
        dyv = dy_ref[...].astype(F32)
        prev = jnp.where(i == 0, 0.0, cp_ref[...].astype(F32) * hp_ref[...].astype(F32))
        ext = jnp.concatenate([prev, cv * hv], axis=0)
        conv = _taps_back(ext, w, 3)
        dconv = dyv * bv
        nxt = jnp.where(i == n_tiles - 1, 0.0, dyn_ref[...].astype(F32) * bn_ref[...].astype(F32))
        dc = _taps_fwd(jnp.concatenate([dconv, nxt], axis=0), w, 3, tm)
        d_ref[:, 0:D] = (dyv * conv).astype(ACT)
        d_ref[:, D:2 * D] = (dc * hv).astype(ACT)
        d_ref[:, 2 * D:3 * D] = (dc * cv).astype(ACT)
        for k in range(3):
            s = 2 - k
            shifted = (pltpu.roll(ext, s, 0) if s else ext)[HALO:, :]
            dw_ref[k:k + 1, :] += jnp.sum(dconv * shifted, axis=0, keepdims=True)

    c0 = ZQ // D
    tile = [pl.BlockSpec((tm, D), lambda i, c=c: (i, c)) for c in (c0, c0 + 1, c0 + 2)]
    return _pcall(
        body, name=name, grid=(n_tiles,),
        in_specs=tile + [_prev_halo(tm, c0 + 1), _prev_halo(tm, c0 + 2), _next_halo(tm, c0, n_tiles),
                         pl.BlockSpec((tm, D), lambda i: (i, 0)), _next_halo(tm, 0, n_tiles), _full((HALO, D))],
        out_specs=[pl.BlockSpec((tm, 3 * D), lambda i: (i, 0)), _full((HALO, D))],
        out_shape=[jax.ShapeDtypeStruct((t, 3 * D), ACT), jax.ShapeDtypeStruct((HALO, D), F32)],
        compiler_params=_cp(1),
    )(p, p, p, p, p, p, dy, dy, cw)


def _l2_heads(s, scale):
    outs, rs = [], []
    for hh in range(DN_HEADS):
        blk = s[:, hh * HEAD:(hh + 1) * HEAD]
        r = lax.rsqrt(jnp.sum(blk * blk, axis=-1, keepdims=True) + EPS)
        outs.append(blk * (r * scale))
        rs.append(r)
    return outs, rs


_QKV_SCALE = (HEAD ** -0.5, 1.0, None)


def _qkv_fwd(p, cw, name):
    t = p.shape[0]
    tm = min(256, t)

    def body(q_ref, k_ref, v_ref, qp_ref, kp_ref, vp_ref, w_ref, o_ref):
        first = pl.program_id(0) == 0
        for part, (ref, pref) in enumerate(((q_ref, qp_ref), (k_ref, kp_ref), (v_ref, vp_ref))):
            prev = jnp.where(first, 0.0, pref[...].astype(F32))
            ext = jnp.concatenate([prev, ref[...].astype(F32)], axis=0)
            s = _silu(_taps_back(ext, w_ref[:, part * D:(part + 1) * D], 4))
            if _QKV_SCALE[part] is None:
                o_ref[:, part * D:(part + 1) * D] = s.astype(ACT)
            else:
                outs, _ = _l2_heads(s, _QKV_SCALE[part])
                for hh in range(DN_HEADS):
                    o_ref[:, part * D + hh * HEAD:part * D + (hh + 1) * HEAD] = outs[hh].astype(ACT)

    c0 = ZQ // D
    tile = [pl.BlockSpec((tm, D), lambda i, c=c: (i, c)) for c in (c0, c0 + 1, c0 + 2)]
    halo = [_prev_halo(tm, c) for c in (c0, c0 + 1, c0 + 2)]
    return _pcall(body, name=name, grid=(t // tm,), in_specs=tile + halo + [_full((HALO, 3 * D))],
                  out_specs=pl.BlockSpec((tm, 3 * D), lambda i: (i, 0)),
                  out_shape=jax.ShapeDtypeStruct((t, 3 * D), ACT), compiler_params=_cp(1))(p, p, p, p, p, p, cw)


def _qkv_bwd(p, dqkv, cw, name):
    t = p.shape[0]
    tm = min(256, t)
    n_tiles = t // tm

    def body(*refs):
        tiles, prevs, nexts = refs[0:3], refs[3:6], refs[6:9]
        d_tiles, d_nexts = refs[9:12], refs[12:15]
        w_ref, o_ref, dw_ref = refs[15:]
        i = pl.program_id(0)

        @pl.when(i == 0)
        def _():
            dw_ref[...] = jnp.zeros_like(dw_ref)

        for part in range(3):
            w = w_ref[:, part * D:(part + 1) * D]
            prev = jnp.where(i == 0, 0.0, prevs[part][...].astype(F32))
            ext = jnp.concatenate([prev, tiles[part][...].astype(F32), nexts[part][...].astype(F32)], axis=0)
            xc = _taps_back(ext, w, 4)
            dout = jnp.concatenate([d_tiles[part][...].astype(F32), d_nexts[part][...].astype(F32)], axis=0)
            s = _silu(xc)
            if _QKV_SCALE[part] is None:
                ds = dout
            else:
                scale = _QKV_SCALE[part]
                pieces = []
                for hh in range(DN_HEADS):
                    blk = s[:, hh * HEAD:(hh + 1) * HEAD]
                    dblk = dout[:, hh * HEAD:(hh + 1) * HEAD]
                    r = lax.rsqrt(jnp.sum(blk * blk, axis=-1, keepdims=True) + EPS)
                    pieces.append(scale * r * (dblk - blk * (r * r) * jnp.sum(dblk * blk, axis=-1, keepdims=True)))
                ds = jnp.concatenate(pieces, axis=1)
            dxc = ds * _dsilu(xc)
            row = lax.broadcasted_iota(jnp.int32, (tm + HALO, 1), 0)
            dxc = jnp.where(jnp.logical_and(i == n_tiles - 1, row >= tm), 0.0, dxc)
            o_ref[:, part * D:(part + 1) * D] = _taps_fwd(dxc, w, 4, tm).astype(ACT)
            for k in range(4):
                sh = 3 - k
                shifted = (pltpu.roll(ext, sh, 0) if sh else ext)[HALO:HALO + tm, :]
                dw_ref[k:k + 1, part * D:(part + 1) * D] += jnp.sum(dxc[0:tm, :] * shifted, axis=0, keepdims=True)

    c0 = ZQ // D
    cols = (c0, c0 + 1, c0 + 2)
    tile = [pl.BlockSpec((tm, D), lambda i, c=c: (i, c)) for c in cols]
    dtile = [pl.BlockSpec((tm, D), lambda i, c=c: (i, c)) for c in range(3)]
    in_specs = (tile + [_prev_halo(tm, c) for c in cols] + [_next_halo(tm, c, n_tiles) for c in cols]
                + dtile + [_next_halo(tm, c, n_tiles) for c in range(3)] + [_full((HALO, 3 * D))])
    return _pcall(
        body, name=name, grid=(n_tiles,), in_specs=in_specs,
        out_specs=[pl.BlockSpec((tm, 3 * D), lambda i: (i, 0)), _full((HALO, 3 * D))],
        out_shape=[jax.ShapeDtypeStruct((t, 3 * D), ACT), jax.ShapeDtypeStruct((HALO, 3 * D), F32)],
        compiler_params=_cp(1),
    )(*([p] * 9), *([dqkv] * 6), cw)


def _tri_masks(n):
    row = lax.broadcasted_iota(jnp.int32, (n, n), 0)
    col = lax.broadcasted_iota(jnp.int32, (n, n), 1)
    return row, col


@jax.custom_vjp
def _unit_lower_inverses(mats):
    n = DN_CHUNK
    row, col = _tri_masks(n)
    eye = (row == col).astype(F32)
    same16 = (row // 16) == (col // 16)
    same32 = (row // 32) == (col // 32)
    pw = [jnp.where(same16, a, 0.0) for a in mats]
    x = [eye - p for p in pw]
    for _ in range(3):
        pw = [_hnn(p, p) for p in pw]
        x = [_hnn(xi, eye + p) for xi, p in zip(x, pw)]
    for keep in (jnp.logical_and(same32, jnp.logical_not(same16)), jnp.logical_not(same32)):
        inner = [_hnn(jnp.where(keep, a, 0.0), xi) for a, xi in zip(mats, x)]
        x = [xi - _hnn(xi, y) for xi, y in zip(x, inner)]
    return tuple(x)


def _uli_fwd(mats):
    t = _unit_lower_inverses(mats)
    return t, t


def _uli_bwd(ts, gs):
    inner = [_hnt(g, t) for g, t in zip(gs, ts)]
    return (tuple(-_htn(t, y) for t, y in zip(ts, inner)),)


_unit_lower_inverses.defvjp(_uli_fwd, _uli_bwd)


@jax.custom_vjp
def _known_inverses(mats, ts):
    return ts


_known_inverses.defvjp(lambda mats, ts: (ts, ts),
                       lambda ts, gs: (_uli_bwd(ts, gs)[0], tuple(jnp.zeros_like(t) for t in ts)))


def _pick_col(m, k):
    return jnp.sum(m * _onehot_row(m.shape[1], k), axis=1, keepdims=True)


def _pick_row(m, k):
    hot = (lax.broadcasted_iota(jnp.int32, (m.shape[0], 1), 0) == k).astype(F32)
    return jnp.sum(m * hot, axis=0, keepdims=True)


def _delta_chunk(states, qs, ks, vs, ab, alog, dtb, onw, known_inverses=None):
    n = DN_CHUNK
    heads = range(DN_HEADS)
    row, col = _tri_masks(n)
    incl = row >= col
    lane = lax.broadcasted_iota(jnp.int32, (1, HEAD), 1)
    g_all = jnp.where(lane < DN_HEADS, -jnp.exp(alog) * jax.nn.softplus(ab + dtb), 0.0)
    c_cols = _hnn(incl.astype(F32), g_all)
    c_rows = _htn(g_all, (row <= col).astype(F32))
    g_tot = jnp.sum(g_all, axis=0, keepdims=True)
    beta_all = jax.nn.sigmoid(ab)
    ccol = [_pick_col(c_cols, h) for h in heads]
    crow = [_pick_row(c_rows, h) for h in heads]
    gl = [_pick_col(g_tot, h) for h in heads]
    beta = [_pick_col(beta_all, DN_HEADS + h) for h in heads]
    decay = [jnp.exp(jnp.where(incl, ccol[h] - crow[h], -1e30)) for h in heads]
    eg = [jnp.exp(ccol[h]) for h in heads]
    kb = [ks[h] * beta[h] for h in heads]
    amat = [jnp.where(row > col, _nt(kb[h], ks[h]) * decay[h], 0.0) for h in heads]
    if known_inverses is None:
        tmat = _unit_lower_inverses(tuple(amat))
    else:
        tmat = _known_inverses(tuple(amat), tuple(known_inverses))
    u = [_nn(tmat[h], vs[h] * beta[h]) for h in heads]
    w = [_nn(tmat[h], kb[h] * eg[h]) for h in heads]
    qk = [_nt(qs[h], ks[h]) * decay[h] for h in heads]
    v_new = [u[h] - _nn(w[h], states[h]) for h in heads]
    o = [_nn(qs[h] * eg[h], states[h]) + _nn(qk[h], v_new[h]) for h in heads]
    new_states = [states[h] * jnp.exp(gl[h]) + _tn(ks[h] * jnp.exp(gl[h] - ccol[h]), v_new[h]) for h in heads]
    ys = [o[h] * lax.rsqrt(jnp.mean(o[h] * o[h], axis=-1, keepdims=True) + EPS) * onw for h in heads]
    return (ys, new_states), tmat


def _head_cols(ref):
    return [ref[:, h * HEAD:(h + 1) * HEAD].astype(F32) for h in range(DN_HEADS)]


def _delta_fwd(qkv, ab, alog, dtb, onw, name):
    t = qkv.shape[0]
    nc = t // DN_CHUNK

    def body(q_ref, k_ref, v_ref, ab_ref, alog_ref, dtb_ref, onw_ref, y_ref, keep_ref, inv_ref, state):
        @pl.when(pl.program_id(0) == 0)
        def _():
            state[...] = jnp.zeros_like(state)

        s0 = [state[hh] for hh in range(DN_HEADS)]
        (ys, s1), tmat = _delta_chunk(s0, _head_cols(q_ref), _head_cols(k_ref), _head_cols(v_ref), ab_ref[...],
                                      alog_ref[...], dtb_ref[...], onw_ref[...])
        for hh in range(DN_HEADS):
            keep_ref[0, hh] = s0[hh]
            inv_ref[0, hh] = tmat[hh]
            state[hh] = s1[hh]
            y_ref[:, hh * HEAD:(hh + 1) * HEAD] = ys[hh].astype(ACT)

    chunk = [pl.BlockSpec((DN_CHUNK, D), lambda i, c=c: (i, c)) for c in range(3)]
    return _pcall(
        body, name=name, grid=(nc,),
        in_specs=chunk + [pl.BlockSpec((DN_CHUNK, HEAD), lambda i: (i, 0)), _full((1, HEAD)), _full((1, HEAD)),
                          _full((1, HEAD))],
        out_specs=[pl.BlockSpec((DN_CHUNK, D), lambda i: (i, 0)),
                   pl.BlockSpec((1, DN_HEADS, HEAD, HEAD), lambda i: (i, 0, 0, 0)),
                   pl.BlockSpec((1, DN_HEADS, DN_CHUNK, DN_CHUNK), lambda i: (i, 0, 0, 0))],
        out_shape=[jax.ShapeDtypeStruct((t, D), ACT), jax.ShapeDtypeStruct((nc, DN_HEADS, HEAD, HEAD), F32),
                   jax.ShapeDtypeStruct((nc, DN_HEADS, DN_CHUNK, DN_CHUNK), F32)],
        scratch_shapes=[pltpu.VMEM((DN_HEADS, HEAD, HEAD), F32)], compiler_params=_cp(1),
    )(qkv, qkv, qkv, ab, alog, dtb, onw)


def _delta_bwd(qkv, ab, alog, dtb, onw, keep, inv, dy, name):
    t = qkv.shape[0]
    nc = t // DN_CHUNK

    def body(q_ref, k_ref, v_ref, ab_ref, alog_ref, dtb_ref, onw_ref, keep_ref, inv_ref, dy_ref,
             dqkv_ref, dab_ref, dalog_ref, ddtb_ref, donw_ref, dstate):
        @pl.when(pl.program_id(0) == 0)
        def _():
            dstate[...] = jnp.zeros_like(dstate)
            dalog_ref[...] = jnp.zeros_like(dalog_ref)
            ddtb_ref[...] = jnp.zeros_like(ddtb_ref)
            donw_ref[...] = jnp.zeros_like(donw_ref)

        s0 = [keep_ref[0, hh] for hh in range(DN_HEADS)]
        known = [inv_ref[0, hh] for hh in range(DN_HEADS)]
        _, vjp, _ = jax.vjp(functools.partial(_delta_chunk, known_inverses=known), s0, _head_cols(q_ref),
                            _head_cols(k_ref), _head_cols(v_ref), ab_ref[...], alog_ref[...], dtb_ref[...], onw_ref[...],
                            has_aux=True)
        ds, dq, dk, dv, dab, dal, ddt, don = vjp((_head_cols(dy_ref), [dstate[hh] for hh in range(DN_HEADS)]))
        for hh in range(DN_HEADS):
            lo = hh * HEAD
            dstate[hh] = ds[hh]
            dqkv_ref[:, lo:lo + HEAD] = dq[hh].astype(ACT)
            dqkv_ref[:, D + lo:D + lo + HEAD] = dk[hh].astype(ACT)
            dqkv_ref[:, 2 * D + lo:2 * D + lo + HEAD] = dv[hh].astype(ACT)
        dab_ref[...] = dab
        dalog_ref[...] += dal
        ddtb_ref[...] += ddt
        donw_ref[...] += don

    rev = lambda i: nc - 1 - i
    chunk = [pl.BlockSpec((DN_CHUNK, D), lambda i, c=c: (rev(i), c)) for c in range(3)]
    small = jax.ShapeDtypeStruct((1, HEAD), F32)
    return _pcall(
        body, name=name, grid=(nc,),
        in_specs=chunk + [pl.BlockSpec((DN_CHUNK, HEAD), lambda i: (rev(i), 0)), _full((1, HEAD)), _full((1, HEAD)),
                          _full((1, HEAD)), pl.BlockSpec((1, DN_HEADS, HEAD, HEAD), lambda i: (rev(i), 0, 0, 0)),
                          pl.BlockSpec((1, DN_HEADS, DN_CHUNK, DN_CHUNK), lambda i: (rev(i), 0, 0, 0)),
                          pl.BlockSpec((DN_CHUNK, D), lambda i: (rev(i), 0))],
        out_specs=[pl.BlockSpec((DN_CHUNK, 3 * D), lambda i: (rev(i), 0)),
                   pl.BlockSpec((DN_CHUNK, HEAD), lambda i: (rev(i), 0)), _full((1, HEAD)), _full((1, HEAD)),
                   _full((1, HEAD))],
        out_shape=[jax.ShapeDtypeStruct((t, 3 * D), ACT), jax.ShapeDtypeStruct((t, HEAD), F32), small, small, small],
        scratch_shapes=[pltpu.VMEM((DN_HEADS, HEAD, HEAD), F32)], compiler_params=_cp(1),
    )(qkv, qkv, qkv, ab, alog, dtb, onw, keep, inv, dy)


def _loss_head(y, target):
    t = y.shape[0]
    tm = min(512, t)

    def body(y_ref, t_ref, l_ref, dy_ref):
        @pl.when(pl.program_id(0) == 0)
        def _():
            l_ref[...] = jnp.zeros_like(l_ref)

        diff = y_ref[...] - t_ref[...]
        dy_ref[...] = diff * (1.0 / D)
        l_ref[...] += 0.5 * jnp.sum(jnp.sum(diff * diff, axis=-1, keepdims=True) * (1.0 / D), axis=0, keepdims=True)

    row = pl.BlockSpec((tm, D), lambda i: (i, 0))
    return _pcall(body, name="loss_head", grid=(t // tm,), in_specs=[row, row], out_specs=[_full((8, 128)), row],
                  out_shape=[jax.ShapeDtypeStruct((8, 128), F32), jax.ShapeDtypeStruct((t, D), F32)],
                  compiler_params=_cp(1))(y, target)


_SMALL_SHARDED = (("a_ln_w", (2, 128), 1), ("a_ln_b", (2, 128), 1), ("b_conv_w", (1, 3, 128), 2),
                  ("c_conv_w", (1, 4, 384), 2))
_REPLICATED = (("mem_norm_w", (1024,)), ("norm_pre", (4, 1024)), ("norm_post", (4, 1024)),
               ("a_b_s", (2, 8, 128)), ("c_a_log", (1, 8)), ("c_dt_bias", (1, 8)), ("c_o_norm_w", (1, 128)))


def _layer_shards(given, prefix):
    w_out = given[prefix + "w_out"]
    w_ins = [given[prefix + "a_w_in"][0], given[prefix + "b_w_in"][0], given[prefix + "c_w_in"][0],
             given[prefix + "a_w_in"][1]]
    rows = [jnp.concatenate([given[prefix + "w_mem_kv"], w_out[0]])] + [w_out[i] for i in (1, 2, 3)]
    return [[rows[i], w_ins[i]] for i in range(4)]


def _as_rows(shape):
    return (math.prod(shape[:-1]) if len(shape) > 1 else 1, shape[-1])


def _row_table(shapes):
    table, at = [], 0
    for shape in shapes:
        rows, cols = _as_rows(shape)
        table.append((at, rows, cols))
        at += rows
    return table


def _pack_rows(arrays, shapes, rows, lead=()):
    n_lead = len(lead)
    parts = [jnp.pad(a.reshape(lead + _as_rows(s)), [(0, 0)] * (n_lead + 1) + [(0, D - s[-1])])
             for a, s in zip(arrays, shapes)]
    block = jnp.concatenate(parts, axis=n_lead)
    return jnp.pad(block, [(0, 0)] * n_lead + [(0, rows - block.shape[n_lead]), (0, 0)])


def _unpack_rows(packed, shapes, lead=()):
    n_lead = len(lead)
    return [packed[(slice(None),) * n_lead + (slice(r0, r0 + nr), slice(0, nc))].reshape(lead + s)
            for (r0, nr, nc), s in zip(_row_table(shapes), shapes)]


def _join_shards(blocks, axis):
    moved = jnp.moveaxis(blocks, 0, axis)
    shape = moved.shape
    return moved.reshape(shape[:axis] + (shape[axis] * shape[axis + 1],) + shape[axis + 2:])


def _split_shards(full, axis):
    shape = full.shape
    split = full.reshape(shape[:axis] + (N_DEV, shape[axis] // N_DEV) + shape[axis + 1:])
    return jnp.moveaxis(split, axis, 0)


_A_COLS = ((2560, 1536), (2048, 512), (0, 2048))
_BC_COLS = ((3584, 1536), (3072, 512), (0, 3072))
_C_COLS = ((3600, 1536), (3088, 512), (0, 3072))


def _reorder_cols(w, cols):
    return jnp.concatenate([w[:, s:s + n] for s, n in cols], axis=1)


def _restore_cols(pieces_in_my_order, cols, extra=()):
    placed = sorted(list(zip([s for s, _ in cols], pieces_in_my_order)) + list(extra), key=lambda sp: sp[0])
    return jnp.concatenate([piece for _, piece in placed], axis=1)


def kernel(x, mem, mem_norm_w, w_mem_kv, norm_pre, norm_post, w_out, a_w_in, a_ln_w, a_ln_b, a_w_s, a_b_s, b_w_in, b_conv_w, c_w_in, c_conv_w, c_a_log, c_dt_bias, c_o_norm_w, loss_target, m_mem_norm_w, m_w_mem_kv, m_norm_pre, m_norm_post, m_w_out, m_a_w_in, m_a_ln_w, m_a_ln_b, m_a_w_s, m_a_b_s, m_b_w_in, m_b_conv_w, m_c_w_in, m_c_conv_w, m_c_a_log, m_c_dt_bias, m_c_o_norm_w, v_mem_norm_w, v_w_mem_kv, v_norm_pre, v_norm_post, v_w_out, v_a_w_in, v_a_ln_w, v_a_ln_b, v_a_w_s, v_a_b_s, v_b_w_in, v_b_conv_w, v_c_w_in, v_c_conv_w, v_c_a_log, v_c_dt_bias, v_c_o_norm_w):
    given = dict(locals())
    x0 = x[0]
    mem0 = mem[0]
    target = loss_target[0]

    w_sh, m_sh, v_sh = (_layer_shards(given, pre) for pre in ("", "m_", "v_"))
    small_names = [n for n, _, _ in _SMALL_SHARDED]
    small_shapes = [s for _, s, _ in _SMALL_SHARDED]
    repl_names = [n for n, _ in _REPLICATED]
    repl_shapes = [s for _, s in _REPLICATED]
    small = _pack_rows([given[n] for n in small_names], small_shapes, R_SMALL)
    g_rows0, g_in0, g_small = _all_gather([w.astype(MXU) for w in w_sh[0]] + [small], "gather_weights")
    small_full = _unpack_rows(g_small, small_shapes, lead=(N_DEV,))
    full = {n: _join_shards(blocks, ax) for (n, _, ax), blocks in zip(_SMALL_SHARDED, small_full)}
    wkv = g_rows0[:, :128].reshape(D, 2 * D_XA)
    wouts = [g_rows0[:, 128:].reshape(D_CAT, D)]
    w_in = [_reorder_cols(_join_shards(g_in0, 1), _A_COLS)]
    w_cab = None
    lnw = [full["a_ln_w"][j][None, :] for j in range(2)]
    lnb = [full["a_ln_b"][j][None, :] for j in range(2)]
    ws = [a_w_s[j] for j in range(2)]
    bst = [jnp.pad(a_b_s[j].T, ((0, 0), (0, HEAD - GM_GROUPS))) for j in range(2)]
    cw_b = jnp.pad(full["b_conv_w"][0], ((0, HALO - 3), (0, 0)))
    cw_c = jnp.pad(full["c_conv_w"][0], ((0, HALO - 4), (0, 0)))
    alog = jnp.pad(c_a_log, ((0, 0), (0, HEAD - DN_HEADS)))
    dtb = jnp.pad(c_dt_bias, ((0, 0), (0, HEAD - DN_HEADS)))
    onw = c_o_norm_w
    mw = mem_norm_w[None, :]

    kv = _memkv_fwd(mem0, mw, wkv)
    xs, saved = [x0], []
    for i in range(4):
        kind = i % 3
        npre, npost = norm_pre[i][None, :], norm_post[i][None, :]
        ahead = [w.astype(MXU) for w in w_sh[i + 1]] if i < 3 else []
        res = _proj_fwd(xs[i], npre, w_in[i], w_cab if kind == 2 else None, f"proj_fwd_{i}", gather=ahead)
        if ahead:
            g_wout, g_in = res[-2:]
            wouts.append(g_wout.reshape(D_CAT, D))
            if i + 1 == 2:
                c_full = jnp.concatenate([g_in[d] for d in range(N_DEV)], axis=1)
                w_in.append(_reorder_cols(c_full, _C_COLS))
                w_cab = jnp.pad(c_full[:, 3072:3088], ((0, 0), (0, HEAD - 16)))
            else:
                w_in.append(_reorder_cols(_join_shards(g_in, 1), _BC_COLS if i + 1 == 1 else _A_COLS))
        if kind == 2:
            p, h, ab = res[:3]
            qkv = _qkv_fwd(p, cw_c, f"qkv_fwd_{i}")
            ymix, keep, inv = _delta_fwd(qkv, ab, alog, dtb, onw, f"delta_fwd_{i}")
            extra = (qkv, ab, keep, inv)
        else:
            p, h = res[:2]
            if kind == 0:
                ymix = _gmlp_fwd(p, lnw[i // 3], lnb[i // 3], ws[i // 3], bst[i // 3], f"gmlp_fwd_{i}")
            else:
                ymix = _sconv_fwd(p, cw_b, f"sconv_fwd_{i}")
            extra = ()
        xn, o = _tail_fwd(p, ymix, xs[i], kv, wouts[i], npost, f"tail_fwd_{i}")
        xs.append(xn)
        saved.append((p, h, ymix, o, extra))

    loss_tile, dx = _loss_head(xs[4], target)
    loss = lax.psum(loss_tile[0, 0], ("x", "y", "c"))

    g = {}
    d_npre, d_npost = [None] * 4, [None] * 4
    d_ws, d_bs, d_lnw, d_lnb = [None] * 2, [None] * 2, [None] * 2, [None] * 2
    dkv = None
    pending, landed = [], [None] * 4
    for i in reversed(range(4)):
        kind = i % 3
        p, h, ymix, o, extra = saved[i]
        npre, npost = norm_pre[i][None, :], norm_post[i][None, :]
        res = _tail_bwd(dx, o, p, ymix, kv, wouts[i], npost, f"tail_bwd_{i}", exchange=pending)
        dzq, dymix, d_wout, d_npost[i], dkv_i = res[:5]
        if pending:
            landed[i + 1] = res[5:]
        dkv = dkv_i if dkv is None else dkv + dkv_i
        w_zq, w_mix = w_in[i][:, :ZQ], w_in[i][:, ZQ:]
        dw_zq = _matmul_tn(h, dzq, f"dw_zq_{i}")
        if kind == 0:
            j = i // 3
            dmix, d_lnw[j], d_lnb[j], d_ws[j], dbst = _gmlp_bwd(p, dymix, lnw[j], lnb[j], ws[j], bst[j], f"gmlp_bwd_{i}")
            d_bs[j] = dbst[:, :GM_GROUPS].T
            dw_mix = _matmul_tn(h, dmix, f"dw_mix_{i}")
            d_win = _restore_cols([dw_zq[:, :D_CAT], dw_zq[:, D_CAT:], dw_mix], _A_COLS)
            dps, wparts = [dzq, dmix], [w_zq, w_mix]
        elif kind == 1:
            dmix, dcw = _sconv_bwd(p, dymix, cw_b, f"sconv_bwd_{i}")
            g["b_conv_w"] = dcw[None, :3]
            dw_mix = _matmul_tn(h, dmix, f"dw_mix_{i}")
            d_win = _restore_cols([dw_zq[:, :D_CAT], dw_zq[:, D_CAT:], dw_mix], _BC_COLS)
            dps, wparts = [dzq, dmix], [w_zq, w_mix]
        else:
            qkv, ab, keep, inv = extra
            dqkv, dab, dalog, ddtb, donw = _delta_bwd(qkv, ab, alog, dtb, onw, keep, inv, dymix, f"delta_bwd_{i}")
            dmix, dcw = _qkv_bwd(p, dqkv, cw_c, f"qkv_bwd_{i}")
            g["c_conv_w"] = dcw[None, :4]
            g["c_a_log"], g["c_dt_bias"], g["c_o_norm_w"] = dalog[:, :DN_HEADS], ddtb[:, :DN_HEADS], donw
            dw_mix = _matmul_tn(h, dmix, f"dw_mix_{i}")
            dw_ab = _matmul_tn(h, dab, f"dw_ab_{i}")
            d_win = _restore_cols([dw_zq[:, :D_CAT], dw_zq[:, D_CAT:], dw_mix], _C_COLS, extra=[(3072, dw_ab[:, :16])])
            dps, wparts = [dzq, dmix, dab], [w_zq, w_mix, w_cab]
        width = d_win.shape[1] // N_DEV
        pending = [d_wout.reshape(N_DEV, D_CAT // N_DEV, D).astype(XCH),
                   jnp.stack([d_win[:, d * width:(d + 1) * width] for d in range(N_DEV)]).astype(XCH)]
        if i > 0:
            dx, d_npre[i] = _proj_bwd_x(dps, wparts, xs[i], dx, npre, f"proj_bwd_x_{i}")

    d_wkv, d_mw = _memkv_bwd(mem0, mw, wkv, dkv)
    g["mem_norm_w"] = d_mw[0]
    g["norm_pre"] = jnp.concatenate([jnp.zeros((1, D), F32)] + d_npre[1:], axis=0)
    g["norm_post"] = jnp.concatenate(d_npost, axis=0)
    g["a_ln_w"] = jnp.concatenate(d_lnw, axis=0)
    g["a_ln_b"] = jnp.concatenate(d_lnb, axis=0)
    g["a_b_s"] = jnp.stack(d_bs)
    pending[0] = jnp.concatenate([d_wkv.reshape(N_DEV, 128, D).astype(XCH), pending[0]], axis=1)
    e_small = _pack_rows([_split_shards(g[n], ax) for n, _, ax in _SMALL_SHARDED], small_shapes, R_SMALL, lead=(N_DEV,))
    d_ws_pack = jnp.stack(d_ws).reshape(2 * GM_GROUPS * GM_CHUNK * GM_CHUNK // D, D).astype(XCH)
    r_pack = _pack_rows([g[n] for n in repl_names], repl_shapes, R_REPL)
    dx, d_npre0, l_rows, l_in, l_small, ws_all, r_all = _proj_bwd_x(
        dps, wparts, xs[0], dx, norm_pre[0][None, :], "proj_bwd_x_0", exchange=pending + [e_small],
        bcast=[d_ws_pack, r_pack])
    landed[0] = (l_rows, l_in)
    npre0_all, = _all_gather([jnp.pad(d_npre0, ((0, HALO - 1), (0, 0)))], "gather_norm_pre0")
    r_all = r_all.at[:, 1, :].set(npre0_all[:, 0, :])

    res = [[_reduce_adamw(landed[i][a], w_sh[i][a], m_sh[i][a], v_sh[i][a], f"adamw_{i}_{a}") for a in range(2)]
           for i in range(4)]
    res_small = _reduce_adamw_vectors(
        l_small, small, *(_pack_rows([given[pre + n] for n in small_names], small_shapes, R_SMALL) for pre in ("m_", "v_")),
        small_shapes, "adamw_small")
    res_ws = _reduce_adamw(ws_all, *(given[pre + "a_w_s"].reshape(d_ws_pack.shape) for pre in ("", "m_", "v_")),
                           "adamw_a_w_s")
    res_repl = _reduce_adamw_vectors(
        r_all, *(_pack_rows([given[pre + n] for n in repl_names], repl_shapes, R_REPL) for pre in ("", "m_", "v_")),
        repl_shapes, "adamw_replicated")

    order = ["mem_norm_w", "w_mem_kv", "norm_pre", "norm_post", "w_out", "a_w_in", "a_ln_w", "a_ln_b", "a_w_s", "a_b_s",
             "b_w_in", "b_conv_w", "c_w_in", "c_conv_w", "c_a_log", "c_dt_bias", "c_o_norm_w"]
    outs = [loss, dx[None]]
    for kind in range(4):
        got = dict(zip(repl_names, res_repl[kind]))
        got["a_w_s"] = res_ws[kind].reshape(a_w_s.shape)
        got.update(zip(small_names, res_small[kind]))
        got["w_mem_kv"] = res[0][0][kind][:128]
        got["w_out"] = jnp.stack([res[0][0][kind][128:]] + [res[i][0][kind] for i in (1, 2, 3)])
        got["a_w_in"] = jnp.stack([res[0][1][kind], res[3][1][kind]])
        got["b_w_in"] = res[1][1][kind][None]
        got["c_w_in"] = res[2][1][kind][None]
        outs += [got[n] for n in order]
    return tuple(outs)
```

```python
import functools
import math

import jax
import jax.numpy as jnp
from jax import lax
from jax.experimental import pallas as pl
from jax.experimental.pallas import tpu as pltpu

F32 = jnp.float32
MXU = jnp.bfloat16
ACT = jnp.bfloat16

D = 1024
D_XA = 512
D_CAT = 1536
N_MEM = 256
XA_HEADS = 4
HEAD = 128
ZQ = D_CAT + D_XA
EPS = 1e-6
GM_CHUNK = 128
GM_CHUNKS_PER_STEP = 4
GM_GROUPS = 8
DN_HEADS = 8
DN_CHUNK = 64
N_DEV = 8
HALO = 8
VMEM_LIMIT = 56 * 1024 * 1024
XCH = jnp.bfloat16
R_REPL = 32
R_SMALL = 16

ADAM_LR = 0.001
ADAM_B1 = 0.9
ADAM_B2 = 0.999
ADAM_EPS = 1e-08
ADAM_WD = 0.01
ADAM_STEP = 10

NN = ((1,), (0,))
NT = ((1,), (1,))
TN = ((0,), (0,))
MESH = pl.DeviceIdType.MESH


def _pcall(body, **kw):
    return pl.pallas_call(body, **kw)


def _cp(n_axes):
    return pltpu.CompilerParams(dimension_semantics=("arbitrary",) * n_axes, vmem_limit_bytes=VMEM_LIMIT)


def _dot(a, b, dims, prec=None):
    return lax.dot_general(a, b, (dims, ((), ())), preferred_element_type=F32, precision=prec)


def _mdot(a, b, dims):
    return _dot(a.astype(MXU), b.astype(MXU), dims)


def _make_mms(raw):
    @jax.custom_vjp
    def nn(a, b):
        return raw(a, b, NN)

    @jax.custom_vjp
    def nt(a, b):
        return raw(a, b, NT)

    @jax.custom_vjp
    def tn(a, b):
        return raw(a, b, TN)

    nn.defvjp(lambda a, b: (nn(a, b), (a, b)), lambda r, g: (nt(g, r[1]), tn(r[0], g)))
    nt.defvjp(lambda a, b: (nt(a, b), (a, b)), lambda r, g: (nn(g, r[1]), tn(g, r[0])))
    tn.defvjp(lambda a, b: (tn(a, b), (a, b)), lambda r, g: (nt(r[1], g), nn(r[0], g)))
    return nn, nt, tn


_nn, _nt, _tn = _make_mms(_mdot)


def _split_dot(a, b, dims):
    ah = a.astype(jnp.bfloat16)
    bh = b.astype(jnp.bfloat16)
    al = (a - ah.astype(F32)).astype(jnp.bfloat16)
    bl = (b - bh.astype(F32)).astype(jnp.bfloat16)
    return _dot(ah, bh, dims) + (_dot(ah, bl, dims) + _dot(al, bh, dims))


_hnn, _hnt, _htn = _make_mms(_split_dot)


def _full(shape):
    return pl.BlockSpec(shape, lambda *_: (0,) * len(shape))


def _silu(z):
    return z * jax.nn.sigmoid(z)


def _dsilu(z):
    s = jax.nn.sigmoid(z)
    return s * (1.0 + z * (1.0 - s))


def _onehot_row(n, k):
    return (lax.broadcasted_iota(jnp.int32, (1, n), 1) == k).astype(F32)


_HBM = pl.BlockSpec(memory_space=pl.ANY)


def _sem_shapes(n_remote, n_local):
    return [pltpu.SemaphoreType.DMA((n_remote,)), pltpu.SemaphoreType.DMA((n_remote,)),
            pltpu.SemaphoreType.DMA((n_local,))]


def _gather_parts(x_refs, out_refs, send_sems, recv_sems, local_sems):
    n = len(x_refs)
    x, y, cc = lax.axis_index("x"), lax.axis_index("y"), lax.axis_index("c")
    me, sibling = (x, y, cc), (x, y, 1 - cc)
    chips = [(1 - x, y), (x, 1 - y), (1 - x, 1 - y)]

    def slot(a, px, py, pc):
        return out_refs[a].at[4 * px + 2 * py + pc]

    def copy(k, a, block, to, src=None):
        return pltpu.make_async_remote_copy(
            src_ref=slot(a, *block) if src is None else src, dst_ref=slot(a, *block),
            send_sem=send_sems.at[k * n + a], recv_sem=recv_sems.at[k * n + a], device_id=to, device_id_type=MESH)

    mine = [pltpu.make_async_copy(x_refs[a], slot(a, *me), local_sems.at[a]) for a in range(n)]
    first = [copy(0, a, me, sibling, src=x_refs[a]) for a in range(n)]
    first += [copy(1 + j, a, me, (*chip, cc), src=x_refs[a]) for j, chip in enumerate(chips) for a in range(n)]

    def begin():
        for cp in mine + first:
            cp.start()

    def end():
        passed = []
        for j, chip in enumerate(chips):
            for a in range(n):
                copy(1 + j, a, (*chip, cc), me).wait_recv()
                passed.append(copy(4 + j, a, (*chip, cc), sibling))
                passed[-1].start()
        for a in range(n):
            copy(0, a, sibling, me).wait_recv()
        for j, chip in enumerate(chips):
            for a in range(n):
                copy(4 + j, a, (*chip, 1 - cc), me).wait_recv()
        for cp in first + passed:
            cp.wait_send()
        for cp in mine:
            cp.wait()

    return begin, end


def _exchange_parts(g_refs, out_refs, b_refs, ball_refs, send_sems, recv_sems, local_sems):
    n, nb = len(g_refs), len(b_refs)
    per_peer = n + nb
    x, y, cc = lax.axis_index("x"), lax.axis_index("y"), lax.axis_index("c")
    my_idx = 4 * x + 2 * y + cc
    mine = [pltpu.make_async_copy(g_refs[a].at[my_idx], out_refs[a].at[my_idx], local_sems.at[a]) for a in range(n)]
    mine += [pltpu.make_async_copy(b_refs[a], ball_refs[a].at[my_idx], local_sems.at[n + a]) for a in range(nb)]
    copies = []
    for k in range(1, N_DEV):
        px = 1 - x if k & 4 else x
        py = 1 - y if k & 2 else y
        pc = 1 - cc if k & 1 else cc
        base = (k - 1) * per_peer
        for a in range(n):
            copies.append(pltpu.make_async_remote_copy(
                src_ref=g_refs[a].at[4 * px + 2 * py + pc], dst_ref=out_refs[a].at[my_idx],
                send_sem=send_sems.at[base + a], recv_sem=recv_sems.at[base + a],
                device_id=(px, py, pc), device_id_type=MESH))
        for a in range(nb):
            copies.append(pltpu.make_async_remote_copy(
                src_ref=b_refs[a], dst_ref=ball_refs[a].at[my_idx], send_sem=send_sems.at[base + n + a],
                recv_sem=recv_sems.at[base + n + a], device_id=(px, py, pc), device_id_type=MESH))

    def begin():
        for cp in mine + copies:
            cp.start()

    def end():
        for cp in copies:
            cp.wait_recv()
        for cp in copies:
            cp.wait_send()
        for cp in mine:
            cp.wait()

    return begin, end


def _pcall_hosting(body, args, exchange, bcast, *, name, grid, in_specs, out_specs, out_shape, compiler_params):
    n_in, n_out, n_x, n_b = len(in_specs), len(out_specs), len(exchange), len(bcast)
    n_c, n_steps = n_x + n_b, grid[0]

    def hosted(*refs):
        ins, c_in = refs[:n_in], refs[n_in:n_in + n_c]
        outs, c_out = refs[n_in + n_c:n_in + n_c + n_out], refs[n_in + n_c + n_out:n_in + 2 * n_c + n_out]
        begin, end = _exchange_parts(c_in[:n_x], c_out[:n_x], c_in[n_x:], c_out[n_x:], *refs[n_in + 2 * n_c + n_out:])
        pl.when(pl.program_id(0) == 0)(begin)
        body(*ins, *outs)
        pl.when(pl.program_id(0) == n_steps - 1)(end)

    landed_shape = [jax.ShapeDtypeStruct(e.shape, e.dtype) for e in exchange]
    landed_shape += [jax.ShapeDtypeStruct((N_DEV,) + b.shape, b.dtype) for b in bcast]
    res = _pcall(hosted, name=name, grid=grid, in_specs=list(in_specs) + [_HBM] * n_c,
                 out_specs=list(out_specs) + [_HBM] * n_c, out_shape=list(out_shape) + landed_shape,
                 scratch_shapes=_sem_shapes(7 * n_c, n_c), compiler_params=compiler_params)(*args, *exchange, *bcast)
    return res[:n_out], res[n_out:]


def _all_gather(blks, name):
    n = len(blks)

    def body(*refs):
        begin, end = _gather_parts(refs[:n], refs[n:2 * n], *refs[2 * n:])
        begin()
        end()

    return _pcall(body, name=name, out_shape=[jax.ShapeDtypeStruct((N_DEV,) + b.shape, b.dtype) for b in blks],
                  in_specs=[_HBM] * n, out_specs=[_HBM] * n, scratch_shapes=_sem_shapes(7 * n, n))(*blks)


def _sum_and_adamw(p_ref, w, m, v):
    g = p_ref[0].astype(F32)
    for s in range(1, N_DEV):
        g = g + p_ref[s].astype(F32)
    nm = ADAM_B1 * m + (1.0 - ADAM_B1) * g
    nv = ADAM_B2 * v + (1.0 - ADAM_B2) * (g * g)
    m_hat = nm / (1.0 - ADAM_B1 ** ADAM_STEP)
    v_hat = nv / (1.0 - ADAM_B2 ** ADAM_STEP)
    return g, -ADAM_LR * (m_hat / (jnp.sqrt(v_hat) + ADAM_EPS) + ADAM_WD * w), nm, nv


def _reduce_adamw_vectors(parts, w, m, v, shapes, name):
    table = _row_table(shapes)

    def body(p_ref, w_ref, m_ref, v_ref, *out_refs):
        results = _sum_and_adamw(p_ref, w_ref[...], m_ref[...], v_ref[...])
        for kind, val in enumerate(results):
            for t, (r0, nr, nc) in enumerate(table):
                out_refs[kind * len(table) + t][...] = val[r0:r0 + nr, 0:nc]

    outs = _pcall(body, name=name, out_shape=[jax.ShapeDtypeStruct((nr, nc), F32) for _, nr, nc in table] * 4)(parts, w, m, v)
    return [[outs[kind * len(table) + t].reshape(s) for t, s in enumerate(shapes)] for kind in range(4)]


def _reduce_adamw(parts, w, m, v, name):
    lead = w.shape[:-2]
    r, c = w.shape[-2:]
    tr = 128 if r % 128 == 0 else r
    zeros = (0,) * len(lead)
    at = zeros + (slice(None), slice(None))

    def body(p_ref, w_ref, m_ref, v_ref, g_ref, d_ref, nm_ref, nv_ref):
        g_ref[at], d_ref[at], nm_ref[at], nv_ref[at] = _sum_and_adamw(p_ref, w_ref[at], m_ref[at], v_ref[at])

    row = pl.BlockSpec((1,) * len(lead) + (tr, c), lambda i: zeros + (i, 0))
    out = jax.ShapeDtypeStruct(w.shape, F32)
    return _pcall(
        body, name=name, grid=(r // tr,),
        in_specs=[pl.BlockSpec((N_DEV, tr, c), lambda i: (0, i, 0)), row, row, row],
        out_specs=[row, row, row, row], out_shape=[out, out, out, out], compiler_params=_cp(1),
    )(parts, w, m, v)


def _proj_fwd(x, nw, w, wab, name, gather=()):
    t, npj = x.shape[0], w.shape[1]
    tm, tn = min(512, t), 1024
    has_ab = wab is not None
    n_in, n_out, n_g = 3 + has_ab, 2 + has_ab, len(gather)
    n_i = t // tm

    def body(*refs):
        ins, g_ins = refs[:n_in], refs[n_in:n_in + n_g]
        outs = refs[n_in + n_g:n_in + n_g + n_out]
        g_outs = refs[n_in + n_g + n_out:n_in + 2 * n_g + n_out]
        if has_ab:
            (x_ref, nw_ref, w_ref, wab_ref), (p_ref, h_ref, ab_ref) = ins, outs
        else:
            (x_ref, nw_ref, w_ref), (p_ref, h_ref) = ins, outs
        if n_g:
            begin, end = _gather_parts(g_ins, g_outs, *refs[n_in + 2 * n_g + n_out:])
            pl.when(pl.program_id(0) == 0)(begin)

        xv = x_ref[...]
        hv = (xv * lax.rsqrt(jnp.mean(xv * xv, axis=-1, keepdims=True) + EPS) * nw_ref[...]).astype(MXU)
        h_ref[...] = hv.astype(ACT)
        if has_ab:
            ab_ref[...] = _dot(hv, wab_ref[...], NN)
        for j in range(npj // tn):
            p_ref[:, j * tn:(j + 1) * tn] = _dot(hv, w_ref[:, j * tn:(j + 1) * tn], NN).astype(ACT)
        if n_g:
            pl.when(pl.program_id(0) == n_i - 1)(end)

    in_specs = [pl.BlockSpec((tm, D), lambda i: (i, 0)), _full((1, D)), _full((D, npj))]
    out_specs = [pl.BlockSpec((tm, npj), lambda i: (i, 0)), pl.BlockSpec((tm, D), lambda i: (i, 0))]
    out_shape = [jax.ShapeDtypeStruct((t, npj), ACT), jax.ShapeDtypeStruct((t, D), ACT)]
    args = [x, nw, w]
    if has_ab:
        in_specs.append(_full((D, HEAD)))
        out_specs.append(pl.BlockSpec((tm, HEAD), lambda i: (i, 0)))
        out_shape.append(jax.ShapeDtypeStruct((t, HEAD), F32))
        args.append(wab)
    scratch = []
    if n_g:
        in_specs += [_HBM] * n_g
        out_specs += [_HBM] * n_g
        out_shape += [jax.ShapeDtypeStruct((N_DEV,) + b.shape, b.dtype) for b in gather]
        args += list(gather)
        scratch += _sem_shapes(7 * n_g, n_g)
    return _pcall(body, name=name, grid=(n_i,), in_specs=in_specs, out_specs=out_specs,
                  out_shape=out_shape, scratch_shapes=scratch, compiler_params=_cp(1))(*args)


def _proj_bwd_x(dps, ws, x, dxn, nw, name, exchange=(), bcast=()):
    t = x.shape[0]
    tm = min(256, t)
    n, n_x, n_b = len(dps), len(exchange), len(bcast)
    n_c, n_steps = n_x + n_b, t // tm

    def body(*refs):
        dp_refs, w_refs = refs[:n], refs[n:2 * n]
        x_ref, dxn_ref, nw_ref = refs[2 * n:2 * n + 3]
        c_in = refs[2 * n + 3:2 * n + 3 + n_c]
        dx_ref, dnw_ref = refs[2 * n + 3 + n_c:2 * n + 5 + n_c]
        c_out = refs[2 * n + 5 + n_c:2 * n + 5 + 2 * n_c]
        if n_c:
            begin, end = _exchange_parts(c_in[:n_x], c_out[:n_x], c_in[n_x:], c_out[n_x:], *refs[2 * n + 5 + 2 * n_c:])
            pl.when(pl.program_id(0) == 0)(begin)
        dh = _mdot(dp_refs[0][...], w_refs[0][...], NT)
        for k in range(1, n):
            dh = dh + _mdot(dp_refs[k][...], w_refs[k][...], NT)
        xv = x_ref[...]
        r = lax.rsqrt(jnp.mean(xv * xv, axis=-1, keepdims=True) + EPS)

        @pl.when(pl.program_id(0) == 0)
        def _():
            dnw_ref[...] = jnp.zeros_like(dnw_ref)

        dnw_ref[...] += jnp.sum(dh * xv * r, axis=0, keepdims=True)
        dhw = dh * nw_ref[...]
        dx_ref[...] = dxn_ref[...] + r * (dhw - xv * (r * r) * jnp.mean(dhw * xv, axis=-1, keepdims=True))
        if n_c:
            pl.when(pl.program_id(0) == n_steps - 1)(end)

    row = pl.BlockSpec((tm, D), lambda i: (i, 0))
    in_specs = [pl.BlockSpec((tm, dp.shape[1]), lambda i: (i, 0)) for dp in dps]
    in_specs += [_full(w.shape) for w in ws]
    in_specs += [row, row, _full((1, D))] + [_HBM] * n_c
    out_shape = [jax.ShapeDtypeStruct((t, D), F32), jax.ShapeDtypeStruct((1, D), F32)]
    out_shape += [jax.ShapeDtypeStruct(e.shape, e.dtype) for e in exchange]
    out_shape += [jax.ShapeDtypeStruct((N_DEV,) + b.shape, b.dtype) for b in bcast]
    return _pcall(body, name=name, grid=(n_steps,), in_specs=in_specs, out_specs=[row, _full((1, D))] + [_HBM] * n_c,
                  out_shape=out_shape, scratch_shapes=_sem_shapes(7 * n_c, n_c) if n_c else [],
                  compiler_params=_cp(1))(*dps, *ws, x, dxn, nw, *exchange, *bcast)


def _matmul_tn(a, b, name, out_dtype=XCH):
    t, m = a.shape
    n = b.shape[1]
    tm, tn = min(1024, t), min(1024, n)
    n_t = t // tm

    def body(a_ref, b_ref, o_ref, acc):
        @pl.when(pl.program_id(1) == 0)
        def _():
            acc[...] = jnp.zeros_like(acc)

        acc[...] += _mdot(a_ref[...], b_ref[...], TN)

        @pl.when(pl.program_id(1) == n_t - 1)
        def _():
            o_ref[...] = acc[...].astype(out_dtype)

    return _pcall(body, name=name, grid=(n // tn, n_t),
                  in_specs=[pl.BlockSpec((tm, m), lambda j, i: (i, 0)), pl.BlockSpec((tm, tn), lambda j, i: (i, j))],
                  out_specs=pl.BlockSpec((m, tn), lambda j, i: (0, j)),
                  out_shape=jax.ShapeDtypeStruct((m, n), out_dtype), scratch_shapes=[pltpu.VMEM((m, tn), F32)],
                  compiler_params=_cp(2))(a, b)


def _memkv_fwd(mem, mw, wkv):
    def body(mem_ref, mw_ref, w_ref, kv_ref):
        mv = mem_ref[...]
        mn = mv * lax.rsqrt(jnp.mean(mv * mv, axis=-1, keepdims=True) + EPS) * mw_ref[...]
        kv_ref[...] = _mdot(mn, w_ref[...], NN)

    return _pcall(body, name="memkv_fwd", out_shape=jax.ShapeDtypeStruct((N_MEM, 2 * D_XA), F32),
                  compiler_params=pltpu.CompilerParams(vmem_limit_bytes=VMEM_LIMIT))(mem, mw, wkv)


def _memkv_bwd(mem, mw, wkv, dkv):
    def body(mem_ref, mw_ref, w_ref, dkv_ref, dw_ref, dmw_ref):
        mv = mem_ref[...]
        r = lax.rsqrt(jnp.mean(mv * mv, axis=-1, keepdims=True) + EPS)
        mn = mv * r * mw_ref[...]
        dkvv = dkv_ref[...]
        dw_ref[...] = _mdot(mn, dkvv, TN)
        dmn = _mdot(dkvv, w_ref[...], NT)
        dmw_ref[...] = jnp.sum(dmn * mv * r, axis=0, keepdims=True)

    return _pcall(body, name="memkv_bwd",
                  out_shape=[jax.ShapeDtypeStruct((D, 2 * D_XA), F32), jax.ShapeDtypeStruct((1, D), F32)],
                  compiler_params=pltpu.CompilerParams(vmem_limit_bytes=VMEM_LIMIT))(mem, mw, wkv, dkv)


def _attend(q, kv):
    ps, outs = [], []
    for hh in range(XA_HEADS):
        qh = q[:, hh * HEAD:(hh + 1) * HEAD]
        kh = kv[:, hh * HEAD:(hh + 1) * HEAD]
        vh = kv[:, D_XA + hh * HEAD:D_XA + (hh + 1) * HEAD]
        s = _mdot(qh, kh, NT) * (HEAD ** -0.5)
        s = s - jnp.max(s, axis=-1, keepdims=True)
        e = jnp.exp(s)
        p = e / jnp.sum(e, axis=-1, keepdims=True)
        ps.append(p)
        outs.append(_mdot(p, vh, NN))
    return ps, outs


def _tail_fwd(p, ymix, x, kv, wout, npost, name):
    t = x.shape[0]
    tm = min(512, t)

    def body(z_ref, q_ref, y_ref, x_ref, kv_ref, w_ref, np_ref, xn_ref, o_ref):
        _, outs = _attend(q_ref[...].astype(F32), kv_ref[...])
        cat = jnp.concatenate([y_ref[...].astype(F32)] + outs, axis=1)
        g = cat * _silu(z_ref[...].astype(F32))
        o = _mdot(g, w_ref[...], NN)
        o_ref[...] = o
        xn_ref[...] = x_ref[...] + o * lax.rsqrt(jnp.mean(o * o, axis=-1, keepdims=True) + EPS) * np_ref[...]

    row = pl.BlockSpec((tm, D), lambda i: (i, 0))
    return _pcall(
        body, name=name, grid=(t // tm,),
        in_specs=[pl.BlockSpec((tm, D_CAT), lambda i: (i, 0)), pl.BlockSpec((tm, D_XA), lambda i: (i, D_CAT // D_XA)),
                  row, row, _full((N_MEM, 2 * D_XA)), _full((D_CAT, D)), _full((1, D))],
        out_specs=[row, row],
        out_shape=[jax.ShapeDtypeStruct((t, D), F32), jax.ShapeDtypeStruct((t, D), F32)], compiler_params=_cp(1),
    )(p, p, ymix, x, kv, wout, npost)


def _tail_bwd(dxn, o, p, ymix, kv, wout, npost, name, exchange=()):
    t = dxn.shape[0]
    tm = min(256, t)
    n_x, n_steps = len(exchange), t // tm

    def body(*refs):
        dxn_ref, o_ref, z_ref, q_ref, y_ref, kv_ref, w_ref, np_ref = refs[:8]
        dzq_ref, dy_ref, dw_ref, dnp_ref, dkv_ref = refs[8 + n_x:13 + n_x]
        if n_x:
            begin, end = _exchange_parts(refs[8:8 + n_x], refs[13 + n_x:13 + 2 * n_x], (), (), *refs[13 + 2 * n_x:])
            pl.when(pl.program_id(0) == 0)(begin)

        @pl.when(pl.program_id(0) == 0)
        def _():
            dw_ref[...] = jnp.zeros_like(dw_ref)
            dnp_ref[...] = jnp.zeros_like(dnp_ref)
            dkv_ref[...] = jnp.zeros_like(dkv_ref)

        q = q_ref[...].astype(F32)
        kvv = kv_ref[...]
        z = z_ref[...].astype(F32)
        ps, outs = _attend(q, kvv)
        cat = jnp.concatenate([y_ref[...].astype(F32)] + outs, axis=1)
        sz = _silu(z)
        g = cat * sz
        ov = o_ref[...]
        dr = dxn_ref[...]
        rr = lax.rsqrt(jnp.mean(ov * ov, axis=-1, keepdims=True) + EPS)
        dnp_ref[...] += jnp.sum(dr * ov * rr, axis=0, keepdims=True)
        dow = dr * np_ref[...]
        do = rr * (dow - ov * (rr * rr) * jnp.mean(dow * ov, axis=-1, keepdims=True))
        dg = _mdot(do, w_ref[...], NT)
        dw_ref[...] += _mdot(g, do, TN)
        dcat = dg * sz
        dzq_ref[:, 0:D_CAT] = (dg * cat * _dsilu(z)).astype(ACT)
        dy_ref[...] = dcat[:, 0:D].astype(ACT)
        for hh in range(XA_HEADS):
            lo = hh * HEAD
            doh = dcat[:, D + lo:D + lo + HEAD]
            qh = q[:, lo:lo + HEAD]
            kh = kvv[:, lo:lo + HEAD]
            vh = kvv[:, D_XA + lo:D_XA + lo + HEAD]
            ph = ps[hh]
            dp = _mdot(doh, vh, NT)
            ds = ph * (dp - jnp.sum(dp * ph, axis=-1, keepdims=True))
            dzq_ref[:, D_CAT + lo:D_CAT + lo + HEAD] = (_mdot(ds, kh, NN) * (HEAD ** -0.5)).astype(ACT)
            dkv_ref[:, lo:lo + HEAD] += _mdot(ds, qh, TN) * (HEAD ** -0.5)
            dkv_ref[:, D_XA + lo:D_XA + lo + HEAD] += _mdot(ph, doh, TN)
        if n_x:
            pl.when(pl.program_id(0) == n_steps - 1)(end)

    row = pl.BlockSpec((tm, D), lambda i: (i, 0))
    return _pcall(
        body, name=name, grid=(n_steps,),
        in_specs=[row, row, pl.BlockSpec((tm, D_CAT), lambda i: (i, 0)),
                  pl.BlockSpec((tm, D_XA), lambda i: (i, D_CAT // D_XA)), row,
                  _full((N_MEM, 2 * D_XA)), _full((D_CAT, D)), _full((1, D))] + [_HBM] * n_x,
        out_specs=[pl.BlockSpec((tm, ZQ), lambda i: (i, 0)), row, _full((D_CAT, D)), _full((1, D)),
                   _full((N_MEM, 2 * D_XA))] + [_HBM] * n_x,
        out_shape=[jax.ShapeDtypeStruct((t, ZQ), ACT), jax.ShapeDtypeStruct((t, D), ACT),
                   jax.ShapeDtypeStruct((D_CAT, D), F32), jax.ShapeDtypeStruct((1, D), F32),
                   jax.ShapeDtypeStruct((N_MEM, 2 * D_XA), F32)]
        + [jax.ShapeDtypeStruct(e.shape, e.dtype) for e in exchange],
        scratch_shapes=_sem_shapes(7 * n_x, n_x) if n_x else [], compiler_params=_cp(1),
    )(dxn, o, p, p, ymix, kv, wout, npost, *exchange)


def _gmlp_chunk(us, vs, lnws, lnbs, wss, bss):
    gv = [jax.nn.gelu(v) for v in vs]
    mean = sum(jnp.sum(v, axis=-1, keepdims=True) for v in gv) / D
    cen = [v - mean for v in gv]
    var = sum(jnp.sum(c * c, axis=-1, keepdims=True) for c in cen) / D
    rstd = lax.rsqrt(var + EPS)
    row = lax.broadcasted_iota(jnp.int32, (GM_CHUNK, GM_CHUNK), 0)
    col = lax.broadcasted_iota(jnp.int32, (GM_CHUNK, GM_CHUNK), 1)
    ys = []
    for g in range(GM_GROUPS):
        vn = cen[g] * rstd * lnws[g] + lnbs[g]
        sp = _nn(jnp.where(row >= col, wss[g], 0.0), vn) + bss[g]
        ys.append(jax.nn.gelu(us[g]) * sp)
    return ys


def _split_cols(v, n, width=HEAD):
    return [v[:, k * width:(k + 1) * width] for k in range(n)]


def _gmlp_operands(u_ref, v_ref, lnw_ref, lnb_ref, ws_ref, bst_ref, r0):
    us = _split_cols(u_ref[r0:r0 + GM_CHUNK, :].astype(F32), GM_GROUPS)
    vs = _split_cols(v_ref[r0:r0 + GM_CHUNK, :].astype(F32), GM_GROUPS)
    lnws = _split_cols(lnw_ref[...], GM_GROUPS)
    lnbs = _split_cols(lnb_ref[...], GM_GROUPS)
    wss = [ws_ref[g] for g in range(GM_GROUPS)]
    bst = bst_ref[...]
    bss = [jnp.sum(bst * _onehot_row(HEAD, g), axis=1, keepdims=True) for g in range(GM_GROUPS)]
    return us, vs, lnws, lnbs, wss, bss


def _gmlp_rows(t):
    return min(GM_CHUNKS_PER_STEP, t // GM_CHUNK) * GM_CHUNK


def _gmlp_specs(rows):
    return [pl.BlockSpec((rows, D), lambda i: (i, ZQ // D)), pl.BlockSpec((rows, D), lambda i: (i, ZQ // D + 1)),
            _full((1, D)), _full((1, D)), _full((GM_GROUPS, GM_CHUNK, GM_CHUNK)), _full((GM_CHUNK, HEAD))]


def _gmlp_fwd(p, lnw, lnb, ws, bst, name):
    t = p.shape[0]
    rows = _gmlp_rows(t)

    def body(u_ref, v_ref, lnw_ref, lnb_ref, ws_ref, bst_ref, y_ref):
        for r0 in range(0, rows, GM_CHUNK):
            ys = _gmlp_chunk(*_gmlp_operands(u_ref, v_ref, lnw_ref, lnb_ref, ws_ref, bst_ref, r0))
            for g in range(GM_GROUPS):
                y_ref[r0:r0 + GM_CHUNK, g * HEAD:(g + 1) * HEAD] = ys[g].astype(ACT)

    return _pcall(body, name=name, grid=(t // rows,), in_specs=_gmlp_specs(rows),
                  out_specs=pl.BlockSpec((rows, D), lambda i: (i, 0)),
                  out_shape=jax.ShapeDtypeStruct((t, D), ACT), compiler_params=_cp(1))(p, p, lnw, lnb, ws, bst)


def _gmlp_bwd(p, dy, lnw, lnb, ws, bst, name, exchange=(), bcast=()):
    t = p.shape[0]
    rows = _gmlp_rows(t)

    def body(u_ref, v_ref, lnw_ref, lnb_ref, ws_ref, bst_ref, dy_ref, duv_ref, dlnw_ref, dlnb_ref, dws_ref, dbst_ref):
        @pl.when(pl.program_id(0) == 0)
        def _():
            dlnw_ref[...] = jnp.zeros_like(dlnw_ref)
            dlnb_ref[...] = jnp.zeros_like(dlnb_ref)
            dws_ref[...] = jnp.zeros_like(dws_ref)
            dbst_ref[...] = jnp.zeros_like(dbst_ref)

        for r0 in range(0, rows, GM_CHUNK):
            ops = _gmlp_operands(u_ref, v_ref, lnw_ref, lnb_ref, ws_ref, bst_ref, r0)
            _, vjp = jax.vjp(_gmlp_chunk, *ops)
            dus, dvs, dlnws, dlnbs, dwss, dbss = vjp(_split_cols(dy_ref[r0:r0 + GM_CHUNK, :].astype(F32), GM_GROUPS))
            dbst = jnp.zeros((GM_CHUNK, HEAD), F32)
            for g in range(GM_GROUPS):
                lo = g * HEAD
                duv_ref[r0:r0 + GM_CHUNK, lo:lo + HEAD] = dus[g].astype(ACT)
                duv_ref[r0:r0 + GM_CHUNK, D + lo:D + lo + HEAD] = dvs[g].astype(ACT)
                dlnw_ref[:, lo:lo + HEAD] += dlnws[g]
                dlnb_ref[:, lo:lo + HEAD] += dlnbs[g]
                dws_ref[g] += dwss[g]
                dbst = dbst + dbss[g] * _onehot_row(HEAD, g)
            dbst_ref[...] += dbst

    call = dict(
        name=name, grid=(t // rows,),
        in_specs=_gmlp_specs(rows) + [pl.BlockSpec((rows, D), lambda i: (i, 0))],
        out_specs=[pl.BlockSpec((rows, 2 * D), lambda i: (i, 0)), _full((1, D)), _full((1, D)),
                   _full((GM_GROUPS, GM_CHUNK, GM_CHUNK)), _full((GM_CHUNK, HEAD))],
        out_shape=[jax.ShapeDtypeStruct((t, 2 * D), ACT), jax.ShapeDtypeStruct((1, D), F32),
                   jax.ShapeDtypeStruct((1, D), F32), jax.ShapeDtypeStruct((GM_GROUPS, GM_CHUNK, GM_CHUNK), F32),
                   jax.ShapeDtypeStruct((GM_CHUNK, HEAD), F32)],
        compiler_params=_cp(1))
    args = (p, p, lnw, lnb, ws, bst, dy)
    if not exchange and not bcast:
        return _pcall(body, **call)(*args), ()
    return _pcall_hosting(body, args, exchange, bcast, **call)


def _prev_halo(tm, col):
    return pl.BlockSpec((HALO, D), lambda i: (jnp.maximum(i * (tm // HALO) - 1, 0), col))


def _next_halo(tm, col, n_tiles):
    return pl.BlockSpec((HALO, D), lambda i: (jnp.minimum(i + 1, n_tiles - 1) * (tm // HALO), col))


def _taps_back(ext, w, width):
    acc = None
    for k in range(width):
        s = width - 1 - k
        term = w[k:k + 1, :] * (pltpu.roll(ext, s, 0) if s else ext)[HALO:, :]
        acc = term if acc is None else acc + term
    return acc


def _taps_fwd(ext, w, width, n):
    rows = ext.shape[0]
    acc = None
    for k in range(width):
        s = width - 1 - k
        term = w[k:k + 1, :] * (pltpu.roll(ext, rows - s, 0) if s else ext)[0:n, :]
        acc = term if acc is None else acc + term
    return acc


def _sconv_fwd(p, cw, name):
    t = p.shape[0]
    tm = min(256, t)

    def body(b_ref, c_ref, h_ref, cp_ref, hp_ref, w_ref, y_ref):
        first = pl.program_id(0) == 0
        prev = jnp.where(first, 0.0, cp_ref[...].astype(F32) * hp_ref[...].astype(F32))
        ext = jnp.concatenate([prev, c_ref[...].astype(F32) * h_ref[...].astype(F32)], axis=0)
        y_ref[...] = (b_ref[...].astype(F32) * _taps_back(ext, w_ref[...], 3)).astype(ACT)

    c0 = ZQ // D
    tile = [pl.BlockSpec((tm, D), lambda i, c=c: (i, c)) for c in (c0, c0 + 1, c0 + 2)]
    return _pcall(body, name=name, grid=(t // tm,),
                  in_specs=tile + [_prev_halo(tm, c0 + 1), _prev_halo(tm, c0 + 2), _full((HALO, D))],
                  out_specs=pl.BlockSpec((tm, D), lambda i: (i, 0)),
                  out_shape=jax.ShapeDtypeStruct((t, D), ACT), compiler_params=_cp(1))(p, p, p, p, p, cw)


def _sconv_bwd(p, dy, cw, name):
    t = p.shape[0]
    tm = min(256, t)
    n_tiles = t // tm

    def body(b_ref, c_ref, h_ref, cp_ref, hp_ref, bn_ref, dy_ref, dyn_ref, w_ref, d_ref, dw_ref):
        i = pl.program_id(0)

        @pl.when(i == 0)
        def _():
            dw_ref[...] = jnp.zeros_like(dw_ref)

        w = w_ref[...]
        bv, cv, hv = b_ref[...].astype(F32), c_ref[...].astype(F32), h_ref[...].astype(F32)
        dyv = dy_ref[...].astype(F32)
        prev = jnp.where(i == 0, 0.0, cp_ref[...].astype(F32) * hp_ref[...].astype(F32))
        ext = jnp.concatenate([prev, cv * hv], axis=0)
        conv = _taps_back(ext, w, 3)
        dconv = dyv * bv
        nxt = jnp.where(i == n_tiles - 1, 0.0, dyn_ref[...].astype(F32) * bn_ref[...].astype(F32))
        dc = _taps_fwd(jnp.concatenate([dconv, nxt], axis=0), w, 3, tm)
        d_ref[:, 0:D] = (dyv * conv).astype(ACT)
        d_ref[:, D:2 * D] = (dc * hv).astype(ACT)
        d_ref[:, 2 * D:3 * D] = (dc * cv).astype(ACT)
        for k in range(3):
            s = 2 - k
            shifted = (pltpu.roll(ext, s, 0) if s else ext)[HALO:, :]
            dw_ref[k:k + 1, :] += jnp.sum(dconv * shifted, axis=0, keepdims=True)

    c0 = ZQ // D
    tile = [pl.BlockSpec((tm, D), lambda i, c=c: (i, c)) for c in (c0, c0 + 1, c0 + 2)]
    return _pcall(
        body, name=name, grid=(n_tiles,),
        in_specs=tile + [_prev_halo(tm, c0 + 1), _prev_halo(tm, c0 + 2), _next_halo(tm, c0, n_tiles),
                         pl.BlockSpec((tm, D), lambda i: (i, 0)), _next_halo(tm, 0, n_tiles), _full((HALO, D))],
        out_specs=[pl.BlockSpec((tm, 3 * D), lambda i: (i, 0)), _full((HALO, D))],
        out_shape=[jax.ShapeDtypeStruct((t, 3 * D), ACT), jax.ShapeDtypeStruct((HALO, D), F32)],
        compiler_params=_cp(1),
    )(p, p, p, p, p, p, dy, dy, cw)


def _l2_heads(s, scale):
    outs, rs = [], []
    for hh in range(DN_HEADS):
        blk = s[:, hh * HEAD:(hh + 1) * HEAD]
        r = lax.rsqrt(jnp.sum(blk * blk, axis=-1, keepdims=True) + EPS)
        outs.append(blk * (r * scale))
        rs.append(r)
    return outs, rs


_QKV_SCALE = (HEAD ** -0.5, 1.0, None)


def _qkv_fwd(p, cw, name):
    t = p.shape[0]
    tm = min(256, t)

    def body(q_ref, k_ref, v_ref, qp_ref, kp_ref, vp_ref, w_ref, o_ref):
        first = pl.program_id(0) == 0
        for part, (ref, pref) in enumerate(((q_ref, qp_ref), (k_ref, kp_ref), (v_ref, vp_ref))):
            prev = jnp.where(first, 0.0, pref[...].astype(F32))
            ext = jnp.concatenate([prev, ref[...].astype(F32)], axis=0)
            s = _silu(_taps_back(ext, w_ref[:, part * D:(part + 1) * D], 4))
            if _QKV_SCALE[part] is None:
                o_ref[:, part * D:(part + 1) * D] = s.astype(ACT)
            else:
                outs, _ = _l2_heads(s, _QKV_SCALE[part])
                for hh in range(DN_HEADS):
                    o_ref[:, part * D + hh * HEAD:part * D + (hh + 1) * HEAD] = outs[hh].astype(ACT)

    c0 = ZQ // D
    tile = [pl.BlockSpec((tm, D), lambda i, c=c: (i, c)) for c in (c0, c0 + 1, c0 + 2)]
    halo = [_prev_halo(tm, c) for c in (c0, c0 + 1, c0 + 2)]
    return _pcall(body, name=name, grid=(t // tm,), in_specs=tile + halo + [_full((HALO, 3 * D))],
                  out_specs=pl.BlockSpec((tm, 3 * D), lambda i: (i, 0)),
                  out_shape=jax.ShapeDtypeStruct((t, 3 * D), ACT), compiler_params=_cp(1))(p, p, p, p, p, p, cw)


def _qkv_bwd(p, dqkv, cw, name):
    t = p.shape[0]
    tm = min(256, t)
    n_tiles = t // tm

    def body(*refs):
        tiles, prevs, nexts = refs[0:3], refs[3:6], refs[6:9]
        d_tiles, d_nexts = refs[9:12], refs[12:15]
        w_ref, o_ref, dw_ref = refs[15:]
        i = pl.program_id(0)

        @pl.when(i == 0)
        def _():
            dw_ref[...] = jnp.zeros_like(dw_ref)

        for part in range(3):
            w = w_ref[:, part * D:(part + 1) * D]
            prev = jnp.where(i == 0, 0.0, prevs[part][...].astype(F32))
            ext = jnp.concatenate([prev, tiles[part][...].astype(F32), nexts[part][...].astype(F32)], axis=0)
            xc = _taps_back(ext, w, 4)
            dout = jnp.concatenate([d_tiles[part][...].astype(F32), d_nexts[part][...].astype(F32)], axis=0)
            s = _silu(xc)
            if _QKV_SCALE[part] is None:
                ds = dout
            else:
                scale = _QKV_SCALE[part]
                pieces = []
                for hh in range(DN_HEADS):
                    blk = s[:, hh * HEAD:(hh + 1) * HEAD]
                    dblk = dout[:, hh * HEAD:(hh + 1) * HEAD]
                    r = lax.rsqrt(jnp.sum(blk * blk, axis=-1, keepdims=True) + EPS)
                    pieces.append(scale * r * (dblk - blk * (r * r) * jnp.sum(dblk * blk, axis=-1, keepdims=True)))
                ds = jnp.concatenate(pieces, axis=1)
            dxc = ds * _dsilu(xc)
            row = lax.broadcasted_iota(jnp.int32, (tm + HALO, 1), 0)
            dxc = jnp.where(jnp.logical_and(i == n_tiles - 1, row >= tm), 0.0, dxc)
            o_ref[:, part * D:(part + 1) * D] = _taps_fwd(dxc, w, 4, tm).astype(ACT)
            for k in range(4):
                sh = 3 - k
                shifted = (pltpu.roll(ext, sh, 0) if sh else ext)[HALO:HALO + tm, :]
                dw_ref[k:k + 1, part * D:(part + 1) * D] += jnp.sum(dxc[0:tm, :] * shifted, axis=0, keepdims=True)

    c0 = ZQ // D
    cols = (c0, c0 + 1, c0 + 2)
    tile = [pl.BlockSpec((tm, D), lambda i, c=c: (i, c)) for c in cols]
    dtile = [pl.BlockSpec((tm, D), lambda i, c=c: (i, c)) for c in range(3)]
    in_specs = (tile + [_prev_halo(tm, c) for c in cols] + [_next_halo(tm, c, n_tiles) for c in cols]
                + dtile + [_next_halo(tm, c, n_tiles) for c in range(3)] + [_full((HALO, 3 * D))])
    return _pcall(
        body, name=name, grid=(n_tiles,), in_specs=in_specs,
        out_specs=[pl.BlockSpec((tm, 3 * D), lambda i: (i, 0)), _full((HALO, 3 * D))],
        out_shape=[jax.ShapeDtypeStruct((t, 3 * D), ACT), jax.ShapeDtypeStruct((HALO, 3 * D), F32)],
        compiler_params=_cp(1),
    )(*([p] * 9), *([dqkv] * 6), cw)


def _tri_masks(n):
    row = lax.broadcasted_iota(jnp.int32, (n, n), 0)
    col = lax.broadcasted_iota(jnp.int32, (n, n), 1)
    return row, col


@jax.custom_vjp
def _unit_lower_inverses(mats):
    n = DN_CHUNK
    row, col = _tri_masks(n)
    eye = (row == col).astype(F32)
    same16 = (row // 16) == (col // 16)
    same32 = (row // 32) == (col // 32)
    pw = [jnp.where(same16, a, 0.0) for a in mats]
    x = [eye - p for p in pw]
    for _ in range(3):
        pw = [_hnn(p, p) for p in pw]
        x = [_hnn(xi, eye + p) for xi, p in zip(x, pw)]
    for keep in (jnp.logical_and(same32, jnp.logical_not(same16)), jnp.logical_not(same32)):
        inner = [_hnn(jnp.where(keep, a, 0.0), xi) for a, xi in zip(mats, x)]
        x = [xi - _hnn(xi, y) for xi, y in zip(x, inner)]
    return tuple(x)


def _uli_fwd(mats):
    t = _unit_lower_inverses(mats)
    return t, t


def _uli_bwd(ts, gs):
    inner = [_hnt(g, t) for g, t in zip(gs, ts)]
    return (tuple(-_htn(t, y) for t, y in zip(ts, inner)),)


_unit_lower_inverses.defvjp(_uli_fwd, _uli_bwd)


@jax.custom_vjp
def _known_inverses(mats, ts):
    return ts


_known_inverses.defvjp(lambda mats, ts: (ts, ts),
                       lambda ts, gs: (_uli_bwd(ts, gs)[0], tuple(jnp.zeros_like(t) for t in ts)))


def _pick_col(m, k):
    return jnp.sum(m * _onehot_row(m.shape[1], k), axis=1, keepdims=True)


def _pick_row(m, k):
    hot = (lax.broadcasted_iota(jnp.int32, (m.shape[0], 1), 0) == k).astype(F32)
    return jnp.sum(m * hot, axis=0, keepdims=True)


def _delta_chunk(states, qs, ks, vs, ab, alog, dtb, onw, known_inverses=None):
    n = DN_CHUNK
    heads = range(DN_HEADS)
    row, col = _tri_masks(n)
    incl = row >= col
    lane = lax.broadcasted_iota(jnp.int32, (1, HEAD), 1)
    g_all = jnp.where(lane < DN_HEADS, -jnp.exp(alog) * jax.nn.softplus(ab + dtb), 0.0)
    c_cols = _hnn(incl.astype(F32), g_all)
    c_rows = _htn(g_all, (row <= col).astype(F32))
    g_tot = jnp.sum(g_all, axis=0, keepdims=True)
    beta_all = jax.nn.sigmoid(ab)
    ccol = [_pick_col(c_cols, h) for h in heads]
    crow = [_pick_row(c_rows, h) for h in heads]
    gl = [_pick_col(g_tot, h) for h in heads]
    beta = [_pick_col(beta_all, DN_HEADS + h) for h in heads]
    decay = [jnp.exp(jnp.where(incl, ccol[h] - crow[h], -1e30)) for h in heads]
    eg = [jnp.exp(ccol[h]) for h in heads]
    kb = [ks[h] * beta[h] for h in heads]
    amat = [jnp.where(row > col, _nt(kb[h], ks[h]) * decay[h], 0.0) for h in heads]
    if known_inverses is None:
        tmat = _unit_lower_inverses(tuple(amat))
    else:
        tmat = _known_inverses(tuple(amat), tuple(known_inverses))
    u = [_nn(tmat[h], vs[h] * beta[h]) for h in heads]
    w = [_nn(tmat[h], kb[h] * eg[h]) for h in heads]
    qk = [_nt(qs[h], ks[h]) * decay[h] for h in heads]
    v_new = [u[h] - _nn(w[h], states[h]) for h in heads]
    o = [_nn(qs[h] * eg[h], states[h]) + _nn(qk[h], v_new[h]) for h in heads]
    new_states = [states[h] * jnp.exp(gl[h]) + _tn(ks[h] * jnp.exp(gl[h] - ccol[h]), v_new[h]) for h in heads]
    ys = [o[h] * lax.rsqrt(jnp.mean(o[h] * o[h], axis=-1, keepdims=True) + EPS) * onw for h in heads]
    return (ys, new_states), tmat


def _head_cols(ref):
    return [ref[:, h * HEAD:(h + 1) * HEAD].astype(F32) for h in range(DN_HEADS)]


def _delta_fwd(qkv, ab, alog, dtb, onw, name):
    t = qkv.shape[0]
    nc = t // DN_CHUNK

    def body(q_ref, k_ref, v_ref, ab_ref, alog_ref, dtb_ref, onw_ref, y_ref, keep_ref, inv_ref, state):
        @pl.when(pl.program_id(0) == 0)
        def _():
            state[...] = jnp.zeros_like(state)

        s0 = [state[hh] for hh in range(DN_HEADS)]
        (ys, s1), tmat = _delta_chunk(s0, _head_cols(q_ref), _head_cols(k_ref), _head_cols(v_ref), ab_ref[...],
                                      alog_ref[...], dtb_ref[...], onw_ref[...])
        for hh in range(DN_HEADS):
            keep_ref[0, hh] = s0[hh]
            inv_ref[0, hh] = tmat[hh]
            state[hh] = s1[hh]
            y_ref[:, hh * HEAD:(hh + 1) * HEAD] = ys[hh].astype(ACT)

    chunk = [pl.BlockSpec((DN_CHUNK, D), lambda i, c=c: (i, c)) for c in range(3)]
    return _pcall(
        body, name=name, grid=(nc,),
        in_specs=chunk + [pl.BlockSpec((DN_CHUNK, HEAD), lambda i: (i, 0)), _full((1, HEAD)), _full((1, HEAD)),
                          _full((1, HEAD))],
        out_specs=[pl.BlockSpec((DN_CHUNK, D), lambda i: (i, 0)),
                   pl.BlockSpec((1, DN_HEADS, HEAD, HEAD), lambda i: (i, 0, 0, 0)),
                   pl.BlockSpec((1, DN_HEADS, DN_CHUNK, DN_CHUNK), lambda i: (i, 0, 0, 0))],
        out_shape=[jax.ShapeDtypeStruct((t, D), ACT), jax.ShapeDtypeStruct((nc, DN_HEADS, HEAD, HEAD), F32),
                   jax.ShapeDtypeStruct((nc, DN_HEADS, DN_CHUNK, DN_CHUNK), F32)],
        scratch_shapes=[pltpu.VMEM((DN_HEADS, HEAD, HEAD), F32)], compiler_params=_cp(1),
    )(qkv, qkv, qkv, ab, alog, dtb, onw)


def _delta_bwd(qkv, ab, alog, dtb, onw, keep, inv, dy, name):
    t = qkv.shape[0]
    nc = t // DN_CHUNK

    def body(q_ref, k_ref, v_ref, ab_ref, alog_ref, dtb_ref, onw_ref, keep_ref, inv_ref, dy_ref,
             dqkv_ref, dab_ref, dalog_ref, ddtb_ref, donw_ref, dstate):
        @pl.when(pl.program_id(0) == 0)
        def _():
            dstate[...] = jnp.zeros_like(dstate)
            dalog_ref[...] = jnp.zeros_like(dalog_ref)
            ddtb_ref[...] = jnp.zeros_like(ddtb_ref)
            donw_ref[...] = jnp.zeros_like(donw_ref)

        s0 = [keep_ref[0, hh] for hh in range(DN_HEADS)]
        known = [inv_ref[0, hh] for hh in range(DN_HEADS)]
        _, vjp, _ = jax.vjp(functools.partial(_delta_chunk, known_inverses=known), s0, _head_cols(q_ref),
                            _head_cols(k_ref), _head_cols(v_ref), ab_ref[...], alog_ref[...], dtb_ref[...], onw_ref[...],
                            has_aux=True)
        ds, dq, dk, dv, dab, dal, ddt, don = vjp((_head_cols(dy_ref), [dstate[hh] for hh in range(DN_HEADS)]))
        for hh in range(DN_HEADS):
            lo = hh * HEAD
            dstate[hh] = ds[hh]
            dqkv_ref[:, lo:lo + HEAD] = dq[hh].astype(ACT)
            dqkv_ref[:, D + lo:D + lo + HEAD] = dk[hh].astype(ACT)
            dqkv_ref[:, 2 * D + lo:2 * D + lo + HEAD] = dv[hh].astype(ACT)
        dab_ref[...] = dab
        dalog_ref[...] += dal
        ddtb_ref[...] += ddt
        donw_ref[...] += don

    rev = lambda i: nc - 1 - i
    chunk = [pl.BlockSpec((DN_CHUNK, D), lambda i, c=c: (rev(i), c)) for c in range(3)]
    small = jax.ShapeDtypeStruct((1, HEAD), F32)
    return _pcall(
        body, name=name, grid=(nc,),
        in_specs=chunk + [pl.BlockSpec((DN_CHUNK, HEAD), lambda i: (rev(i), 0)), _full((1, HEAD)), _full((1, HEAD)),
                          _full((1, HEAD)), pl.BlockSpec((1, DN_HEADS, HEAD, HEAD), lambda i: (rev(i), 0, 0, 0)),
                          pl.BlockSpec((1, DN_HEADS, DN_CHUNK, DN_CHUNK), lambda i: (rev(i), 0, 0, 0)),
                          pl.BlockSpec((DN_CHUNK, D), lambda i: (rev(i), 0))],
        out_specs=[pl.BlockSpec((DN_CHUNK, 3 * D), lambda i: (rev(i), 0)),
                   pl.BlockSpec((DN_CHUNK, HEAD), lambda i: (rev(i), 0)), _full((1, HEAD)), _full((1, HEAD)),
                   _full((1, HEAD))],
        out_shape=[jax.ShapeDtypeStruct((t, 3 * D), ACT), jax.ShapeDtypeStruct((t, HEAD), F32), small, small, small],
        scratch_shapes=[pltpu.VMEM((DN_HEADS, HEAD, HEAD), F32)], compiler_params=_cp(1),
    )(qkv, qkv, qkv, ab, alog, dtb, onw, keep, inv, dy)


def _loss_head(y, target):
    t = y.shape[0]
    tm = min(512, t)

    def body(y_ref, t_ref, l_ref, dy_ref):
        @pl.when(pl.program_id(0) == 0)
        def _():
            l_ref[...] = jnp.zeros_like(l_ref)

        diff = y_ref[...] - t_ref[...]
        dy_ref[...] = diff * (1.0 / D)
        l_ref[...] += 0.5 * jnp.sum(jnp.sum(diff * diff, axis=-1, keepdims=True) * (1.0 / D), axis=0, keepdims=True)

    row = pl.BlockSpec((tm, D), lambda i: (i, 0))
    return _pcall(body, name="loss_head", grid=(t // tm,), in_specs=[row, row], out_specs=[_full((8, 128)), row],
                  out_shape=[jax.ShapeDtypeStruct((8, 128), F32), jax.ShapeDtypeStruct((t, D), F32)],
                  compiler_params=_cp(1))(y, target)


_SMALL_SHARDED = (("a_ln_w", (2, 128), 1), ("a_ln_b", (2, 128), 1), ("b_conv_w", (1, 3, 128), 2),
                  ("c_conv_w", (1, 4, 384), 2))
_REPLICATED = (("mem_norm_w", (1024,)), ("norm_pre", (4, 1024)), ("norm_post", (4, 1024)),
               ("a_b_s", (2, 8, 128)), ("c_a_log", (1, 8)), ("c_dt_bias", (1, 8)), ("c_o_norm_w", (1, 128)))


def _layer_shards(given, prefix):
    w_out = given[prefix + "w_out"]
    w_ins = [given[prefix + "a_w_in"][0], given[prefix + "b_w_in"], given[prefix + "c_w_in"], given[prefix + "a_w_in"][1]]
    return [[w_out[i], w_ins[i]] for i in range(4)]


def _as_matmul_operand(w):
    return w.reshape(w.shape[-2:]).astype(MXU)


def _as_rows(shape):
    return (math.prod(shape[:-1]) if len(shape) > 1 else 1, shape[-1])


def _row_table(shapes):
    table, at = [], 0
    for shape in shapes:
        rows, cols = _as_rows(shape)
        table.append((at, rows, cols))
        at += rows
    return table


def _pack_rows(arrays, shapes, rows, lead=()):
    n_lead = len(lead)
    parts = [jnp.pad(a.reshape(lead + _as_rows(s)), [(0, 0)] * (n_lead + 1) + [(0, D - s[-1])])
             for a, s in zip(arrays, shapes)]
    block = jnp.concatenate(parts, axis=n_lead)
    return jnp.pad(block, [(0, 0)] * n_lead + [(0, rows - block.shape[n_lead]), (0, 0)])


def _unpack_rows(packed, shapes, lead=()):
    n_lead = len(lead)
    return [packed[(slice(None),) * n_lead + (slice(r0, r0 + nr), slice(0, nc))].reshape(lead + s)
            for (r0, nr, nc), s in zip(_row_table(shapes), shapes)]


def _join_shards(blocks, axis):
    moved = jnp.moveaxis(blocks, 0, axis)
    shape = moved.shape
    return moved.reshape(shape[:axis] + (shape[axis] * shape[axis + 1],) + shape[axis + 2:])


def _split_shards(full, axis):
    shape = full.shape
    split = full.reshape(shape[:axis] + (N_DEV, shape[axis] // N_DEV) + shape[axis + 1:])
    return jnp.moveaxis(split, axis, 0)


_A_COLS = ((2560, 1536), (2048, 512), (0, 2048))
_BC_COLS = ((3584, 1536), (3072, 512), (0, 3072))
_C_COLS = ((3600, 1536), (3088, 512), (0, 3072))


def _reorder_cols(w, cols):
    return jnp.concatenate([w[:, s:s + n] for s, n in cols], axis=1)


def _restore_cols(pieces_in_my_order, cols, extra=()):
    placed = sorted(list(zip([s for s, _ in cols], pieces_in_my_order)) + list(extra), key=lambda sp: sp[0])
    return jnp.concatenate([piece for _, piece in placed], axis=1)


def kernel(x, mem, mem_norm_w, w_mem_kv, norm_pre, norm_post, w_out, a_w_in, a_ln_w, a_ln_b, a_w_s, a_b_s, b_w_in, b_conv_w, c_w_in, c_conv_w, c_a_log, c_dt_bias, c_o_norm_w, loss_target, m_mem_norm_w, m_w_mem_kv, m_norm_pre, m_norm_post, m_w_out, m_a_w_in, m_a_ln_w, m_a_ln_b, m_a_w_s, m_a_b_s, m_b_w_in, m_b_conv_w, m_c_w_in, m_c_conv_w, m_c_a_log, m_c_dt_bias, m_c_o_norm_w, v_mem_norm_w, v_w_mem_kv, v_norm_pre, v_norm_post, v_w_out, v_a_w_in, v_a_ln_w, v_a_ln_b, v_a_w_s, v_a_b_s, v_b_w_in, v_b_conv_w, v_c_w_in, v_c_conv_w, v_c_a_log, v_c_dt_bias, v_c_o_norm_w):
    given = dict(locals())
    x0 = x[0]
    mem0 = mem[0]
    target = loss_target[0]

    w_sh, m_sh, v_sh = (_layer_shards(given, pre) for pre in ("", "m_", "v_"))
    small_names = [n for n, _, _ in _SMALL_SHARDED]
    small_shapes = [s for _, s, _ in _SMALL_SHARDED]
    repl_names = [n for n, _ in _REPLICATED]
    repl_shapes = [s for _, s in _REPLICATED]
    small = _pack_rows([given[n] for n in small_names], small_shapes, R_SMALL)
    g_wkv, g_wout0, g_in0, g_small = _all_gather(
        [w_mem_kv.astype(MXU)] + [_as_matmul_operand(w) for w in w_sh[0]] + [small], "gather_weights")
    small_full = _unpack_rows(g_small, small_shapes, lead=(N_DEV,))
    full = {n: _join_shards(blocks, ax) for (n, _, ax), blocks in zip(_SMALL_SHARDED, small_full)}
    wkv = g_wkv.reshape(D, 2 * D_XA)
    wouts = [g_wout0.reshape(D_CAT, D)]
    w_in = [_reorder_cols(_join_shards(g_in0, 1), _A_COLS)]
    w_cab = None
    lnw = [full["a_ln_w"][j][None, :] for j in range(2)]
    lnb = [full["a_ln_b"][j][None, :] for j in range(2)]
    ws = [a_w_s[j] for j in range(2)]
    bst = [jnp.pad(a_b_s[j].T, ((0, 0), (0, HEAD - GM_GROUPS))) for j in range(2)]
    cw_b = jnp.pad(full["b_conv_w"][0], ((0, HALO - 3), (0, 0)))
    cw_c = jnp.pad(full["c_conv_w"][0], ((0, HALO - 4), (0, 0)))
    alog = jnp.pad(c_a_log, ((0, 0), (0, HEAD - DN_HEADS)))
    dtb = jnp.pad(c_dt_bias, ((0, 0), (0, HEAD - DN_HEADS)))
    onw = c_o_norm_w
    mw = mem_norm_w[None, :]

    kv = _memkv_fwd(mem0, mw, wkv)
    xs, saved = [x0], []
    for i in range(4):
        kind = i % 3
        npre, npost = norm_pre[i][None, :], norm_post[i][None, :]
        ahead = [_as_matmul_operand(w) for w in w_sh[i + 1]] if i < 3 else []
        res = _proj_fwd(xs[i], npre, w_in[i], w_cab if kind == 2 else None, f"proj_fwd_{i}", gather=ahead)
        if ahead:
            g_wout, g_in = res[-2:]
            wouts.append(g_wout.reshape(D_CAT, D))
            if i + 1 == 2:
                c_full = jnp.concatenate([g_in[d] for d in range(N_DEV)], axis=1)
                w_in.append(_reorder_cols(c_full, _C_COLS))
                w_cab = jnp.pad(c_full[:, 3072:3088], ((0, 0), (0, HEAD - 16)))
            else:
                w_in.append(_reorder_cols(_join_shards(g_in, 1), _BC_COLS if i + 1 == 1 else _A_COLS))
        if kind == 2:
            p, h, ab = res[:3]
            qkv = _qkv_fwd(p, cw_c, f"qkv_fwd_{i}")
            ymix, keep, inv = _delta_fwd(qkv, ab, alog, dtb, onw, f"delta_fwd_{i}")
            extra = (qkv, ab, keep, inv)
        else:
            p, h = res[:2]
            if kind == 0:
                ymix = _gmlp_fwd(p, lnw[i // 3], lnb[i // 3], ws[i // 3], bst[i // 3], f"gmlp_fwd_{i}")
            else:
                ymix = _sconv_fwd(p, cw_b, f"sconv_fwd_{i}")
            extra = ()
        xn, o = _tail_fwd(p, ymix, xs[i], kv, wouts[i], npost, f"tail_fwd_{i}")
        xs.append(xn)
        saved.append((p, h, ymix, o, extra))

    loss_tile, dx = _loss_head(xs[4], target)
    loss = lax.psum(loss_tile[0, 0], ("x", "y", "c"))

    g = {}
    d_npre, d_npost = [None] * 4, [None] * 4
    d_ws, d_bs, d_lnw, d_lnb = [None] * 2, [None] * 2, [None] * 2, [None] * 2
    dkv = None
    pending, landed = [], [None] * 4
    for i in reversed(range(4)):
        kind = i % 3
        p, h, ymix, o, extra = saved[i]
        npre, npost = norm_pre[i][None, :], norm_post[i][None, :]
        res = _tail_bwd(dx, o, p, ymix, kv, wouts[i], npost, f"tail_bwd_{i}", exchange=pending)
        dzq, dymix, d_wout, d_npost[i], dkv_i = res[:5]
        if pending:
            landed[i + 1] = res[5:]
        dkv = dkv_i if dkv is None else dkv + dkv_i
        w_zq, w_mix = w_in[i][:, :ZQ], w_in[i][:, ZQ:]
        dw_zq = _matmul_tn(h, dzq, f"dw_zq_{i}")
        if kind == 0:
            j = i // 3
            early_x, early_b = [], []
            if i == 0:
                d_wkv, d_mw = _memkv_bwd(mem0, mw, wkv, dkv)
                early_x = [d_wkv.reshape(N_DEV, 128, D).astype(XCH), d_wout.reshape(N_DEV, D_CAT // N_DEV, D).astype(XCH)]
                early_b = [d_ws[1].reshape(GM_CHUNK, D).astype(XCH)]
            (dmix, d_lnw[j], d_lnb[j], d_ws[j], dbst), early_landed = _gmlp_bwd(
                p, dymix, lnw[j], lnb[j], ws[j], bst[j], f"gmlp_bwd_{i}", exchange=early_x, bcast=early_b)
            d_bs[j] = dbst[:, :GM_GROUPS].T
            dw_mix = _matmul_tn(h, dmix, f"dw_mix_{i}")
            d_win = _restore_cols([dw_zq[:, :D_CAT], dw_zq[:, D_CAT:], dw_mix], _A_COLS)
            dps, wparts = [dzq, dmix], [w_zq, w_mix]
        elif kind == 1:
            dmix, dcw = _sconv_bwd(p, dymix, cw_b, f"sconv_bwd_{i}")
            g["b_conv_w"] = dcw[None, :3]
            dw_mix = _matmul_tn(h, dmix, f"dw_mix_{i}")
            d_win = _restore_cols([dw_zq[:, :D_CAT], dw_zq[:, D_CAT:], dw_mix], _BC_COLS)
            dps, wparts = [dzq, dmix], [w_zq, w_mix]
        else:
            qkv, ab, keep, inv = extra
            dqkv, dab, dalog, ddtb, donw = _delta_bwd(qkv, ab, alog, dtb, onw, keep, inv, dymix, f"delta_bwd_{i}")
            dmix, dcw = _qkv_bwd(p, dqkv, cw_c, f"qkv_bwd_{i}")
            g["c_conv_w"] = dcw[None, :4]
            g["c_a_log"], g["c_dt_bias"], g["c_o_norm_w"] = dalog[:, :DN_HEADS], ddtb[:, :DN_HEADS], donw
            dw_mix = _matmul_tn(h, dmix, f"dw_mix_{i}")
            dw_ab = _matmul_tn(h, dab, f"dw_ab_{i}")
            d_win = _restore_cols([dw_zq[:, :D_CAT], dw_zq[:, D_CAT:], dw_mix], _C_COLS, extra=[(3072, dw_ab[:, :16])])
            dps, wparts = [dzq, dmix, dab], [w_zq, w_mix, w_cab]
        width = d_win.shape[1] // N_DEV
        pending = [] if i == 0 else [d_wout.reshape(N_DEV, D_CAT // N_DEV, D).astype(XCH)]
        pending.append(jnp.stack([d_win[:, d * width:(d + 1) * width] for d in range(N_DEV)]).astype(XCH))
        if i > 0:
            dx, d_npre[i] = _proj_bwd_x(dps, wparts, xs[i], dx, npre, f"proj_bwd_x_{i}")

    g["mem_norm_w"] = d_mw[0]
    g["norm_pre"] = jnp.concatenate([jnp.zeros((1, D), F32)] + d_npre[1:], axis=0)
    g["norm_post"] = jnp.concatenate(d_npost, axis=0)
    g["a_ln_w"] = jnp.concatenate(d_lnw, axis=0)
    g["a_ln_b"] = jnp.concatenate(d_lnb, axis=0)
    g["a_b_s"] = jnp.stack(d_bs)
    e_small = _pack_rows([_split_shards(g[n], ax) for n, _, ax in _SMALL_SHARDED], small_shapes, R_SMALL, lead=(N_DEV,))
    r_pack = _pack_rows([g[n] for n in repl_names], repl_shapes, R_REPL)
    dx, d_npre0, l_in, l_small, ws0_all, r_all = _proj_bwd_x(
        dps, wparts, xs[0], dx, norm_pre[0][None, :], "proj_bwd_x_0", exchange=pending + [e_small],
        bcast=[d_ws[0].reshape(GM_CHUNK, D).astype(XCH), r_pack])
    l_wkv, l_wout0, ws3_all = early_landed
    landed[0] = (l_wout0, l_in)
    npre0_all, = _all_gather([jnp.pad(d_npre0, ((0, HALO - 1), (0, 0)))], "gather_norm_pre0")
    r_all = r_all.at[:, 1, :].set(npre0_all[:, 0, :])

    res = [[_reduce_adamw(landed[i][a], w_sh[i][a], m_sh[i][a], v_sh[i][a], f"adamw_{i}_{a}") for a in range(2)]
           for i in range(4)]
    res_wkv = _reduce_adamw(l_wkv, w_mem_kv, m_w_mem_kv, v_w_mem_kv, "adamw_w_mem_kv")
    res_small = _reduce_adamw_vectors(
        l_small, small, *(_pack_rows([given[pre + n] for n in small_names], small_shapes, R_SMALL) for pre in ("m_", "v_")),
        small_shapes, "adamw_small")
    res_ws = [_reduce_adamw(parts, *(given[pre + "a_w_s"][j].reshape(GM_CHUNK, D) for pre in ("", "m_", "v_")),
                            f"adamw_a_w_s_{j}") for j, parts in enumerate((ws0_all, ws3_all))]
    res_repl = _reduce_adamw_vectors(
        r_all, *(_pack_rows([given[pre + n] for n in repl_names], repl_shapes, R_REPL) for pre in ("", "m_", "v_")),
        repl_shapes, "adamw_replicated")

    order = ["mem_norm_w", "w_mem_kv", "norm_pre", "norm_post", "w_out", "a_w_in", "a_ln_w", "a_ln_b", "a_w_s", "a_b_s",
             "b_w_in", "b_conv_w", "c_w_in", "c_conv_w", "c_a_log", "c_dt_bias", "c_o_norm_w"]
    outs = [loss, dx[None]]
    for kind in range(4):
        got = dict(zip(repl_names, res_repl[kind]))
        got["a_w_s"] = jnp.stack([r[kind] for r in res_ws]).reshape(a_w_s.shape)
        got.update(zip(small_names, res_small[kind]))
        got["w_mem_kv"] = res_wkv[kind]
        got["w_out"] = jnp.stack([res[i][0][kind] for i in range(4)])
        got["a_w_in"] = jnp.stack([res[0][1][kind], res[3][1][kind]])
        got["b_w_in"] = res[1][1][kind]
        got["c_w_in"] = res[2][1][kind]
        outs += [got[n] for n in order]
    return tuple(outs)
```

```python
import functools
import math

import jax
import jax.numpy as jnp
from jax import lax
from jax.experimental import pallas as pl
from jax.experimental.pallas import tpu as pltpu

F32 = jnp.float32
MXU = jnp.bfloat16
ACT = jnp.bfloat16

D = 1024
D_XA = 512
D_CAT = 1536
N_MEM = 256
XA_HEADS = 4
HEAD = 128
ZQ = D_CAT + D_XA
EPS = 1e-6
GM_CHUNK = 128
GM_CHUNKS_PER_STEP = 4
GM_GROUPS = 8
DN_HEADS = 8
DN_CHUNK = 64
N_DEV = 8
HALO = 8
VMEM_LIMIT = 56 * 1024 * 1024
XCH = jnp.bfloat16
R_REPL = 32
R_SMALL = 16

ADAM_LR = 0.001
ADAM_B1 = 0.9
ADAM_B2 = 0.999
ADAM_EPS = 1e-08
ADAM_WD = 0.01
ADAM_STEP = 10

NN = ((1,), (0,))
NT = ((1,), (1,))
TN = ((0,), (0,))
MESH = pl.DeviceIdType.MESH


def _pcall(body, **kw):
    return pl.pallas_call(body, **kw)


def _cp(n_axes):
    return pltpu.CompilerParams(dimension_semantics=("arbitrary",) * n_axes, vmem_limit_bytes=VMEM_LIMIT)


def _dot(a, b, dims, prec=None):
    return lax.dot_general(a, b, (dims, ((), ())), preferred_element_type=F32, precision=prec)


def _mdot(a, b, dims):
    return _dot(a.astype(MXU), b.astype(MXU), dims)


def _make_mms(raw):
    @jax.custom_vjp
    def nn(a, b):
        return raw(a, b, NN)

    @jax.custom_vjp
    def nt(a, b):
        return raw(a, b, NT)

    @jax.custom_vjp
    def tn(a, b):
        return raw(a, b, TN)

    nn.defvjp(lambda a, b: (nn(a, b), (a, b)), lambda r, g: (nt(g, r[1]), tn(r[0], g)))
    nt.defvjp(lambda a, b: (nt(a, b), (a, b)), lambda r, g: (nn(g, r[1]), tn(g, r[0])))
    tn.defvjp(lambda a, b: (tn(a, b), (a, b)), lambda r, g: (nt(r[1], g), nn(r[0], g)))
    return nn, nt, tn


_nn, _nt, _tn = _make_mms(_mdot)


def _split_dot(a, b, dims):
    ah = a.astype(jnp.bfloat16)
    bh = b.astype(jnp.bfloat16)
    al = (a - ah.astype(F32)).astype(jnp.bfloat16)
    bl = (b - bh.astype(F32)).astype(jnp.bfloat16)
    return _dot(ah, bh, dims) + (_dot(ah, bl, dims) + _dot(al, bh, dims))


_hnn, _hnt, _htn = _make_mms(_split_dot)


def _full(shape):
    return pl.BlockSpec(shape, lambda *_: (0,) * len(shape))


def _silu(z):
    return z * jax.nn.sigmoid(z)


def _dsilu(z):
    s = jax.nn.sigmoid(z)
    return s * (1.0 + z * (1.0 - s))


def _onehot_row(n, k):
    return (lax.broadcasted_iota(jnp.int32, (1, n), 1) == k).astype(F32)


_HBM = pl.BlockSpec(memory_space=pl.ANY)


def _sem_shapes(n_remote, n_local):
    return [pltpu.SemaphoreType.DMA((n_remote,)), pltpu.SemaphoreType.DMA((n_remote,)),
            pltpu.SemaphoreType.DMA((n_local,))]


def _gather_parts(x_refs, out_refs, send_sems, recv_sems, local_sems):
    n = len(x_refs)
    x, y, cc = lax.axis_index("x"), lax.axis_index("y"), lax.axis_index("c")
    me, sibling = (x, y, cc), (x, y, 1 - cc)
    chips = [(1 - x, y), (x, 1 - y), (1 - x, 1 - y)]

    def slot(a, px, py, pc):
        return out_refs[a].at[4 * px + 2 * py + pc]

    def copy(k, a, block, to, src=None):
        return pltpu.make_async_remote_copy(
            src_ref=slot(a, *block) if src is None else src, dst_ref=slot(a, *block),
            send_sem=send_sems.at[k * n + a], recv_sem=recv_sems.at[k * n + a], device_id=to, device_id_type=MESH)

    mine = [pltpu.make_async_copy(x_refs[a], slot(a, *me), local_sems.at[a]) for a in range(n)]
    first = [copy(0, a, me, sibling, src=x_refs[a]) for a in range(n)]
    first += [copy(1 + j, a, me, (*chip, cc), src=x_refs[a]) for j, chip in enumerate(chips) for a in range(n)]

    def begin():
        for cp in mine + first:
            cp.start()

    def end():
        passed = []
        for j, chip in enumerate(chips):
            for a in range(n):
                copy(1 + j, a, (*chip, cc), me).wait_recv()
                passed.append(copy(4 + j, a, (*chip, cc), sibling))
                passed[-1].start()
        for a in range(n):
            copy(0, a, sibling, me).wait_recv()
        for j, chip in enumerate(chips):
            for a in range(n):
                copy(4 + j, a, (*chip, 1 - cc), me).wait_recv()
        for cp in first + passed:
            cp.wait_send()
        for cp in mine:
            cp.wait()

    return begin, end


def _exchange_parts(g_refs, out_refs, b_refs, ball_refs, send_sems, recv_sems, local_sems):
    n, nb = len(g_refs), len(b_refs)
    per_peer = n + nb
    x, y, cc = lax.axis_index("x"), lax.axis_index("y"), lax.axis_index("c")
    my_idx = 4 * x + 2 * y + cc
    mine = [pltpu.make_async_copy(g_refs[a].at[my_idx], out_refs[a].at[my_idx], local_sems.at[a]) for a in range(n)]
    mine += [pltpu.make_async_copy(b_refs[a], ball_refs[a].at[my_idx], local_sems.at[n + a]) for a in range(nb)]
    copies = []
    for k in range(1, N_DEV):
        px = 1 - x if k & 4 else x
        py = 1 - y if k & 2 else y
        pc = 1 - cc if k & 1 else cc
        base = (k - 1) * per_peer
        for a in range(n):
            copies.append(pltpu.make_async_remote_copy(
                src_ref=g_refs[a].at[4 * px + 2 * py + pc], dst_ref=out_refs[a].at[my_idx],
                send_sem=send_sems.at[base + a], recv_sem=recv_sems.at[base + a],
                device_id=(px, py, pc), device_id_type=MESH))
        for a in range(nb):
            copies.append(pltpu.make_async_remote_copy(
                src_ref=b_refs[a], dst_ref=ball_refs[a].at[my_idx], send_sem=send_sems.at[base + n + a],
                recv_sem=recv_sems.at[base + n + a], device_id=(px, py, pc), device_id_type=MESH))

    def begin():
        for cp in mine + copies:
            cp.start()

    def end():
        for cp in copies:
            cp.wait_recv()
        for cp in copies:
            cp.wait_send()
        for cp in mine:
            cp.wait()

    return begin, end


def _pcall_hosting(body, args, exchange, bcast, *, name, grid, in_specs, out_specs, out_shape, compiler_params):
    n_in, n_out, n_x, n_b = len(in_specs), len(out_specs), len(exchange), len(bcast)
    n_c, n_steps = n_x + n_b, grid[0]

    def hosted(*refs):
        ins, c_in = refs[:n_in], refs[n_in:n_in + n_c]
        outs, c_out = refs[n_in + n_c:n_in + n_c + n_out], refs[n_in + n_c + n_out:n_in + 2 * n_c + n_out]
        begin, end = _exchange_parts(c_in[:n_x], c_out[:n_x], c_in[n_x:], c_out[n_x:], *refs[n_in + 2 * n_c + n_out:])
        pl.when(pl.program_id(0) == 0)(begin)
        body(*ins, *outs)
        pl.when(pl.program_id(0) == n_steps - 1)(end)

    landed_shape = [jax.ShapeDtypeStruct(e.shape, e.dtype) for e in exchange]
    landed_shape += [jax.ShapeDtypeStruct((N_DEV,) + b.shape, b.dtype) for b in bcast]
    res = _pcall(hosted, name=name, grid=grid, in_specs=list(in_specs) + [_HBM] * n_c,
                 out_specs=list(out_specs) + [_HBM] * n_c, out_shape=list(out_shape) + landed_shape,
                 scratch_shapes=_sem_shapes(7 * n_c, n_c), compiler_params=compiler_params)(*args, *exchange, *bcast)
    return res[:n_out], res[n_out:]


def _all_gather(blks, name):
    n = len(blks)

    def body(*refs):
        begin, end = _gather_parts(refs[:n], refs[n:2 * n], *refs[2 * n:])
        begin()
        end()

    return _pcall(body, name=name, out_shape=[jax.ShapeDtypeStruct((N_DEV,) + b.shape, b.dtype) for b in blks],
                  in_specs=[_HBM] * n, out_specs=[_HBM] * n, scratch_shapes=_sem_shapes(7 * n, n))(*blks)


def _sum_and_adamw(p_ref, w, m, v):
    g = p_ref[0].astype(F32)
    for s in range(1, N_DEV):
        g = g + p_ref[s].astype(F32)
    nm = ADAM_B1 * m + (1.0 - ADAM_B1) * g
    nv = ADAM_B2 * v + (1.0 - ADAM_B2) * (g * g)
    m_hat = nm / (1.0 - ADAM_B1 ** ADAM_STEP)
    v_hat = nv / (1.0 - ADAM_B2 ** ADAM_STEP)
    return g, -ADAM_LR * (m_hat / (jnp.sqrt(v_hat) + ADAM_EPS) + ADAM_WD * w), nm, nv


def _reduce_adamw_vectors(parts, w, m, v, shapes, name):
    table = _row_table(shapes)

    def body(p_ref, w_ref, m_ref, v_ref, *out_refs):
        results = _sum_and_adamw(p_ref, w_ref[...], m_ref[...], v_ref[...])
        for kind, val in enumerate(results):
            for t, (r0, nr, nc) in enumerate(table):
                out_refs[kind * len(table) + t][...] = val[r0:r0 + nr, 0:nc]

    outs = _pcall(body, name=name, out_shape=[jax.ShapeDtypeStruct((nr, nc), F32) for _, nr, nc in table] * 4)(parts, w, m, v)
    return [[outs[kind * len(table) + t].reshape(s) for t, s in enumerate(shapes)] for kind in range(4)]


def _reduce_adamw(parts, w, m, v, name):
    lead = w.shape[:-2]
    r, c = w.shape[-2:]
    tr = 128 if r % 128 == 0 else r
    zeros = (0,) * len(lead)
    at = zeros + (slice(None), slice(None))

    def body(p_ref, w_ref, m_ref, v_ref, g_ref, d_ref, nm_ref, nv_ref):
        g_ref[at], d_ref[at], nm_ref[at], nv_ref[at] = _sum_and_adamw(p_ref, w_ref[at], m_ref[at], v_ref[at])

    row = pl.BlockSpec((1,) * len(lead) + (tr, c), lambda i: zeros + (i, 0))
    out = jax.ShapeDtypeStruct(w.shape, F32)
    return _pcall(
        body, name=name, grid=(r // tr,),
        in_specs=[pl.BlockSpec((N_DEV, tr, c), lambda i: (0, i, 0)), row, row, row],
        out_specs=[row, row, row, row], out_shape=[out, out, out, out], compiler_params=_cp(1),
    )(parts, w, m, v)


def _proj_fwd(x, nw, w, wab, name, gather=()):
    t, npj = x.shape[0], w.shape[1]
    tm, tn = min(512, t), 1024
    has_ab = wab is not None
    n_in, n_out, n_g = 3 + has_ab, 2 + has_ab, len(gather)
    n_i = t // tm

    def body(*refs):
        ins, g_ins = refs[:n_in], refs[n_in:n_in + n_g]
        outs = refs[n_in + n_g:n_in + n_g + n_out]
        g_outs = refs[n_in + n_g + n_out:n_in + 2 * n_g + n_out]
        if has_ab:
            (x_ref, nw_ref, w_ref, wab_ref), (p_ref, h_ref, ab_ref) = ins, outs
        else:
            (x_ref, nw_ref, w_ref), (p_ref, h_ref) = ins, outs
        if n_g:
            begin, end = _gather_parts(g_ins, g_outs, *refs[n_in + 2 * n_g + n_out:])
            pl.when(pl.program_id(0) == 0)(begin)

        xv = x_ref[...]
        hv = (xv * lax.rsqrt(jnp.mean(xv * xv, axis=-1, keepdims=True) + EPS) * nw_ref[...]).astype(MXU)
        h_ref[...] = hv.astype(ACT)
        if has_ab:
            ab_ref[...] = _dot(hv, wab_ref[...], NN)
        for j in range(npj // tn):
            p_ref[:, j * tn:(j + 1) * tn] = _dot(hv, w_ref[:, j * tn:(j + 1) * tn], NN).astype(ACT)
        if n_g:
            pl.when(pl.program_id(0) == n_i - 1)(end)

    in_specs = [pl.BlockSpec((tm, D), lambda i: (i, 0)), _full((1, D)), _full((D, npj))]
    out_specs = [pl.BlockSpec((tm, npj), lambda i: (i, 0)), pl.BlockSpec((tm, D), lambda i: (i, 0))]
    out_shape = [jax.ShapeDtypeStruct((t, npj), ACT), jax.ShapeDtypeStruct((t, D), ACT)]
    args = [x, nw, w]
    if has_ab:
        in_specs.append(_full((D, HEAD)))
        out_specs.append(pl.BlockSpec((tm, HEAD), lambda i: (i, 0)))
        out_shape.append(jax.ShapeDtypeStruct((t, HEAD), F32))
        args.append(wab)
    scratch = []
    if n_g:
        in_specs += [_HBM] * n_g
        out_specs += [_HBM] * n_g
        out_shape += [jax.ShapeDtypeStruct((N_DEV,) + b.shape, b.dtype) for b in gather]
        args += list(gather)
        scratch += _sem_shapes(7 * n_g, n_g)
    return _pcall(body, name=name, grid=(n_i,), in_specs=in_specs, out_specs=out_specs,
                  out_shape=out_shape, scratch_shapes=scratch, compiler_params=_cp(1))(*args)


def _proj_bwd_x(dps, ws, x, dxn, nw, name, exchange=(), bcast=()):
    t = x.shape[0]
    tm = min(256, t)
    n, n_x, n_b = len(dps), len(exchange), len(bcast)
    n_c, n_steps = n_x + n_b, t // tm

    def body(*refs):
        dp_refs, w_refs = refs[:n], refs[n:2 * n]
        x_ref, dxn_ref, nw_ref = refs[2 * n:2 * n + 3]
        c_in = refs[2 * n + 3:2 * n + 3 + n_c]
        dx_ref, dnw_ref = refs[2 * n + 3 + n_c:2 * n + 5 + n_c]
        c_out = refs[2 * n + 5 + n_c:2 * n + 5 + 2 * n_c]
        if n_c:
            begin, end = _exchange_parts(c_in[:n_x], c_out[:n_x], c_in[n_x:], c_out[n_x:], *refs[2 * n + 5 + 2 * n_c:])
            pl.when(pl.program_id(0) == 0)(begin)
        dh = _mdot(dp_refs[0][...], w_refs[0][...], NT)
        for k in range(1, n):
            dh = dh + _mdot(dp_refs[k][...], w_refs[k][...], NT)
        xv = x_ref[...]
        r = lax.rsqrt(jnp.mean(xv * xv, axis=-1, keepdims=True) + EPS)

        @pl.when(pl.program_id(0) == 0)
        def _():
            dnw_ref[...] = jnp.zeros_like(dnw_ref)

        dnw_ref[...] += jnp.sum(dh * xv * r, axis=0, keepdims=True)
        dhw = dh * nw_ref[...]
        dx_ref[...] = dxn_ref[...] + r * (dhw - xv * (r * r) * jnp.mean(dhw * xv, axis=-1, keepdims=True))
        if n_c:
            pl.when(pl.program_id(0) == n_steps - 1)(end)

    row = pl.BlockSpec((tm, D), lambda i: (i, 0))
    in_specs = [pl.BlockSpec((tm, dp.shape[1]), lambda i: (i, 0)) for dp in dps]
    in_specs += [_full(w.shape) for w in ws]
    in_specs += [row, row, _full((1, D))] + [_HBM] * n_c
    out_shape = [jax.ShapeDtypeStruct((t, D), F32), jax.ShapeDtypeStruct((1, D), F32)]
    out_shape += [jax.ShapeDtypeStruct(e.shape, e.dtype) for e in exchange]
    out_shape += [jax.ShapeDtypeStruct((N_DEV,) + b.shape, b.dtype) for b in bcast]
    return _pcall(body, name=name, grid=(n_steps,), in_specs=in_specs, out_specs=[row, _full((1, D))] + [_HBM] * n_c,
                  out_shape=out_shape, scratch_shapes=_sem_shapes(7 * n_c, n_c) if n_c else [],
                  compiler_params=_cp(1))(*dps, *ws, x, dxn, nw, *exchange, *bcast)


def _matmul_tn(a, b, name, out_dtype=XCH):
    t, m = a.shape
    n = b.shape[1]
    tm, tn = min(1024, t), min(1024, n)
    n_t = t // tm

    def body(a_ref, b_ref, o_ref, acc):
        @pl.when(pl.program_id(1) == 0)
        def _():
            acc[...] = jnp.zeros_like(acc)

        acc[...] += _mdot(a_ref[...], b_ref[...], TN)

        @pl.when(pl.program_id(1) == n_t - 1)
        def _():
            o_ref[...] = acc[...].astype(out_dtype)

    return _pcall(body, name=name, grid=(n // tn, n_t),
                  in_specs=[pl.BlockSpec((tm, m), lambda j, i: (i, 0)), pl.BlockSpec((tm, tn), lambda j, i: (i, j))],
                  out_specs=pl.BlockSpec((m, tn), lambda j, i: (0, j)),
                  out_shape=jax.ShapeDtypeStruct((m, n), out_dtype), scratch_shapes=[pltpu.VMEM((m, tn), F32)],
                  compiler_params=_cp(2))(a, b)


def _memkv_fwd(mem, mw, wkv):
    def body(mem_ref, mw_ref, w_ref, kv_ref):
        mv = mem_ref[...]
        mn = mv * lax.rsqrt(jnp.mean(mv * mv, axis=-1, keepdims=True) + EPS) * mw_ref[...]
        kv_ref[...] = _mdot(mn, w_ref[...], NN)

    return _pcall(body, name="memkv_fwd", out_shape=jax.ShapeDtypeStruct((N_MEM, 2 * D_XA), F32),
                  compiler_params=pltpu.CompilerParams(vmem_limit_bytes=VMEM_LIMIT))(mem, mw, wkv)


def _memkv_bwd(mem, mw, wkv, dkv):
    def body(mem_ref, mw_ref, w_ref, dkv_ref, dw_ref, dmw_ref):
        mv = mem_ref[...]
        r = lax.rsqrt(jnp.mean(mv * mv, axis=-1, keepdims=True) + EPS)
        mn = mv * r * mw_ref[...]
        dkvv = dkv_ref[...]
        dw_ref[...] = _mdot(mn, dkvv, TN)
        dmn = _mdot(dkvv, w_ref[...], NT)
        dmw_ref[...] = jnp.sum(dmn * mv * r, axis=0, keepdims=True)

    return _pcall(body, name="memkv_bwd",
                  out_shape=[jax.ShapeDtypeStruct((D, 2 * D_XA), F32), jax.ShapeDtypeStruct((1, D), F32)],
                  compiler_params=pltpu.CompilerParams(vmem_limit_bytes=VMEM_LIMIT))(mem, mw, wkv, dkv)


def _attend(q, kv):
    heads = range(XA_HEADS)
    qs = [q[:, h * HEAD:(h + 1) * HEAD] for h in heads]
    ks = [kv[:, h * HEAD:(h + 1) * HEAD] for h in heads]
    vs = [kv[:, D_XA + h * HEAD:D_XA + (h + 1) * HEAD] for h in heads]
    ss = [_mdot(qs[h], ks[h], NT) * (HEAD ** -0.5) for h in heads]
    es = [jnp.exp(s - jnp.max(s, axis=-1, keepdims=True)) for s in ss]
    ps = [e / jnp.sum(e, axis=-1, keepdims=True) for e in es]
    return ps, [_mdot(ps[h], vs[h], NN) for h in heads]


def _tail_fwd(p, ymix, x, kv, wout, npost, name):
    t = x.shape[0]
    tm = min(512, t)

    def body(z_ref, q_ref, y_ref, x_ref, kv_ref, w_ref, np_ref, xn_ref, o_ref):
        _, outs = _attend(q_ref[...], kv_ref[...])
        cat = jnp.concatenate([y_ref[...]] + [a.astype(ACT) for a in outs], axis=1)
        g = cat * _silu(z_ref[...])
        o = _mdot(g, w_ref[...], NN)
        o_ref[...] = o
        xn_ref[...] = x_ref[...] + o * lax.rsqrt(jnp.mean(o * o, axis=-1, keepdims=True) + EPS) * np_ref[...]

    row = pl.BlockSpec((tm, D), lambda i: (i, 0))
    return _pcall(
        body, name=name, grid=(t // tm,),
        in_specs=[pl.BlockSpec((tm, D_CAT), lambda i: (i, 0)), pl.BlockSpec((tm, D_XA), lambda i: (i, D_CAT // D_XA)),
                  row, row, _full((N_MEM, 2 * D_XA)), _full((D_CAT, D)), _full((1, D))],
        out_specs=[row, row],
        out_shape=[jax.ShapeDtypeStruct((t, D), F32), jax.ShapeDtypeStruct((t, D), F32)], compiler_params=_cp(1),
    )(p, p, ymix, x, kv, wout, npost)


def _tail_bwd(dxn, o, p, ymix, kv, wout, npost, name, exchange=()):
    t = dxn.shape[0]
    tm = min(256, t)
    n_x, n_steps = len(exchange), t // tm

    def body(*refs):
        dxn_ref, o_ref, z_ref, q_ref, y_ref, kv_ref, w_ref, np_ref = refs[:8]
        dzq_ref, dy_ref, dw_ref, dnp_ref, dkv_ref = refs[8 + n_x:13 + n_x]
        if n_x:
            begin, end = _exchange_parts(refs[8:8 + n_x], refs[13 + n_x:13 + 2 * n_x], (), (), *refs[13 + 2 * n_x:])
            pl.when(pl.program_id(0) == 0)(begin)

        @pl.when(pl.program_id(0) == 0)
        def _():
            dw_ref[...] = jnp.zeros_like(dw_ref)
            dnp_ref[...] = jnp.zeros_like(dnp_ref)
            dkv_ref[...] = jnp.zeros_like(dkv_ref)

        q = q_ref[...]
        kvv = kv_ref[...]
        z = z_ref[...]
        ps, outs = _attend(q, kvv)
        cat = jnp.concatenate([y_ref[...]] + [a.astype(ACT) for a in outs], axis=1)
        sz = _silu(z)
        g = cat * sz
        ov = o_ref[...]
        dr = dxn_ref[...]
        rr = lax.rsqrt(jnp.mean(ov * ov, axis=-1, keepdims=True) + EPS)
        dnp_ref[...] += jnp.sum(dr * ov * rr, axis=0, keepdims=True)
        dow = dr * np_ref[...]
        do = rr * (dow - ov * (rr * rr) * jnp.mean(dow * ov, axis=-1, keepdims=True))
        dg = _mdot(do, w_ref[...], NT).astype(ACT)
        dw_ref[...] += _mdot(g, do, TN)
        dcat = dg * sz
        dzq_ref[:, 0:D_CAT] = dg * cat * _dsilu(z)
        dy_ref[...] = dcat[:, 0:D]
        heads = range(XA_HEADS)
        dohs = [dcat[:, D + h * HEAD:D + (h + 1) * HEAD] for h in heads]
        dps = [_mdot(dohs[h], kvv[:, D_XA + h * HEAD:D_XA + (h + 1) * HEAD], NT) for h in heads]
        dss = [ps[h] * (dps[h] - jnp.sum(dps[h] * ps[h], axis=-1, keepdims=True)) for h in heads]
        dqs = [_mdot(dss[h], kvv[:, h * HEAD:(h + 1) * HEAD], NN) * (HEAD ** -0.5) for h in heads]
        dks = [_mdot(dss[h], q[:, h * HEAD:(h + 1) * HEAD], TN) * (HEAD ** -0.5) for h in heads]
        dvs = [_mdot(ps[h], dohs[h], TN) for h in heads]
        for h in heads:
            lo = h * HEAD
            dzq_ref[:, D_CAT + lo:D_CAT + lo + HEAD] = dqs[h].astype(ACT)
            dkv_ref[:, lo:lo + HEAD] += dks[h]
            dkv_ref[:, D_XA + lo:D_XA + lo + HEAD] += dvs[h]
        if n_x:
            pl.when(pl.program_id(0) == n_steps - 1)(end)

    row = pl.BlockSpec((tm, D), lambda i: (i, 0))
    return _pcall(
        body, name=name, grid=(n_steps,),
        in_specs=[row, row, pl.BlockSpec((tm, D_CAT), lambda i: (i, 0)),
                  pl.BlockSpec((tm, D_XA), lambda i: (i, D_CAT // D_XA)), row,
                  _full((N_MEM, 2 * D_XA)), _full((D_CAT, D)), _full((1, D))] + [_HBM] * n_x,
        out_specs=[pl.BlockSpec((tm, ZQ), lambda i: (i, 0)), row, _full((D_CAT, D)), _full((1, D)),
                   _full((N_MEM, 2 * D_XA))] + [_HBM] * n_x,
        out_shape=[jax.ShapeDtypeStruct((t, ZQ), ACT), jax.ShapeDtypeStruct((t, D), ACT),
                   jax.ShapeDtypeStruct((D_CAT, D), F32), jax.ShapeDtypeStruct((1, D), F32),
                   jax.ShapeDtypeStruct((N_MEM, 2 * D_XA), F32)]
        + [jax.ShapeDtypeStruct(e.shape, e.dtype) for e in exchange],
        scratch_shapes=_sem_shapes(7 * n_x, n_x) if n_x else [], compiler_params=_cp(1),
    )(dxn, o, p, p, ymix, kv, wout, npost, *exchange)


def _gmlp_chunk(us, vs, lnws, lnbs, wss, bss):
    gv = [jax.nn.gelu(v) for v in vs]
    mean = sum(jnp.sum(v, axis=-1, keepdims=True) for v in gv) / D
    cen = [v - mean for v in gv]
    var = sum(jnp.sum(c * c, axis=-1, keepdims=True) for c in cen) / D
    rstd = lax.rsqrt(var + EPS)
    row = lax.broadcasted_iota(jnp.int32, (GM_CHUNK, GM_CHUNK), 0)
    col = lax.broadcasted_iota(jnp.int32, (GM_CHUNK, GM_CHUNK), 1)
    ys = []
    for g in range(GM_GROUPS):
        vn = cen[g] * rstd * lnws[g] + lnbs[g]
        sp = _nn(jnp.where(row >= col, wss[g], 0.0), vn) + bss[g]
        ys.append(jax.nn.gelu(us[g]) * sp)
    return ys


def _split_cols(v, n, width=HEAD):
    return [v[:, k * width:(k + 1) * width] for k in range(n)]


def _gmlp_operands(u_ref, v_ref, lnw_ref, lnb_ref, ws_ref, bst_ref, r0):
    us = _split_cols(u_ref[r0:r0 + GM_CHUNK, :].astype(F32), GM_GROUPS)
    vs = _split_cols(v_ref[r0:r0 + GM_CHUNK, :].astype(F32), GM_GROUPS)
    lnws = _split_cols(lnw_ref[...], GM_GROUPS)
    lnbs = _split_cols(lnb_ref[...], GM_GROUPS)
    wss = [ws_ref[g] for g in range(GM_GROUPS)]
    bst = bst_ref[...]
    bss = [jnp.sum(bst * _onehot_row(HEAD, g), axis=1, keepdims=True) for g in range(GM_GROUPS)]
    return us, vs, lnws, lnbs, wss, bss


def _gmlp_rows(t):
    return min(GM_CHUNKS_PER_STEP, t // GM_CHUNK) * GM_CHUNK


def _gmlp_specs(rows):
    return [pl.BlockSpec((rows, D), lambda i: (i, ZQ // D)), pl.BlockSpec((rows, D), lambda i: (i, ZQ // D + 1)),
            _full((1, D)), _full((1, D)), _full((GM_GROUPS, GM_CHUNK, GM_CHUNK)), _full((GM_CHUNK, HEAD))]


def _gmlp_fwd(p, lnw, lnb, ws, bst, name):
    t = p.shape[0]
    rows = _gmlp_rows(t)

    def body(u_ref, v_ref, lnw_ref, lnb_ref, ws_ref, bst_ref, y_ref):
        for r0 in range(0, rows, GM_CHUNK):
            ys = _gmlp_chunk(*_gmlp_operands(u_ref, v_ref, lnw_ref, lnb_ref, ws_ref, bst_ref, r0))
            for g in range(GM_GROUPS):
                y_ref[r0:r0 + GM_CHUNK, g * HEAD:(g + 1) * HEAD] = ys[g].astype(ACT)

    return _pcall(body, name=name, grid=(t // rows,), in_specs=_gmlp_specs(rows),
                  out_specs=pl.BlockSpec((rows, D), lambda i: (i, 0)),
                  out_shape=jax.ShapeDtypeStruct((t, D), ACT), compiler_params=_cp(1))(p, p, lnw, lnb, ws, bst)


def _gmlp_bwd(p, dy, lnw, lnb, ws, bst, name, exchange=(), bcast=()):
    t = p.shape[0]
    rows = _gmlp_rows(t)

    def body(u_ref, v_ref, lnw_ref, lnb_ref, ws_ref, bst_ref, dy_ref, duv_ref, dlnw_ref, dlnb_ref, dws_ref, dbst_ref):
        @pl.when(pl.program_id(0) == 0)
        def _():
            dlnw_ref[...] = jnp.zeros_like(dlnw_ref)
            dlnb_ref[...] = jnp.zeros_like(dlnb_ref)
            dws_ref[...] = jnp.zeros_like(dws_ref)
            dbst_ref[...] = jnp.zeros_like(dbst_ref)

        for r0 in range(0, rows, GM_CHUNK):
            ops = _gmlp_operands(u_ref, v_ref, lnw_ref, lnb_ref, ws_ref, bst_ref, r0)
            _, vjp = jax.vjp(_gmlp_chunk, *ops)
            dus, dvs, dlnws, dlnbs, dwss, dbss = vjp(_split_cols(dy_ref[r0:r0 + GM_CHUNK, :].astype(F32), GM_GROUPS))
            dbst = jnp.zeros((GM_CHUNK, HEAD), F32)
            for g in range(GM_GROUPS):
                lo = g * HEAD
                duv_ref[r0:r0 + GM_CHUNK, lo:lo + HEAD] = dus[g].astype(ACT)
                duv_ref[r0:r0 + GM_CHUNK, D + lo:D + lo + HEAD] = dvs[g].astype(ACT)
                dlnw_ref[:, lo:lo + HEAD] += dlnws[g]
                dlnb_ref[:, lo:lo + HEAD] += dlnbs[g]
                dws_ref[g] += dwss[g]
                dbst = dbst + dbss[g] * _onehot_row(HEAD, g)
            dbst_ref[...] += dbst

    call = dict(
        name=name, grid=(t // rows,),
        in_specs=_gmlp_specs(rows) + [pl.BlockSpec((rows, D), lambda i: (i, 0))],
        out_specs=[pl.BlockSpec((rows, 2 * D), lambda i: (i, 0)), _full((1, D)), _full((1, D)),
                   _full((GM_GROUPS, GM_CHUNK, GM_CHUNK)), _full((GM_CHUNK, HEAD))],
        out_shape=[jax.ShapeDtypeStruct((t, 2 * D), ACT), jax.ShapeDtypeStruct((1, D), F32),
                   jax.ShapeDtypeStruct((1, D), F32), jax.ShapeDtypeStruct((GM_GROUPS, GM_CHUNK, GM_CHUNK), F32),
                   jax.ShapeDtypeStruct((GM_CHUNK, HEAD), F32)],
        compiler_params=_cp(1))
    args = (p, p, lnw, lnb, ws, bst, dy)
    if not exchange and not bcast:
        return _pcall(body, **call)(*args), ()
    return _pcall_hosting(body, args, exchange, bcast, **call)


def _prev_halo(tm, col):
    return pl.BlockSpec((HALO, D), lambda i: (jnp.maximum(i * (tm // HALO) - 1, 0), col))


def _next_halo(tm, col, n_tiles):
    return pl.BlockSpec((HALO, D), lambda i: (jnp.minimum(i + 1, n_tiles - 1) * (tm // HALO), col))


def _taps_back(ext, w, width):
    acc = None
    for k in range(width):
        s = width - 1 - k
        term = w[k:k + 1, :] * (pltpu.roll(ext, s, 0) if s else ext)[HALO:, :]
        acc = term if acc is None else acc + term
    return acc


def _taps_fwd(ext, w, width, n):
    rows = ext.shape[0]
    acc = None
    for k in range(width):
        s = width - 1 - k
        term = w[k:k + 1, :] * (pltpu.roll(ext, rows - s, 0) if s else ext)[0:n, :]
        acc = term if acc is None else acc + term
    return acc


def _sconv_fwd(p, cw, name):
    t = p.shape[0]
    tm = min(256, t)

    def body(b_ref, c_ref, h_ref, cp_ref, hp_ref, w_ref, y_ref):
        first = pl.program_id(0) == 0
        prev = jnp.where(first, 0.0, cp_ref[...].astype(F32) * hp_ref[...].astype(F32))
        ext = jnp.concatenate([prev, c_ref[...].astype(F32) * h_ref[...].astype(F32)], axis=0)
        y_ref[...] = (b_ref[...].astype(F32) * _taps_back(ext, w_ref[...], 3)).astype(ACT)

    c0 = ZQ // D
    tile = [pl.BlockSpec((tm, D), lambda i, c=c: (i, c)) for c in (c0, c0 + 1, c0 + 2)]
    return _pcall(body, name=name, grid=(t // tm,),
                  in_specs=tile + [_prev_halo(tm, c0 + 1), _prev_halo(tm, c0 + 2), _full((HALO, D))],
                  out_specs=pl.BlockSpec((tm, D), lambda i: (i, 0)),
                  out_shape=jax.ShapeDtypeStruct((t, D), ACT), compiler_params=_cp(1))(p, p, p, p, p, cw)


def _sconv_bwd(p, dy, cw, name):
    t = p.shape[0]
    tm = min(256, t)
    n_tiles = t // tm

    def body(b_ref, c_ref, h_ref, cp_ref, hp_ref, bn_ref, dy_ref, dyn_ref, w_ref, d_ref, dw_ref):
        i = pl.program_id(0)

        @pl.when(i == 0)
        def _():
            dw_ref[...] = jnp.zeros_like(dw_ref)

        w = w_ref[...]
        bv, cv, hv = b_ref[...].astype(F32), c_ref[...].astype(F32), h_ref[...].astype(F32)
        dyv = dy_ref[...].astype(F32)
        prev = jnp.where(i == 0, 0.0, cp_ref[...].astype(F32) * hp_ref[...].astype(F32))
        ext = jnp.concatenate([prev, cv * hv], axis=0)
        conv = _taps_back(ext, w, 3)
        dconv = dyv * bv
        nxt = jnp.where(i == n_tiles - 1, 0.0, dyn_ref[...].astype(F32) * bn_ref[...].astype(F32))
        dc = _taps_fwd(jnp.concatenate([dconv, nxt], axis=0), w, 3, tm)
        d_ref[:, 0:D] = (dyv * conv).astype(ACT)
        d_ref[:, D:2 * D] = (dc * hv).astype(ACT)
        d_ref[:, 2 * D:3 * D] = (dc * cv).astype(ACT)
        for k in range(3):
            s = 2 - k
            shifted = (pltpu.roll(ext, s, 0) if s else ext)[HALO:, :]
            dw_ref[k:k + 1, :] += jnp.sum(dconv * shifted, axis=0, keepdims=True)

    c0 = ZQ // D
    tile = [pl.BlockSpec((tm, D), lambda i, c=c: (i, c)) for c in (c0, c0 + 1, c0 + 2)]
    return _pcall(
        body, name=name, grid=(n_tiles,),
        in_specs=tile + [_prev_halo(tm, c0 + 1), _prev_halo(tm, c0 + 2), _next_halo(tm, c0, n_tiles),
                         pl.BlockSpec((tm, D), lambda i: (i, 0)), _next_halo(tm, 0, n_tiles), _full((HALO, D))],
        out_specs=[pl.BlockSpec((tm, 3 * D), lambda i: (i, 0)), _full((HALO, D))],
        out_shape=[jax.ShapeDtypeStruct((t, 3 * D), ACT), jax.ShapeDtypeStruct((HALO, D), F32)],
        compiler_params=_cp(1),
    )(p, p, p, p, p, p, dy, dy, cw)


def _l2_heads(s, scale):
    outs, rs = [], []
    for hh in range(DN_HEADS):
        blk = s[:, hh * HEAD:(hh + 1) * HEAD]
        r = lax.rsqrt(jnp.sum(blk * blk, axis=-1, keepdims=True) + EPS)
        outs.append(blk * (r * scale))
        rs.append(r)
    return outs, rs


_QKV_SCALE = (HEAD ** -0.5, 1.0, None)


def _qkv_fwd(p, cw, name):
    t = p.shape[0]
    tm = min(256, t)

    def body(q_ref, k_ref, v_ref, qp_ref, kp_ref, vp_ref, w_ref, o_ref):
        first = pl.program_id(0) == 0
        for part, (ref, pref) in enumerate(((q_ref, qp_ref), (k_ref, kp_ref), (v_ref, vp_ref))):
            prev = jnp.where(first, 0.0, pref[...].astype(F32))
            ext = jnp.concatenate([prev, ref[...].astype(F32)], axis=0)
            s = _silu(_taps_back(ext, w_ref[:, part * D:(part + 1) * D], 4))
            if _QKV_SCALE[part] is None:
                o_ref[:, part * D:(part + 1) * D] = s.astype(ACT)
            else:
                outs, _ = _l2_heads(s, _QKV_SCALE[part])
                for hh in range(DN_HEADS):
                    o_ref[:, part * D + hh * HEAD:part * D + (hh + 1) * HEAD] = outs[hh].astype(ACT)

    c0 = ZQ // D
    tile = [pl.BlockSpec((tm, D), lambda i, c=c: (i, c)) for c in (c0, c0 + 1, c0 + 2)]
    halo = [_prev_halo(tm, c) for c in (c0, c0 + 1, c0 + 2)]
    return _pcall(body, name=name, grid=(t // tm,), in_specs=tile + halo + [_full((HALO, 3 * D))],
                  out_specs=pl.BlockSpec((tm, 3 * D), lambda i: (i, 0)),
                  out_shape=jax.ShapeDtypeStruct((t, 3 * D), ACT), compiler_params=_cp(1))(p, p, p, p, p, p, cw)


def _qkv_bwd(p, dqkv, cw, name):
    t = p.shape[0]
    tm = min(256, t)
    n_tiles = t // tm

    def body(*refs):
        tiles, prevs, nexts = refs[0:3], refs[3:6], refs[6:9]
        d_tiles, d_nexts = refs[9:12], refs[12:15]
        w_ref, o_ref, dw_ref = refs[15:]
        i = pl.program_id(0)

        @pl.when(i == 0)
        def _():
            dw_ref[...] = jnp.zeros_like(dw_ref)

        for part in range(3):
            w = w_ref[:, part * D:(part + 1) * D]
            prev = jnp.where(i == 0, 0.0, prevs[part][...].astype(F32))
            ext = jnp.concatenate([prev, tiles[part][...].astype(F32), nexts[part][...].astype(F32)], axis=0)
            xc = _taps_back(ext, w, 4)
            dout = jnp.concatenate([d_tiles[part][...].astype(F32), d_nexts[part][...].astype(F32)], axis=0)
            s = _silu(xc)
            if _QKV_SCALE[part] is None:
                ds = dout
            else:
                scale = _QKV_SCALE[part]
                pieces = []
                for hh in range(DN_HEADS):
                    blk = s[:, hh * HEAD:(hh + 1) * HEAD]
                    dblk = dout[:, hh * HEAD:(hh + 1) * HEAD]
                    r = lax.rsqrt(jnp.sum(blk * blk, axis=-1, keepdims=True) + EPS)
                    pieces.append(scale * r * (dblk - blk * (r * r) * jnp.sum(dblk * blk, axis=-1, keepdims=True)))
                ds = jnp.concatenate(pieces, axis=1)
            dxc = ds * _dsilu(xc)
            row = lax.broadcasted_iota(jnp.int32, (tm + HALO, 1), 0)
            dxc = jnp.where(jnp.logical_and(i == n_tiles - 1, row >= tm), 0.0, dxc)
            o_ref[:, part * D:(part + 1) * D] = _taps_fwd(dxc, w, 4, tm).astype(ACT)
            for k in range(4):
                sh = 3 - k
                shifted = (pltpu.roll(ext, sh, 0) if sh else ext)[HALO:HALO + tm, :]
                dw_ref[k:k + 1, part * D:(part + 1) * D] += jnp.sum(dxc[0:tm, :] * shifted, axis=0, keepdims=True)

    c0 = ZQ // D
    cols = (c0, c0 + 1, c0 + 2)
    tile = [pl.BlockSpec((tm, D), lambda i, c=c: (i, c)) for c in cols]
    dtile = [pl.BlockSpec((tm, D), lambda i, c=c: (i, c)) for c in range(3)]
    in_specs = (tile + [_prev_halo(tm, c) for c in cols] + [_next_halo(tm, c, n_tiles) for c in cols]
                + dtile + [_next_halo(tm, c, n_tiles) for c in range(3)] + [_full((HALO, 3 * D))])
    return _pcall(
        body, name=name, grid=(n_tiles,), in_specs=in_specs,
        out_specs=[pl.BlockSpec((tm, 3 * D), lambda i: (i, 0)), _full((HALO, 3 * D))],
        out_shape=[jax.ShapeDtypeStruct((t, 3 * D), ACT), jax.ShapeDtypeStruct((HALO, 3 * D), F32)],
        compiler_params=_cp(1),
    )(*([p] * 9), *([dqkv] * 6), cw)


def _tri_masks(n):
    row = lax.broadcasted_iota(jnp.int32, (n, n), 0)
    col = lax.broadcasted_iota(jnp.int32, (n, n), 1)
    return row, col


@jax.custom_vjp
def _unit_lower_inverses(mats):
    n = DN_CHUNK
    row, col = _tri_masks(n)
    eye = (row == col).astype(F32)
    same16 = (row // 16) == (col // 16)
    same32 = (row // 32) == (col // 32)
    pw = [jnp.where(same16, a, 0.0) for a in mats]
    x = [eye - p for p in pw]
    for _ in range(3):
        pw = [_hnn(p, p) for p in pw]
        x = [_hnn(xi, eye + p) for xi, p in zip(x, pw)]
    for keep in (jnp.logical_and(same32, jnp.logical_not(same16)), jnp.logical_not(same32)):
        inner = [_hnn(jnp.where(keep, a, 0.0), xi) for a, xi in zip(mats, x)]
        x = [xi - _hnn(xi, y) for xi, y in zip(x, inner)]
    return tuple(x)


def _uli_fwd(mats):
    t = _unit_lower_inverses(mats)
    return t, t


def _uli_bwd(ts, gs):
    inner = [_hnt(g, t) for g, t in zip(gs, ts)]
    return (tuple(-_htn(t, y) for t, y in zip(ts, inner)),)


_unit_lower_inverses.defvjp(_uli_fwd, _uli_bwd)


@jax.custom_vjp
def _known_inverses(mats, ts):
    return ts


_known_inverses.defvjp(lambda mats, ts: (ts, ts),
                       lambda ts, gs: (_uli_bwd(ts, gs)[0], tuple(jnp.zeros_like(t) for t in ts)))


def _pick_col(m, k):
    return jnp.sum(m * _onehot_row(m.shape[1], k), axis=1, keepdims=True)


def _pick_row(m, k):
    hot = (lax.broadcasted_iota(jnp.int32, (m.shape[0], 1), 0) == k).astype(F32)
    return jnp.sum(m * hot, axis=0, keepdims=True)


def _delta_chunk(states, qs, ks, vs, ab, alog, dtb, onw, known_inverses=None):
    n = DN_CHUNK
    heads = range(DN_HEADS)
    row, col = _tri_masks(n)
    incl = row >= col
    lane = lax.broadcasted_iota(jnp.int32, (1, HEAD), 1)
    g_all = jnp.where(lane < DN_HEADS, -jnp.exp(alog) * jax.nn.softplus(ab + dtb), 0.0)
    c_cols = _hnn(incl.astype(F32), g_all)
    c_rows = _htn(g_all, (row <= col).astype(F32))
    g_tot = jnp.sum(g_all, axis=0, keepdims=True)
    beta_all = jax.nn.sigmoid(ab)
    ccol = [_pick_col(c_cols, h) for h in heads]
    crow = [_pick_row(c_rows, h) for h in heads]
    gl = [_pick_col(g_tot, h) for h in heads]
    beta = [_pick_col(beta_all, DN_HEADS + h) for h in heads]
    decay = [jnp.exp(jnp.where(incl, ccol[h] - crow[h], -1e30)) for h in heads]
    eg = [jnp.exp(ccol[h]) for h in heads]
    kb = [ks[h] * beta[h] for h in heads]
    amat = [jnp.where(row > col, _nt(kb[h], ks[h]) * decay[h], 0.0) for h in heads]
    if known_inverses is None:
        tmat = _unit_lower_inverses(tuple(amat))
    else:
        tmat = _known_inverses(tuple(amat), tuple(known_inverses))
    u = [_nn(tmat[h], vs[h] * beta[h]) for h in heads]
    w = [_nn(tmat[h], kb[h] * eg[h]) for h in heads]
    qk = [_nt(qs[h], ks[h]) * decay[h] for h in heads]
    v_new = [u[h] - _nn(w[h], states[h]) for h in heads]
    o = [_nn(qs[h] * eg[h], states[h]) + _nn(qk[h], v_new[h]) for h in heads]
    new_states = [states[h] * jnp.exp(gl[h]) + _tn(ks[h] * jnp.exp(gl[h] - ccol[h]), v_new[h]) for h in heads]
    ys = [o[h] * lax.rsqrt(jnp.mean(o[h] * o[h], axis=-1, keepdims=True) + EPS) * onw for h in heads]
    return (ys, new_states), tmat


def _head_cols(ref):
    return [ref[:, h * HEAD:(h + 1) * HEAD].astype(F32) for h in range(DN_HEADS)]


def _delta_fwd(qkv, ab, alog, dtb, onw, name):
    t = qkv.shape[0]
    nc = t // DN_CHUNK

    def body(q_ref, k_ref, v_ref, ab_ref, alog_ref, dtb_ref, onw_ref, y_ref, keep_ref, inv_ref, state):
        @pl.when(pl.program_id(0) == 0)
        def _():
            state[...] = jnp.zeros_like(state)

        s0 = [state[hh] for hh in range(DN_HEADS)]
        (ys, s1), tmat = _delta_chunk(s0, _head_cols(q_ref), _head_cols(k_ref), _head_cols(v_ref), ab_ref[...],
                                      alog_ref[...], dtb_ref[...], onw_ref[...])
        for hh in range(DN_HEADS):
            keep_ref[0, hh] = s0[hh]
            inv_ref[0, hh] = tmat[hh]
            state[hh] = s1[hh]
            y_ref[:, hh * HEAD:(hh + 1) * HEAD] = ys[hh].astype(ACT)

    chunk = [pl.BlockSpec((DN_CHUNK, D), lambda i, c=c: (i, c)) for c in range(3)]
    return _pcall(
        body, name=name, grid=(nc,),
        in_specs=chunk + [pl.BlockSpec((DN_CHUNK, HEAD), lambda i: (i, 0)), _full((1, HEAD)), _full((1, HEAD)),
                          _full((1, HEAD))],
        out_specs=[pl.BlockSpec((DN_CHUNK, D), lambda i: (i, 0)),
                   pl.BlockSpec((1, DN_HEADS, HEAD, HEAD), lambda i: (i, 0, 0, 0)),
                   pl.BlockSpec((1, DN_HEADS, DN_CHUNK, DN_CHUNK), lambda i: (i, 0, 0, 0))],
        out_shape=[jax.ShapeDtypeStruct((t, D), ACT), jax.ShapeDtypeStruct((nc, DN_HEADS, HEAD, HEAD), F32),
                   jax.ShapeDtypeStruct((nc, DN_HEADS, DN_CHUNK, DN_CHUNK), F32)],
        scratch_shapes=[pltpu.VMEM((DN_HEADS, HEAD, HEAD), F32)], compiler_params=_cp(1),
    )(qkv, qkv, qkv, ab, alog, dtb, onw)


def _delta_bwd(qkv, ab, alog, dtb, onw, keep, inv, dy, name):
    t = qkv.shape[0]
    nc = t // DN_CHUNK

    def body(q_ref, k_ref, v_ref, ab_ref, alog_ref, dtb_ref, onw_ref, keep_ref, inv_ref, dy_ref,
             dqkv_ref, dab_ref, dalog_ref, ddtb_ref, donw_ref, dstate):
        @pl.when(pl.program_id(0) == 0)
        def _():
            dstate[...] = jnp.zeros_like(dstate)
            dalog_ref[...] = jnp.zeros_like(dalog_ref)
            ddtb_ref[...] = jnp.zeros_like(ddtb_ref)
            donw_ref[...] = jnp.zeros_like(donw_ref)

        s0 = [keep_ref[0, hh] for hh in range(DN_HEADS)]
        known = [inv_ref[0, hh] for hh in range(DN_HEADS)]
        _, vjp, _ = jax.vjp(functools.partial(_delta_chunk, known_inverses=known), s0, _head_cols(q_ref),
                            _head_cols(k_ref), _head_cols(v_ref), ab_ref[...], alog_ref[...], dtb_ref[...], onw_ref[...],
                            has_aux=True)
        ds, dq, dk, dv, dab, dal, ddt, don = vjp((_head_cols(dy_ref), [dstate[hh] for hh in range(DN_HEADS)]))
        for hh in range(DN_HEADS):
            lo = hh * HEAD
            dstate[hh] = ds[hh]
            dqkv_ref[:, lo:lo + HEAD] = dq[hh].astype(ACT)
            dqkv_ref[:, D + lo:D + lo + HEAD] = dk[hh].astype(ACT)
            dqkv_ref[:, 2 * D + lo:2 * D + lo + HEAD] = dv[hh].astype(ACT)
        dab_ref[...] = dab
        dalog_ref[...] += dal
        ddtb_ref[...] += ddt
        donw_ref[...] += don

    rev = lambda i: nc - 1 - i
    chunk = [pl.BlockSpec((DN_CHUNK, D), lambda i, c=c: (rev(i), c)) for c in range(3)]
    small = jax.ShapeDtypeStruct((1, HEAD), F32)
    return _pcall(
        body, name=name, grid=(nc,),
        in_specs=chunk + [pl.BlockSpec((DN_CHUNK, HEAD), lambda i: (rev(i), 0)), _full((1, HEAD)), _full((1, HEAD)),
                          _full((1, HEAD)), pl.BlockSpec((1, DN_HEADS, HEAD, HEAD), lambda i: (rev(i), 0, 0, 0)),
                          pl.BlockSpec((1, DN_HEADS, DN_CHUNK, DN_CHUNK), lambda i: (rev(i), 0, 0, 0)),
                          pl.BlockSpec((DN_CHUNK, D), lambda i: (rev(i), 0))],
        out_specs=[pl.BlockSpec((DN_CHUNK, 3 * D), lambda i: (rev(i), 0)),
                   pl.BlockSpec((DN_CHUNK, HEAD), lambda i: (rev(i), 0)), _full((1, HEAD)), _full((1, HEAD)),
                   _full((1, HEAD))],
        out_shape=[jax.ShapeDtypeStruct((t, 3 * D), ACT), jax.ShapeDtypeStruct((t, HEAD), F32), small, small, small],
        scratch_shapes=[pltpu.VMEM((DN_HEADS, HEAD, HEAD), F32)], compiler_params=_cp(1),
    )(qkv, qkv, qkv, ab, alog, dtb, onw, keep, inv, dy)


def _loss_head(y, target):
    t = y.shape[0]
    tm = min(512, t)

    def body(y_ref, t_ref, l_ref, dy_ref):
        @pl.when(pl.program_id(0) == 0)
        def _():
            l_ref[...] = jnp.zeros_like(l_ref)

        diff = y_ref[...] - t_ref[...]
        dy_ref[...] = diff * (1.0 / D)
        l_ref[...] += 0.5 * jnp.sum(jnp.sum(diff * diff, axis=-1, keepdims=True) * (1.0 / D), axis=0, keepdims=True)

    row = pl.BlockSpec((tm, D), lambda i: (i, 0))
    return _pcall(body, name="loss_head", grid=(t // tm,), in_specs=[row, row], out_specs=[_full((8, 128)), row],
                  out_shape=[jax.ShapeDtypeStruct((8, 128), F32), jax.ShapeDtypeStruct((t, D), F32)],
                  compiler_params=_cp(1))(y, target)


_SMALL_SHARDED = (("a_ln_w", (2, 128), 1), ("a_ln_b", (2, 128), 1), ("b_conv_w", (1, 3, 128), 2),
                  ("c_conv_w", (1, 4, 384), 2))
_REPLICATED = (("mem_norm_w", (1024,)), ("norm_pre", (4, 1024)), ("norm_post", (4, 1024)),
               ("a_b_s", (2, 8, 128)), ("c_a_log", (1, 8)), ("c_dt_bias", (1, 8)), ("c_o_norm_w", (1, 128)))


def _layer_shards(given, prefix):
    w_out = given[prefix + "w_out"]
    w_ins = [given[prefix + "a_w_in"][0], given[prefix + "b_w_in"], given[prefix + "c_w_in"], given[prefix + "a_w_in"][1]]
    return [[w_out[i], w_ins[i]] for i in range(4)]


def _as_matmul_operand(w):
    return w.reshape(w.shape[-2:]).astype(MXU)


def _as_rows(shape):
    return (math.prod(shape[:-1]) if len(shape) > 1 else 1, shape[-1])


def _row_table(shapes):
    table, at = [], 0
    for shape in shapes:
        rows, cols = _as_rows(shape)
        table.append((at, rows, cols))
        at += rows
    return table


def _pack_rows(arrays, shapes, rows, lead=()):
    n_lead = len(lead)
    parts = [jnp.pad(a.reshape(lead + _as_rows(s)), [(0, 0)] * (n_lead + 1) + [(0, D - s[-1])])
             for a, s in zip(arrays, shapes)]
    block = jnp.concatenate(parts, axis=n_lead)
    return jnp.pad(block, [(0, 0)] * n_lead + [(0, rows - block.shape[n_lead]), (0, 0)])


def _unpack_rows(packed, shapes, lead=()):
    n_lead = len(lead)
    return [packed[(slice(None),) * n_lead + (slice(r0, r0 + nr), slice(0, nc))].reshape(lead + s)
            for (r0, nr, nc), s in zip(_row_table(shapes), shapes)]


def _join_shards(blocks, axis):
    moved = jnp.moveaxis(blocks, 0, axis)
    shape = moved.shape
    return moved.reshape(shape[:axis] + (shape[axis] * shape[axis + 1],) + shape[axis + 2:])


def _split_shards(full, axis):
    shape = full.shape
    split = full.reshape(shape[:axis] + (N_DEV, shape[axis] // N_DEV) + shape[axis + 1:])
    return jnp.moveaxis(split, axis, 0)


_A_COLS = ((2560, 1536), (2048, 512), (0, 2048))
_BC_COLS = ((3584, 1536), (3072, 512), (0, 3072))
_C_COLS = ((3600, 1536), (3088, 512), (0, 3072))


def _reorder_cols(w, cols):
    return jnp.concatenate([w[:, s:s + n] for s, n in cols], axis=1)


def _restore_cols(pieces_in_my_order, cols, extra=()):
    placed = sorted(list(zip([s for s, _ in cols], pieces_in_my_order)) + list(extra), key=lambda sp: sp[0])
    return jnp.concatenate([piece for _, piece in placed], axis=1)


def kernel(x, mem, mem_norm_w, w_mem_kv, norm_pre, norm_post, w_out, a_w_in, a_ln_w, a_ln_b, a_w_s, a_b_s, b_w_in, b_conv_w, c_w_in, c_conv_w, c_a_log, c_dt_bias, c_o_norm_w, loss_target, m_mem_norm_w, m_w_mem_kv, m_norm_pre, m_norm_post, m_w_out, m_a_w_in, m_a_ln_w, m_a_ln_b, m_a_w_s, m_a_b_s, m_b_w_in, m_b_conv_w, m_c_w_in, m_c_conv_w, m_c_a_log, m_c_dt_bias, m_c_o_norm_w, v_mem_norm_w, v_w_mem_kv, v_norm_pre, v_norm_post, v_w_out, v_a_w_in, v_a_ln_w, v_a_ln_b, v_a_w_s, v_a_b_s, v_b_w_in, v_b_conv_w, v_c_w_in, v_c_conv_w, v_c_a_log, v_c_dt_bias, v_c_o_norm_w):
    given = dict(locals())
    x0 = x[0]
    mem0 = mem[0]
    target = loss_target[0]

    w_sh, m_sh, v_sh = (_layer_shards(given, pre) for pre in ("", "m_", "v_"))
    small_names = [n for n, _, _ in _SMALL_SHARDED]
    small_shapes = [s for _, s, _ in _SMALL_SHARDED]
    repl_names = [n for n, _ in _REPLICATED]
    repl_shapes = [s for _, s in _REPLICATED]
    small = _pack_rows([given[n] for n in small_names], small_shapes, R_SMALL)
    g_wkv, g_wout0, g_in0, g_small = _all_gather(
        [w_mem_kv.astype(MXU)] + [_as_matmul_operand(w) for w in w_sh[0]] + [small], "gather_weights")
    small_full = _unpack_rows(g_small, small_shapes, lead=(N_DEV,))
    full = {n: _join_shards(blocks, ax) for (n, _, ax), blocks in zip(_SMALL_SHARDED, small_full)}
    wkv = g_wkv.reshape(D, 2 * D_XA)
    wouts = [g_wout0.reshape(D_CAT, D)]
    w_in = [_reorder_cols(_join_shards(g_in0, 1), _A_COLS)]
    w_cab = None
    lnw = [full["a_ln_w"][j][None, :] for j in range(2)]
    lnb = [full["a_ln_b"][j][None, :] for j in range(2)]
    ws = [a_w_s[j] for j in range(2)]
    bst = [jnp.pad(a_b_s[j].T, ((0, 0), (0, HEAD - GM_GROUPS))) for j in range(2)]
    cw_b = jnp.pad(full["b_conv_w"][0], ((0, HALO - 3), (0, 0)))
    cw_c = jnp.pad(full["c_conv_w"][0], ((0, HALO - 4), (0, 0)))
    alog = jnp.pad(c_a_log, ((0, 0), (0, HEAD - DN_HEADS)))
    dtb = jnp.pad(c_dt_bias, ((0, 0), (0, HEAD - DN_HEADS)))
    onw = c_o_norm_w
    mw = mem_norm_w[None, :]

    kv = _memkv_fwd(mem0, mw, wkv).astype(MXU)
    xs, saved = [x0], []
    for i in range(4):
        kind = i % 3
        npre, npost = norm_pre[i][None, :], norm_post[i][None, :]
        ahead = [_as_matmul_operand(w) for w in w_sh[i + 1]] if i < 3 else []
        res = _proj_fwd(xs[i], npre, w_in[i], w_cab if kind == 2 else None, f"proj_fwd_{i}", gather=ahead)
        if ahead:
            g_wout, g_in = res[-2:]
            wouts.append(g_wout.reshape(D_CAT, D))
            if i + 1 == 2:
                c_full = jnp.concatenate([g_in[d] for d in range(N_DEV)], axis=1)
                w_in.append(_reorder_cols(c_full, _C_COLS))
                w_cab = jnp.pad(c_full[:, 3072:3088], ((0, 0), (0, HEAD - 16)))
            else:
                w_in.append(_reorder_cols(_join_shards(g_in, 1), _BC_COLS if i + 1 == 1 else _A_COLS))
        if kind == 2:
            p, h, ab = res[:3]
            qkv = _qkv_fwd(p, cw_c, f"qkv_fwd_{i}")
            ymix, keep, inv = _delta_fwd(qkv, ab, alog, dtb, onw, f"delta_fwd_{i}")
            extra = (qkv, ab, keep, inv)
        else:
            p, h = res[:2]
            if kind == 0:
                ymix = _gmlp_fwd(p, lnw[i // 3], lnb[i // 3], ws[i // 3], bst[i // 3], f"gmlp_fwd_{i}")
            else:
                ymix = _sconv_fwd(p, cw_b, f"sconv_fwd_{i}")
            extra = ()
        xn, o = _tail_fwd(p, ymix, xs[i], kv, wouts[i], npost, f"tail_fwd_{i}")
        xs.append(xn)
        saved.append((p, h, ymix, o, extra))

    loss_tile, dx = _loss_head(xs[4], target)
    loss = lax.psum(loss_tile[0, 0], ("x", "y", "c"))

    g = {}
    d_npre, d_npost = [None] * 4, [None] * 4
    d_ws, d_bs, d_lnw, d_lnb = [None] * 2, [None] * 2, [None] * 2, [None] * 2
    dkv = None
    pending, landed = [], [None] * 4
    for i in reversed(range(4)):
        kind = i % 3
        p, h, ymix, o, extra = saved[i]
        npre, npost = norm_pre[i][None, :], norm_post[i][None, :]
        res = _tail_bwd(dx, o, p, ymix, kv, wouts[i], npost, f"tail_bwd_{i}", exchange=pending)
        dzq, dymix, d_wout, d_npost[i], dkv_i = res[:5]
        if pending:
            landed[i + 1] = res[5:]
        dkv = dkv_i if dkv is None else dkv + dkv_i
        w_zq, w_mix = w_in[i][:, :ZQ], w_in[i][:, ZQ:]
        dw_zq = _matmul_tn(h, dzq, f"dw_zq_{i}")
        if kind == 0:
            j = i // 3
            early_x, early_b = [], []
            if i == 0:
                d_wkv, d_mw = _memkv_bwd(mem0, mw, wkv, dkv)
                early_x = [d_wkv.reshape(N_DEV, 128, D).astype(XCH), d_wout.reshape(N_DEV, D_CAT // N_DEV, D).astype(XCH)]
                early_b = [d_ws[1].reshape(GM_CHUNK, D).astype(XCH)]
            (dmix, d_lnw[j], d_lnb[j], d_ws[j], dbst), early_landed = _gmlp_bwd(
                p, dymix, lnw[j], lnb[j], ws[j], bst[j], f"gmlp_bwd_{i}", exchange=early_x, bcast=early_b)
            d_bs[j] = dbst[:, :GM_GROUPS].T
            dw_mix = _matmul_tn(h, dmix, f"dw_mix_{i}")
            d_win = _restore_cols([dw_zq[:, :D_CAT], dw_zq[:, D_CAT:], dw_mix], _A_COLS)
            dps, wparts = [dzq, dmix], [w_zq, w_mix]
        elif kind == 1:
            dmix, dcw = _sconv_bwd(p, dymix, cw_b, f"sconv_bwd_{i}")
            g["b_conv_w"] = dcw[None, :3]
            dw_mix = _matmul_tn(h, dmix, f"dw_mix_{i}")
            d_win = _restore_cols([dw_zq[:, :D_CAT], dw_zq[:, D_CAT:], dw_mix], _BC_COLS)
            dps, wparts = [dzq, dmix], [w_zq, w_mix]
        else:
            qkv, ab, keep, inv = extra
            dqkv, dab, dalog, ddtb, donw = _delta_bwd(qkv, ab, alog, dtb, onw, keep, inv, dymix, f"delta_bwd_{i}")
            dmix, dcw = _qkv_bwd(p, dqkv, cw_c, f"qkv_bwd_{i}")
            g["c_conv_w"] = dcw[None, :4]
            g["c_a_log"], g["c_dt_bias"], g["c_o_norm_w"] = dalog[:, :DN_HEADS], ddtb[:, :DN_HEADS], donw
            dw_mix = _matmul_tn(h, dmix, f"dw_mix_{i}")
            dw_ab = _matmul_tn(h, dab, f"dw_ab_{i}")
            d_win = _restore_cols([dw_zq[:, :D_CAT], dw_zq[:, D_CAT:], dw_mix], _C_COLS, extra=[(3072, dw_ab[:, :16])])
            dps, wparts = [dzq, dmix, dab], [w_zq, w_mix, w_cab]
        width = d_win.shape[1] // N_DEV
        pending = [] if i == 0 else [d_wout.reshape(N_DEV, D_CAT // N_DEV, D).astype(XCH)]
        pending.append(jnp.stack([d_win[:, d * width:(d + 1) * width] for d in range(N_DEV)]).astype(XCH))
        if i > 0:
            dx, d_npre[i] = _proj_bwd_x(dps, wparts, xs[i], dx, npre, f"proj_bwd_x_{i}")

    g["mem_norm_w"] = d_mw[0]
    g["norm_pre"] = jnp.concatenate([jnp.zeros((1, D), F32)] + d_npre[1:], axis=0)
    g["norm_post"] = jnp.concatenate(d_npost, axis=0)
    g["a_ln_w"] = jnp.concatenate(d_lnw, axis=0)
    g["a_ln_b"] = jnp.concatenate(d_lnb, axis=0)
    g["a_b_s"] = jnp.stack(d_bs)
    e_small = _pack_rows([_split_shards(g[n], ax) for n, _, ax in _SMALL_SHARDED], small_shapes, R_SMALL, lead=(N_DEV,))
    r_pack = _pack_rows([g[n] for n in repl_names], repl_shapes, R_REPL)
    dx, d_npre0, l_in, l_small, ws0_all, r_all = _proj_bwd_x(
        dps, wparts, xs[0], dx, norm_pre[0][None, :], "proj_bwd_x_0", exchange=pending + [e_small],
        bcast=[d_ws[0].reshape(GM_CHUNK, D).astype(XCH), r_pack])
    l_wkv, l_wout0, ws3_all = early_landed
    landed[0] = (l_wout0, l_in)
    npre0_all, = _all_gather([jnp.pad(d_npre0, ((0, HALO - 1), (0, 0)))], "gather_norm_pre0")
    r_all = r_all.at[:, 1, :].set(npre0_all[:, 0, :])

    res = [[_reduce_adamw(landed[i][a], w_sh[i][a], m_sh[i][a], v_sh[i][a], f"adamw_{i}_{a}") for a in range(2)]
           for i in range(4)]
    res_wkv = _reduce_adamw(l_wkv, w_mem_kv, m_w_mem_kv, v_w_mem_kv, "adamw_w_mem_kv")
    res_small = _reduce_adamw_vectors(
        l_small, small, *(_pack_rows([given[pre + n] for n in small_names], small_shapes, R_SMALL) for pre in ("m_", "v_")),
        small_shapes, "adamw_small")
    res_ws = [_reduce_adamw(parts, *(given[pre + "a_w_s"][j].reshape(GM_CHUNK, D) for pre in ("", "m_", "v_")),
                            f"adamw_a_w_s_{j}") for j, parts in enumerate((ws0_all, ws3_all))]
    res_repl = _reduce_adamw_vectors(
        r_all, *(_pack_rows([given[pre + n] for n in repl_names], repl_shapes, R_REPL) for pre in ("", "m_", "v_")),
        repl_shapes, "adamw_replicated")

    order = ["mem_norm_w", "w_mem_kv", "norm_pre", "norm_post", "w_out", "a_w_in", "a_ln_w", "a_ln_b", "a_w_s", "a_b_s",
             "b_w_in", "b_conv_w", "c_w_in", "c_conv_w", "c_a_log", "c_dt_bias", "c_o_norm_w"]
    outs = [loss, dx[None]]
    for kind in range(4):
        got = dict(zip(repl_names, res_repl[kind]))
        got["a_w_s"] = jnp.stack([r[kind] for r in res_ws]).reshape(a_w_s.shape)
        got.update(zip(small_names, res_small[kind]))
        got["w_mem_kv"] = res_wkv[kind]
        got["w_out"] = jnp.stack([res[i][0][kind] for i in range(4)])
        got["a_w_in"] = jnp.stack([res[0][1][kind], res[3][1][kind]])
        got["b_w_in"] = res[1][1][kind]
        got["c_w_in"] = res[2][1][kind]
        outs += [got[n] for n in order]
    return tuple(outs)
```

```python
import functools
import math

import jax
import jax.numpy as jnp
from jax import lax
from jax.experimental import pallas as pl
from jax.experimental.pallas import tpu as pltpu

F32 = jnp.float32
MXU = jnp.bfloat16
ACT = jnp.bfloat16

D = 1024
D_XA = 512
D_CAT = 1536
N_MEM = 256
XA_HEADS = 4
HEAD = 128
ZQ = D_CAT + D_XA
EPS = 1e-6
GM_CHUNK = 128
GM_CHUNKS_PER_STEP = 4
GM_GROUPS = 8
DN_HEADS = 8
DN_CHUNK = 64
N_DEV = 8
HALO = 8
VMEM_LIMIT = 56 * 1024 * 1024
XCH = jnp.bfloat16
R_REPL = 32
R_SMALL = 16

ADAM_LR = 0.001
ADAM_B1 = 0.9
ADAM_B2 = 0.999
ADAM_EPS = 1e-08
ADAM_WD = 0.01
ADAM_STEP = 10

NN = ((1,), (0,))
NT = ((1,), (1,))
TN = ((0,), (0,))
MESH = pl.DeviceIdType.MESH


def _pcall(body, **kw):
    return pl.pallas_call(body, **kw)


def _cp(n_axes):
    return pltpu.CompilerParams(dimension_semantics=("arbitrary",) * n_axes, vmem_limit_bytes=VMEM_LIMIT)


def _dot(a, b, dims, prec=None):
    return lax.dot_general(a, b, (dims, ((), ())), preferred_element_type=F32, precision=prec)


def _mdot(a, b, dims):
    return _dot(a.astype(MXU), b.astype(MXU), dims)


def _make_mms(raw):
    @jax.custom_vjp
    def nn(a, b):
        return raw(a, b, NN)

    @jax.custom_vjp
    def nt(a, b):
        return raw(a, b, NT)

    @jax.custom_vjp
    def tn(a, b):
        return raw(a, b, TN)

    nn.defvjp(lambda a, b: (nn(a, b), (a, b)), lambda r, g: (nt(g, r[1]), tn(r[0], g)))
    nt.defvjp(lambda a, b: (nt(a, b), (a, b)), lambda r, g: (nn(g, r[1]), tn(g, r[0])))
    tn.defvjp(lambda a, b: (tn(a, b), (a, b)), lambda r, g: (nt(r[1], g), nn(r[0], g)))
    return nn, nt, tn


_nn, _nt, _tn = _make_mms(_mdot)


def _split_dot(a, b, dims):
    ah = a.astype(jnp.bfloat16)
    bh = b.astype(jnp.bfloat16)
    al = (a - ah.astype(F32)).astype(jnp.bfloat16)
    bl = (b - bh.astype(F32)).astype(jnp.bfloat16)
    return _dot(ah, bh, dims) + (_dot(ah, bl, dims) + _dot(al, bh, dims))


_hnn, _hnt, _htn = _make_mms(_split_dot)


def _full(shape):
    return pl.BlockSpec(shape, lambda *_: (0,) * len(shape))


def _silu(z):
    return z * jax.nn.sigmoid(z)


def _dsilu(z):
    s = jax.nn.sigmoid(z)
    return s * (1.0 + z * (1.0 - s))


def _onehot_row(n, k):
    return (lax.broadcasted_iota(jnp.int32, (1, n), 1) == k).astype(F32)


_HBM = pl.BlockSpec(memory_space=pl.ANY)


def _sem_shapes(n_remote, n_local):
    return [pltpu.SemaphoreType.DMA((n_remote,)), pltpu.SemaphoreType.DMA((n_remote,)),
            pltpu.SemaphoreType.DMA((n_local,))]


def _gather_parts(x_refs, out_refs, send_sems, recv_sems, local_sems):
    n = len(x_refs)
    x, y, cc = lax.axis_index("x"), lax.axis_index("y"), lax.axis_index("c")
    me, sibling = (x, y, cc), (x, y, 1 - cc)
    chips = [(1 - x, y), (x, 1 - y), (1 - x, 1 - y)]

    def slot(a, px, py, pc):
        return out_refs[a].at[4 * px + 2 * py + pc]

    def copy(k, a, block, to, src=None):
        return pltpu.make_async_remote_copy(
            src_ref=slot(a, *block) if src is None else src, dst_ref=slot(a, *block),
            send_sem=send_sems.at[k * n + a], recv_sem=recv_sems.at[k * n + a], device_id=to, device_id_type=MESH)

    mine = [pltpu.make_async_copy(x_refs[a], slot(a, *me), local_sems.at[a]) for a in range(n)]
    first = [copy(0, a, me, sibling, src=x_refs[a]) for a in range(n)]
    first += [copy(1 + j, a, me, (*chip, cc), src=x_refs[a]) for j, chip in enumerate(chips) for a in range(n)]

    def begin():
        for cp in mine + first:
            cp.start()

    def end():
        passed = []
        for j, chip in enumerate(chips):
            for a in range(n):
                copy(1 + j, a, (*chip, cc), me).wait_recv()
                passed.append(copy(4 + j, a, (*chip, cc), sibling))
                passed[-1].start()
        for a in range(n):
            copy(0, a, sibling, me).wait_recv()
        for j, chip in enumerate(chips):
            for a in range(n):
                copy(4 + j, a, (*chip, 1 - cc), me).wait_recv()
        for cp in first + passed:
            cp.wait_send()
        for cp in mine:
            cp.wait()

    return begin, end


def _exchange_parts(g_refs, out_refs, b_refs, ball_refs, send_sems, recv_sems, local_sems):
    n, nb = len(g_refs), len(b_refs)
    per_peer = n + nb
    x, y, cc = lax.axis_index("x"), lax.axis_index("y"), lax.axis_index("c")
    my_idx = 4 * x + 2 * y + cc
    mine = [pltpu.make_async_copy(g_refs[a].at[my_idx], out_refs[a].at[my_idx], local_sems.at[a]) for a in range(n)]
    mine += [pltpu.make_async_copy(b_refs[a], ball_refs[a].at[my_idx], local_sems.at[n + a]) for a in range(nb)]
    copies = []
    for k in range(1, N_DEV):
        px = 1 - x if k & 4 else x
        py = 1 - y if k & 2 else y
        pc = 1 - cc if k & 1 else cc
        base = (k - 1) * per_peer
        for a in range(n):
            copies.append(pltpu.make_async_remote_copy(
                src_ref=g_refs[a].at[4 * px + 2 * py + pc], dst_ref=out_refs[a].at[my_idx],
                send_sem=send_sems.at[base + a], recv_sem=recv_sems.at[base + a],
                device_id=(px, py, pc), device_id_type=MESH))
        for a in range(nb):
            copies.append(pltpu.make_async_remote_copy(
                src_ref=b_refs[a], dst_ref=ball_refs[a].at[my_idx], send_sem=send_sems.at[base + n + a],
                recv_sem=recv_sems.at[base + n + a], device_id=(px, py, pc), device_id_type=MESH))

    def begin():
        for cp in mine + copies:
            cp.start()

    def end():
        for cp in copies:
            cp.wait_recv()
        for cp in copies:
            cp.wait_send()
        for cp in mine:
            cp.wait()

    return begin, end


def _pcall_hosting(body, args, exchange, bcast, *, name, grid, in_specs, out_specs, out_shape, compiler_params):
    n_in, n_out, n_x, n_b = len(in_specs), len(out_specs), len(exchange), len(bcast)
    n_c, n_steps = n_x + n_b, grid[0]

    def hosted(*refs):
        ins, c_in = refs[:n_in], refs[n_in:n_in + n_c]
        outs, c_out = refs[n_in + n_c:n_in + n_c + n_out], refs[n_in + n_c + n_out:n_in + 2 * n_c + n_out]
        begin, end = _exchange_parts(c_in[:n_x], c_out[:n_x], c_in[n_x:], c_out[n_x:], *refs[n_in + 2 * n_c + n_out:])
        pl.when(pl.program_id(0) == 0)(begin)
        body(*ins, *outs)
        pl.when(pl.program_id(0) == n_steps - 1)(end)

    landed_shape = [jax.ShapeDtypeStruct(e.shape, e.dtype) for e in exchange]
    landed_shape += [jax.ShapeDtypeStruct((N_DEV,) + b.shape, b.dtype) for b in bcast]
    res = _pcall(hosted, name=name, grid=grid, in_specs=list(in_specs) + [_HBM] * n_c,
                 out_specs=list(out_specs) + [_HBM] * n_c, out_shape=list(out_shape) + landed_shape,
                 scratch_shapes=_sem_shapes(7 * n_c, n_c), compiler_params=compiler_params)(*args, *exchange, *bcast)
    return res[:n_out], res[n_out:]


def _all_gather(blks, name):
    n = len(blks)

    def body(*refs):
        begin, end = _gather_parts(refs[:n], refs[n:2 * n], *refs[2 * n:])
        begin()
        end()

    return _pcall(body, name=name, out_shape=[jax.ShapeDtypeStruct((N_DEV,) + b.shape, b.dtype) for b in blks],
                  in_specs=[_HBM] * n, out_specs=[_HBM] * n, scratch_shapes=_sem_shapes(7 * n, n))(*blks)


def _sum_and_adamw(p_ref, w, m, v):
    g = p_ref[0].astype(F32)
    for s in range(1, N_DEV):
        g = g + p_ref[s].astype(F32)
    nm = ADAM_B1 * m + (1.0 - ADAM_B1) * g
    nv = ADAM_B2 * v + (1.0 - ADAM_B2) * (g * g)
    m_hat = nm / (1.0 - ADAM_B1 ** ADAM_STEP)
    v_hat = nv / (1.0 - ADAM_B2 ** ADAM_STEP)
    return g, -ADAM_LR * (m_hat / (jnp.sqrt(v_hat) + ADAM_EPS) + ADAM_WD * w), nm, nv


def _reduce_adamw_vectors(parts, w, m, v, shapes, name):
    table = _row_table(shapes)

    def body(p_ref, w_ref, m_ref, v_ref, *out_refs):
        results = _sum_and_adamw(p_ref, w_ref[...], m_ref[...], v_ref[...])
        for kind, val in enumerate(results):
            for t, (r0, nr, nc) in enumerate(table):
                out_refs[kind * len(table) + t][...] = val[r0:r0 + nr, 0:nc]

    outs = _pcall(body, name=name, out_shape=[jax.ShapeDtypeStruct((nr, nc), F32) for _, nr, nc in table] * 4)(parts, w, m, v)
    return [[outs[kind * len(table) + t].reshape(s) for t, s in enumerate(shapes)] for kind in range(4)]


def _reduce_adamw(parts, w, m, v, name):
    lead = w.shape[:-2]
    r, c = w.shape[-2:]
    tr = 128 if r % 128 == 0 else r
    zeros = (0,) * len(lead)
    at = zeros + (slice(None), slice(None))

    def body(p_ref, w_ref, m_ref, v_ref, g_ref, d_ref, nm_ref, nv_ref):
        g_ref[at], d_ref[at], nm_ref[at], nv_ref[at] = _sum_and_adamw(p_ref, w_ref[at], m_ref[at], v_ref[at])

    row = pl.BlockSpec((1,) * len(lead) + (tr, c), lambda i: zeros + (i, 0))
    out = jax.ShapeDtypeStruct(w.shape, F32)
    return _pcall(
        body, name=name, grid=(r // tr,),
        in_specs=[pl.BlockSpec((N_DEV, tr, c), lambda i: (0, i, 0)), row, row, row],
        out_specs=[row, row, row, row], out_shape=[out, out, out, out], compiler_params=_cp(1),
    )(parts, w, m, v)


def _proj_fwd(x, nw, w, wab, name, gather=()):
    t, npj = x.shape[0], w.shape[1]
    tm, tn = min(512, t), 1024
    has_ab = wab is not None
    n_in, n_out, n_g = 3 + has_ab, 2 + has_ab, len(gather)
    n_i = t // tm

    def body(*refs):
        ins, g_ins = refs[:n_in], refs[n_in:n_in + n_g]
        outs = refs[n_in + n_g:n_in + n_g + n_out]
        g_outs = refs[n_in + n_g + n_out:n_in + 2 * n_g + n_out]
        if has_ab:
            (x_ref, nw_ref, w_ref, wab_ref), (p_ref, h_ref, ab_ref) = ins, outs
        else:
            (x_ref, nw_ref, w_ref), (p_ref, h_ref) = ins, outs
        if n_g:
            begin, end = _gather_parts(g_ins, g_outs, *refs[n_in + 2 * n_g + n_out:])
            pl.when(pl.program_id(0) == 0)(begin)

        xv = x_ref[...]
        hv = (xv * lax.rsqrt(jnp.mean(xv * xv, axis=-1, keepdims=True) + EPS) * nw_ref[...]).astype(MXU)
        h_ref[...] = hv.astype(ACT)
        if has_ab:
            ab_ref[...] = _dot(hv, wab_ref[...], NN)
        for j in range(npj // tn):
            p_ref[:, j * tn:(j + 1) * tn] = _dot(hv, w_ref[:, j * tn:(j + 1) * tn], NN).astype(ACT)
        if n_g:
            pl.when(pl.program_id(0) == n_i - 1)(end)

    in_specs = [pl.BlockSpec((tm, D), lambda i: (i, 0)), _full((1, D)), _full((D, npj))]
    out_specs = [pl.BlockSpec((tm, npj), lambda i: (i, 0)), pl.BlockSpec((tm, D), lambda i: (i, 0))]
    out_shape = [jax.ShapeDtypeStruct((t, npj), ACT), jax.ShapeDtypeStruct((t, D), ACT)]
    args = [x, nw, w]
    if has_ab:
        in_specs.append(_full((D, HEAD)))
        out_specs.append(pl.BlockSpec((tm, HEAD), lambda i: (i, 0)))
        out_shape.append(jax.ShapeDtypeStruct((t, HEAD), F32))
        args.append(wab)
    scratch = []
    if n_g:
        in_specs += [_HBM] * n_g
        out_specs += [_HBM] * n_g
        out_shape += [jax.ShapeDtypeStruct((N_DEV,) + b.shape, b.dtype) for b in gather]
        args += list(gather)
        scratch += _sem_shapes(7 * n_g, n_g)
    return _pcall(body, name=name, grid=(n_i,), in_specs=in_specs, out_specs=out_specs,
                  out_shape=out_shape, scratch_shapes=scratch, compiler_params=_cp(1))(*args)


def _proj_bwd_x(dps, ws, x, dxn, nw, name, exchange=(), bcast=()):
    t = x.shape[0]
    tm = min(256, t)
    n, n_x, n_b = len(dps), len(exchange), len(bcast)
    n_c, n_steps = n_x + n_b, t // tm

    def body(*refs):
        dp_refs, w_refs = refs[:n], refs[n:2 * n]
        x_ref, dxn_ref, nw_ref = refs[2 * n:2 * n + 3]
        c_in = refs[2 * n + 3:2 * n + 3 + n_c]
        dx_ref, dnw_ref = refs[2 * n + 3 + n_c:2 * n + 5 + n_c]
        c_out = refs[2 * n + 5 + n_c:2 * n + 5 + 2 * n_c]
        if n_c:
            begin, end = _exchange_parts(c_in[:n_x], c_out[:n_x], c_in[n_x:], c_out[n_x:], *refs[2 * n + 5 + 2 * n_c:])
            pl.when(pl.program_id(0) == 0)(begin)
        dh = _mdot(dp_refs[0][...], w_refs[0][...], NT)
        for k in range(1, n):
            dh = dh + _mdot(dp_refs[k][...], w_refs[k][...], NT)
        xv = x_ref[...]
        r = lax.rsqrt(jnp.mean(xv * xv, axis=-1, keepdims=True) + EPS)

        @pl.when(pl.program_id(0) == 0)
        def _():
            dnw_ref[...] = jnp.zeros_like(dnw_ref)

        dnw_ref[...] += jnp.sum(dh * xv * r, axis=0, keepdims=True)
        dhw = dh * nw_ref[...]
        dx_ref[...] = dxn_ref[...] + r * (dhw - xv * (r * r) * jnp.mean(dhw * xv, axis=-1, keepdims=True))
        if n_c:
            pl.when(pl.program_id(0) == n_steps - 1)(end)

    row = pl.BlockSpec((tm, D), lambda i: (i, 0))
    in_specs = [pl.BlockSpec((tm, dp.shape[1]), lambda i: (i, 0)) for dp in dps]
    in_specs += [_full(w.shape) for w in ws]
    in_specs += [row, row, _full((1, D))] + [_HBM] * n_c
    out_shape = [jax.ShapeDtypeStruct((t, D), F32), jax.ShapeDtypeStruct((1, D), F32)]
    out_shape += [jax.ShapeDtypeStruct(e.shape, e.dtype) for e in exchange]
    out_shape += [jax.ShapeDtypeStruct((N_DEV,) + b.shape, b.dtype) for b in bcast]
    return _pcall(body, name=name, grid=(n_steps,), in_specs=in_specs, out_specs=[row, _full((1, D))] + [_HBM] * n_c,
                  out_shape=out_shape, scratch_shapes=_sem_shapes(7 * n_c, n_c) if n_c else [],
                  compiler_params=_cp(1))(*dps, *ws, x, dxn, nw, *exchange, *bcast)


def _matmul_tn(a, b, name, out_dtype=XCH):
    t, m = a.shape
    n = b.shape[1]
    tm, tn = min(1024, t), min(1024, n)
    n_t = t // tm

    def body(a_ref, b_ref, o_ref, acc):
        @pl.when(pl.program_id(1) == 0)
        def _():
            acc[...] = jnp.zeros_like(acc)

        acc[...] += _mdot(a_ref[...], b_ref[...], TN)

        @pl.when(pl.program_id(1) == n_t - 1)
        def _():
            o_ref[...] = acc[...].astype(out_dtype)

    return _pcall(body, name=name, grid=(n // tn, n_t),
                  in_specs=[pl.BlockSpec((tm, m), lambda j, i: (i, 0)), pl.BlockSpec((tm, tn), lambda j, i: (i, j))],
                  out_specs=pl.BlockSpec((m, tn), lambda j, i: (0, j)),
                  out_shape=jax.ShapeDtypeStruct((m, n), out_dtype), scratch_shapes=[pltpu.VMEM((m, tn), F32)],
                  compiler_params=_cp(2))(a, b)


def _memkv_fwd(mem, mw, wkv):
    def body(mem_ref, mw_ref, w_ref, kv_ref):
        mv = mem_ref[...]
        mn = mv * lax.rsqrt(jnp.mean(mv * mv, axis=-1, keepdims=True) + EPS) * mw_ref[...]
        kv_ref[...] = _mdot(mn, w_ref[...], NN)

    return _pcall(body, name="memkv_fwd", out_shape=jax.ShapeDtypeStruct((N_MEM, 2 * D_XA), F32),
                  compiler_params=pltpu.CompilerParams(vmem_limit_bytes=VMEM_LIMIT))(mem, mw, wkv)


def _memkv_bwd(mem, mw, wkv, dkv):
    def body(mem_ref, mw_ref, w_ref, dkv_ref, dw_ref, dmw_ref):
        mv = mem_ref[...]
        r = lax.rsqrt(jnp.mean(mv * mv, axis=-1, keepdims=True) + EPS)
        mn = mv * r * mw_ref[...]
        dkvv = dkv_ref[...]
        dw_ref[...] = _mdot(mn, dkvv, TN)
        dmn = _mdot(dkvv, w_ref[...], NT)
        dmw_ref[...] = jnp.sum(dmn * mv * r, axis=0, keepdims=True)

    return _pcall(body, name="memkv_bwd",
                  out_shape=[jax.ShapeDtypeStruct((D, 2 * D_XA), F32), jax.ShapeDtypeStruct((1, D), F32)],
                  compiler_params=pltpu.CompilerParams(vmem_limit_bytes=VMEM_LIMIT))(mem, mw, wkv, dkv)


def _attend(q, kv):
    heads = range(XA_HEADS)
    qs = [q[:, h * HEAD:(h + 1) * HEAD] for h in heads]
    ks = [kv[:, h * HEAD:(h + 1) * HEAD] for h in heads]
    vs = [kv[:, D_XA + h * HEAD:D_XA + (h + 1) * HEAD] for h in heads]
    ss = [_mdot(qs[h], ks[h], NT) * (HEAD ** -0.5) for h in heads]
    es = [jnp.exp(s - jnp.max(s, axis=-1, keepdims=True)) for s in ss]
    ps = [e / jnp.sum(e, axis=-1, keepdims=True) for e in es]
    return ps, [_mdot(ps[h], vs[h], NN) for h in heads]


def _tail_fwd(p, ymix, x, kv, wout, npost, name):
    t = x.shape[0]
    tm = min(512, t)

    def body(z_ref, q_ref, y_ref, x_ref, kv_ref, w_ref, np_ref, xn_ref, o_ref):
        _, outs = _attend(q_ref[...], kv_ref[...])
        cat = jnp.concatenate([y_ref[...]] + [a.astype(ACT) for a in outs], axis=1)
        g = cat * _silu(z_ref[...])
        o = _mdot(g, w_ref[...], NN)
        o_ref[...] = o
        xn_ref[...] = x_ref[...] + o * lax.rsqrt(jnp.mean(o * o, axis=-1, keepdims=True) + EPS) * np_ref[...]

    row = pl.BlockSpec((tm, D), lambda i: (i, 0))
    return _pcall(
        body, name=name, grid=(t // tm,),
        in_specs=[pl.BlockSpec((tm, D_CAT), lambda i: (i, 0)), pl.BlockSpec((tm, D_XA), lambda i: (i, D_CAT // D_XA)),
                  row, row, _full((N_MEM, 2 * D_XA)), _full((D_CAT, D)), _full((1, D))],
        out_specs=[row, row],
        out_shape=[jax.ShapeDtypeStruct((t, D), F32), jax.ShapeDtypeStruct((t, D), F32)], compiler_params=_cp(1),
    )(p, p, ymix, x, kv, wout, npost)


def _tail_bwd(dxn, o, p, ymix, kv, wout, npost, name, exchange=()):
    t = dxn.shape[0]
    tm = min(256, t)
    n_x, n_steps = len(exchange), t // tm

    def body(*refs):
        dxn_ref, o_ref, z_ref, q_ref, y_ref, kv_ref, w_ref, np_ref = refs[:8]
        dzq_ref, dy_ref, dw_ref, dnp_ref, dkv_ref = refs[8 + n_x:13 + n_x]
        if n_x:
            begin, end = _exchange_parts(refs[8:8 + n_x], refs[13 + n_x:13 + 2 * n_x], (), (), *refs[13 + 2 * n_x:])
            pl.when(pl.program_id(0) == 0)(begin)

        @pl.when(pl.program_id(0) == 0)
        def _():
            dw_ref[...] = jnp.zeros_like(dw_ref)
            dnp_ref[...] = jnp.zeros_like(dnp_ref)
            dkv_ref[...] = jnp.zeros_like(dkv_ref)

        q = q_ref[...]
        kvv = kv_ref[...]
        z = z_ref[...]
        ps, outs = _attend(q, kvv)
        cat = jnp.concatenate([y_ref[...]] + [a.astype(ACT) for a in outs], axis=1)
        sz = _silu(z)
        g = cat * sz
        ov = o_ref[...]
        dr = dxn_ref[...]
        rr = lax.rsqrt(jnp.mean(ov * ov, axis=-1, keepdims=True) + EPS)
        dnp_ref[...] += jnp.sum(dr * ov * rr, axis=0, keepdims=True)
        dow = dr * np_ref[...]
        do = rr * (dow - ov * (rr * rr) * jnp.mean(dow * ov, axis=-1, keepdims=True))
        dg = _mdot(do, w_ref[...], NT).astype(ACT)
        dw_ref[...] += _mdot(g, do, TN)
        dcat = dg * sz
        dzq_ref[:, 0:D_CAT] = dg * cat * _dsilu(z)
        dy_ref[...] = dcat[:, 0:D]
        heads = range(XA_HEADS)
        dohs = [dcat[:, D + h * HEAD:D + (h + 1) * HEAD] for h in heads]
        dps = [_mdot(dohs[h], kvv[:, D_XA + h * HEAD:D_XA + (h + 1) * HEAD], NT) for h in heads]
        dss = [ps[h] * (dps[h] - jnp.sum(dps[h] * ps[h], axis=-1, keepdims=True)) for h in heads]
        dqs = [_mdot(dss[h], kvv[:, h * HEAD:(h + 1) * HEAD], NN) * (HEAD ** -0.5) for h in heads]
        dks = [_mdot(dss[h], q[:, h * HEAD:(h + 1) * HEAD], TN) * (HEAD ** -0.5) for h in heads]
        dvs = [_mdot(ps[h], dohs[h], TN) for h in heads]
        for h in heads:
            lo = h * HEAD
            dzq_ref[:, D_CAT + lo:D_CAT + lo + HEAD] = dqs[h].astype(ACT)
            dkv_ref[:, lo:lo + HEAD] += dks[h]
            dkv_ref[:, D_XA + lo:D_XA + lo + HEAD] += dvs[h]
        if n_x:
            pl.when(pl.program_id(0) == n_steps - 1)(end)

    row = pl.BlockSpec((tm, D), lambda i: (i, 0))
    return _pcall(
        body, name=name, grid=(n_steps,),
        in_specs=[row, row, pl.BlockSpec((tm, D_CAT), lambda i: (i, 0)),
                  pl.BlockSpec((tm, D_XA), lambda i: (i, D_CAT // D_XA)), row,
                  _full((N_MEM, 2 * D_XA)), _full((D_CAT, D)), _full((1, D))] + [_HBM] * n_x,
        out_specs=[pl.BlockSpec((tm, ZQ), lambda i: (i, 0)), row, _full((D_CAT, D)), _full((1, D)),
                   _full((N_MEM, 2 * D_XA))] + [_HBM] * n_x,
        out_shape=[jax.ShapeDtypeStruct((t, ZQ), ACT), jax.ShapeDtypeStruct((t, D), ACT),
                   jax.ShapeDtypeStruct((D_CAT, D), F32), jax.ShapeDtypeStruct((1, D), F32),
                   jax.ShapeDtypeStruct((N_MEM, 2 * D_XA), F32)]
        + [jax.ShapeDtypeStruct(e.shape, e.dtype) for e in exchange],
        scratch_shapes=_sem_shapes(7 * n_x, n_x) if n_x else [], compiler_params=_cp(1),
    )(dxn, o, p, p, ymix, kv, wout, npost, *exchange)


def _gmlp_chunk(us, vs, lnws, lnbs, wss, bss):
    gv = [jax.nn.gelu(v) for v in vs]
    mean = sum(jnp.sum(v, axis=-1, keepdims=True) for v in gv) / D
    cen = [v - mean for v in gv]
    var = sum(jnp.sum(c * c, axis=-1, keepdims=True) for c in cen) / D
    rstd = lax.rsqrt(var + EPS)
    row = lax.broadcasted_iota(jnp.int32, (GM_CHUNK, GM_CHUNK), 0)
    col = lax.broadcasted_iota(jnp.int32, (GM_CHUNK, GM_CHUNK), 1)
    ys = []
    for g in range(GM_GROUPS):
        vn = cen[g] * rstd * lnws[g] + lnbs[g]
        sp = _nn(jnp.where(row >= col, wss[g], 0.0), vn) + bss[g]
        ys.append(jax.nn.gelu(us[g]) * sp)
    return ys


def _split_cols(v, n, width=HEAD):
    return [v[:, k * width:(k + 1) * width] for k in range(n)]


def _gmlp_operands(u_ref, v_ref, lnw_ref, lnb_ref, ws_ref, bst_ref, r0):
    us = _split_cols(u_ref[r0:r0 + GM_CHUNK, :].astype(F32), GM_GROUPS)
    vs = _split_cols(v_ref[r0:r0 + GM_CHUNK, :].astype(F32), GM_GROUPS)
    lnws = _split_cols(lnw_ref[...], GM_GROUPS)
    lnbs = _split_cols(lnb_ref[...], GM_GROUPS)
    wss = [ws_ref[g] for g in range(GM_GROUPS)]
    bst = bst_ref[...]
    bss = [jnp.sum(bst * _onehot_row(HEAD, g), axis=1, keepdims=True) for g in range(GM_GROUPS)]
    return us, vs, lnws, lnbs, wss, bss


def _gmlp_rows(t):
    return min(GM_CHUNKS_PER_STEP, t // GM_CHUNK) * GM_CHUNK


def _gmlp_specs(rows):
    return [pl.BlockSpec((rows, D), lambda i: (i, ZQ // D)), pl.BlockSpec((rows, D), lambda i: (i, ZQ // D + 1)),
            _full((1, D)), _full((1, D)), _full((GM_GROUPS, GM_CHUNK, GM_CHUNK)), _full((GM_CHUNK, HEAD))]


def _gmlp_fwd(p, lnw, lnb, ws, bst, name):
    t = p.shape[0]
    rows = _gmlp_rows(t)

    def body(u_ref, v_ref, lnw_ref, lnb_ref, ws_ref, bst_ref, y_ref):
        for r0 in range(0, rows, GM_CHUNK):
            ys = _gmlp_chunk(*_gmlp_operands(u_ref, v_ref, lnw_ref, lnb_ref, ws_ref, bst_ref, r0))
            for g in range(GM_GROUPS):
                y_ref[r0:r0 + GM_CHUNK, g * HEAD:(g + 1) * HEAD] = ys[g].astype(ACT)

    return _pcall(body, name=name, grid=(t // rows,), in_specs=_gmlp_specs(rows),
                  out_specs=pl.BlockSpec((rows, D), lambda i: (i, 0)),
                  out_shape=jax.ShapeDtypeStruct((t, D), ACT), compiler_params=_cp(1))(p, p, lnw, lnb, ws, bst)


def _gmlp_bwd(p, dy, lnw, lnb, ws, bst, name, exchange=(), bcast=()):
    t = p.shape[0]
    rows = _gmlp_rows(t)

    def body(u_ref, v_ref, lnw_ref, lnb_ref, ws_ref, bst_ref, dy_ref, duv_ref, dlnw_ref, dlnb_ref, dws_ref, dbst_ref):
        @pl.when(pl.program_id(0) == 0)
        def _():
            dlnw_ref[...] = jnp.zeros_like(dlnw_ref)
            dlnb_ref[...] = jnp.zeros_like(dlnb_ref)
            dws_ref[...] = jnp.zeros_like(dws_ref)
            dbst_ref[...] = jnp.zeros_like(dbst_ref)

        for r0 in range(0, rows, GM_CHUNK):
            ops = _gmlp_operands(u_ref, v_ref, lnw_ref, lnb_ref, ws_ref, bst_ref, r0)
            _, vjp = jax.vjp(_gmlp_chunk, *ops)
            dus, dvs, dlnws, dlnbs, dwss, dbss = vjp(_split_cols(dy_ref[r0:r0 + GM_CHUNK, :].astype(F32), GM_GROUPS))
            dbst = jnp.zeros((GM_CHUNK, HEAD), F32)
            for g in range(GM_GROUPS):
                lo = g * HEAD
                duv_ref[r0:r0 + GM_CHUNK, lo:lo + HEAD] = dus[g].astype(ACT)
                duv_ref[r0:r0 + GM_CHUNK, D + lo:D + lo + HEAD] = dvs[g].astype(ACT)
                dlnw_ref[:, lo:lo + HEAD] += dlnws[g]
                dlnb_ref[:, lo:lo + HEAD] += dlnbs[g]
                dws_ref[g] += dwss[g]
                dbst = dbst + dbss[g] * _onehot_row(HEAD, g)
            dbst_ref[...] += dbst

    call = dict(
        name=name, grid=(t // rows,),
        in_specs=_gmlp_specs(rows) + [pl.BlockSpec((rows, D), lambda i: (i, 0))],
        out_specs=[pl.BlockSpec((rows, 2 * D), lambda i: (i, 0)), _full((1, D)), _full((1, D)),
                   _full((GM_GROUPS, GM_CHUNK, GM_CHUNK)), _full((GM_CHUNK, HEAD))],
        out_shape=[jax.ShapeDtypeStruct((t, 2 * D), ACT), jax.ShapeDtypeStruct((1, D), F32),
                   jax.ShapeDtypeStruct((1, D), F32), jax.ShapeDtypeStruct((GM_GROUPS, GM_CHUNK, GM_CHUNK), F32),
                   jax.ShapeDtypeStruct((GM_CHUNK, HEAD), F32)],
        compiler_params=_cp(1))
    args = (p, p, lnw, lnb, ws, bst, dy)
    if not exchange and not bcast:
        return _pcall(body, **call)(*args), ()
    return _pcall_hosting(body, args, exchange, bcast, **call)


def _prev_halo(tm, col):
    return pl.BlockSpec((HALO, D), lambda i: (jnp.maximum(i * (tm // HALO) - 1, 0), col))


def _next_halo(tm, col, n_tiles):
    return pl.BlockSpec((HALO, D), lambda i: (jnp.minimum(i + 1, n_tiles - 1) * (tm // HALO), col))


def _taps_back(ext, w, width):
    acc = None
    for k in range(width):
        s = width - 1 - k
        term = w[k:k + 1, :] * (pltpu.roll(ext, s, 0) if s else ext)[HALO:, :]
        acc = term if acc is None else acc + term
    return acc


def _taps_fwd(ext, w, width, n):
    rows = ext.shape[0]
    acc = None
    for k in range(width):
        s = width - 1 - k
        term = w[k:k + 1, :] * (pltpu.roll(ext, rows - s, 0) if s else ext)[0:n, :]
        acc = term if acc is None else acc + term
    return acc


def _sconv_fwd(p, cw, name):
    t = p.shape[0]
    tm = min(256, t)

    def body(b_ref, c_ref, h_ref, cp_ref, hp_ref, w_ref, y_ref):
        first = pl.program_id(0) == 0
        prev = jnp.where(first, 0.0, cp_ref[...].astype(F32) * hp_ref[...].astype(F32))
        ext = jnp.concatenate([prev, c_ref[...].astype(F32) * h_ref[...].astype(F32)], axis=0)
        y_ref[...] = (b_ref[...].astype(F32) * _taps_back(ext, w_ref[...], 3)).astype(ACT)

    c0 = ZQ // D
    tile = [pl.BlockSpec((tm, D), lambda i, c=c: (i, c)) for c in (c0, c0 + 1, c0 + 2)]
    return _pcall(body, name=name, grid=(t // tm,),
                  in_specs=tile + [_prev_halo(tm, c0 + 1), _prev_halo(tm, c0 + 2), _full((HALO, D))],
                  out_specs=pl.BlockSpec((tm, D), lambda i: (i, 0)),
                  out_shape=jax.ShapeDtypeStruct((t, D), ACT), compiler_params=_cp(1))(p, p, p, p, p, cw)


def _sconv_bwd(p, dy, cw, name):
    t = p.shape[0]
    tm = min(256, t)
    n_tiles = t // tm

    def body(b_ref, c_ref, h_ref, cp_ref, hp_ref, bn_ref, dy_ref, dyn_ref, w_ref, d_ref, dw_ref):
        i = pl.program_id(0)

        @pl.when(i == 0)
        def _():
            dw_ref[...] = jnp.zeros_like(dw_ref)

        w = w_ref[...]
        bv, cv, hv = b_ref[...].astype(F32), c_ref[...].astype(F32), h_ref[...].astype(F32)
        dyv = dy_ref[...].astype(F32)
        prev = jnp.where(i == 0, 0.0, cp_ref[...].astype(F32) * hp_ref[...].astype(F32))
        ext = jnp.concatenate([prev, cv * hv], axis=0)
        conv = _taps_back(ext, w, 3)
        dconv = dyv * bv
        nxt = jnp.where(i == n_tiles - 1, 0.0, dyn_ref[...].astype(F32) * bn_ref[...].astype(F32))
        dc = _taps_fwd(jnp.concatenate([dconv, nxt], axis=0), w, 3, tm)
        d_ref[:, 0:D] = (dyv * conv).astype(ACT)
        d_ref[:, D:2 * D] = (dc * hv).astype(ACT)
        d_ref[:, 2 * D:3 * D] = (dc * cv).astype(ACT)
        for k in range(3):
            s = 2 - k
            shifted = (pltpu.roll(ext, s, 0) if s else ext)[HALO:, :]
            dw_ref[k:k + 1, :] += jnp.sum(dconv * shifted, axis=0, keepdims=True)

    c0 = ZQ // D
    tile = [pl.BlockSpec((tm, D), lambda i, c=c: (i, c)) for c in (c0, c0 + 1, c0 + 2)]
    return _pcall(
        body, name=name, grid=(n_tiles,),
        in_specs=tile + [_prev_halo(tm, c0 + 1), _prev_halo(tm, c0 + 2), _next_halo(tm, c0, n_tiles),
                         pl.BlockSpec((tm, D), lambda i: (i, 0)), _next_halo(tm, 0, n_tiles), _full((HALO, D))],
        out_specs=[pl.BlockSpec((tm, 3 * D), lambda i: (i, 0)), _full((HALO, D))],
        out_shape=[jax.ShapeDtypeStruct((t, 3 * D), ACT), jax.ShapeDtypeStruct((HALO, D), F32)],
        compiler_params=_cp(1),
    )(p, p, p, p, p, p, dy, dy, cw)


def _l2_heads(s, scale):
    outs, rs = [], []
    for hh in range(DN_HEADS):
        blk = s[:, hh * HEAD:(hh + 1) * HEAD]
        r = lax.rsqrt(jnp.sum(blk * blk, axis=-1, keepdims=True) + EPS)
        outs.append(blk * (r * scale))
        rs.append(r)
    return outs, rs


_QKV_SCALE = (HEAD ** -0.5, 1.0, None)


def _qkv_fwd(p, cw, name):
    t = p.shape[0]
    tm = min(256, t)

    def body(q_ref, k_ref, v_ref, qp_ref, kp_ref, vp_ref, w_ref, o_ref):
        first = pl.program_id(0) == 0
        for part, (ref, pref) in enumerate(((q_ref, qp_ref), (k_ref, kp_ref), (v_ref, vp_ref))):
            prev = jnp.where(first, 0.0, pref[...].astype(F32))
            ext = jnp.concatenate([prev, ref[...].astype(F32)], axis=0)
            s = _silu(_taps_back(ext, w_ref[:, part * D:(part + 1) * D], 4))
            if _QKV_SCALE[part] is None:
                o_ref[:, part * D:(part + 1) * D] = s.astype(ACT)
            else:
                outs, _ = _l2_heads(s, _QKV_SCALE[part])
                for hh in range(DN_HEADS):
                    o_ref[:, part * D + hh * HEAD:part * D + (hh + 1) * HEAD] = outs[hh].astype(ACT)

    c0 = ZQ // D
    tile = [pl.BlockSpec((tm, D), lambda i, c=c: (i, c)) for c in (c0, c0 + 1, c0 + 2)]
    halo = [_prev_halo(tm, c) for c in (c0, c0 + 1, c0 + 2)]
    return _pcall(body, name=name, grid=(t // tm,), in_specs=tile + halo + [_full((HALO, 3 * D))],
                  out_specs=pl.BlockSpec((tm, 3 * D), lambda i: (i, 0)),
                  out_shape=jax.ShapeDtypeStruct((t, 3 * D), ACT), compiler_params=_cp(1))(p, p, p, p, p, p, cw)


def _qkv_bwd(p, dqkv, cw, name):
    t = p.shape[0]
    tm = min(256, t)
    n_tiles = t // tm

    def body(*refs):
        tiles, prevs, nexts = refs[0:3], refs[3:6], refs[6:9]
        d_tiles, d_nexts = refs[9:12], refs[12:15]
        w_ref, o_ref, dw_ref = refs[15:]
        i = pl.program_id(0)

        @pl.when(i == 0)
        def _():
            dw_ref[...] = jnp.zeros_like(dw_ref)

        for part in range(3):
            w = w_ref[:, part * D:(part + 1) * D]
            prev = jnp.where(i == 0, 0.0, prevs[part][...].astype(F32))
            ext = jnp.concatenate([prev, tiles[part][...].astype(F32), nexts[part][...].astype(F32)], axis=0)
            xc = _taps_back(ext, w, 4)
            dout = jnp.concatenate([d_tiles[part][...].astype(F32), d_nexts[part][...].astype(F32)], axis=0)
            s = _silu(xc)
            if _QKV_SCALE[part] is None:
                ds = dout
            else:
                scale = _QKV_SCALE[part]
                pieces = []
                for hh in range(DN_HEADS):
                    blk = s[:, hh * HEAD:(hh + 1) * HEAD]
                    dblk = dout[:, hh * HEAD:(hh + 1) * HEAD]
                    r = lax.rsqrt(jnp.sum(blk * blk, axis=-1, keepdims=True) + EPS)
                    pieces.append(scale * r * (dblk - blk * (r * r) * jnp.sum(dblk * blk, axis=-1, keepdims=True)))
                ds = jnp.concatenate(pieces, axis=1)
            dxc = ds * _dsilu(xc)
            row = lax.broadcasted_iota(jnp.int32, (tm + HALO, 1), 0)
            dxc = jnp.where(jnp.logical_and(i == n_tiles - 1, row >= tm), 0.0, dxc)
            o_ref[:, part * D:(part + 1) * D] = _taps_fwd(dxc, w, 4, tm).astype(ACT)
            for k in range(4):
                sh = 3 - k
                shifted = (pltpu.roll(ext, sh, 0) if sh else ext)[HALO:HALO + tm, :]
                dw_ref[k:k + 1, part * D:(part + 1) * D] += jnp.sum(dxc[0:tm, :] * shifted, axis=0, keepdims=True)

    c0 = ZQ // D
    cols = (c0, c0 + 1, c0 + 2)
    tile = [pl.BlockSpec((tm, D), lambda i, c=c: (i, c)) for c in cols]
    dtile = [pl.BlockSpec((tm, D), lambda i, c=c: (i, c)) for c in range(3)]
    in_specs = (tile + [_prev_halo(tm, c) for c in cols] + [_next_halo(tm, c, n_tiles) for c in cols]
                + dtile + [_next_halo(tm, c, n_tiles) for c in range(3)] + [_full((HALO, 3 * D))])
    return _pcall(
        body, name=name, grid=(n_tiles,), in_specs=in_specs,
        out_specs=[pl.BlockSpec((tm, 3 * D), lambda i: (i, 0)), _full((HALO, 3 * D))],
        out_shape=[jax.ShapeDtypeStruct((t, 3 * D), ACT), jax.ShapeDtypeStruct((HALO, 3 * D), F32)],
        compiler_params=_cp(1),
    )(*([p] * 9), *([dqkv] * 6), cw)


def _tri_masks(n):
    row = lax.broadcasted_iota(jnp.int32, (n, n), 0)
    col = lax.broadcasted_iota(jnp.int32, (n, n), 1)
    return row, col


@jax.custom_vjp
def _unit_lower_inverses(mats):
    n = DN_CHUNK
    row, col = _tri_masks(n)
    eye = (row == col).astype(F32)
    same16 = (row // 16) == (col // 16)
    same32 = (row // 32) == (col // 32)
    pw = [jnp.where(same16, a, 0.0) for a in mats]
    x = [eye - p for p in pw]
    for _ in range(3):
        pw = [_hnn(p, p) for p in pw]
        x = [_hnn(xi, eye + p) for xi, p in zip(x, pw)]
    for keep in (jnp.logical_and(same32, jnp.logical_not(same16)), jnp.logical_not(same32)):
        inner = [_hnn(jnp.where(keep, a, 0.0), xi) for a, xi in zip(mats, x)]
        x = [xi - _hnn(xi, y) for xi, y in zip(x, inner)]
    return tuple(x)


def _uli_fwd(mats):
    t = _unit_lower_inverses(mats)
    return t, t


def _uli_bwd(ts, gs):
    inner = [_hnt(g, t) for g, t in zip(gs, ts)]
    return (tuple(-_htn(t, y) for t, y in zip(ts, inner)),)


_unit_lower_inverses.defvjp(_uli_fwd, _uli_bwd)


@jax.custom_vjp
def _known_inverses(mats, ts):
    return ts


_known_inverses.defvjp(lambda mats, ts: (ts, ts),
                       lambda ts, gs: (_uli_bwd(ts, gs)[0], tuple(jnp.zeros_like(t) for t in ts)))


def _pick_col(m, k):
    return jnp.sum(m * _onehot_row(m.shape[1], k), axis=1, keepdims=True)


def _pick_row(m, k):
    hot = (lax.broadcasted_iota(jnp.int32, (m.shape[0], 1), 0) == k).astype(F32)
    return jnp.sum(m * hot, axis=0, keepdims=True)


def _delta_chunk(states, qs, ks, vs, ab, alog, dtb, onw, known_inverses=None):
    n = DN_CHUNK
    heads = range(DN_HEADS)
    row, col = _tri_masks(n)
    incl = row >= col
    lane = lax.broadcasted_iota(jnp.int32, (1, HEAD), 1)
    g_all = jnp.where(lane < DN_HEADS, -jnp.exp(alog) * jax.nn.softplus(ab + dtb), 0.0)
    c_cols = _hnn(incl.astype(F32), g_all)
    c_rows = _htn(g_all, (row <= col).astype(F32))
    g_tot = jnp.sum(g_all, axis=0, keepdims=True)
    beta_all = jax.nn.sigmoid(ab)
    ccol = [_pick_col(c_cols, h) for h in heads]
    crow = [_pick_row(c_rows, h) for h in heads]
    gl = [_pick_col(g_tot, h) for h in heads]
    beta = [_pick_col(beta_all, DN_HEADS + h) for h in heads]
    decay = [jnp.exp(jnp.where(incl, ccol[h] - crow[h], -1e30)) for h in heads]
    eg = [jnp.exp(ccol[h]) for h in heads]
    kb = [ks[h] * beta[h] for h in heads]
    amat = [jnp.where(row > col, _nt(kb[h], ks[h]) * decay[h], 0.0) for h in heads]
    if known_inverses is None:
        tmat = _unit_lower_inverses(tuple(amat))
    else:
        tmat = _known_inverses(tuple(amat), tuple(known_inverses))
    u = [_nn(tmat[h], vs[h] * beta[h]) for h in heads]
    w = [_nn(tmat[h], kb[h] * eg[h]) for h in heads]
    qk = [_nt(qs[h], ks[h]) * decay[h] for h in heads]
    v_new = [u[h] - _nn(w[h], states[h]) for h in heads]
    o = [_nn(qs[h] * eg[h], states[h]) + _nn(qk[h], v_new[h]) for h in heads]
    new_states = [states[h] * jnp.exp(gl[h]) + _tn(ks[h] * jnp.exp(gl[h] - ccol[h]), v_new[h]) for h in heads]
    ys = [o[h] * lax.rsqrt(jnp.mean(o[h] * o[h], axis=-1, keepdims=True) + EPS) * onw for h in heads]
    return (ys, new_states), tmat


def _head_cols(ref):
    return [ref[:, h * HEAD:(h + 1) * HEAD].astype(F32) for h in range(DN_HEADS)]


def _delta_fwd(qkv, ab, alog, dtb, onw, name):
    t = qkv.shape[0]
    nc = t // DN_CHUNK

    def body(q_ref, k_ref, v_ref, ab_ref, alog_ref, dtb_ref, onw_ref, y_ref, keep_ref, inv_ref, state):
        @pl.when(pl.program_id(0) == 0)
        def _():
            state[...] = jnp.zeros_like(state)

        s0 = [state[hh] for hh in range(DN_HEADS)]
        (ys, s1), tmat = _delta_chunk(s0, _head_cols(q_ref), _head_cols(k_ref), _head_cols(v_ref), ab_ref[...],
                                      alog_ref[...], dtb_ref[...], onw_ref[...])
        for hh in range(DN_HEADS):
            keep_ref[0, hh] = s0[hh]
            inv_ref[0, hh] = tmat[hh]
            state[hh] = s1[hh]
            y_ref[:, hh * HEAD:(hh + 1) * HEAD] = ys[hh].astype(ACT)

    chunk = [pl.BlockSpec((DN_CHUNK, D), lambda i, c=c: (i, c)) for c in range(3)]
    return _pcall(
        body, name=name, grid=(nc,),
        in_specs=chunk + [pl.BlockSpec((DN_CHUNK, HEAD), lambda i: (i, 0)), _full((1, HEAD)), _full((1, HEAD)),
                          _full((1, HEAD))],
        out_specs=[pl.BlockSpec((DN_CHUNK, D), lambda i: (i, 0)),
                   pl.BlockSpec((1, DN_HEADS, HEAD, HEAD), lambda i: (i, 0, 0, 0)),
                   pl.BlockSpec((1, DN_HEADS, DN_CHUNK, DN_CHUNK), lambda i: (i, 0, 0, 0))],
        out_shape=[jax.ShapeDtypeStruct((t, D), ACT), jax.ShapeDtypeStruct((nc, DN_HEADS, HEAD, HEAD), F32),
                   jax.ShapeDtypeStruct((nc, DN_HEADS, DN_CHUNK, DN_CHUNK), F32)],
        scratch_shapes=[pltpu.VMEM((DN_HEADS, HEAD, HEAD), F32)], compiler_params=_cp(1),
    )(qkv, qkv, qkv, ab, alog, dtb, onw)


def _delta_bwd(qkv, ab, alog, dtb, onw, keep, inv, dy, name):
    t = qkv.shape[0]
    nc = t // DN_CHUNK

    def body(q_ref, k_ref, v_ref, ab_ref, alog_ref, dtb_ref, onw_ref, keep_ref, inv_ref, dy_ref,
             dqkv_ref, dab_ref, dalog_ref, ddtb_ref, donw_ref, dstate):
        @pl.when(pl.program_id(0) == 0)
        def _():
            dstate[...] = jnp.zeros_like(dstate)
            dalog_ref[...] = jnp.zeros_like(dalog_ref)
            ddtb_ref[...] = jnp.zeros_like(ddtb_ref)
            donw_ref[...] = jnp.zeros_like(donw_ref)

        s0 = [keep_ref[0, hh] for hh in range(DN_HEADS)]
        known = [inv_ref[0, hh] for hh in range(DN_HEADS)]
        _, vjp, _ = jax.vjp(functools.partial(_delta_chunk, known_inverses=known), s0, _head_cols(q_ref),
                            _head_cols(k_ref), _head_cols(v_ref), ab_ref[...], alog_ref[...], dtb_ref[...], onw_ref[...],
                            has_aux=True)
        ds, dq, dk, dv, dab, dal, ddt, don = vjp((_head_cols(dy_ref), [dstate[hh] for hh in range(DN_HEADS)]))
        for hh in range(DN_HEADS):
            lo = hh * HEAD
            dstate[hh] = ds[hh]
            dqkv_ref[:, lo:lo + HEAD] = dq[hh].astype(ACT)
            dqkv_ref[:, D + lo:D + lo + HEAD] = dk[hh].astype(ACT)
            dqkv_ref[:, 2 * D + lo:2 * D + lo + HEAD] = dv[hh].astype(ACT)
        dab_ref[...] = dab
        dalog_ref[...] += dal
        ddtb_ref[...] += ddt
        donw_ref[...] += don

    rev = lambda i: nc - 1 - i
    chunk = [pl.BlockSpec((DN_CHUNK, D), lambda i, c=c: (rev(i), c)) for c in range(3)]
    small = jax.ShapeDtypeStruct((1, HEAD), F32)
    return _pcall(
        body, name=name, grid=(nc,),
        in_specs=chunk + [pl.BlockSpec((DN_CHUNK, HEAD), lambda i: (rev(i), 0)), _full((1, HEAD)), _full((1, HEAD)),
                          _full((1, HEAD)), pl.BlockSpec((1, DN_HEADS, HEAD, HEAD), lambda i: (rev(i), 0, 0, 0)),
                          pl.BlockSpec((1, DN_HEADS, DN_CHUNK, DN_CHUNK), lambda i: (rev(i), 0, 0, 0)),
                          pl.BlockSpec((DN_CHUNK, D), lambda i: (rev(i), 0))],
        out_specs=[pl.BlockSpec((DN_CHUNK, 3 * D), lambda i: (rev(i), 0)),
                   pl.BlockSpec((DN_CHUNK, HEAD), lambda i: (rev(i), 0)), _full((1, HEAD)), _full((1, HEAD)),
                   _full((1, HEAD))],
        out_shape=[jax.ShapeDtypeStruct((t, 3 * D), ACT), jax.ShapeDtypeStruct((t, HEAD), F32), small, small, small],
        scratch_shapes=[pltpu.VMEM((DN_HEADS, HEAD, HEAD), F32)], compiler_params=_cp(1),
    )(qkv, qkv, qkv, ab, alog, dtb, onw, keep, inv, dy)


def _loss_head(y, target):
    t = y.shape[0]
    tm = min(512, t)

    def body(y_ref, t_ref, l_ref, dy_ref):
        @pl.when(pl.program_id(0) == 0)
        def _():
            l_ref[...] = jnp.zeros_like(l_ref)

        diff = y_ref[...] - t_ref[...]
        dy_ref[...] = diff * (1.0 / D)
        l_ref[...] += 0.5 * jnp.sum(jnp.sum(diff * diff, axis=-1, keepdims=True) * (1.0 / D), axis=0, keepdims=True)

    row = pl.BlockSpec((tm, D), lambda i: (i, 0))
    return _pcall(body, name="loss_head", grid=(t // tm,), in_specs=[row, row], out_specs=[_full((8, 128)), row],
                  out_shape=[jax.ShapeDtypeStruct((8, 128), F32), jax.ShapeDtypeStruct((t, D), F32)],
                  compiler_params=_cp(1))(y, target)


_SMALL_SHARDED = (("a_ln_w", (2, 128), 1), ("a_ln_b", (2, 128), 1), ("b_conv_w", (1, 3, 128), 2),
                  ("c_conv_w", (1, 4, 384), 2))
_REPLICATED = (("mem_norm_w", (1024,)), ("norm_pre", (4, 1024)), ("norm_post", (4, 1024)),
               ("a_b_s", (2, 8, 128)), ("c_a_log", (1, 8)), ("c_dt_bias", (1, 8)), ("c_o_norm_w", (1, 128)))


def _layer_shards(given, prefix):
    w_out = given[prefix + "w_out"]
    w_ins = [given[prefix + "a_w_in"][0], given[prefix + "b_w_in"], given[prefix + "c_w_in"], given[prefix + "a_w_in"][1]]
    return [[w_out[i], w_ins[i]] for i in range(4)]


def _as_matmul_operand(w):
    return w.reshape(w.shape[-2:]).astype(MXU)


def _as_rows(shape):
    return (math.prod(shape[:-1]) if len(shape) > 1 else 1, shape[-1])


def _row_table(shapes):
    table, at = [], 0
    for shape in shapes:
        rows, cols = _as_rows(shape)
        table.append((at, rows, cols))
        at += rows
    return table


def _pack_rows(arrays, shapes, rows, lead=()):
    n_lead = len(lead)
    parts = [jnp.pad(a.reshape(lead + _as_rows(s)), [(0, 0)] * (n_lead + 1) + [(0, D - s[-1])])
             for a, s in zip(arrays, shapes)]
    block = jnp.concatenate(parts, axis=n_lead)
    return jnp.pad(block, [(0, 0)] * n_lead + [(0, rows - block.shape[n_lead]), (0, 0)])


def _unpack_rows(packed, shapes, lead=()):
    n_lead = len(lead)
    return [packed[(slice(None),) * n_lead + (slice(r0, r0 + nr), slice(0, nc))].reshape(lead + s)
            for (r0, nr, nc), s in zip(_row_table(shapes), shapes)]


def _join_shards(blocks, axis):
    moved = jnp.moveaxis(blocks, 0, axis)
    shape = moved.shape
    return moved.reshape(shape[:axis] + (shape[axis] * shape[axis + 1],) + shape[axis + 2:])


def _split_shards(full, axis):
    shape = full.shape
    split = full.reshape(shape[:axis] + (N_DEV, shape[axis] // N_DEV) + shape[axis + 1:])
    return jnp.moveaxis(split, axis, 0)


_A_COLS = ((2560, 1536), (2048, 512), (0, 2048))
_BC_COLS = ((3584, 1536), (3072, 512), (0, 3072))
_C_COLS = ((3600, 1536), (3088, 512), (0, 3072))


def _reorder_cols(w, cols):
    return jnp.concatenate([w[:, s:s + n] for s, n in cols], axis=1)


def _restore_cols(pieces_in_my_order, cols, extra=()):
    placed = sorted(list(zip([s for s, _ in cols], pieces_in_my_order)) + list(extra), key=lambda sp: sp[0])
    return jnp.concatenate([piece for _, piece in placed], axis=1)


def kernel(x, mem, mem_norm_w, w_mem_kv, norm_pre, norm_post, w_out, a_w_in, a_ln_w, a_ln_b, a_w_s, a_b_s, b_w_in, b_conv_w, c_w_in, c_conv_w, c_a_log, c_dt_bias, c_o_norm_w, loss_target, m_mem_norm_w, m_w_mem_kv, m_norm_pre, m_norm_post, m_w_out, m_a_w_in, m_a_ln_w, m_a_ln_b, m_a_w_s, m_a_b_s, m_b_w_in, m_b_conv_w, m_c_w_in, m_c_conv_w, m_c_a_log, m_c_dt_bias, m_c_o_norm_w, v_mem_norm_w, v_w_mem_kv, v_norm_pre, v_norm_post, v_w_out, v_a_w_in, v_a_ln_w, v_a_ln_b, v_a_w_s, v_a_b_s, v_b_w_in, v_b_conv_w, v_c_w_in, v_c_conv_w, v_c_a_log, v_c_dt_bias, v_c_o_norm_w):
    given = dict(locals())
    x0 = x[0]
    mem0 = mem[0]
    target = loss_target[0]

    w_sh, m_sh, v_sh = (_layer_shards(given, pre) for pre in ("", "m_", "v_"))
    small_names = [n for n, _, _ in _SMALL_SHARDED]
    small_shapes = [s for _, s, _ in _SMALL_SHARDED]
    repl_names = [n for n, _ in _REPLICATED]
    repl_shapes = [s for _, s in _REPLICATED]
    small = _pack_rows([given[n] for n in small_names], small_shapes, R_SMALL)
    g_wkv, g_wout0, g_in0, g_small = _all_gather(
        [w_mem_kv.astype(MXU)] + [_as_matmul_operand(w) for w in w_sh[0]] + [small], "gather_weights")
    small_full = _unpack_rows(g_small, small_shapes, lead=(N_DEV,))
    full = {n: _join_shards(blocks, ax) for (n, _, ax), blocks in zip(_SMALL_SHARDED, small_full)}
    wkv = g_wkv.reshape(D, 2 * D_XA)
    wouts = [g_wout0.reshape(D_CAT, D)]
    w_in = [_reorder_cols(_join_shards(g_in0, 1), _A_COLS)]
    w_cab = None
    lnw = [full["a_ln_w"][j][None, :] for j in range(2)]
    lnb = [full["a_ln_b"][j][None, :] for j in range(2)]
    ws = [a_w_s[j] for j in range(2)]
    bst = [jnp.pad(a_b_s[j].T, ((0, 0), (0, HEAD - GM_GROUPS))) for j in range(2)]
    cw_b = jnp.pad(full["b_conv_w"][0], ((0, HALO - 3), (0, 0)))
    cw_c = jnp.pad(full["c_conv_w"][0], ((0, HALO - 4), (0, 0)))
    alog = jnp.pad(c_a_log, ((0, 0), (0, HEAD - DN_HEADS)))
    dtb = jnp.pad(c_dt_bias, ((0, 0), (0, HEAD - DN_HEADS)))
    onw = c_o_norm_w
    mw = mem_norm_w[None, :]

    kv = _memkv_fwd(mem0, mw, wkv).astype(MXU)
    xs, saved = [x0], []
    for i in range(4):
        kind = i % 3
        npre, npost = norm_pre[i][None, :], norm_post[i][None, :]
        ahead = [_as_matmul_operand(w) for w in w_sh[i + 1]] if i < 3 else []
        res = _proj_fwd(xs[i], npre, w_in[i], w_cab if kind == 2 else None, f"proj_fwd_{i}", gather=ahead)
        if ahead:
            g_wout, g_in = res[-2:]
            wouts.append(g_wout.reshape(D_CAT, D))
            if i + 1 == 2:
                c_full = jnp.concatenate([g_in[d] for d in range(N_DEV)], axis=1)
                w_in.append(_reorder_cols(c_full, _C_COLS))
                w_cab = jnp.pad(c_full[:, 3072:3088], ((0, 0), (0, HEAD - 16)))
            else:
                w_in.append(_reorder_cols(_join_shards(g_in, 1), _BC_COLS if i + 1 == 1 else _A_COLS))
        if kind == 2:
            p, h, ab = res[:3]
            qkv = _qkv_fwd(p, cw_c, f"qkv_fwd_{i}")
            ymix, keep, inv = _delta_fwd(qkv, ab, alog, dtb, onw, f"delta_fwd_{i}")
            extra = (qkv, ab, keep, inv)
        else:
            p, h = res[:2]
            if kind == 0:
                ymix = _gmlp_fwd(p, lnw[i // 3], lnb[i // 3], ws[i // 3], bst[i // 3], f"gmlp_fwd_{i}")
            else:
                ymix = _sconv_fwd(p, cw_b, f"sconv_fwd_{i}")
            extra = ()
        xn, o = _tail_fwd(p, ymix, xs[i], kv, wouts[i], npost, f"tail_fwd_{i}")
        xs.append(xn)
        saved.append((p, h, ymix, o, extra))

    loss_tile, dx = _loss_head(xs[4], target)
    loss = lax.psum(loss_tile[0, 0], ("x", "y", "c"))

    g = {}
    d_npre, d_npost = [None] * 4, [None] * 4
    d_ws, d_bs, d_lnw, d_lnb = [None] * 2, [None] * 2, [None] * 2, [None] * 2
    dkv = None
    pend_win, pend_wout = None, None
    landed_win, landed_wout = [None] * 4, [None] * 4
    for i in reversed(range(4)):
        kind = i % 3
        p, h, ymix, o, extra = saved[i]
        npre, npost = norm_pre[i][None, :], norm_post[i][None, :]
        res = _tail_bwd(dx, o, p, ymix, kv, wouts[i], npost, f"tail_bwd_{i}", exchange=[pend_win] if i < 3 else [])
        dzq, dymix, d_wout, d_npost[i], dkv_i = res[:5]
        if i < 3:
            landed_win[i + 1] = res[5]
        d_wout = d_wout.reshape(N_DEV, D_CAT // N_DEV, D).astype(XCH)
        dkv = dkv_i if dkv is None else dkv + dkv_i
        w_zq, w_mix = w_in[i][:, :ZQ], w_in[i][:, ZQ:]
        dw_zq = _matmul_tn(h, dzq, f"dw_zq_{i}")
        if kind == 0:
            j = i // 3
            early_x, early_b = [], []
            if i == 0:
                d_wkv, d_mw = _memkv_bwd(mem0, mw, wkv, dkv)
                early_x = [d_wkv.reshape(N_DEV, 128, D).astype(XCH), d_wout, pend_wout]
                early_b = [d_ws[1].reshape(GM_CHUNK, D).astype(XCH)]
            (dmix, d_lnw[j], d_lnb[j], d_ws[j], dbst), early_landed = _gmlp_bwd(
                p, dymix, lnw[j], lnb[j], ws[j], bst[j], f"gmlp_bwd_{i}", exchange=early_x, bcast=early_b)
            d_bs[j] = dbst[:, :GM_GROUPS].T
            dw_mix = _matmul_tn(h, dmix, f"dw_mix_{i}")
            d_win = _restore_cols([dw_zq[:, :D_CAT], dw_zq[:, D_CAT:], dw_mix], _A_COLS)
            dps, wparts = [dzq, dmix], [w_zq, w_mix]
        elif kind == 1:
            dmix, dcw = _sconv_bwd(p, dymix, cw_b, f"sconv_bwd_{i}")
            g["b_conv_w"] = dcw[None, :3]
            dw_mix = _matmul_tn(h, dmix, f"dw_mix_{i}")
            d_win = _restore_cols([dw_zq[:, :D_CAT], dw_zq[:, D_CAT:], dw_mix], _BC_COLS)
            dps, wparts = [dzq, dmix], [w_zq, w_mix]
        else:
            qkv, ab, keep, inv = extra
            dqkv, dab, dalog, ddtb, donw = _delta_bwd(qkv, ab, alog, dtb, onw, keep, inv, dymix, f"delta_bwd_{i}")
            dmix, dcw = _qkv_bwd(p, dqkv, cw_c, f"qkv_bwd_{i}")
            g["c_conv_w"] = dcw[None, :4]
            g["c_a_log"], g["c_dt_bias"], g["c_o_norm_w"] = dalog[:, :DN_HEADS], ddtb[:, :DN_HEADS], donw
            dw_mix = _matmul_tn(h, dmix, f"dw_mix_{i}")
            dw_ab = _matmul_tn(h, dab, f"dw_ab_{i}")
            d_win = _restore_cols([dw_zq[:, :D_CAT], dw_zq[:, D_CAT:], dw_mix], _C_COLS, extra=[(3072, dw_ab[:, :16])])
            dps, wparts = [dzq, dmix, dab], [w_zq, w_mix, w_cab]
        width = d_win.shape[1] // N_DEV
        pend_win = jnp.stack([d_win[:, d * width:(d + 1) * width] for d in range(N_DEV)]).astype(XCH)
        if i > 0:
            res = _proj_bwd_x(dps, wparts, xs[i], dx, npre, f"proj_bwd_x_{i}", exchange=[pend_wout] if i < 3 else [])
            dx, d_npre[i] = res[:2]
            if i < 3:
                landed_wout[i + 1] = res[2]
            pend_wout = d_wout

    g["mem_norm_w"] = d_mw[0]
    g["norm_pre"] = jnp.concatenate([jnp.zeros((1, D), F32)] + d_npre[1:], axis=0)
    g["norm_post"] = jnp.concatenate(d_npost, axis=0)
    g["a_ln_w"] = jnp.concatenate(d_lnw, axis=0)
    g["a_ln_b"] = jnp.concatenate(d_lnb, axis=0)
    g["a_b_s"] = jnp.stack(d_bs)
    e_small = _pack_rows([_split_shards(g[n], ax) for n, _, ax in _SMALL_SHARDED], small_shapes, R_SMALL, lead=(N_DEV,))
    r_pack = _pack_rows([g[n] for n in repl_names], repl_shapes, R_REPL)
    dx, d_npre0, l_in, l_small, ws0_all, r_all = _proj_bwd_x(
        dps, wparts, xs[0], dx, norm_pre[0][None, :], "proj_bwd_x_0", exchange=[pend_win, e_small],
        bcast=[d_ws[0].reshape(GM_CHUNK, D).astype(XCH), r_pack])
    l_wkv, landed_wout[0], landed_wout[1], ws3_all = early_landed
    landed_win[0] = l_in
    npre0_all, = _all_gather([jnp.pad(d_npre0, ((0, HALO - 1), (0, 0)))], "gather_norm_pre0")
    r_all = r_all.at[:, 1, :].set(npre0_all[:, 0, :])

    res = [[_reduce_adamw(parts, w_sh[i][a], m_sh[i][a], v_sh[i][a], f"adamw_{i}_{a}")
            for a, parts in enumerate((landed_wout[i], landed_win[i]))] for i in range(4)]
    res_wkv = _reduce_adamw(l_wkv, w_mem_kv, m_w_mem_kv, v_w_mem_kv, "adamw_w_mem_kv")
    res_small = _reduce_adamw_vectors(
        l_small, small, *(_pack_rows([given[pre + n] for n in small_names], small_shapes, R_SMALL) for pre in ("m_", "v_")),
        small_shapes, "adamw_small")
    res_ws = [_reduce_adamw(parts, *(given[pre + "a_w_s"][j].reshape(GM_CHUNK, D) for pre in ("", "m_", "v_")),
                            f"adamw_a_w_s_{j}") for j, parts in enumerate((ws0_all, ws3_all))]
    res_repl = _reduce_adamw_vectors(
        r_all, *(_pack_rows([given[pre + n] for n in repl_names], repl_shapes, R_REPL) for pre in ("", "m_", "v_")),
        repl_shapes, "adamw_replicated")

    order = ["mem_norm_w", "w_mem_kv", "norm_pre", "norm_post", "w_out", "a_w_in", "a_ln_w", "a_ln_b", "a_w_s", "a_b_s",
             "b_w_in", "b_conv_w", "c_w_in", "c_conv_w", "c_a_log", "c_dt_bias", "c_o_norm_w"]
    outs = [loss, dx[None]]
    for kind in range(4):
        got = dict(zip(repl_names, res_repl[kind]))
        got["a_w_s"] = jnp.stack([r[kind] for r in res_ws]).reshape(a_w_s.shape)
        got.update(zip(small_names, res_small[kind]))
        got["w_mem_kv"] = res_wkv[kind]
        got["w_out"] = jnp.stack([res[i][0][kind] for i in range(4)])
        got["a_w_in"] = jnp.stack([res[0][1][kind], res[3][1][kind]])
        got["b_w_in"] = res[1][1][kind]
        got["c_w_in"] = res[2][1][kind]
        outs += [got[n] for n in order]
    return tuple(outs)
```

```python
import functools
import math

import jax
import jax.numpy as jnp
from jax import lax
from jax.experimental import pallas as pl
from jax.experimental.pallas import tpu as pltpu

F32 = jnp.float32
MXU = jnp.bfloat16
ACT = jnp.bfloat16

D = 1024
D_XA = 512
D_CAT = 1536
N_MEM = 256
XA_HEADS = 4
HEAD = 128
ZQ = D_CAT + D_XA
EPS = 1e-6
GM_CHUNK = 128
GM_CHUNKS_PER_STEP = 4
GM_GROUPS = 8
DN_HEADS = 8
DN_CHUNK = 64
DN_CHUNKS_PER_STEP = 4
N_DEV = 8
HALO = 8
VMEM_LIMIT = 56 * 1024 * 1024
XCH = jnp.bfloat16
R_REPL = 32
R_SMALL = 16

ADAM_LR = 0.001
ADAM_B1 = 0.9
ADAM_B2 = 0.999
ADAM_EPS = 1e-08
ADAM_WD = 0.01
ADAM_STEP = 10

NN = ((1,), (0,))
NT = ((1,), (1,))
TN = ((0,), (0,))
MESH = pl.DeviceIdType.MESH


def _pcall(body, **kw):
    return pl.pallas_call(body, **kw)


def _cp(n_axes):
    return pltpu.CompilerParams(dimension_semantics=("arbitrary",) * n_axes, vmem_limit_bytes=VMEM_LIMIT)


def _dot(a, b, dims, prec=None):
    return lax.dot_general(a, b, (dims, ((), ())), preferred_element_type=F32, precision=prec)


def _mdot(a, b, dims):
    return _dot(a.astype(MXU), b.astype(MXU), dims)


def _make_mms(raw):
    @jax.custom_vjp
    def nn(a, b):
        return raw(a, b, NN)

    @jax.custom_vjp
    def nt(a, b):
        return raw(a, b, NT)

    @jax.custom_vjp
    def tn(a, b):
        return raw(a, b, TN)

    nn.defvjp(lambda a, b: (nn(a, b), (a, b)), lambda r, g: (nt(g, r[1]), tn(r[0], g)))
    nt.defvjp(lambda a, b: (nt(a, b), (a, b)), lambda r, g: (nn(g, r[1]), tn(g, r[0])))
    tn.defvjp(lambda a, b: (tn(a, b), (a, b)), lambda r, g: (nt(r[1], g), nn(r[0], g)))
    return nn, nt, tn


_nn, _nt, _tn = _make_mms(_mdot)


def _split_dot(a, b, dims):
    ah = a.astype(jnp.bfloat16)
    bh = b.astype(jnp.bfloat16)
    al = (a - ah.astype(F32)).astype(jnp.bfloat16)
    bl = (b - bh.astype(F32)).astype(jnp.bfloat16)
    return _dot(ah, bh, dims) + (_dot(ah, bl, dims) + _dot(al, bh, dims))


_hnn, _hnt, _htn = _make_mms(_split_dot)


def _full(shape):
    return pl.BlockSpec(shape, lambda *_: (0,) * len(shape))


def _silu(z):
    return z * jax.nn.sigmoid(z)


def _dsilu(z):
    s = jax.nn.sigmoid(z)
    return s * (1.0 + z * (1.0 - s))


def _onehot_row(n, k):
    return (lax.broadcasted_iota(jnp.int32, (1, n), 1) == k).astype(F32)


_HBM = pl.BlockSpec(memory_space=pl.ANY)


def _sem_shapes(n_remote, n_local):
    return [pltpu.SemaphoreType.DMA((n_remote,)), pltpu.SemaphoreType.DMA((n_remote,)),
            pltpu.SemaphoreType.DMA((n_local,))]


def _gather_parts(x_refs, out_refs, send_sems, recv_sems, local_sems):
    n = len(x_refs)
    x, y, cc = lax.axis_index("x"), lax.axis_index("y"), lax.axis_index("c")
    me, sibling = (x, y, cc), (x, y, 1 - cc)
    chips = [(1 - x, y), (x, 1 - y), (1 - x, 1 - y)]

    def slot(a, px, py, pc):
        return out_refs[a].at[4 * px + 2 * py + pc]

    def copy(k, a, block, to, src=None):
        return pltpu.make_async_remote_copy(
            src_ref=slot(a, *block) if src is None else src, dst_ref=slot(a, *block),
            send_sem=send_sems.at[k * n + a], recv_sem=recv_sems.at[k * n + a], device_id=to, device_id_type=MESH)

    mine = [pltpu.make_async_copy(x_refs[a], slot(a, *me), local_sems.at[a]) for a in range(n)]
    first = [copy(0, a, me, sibling, src=x_refs[a]) for a in range(n)]
    first += [copy(1 + j, a, me, (*chip, cc), src=x_refs[a]) for j, chip in enumerate(chips) for a in range(n)]

    def begin():
        for cp in mine + first:
            cp.start()

    def end():
        passed = []
        for j, chip in enumerate(chips):
            for a in range(n):
                copy(1 + j, a, (*chip, cc), me).wait_recv()
                passed.append(copy(4 + j, a, (*chip, cc), sibling))
                passed[-1].start()
        for a in range(n):
            copy(0, a, sibling, me).wait_recv()
        for j, chip in enumerate(chips):
            for a in range(n):
                copy(4 + j, a, (*chip, 1 - cc), me).wait_recv()
        for cp in first + passed:
            cp.wait_send()
        for cp in mine:
            cp.wait()

    return begin, end


def _exchange_parts(g_refs, out_refs, b_refs, ball_refs, send_sems, recv_sems, local_sems):
    n, nb = len(g_refs), len(b_refs)
    per_peer = n + nb
    x, y, cc = lax.axis_index("x"), lax.axis_index("y"), lax.axis_index("c")
    my_idx = 4 * x + 2 * y + cc
    mine = [pltpu.make_async_copy(g_refs[a].at[my_idx], out_refs[a].at[my_idx], local_sems.at[a]) for a in range(n)]
    mine += [pltpu.make_async_copy(b_refs[a], ball_refs[a].at[my_idx], local_sems.at[n + a]) for a in range(nb)]
    copies = []
    for k in range(1, N_DEV):
        px = 1 - x if k & 4 else x
        py = 1 - y if k & 2 else y
        pc = 1 - cc if k & 1 else cc
        base = (k - 1) * per_peer
        for a in range(n):
            copies.append(pltpu.make_async_remote_copy(
                src_ref=g_refs[a].at[4 * px + 2 * py + pc], dst_ref=out_refs[a].at[my_idx],
                send_sem=send_sems.at[base + a], recv_sem=recv_sems.at[base + a],
                device_id=(px, py, pc), device_id_type=MESH))
        for a in range(nb):
            copies.append(pltpu.make_async_remote_copy(
                src_ref=b_refs[a], dst_ref=ball_refs[a].at[my_idx], send_sem=send_sems.at[base + n + a],
                recv_sem=recv_sems.at[base + n + a], device_id=(px, py, pc), device_id_type=MESH))

    def begin():
        for cp in mine + copies:
            cp.start()

    def end():
        for cp in copies:
            cp.wait_recv()
        for cp in copies:
            cp.wait_send()
        for cp in mine:
            cp.wait()

    return begin, end


def _pcall_hosting(body, args, exchange, bcast, *, name, grid, in_specs, out_specs, out_shape, compiler_params):
    n_in, n_out, n_x, n_b = len(in_specs), len(out_specs), len(exchange), len(bcast)
    n_c, n_steps = n_x + n_b, grid[0]

    def hosted(*refs):
        ins, c_in = refs[:n_in], refs[n_in:n_in + n_c]
        outs, c_out = refs[n_in + n_c:n_in + n_c + n_out], refs[n_in + n_c + n_out:n_in + 2 * n_c + n_out]
        begin, end = _exchange_parts(c_in[:n_x], c_out[:n_x], c_in[n_x:], c_out[n_x:], *refs[n_in + 2 * n_c + n_out:])
        pl.when(pl.program_id(0) == 0)(begin)
        body(*ins, *outs)
        pl.when(pl.program_id(0) == n_steps - 1)(end)

    landed_shape = [jax.ShapeDtypeStruct(e.shape, e.dtype) for e in exchange]
    landed_shape += [jax.ShapeDtypeStruct((N_DEV,) + b.shape, b.dtype) for b in bcast]
    res = _pcall(hosted, name=name, grid=grid, in_specs=list(in_specs) + [_HBM] * n_c,
                 out_specs=list(out_specs) + [_HBM] * n_c, out_shape=list(out_shape) + landed_shape,
                 scratch_shapes=_sem_shapes(7 * n_c, n_c), compiler_params=compiler_params)(*args, *exchange, *bcast)
    return res[:n_out], res[n_out:]


def _all_gather(blks, name):
    n = len(blks)

    def body(*refs):
        begin, end = _gather_parts(refs[:n], refs[n:2 * n], *refs[2 * n:])
        begin()
        end()

    return _pcall(body, name=name, out_shape=[jax.ShapeDtypeStruct((N_DEV,) + b.shape, b.dtype) for b in blks],
                  in_specs=[_HBM] * n, out_specs=[_HBM] * n, scratch_shapes=_sem_shapes(7 * n, n))(*blks)


def _sum_and_adamw(p_ref, w, m, v):
    g = p_ref[0].astype(F32)
    for s in range(1, N_DEV):
        g = g + p_ref[s].astype(F32)
    nm = ADAM_B1 * m + (1.0 - ADAM_B1) * g
    nv = ADAM_B2 * v + (1.0 - ADAM_B2) * (g * g)
    m_hat = nm / (1.0 - ADAM_B1 ** ADAM_STEP)
    v_hat = nv / (1.0 - ADAM_B2 ** ADAM_STEP)
    return g, -ADAM_LR * (m_hat / (jnp.sqrt(v_hat) + ADAM_EPS) + ADAM_WD * w), nm, nv


def _reduce_adamw_vectors(parts, w, m, v, shapes, name):
    table = _row_table(shapes)

    def body(p_ref, w_ref, m_ref, v_ref, *out_refs):
        results = _sum_and_adamw(p_ref, w_ref[...], m_ref[...], v_ref[...])
        for kind, val in enumerate(results):
            for t, (r0, nr, nc) in enumerate(table):
                out_refs[kind * len(table) + t][...] = val[r0:r0 + nr, 0:nc]

    outs = _pcall(body, name=name, out_shape=[jax.ShapeDtypeStruct((nr, nc), F32) for _, nr, nc in table] * 4)(parts, w, m, v)
    return [[outs[kind * len(table) + t].reshape(s) for t, s in enumerate(shapes)] for kind in range(4)]


def _reduce_adamw(parts, w, m, v, name):
    lead = w.shape[:-2]
    r, c = w.shape[-2:]
    tr = 128 if r % 128 == 0 else r
    zeros = (0,) * len(lead)
    at = zeros + (slice(None), slice(None))

    def body(p_ref, w_ref, m_ref, v_ref, g_ref, d_ref, nm_ref, nv_ref):
        g_ref[at], d_ref[at], nm_ref[at], nv_ref[at] = _sum_and_adamw(p_ref, w_ref[at], m_ref[at], v_ref[at])

    row = pl.BlockSpec((1,) * len(lead) + (tr, c), lambda i: zeros + (i, 0))
    out = jax.ShapeDtypeStruct(w.shape, F32)
    return _pcall(
        body, name=name, grid=(r // tr,),
        in_specs=[pl.BlockSpec((N_DEV, tr, c), lambda i: (0, i, 0)), row, row, row],
        out_specs=[row, row, row, row], out_shape=[out, out, out, out], compiler_params=_cp(1),
    )(parts, w, m, v)


def _proj_fwd(x, nw, w, wab, name, gather=()):
    t, npj = x.shape[0], w.shape[1]
    tm, tn = min(512, t), 1024
    has_ab = wab is not None
    n_in, n_out, n_g = 3 + has_ab, 2 + has_ab, len(gather)
    n_i = t // tm

    def body(*refs):
        ins, g_ins = refs[:n_in], refs[n_in:n_in + n_g]
        outs = refs[n_in + n_g:n_in + n_g + n_out]
        g_outs = refs[n_in + n_g + n_out:n_in + 2 * n_g + n_out]
        if has_ab:
            (x_ref, nw_ref, w_ref, wab_ref), (p_ref, h_ref, ab_ref) = ins, outs
        else:
            (x_ref, nw_ref, w_ref), (p_ref, h_ref) = ins, outs
        if n_g:
            begin, end = _gather_parts(g_ins, g_outs, *refs[n_in + 2 * n_g + n_out:])
            pl.when(pl.program_id(0) == 0)(begin)

        xv = x_ref[...]
        hv = (xv * lax.rsqrt(jnp.mean(xv * xv, axis=-1, keepdims=True) + EPS) * nw_ref[...]).astype(MXU)
        h_ref[...] = hv.astype(ACT)
        if has_ab:
            ab_ref[...] = _dot(hv, wab_ref[...], NN)
        for j in range(npj // tn):
            p_ref[:, j * tn:(j + 1) * tn] = _dot(hv, w_ref[:, j * tn:(j + 1) * tn], NN).astype(ACT)
        if n_g:
            pl.when(pl.program_id(0) == n_i - 1)(end)

    in_specs = [pl.BlockSpec((tm, D), lambda i: (i, 0)), _full((1, D)), _full((D, npj))]
    out_specs = [pl.BlockSpec((tm, npj), lambda i: (i, 0)), pl.BlockSpec((tm, D), lambda i: (i, 0))]
    out_shape = [jax.ShapeDtypeStruct((t, npj), ACT), jax.ShapeDtypeStruct((t, D), ACT)]
    args = [x, nw, w]
    if has_ab:
        in_specs.append(_full((D, HEAD)))
        out_specs.append(pl.BlockSpec((tm, HEAD), lambda i: (i, 0)))
        out_shape.append(jax.ShapeDtypeStruct((t, HEAD), F32))
        args.append(wab)
    scratch = []
    if n_g:
        in_specs += [_HBM] * n_g
        out_specs += [_HBM] * n_g
        out_shape += [jax.ShapeDtypeStruct((N_DEV,) + b.shape, b.dtype) for b in gather]
        args += list(gather)
        scratch += _sem_shapes(7 * n_g, n_g)
    return _pcall(body, name=name, grid=(n_i,), in_specs=in_specs, out_specs=out_specs,
                  out_shape=out_shape, scratch_shapes=scratch, compiler_params=_cp(1))(*args)


def _proj_bwd_x(dps, ws, x, dxn, nw, name, exchange=(), bcast=()):
    t = x.shape[0]
    tm = min(256, t)
    n, n_x, n_b = len(dps), len(exchange), len(bcast)
    n_c, n_steps = n_x + n_b, t // tm

    def body(*refs):
        dp_refs, w_refs = refs[:n], refs[n:2 * n]
        x_ref, dxn_ref, nw_ref = refs[2 * n:2 * n + 3]
        c_in = refs[2 * n + 3:2 * n + 3 + n_c]
        dx_ref, dnw_ref = refs[2 * n + 3 + n_c:2 * n + 5 + n_c]
        c_out = refs[2 * n + 5 + n_c:2 * n + 5 + 2 * n_c]
        if n_c:
            begin, end = _exchange_parts(c_in[:n_x], c_out[:n_x], c_in[n_x:], c_out[n_x:], *refs[2 * n + 5 + 2 * n_c:])
            pl.when(pl.program_id(0) == 0)(begin)
        dh = _mdot(dp_refs[0][...], w_refs[0][...], NT)
        for k in range(1, n):
            dh = dh + _mdot(dp_refs[k][...], w_refs[k][...], NT)
        xv = x_ref[...]
        r = lax.rsqrt(jnp.mean(xv * xv, axis=-1, keepdims=True) + EPS)

        @pl.when(pl.program_id(0) == 0)
        def _():
            dnw_ref[...] = jnp.zeros_like(dnw_ref)

        dnw_ref[...] += jnp.sum(dh * xv * r, axis=0, keepdims=True)
        dhw = dh * nw_ref[...]
        dx_ref[...] = dxn_ref[...] + r * (dhw - xv * (r * r) * jnp.mean(dhw * xv, axis=-1, keepdims=True))
        if n_c:
            pl.when(pl.program_id(0) == n_steps - 1)(end)

    row = pl.BlockSpec((tm, D), lambda i: (i, 0))
    in_specs = [pl.BlockSpec((tm, dp.shape[1]), lambda i: (i, 0)) for dp in dps]
    in_specs += [_full(w.shape) for w in ws]
    in_specs += [row, row, _full((1, D))] + [_HBM] * n_c
    out_shape = [jax.ShapeDtypeStruct((t, D), F32), jax.ShapeDtypeStruct((1, D), F32)]
    out_shape += [jax.ShapeDtypeStruct(e.shape, e.dtype) for e in exchange]
    out_shape += [jax.ShapeDtypeStruct((N_DEV,) + b.shape, b.dtype) for b in bcast]
    return _pcall(body, name=name, grid=(n_steps,), in_specs=in_specs, out_specs=[row, _full((1, D))] + [_HBM] * n_c,
                  out_shape=out_shape, scratch_shapes=_sem_shapes(7 * n_c, n_c) if n_c else [],
                  compiler_params=_cp(1))(*dps, *ws, x, dxn, nw, *exchange, *bcast)


def _matmul_tn(a, b, name, out_dtype=XCH):
    t, m = a.shape
    n = b.shape[1]
    tm, tn = min(1024, t), min(1024, n)
    n_t = t // tm

    def body(a_ref, b_ref, o_ref, acc):
        @pl.when(pl.program_id(1) == 0)
        def _():
            acc[...] = jnp.zeros_like(acc)

        acc[...] += _mdot(a_ref[...], b_ref[...], TN)

        @pl.when(pl.program_id(1) == n_t - 1)
        def _():
            o_ref[...] = acc[...].astype(out_dtype)

    return _pcall(body, name=name, grid=(n // tn, n_t),
                  in_specs=[pl.BlockSpec((tm, m), lambda j, i: (i, 0)), pl.BlockSpec((tm, tn), lambda j, i: (i, j))],
                  out_specs=pl.BlockSpec((m, tn), lambda j, i: (0, j)),
                  out_shape=jax.ShapeDtypeStruct((m, n), out_dtype), scratch_shapes=[pltpu.VMEM((m, tn), F32)],
                  compiler_params=_cp(2))(a, b)


def _memkv_fwd(mem, mw, wkv):
    def body(mem_ref, mw_ref, w_ref, kv_ref):
        mv = mem_ref[...]
        mn = mv * lax.rsqrt(jnp.mean(mv * mv, axis=-1, keepdims=True) + EPS) * mw_ref[...]
        kv_ref[...] = _mdot(mn, w_ref[...], NN)

    return _pcall(body, name="memkv_fwd", out_shape=jax.ShapeDtypeStruct((N_MEM, 2 * D_XA), F32),
                  compiler_params=pltpu.CompilerParams(vmem_limit_bytes=VMEM_LIMIT))(mem, mw, wkv)


def _memkv_bwd(mem, mw, wkv, dkv):
    def body(mem_ref, mw_ref, w_ref, dkv_ref, dw_ref, dmw_ref):
        mv = mem_ref[...]
        r = lax.rsqrt(jnp.mean(mv * mv, axis=-1, keepdims=True) + EPS)
        mn = mv * r * mw_ref[...]
        dkvv = dkv_ref[...]
        dw_ref[...] = _mdot(mn, dkvv, TN)
        dmn = _mdot(dkvv, w_ref[...], NT)
        dmw_ref[...] = jnp.sum(dmn * mv * r, axis=0, keepdims=True)

    return _pcall(body, name="memkv_bwd",
                  out_shape=[jax.ShapeDtypeStruct((D, 2 * D_XA), F32), jax.ShapeDtypeStruct((1, D), F32)],
                  compiler_params=pltpu.CompilerParams(vmem_limit_bytes=VMEM_LIMIT))(mem, mw, wkv, dkv)


def _attend(q, kv):
    heads = range(XA_HEADS)
    qs = [q[:, h * HEAD:(h + 1) * HEAD] for h in heads]
    ks = [kv[:, h * HEAD:(h + 1) * HEAD] for h in heads]
    vs = [kv[:, D_XA + h * HEAD:D_XA + (h + 1) * HEAD] for h in heads]
    ss = [_mdot(qs[h], ks[h], NT) * (HEAD ** -0.5) for h in heads]
    es = [jnp.exp(s - jnp.max(s, axis=-1, keepdims=True)) for s in ss]
    ps = [e / jnp.sum(e, axis=-1, keepdims=True) for e in es]
    return ps, [_mdot(ps[h], vs[h], NN) for h in heads]


def _tail_fwd(p, ymix, x, kv, wout, npost, name):
    t = x.shape[0]
    tm = min(512, t)

    def body(z_ref, q_ref, y_ref, x_ref, kv_ref, w_ref, np_ref, xn_ref, o_ref):
        _, outs = _attend(q_ref[...], kv_ref[...])
        cat = jnp.concatenate([y_ref[...]] + [a.astype(ACT) for a in outs], axis=1)
        g = cat * _silu(z_ref[...])
        o = _mdot(g, w_ref[...], NN)
        o_ref[...] = o
        xn_ref[...] = x_ref[...] + o * lax.rsqrt(jnp.mean(o * o, axis=-1, keepdims=True) + EPS) * np_ref[...]

    row = pl.BlockSpec((tm, D), lambda i: (i, 0))
    return _pcall(
        body, name=name, grid=(t // tm,),
        in_specs=[pl.BlockSpec((tm, D_CAT), lambda i: (i, 0)), pl.BlockSpec((tm, D_XA), lambda i: (i, D_CAT // D_XA)),
                  row, row, _full((N_MEM, 2 * D_XA)), _full((D_CAT, D)), _full((1, D))],
        out_specs=[row, row],
        out_shape=[jax.ShapeDtypeStruct((t, D), F32), jax.ShapeDtypeStruct((t, D), F32)], compiler_params=_cp(1),
    )(p, p, ymix, x, kv, wout, npost)


def _tail_bwd(dxn, o, p, ymix, kv, wout, npost, name, exchange=()):
    t = dxn.shape[0]
    tm = min(256, t)
    n_x, n_steps = len(exchange), t // tm

    def body(*refs):
        dxn_ref, o_ref, z_ref, q_ref, y_ref, kv_ref, w_ref, np_ref = refs[:8]
        dzq_ref, dy_ref, dw_ref, dnp_ref, dkv_ref = refs[8 + n_x:13 + n_x]
        if n_x:
            begin, end = _exchange_parts(refs[8:8 + n_x], refs[13 + n_x:13 + 2 * n_x], (), (), *refs[13 + 2 * n_x:])
            pl.when(pl.program_id(0) == 0)(begin)

        @pl.when(pl.program_id(0) == 0)
        def _():
            dw_ref[...] = jnp.zeros_like(dw_ref)
            dnp_ref[...] = jnp.zeros_like(dnp_ref)
            dkv_ref[...] = jnp.zeros_like(dkv_ref)

        q = q_ref[...]
        kvv = kv_ref[...]
        z = z_ref[...]
        ps, outs = _attend(q, kvv)
        cat = jnp.concatenate([y_ref[...]] + [a.astype(ACT) for a in outs], axis=1)
        sz = _silu(z)
        g = cat * sz
        ov = o_ref[...]
        dr = dxn_ref[...]
        rr = lax.rsqrt(jnp.mean(ov * ov, axis=-1, keepdims=True) + EPS)
        dnp_ref[...] += jnp.sum(dr * ov * rr, axis=0, keepdims=True)
        dow = dr * np_ref[...]
        do = rr * (dow - ov * (rr * rr) * jnp.mean(dow * ov, axis=-1, keepdims=True))
        dg = _mdot(do, w_ref[...], NT).astype(ACT)
        dw_ref[...] += _mdot(g, do, TN)
        dcat = dg * sz
        dzq_ref[:, 0:D_CAT] = dg * cat * _dsilu(z)
        dy_ref[...] = dcat[:, 0:D]
        heads = range(XA_HEADS)
        dohs = [dcat[:, D + h * HEAD:D + (h + 1) * HEAD] for h in heads]
        dps = [_mdot(dohs[h], kvv[:, D_XA + h * HEAD:D_XA + (h + 1) * HEAD], NT) for h in heads]
        dss = [ps[h] * (dps[h] - jnp.sum(dps[h] * ps[h], axis=-1, keepdims=True)) for h in heads]
        dqs = [_mdot(dss[h], kvv[:, h * HEAD:(h + 1) * HEAD], NN) * (HEAD ** -0.5) for h in heads]
        dks = [_mdot(dss[h], q[:, h * HEAD:(h + 1) * HEAD], TN) * (HEAD ** -0.5) for h in heads]
        dvs = [_mdot(ps[h], dohs[h], TN) for h in heads]
        for h in heads:
            lo = h * HEAD
            dzq_ref[:, D_CAT + lo:D_CAT + lo + HEAD] = dqs[h].astype(ACT)
            dkv_ref[:, lo:lo + HEAD] += dks[h]
            dkv_ref[:, D_XA + lo:D_XA + lo + HEAD] += dvs[h]
        if n_x:
            pl.when(pl.program_id(0) == n_steps - 1)(end)

    row = pl.BlockSpec((tm, D), lambda i: (i, 0))
    return _pcall(
        body, name=name, grid=(n_steps,),
        in_specs=[row, row, pl.BlockSpec((tm, D_CAT), lambda i: (i, 0)),
                  pl.BlockSpec((tm, D_XA), lambda i: (i, D_CAT // D_XA)), row,
                  _full((N_MEM, 2 * D_XA)), _full((D_CAT, D)), _full((1, D))] + [_HBM] * n_x,
        out_specs=[pl.BlockSpec((tm, ZQ), lambda i: (i, 0)), row, _full((D_CAT, D)), _full((1, D)),
                   _full((N_MEM, 2 * D_XA))] + [_HBM] * n_x,
        out_shape=[jax.ShapeDtypeStruct((t, ZQ), ACT), jax.ShapeDtypeStruct((t, D), ACT),
                   jax.ShapeDtypeStruct((D_CAT, D), F32), jax.ShapeDtypeStruct((1, D), F32),
                   jax.ShapeDtypeStruct((N_MEM, 2 * D_XA), F32)]
        + [jax.ShapeDtypeStruct(e.shape, e.dtype) for e in exchange],
        scratch_shapes=_sem_shapes(7 * n_x, n_x) if n_x else [], compiler_params=_cp(1),
    )(dxn, o, p, p, ymix, kv, wout, npost, *exchange)


def _gmlp_chunk(us, vs, lnws, lnbs, wss, bss):
    gv = [jax.nn.gelu(v) for v in vs]
    mean = sum(jnp.sum(v, axis=-1, keepdims=True) for v in gv) / D
    cen = [v - mean for v in gv]
    var = sum(jnp.sum(c * c, axis=-1, keepdims=True) for c in cen) / D
    rstd = lax.rsqrt(var + EPS)
    row = lax.broadcasted_iota(jnp.int32, (GM_CHUNK, GM_CHUNK), 0)
    col = lax.broadcasted_iota(jnp.int32, (GM_CHUNK, GM_CHUNK), 1)
    ys = []
    for g in range(GM_GROUPS):
        vn = cen[g] * rstd * lnws[g] + lnbs[g]
        sp = _nn(jnp.where(row >= col, wss[g], 0.0), vn) + bss[g]
        ys.append(jax.nn.gelu(us[g]) * sp)
    return ys


def _split_cols(v, n, width=HEAD):
    return [v[:, k * width:(k + 1) * width] for k in range(n)]


def _gmlp_operands(u_ref, v_ref, lnw_ref, lnb_ref, ws_ref, bst_ref, r0):
    us = _split_cols(u_ref[r0:r0 + GM_CHUNK, :].astype(F32), GM_GROUPS)
    vs = _split_cols(v_ref[r0:r0 + GM_CHUNK, :].astype(F32), GM_GROUPS)
    lnws = _split_cols(lnw_ref[...], GM_GROUPS)
    lnbs = _split_cols(lnb_ref[...], GM_GROUPS)
    wss = [ws_ref[g] for g in range(GM_GROUPS)]
    bst = bst_ref[...]
    bss = [jnp.sum(bst * _onehot_row(HEAD, g), axis=1, keepdims=True) for g in range(GM_GROUPS)]
    return us, vs, lnws, lnbs, wss, bss


def _gmlp_rows(t):
    return min(GM_CHUNKS_PER_STEP, t // GM_CHUNK) * GM_CHUNK


def _gmlp_specs(rows):
    return [pl.BlockSpec((rows, D), lambda i: (i, ZQ // D)), pl.BlockSpec((rows, D), lambda i: (i, ZQ // D + 1)),
            _full((1, D)), _full((1, D)), _full((GM_GROUPS, GM_CHUNK, GM_CHUNK)), _full((GM_CHUNK, HEAD))]


def _gmlp_fwd(p, lnw, lnb, ws, bst, name):
    t = p.shape[0]
    rows = _gmlp_rows(t)

    def body(u_ref, v_ref, lnw_ref, lnb_ref, ws_ref, bst_ref, y_ref):
        for r0 in range(0, rows, GM_CHUNK):
            ys = _gmlp_chunk(*_gmlp_operands(u_ref, v_ref, lnw_ref, lnb_ref, ws_ref, bst_ref, r0))
            for g in range(GM_GROUPS):
                y_ref[r0:r0 + GM_CHUNK, g * HEAD:(g + 1) * HEAD] = ys[g].astype(ACT)

    return _pcall(body, name=name, grid=(t // rows,), in_specs=_gmlp_specs(rows),
                  out_specs=pl.BlockSpec((rows, D), lambda i: (i, 0)),
                  out_shape=jax.ShapeDtypeStruct((t, D), ACT), compiler_params=_cp(1))(p, p, lnw, lnb, ws, bst)


def _gmlp_bwd(p, dy, lnw, lnb, ws, bst, name, exchange=(), bcast=()):
    t = p.shape[0]
    rows = _gmlp_rows(t)

    def body(u_ref, v_ref, lnw_ref, lnb_ref, ws_ref, bst_ref, dy_ref, duv_ref, dlnw_ref, dlnb_ref, dws_ref, dbst_ref):
        @pl.when(pl.program_id(0) == 0)
        def _():
            dlnw_ref[...] = jnp.zeros_like(dlnw_ref)
            dlnb_ref[...] = jnp.zeros_like(dlnb_ref)
            dws_ref[...] = jnp.zeros_like(dws_ref)
            dbst_ref[...] = jnp.zeros_like(dbst_ref)

        for r0 in range(0, rows, GM_CHUNK):
            ops = _gmlp_operands(u_ref, v_ref, lnw_ref, lnb_ref, ws_ref, bst_ref, r0)
            _, vjp = jax.vjp(_gmlp_chunk, *ops)
            dus, dvs, dlnws, dlnbs, dwss, dbss = vjp(_split_cols(dy_ref[r0:r0 + GM_CHUNK, :].astype(F32), GM_GROUPS))
            dbst = jnp.zeros((GM_CHUNK, HEAD), F32)
            for g in range(GM_GROUPS):
                lo = g * HEAD
                duv_ref[r0:r0 + GM_CHUNK, lo:lo + HEAD] = dus[g].astype(ACT)
                duv_ref[r0:r0 + GM_CHUNK, D + lo:D + lo + HEAD] = dvs[g].astype(ACT)
                dlnw_ref[:, lo:lo + HEAD] += dlnws[g]
                dlnb_ref[:, lo:lo + HEAD] += dlnbs[g]
                dws_ref[g] += dwss[g]
                dbst = dbst + dbss[g] * _onehot_row(HEAD, g)
            dbst_ref[...] += dbst

    call = dict(
        name=name, grid=(t // rows,),
        in_specs=_gmlp_specs(rows) + [pl.BlockSpec((rows, D), lambda i: (i, 0))],
        out_specs=[pl.BlockSpec((rows, 2 * D), lambda i: (i, 0)), _full((1, D)), _full((1, D)),
                   _full((GM_GROUPS, GM_CHUNK, GM_CHUNK)), _full((GM_CHUNK, HEAD))],
        out_shape=[jax.ShapeDtypeStruct((t, 2 * D), ACT), jax.ShapeDtypeStruct((1, D), F32),
                   jax.ShapeDtypeStruct((1, D), F32), jax.ShapeDtypeStruct((GM_GROUPS, GM_CHUNK, GM_CHUNK), F32),
                   jax.ShapeDtypeStruct((GM_CHUNK, HEAD), F32)],
        compiler_params=_cp(1))
    args = (p, p, lnw, lnb, ws, bst, dy)
    if not exchange and not bcast:
        return _pcall(body, **call)(*args), ()
    return _pcall_hosting(body, args, exchange, bcast, **call)


def _prev_halo(tm, col):
    return pl.BlockSpec((HALO, D), lambda i: (jnp.maximum(i * (tm // HALO) - 1, 0), col))


def _next_halo(tm, col, n_tiles):
    return pl.BlockSpec((HALO, D), lambda i: (jnp.minimum(i + 1, n_tiles - 1) * (tm // HALO), col))


def _taps_back(ext, w, width):
    acc = None
    for k in range(width):
        s = width - 1 - k
        term = w[k:k + 1, :] * (pltpu.roll(ext, s, 0) if s else ext)[HALO:, :]
        acc = term if acc is None else acc + term
    return acc


def _taps_fwd(ext, w, width, n):
    rows = ext.shape[0]
    acc = None
    for k in range(width):
        s = width - 1 - k
        term = w[k:k + 1, :] * (pltpu.roll(ext, rows - s, 0) if s else ext)[0:n, :]
        acc = term if acc is None else acc + term
    return acc


def _sconv_fwd(p, cw, name):
    t = p.shape[0]
    tm = min(256, t)

    def body(b_ref, c_ref, h_ref, cp_ref, hp_ref, w_ref, y_ref):
        first = pl.program_id(0) == 0
        prev = jnp.where(first, 0.0, cp_ref[...].astype(F32) * hp_ref[...].astype(F32))
        ext = jnp.concatenate([prev, c_ref[...].astype(F32) * h_ref[...].astype(F32)], axis=0)
        y_ref[...] = (b_ref[...].astype(F32) * _taps_back(ext, w_ref[...], 3)).astype(ACT)

    c0 = ZQ // D
    tile = [pl.BlockSpec((tm, D), lambda i, c=c: (i, c)) for c in (c0, c0 + 1, c0 + 2)]
    return _pcall(body, name=name, grid=(t // tm,),
                  in_specs=tile + [_prev_halo(tm, c0 + 1), _prev_halo(tm, c0 + 2), _full((HALO, D))],
                  out_specs=pl.BlockSpec((tm, D), lambda i: (i, 0)),
                  out_shape=jax.ShapeDtypeStruct((t, D), ACT), compiler_params=_cp(1))(p, p, p, p, p, cw)


def _sconv_bwd(p, dy, cw, name):
    t = p.shape[0]
    tm = min(256, t)
    n_tiles = t // tm

    def body(b_ref, c_ref, h_ref, cp_ref, hp_ref, bn_ref, dy_ref, dyn_ref, w_ref, d_ref, dw_ref):
        i = pl.program_id(0)

        @pl.when(i == 0)
        def _():
            dw_ref[...] = jnp.zeros_like(dw_ref)

        w = w_ref[...]
        bv, cv, hv = b_ref[...].astype(F32), c_ref[...].astype(F32), h_ref[...].astype(F32)
        dyv = dy_ref[...].astype(F32)
        prev = jnp.where(i == 0, 0.0, cp_ref[...].astype(F32) * hp_ref[...].astype(F32))
        ext = jnp.concatenate([prev, cv * hv], axis=0)
        conv = _taps_back(ext, w, 3)
        dconv = dyv * bv
        nxt = jnp.where(i == n_tiles - 1, 0.0, dyn_ref[...].astype(F32) * bn_ref[...].astype(F32))
        dc = _taps_fwd(jnp.concatenate([dconv, nxt], axis=0), w, 3, tm)
        d_ref[:, 0:D] = (dyv * conv).astype(ACT)
        d_ref[:, D:2 * D] = (dc * hv).astype(ACT)
        d_ref[:, 2 * D:3 * D] = (dc * cv).astype(ACT)
        for k in range(3):
            s = 2 - k
            shifted = (pltpu.roll(ext, s, 0) if s else ext)[HALO:, :]
            dw_ref[k:k + 1, :] += jnp.sum(dconv * shifted, axis=0, keepdims=True)

    c0 = ZQ // D
    tile = [pl.BlockSpec((tm, D), lambda i, c=c: (i, c)) for c in (c0, c0 + 1, c0 + 2)]
    return _pcall(
        body, name=name, grid=(n_tiles,),
        in_specs=tile + [_prev_halo(tm, c0 + 1), _prev_halo(tm, c0 + 2), _next_halo(tm, c0, n_tiles),
                         pl.BlockSpec((tm, D), lambda i: (i, 0)), _next_halo(tm, 0, n_tiles), _full((HALO, D))],
        out_specs=[pl.BlockSpec((tm, 3 * D), lambda i: (i, 0)), _full((HALO, D))],
        out_shape=[jax.ShapeDtypeStruct((t, 3 * D), ACT), jax.ShapeDtypeStruct((HALO, D), F32)],
        compiler_params=_cp(1),
    )(p, p, p, p, p, p, dy, dy, cw)


def _l2_heads(s, scale):
    outs, rs = [], []
    for hh in range(DN_HEADS):
        blk = s[:, hh * HEAD:(hh + 1) * HEAD]
        r = lax.rsqrt(jnp.sum(blk * blk, axis=-1, keepdims=True) + EPS)
        outs.append(blk * (r * scale))
        rs.append(r)
    return outs, rs


_QKV_SCALE = (HEAD ** -0.5, 1.0, None)


def _qkv_fwd(p, cw, name):
    t = p.shape[0]
    tm = min(256, t)

    def body(q_ref, k_ref, v_ref, qp_ref, kp_ref, vp_ref, w_ref, o_ref):
        first = pl.program_id(0) == 0
        for part, (ref, pref) in enumerate(((q_ref, qp_ref), (k_ref, kp_ref), (v_ref, vp_ref))):
            prev = jnp.where(first, 0.0, pref[...].astype(F32))
            ext = jnp.concatenate([prev, ref[...].astype(F32)], axis=0)
            s = _silu(_taps_back(ext, w_ref[:, part * D:(part + 1) * D], 4))
            if _QKV_SCALE[part] is None:
                o_ref[:, part * D:(part + 1) * D] = s.astype(ACT)
            else:
                outs, _ = _l2_heads(s, _QKV_SCALE[part])
                for hh in range(DN_HEADS):
                    o_ref[:, part * D + hh * HEAD:part * D + (hh + 1) * HEAD] = outs[hh].astype(ACT)

    c0 = ZQ // D
    tile = [pl.BlockSpec((tm, D), lambda i, c=c: (i, c)) for c in (c0, c0 + 1, c0 + 2)]
    halo = [_prev_halo(tm, c) for c in (c0, c0 + 1, c0 + 2)]
    return _pcall(body, name=name, grid=(t // tm,), in_specs=tile + halo + [_full((HALO, 3 * D))],
                  out_specs=pl.BlockSpec((tm, 3 * D), lambda i: (i, 0)),
                  out_shape=jax.ShapeDtypeStruct((t, 3 * D), ACT), compiler_params=_cp(1))(p, p, p, p, p, p, cw)


def _qkv_bwd(p, dqkv, cw, name):
    t = p.shape[0]
    tm = min(256, t)
    n_tiles = t // tm

    def body(*refs):
        tiles, prevs, nexts = refs[0:3], refs[3:6], refs[6:9]
        d_tiles, d_nexts = refs[9:12], refs[12:15]
        w_ref, o_ref, dw_ref = refs[15:]
        i = pl.program_id(0)

        @pl.when(i == 0)
        def _():
            dw_ref[...] = jnp.zeros_like(dw_ref)

        for part in range(3):
            w = w_ref[:, part * D:(part + 1) * D]
            prev = jnp.where(i == 0, 0.0, prevs[part][...].astype(F32))
            ext = jnp.concatenate([prev, tiles[part][...].astype(F32), nexts[part][...].astype(F32)], axis=0)
            xc = _taps_back(ext, w, 4)
            dout = jnp.concatenate([d_tiles[part][...].astype(F32), d_nexts[part][...].astype(F32)], axis=0)
            s = _silu(xc)
            if _QKV_SCALE[part] is None:
                ds = dout
            else:
                scale = _QKV_SCALE[part]
                pieces = []
                for hh in range(DN_HEADS):
                    blk = s[:, hh * HEAD:(hh + 1) * HEAD]
                    dblk = dout[:, hh * HEAD:(hh + 1) * HEAD]
                    r = lax.rsqrt(jnp.sum(blk * blk, axis=-1, keepdims=True) + EPS)
                    pieces.append(scale * r * (dblk - blk * (r * r) * jnp.sum(dblk * blk, axis=-1, keepdims=True)))
                ds = jnp.concatenate(pieces, axis=1)
            dxc = ds * _dsilu(xc)
            row = lax.broadcasted_iota(jnp.int32, (tm + HALO, 1), 0)
            dxc = jnp.where(jnp.logical_and(i == n_tiles - 1, row >= tm), 0.0, dxc)
            o_ref[:, part * D:(part + 1) * D] = _taps_fwd(dxc, w, 4, tm).astype(ACT)
            for k in range(4):
                sh = 3 - k
                shifted = (pltpu.roll(ext, sh, 0) if sh else ext)[HALO:HALO + tm, :]
                dw_ref[k:k + 1, part * D:(part + 1) * D] += jnp.sum(dxc[0:tm, :] * shifted, axis=0, keepdims=True)

    c0 = ZQ // D
    cols = (c0, c0 + 1, c0 + 2)
    tile = [pl.BlockSpec((tm, D), lambda i, c=c: (i, c)) for c in cols]
    dtile = [pl.BlockSpec((tm, D), lambda i, c=c: (i, c)) for c in range(3)]
    in_specs = (tile + [_prev_halo(tm, c) for c in cols] + [_next_halo(tm, c, n_tiles) for c in cols]
                + dtile + [_next_halo(tm, c, n_tiles) for c in range(3)] + [_full((HALO, 3 * D))])
    return _pcall(
        body, name=name, grid=(n_tiles,), in_specs=in_specs,
        out_specs=[pl.BlockSpec((tm, 3 * D), lambda i: (i, 0)), _full((HALO, 3 * D))],
        out_shape=[jax.ShapeDtypeStruct((t, 3 * D), ACT), jax.ShapeDtypeStruct((HALO, 3 * D), F32)],
        compiler_params=_cp(1),
    )(*([p] * 9), *([dqkv] * 6), cw)


def _tri_masks(n):
    row = lax.broadcasted_iota(jnp.int32, (n, n), 0)
    col = lax.broadcasted_iota(jnp.int32, (n, n), 1)
    return row, col


@jax.custom_vjp
def _unit_lower_inverses(mats):
    n = DN_CHUNK
    row, col = _tri_masks(n)
    eye = (row == col).astype(F32)
    same16 = (row // 16) == (col // 16)
    same32 = (row // 32) == (col // 32)
    pw = [jnp.where(same16, a, 0.0) for a in mats]
    x = [eye - p for p in pw]
    for _ in range(3):
        pw = [_hnn(p, p) for p in pw]
        x = [_hnn(xi, eye + p) for xi, p in zip(x, pw)]
    for keep in (jnp.logical_and(same32, jnp.logical_not(same16)), jnp.logical_not(same32)):
        inner = [_hnn(jnp.where(keep, a, 0.0), xi) for a, xi in zip(mats, x)]
        x = [xi - _hnn(xi, y) for xi, y in zip(x, inner)]
    return tuple(x)


def _uli_fwd(mats):
    t = _unit_lower_inverses(mats)
    return t, t


def _uli_bwd(ts, gs):
    inner = [_nt(g, t) for g, t in zip(gs, ts)]
    return (tuple(-_tn(t, y) for t, y in zip(ts, inner)),)


_unit_lower_inverses.defvjp(_uli_fwd, _uli_bwd)


@jax.custom_vjp
def _known_inverses(mats, ts):
    return ts


_known_inverses.defvjp(lambda mats, ts: (ts, ts),
                       lambda ts, gs: (_uli_bwd(ts, gs)[0], tuple(jnp.zeros_like(t) for t in ts)))


def _pick_col(m, k):
    return jnp.sum(m * _onehot_row(m.shape[1], k), axis=1, keepdims=True)


def _pick_row(m, k):
    hot = (lax.broadcasted_iota(jnp.int32, (m.shape[0], 1), 0) == k).astype(F32)
    return jnp.sum(m * hot, axis=0, keepdims=True)


def _delta_chunk(states, qs, ks, vs, ab, alog, dtb, onw, known_inverses=None):
    n = DN_CHUNK
    heads = range(DN_HEADS)
    row, col = _tri_masks(n)
    incl = row >= col
    lane = lax.broadcasted_iota(jnp.int32, (1, HEAD), 1)
    g_all = jnp.where(lane < DN_HEADS, -jnp.exp(alog) * jax.nn.softplus(ab + dtb), 0.0)
    c_cols = _hnn(incl.astype(F32), g_all)
    c_rows = _htn(g_all, (row <= col).astype(F32))
    g_tot = jnp.sum(g_all, axis=0, keepdims=True)
    beta_all = jax.nn.sigmoid(ab)
    ccol = [_pick_col(c_cols, h) for h in heads]
    crow = [_pick_row(c_rows, h) for h in heads]
    gl = [_pick_col(g_tot, h) for h in heads]
    beta = [_pick_col(beta_all, DN_HEADS + h) for h in heads]
    decay = [jnp.exp(jnp.where(incl, ccol[h] - crow[h], -1e30)) for h in heads]
    eg = [jnp.exp(ccol[h]) for h in heads]
    kb = [ks[h] * beta[h] for h in heads]
    amat = [jnp.where(row > col, _nt(kb[h], ks[h]) * decay[h], 0.0) for h in heads]
    if known_inverses is None:
        tmat = _unit_lower_inverses(tuple(amat))
    else:
        tmat = _known_inverses(tuple(amat), tuple(known_inverses))
    u = [_nn(tmat[h], vs[h] * beta[h]) for h in heads]
    w = [_nn(tmat[h], kb[h] * eg[h]) for h in heads]
    qk = [_nt(qs[h], ks[h]) * decay[h] for h in heads]
    v_new = [u[h] - _nn(w[h], states[h]) for h in heads]
    o = [_nn(qs[h] * eg[h], states[h]) + _nn(qk[h], v_new[h]) for h in heads]
    new_states = [states[h] * jnp.exp(gl[h]) + _tn(ks[h] * jnp.exp(gl[h] - ccol[h]), v_new[h]) for h in heads]
    ys = [o[h] * lax.rsqrt(jnp.mean(o[h] * o[h], axis=-1, keepdims=True) + EPS) * onw for h in heads]
    return (ys, new_states), tmat


def _head_cols(ref, r0):
    return [ref[r0:r0 + DN_CHUNK, h * HEAD:(h + 1) * HEAD].astype(F32) for h in range(DN_HEADS)]


def _delta_rows(t):
    return min(DN_CHUNKS_PER_STEP, t // DN_CHUNK) * DN_CHUNK


def _delta_fwd(qkv, ab, alog, dtb, onw, name):
    t = qkv.shape[0]
    rows = _delta_rows(t)
    per_step = rows // DN_CHUNK

    def body(q_ref, k_ref, v_ref, ab_ref, alog_ref, dtb_ref, onw_ref, y_ref, keep_ref, inv_ref, state):
        @pl.when(pl.program_id(0) == 0)
        def _():
            state[...] = jnp.zeros_like(state)

        s = [state[hh] for hh in range(DN_HEADS)]
        for c in range(per_step):
            r0 = c * DN_CHUNK
            (ys, s1), tmat = _delta_chunk(s, _head_cols(q_ref, r0), _head_cols(k_ref, r0), _head_cols(v_ref, r0),
                                          ab_ref[r0:r0 + DN_CHUNK, :], alog_ref[...], dtb_ref[...], onw_ref[...])
            for hh in range(DN_HEADS):
                keep_ref[c, hh] = s[hh]
                inv_ref[c, hh] = tmat[hh]
                y_ref[r0:r0 + DN_CHUNK, hh * HEAD:(hh + 1) * HEAD] = ys[hh].astype(ACT)
            s = s1
        for hh in range(DN_HEADS):
            state[hh] = s[hh]

    block = [pl.BlockSpec((rows, D), lambda i, c=c: (i, c)) for c in range(3)]
    nc = t // DN_CHUNK
    return _pcall(
        body, name=name, grid=(t // rows,),
        in_specs=block + [pl.BlockSpec((rows, HEAD), lambda i: (i, 0)), _full((1, HEAD)), _full((1, HEAD)),
                          _full((1, HEAD))],
        out_specs=[pl.BlockSpec((rows, D), lambda i: (i, 0)),
                   pl.BlockSpec((per_step, DN_HEADS, HEAD, HEAD), lambda i: (i, 0, 0, 0)),
                   pl.BlockSpec((per_step, DN_HEADS, DN_CHUNK, DN_CHUNK), lambda i: (i, 0, 0, 0))],
        out_shape=[jax.ShapeDtypeStruct((t, D), ACT), jax.ShapeDtypeStruct((nc, DN_HEADS, HEAD, HEAD), F32),
                   jax.ShapeDtypeStruct((nc, DN_HEADS, DN_CHUNK, DN_CHUNK), F32)],
        scratch_shapes=[pltpu.VMEM((DN_HEADS, HEAD, HEAD), F32)], compiler_params=_cp(1),
    )(qkv, qkv, qkv, ab, alog, dtb, onw)


def _delta_bwd(qkv, ab, alog, dtb, onw, keep, inv, dy, name):
    t = qkv.shape[0]
    rows = _delta_rows(t)
    per_step, n_steps = rows // DN_CHUNK, t // rows

    def body(q_ref, k_ref, v_ref, ab_ref, alog_ref, dtb_ref, onw_ref, keep_ref, inv_ref, dy_ref,
             dqkv_ref, dab_ref, dalog_ref, ddtb_ref, donw_ref, dstate):
        @pl.when(pl.program_id(0) == 0)
        def _():
            dstate[...] = jnp.zeros_like(dstate)
            dalog_ref[...] = jnp.zeros_like(dalog_ref)
            ddtb_ref[...] = jnp.zeros_like(ddtb_ref)
            donw_ref[...] = jnp.zeros_like(donw_ref)

        ds = [dstate[hh] for hh in range(DN_HEADS)]
        for c in reversed(range(per_step)):
            r0 = c * DN_CHUNK
            s0 = [keep_ref[c, hh] for hh in range(DN_HEADS)]
            known = [inv_ref[c, hh] for hh in range(DN_HEADS)]
            _, vjp, _ = jax.vjp(functools.partial(_delta_chunk, known_inverses=known), s0, _head_cols(q_ref, r0),
                                _head_cols(k_ref, r0), _head_cols(v_ref, r0), ab_ref[r0:r0 + DN_CHUNK, :], alog_ref[...],
                                dtb_ref[...], onw_ref[...], has_aux=True)
            ds, dq, dk, dv, dab, dal, ddt, don = vjp((_head_cols(dy_ref, r0), ds))
            for hh in range(DN_HEADS):
                lo = hh * HEAD
                dqkv_ref[r0:r0 + DN_CHUNK, lo:lo + HEAD] = dq[hh].astype(ACT)
                dqkv_ref[r0:r0 + DN_CHUNK, D + lo:D + lo + HEAD] = dk[hh].astype(ACT)
                dqkv_ref[r0:r0 + DN_CHUNK, 2 * D + lo:2 * D + lo + HEAD] = dv[hh].astype(ACT)
            dab_ref[r0:r0 + DN_CHUNK, :] = dab
            dalog_ref[...] += dal
            ddtb_ref[...] += ddt
            donw_ref[...] += don
        for hh in range(DN_HEADS):
            dstate[hh] = ds[hh]

    rev = lambda i: n_steps - 1 - i
    block = [pl.BlockSpec((rows, D), lambda i, c=c: (rev(i), c)) for c in range(3)]
    small = jax.ShapeDtypeStruct((1, HEAD), F32)
    return _pcall(
        body, name=name, grid=(n_steps,),
        in_specs=block + [pl.BlockSpec((rows, HEAD), lambda i: (rev(i), 0)), _full((1, HEAD)), _full((1, HEAD)),
                          _full((1, HEAD)), pl.BlockSpec((per_step, DN_HEADS, HEAD, HEAD), lambda i: (rev(i), 0, 0, 0)),
                          pl.BlockSpec((per_step, DN_HEADS, DN_CHUNK, DN_CHUNK), lambda i: (rev(i), 0, 0, 0)),
                          pl.BlockSpec((rows, D), lambda i: (rev(i), 0))],
        out_specs=[pl.BlockSpec((rows, 3 * D), lambda i: (rev(i), 0)),
                   pl.BlockSpec((rows, HEAD), lambda i: (rev(i), 0)), _full((1, HEAD)), _full((1, HEAD)),
                   _full((1, HEAD))],
        out_shape=[jax.ShapeDtypeStruct((t, 3 * D), ACT), jax.ShapeDtypeStruct((t, HEAD), F32), small, small, small],
        scratch_shapes=[pltpu.VMEM((DN_HEADS, HEAD, HEAD), F32)], compiler_params=_cp(1),
    )(qkv, qkv, qkv, ab, alog, dtb, onw, keep, inv, dy)


def _loss_head(y, target):
    t = y.shape[0]
    tm = min(512, t)

    def body(y_ref, t_ref, l_ref, dy_ref):
        @pl.when(pl.program_id(0) == 0)
        def _():
            l_ref[...] = jnp.zeros_like(l_ref)

        diff = y_ref[...] - t_ref[...]
        dy_ref[...] = diff * (1.0 / D)
        l_ref[...] += 0.5 * jnp.sum(jnp.sum(diff * diff, axis=-1, keepdims=True) * (1.0 / D), axis=0, keepdims=True)

    row = pl.BlockSpec((tm, D), lambda i: (i, 0))
    return _pcall(body, name="loss_head", grid=(t // tm,), in_specs=[row, row], out_specs=[_full((8, 128)), row],
                  out_shape=[jax.ShapeDtypeStruct((8, 128), F32), jax.ShapeDtypeStruct((t, D), F32)],
                  compiler_params=_cp(1))(y, target)


_SMALL_SHARDED = (("a_ln_w", (2, 128), 1), ("a_ln_b", (2, 128), 1), ("b_conv_w", (1, 3, 128), 2),
                  ("c_conv_w", (1, 4, 384), 2))
_REPLICATED = (("mem_norm_w", (1024,)), ("norm_pre", (4, 1024)), ("norm_post", (4, 1024)),
               ("a_b_s", (2, 8, 128)), ("c_a_log", (1, 8)), ("c_dt_bias", (1, 8)), ("c_o_norm_w", (1, 128)))


def _layer_shards(given, prefix):
    w_out = given[prefix + "w_out"]
    w_ins = [given[prefix + "a_w_in"][0], given[prefix + "b_w_in"], given[prefix + "c_w_in"], given[prefix + "a_w_in"][1]]
    return [[w_out[i], w_ins[i]] for i in range(4)]


def _as_matmul_operand(w):
    return w.reshape(w.shape[-2:]).astype(MXU)


def _as_rows(shape):
    return (math.prod(shape[:-1]) if len(shape) > 1 else 1, shape[-1])


def _row_table(shapes):
    table, at = [], 0
    for shape in shapes:
        rows, cols = _as_rows(shape)
        table.append((at, rows, cols))
        at += rows
    return table


def _pack_rows(arrays, shapes, rows, lead=()):
    n_lead = len(lead)
    parts = [jnp.pad(a.reshape(lead + _as_rows(s)), [(0, 0)] * (n_lead + 1) + [(0, D - s[-1])])
             for a, s in zip(arrays, shapes)]
    block = jnp.concatenate(parts, axis=n_lead)
    return jnp.pad(block, [(0, 0)] * n_lead + [(0, rows - block.shape[n_lead]), (0, 0)])


def _unpack_rows(packed, shapes, lead=()):
    n_lead = len(lead)
    return [packed[(slice(None),) * n_lead + (slice(r0, r0 + nr), slice(0, nc))].reshape(lead + s)
            for (r0, nr, nc), s in zip(_row_table(shapes), shapes)]


def _join_shards(blocks, axis):
    moved = jnp.moveaxis(blocks, 0, axis)
    shape = moved.shape
    return moved.reshape(shape[:axis] + (shape[axis] * shape[axis + 1],) + shape[axis + 2:])


def _split_shards(full, axis):
    shape = full.shape
    split = full.reshape(shape[:axis] + (N_DEV, shape[axis] // N_DEV) + shape[axis + 1:])
    return jnp.moveaxis(split, axis, 0)


_A_COLS = ((2560, 1536), (2048, 512), (0, 2048))
_BC_COLS = ((3584, 1536), (3072, 512), (0, 3072))
_C_COLS = ((3600, 1536), (3088, 512), (0, 3072))


def _reorder_cols(w, cols):
    return jnp.concatenate([w[:, s:s + n] for s, n in cols], axis=1)


def _restore_cols(pieces_in_my_order, cols, extra=()):
    placed = sorted(list(zip([s for s, _ in cols], pieces_in_my_order)) + list(extra), key=lambda sp: sp[0])
    return jnp.concatenate([piece for _, piece in placed], axis=1)


def kernel(x, mem, mem_norm_w, w_mem_kv, norm_pre, norm_post, w_out, a_w_in, a_ln_w, a_ln_b, a_w_s, a_b_s, b_w_in, b_conv_w, c_w_in, c_conv_w, c_a_log, c_dt_bias, c_o_norm_w, loss_target, m_mem_norm_w, m_w_mem_kv, m_norm_pre, m_norm_post, m_w_out, m_a_w_in, m_a_ln_w, m_a_ln_b, m_a_w_s, m_a_b_s, m_b_w_in, m_b_conv_w, m_c_w_in, m_c_conv_w, m_c_a_log, m_c_dt_bias, m_c_o_norm_w, v_mem_norm_w, v_w_mem_kv, v_norm_pre, v_norm_post, v_w_out, v_a_w_in, v_a_ln_w, v_a_ln_b, v_a_w_s, v_a_b_s, v_b_w_in, v_b_conv_w, v_c_w_in, v_c_conv_w, v_c_a_log, v_c_dt_bias, v_c_o_norm_w):
    given = dict(locals())
    x0 = x[0]
    mem0 = mem[0]
    target = loss_target[0]

    w_sh, m_sh, v_sh = (_layer_shards(given, pre) for pre in ("", "m_", "v_"))
    small_names = [n for n, _, _ in _SMALL_SHARDED]
    small_shapes = [s for _, s, _ in _SMALL_SHARDED]
    repl_names = [n for n, _ in _REPLICATED]
    repl_shapes = [s for _, s in _REPLICATED]
    small = _pack_rows([given[n] for n in small_names], small_shapes, R_SMALL)
    g_wkv, g_wout0, g_in0, g_small = _all_gather(
        [w_mem_kv.astype(MXU)] + [_as_matmul_operand(w) for w in w_sh[0]] + [small], "gather_weights")
    small_full = _unpack_rows(g_small, small_shapes, lead=(N_DEV,))
    full = {n: _join_shards(blocks, ax) for (n, _, ax), blocks in zip(_SMALL_SHARDED, small_full)}
    wkv = g_wkv.reshape(D, 2 * D_XA)
    wouts = [g_wout0.reshape(D_CAT, D)]
    w_in = [_reorder_cols(_join_shards(g_in0, 1), _A_COLS)]
    w_cab = None
    lnw = [full["a_ln_w"][j][None, :] for j in range(2)]
    lnb = [full["a_ln_b"][j][None, :] for j in range(2)]
    ws = [a_w_s[j] for j in range(2)]
    bst = [jnp.pad(a_b_s[j].T, ((0, 0), (0, HEAD - GM_GROUPS))) for j in range(2)]
    cw_b = jnp.pad(full["b_conv_w"][0], ((0, HALO - 3), (0, 0)))
    cw_c = jnp.pad(full["c_conv_w"][0], ((0, HALO - 4), (0, 0)))
    alog = jnp.pad(c_a_log, ((0, 0), (0, HEAD - DN_HEADS)))
    dtb = jnp.pad(c_dt_bias, ((0, 0), (0, HEAD - DN_HEADS)))
    onw = c_o_norm_w
    mw = mem_norm_w[None, :]

    kv = _memkv_fwd(mem0, mw, wkv).astype(MXU)
    xs, saved = [x0], []
    for i in range(4):
        kind = i % 3
        npre, npost = norm_pre[i][None, :], norm_post[i][None, :]
        ahead = [_as_matmul_operand(w) for w in w_sh[i + 1]] if i < 3 else []
        res = _proj_fwd(xs[i], npre, w_in[i], w_cab if kind == 2 else None, f"proj_fwd_{i}", gather=ahead)
        if ahead:
            g_wout, g_in = res[-2:]
            wouts.append(g_wout.reshape(D_CAT, D))
            if i + 1 == 2:
                c_full = jnp.concatenate([g_in[d] for d in range(N_DEV)], axis=1)
                w_in.append(_reorder_cols(c_full, _C_COLS))
                w_cab = jnp.pad(c_full[:, 3072:3088], ((0, 0), (0, HEAD - 16)))
            else:
                w_in.append(_reorder_cols(_join_shards(g_in, 1), _BC_COLS if i + 1 == 1 else _A_COLS))
        if kind == 2:
            p, h, ab = res[:3]
            qkv = _qkv_fwd(p, cw_c, f"qkv_fwd_{i}")
            ymix, keep, inv = _delta_fwd(qkv, ab, alog, dtb, onw, f"delta_fwd_{i}")
            extra = (qkv, ab, keep, inv)
        else:
            p, h = res[:2]
            if kind == 0:
                ymix = _gmlp_fwd(p, lnw[i // 3], lnb[i // 3], ws[i // 3], bst[i // 3], f"gmlp_fwd_{i}")
            else:
                ymix = _sconv_fwd(p, cw_b, f"sconv_fwd_{i}")
            extra = ()
        xn, o = _tail_fwd(p, ymix, xs[i], kv, wouts[i], npost, f"tail_fwd_{i}")
        xs.append(xn)
        saved.append((p, h, ymix, o, extra))

    loss_tile, dx = _loss_head(xs[4], target)
    loss = lax.psum(loss_tile[0, 0], ("x", "y", "c"))

    g = {}
    d_npre, d_npost = [None] * 4, [None] * 4
    d_ws, d_bs, d_lnw, d_lnb = [None] * 2, [None] * 2, [None] * 2, [None] * 2
    dkv = None
    pend_win, pend_wout = None, None
    landed_win, landed_wout = [None] * 4, [None] * 4
    for i in reversed(range(4)):
        kind = i % 3
        p, h, ymix, o, extra = saved[i]
        npre, npost = norm_pre[i][None, :], norm_post[i][None, :]
        res = _tail_bwd(dx, o, p, ymix, kv, wouts[i], npost, f"tail_bwd_{i}", exchange=[pend_win] if i < 3 else [])
        dzq, dymix, d_wout, d_npost[i], dkv_i = res[:5]
        if i < 3:
            landed_win[i + 1] = res[5]
        d_wout = d_wout.reshape(N_DEV, D_CAT // N_DEV, D).astype(XCH)
        dkv = dkv_i if dkv is None else dkv + dkv_i
        w_zq, w_mix = w_in[i][:, :ZQ], w_in[i][:, ZQ:]
        dw_zq = _matmul_tn(h, dzq, f"dw_zq_{i}")
        if kind == 0:
            j = i // 3
            early_x, early_b = [], []
            if i == 0:
                d_wkv, d_mw = _memkv_bwd(mem0, mw, wkv, dkv)
                early_x = [d_wkv.reshape(N_DEV, 128, D).astype(XCH), d_wout, pend_wout]
                early_b = [d_ws[1].reshape(GM_CHUNK, D).astype(XCH)]
            (dmix, d_lnw[j], d_lnb[j], d_ws[j], dbst), early_landed = _gmlp_bwd(
                p, dymix, lnw[j], lnb[j], ws[j], bst[j], f"gmlp_bwd_{i}", exchange=early_x, bcast=early_b)
            d_bs[j] = dbst[:, :GM_GROUPS].T
            dw_mix = _matmul_tn(h, dmix, f"dw_mix_{i}")
            d_win = _restore_cols([dw_zq[:, :D_CAT], dw_zq[:, D_CAT:], dw_mix], _A_COLS)
            dps, wparts = [dzq, dmix], [w_zq, w_mix]
        elif kind == 1:
            dmix, dcw = _sconv_bwd(p, dymix, cw_b, f"sconv_bwd_{i}")
            g["b_conv_w"] = dcw[None, :3]
            dw_mix = _matmul_tn(h, dmix, f"dw_mix_{i}")
            d_win = _restore_cols([dw_zq[:, :D_CAT], dw_zq[:, D_CAT:], dw_mix], _BC_COLS)
            dps, wparts = [dzq, dmix], [w_zq, w_mix]
        else:
            qkv, ab, keep, inv = extra
            dqkv, dab, dalog, ddtb, donw = _delta_bwd(qkv, ab, alog, dtb, onw, keep, inv, dymix, f"delta_bwd_{i}")
            dmix, dcw = _qkv_bwd(p, dqkv, cw_c, f"qkv_bwd_{i}")
            g["c_conv_w"] = dcw[None, :4]
            g["c_a_log"], g["c_dt_bias"], g["c_o_norm_w"] = dalog[:, :DN_HEADS], ddtb[:, :DN_HEADS], donw
            dw_mix = _matmul_tn(h, dmix, f"dw_mix_{i}")
            dw_ab = _matmul_tn(h, dab, f"dw_ab_{i}")
            d_win = _restore_cols([dw_zq[:, :D_CAT], dw_zq[:, D_CAT:], dw_mix], _C_COLS, extra=[(3072, dw_ab[:, :16])])
            dps, wparts = [dzq, dmix, dab], [w_zq, w_mix, w_cab]
        width = d_win.shape[1] // N_DEV
        pend_win = jnp.stack([d_win[:, d * width:(d + 1) * width] for d in range(N_DEV)]).astype(XCH)
        if i > 0:
            res = _proj_bwd_x(dps, wparts, xs[i], dx, npre, f"proj_bwd_x_{i}", exchange=[pend_wout] if i < 3 else [])
            dx, d_npre[i] = res[:2]
            if i < 3:
                landed_wout[i + 1] = res[2]
            pend_wout = d_wout

    g["mem_norm_w"] = d_mw[0]
    g["norm_pre"] = jnp.concatenate([jnp.zeros((1, D), F32)] + d_npre[1:], axis=0)
    g["norm_post"] = jnp.concatenate(d_npost, axis=0)
    g["a_ln_w"] = jnp.concatenate(d_lnw, axis=0)
    g["a_ln_b"] = jnp.concatenate(d_lnb, axis=0)
    g["a_b_s"] = jnp.stack(d_bs)
    e_small = _pack_rows([_split_shards(g[n], ax) for n, _, ax in _SMALL_SHARDED], small_shapes, R_SMALL, lead=(N_DEV,))
    r_pack = _pack_rows([g[n] for n in repl_names], repl_shapes, R_REPL)
    dx, d_npre0, l_in, l_small, ws0_all, r_all = _proj_bwd_x(
        dps, wparts, xs[0], dx, norm_pre[0][None, :], "proj_bwd_x_0", exchange=[pend_win, e_small],
        bcast=[d_ws[0].reshape(GM_CHUNK, D).astype(XCH), r_pack])
    l_wkv, landed_wout[0], landed_wout[1], ws3_all = early_landed
    landed_win[0] = l_in
    npre0_all, = _all_gather([jnp.pad(d_npre0, ((0, HALO - 1), (0, 0)))], "gather_norm_pre0")
    r_all = r_all.at[:, 1, :].set(npre0_all[:, 0, :])

    res = [[_reduce_adamw(parts, w_sh[i][a], m_sh[i][a], v_sh[i][a], f"adamw_{i}_{a}")
            for a, parts in enumerate((landed_wout[i], landed_win[i]))] for i in range(4)]
    res_wkv = _reduce_adamw(l_wkv, w_mem_kv, m_w_mem_kv, v_w_mem_kv, "adamw_w_mem_kv")
    res_small = _reduce_adamw_vectors(
        l_small, small, *(_pack_rows([given[pre + n] for n in small_names], small_shapes, R_SMALL) for pre in ("m_", "v_")),
        small_shapes, "adamw_small")
    res_ws = [_reduce_adamw(parts, *(given[pre + "a_w_s"][j].reshape(GM_CHUNK, D) for pre in ("", "m_", "v_")),
                            f"adamw_a_w_s_{j}") for j, parts in enumerate((ws0_all, ws3_all))]
    res_repl = _reduce_adamw_vectors(
        r_all, *(_pack_rows([given[pre + n] for n in repl_names], repl_shapes, R_REPL) for pre in ("", "m_", "v_")),
        repl_shapes, "adamw_replicated")

    order = ["mem_norm_w", "w_mem_kv", "norm_pre", "norm_post", "w_out", "a_w_in", "a_ln_w", "a_ln_b", "a_w_s", "a_b_s",
             "b_w_in", "b_conv_w", "c_w_in", "c_conv_w", "c_a_log", "c_dt_bias", "c_o_norm_w"]
    outs = [loss, dx[None]]
    for kind in range(4):
        got = dict(zip(repl_names, res_repl[kind]))
        got["a_w_s"] = jnp.stack([r[kind] for r in res_ws]).reshape(a_w_s.shape)
        got.update(zip(small_names, res_small[kind]))
        got["w_mem_kv"] = res_wkv[kind]
        got["w_out"] = jnp.stack([res[i][0][kind] for i in range(4)])
        got["a_w_in"] = jnp.stack([res[0][1][kind], res[3][1][kind]])
        got["b_w_in"] = res[1][1][kind]
        got["c_w_in"] = res[2][1][kind]
        outs += [got[n] for n in order]
    return tuple(outs)
```

```python
import functools
import math

import jax
import jax.numpy as jnp
from jax import lax
from jax.experimental import pallas as pl
from jax.experimental.pallas import tpu as pltpu

F32 = jnp.float32
MXU = jnp.bfloat16
ACT = jnp.bfloat16

D = 1024
D_XA = 512
D_CAT = 1536
N_MEM = 256
XA_HEADS = 4
HEAD = 128
ZQ = D_CAT + D_XA
EPS = 1e-6
GM_CHUNK = 128
GM_CHUNKS_PER_STEP = 4
GM_GROUPS = 8
DN_HEADS = 8
DN_CHUNK = 64
DN_CHUNKS_PER_STEP = 4
N_DEV = 8
HALO = 8
VMEM_LIMIT = 56 * 1024 * 1024
XCH = jnp.bfloat16
R_REPL = 32
R_SMALL = 16

ADAM_LR = 0.001
ADAM_B1 = 0.9
ADAM_B2 = 0.999
ADAM_EPS = 1e-08
ADAM_WD = 0.01
ADAM_STEP = 10

NN = ((1,), (0,))
NT = ((1,), (1,))
TN = ((0,), (0,))
MESH = pl.DeviceIdType.MESH


def _pcall(body, **kw):
    return pl.pallas_call(body, **kw)


def _cp(n_axes):
    return pltpu.CompilerParams(dimension_semantics=("arbitrary",) * n_axes, vmem_limit_bytes=VMEM_LIMIT)


def _dot(a, b, dims, prec=None):
    return lax.dot_general(a, b, (dims, ((), ())), preferred_element_type=F32, precision=prec)


def _mdot(a, b, dims):
    return _dot(a.astype(MXU), b.astype(MXU), dims)


def _make_mms(raw):
    @jax.custom_vjp
    def nn(a, b):
        return raw(a, b, NN)

    @jax.custom_vjp
    def nt(a, b):
        return raw(a, b, NT)

    @jax.custom_vjp
    def tn(a, b):
        return raw(a, b, TN)

    nn.defvjp(lambda a, b: (nn(a, b), (a, b)), lambda r, g: (nt(g, r[1]), tn(r[0], g)))
    nt.defvjp(lambda a, b: (nt(a, b), (a, b)), lambda r, g: (nn(g, r[1]), tn(g, r[0])))
    tn.defvjp(lambda a, b: (tn(a, b), (a, b)), lambda r, g: (nt(r[1], g), nn(r[0], g)))
    return nn, nt, tn


_nn, _nt, _tn = _make_mms(_mdot)


def _split_dot(a, b, dims):
    ah = a.astype(jnp.bfloat16)
    bh = b.astype(jnp.bfloat16)
    al = (a - ah.astype(F32)).astype(jnp.bfloat16)
    bl = (b - bh.astype(F32)).astype(jnp.bfloat16)
    return _dot(ah, bh, dims) + (_dot(ah, bl, dims) + _dot(al, bh, dims))


_hnn, _hnt, _htn = _make_mms(_split_dot)


def _full(shape):
    return pl.BlockSpec(shape, lambda *_: (0,) * len(shape))


def _silu(z):
    return z * jax.nn.sigmoid(z)


def _dsilu(z):
    s = jax.nn.sigmoid(z)
    return s * (1.0 + z * (1.0 - s))


def _onehot_row(n, k):
    return (lax.broadcasted_iota(jnp.int32, (1, n), 1) == k).astype(F32)


_HBM = pl.BlockSpec(memory_space=pl.ANY)


def _sem_shapes(n_remote, n_local):
    return [pltpu.SemaphoreType.DMA((n_remote,)), pltpu.SemaphoreType.DMA((n_remote,)),
            pltpu.SemaphoreType.DMA((n_local,))]


def _gather_parts(x_refs, out_refs, send_sems, recv_sems, local_sems):
    n = len(x_refs)
    x, y, cc = lax.axis_index("x"), lax.axis_index("y"), lax.axis_index("c")
    me, sibling = (x, y, cc), (x, y, 1 - cc)
    chips = [(1 - x, y), (x, 1 - y), (1 - x, 1 - y)]

    def slot(a, px, py, pc):
        return out_refs[a].at[4 * px + 2 * py + pc]

    def copy(k, a, block, to, src=None):
        return pltpu.make_async_remote_copy(
            src_ref=slot(a, *block) if src is None else src, dst_ref=slot(a, *block),
            send_sem=send_sems.at[k * n + a], recv_sem=recv_sems.at[k * n + a], device_id=to, device_id_type=MESH)

    mine = [pltpu.make_async_copy(x_refs[a], slot(a, *me), local_sems.at[a]) for a in range(n)]
    first = [copy(0, a, me, sibling, src=x_refs[a]) for a in range(n)]
    first += [copy(1 + j, a, me, (*chip, cc), src=x_refs[a]) for j, chip in enumerate(chips) for a in range(n)]

    def begin():
        for cp in mine + first:
            cp.start()

    def end():
        passed = []
        for j, chip in enumerate(chips):
            for a in range(n):
                copy(1 + j, a, (*chip, cc), me).wait_recv()
                passed.append(copy(4 + j, a, (*chip, cc), sibling))
                passed[-1].start()
        for a in range(n):
            copy(0, a, sibling, me).wait_recv()
        for j, chip in enumerate(chips):
            for a in range(n):
                copy(4 + j, a, (*chip, 1 - cc), me).wait_recv()
        for cp in first + passed:
            cp.wait_send()
        for cp in mine:
            cp.wait()

    return begin, end


def _exchange_parts(g_refs, out_refs, b_refs, ball_refs, send_sems, recv_sems, local_sems):
    n, nb = len(g_refs), len(b_refs)
    per_peer = n + nb
    x, y, cc = lax.axis_index("x"), lax.axis_index("y"), lax.axis_index("c")
    my_idx = 4 * x + 2 * y + cc
    mine = [pltpu.make_async_copy(g_refs[a].at[my_idx], out_refs[a].at[my_idx], local_sems.at[a]) for a in range(n)]
    mine += [pltpu.make_async_copy(b_refs[a], ball_refs[a].at[my_idx], local_sems.at[n + a]) for a in range(nb)]
    copies = []
    for k in range(1, N_DEV):
        px = 1 - x if k & 4 else x
        py = 1 - y if k & 2 else y
        pc = 1 - cc if k & 1 else cc
        base = (k - 1) * per_peer
        for a in range(n):
            copies.append(pltpu.make_async_remote_copy(
                src_ref=g_refs[a].at[4 * px + 2 * py + pc], dst_ref=out_refs[a].at[my_idx],
                send_sem=send_sems.at[base + a], recv_sem=recv_sems.at[base + a],
                device_id=(px, py, pc), device_id_type=MESH))
        for a in range(nb):
            copies.append(pltpu.make_async_remote_copy(
                src_ref=b_refs[a], dst_ref=ball_refs[a].at[my_idx], send_sem=send_sems.at[base + n + a],
                recv_sem=recv_sems.at[base + n + a], device_id=(px, py, pc), device_id_type=MESH))

    def begin():
        for cp in mine + copies:
            cp.start()

    def end():
        for cp in copies:
            cp.wait_recv()
        for cp in copies:
            cp.wait_send()
        for cp in mine:
            cp.wait()

    return begin, end


def _pcall_hosting(body, args, exchange, bcast, *, name, grid, in_specs, out_specs, out_shape, compiler_params):
    n_in, n_out, n_x, n_b = len(in_specs), len(out_specs), len(exchange), len(bcast)
    n_c, n_steps = n_x + n_b, grid[0]

    def hosted(*refs):
        ins, c_in = refs[:n_in], refs[n_in:n_in + n_c]
        outs, c_out = refs[n_in + n_c:n_in + n_c + n_out], refs[n_in + n_c + n_out:n_in + 2 * n_c + n_out]
        begin, end = _exchange_parts(c_in[:n_x], c_out[:n_x], c_in[n_x:], c_out[n_x:], *refs[n_in + 2 * n_c + n_out:])
        pl.when(pl.program_id(0) == 0)(begin)
        body(*ins, *outs)
        pl.when(pl.program_id(0) == n_steps - 1)(end)

    landed_shape = [jax.ShapeDtypeStruct(e.shape, e.dtype) for e in exchange]
    landed_shape += [jax.ShapeDtypeStruct((N_DEV,) + b.shape, b.dtype) for b in bcast]
    res = _pcall(hosted, name=name, grid=grid, in_specs=list(in_specs) + [_HBM] * n_c,
                 out_specs=list(out_specs) + [_HBM] * n_c, out_shape=list(out_shape) + landed_shape,
                 scratch_shapes=_sem_shapes(7 * n_c, n_c), compiler_params=compiler_params)(*args, *exchange, *bcast)
    return res[:n_out], res[n_out:]


def _all_gather(blks, name):
    n = len(blks)

    def body(*refs):
        begin, end = _gather_parts(refs[:n], refs[n:2 * n], *refs[2 * n:])
        begin()
        end()

    return _pcall(body, name=name, out_shape=[jax.ShapeDtypeStruct((N_DEV,) + b.shape, b.dtype) for b in blks],
                  in_specs=[_HBM] * n, out_specs=[_HBM] * n, scratch_shapes=_sem_shapes(7 * n, n))(*blks)


def _sum_and_adamw(p_ref, w, m, v):
    g = p_ref[0].astype(F32)
    for s in range(1, N_DEV):
        g = g + p_ref[s].astype(F32)
    nm = ADAM_B1 * m + (1.0 - ADAM_B1) * g
    nv = ADAM_B2 * v + (1.0 - ADAM_B2) * (g * g)
    m_hat = nm / (1.0 - ADAM_B1 ** ADAM_STEP)
    v_hat = nv / (1.0 - ADAM_B2 ** ADAM_STEP)
    return g, -ADAM_LR * (m_hat / (jnp.sqrt(v_hat) + ADAM_EPS) + ADAM_WD * w), nm, nv


def _reduce_adamw_vectors(parts, w, m, v, shapes, name):
    table = _row_table(shapes)

    def body(p_ref, w_ref, m_ref, v_ref, *out_refs):
        results = _sum_and_adamw(p_ref, w_ref[...], m_ref[...], v_ref[...])
        for kind, val in enumerate(results):
            for t, (r0, nr, nc) in enumerate(table):
                out_refs[kind * len(table) + t][...] = val[r0:r0 + nr, 0:nc]

    outs = _pcall(body, name=name, out_shape=[jax.ShapeDtypeStruct((nr, nc), F32) for _, nr, nc in table] * 4)(parts, w, m, v)
    return [[outs[kind * len(table) + t].reshape(s) for t, s in enumerate(shapes)] for kind in range(4)]


def _reduce_adamw(parts, w, m, v, name):
    lead = w.shape[:-2]
    r, c = w.shape[-2:]
    tr = 128 if r % 128 == 0 else r
    zeros = (0,) * len(lead)
    at = zeros + (slice(None), slice(None))

    def body(p_ref, w_ref, m_ref, v_ref, g_ref, d_ref, nm_ref, nv_ref):
        g_ref[at], d_ref[at], nm_ref[at], nv_ref[at] = _sum_and_adamw(p_ref, w_ref[at], m_ref[at], v_ref[at])

    row = pl.BlockSpec((1,) * len(lead) + (tr, c), lambda i: zeros + (i, 0))
    out = jax.ShapeDtypeStruct(w.shape, F32)
    return _pcall(
        body, name=name, grid=(r // tr,),
        in_specs=[pl.BlockSpec((N_DEV, tr, c), lambda i: (0, i, 0)), row, row, row],
        out_specs=[row, row, row, row], out_shape=[out, out, out, out], compiler_params=_cp(1),
    )(parts, w, m, v)


def _proj_fwd(x, nw, w, wab, name, gather=()):
    t, npj = x.shape[0], w.shape[1]
    tm, tn = min(512, t), 1024
    has_ab = wab is not None
    n_in, n_out, n_g = 3 + has_ab, 2 + has_ab, len(gather)
    n_i = t // tm

    def body(*refs):
        ins, g_ins = refs[:n_in], refs[n_in:n_in + n_g]
        outs = refs[n_in + n_g:n_in + n_g + n_out]
        g_outs = refs[n_in + n_g + n_out:n_in + 2 * n_g + n_out]
        if has_ab:
            (x_ref, nw_ref, w_ref, wab_ref), (p_ref, h_ref, ab_ref) = ins, outs
        else:
            (x_ref, nw_ref, w_ref), (p_ref, h_ref) = ins, outs
        if n_g:
            begin, end = _gather_parts(g_ins, g_outs, *refs[n_in + 2 * n_g + n_out:])
            pl.when(pl.program_id(0) == 0)(begin)

        xv = x_ref[...]
        hv = (xv * lax.rsqrt(jnp.mean(xv * xv, axis=-1, keepdims=True) + EPS) * nw_ref[...]).astype(MXU)
        h_ref[...] = hv.astype(ACT)
        if has_ab:
            ab_ref[...] = _dot(hv, wab_ref[...], NN)
        for j in range(npj // tn):
            p_ref[:, j * tn:(j + 1) * tn] = _dot(hv, w_ref[:, j * tn:(j + 1) * tn], NN).astype(ACT)
        if n_g:
            pl.when(pl.program_id(0) == n_i - 1)(end)

    in_specs = [pl.BlockSpec((tm, D), lambda i: (i, 0)), _full((1, D)), _full((D, npj))]
    out_specs = [pl.BlockSpec((tm, npj), lambda i: (i, 0)), pl.BlockSpec((tm, D), lambda i: (i, 0))]
    out_shape = [jax.ShapeDtypeStruct((t, npj), ACT), jax.ShapeDtypeStruct((t, D), ACT)]
    args = [x, nw, w]
    if has_ab:
        in_specs.append(_full((D, HEAD)))
        out_specs.append(pl.BlockSpec((tm, HEAD), lambda i: (i, 0)))
        out_shape.append(jax.ShapeDtypeStruct((t, HEAD), F32))
        args.append(wab)
    scratch = []
    if n_g:
        in_specs += [_HBM] * n_g
        out_specs += [_HBM] * n_g
        out_shape += [jax.ShapeDtypeStruct((N_DEV,) + b.shape, b.dtype) for b in gather]
        args += list(gather)
        scratch += _sem_shapes(7 * n_g, n_g)
    return _pcall(body, name=name, grid=(n_i,), in_specs=in_specs, out_specs=out_specs,
                  out_shape=out_shape, scratch_shapes=scratch, compiler_params=_cp(1))(*args)


def _proj_bwd_x(dps, ws, x, dxn, nw, name, exchange=(), bcast=()):
    t = x.shape[0]
    tm = min(512, t)
    n, n_x, n_b = len(dps), len(exchange), len(bcast)
    n_c, n_steps = n_x + n_b, t // tm

    def body(*refs):
        dp_refs, w_refs = refs[:n], refs[n:2 * n]
        x_ref, dxn_ref, nw_ref = refs[2 * n:2 * n + 3]
        c_in = refs[2 * n + 3:2 * n + 3 + n_c]
        dx_ref, dnw_ref = refs[2 * n + 3 + n_c:2 * n + 5 + n_c]
        c_out = refs[2 * n + 5 + n_c:2 * n + 5 + 2 * n_c]
        if n_c:
            begin, end = _exchange_parts(c_in[:n_x], c_out[:n_x], c_in[n_x:], c_out[n_x:], *refs[2 * n + 5 + 2 * n_c:])
            pl.when(pl.program_id(0) == 0)(begin)
        @pl.when(pl.program_id(0) == 0)
        def _():
            dnw_ref[...] = jnp.zeros_like(dnw_ref)

        halves = [slice(r0, r0 + tm // 2) for r0 in (0, tm // 2)]
        dhs = []
        for rows in halves:
            dh = _mdot(dp_refs[0][rows, :], w_refs[0][...], NT)
            for k in range(1, n):
                dh = dh + _mdot(dp_refs[k][rows, :], w_refs[k][...], NT)
            dhs.append(dh)
        for rows, dh in zip(halves, dhs):
            xv = x_ref[rows, :]
            r = lax.rsqrt(jnp.mean(xv * xv, axis=-1, keepdims=True) + EPS)
            dnw_ref[...] += jnp.sum(dh * xv * r, axis=0, keepdims=True)
            dhw = dh * nw_ref[...]
            dx_ref[rows, :] = dxn_ref[rows, :] + r * (dhw - xv * (r * r) * jnp.mean(dhw * xv, axis=-1, keepdims=True))
        if n_c:
            pl.when(pl.program_id(0) == n_steps - 1)(end)

    row = pl.BlockSpec((tm, D), lambda i: (i, 0))
    in_specs = [pl.BlockSpec((tm, dp.shape[1]), lambda i: (i, 0)) for dp in dps]
    in_specs += [_full(w.shape) for w in ws]
    in_specs += [row, row, _full((1, D))] + [_HBM] * n_c
    out_shape = [jax.ShapeDtypeStruct((t, D), F32), jax.ShapeDtypeStruct((1, D), F32)]
    out_shape += [jax.ShapeDtypeStruct(e.shape, e.dtype) for e in exchange]
    out_shape += [jax.ShapeDtypeStruct((N_DEV,) + b.shape, b.dtype) for b in bcast]
    return _pcall(body, name=name, grid=(n_steps,), in_specs=in_specs, out_specs=[row, _full((1, D))] + [_HBM] * n_c,
                  out_shape=out_shape, scratch_shapes=_sem_shapes(7 * n_c, n_c) if n_c else [],
                  compiler_params=_cp(1))(*dps, *ws, x, dxn, nw, *exchange, *bcast)


def _matmul_tn(a, b, name, out_dtype=XCH):
    t, m = a.shape
    n = b.shape[1]
    tm, tn = min(1024, t), min(1024, n)
    n_t = t // tm

    def body(a_ref, b_ref, o_ref, acc):
        @pl.when(pl.program_id(1) == 0)
        def _():
            acc[...] = jnp.zeros_like(acc)

        acc[...] += _mdot(a_ref[...], b_ref[...], TN)

        @pl.when(pl.program_id(1) == n_t - 1)
        def _():
            o_ref[...] = acc[...].astype(out_dtype)

    return _pcall(body, name=name, grid=(n // tn, n_t),
                  in_specs=[pl.BlockSpec((tm, m), lambda j, i: (i, 0)), pl.BlockSpec((tm, tn), lambda j, i: (i, j))],
                  out_specs=pl.BlockSpec((m, tn), lambda j, i: (0, j)),
                  out_shape=jax.ShapeDtypeStruct((m, n), out_dtype), scratch_shapes=[pltpu.VMEM((m, tn), F32)],
                  compiler_params=_cp(2))(a, b)


def _memkv_fwd(mem, mw, wkv):
    def body(mem_ref, mw_ref, w_ref, kv_ref):
        mv = mem_ref[...]
        mn = mv * lax.rsqrt(jnp.mean(mv * mv, axis=-1, keepdims=True) + EPS) * mw_ref[...]
        kv_ref[...] = _mdot(mn, w_ref[...], NN)

    return _pcall(body, name="memkv_fwd", out_shape=jax.ShapeDtypeStruct((N_MEM, 2 * D_XA), F32),
                  compiler_params=pltpu.CompilerParams(vmem_limit_bytes=VMEM_LIMIT))(mem, mw, wkv)


def _memkv_bwd(mem, mw, wkv, dkv):
    def body(mem_ref, mw_ref, w_ref, dkv_ref, dw_ref, dmw_ref):
        mv = mem_ref[...]
        r = lax.rsqrt(jnp.mean(mv * mv, axis=-1, keepdims=True) + EPS)
        mn = mv * r * mw_ref[...]
        dkvv = dkv_ref[...]
        dw_ref[...] = _mdot(mn, dkvv, TN)
        dmn = _mdot(dkvv, w_ref[...], NT)
        dmw_ref[...] = jnp.sum(dmn * mv * r, axis=0, keepdims=True)

    return _pcall(body, name="memkv_bwd",
                  out_shape=[jax.ShapeDtypeStruct((D, 2 * D_XA), F32), jax.ShapeDtypeStruct((1, D), F32)],
                  compiler_params=pltpu.CompilerParams(vmem_limit_bytes=VMEM_LIMIT))(mem, mw, wkv, dkv)


def _attend(q, kv):
    heads = range(XA_HEADS)
    qs = [q[:, h * HEAD:(h + 1) * HEAD] for h in heads]
    ks = [kv[:, h * HEAD:(h + 1) * HEAD] for h in heads]
    vs = [kv[:, D_XA + h * HEAD:D_XA + (h + 1) * HEAD] for h in heads]
    ss = [_mdot(qs[h], ks[h], NT) * (HEAD ** -0.5) for h in heads]
    es = [jnp.exp(s - jnp.max(s, axis=-1, keepdims=True)) for s in ss]
    ps = [e / jnp.sum(e, axis=-1, keepdims=True) for e in es]
    return ps, [_mdot(ps[h], vs[h], NN) for h in heads]


def _tail_fwd(p, ymix, x, kv, wout, npost, name):
    t = x.shape[0]
    tm = min(512, t)

    def body(z_ref, q_ref, y_ref, x_ref, kv_ref, w_ref, np_ref, xn_ref, o_ref):
        _, outs = _attend(q_ref[...], kv_ref[...])
        cat = jnp.concatenate([y_ref[...]] + [a.astype(ACT) for a in outs], axis=1)
        g = cat * _silu(z_ref[...])
        o = _mdot(g, w_ref[...], NN)
        o_ref[...] = o
        xn_ref[...] = x_ref[...] + o * lax.rsqrt(jnp.mean(o * o, axis=-1, keepdims=True) + EPS) * np_ref[...]

    row = pl.BlockSpec((tm, D), lambda i: (i, 0))
    return _pcall(
        body, name=name, grid=(t // tm,),
        in_specs=[pl.BlockSpec((tm, D_CAT), lambda i: (i, 0)), pl.BlockSpec((tm, D_XA), lambda i: (i, D_CAT // D_XA)),
                  row, row, _full((N_MEM, 2 * D_XA)), _full((D_CAT, D)), _full((1, D))],
        out_specs=[row, row],
        out_shape=[jax.ShapeDtypeStruct((t, D), F32), jax.ShapeDtypeStruct((t, D), F32)], compiler_params=_cp(1),
    )(p, p, ymix, x, kv, wout, npost)


def _tail_bwd(dxn, o, p, ymix, kv, wout, npost, name, exchange=()):
    t = dxn.shape[0]
    tm = min(512, t)
    n_x, n_steps = len(exchange), t // tm

    def body(*refs):
        dxn_ref, o_ref, z_ref, q_ref, y_ref, kv_ref, w_ref, np_ref = refs[:8]
        dzq_ref, dy_ref, dw_ref, dnp_ref, dkv_ref = refs[8 + n_x:13 + n_x]
        if n_x:
            begin, end = _exchange_parts(refs[8:8 + n_x], refs[13 + n_x:13 + 2 * n_x], (), (), *refs[13 + 2 * n_x:])
            pl.when(pl.program_id(0) == 0)(begin)

        @pl.when(pl.program_id(0) == 0)
        def _():
            dw_ref[...] = jnp.zeros_like(dw_ref)
            dnp_ref[...] = jnp.zeros_like(dnp_ref)
            dkv_ref[...] = jnp.zeros_like(dkv_ref)

        q = q_ref[...]
        kvv = kv_ref[...]
        z = z_ref[...]
        ps, outs = _attend(q, kvv)
        cat = jnp.concatenate([y_ref[...]] + [a.astype(ACT) for a in outs], axis=1)
        sz = _silu(z)
        g = cat * sz
        ov = o_ref[...]
        dr = dxn_ref[...]
        rr = lax.rsqrt(jnp.mean(ov * ov, axis=-1, keepdims=True) + EPS)
        dnp_ref[...] += jnp.sum(dr * ov * rr, axis=0, keepdims=True)
        dow = dr * np_ref[...]
        do = rr * (dow - ov * (rr * rr) * jnp.mean(dow * ov, axis=-1, keepdims=True))
        dg = _mdot(do, w_ref[...], NT).astype(ACT)
        dw_ref[...] += _mdot(g, do, TN)
        dcat = dg * sz
        dzq_ref[:, 0:D_CAT] = dg * cat * _dsilu(z)
        dy_ref[...] = dcat[:, 0:D]
        heads = range(XA_HEADS)
        dohs = [dcat[:, D + h * HEAD:D + (h + 1) * HEAD] for h in heads]
        dps = [_mdot(dohs[h], kvv[:, D_XA + h * HEAD:D_XA + (h + 1) * HEAD], NT) for h in heads]
        dss = [ps[h] * (dps[h] - jnp.sum(dps[h] * ps[h], axis=-1, keepdims=True)) for h in heads]
        dqs = [_mdot(dss[h], kvv[:, h * HEAD:(h + 1) * HEAD], NN) * (HEAD ** -0.5) for h in heads]
        dks = [_mdot(dss[h], q[:, h * HEAD:(h + 1) * HEAD], TN) * (HEAD ** -0.5) for h in heads]
        dvs = [_mdot(ps[h], dohs[h], TN) for h in heads]
        for h in heads:
            lo = h * HEAD
            dzq_ref[:, D_CAT + lo:D_CAT + lo + HEAD] = dqs[h].astype(ACT)
            dkv_ref[:, lo:lo + HEAD] += dks[h]
            dkv_ref[:, D_XA + lo:D_XA + lo + HEAD] += dvs[h]
        if n_x:
            pl.when(pl.program_id(0) == n_steps - 1)(end)

    row = pl.BlockSpec((tm, D), lambda i: (i, 0))
    return _pcall(
        body, name=name, grid=(n_steps,),
        in_specs=[row, row, pl.BlockSpec((tm, D_CAT), lambda i: (i, 0)),
                  pl.BlockSpec((tm, D_XA), lambda i: (i, D_CAT // D_XA)), row,
                  _full((N_MEM, 2 * D_XA)), _full((D_CAT, D)), _full((1, D))] + [_HBM] * n_x,
        out_specs=[pl.BlockSpec((tm, ZQ), lambda i: (i, 0)), row, _full((D_CAT, D)), _full((1, D)),
                   _full((N_MEM, 2 * D_XA))] + [_HBM] * n_x,
        out_shape=[jax.ShapeDtypeStruct((t, ZQ), ACT), jax.ShapeDtypeStruct((t, D), ACT),
                   jax.ShapeDtypeStruct((D_CAT, D), F32), jax.ShapeDtypeStruct((1, D), F32),
                   jax.ShapeDtypeStruct((N_MEM, 2 * D_XA), F32)]
        + [jax.ShapeDtypeStruct(e.shape, e.dtype) for e in exchange],
        scratch_shapes=_sem_shapes(7 * n_x, n_x) if n_x else [], compiler_params=_cp(1),
    )(dxn, o, p, p, ymix, kv, wout, npost, *exchange)


def _gmlp_chunk(us, vs, lnws, lnbs, wss, bss):
    gv = [jax.nn.gelu(v) for v in vs]
    mean = sum(jnp.sum(v, axis=-1, keepdims=True) for v in gv) / D
    cen = [v - mean for v in gv]
    var = sum(jnp.sum(c * c, axis=-1, keepdims=True) for c in cen) / D
    rstd = lax.rsqrt(var + EPS)
    row = lax.broadcasted_iota(jnp.int32, (GM_CHUNK, GM_CHUNK), 0)
    col = lax.broadcasted_iota(jnp.int32, (GM_CHUNK, GM_CHUNK), 1)
    ys = []
    for g in range(GM_GROUPS):
        vn = cen[g] * rstd * lnws[g] + lnbs[g]
        sp = _nn(jnp.where(row >= col, wss[g], 0.0), vn) + bss[g]
        ys.append(jax.nn.gelu(us[g]) * sp)
    return ys


def _split_cols(v, n, width=HEAD):
    return [v[:, k * width:(k + 1) * width] for k in range(n)]


def _gmlp_operands(u_ref, v_ref, lnw_ref, lnb_ref, ws_ref, bst_ref, r0):
    us = _split_cols(u_ref[r0:r0 + GM_CHUNK, :].astype(F32), GM_GROUPS)
    vs = _split_cols(v_ref[r0:r0 + GM_CHUNK, :].astype(F32), GM_GROUPS)
    lnws = _split_cols(lnw_ref[...], GM_GROUPS)
    lnbs = _split_cols(lnb_ref[...], GM_GROUPS)
    wss = [ws_ref[g] for g in range(GM_GROUPS)]
    bst = bst_ref[...]
    bss = [jnp.sum(bst * _onehot_row(HEAD, g), axis=1, keepdims=True) for g in range(GM_GROUPS)]
    return us, vs, lnws, lnbs, wss, bss


def _gmlp_rows(t):
    return min(GM_CHUNKS_PER_STEP, t // GM_CHUNK) * GM_CHUNK


def _gmlp_specs(rows):
    return [pl.BlockSpec((rows, D), lambda i: (i, ZQ // D)), pl.BlockSpec((rows, D), lambda i: (i, ZQ // D + 1)),
            _full((1, D)), _full((1, D)), _full((GM_GROUPS, GM_CHUNK, GM_CHUNK)), _full((GM_CHUNK, HEAD))]


def _gmlp_fwd(p, lnw, lnb, ws, bst, name):
    t = p.shape[0]
    rows = _gmlp_rows(t)

    def body(u_ref, v_ref, lnw_ref, lnb_ref, ws_ref, bst_ref, y_ref):
        for r0 in range(0, rows, GM_CHUNK):
            ys = _gmlp_chunk(*_gmlp_operands(u_ref, v_ref, lnw_ref, lnb_ref, ws_ref, bst_ref, r0))
            for g in range(GM_GROUPS):
                y_ref[r0:r0 + GM_CHUNK, g * HEAD:(g + 1) * HEAD] = ys[g].astype(ACT)

    return _pcall(body, name=name, grid=(t // rows,), in_specs=_gmlp_specs(rows),
                  out_specs=pl.BlockSpec((rows, D), lambda i: (i, 0)),
                  out_shape=jax.ShapeDtypeStruct((t, D), ACT), compiler_params=_cp(1))(p, p, lnw, lnb, ws, bst)


def _gmlp_bwd(p, dy, lnw, lnb, ws, bst, name, exchange=(), bcast=()):
    t = p.shape[0]
    rows = _gmlp_rows(t)

    def body(u_ref, v_ref, lnw_ref, lnb_ref, ws_ref, bst_ref, dy_ref, duv_ref, dlnw_ref, dlnb_ref, dws_ref, dbst_ref):
        @pl.when(pl.program_id(0) == 0)
        def _():
            dlnw_ref[...] = jnp.zeros_like(dlnw_ref)
            dlnb_ref[...] = jnp.zeros_like(dlnb_ref)
            dws_ref[...] = jnp.zeros_like(dws_ref)
            dbst_ref[...] = jnp.zeros_like(dbst_ref)

        for r0 in range(0, rows, GM_CHUNK):
            ops = _gmlp_operands(u_ref, v_ref, lnw_ref, lnb_ref, ws_ref, bst_ref, r0)
            _, vjp = jax.vjp(_gmlp_chunk, *ops)
            dus, dvs, dlnws, dlnbs, dwss, dbss = vjp(_split_cols(dy_ref[r0:r0 + GM_CHUNK, :].astype(F32), GM_GROUPS))
            dbst = jnp.zeros((GM_CHUNK, HEAD), F32)
            for g in range(GM_GROUPS):
                lo = g * HEAD
                duv_ref[r0:r0 + GM_CHUNK, lo:lo + HEAD] = dus[g].astype(ACT)
                duv_ref[r0:r0 + GM_CHUNK, D + lo:D + lo + HEAD] = dvs[g].astype(ACT)
                dlnw_ref[:, lo:lo + HEAD] += dlnws[g]
                dlnb_ref[:, lo:lo + HEAD] += dlnbs[g]
                dws_ref[g] += dwss[g]
                dbst = dbst + dbss[g] * _onehot_row(HEAD, g)
            dbst_ref[...] += dbst

    call = dict(
        name=name, grid=(t // rows,),
        in_specs=_gmlp_specs(rows) + [pl.BlockSpec((rows, D), lambda i: (i, 0))],
        out_specs=[pl.BlockSpec((rows, 2 * D), lambda i: (i, 0)), _full((1, D)), _full((1, D)),
                   _full((GM_GROUPS, GM_CHUNK, GM_CHUNK)), _full((GM_CHUNK, HEAD))],
        out_shape=[jax.ShapeDtypeStruct((t, 2 * D), ACT), jax.ShapeDtypeStruct((1, D), F32),
                   jax.ShapeDtypeStruct((1, D), F32), jax.ShapeDtypeStruct((GM_GROUPS, GM_CHUNK, GM_CHUNK), F32),
                   jax.ShapeDtypeStruct((GM_CHUNK, HEAD), F32)],
        compiler_params=_cp(1))
    args = (p, p, lnw, lnb, ws, bst, dy)
    if not exchange and not bcast:
        return _pcall(body, **call)(*args), ()
    return _pcall_hosting(body, args, exchange, bcast, **call)


def _prev_halo(tm, col):
    return pl.BlockSpec((HALO, D), lambda i: (jnp.maximum(i * (tm // HALO) - 1, 0), col))


def _next_halo(tm, col, n_tiles):
    return pl.BlockSpec((HALO, D), lambda i: (jnp.minimum(i + 1, n_tiles - 1) * (tm // HALO), col))


def _taps_back(ext, w, width):
    acc = None
    for k in range(width):
        s = width - 1 - k
        term = w[k:k + 1, :] * (pltpu.roll(ext, s, 0) if s else ext)[HALO:, :]
        acc = term if acc is None else acc + term
    return acc


def _taps_fwd(ext, w, width, n):
    rows = ext.shape[0]
    acc = None
    for k in range(width):
        s = width - 1 - k
        term = w[k:k + 1, :] * (pltpu.roll(ext, rows - s, 0) if s else ext)[0:n, :]
        acc = term if acc is None else acc + term
    return acc


def _sconv_fwd(p, cw, name):
    t = p.shape[0]
    tm = min(256, t)

    def body(b_ref, c_ref, h_ref, cp_ref, hp_ref, w_ref, y_ref):
        first = pl.program_id(0) == 0
        prev = jnp.where(first, 0.0, cp_ref[...].astype(F32) * hp_ref[...].astype(F32))
        ext = jnp.concatenate([prev, c_ref[...].astype(F32) * h_ref[...].astype(F32)], axis=0)
        y_ref[...] = (b_ref[...].astype(F32) * _taps_back(ext, w_ref[...], 3)).astype(ACT)

    c0 = ZQ // D
    tile = [pl.BlockSpec((tm, D), lambda i, c=c: (i, c)) for c in (c0, c0 + 1, c0 + 2)]
    return _pcall(body, name=name, grid=(t // tm,),
                  in_specs=tile + [_prev_halo(tm, c0 + 1), _prev_halo(tm, c0 + 2), _full((HALO, D))],
                  out_specs=pl.BlockSpec((tm, D), lambda i: (i, 0)),
                  out_shape=jax.ShapeDtypeStruct((t, D), ACT), compiler_params=_cp(1))(p, p, p, p, p, cw)


def _sconv_bwd(p, dy, cw, name):
    t = p.shape[0]
    tm = min(256, t)
    n_tiles = t // tm

    def body(b_ref, c_ref, h_ref, cp_ref, hp_ref, bn_ref, dy_ref, dyn_ref, w_ref, d_ref, dw_ref):
        i = pl.program_id(0)

        @pl.when(i == 0)
        def _():
            dw_ref[...] = jnp.zeros_like(dw_ref)

        w = w_ref[...]
        bv, cv, hv = b_ref[...].astype(F32), c_ref[...].astype(F32), h_ref[...].astype(F32)
        dyv = dy_ref[...].astype(F32)
        prev = jnp.where(i == 0, 0.0, cp_ref[...].astype(F32) * hp_ref[...].astype(F32))
        ext = jnp.concatenate([prev, cv * hv], axis=0)
        conv = _taps_back(ext, w, 3)
        dconv = dyv * bv
        nxt = jnp.where(i == n_tiles - 1, 0.0, dyn_ref[...].astype(F32) * bn_ref[...].astype(F32))
        dc = _taps_fwd(jnp.concatenate([dconv, nxt], axis=0), w, 3, tm)
        d_ref[:, 0:D] = (dyv * conv).astype(ACT)
        d_ref[:, D:2 * D] = (dc * hv).astype(ACT)
        d_ref[:, 2 * D:3 * D] = (dc * cv).astype(ACT)
        for k in range(3):
            s = 2 - k
            shifted = (pltpu.roll(ext, s, 0) if s else ext)[HALO:, :]
            dw_ref[k:k + 1, :] += jnp.sum(dconv * shifted, axis=0, keepdims=True)

    c0 = ZQ // D
    tile = [pl.BlockSpec((tm, D), lambda i, c=c: (i, c)) for c in (c0, c0 + 1, c0 + 2)]
    return _pcall(
        body, name=name, grid=(n_tiles,),
        in_specs=tile + [_prev_halo(tm, c0 + 1), _prev_halo(tm, c0 + 2), _next_halo(tm, c0, n_tiles),
                         pl.BlockSpec((tm, D), lambda i: (i, 0)), _next_halo(tm, 0, n_tiles), _full((HALO, D))],
        out_specs=[pl.BlockSpec((tm, 3 * D), lambda i: (i, 0)), _full((HALO, D))],
        out_shape=[jax.ShapeDtypeStruct((t, 3 * D), ACT), jax.ShapeDtypeStruct((HALO, D), F32)],
        compiler_params=_cp(1),
    )(p, p, p, p, p, p, dy, dy, cw)


def _l2_heads(s, scale):
    outs, rs = [], []
    for hh in range(DN_HEADS):
        blk = s[:, hh * HEAD:(hh + 1) * HEAD]
        r = lax.rsqrt(jnp.sum(blk * blk, axis=-1, keepdims=True) + EPS)
        outs.append(blk * (r * scale))
        rs.append(r)
    return outs, rs


_QKV_SCALE = (HEAD ** -0.5, 1.0, None)


def _qkv_fwd(p, cw, name):
    t = p.shape[0]
    tm = min(256, t)

    def body(q_ref, k_ref, v_ref, qp_ref, kp_ref, vp_ref, w_ref, o_ref):
        first = pl.program_id(0) == 0
        for part, (ref, pref) in enumerate(((q_ref, qp_ref), (k_ref, kp_ref), (v_ref, vp_ref))):
            prev = jnp.where(first, 0.0, pref[...].astype(F32))
            ext = jnp.concatenate([prev, ref[...].astype(F32)], axis=0)
            s = _silu(_taps_back(ext, w_ref[:, part * D:(part + 1) * D], 4))
            if _QKV_SCALE[part] is None:
                o_ref[:, part * D:(part + 1) * D] = s.astype(ACT)
            else:
                outs, _ = _l2_heads(s, _QKV_SCALE[part])
                for hh in range(DN_HEADS):
                    o_ref[:, part * D + hh * HEAD:part * D + (hh + 1) * HEAD] = outs[hh].astype(ACT)

    c0 = ZQ // D
    tile = [pl.BlockSpec((tm, D), lambda i, c=c: (i, c)) for c in (c0, c0 + 1, c0 + 2)]
    halo = [_prev_halo(tm, c) for c in (c0, c0 + 1, c0 + 2)]
    return _pcall(body, name=name, grid=(t // tm,), in_specs=tile + halo + [_full((HALO, 3 * D))],
                  out_specs=pl.BlockSpec((tm, 3 * D), lambda i: (i, 0)),
                  out_shape=jax.ShapeDtypeStruct((t, 3 * D), ACT), compiler_params=_cp(1))(p, p, p, p, p, p, cw)


def _qkv_bwd(p, dqkv, cw, name):
    t = p.shape[0]
    tm = min(256, t)
    n_tiles = t // tm

    def body(*refs):
        tiles, prevs, nexts = refs[0:3], refs[3:6], refs[6:9]
        d_tiles, d_nexts = refs[9:12], refs[12:15]
        w_ref, o_ref, dw_ref = refs[15:]
        i = pl.program_id(0)

        @pl.when(i == 0)
        def _():
            dw_ref[...] = jnp.zeros_like(dw_ref)

        for part in range(3):
            w = w_ref[:, part * D:(part + 1) * D]
            prev = jnp.where(i == 0, 0.0, prevs[part][...].astype(F32))
            ext = jnp.concatenate([prev, tiles[part][...].astype(F32), nexts[part][...].astype(F32)], axis=0)
            xc = _taps_back(ext, w, 4)
            dout = jnp.concatenate([d_tiles[part][...].astype(F32), d_nexts[part][...].astype(F32)], axis=0)
            s = _silu(xc)
            if _QKV_SCALE[part] is None:
                ds = dout
            else:
                scale = _QKV_SCALE[part]
                pieces = []
                for hh in range(DN_HEADS):
                    blk = s[:, hh * HEAD:(hh + 1) * HEAD]
                    dblk = dout[:, hh * HEAD:(hh + 1) * HEAD]
                    r = lax.rsqrt(jnp.sum(blk * blk, axis=-1, keepdims=True) + EPS)
                    pieces.append(scale * r * (dblk - blk * (r * r) * jnp.sum(dblk * blk, axis=-1, keepdims=True)))
                ds = jnp.concatenate(pieces, axis=1)
            dxc = ds * _dsilu(xc)
            row = lax.broadcasted_iota(jnp.int32, (tm + HALO, 1), 0)
            dxc = jnp.where(jnp.logical_and(i == n_tiles - 1, row >= tm), 0.0, dxc)
            o_ref[:, part * D:(part + 1) * D] = _taps_fwd(dxc, w, 4, tm).astype(ACT)
            for k in range(4):
                sh = 3 - k
                shifted = (pltpu.roll(ext, sh, 0) if sh else ext)[HALO:HALO + tm, :]
                dw_ref[k:k + 1, part * D:(part + 1) * D] += jnp.sum(dxc[0:tm, :] * shifted, axis=0, keepdims=True)

    c0 = ZQ // D
    cols = (c0, c0 + 1, c0 + 2)
    tile = [pl.BlockSpec((tm, D), lambda i, c=c: (i, c)) for c in cols]
    dtile = [pl.BlockSpec((tm, D), lambda i, c=c: (i, c)) for c in range(3)]
    in_specs = (tile + [_prev_halo(tm, c) for c in cols] + [_next_halo(tm, c, n_tiles) for c in cols]
                + dtile + [_next_halo(tm, c, n_tiles) for c in range(3)] + [_full((HALO, 3 * D))])
    return _pcall(
        body, name=name, grid=(n_tiles,), in_specs=in_specs,
        out_specs=[pl.BlockSpec((tm, 3 * D), lambda i: (i, 0)), _full((HALO, 3 * D))],
        out_shape=[jax.ShapeDtypeStruct((t, 3 * D), ACT), jax.ShapeDtypeStruct((HALO, 3 * D), F32)],
        compiler_params=_cp(1),
    )(*([p] * 9), *([dqkv] * 6), cw)


def _tri_masks(n):
    row = lax.broadcasted_iota(jnp.int32, (n, n), 0)
    col = lax.broadcasted_iota(jnp.int32, (n, n), 1)
    return row, col


@jax.custom_vjp
def _unit_lower_inverses(mats):
    n = DN_CHUNK
    row, col = _tri_masks(n)
    eye = (row == col).astype(F32)
    same16 = (row // 16) == (col // 16)
    same32 = (row // 32) == (col // 32)
    pw = [jnp.where(same16, a, 0.0) for a in mats]
    x = [eye - p for p in pw]
    for _ in range(3):
        pw = [_hnn(p, p) for p in pw]
        x = [_hnn(xi, eye + p) for xi, p in zip(x, pw)]
    for keep in (jnp.logical_and(same32, jnp.logical_not(same16)), jnp.logical_not(same32)):
        inner = [_hnn(jnp.where(keep, a, 0.0), xi) for a, xi in zip(mats, x)]
        x = [xi - _hnn(xi, y) for xi, y in zip(x, inner)]
    return tuple(x)


def _uli_fwd(mats):
    t = _unit_lower_inverses(mats)
    return t, t


def _uli_bwd(ts, gs):
    inner = [_nt(g, t) for g, t in zip(gs, ts)]
    return (tuple(-_tn(t, y) for t, y in zip(ts, inner)),)


_unit_lower_inverses.defvjp(_uli_fwd, _uli_bwd)


@jax.custom_vjp
def _known_inverses(mats, ts):
    return ts


_known_inverses.defvjp(lambda mats, ts: (ts, ts),
                       lambda ts, gs: (_uli_bwd(ts, gs)[0], tuple(jnp.zeros_like(t) for t in ts)))


def _pick_col(m, k):
    return jnp.sum(m * _onehot_row(m.shape[1], k), axis=1, keepdims=True)


def _pick_row(m, k):
    hot = (lax.broadcasted_iota(jnp.int32, (m.shape[0], 1), 0) == k).astype(F32)
    return jnp.sum(m * hot, axis=0, keepdims=True)


def _delta_chunk(states, qs, ks, vs, ab, alog, dtb, onw, known_inverses=None):
    n = DN_CHUNK
    heads = range(DN_HEADS)
    row, col = _tri_masks(n)
    incl = row >= col
    lane = lax.broadcasted_iota(jnp.int32, (1, HEAD), 1)
    g_all = jnp.where(lane < DN_HEADS, -jnp.exp(alog) * jax.nn.softplus(ab + dtb), 0.0)
    c_cols = _hnn(incl.astype(F32), g_all)
    c_rows = _htn(g_all, (row <= col).astype(F32))
    g_tot = jnp.sum(g_all, axis=0, keepdims=True)
    beta_all = jax.nn.sigmoid(ab)
    ccol = [_pick_col(c_cols, h) for h in heads]
    crow = [_pick_row(c_rows, h) for h in heads]
    gl = [_pick_col(g_tot, h) for h in heads]
    beta = [_pick_col(beta_all, DN_HEADS + h) for h in heads]
    decay = [jnp.exp(jnp.where(incl, ccol[h] - crow[h], -1e30)) for h in heads]
    eg = [jnp.exp(ccol[h]) for h in heads]
    kb = [ks[h] * beta[h] for h in heads]
    amat = [jnp.where(row > col, _nt(kb[h], ks[h]) * decay[h], 0.0) for h in heads]
    if known_inverses is None:
        tmat = _unit_lower_inverses(tuple(amat))
    else:
        tmat = _known_inverses(tuple(amat), tuple(known_inverses))
    u = [_nn(tmat[h], vs[h] * beta[h]) for h in heads]
    w = [_nn(tmat[h], kb[h] * eg[h]) for h in heads]
    qk = [_nt(qs[h], ks[h]) * decay[h] for h in heads]
    v_new = [u[h] - _nn(w[h], states[h]) for h in heads]
    o = [_nn(qs[h] * eg[h], states[h]) + _nn(qk[h], v_new[h]) for h in heads]
    new_states = [states[h] * jnp.exp(gl[h]) + _tn(ks[h] * jnp.exp(gl[h] - ccol[h]), v_new[h]) for h in heads]
    ys = [o[h] * lax.rsqrt(jnp.mean(o[h] * o[h], axis=-1, keepdims=True) + EPS) * onw for h in heads]
    return (ys, new_states), tmat


def _head_cols(ref, r0):
    return [ref[r0:r0 + DN_CHUNK, h * HEAD:(h + 1) * HEAD].astype(F32) for h in range(DN_HEADS)]


def _delta_rows(t):
    return min(DN_CHUNKS_PER_STEP, t // DN_CHUNK) * DN_CHUNK


def _delta_fwd(qkv, ab, alog, dtb, onw, name):
    t = qkv.shape[0]
    rows = _delta_rows(t)
    per_step = rows // DN_CHUNK

    def body(q_ref, k_ref, v_ref, ab_ref, alog_ref, dtb_ref, onw_ref, y_ref, keep_ref, inv_ref, state):
        @pl.when(pl.program_id(0) == 0)
        def _():
            state[...] = jnp.zeros_like(state)

        s = [state[hh] for hh in range(DN_HEADS)]
        for c in range(per_step):
            r0 = c * DN_CHUNK
            (ys, s1), tmat = _delta_chunk(s, _head_cols(q_ref, r0), _head_cols(k_ref, r0), _head_cols(v_ref, r0),
                                          ab_ref[r0:r0 + DN_CHUNK, :], alog_ref[...], dtb_ref[...], onw_ref[...])
            for hh in range(DN_HEADS):
                keep_ref[c, hh] = s[hh]
                inv_ref[c, hh] = tmat[hh]
                y_ref[r0:r0 + DN_CHUNK, hh * HEAD:(hh + 1) * HEAD] = ys[hh].astype(ACT)
            s = s1
        for hh in range(DN_HEADS):
            state[hh] = s[hh]

    block = [pl.BlockSpec((rows, D), lambda i, c=c: (i, c)) for c in range(3)]
    nc = t // DN_CHUNK
    return _pcall(
        body, name=name, grid=(t // rows,),
        in_specs=block + [pl.BlockSpec((rows, HEAD), lambda i: (i, 0)), _full((1, HEAD)), _full((1, HEAD)),
                          _full((1, HEAD))],
        out_specs=[pl.BlockSpec((rows, D), lambda i: (i, 0)),
                   pl.BlockSpec((per_step, DN_HEADS, HEAD, HEAD), lambda i: (i, 0, 0, 0)),
                   pl.BlockSpec((per_step, DN_HEADS, DN_CHUNK, DN_CHUNK), lambda i: (i, 0, 0, 0))],
        out_shape=[jax.ShapeDtypeStruct((t, D), ACT), jax.ShapeDtypeStruct((nc, DN_HEADS, HEAD, HEAD), F32),
                   jax.ShapeDtypeStruct((nc, DN_HEADS, DN_CHUNK, DN_CHUNK), F32)],
        scratch_shapes=[pltpu.VMEM((DN_HEADS, HEAD, HEAD), F32)], compiler_params=_cp(1),
    )(qkv, qkv, qkv, ab, alog, dtb, onw)


def _delta_bwd(qkv, ab, alog, dtb, onw, keep, inv, dy, name):
    t = qkv.shape[0]
    rows = _delta_rows(t)
    per_step, n_steps = rows // DN_CHUNK, t // rows

    def body(q_ref, k_ref, v_ref, ab_ref, alog_ref, dtb_ref, onw_ref, keep_ref, inv_ref, dy_ref,
             dqkv_ref, dab_ref, dalog_ref, ddtb_ref, donw_ref, dstate):
        @pl.when(pl.program_id(0) == 0)
        def _():
            dstate[...] = jnp.zeros_like(dstate)
            dalog_ref[...] = jnp.zeros_like(dalog_ref)
            ddtb_ref[...] = jnp.zeros_like(ddtb_ref)
            donw_ref[...] = jnp.zeros_like(donw_ref)

        ds = [dstate[hh] for hh in range(DN_HEADS)]
        for c in reversed(range(per_step)):
            r0 = c * DN_CHUNK
            s0 = [keep_ref[c, hh] for hh in range(DN_HEADS)]
            known = [inv_ref[c, hh] for hh in range(DN_HEADS)]
            _, vjp, _ = jax.vjp(functools.partial(_delta_chunk, known_inverses=known), s0, _head_cols(q_ref, r0),
                                _head_cols(k_ref, r0), _head_cols(v_ref, r0), ab_ref[r0:r0 + DN_CHUNK, :], alog_ref[...],
                                dtb_ref[...], onw_ref[...], has_aux=True)
            ds, dq, dk, dv, dab, dal, ddt, don = vjp((_head_cols(dy_ref, r0), ds))
            for hh in range(DN_HEADS):
                lo = hh * HEAD
                dqkv_ref[r0:r0 + DN_CHUNK, lo:lo + HEAD] = dq[hh].astype(ACT)
                dqkv_ref[r0:r0 + DN_CHUNK, D + lo:D + lo + HEAD] = dk[hh].astype(ACT)
                dqkv_ref[r0:r0 + DN_CHUNK, 2 * D + lo:2 * D + lo + HEAD] = dv[hh].astype(ACT)
            dab_ref[r0:r0 + DN_CHUNK, :] = dab
            dalog_ref[...] += dal
            ddtb_ref[...] += ddt
            donw_ref[...] += don
        for hh in range(DN_HEADS):
            dstate[hh] = ds[hh]

    rev = lambda i: n_steps - 1 - i
    block = [pl.BlockSpec((rows, D), lambda i, c=c: (rev(i), c)) for c in range(3)]
    small = jax.ShapeDtypeStruct((1, HEAD), F32)
    return _pcall(
        body, name=name, grid=(n_steps,),
        in_specs=block + [pl.BlockSpec((rows, HEAD), lambda i: (rev(i), 0)), _full((1, HEAD)), _full((1, HEAD)),
                          _full((1, HEAD)), pl.BlockSpec((per_step, DN_HEADS, HEAD, HEAD), lambda i: (rev(i), 0, 0, 0)),
                          pl.BlockSpec((per_step, DN_HEADS, DN_CHUNK, DN_CHUNK), lambda i: (rev(i), 0, 0, 0)),
                          pl.BlockSpec((rows, D), lambda i: (rev(i), 0))],
        out_specs=[pl.BlockSpec((rows, 3 * D), lambda i: (rev(i), 0)),
                   pl.BlockSpec((rows, HEAD), lambda i: (rev(i), 0)), _full((1, HEAD)), _full((1, HEAD)),
                   _full((1, HEAD))],
        out_shape=[jax.ShapeDtypeStruct((t, 3 * D), ACT), jax.ShapeDtypeStruct((t, HEAD), F32), small, small, small],
        scratch_shapes=[pltpu.VMEM((DN_HEADS, HEAD, HEAD), F32)], compiler_params=_cp(1),
    )(qkv, qkv, qkv, ab, alog, dtb, onw, keep, inv, dy)


def _loss_head(y, target):
    t = y.shape[0]
    tm = min(512, t)

    def body(y_ref, t_ref, l_ref, dy_ref):
        @pl.when(pl.program_id(0) == 0)
        def _():
            l_ref[...] = jnp.zeros_like(l_ref)

        diff = y_ref[...] - t_ref[...]
        dy_ref[...] = diff * (1.0 / D)
        l_ref[...] += 0.5 * jnp.sum(jnp.sum(diff * diff, axis=-1, keepdims=True) * (1.0 / D), axis=0, keepdims=True)

    row = pl.BlockSpec((tm, D), lambda i: (i, 0))
    return _pcall(body, name="loss_head", grid=(t // tm,), in_specs=[row, row], out_specs=[_full((8, 128)), row],
                  out_shape=[jax.ShapeDtypeStruct((8, 128), F32), jax.ShapeDtypeStruct((t, D), F32)],
                  compiler_params=_cp(1))(y, target)


_SMALL_SHARDED = (("a_ln_w", (2, 128), 1), ("a_ln_b", (2, 128), 1), ("b_conv_w", (1, 3, 128), 2),
                  ("c_conv_w", (1, 4, 384), 2))
_REPLICATED = (("mem_norm_w", (1024,)), ("norm_pre", (4, 1024)), ("norm_post", (4, 1024)),
               ("a_b_s", (2, 8, 128)), ("c_a_log", (1, 8)), ("c_dt_bias", (1, 8)), ("c_o_norm_w", (1, 128)))


def _layer_shards(given, prefix):
    w_out = given[prefix + "w_out"]
    w_ins = [given[prefix + "a_w_in"][0], given[prefix + "b_w_in"], given[prefix + "c_w_in"], given[prefix + "a_w_in"][1]]
    return [[w_out[i], w_ins[i]] for i in range(4)]


def _as_matmul_operand(w):
    return w.reshape(w.shape[-2:]).astype(MXU)


def _as_rows(shape):
    return (math.prod(shape[:-1]) if len(shape) > 1 else 1, shape[-1])


def _row_table(shapes):
    table, at = [], 0
    for shape in shapes:
        rows, cols = _as_rows(shape)
        table.append((at, rows, cols))
        at += rows
    return table


def _pack_rows(arrays, shapes, rows, lead=()):
    n_lead = len(lead)
    parts = [jnp.pad(a.reshape(lead + _as_rows(s)), [(0, 0)] * (n_lead + 1) + [(0, D - s[-1])])
             for a, s in zip(arrays, shapes)]
    block = jnp.concatenate(parts, axis=n_lead)
    return jnp.pad(block, [(0, 0)] * n_lead + [(0, rows - block.shape[n_lead]), (0, 0)])


def _unpack_rows(packed, shapes, lead=()):
    n_lead = len(lead)
    return [packed[(slice(None),) * n_lead + (slice(r0, r0 + nr), slice(0, nc))].reshape(lead + s)
            for (r0, nr, nc), s in zip(_row_table(shapes), shapes)]


def _join_shards(blocks, axis):
    moved = jnp.moveaxis(blocks, 0, axis)
    shape = moved.shape
    return moved.reshape(shape[:axis] + (shape[axis] * shape[axis + 1],) + shape[axis + 2:])


def _split_shards(full, axis):
    shape = full.shape
    split = full.reshape(shape[:axis] + (N_DEV, shape[axis] // N_DEV) + shape[axis + 1:])
    return jnp.moveaxis(split, axis, 0)


_A_COLS = ((2560, 1536), (2048, 512), (0, 2048))
_BC_COLS = ((3584, 1536), (3072, 512), (0, 3072))
_C_COLS = ((3600, 1536), (3088, 512), (0, 3072))


def _reorder_cols(w, cols):
    return jnp.concatenate([w[:, s:s + n] for s, n in cols], axis=1)


def _restore_cols(pieces_in_my_order, cols, extra=()):
    placed = sorted(list(zip([s for s, _ in cols], pieces_in_my_order)) + list(extra), key=lambda sp: sp[0])
    return jnp.concatenate([piece for _, piece in placed], axis=1)


def kernel(x, mem, mem_norm_w, w_mem_kv, norm_pre, norm_post, w_out, a_w_in, a_ln_w, a_ln_b, a_w_s, a_b_s, b_w_in, b_conv_w, c_w_in, c_conv_w, c_a_log, c_dt_bias, c_o_norm_w, loss_target, m_mem_norm_w, m_w_mem_kv, m_norm_pre, m_norm_post, m_w_out, m_a_w_in, m_a_ln_w, m_a_ln_b, m_a_w_s, m_a_b_s, m_b_w_in, m_b_conv_w, m_c_w_in, m_c_conv_w, m_c_a_log, m_c_dt_bias, m_c_o_norm_w, v_mem_norm_w, v_w_mem_kv, v_norm_pre, v_norm_post, v_w_out, v_a_w_in, v_a_ln_w, v_a_ln_b, v_a_w_s, v_a_b_s, v_b_w_in, v_b_conv_w, v_c_w_in, v_c_conv_w, v_c_a_log, v_c_dt_bias, v_c_o_norm_w):
    given = dict(locals())
    x0 = x[0]
    mem0 = mem[0]
    target = loss_target[0]

    w_sh, m_sh, v_sh = (_layer_shards(given, pre) for pre in ("", "m_", "v_"))
    small_names = [n for n, _, _ in _SMALL_SHARDED]
    small_shapes = [s for _, s, _ in _SMALL_SHARDED]
    repl_names = [n for n, _ in _REPLICATED]
    repl_shapes = [s for _, s in _REPLICATED]
    small = _pack_rows([given[n] for n in small_names], small_shapes, R_SMALL)
    g_wkv, g_wout0, g_in0, g_small = _all_gather(
        [w_mem_kv.astype(MXU)] + [_as_matmul_operand(w) for w in w_sh[0]] + [small], "gather_weights")
    small_full = _unpack_rows(g_small, small_shapes, lead=(N_DEV,))
    full = {n: _join_shards(blocks, ax) for (n, _, ax), blocks in zip(_SMALL_SHARDED, small_full)}
    wkv = g_wkv.reshape(D, 2 * D_XA)
    wouts = [g_wout0.reshape(D_CAT, D)]
    w_in = [_reorder_cols(_join_shards(g_in0, 1), _A_COLS)]
    w_cab = None
    lnw = [full["a_ln_w"][j][None, :] for j in range(2)]
    lnb = [full["a_ln_b"][j][None, :] for j in range(2)]
    ws = [a_w_s[j] for j in range(2)]
    bst = [jnp.pad(a_b_s[j].T, ((0, 0), (0, HEAD - GM_GROUPS))) for j in range(2)]
    cw_b = jnp.pad(full["b_conv_w"][0], ((0, HALO - 3), (0, 0)))
    cw_c = jnp.pad(full["c_conv_w"][0], ((0, HALO - 4), (0, 0)))
    alog = jnp.pad(c_a_log, ((0, 0), (0, HEAD - DN_HEADS)))
    dtb = jnp.pad(c_dt_bias, ((0, 0), (0, HEAD - DN_HEADS)))
    onw = c_o_norm_w
    mw = mem_norm_w[None, :]

    kv = _memkv_fwd(mem0, mw, wkv).astype(MXU)
    xs, saved = [x0], []
    for i in range(4):
        kind = i % 3
        npre, npost = norm_pre[i][None, :], norm_post[i][None, :]
        ahead = [_as_matmul_operand(w) for w in w_sh[i + 1]] if i < 3 else []
        res = _proj_fwd(xs[i], npre, w_in[i], w_cab if kind == 2 else None, f"proj_fwd_{i}", gather=ahead)
        if ahead:
            g_wout, g_in = res[-2:]
            wouts.append(g_wout.reshape(D_CAT, D))
            if i + 1 == 2:
                c_full = jnp.concatenate([g_in[d] for d in range(N_DEV)], axis=1)
                w_in.append(_reorder_cols(c_full, _C_COLS))
                w_cab = jnp.pad(c_full[:, 3072:3088], ((0, 0), (0, HEAD - 16)))
            else:
                w_in.append(_reorder_cols(_join_shards(g_in, 1), _BC_COLS if i + 1 == 1 else _A_COLS))
        if kind == 2:
            p, h, ab = res[:3]
            qkv = _qkv_fwd(p, cw_c, f"qkv_fwd_{i}")
            ymix, keep, inv = _delta_fwd(qkv, ab, alog, dtb, onw, f"delta_fwd_{i}")
            extra = (qkv, ab, keep, inv)
        else:
            p, h = res[:2]
            if kind == 0:
                ymix = _gmlp_fwd(p, lnw[i // 3], lnb[i // 3], ws[i // 3], bst[i // 3], f"gmlp_fwd_{i}")
            else:
                ymix = _sconv_fwd(p, cw_b, f"sconv_fwd_{i}")
            extra = ()
        xn, o = _tail_fwd(p, ymix, xs[i], kv, wouts[i], npost, f"tail_fwd_{i}")
        xs.append(xn)
        saved.append((p, h, ymix, o, extra))

    loss_tile, dx = _loss_head(xs[4], target)
    loss = lax.psum(loss_tile[0, 0], ("x", "y", "c"))

    g = {}
    d_npre, d_npost = [None] * 4, [None] * 4
    d_ws, d_bs, d_lnw, d_lnb = [None] * 2, [None] * 2, [None] * 2, [None] * 2
    dkv = None
    pend_win, pend_wout = None, None
    landed_win, landed_wout = [None] * 4, [None] * 4
    for i in reversed(range(4)):
        kind = i % 3
        p, h, ymix, o, extra = saved[i]
        npre, npost = norm_pre[i][None, :], norm_post[i][None, :]
        res = _tail_bwd(dx, o, p, ymix, kv, wouts[i], npost, f"tail_bwd_{i}", exchange=[pend_win] if i < 3 else [])
        dzq, dymix, d_wout, d_npost[i], dkv_i = res[:5]
        if i < 3:
            landed_win[i + 1] = res[5]
        d_wout = d_wout.reshape(N_DEV, D_CAT // N_DEV, D).astype(XCH)
        dkv = dkv_i if dkv is None else dkv + dkv_i
        w_zq, w_mix = w_in[i][:, :ZQ], w_in[i][:, ZQ:]
        dw_zq = _matmul_tn(h, dzq, f"dw_zq_{i}")
        if kind == 0:
            j = i // 3
            early_x, early_b = [], []
            if i == 0:
                d_wkv, d_mw = _memkv_bwd(mem0, mw, wkv, dkv)
                early_x = [d_wkv.reshape(N_DEV, 128, D).astype(XCH), d_wout, pend_wout]
                early_b = [d_ws[1].reshape(GM_CHUNK, D).astype(XCH)]
            (dmix, d_lnw[j], d_lnb[j], d_ws[j], dbst), early_landed = _gmlp_bwd(
                p, dymix, lnw[j], lnb[j], ws[j], bst[j], f"gmlp_bwd_{i}", exchange=early_x, bcast=early_b)
            d_bs[j] = dbst[:, :GM_GROUPS].T
            dw_mix = _matmul_tn(h, dmix, f"dw_mix_{i}")
            d_win = _restore_cols([dw_zq[:, :D_CAT], dw_zq[:, D_CAT:], dw_mix], _A_COLS)
            dps, wparts = [dzq, dmix], [w_zq, w_mix]
        elif kind == 1:
            dmix, dcw = _sconv_bwd(p, dymix, cw_b, f"sconv_bwd_{i}")
            g["b_conv_w"] = dcw[None, :3]
            dw_mix = _matmul_tn(h, dmix, f"dw_mix_{i}")
            d_win = _restore_cols([dw_zq[:, :D_CAT], dw_zq[:, D_CAT:], dw_mix], _BC_COLS)
            dps, wparts = [dzq, dmix], [w_zq, w_mix]
        else:
            qkv, ab, keep, inv = extra
            dqkv, dab, dalog, ddtb, donw = _delta_bwd(qkv, ab, alog, dtb, onw, keep, inv, dymix, f"delta_bwd_{i}")
            dmix, dcw = _qkv_bwd(p, dqkv, cw_c, f"qkv_bwd_{i}")
            g["c_conv_w"] = dcw[None, :4]
            g["c_a_log"], g["c_dt_bias"], g["c_o_norm_w"] = dalog[:, :DN_HEADS], ddtb[:, :DN_HEADS], donw
            dw_mix = _matmul_tn(h, dmix, f"dw_mix_{i}")
            dw_ab = _matmul_tn(h, dab, f"dw_ab_{i}")
            d_win = _restore_cols([dw_zq[:, :D_CAT], dw_zq[:, D_CAT:], dw_mix], _C_COLS, extra=[(3072, dw_ab[:, :16])])
            dps, wparts = [dzq, dmix, dab], [w_zq, w_mix, w_cab]
        width = d_win.shape[1] // N_DEV
        pend_win = jnp.stack([d_win[:, d * width:(d + 1) * width] for d in range(N_DEV)]).astype(XCH)
        if i > 0:
            res = _proj_bwd_x(dps, wparts, xs[i], dx, npre, f"proj_bwd_x_{i}", exchange=[pend_wout] if i < 3 else [])
            dx, d_npre[i] = res[:2]
            if i < 3:
                landed_wout[i + 1] = res[2]
            pend_wout = d_wout

    g["mem_norm_w"] = d_mw[0]
    g["norm_pre"] = jnp.concatenate([jnp.zeros((1, D), F32)] + d_npre[1:], axis=0)
    g["norm_post"] = jnp.concatenate(d_npost, axis=0)
    g["a_ln_w"] = jnp.concatenate(d_lnw, axis=0)
    g["a_ln_b"] = jnp.concatenate(d_lnb, axis=0)
    g["a_b_s"] = jnp.stack(d_bs)
    e_small = _pack_rows([_split_shards(g[n], ax) for n, _, ax in _SMALL_SHARDED], small_shapes, R_SMALL, lead=(N_DEV,))
    r_pack = _pack_rows([g[n] for n in repl_names], repl_shapes, R_REPL)
    dx, d_npre0, l_in, l_small, ws0_all, r_all = _proj_bwd_x(
        dps, wparts, xs[0], dx, norm_pre[0][None, :], "proj_bwd_x_0", exchange=[pend_win, e_small],
        bcast=[d_ws[0].reshape(GM_CHUNK, D).astype(XCH), r_pack])
    l_wkv, landed_wout[0], landed_wout[1], ws3_all = early_landed
    landed_win[0] = l_in
    npre0_all, = _all_gather([jnp.pad(d_npre0, ((0, HALO - 1), (0, 0)))], "gather_norm_pre0")
    r_all = r_all.at[:, 1, :].set(npre0_all[:, 0, :])

    res = [[_reduce_adamw(parts, w_sh[i][a], m_sh[i][a], v_sh[i][a], f"adamw_{i}_{a}")
            for a, parts in enumerate((landed_wout[i], landed_win[i]))] for i in range(4)]
    res_wkv = _reduce_adamw(l_wkv, w_mem_kv, m_w_mem_kv, v_w_mem_kv, "adamw_w_mem_kv")
    res_small = _reduce_adamw_vectors(
        l_small, small, *(_pack_rows([given[pre + n] for n in small_names], small_shapes, R_SMALL) for pre in ("m_", "v_")),
        small_shapes, "adamw_small")
    res_ws = [_reduce_adamw(parts, *(given[pre + "a_w_s"][j].reshape(GM_CHUNK, D) for pre in ("", "m_", "v_")),
                            f"adamw_a_w_s_{j}") for j, parts in enumerate((ws0_all, ws3_all))]
    res_repl = _reduce_adamw_vectors(
        r_all, *(_pack_rows([given[pre + n] for n in repl_names], repl_shapes, R_REPL) for pre in ("", "m_", "v_")),
        repl_shapes, "adamw_replicated")

    order = ["mem_norm_w", "w_mem_kv", "norm_pre", "norm_post", "w_out", "a_w_in", "a_ln_w", "a_ln_b", "a_w_s", "a_b_s",
             "b_w_in", "b_conv_w", "c_w_in", "c_conv_w", "c_a_log", "c_dt_bias", "c_o_norm_w"]
    outs = [loss, dx[None]]
    for kind in range(4):
        got = dict(zip(repl_names, res_repl[kind]))
        got["a_w_s"] = jnp.stack([r[kind] for r in res_ws]).reshape(a_w_s.shape)
        got.update(zip(small_names, res_small[kind]))
        got["w_mem_kv"] = res_wkv[kind]
        got["w_out"] = jnp.stack([res[i][0][kind] for i in range(4)])
        got["a_w_in"] = jnp.stack([res[0][1][kind], res[3][1][kind]])
        got["b_w_in"] = res[1][1][kind]
        got["c_w_in"] = res[2][1][kind]
        outs += [got[n] for n in order]
    return tuple(outs)
```

```python
import functools
import math

import jax
import jax.numpy as jnp
from jax import lax
from jax.experimental import pallas as pl
from jax.experimental.pallas import tpu as pltpu

F32 = jnp.float32
MXU = jnp.bfloat16
ACT = jnp.bfloat16

D = 1024
D_XA = 512
D_CAT = 1536
N_MEM = 256
XA_HEADS = 4
HEAD = 128
ZQ = D_CAT + D_XA
EPS = 1e-6
GM_CHUNK = 128
GM_CHUNKS_PER_STEP = 4
GM_GROUPS = 8
DN_HEADS = 8
DN_CHUNK = 64
DN_CHUNKS_PER_STEP = 4
N_DEV = 8
HALO = 8
VMEM_LIMIT = 56 * 1024 * 1024
XCH = jnp.bfloat16
R_REPL = 32
R_SMALL = 16

ADAM_LR = 0.001
ADAM_B1 = 0.9
ADAM_B2 = 0.999
ADAM_EPS = 1e-08
ADAM_WD = 0.01
ADAM_STEP = 10

NN = ((1,), (0,))
NT = ((1,), (1,))
TN = ((0,), (0,))
MESH = pl.DeviceIdType.MESH


def _pcall(body, **kw):
    return pl.pallas_call(body, **kw)


def _cp(n_axes):
    return pltpu.CompilerParams(dimension_semantics=("arbitrary",) * n_axes, vmem_limit_bytes=VMEM_LIMIT)


def _dot(a, b, dims, prec=None):
    return lax.dot_general(a, b, (dims, ((), ())), preferred_element_type=F32, precision=prec)


def _mdot(a, b, dims):
    return _dot(a.astype(MXU), b.astype(MXU), dims)


def _make_mms(raw):
    @jax.custom_vjp
    def nn(a, b):
        return raw(a, b, NN)

    @jax.custom_vjp
    def nt(a, b):
        return raw(a, b, NT)

    @jax.custom_vjp
    def tn(a, b):
        return raw(a, b, TN)

    nn.defvjp(lambda a, b: (nn(a, b), (a, b)), lambda r, g: (nt(g, r[1]), tn(r[0], g)))
    nt.defvjp(lambda a, b: (nt(a, b), (a, b)), lambda r, g: (nn(g, r[1]), tn(g, r[0])))
    tn.defvjp(lambda a, b: (tn(a, b), (a, b)), lambda r, g: (nt(r[1], g), nn(r[0], g)))
    return nn, nt, tn


_nn, _nt, _tn = _make_mms(_mdot)


def _split_dot(a, b, dims):
    ah = a.astype(jnp.bfloat16)
    bh = b.astype(jnp.bfloat16)
    al = (a - ah.astype(F32)).astype(jnp.bfloat16)
    bl = (b - bh.astype(F32)).astype(jnp.bfloat16)
    return _dot(ah, bh, dims) + (_dot(ah, bl, dims) + _dot(al, bh, dims))


_hnn, _hnt, _htn = _make_mms(_split_dot)


def _full(shape):
    return pl.BlockSpec(shape, lambda *_: (0,) * len(shape))


def _silu(z):
    return z * jax.nn.sigmoid(z)


def _dsilu(z):
    s = jax.nn.sigmoid(z)
    return s * (1.0 + z * (1.0 - s))


def _onehot_row(n, k):
    return (lax.broadcasted_iota(jnp.int32, (1, n), 1) == k).astype(F32)


_HBM = pl.BlockSpec(memory_space=pl.ANY)


def _sem_shapes(n_remote, n_local):
    return [pltpu.SemaphoreType.DMA((n_remote,)), pltpu.SemaphoreType.DMA((n_remote,)),
            pltpu.SemaphoreType.DMA((n_local,))]


def _gather_parts(x_refs, out_refs, send_sems, recv_sems, local_sems):
    n = len(x_refs)
    x, y, cc = lax.axis_index("x"), lax.axis_index("y"), lax.axis_index("c")
    me, sibling = (x, y, cc), (x, y, 1 - cc)
    chips = [(1 - x, y), (x, 1 - y), (1 - x, 1 - y)]

    def slot(a, px, py, pc):
        return out_refs[a].at[4 * px + 2 * py + pc]

    def copy(k, a, block, to, src=None):
        return pltpu.make_async_remote_copy(
            src_ref=slot(a, *block) if src is None else src, dst_ref=slot(a, *block),
            send_sem=send_sems.at[k * n + a], recv_sem=recv_sems.at[k * n + a], device_id=to, device_id_type=MESH)

    mine = [pltpu.make_async_copy(x_refs[a], slot(a, *me), local_sems.at[a]) for a in range(n)]
    first = [copy(0, a, me, sibling, src=x_refs[a]) for a in range(n)]
    first += [copy(1 + j, a, me, (*chip, cc), src=x_refs[a]) for j, chip in enumerate(chips) for a in range(n)]

    def begin():
        for cp in mine + first:
            cp.start()

    def end():
        passed = []
        for j, chip in enumerate(chips):
            for a in range(n):
                copy(1 + j, a, (*chip, cc), me).wait_recv()
                passed.append(copy(4 + j, a, (*chip, cc), sibling))
                passed[-1].start()
        for a in range(n):
            copy(0, a, sibling, me).wait_recv()
        for j, chip in enumerate(chips):
            for a in range(n):
                copy(4 + j, a, (*chip, 1 - cc), me).wait_recv()
        for cp in first + passed:
            cp.wait_send()
        for cp in mine:
            cp.wait()

    return begin, end


def _exchange_parts(g_refs, out_refs, b_refs, ball_refs, send_sems, recv_sems, local_sems):
    n, nb = len(g_refs), len(b_refs)
    per_peer = n + nb
    x, y, cc = lax.axis_index("x"), lax.axis_index("y"), lax.axis_index("c")
    my_idx = 4 * x + 2 * y + cc
    mine = [pltpu.make_async_copy(g_refs[a].at[my_idx], out_refs[a].at[my_idx], local_sems.at[a]) for a in range(n)]
    mine += [pltpu.make_async_copy(b_refs[a], ball_refs[a].at[my_idx], local_sems.at[n + a]) for a in range(nb)]
    copies = []
    for k in range(1, N_DEV):
        px = 1 - x if k & 4 else x
        py = 1 - y if k & 2 else y
        pc = 1 - cc if k & 1 else cc
        base = (k - 1) * per_peer
        for a in range(n):
            copies.append(pltpu.make_async_remote_copy(
                src_ref=g_refs[a].at[4 * px + 2 * py + pc], dst_ref=out_refs[a].at[my_idx],
                send_sem=send_sems.at[base + a], recv_sem=recv_sems.at[base + a],
                device_id=(px, py, pc), device_id_type=MESH))
        for a in range(nb):
            copies.append(pltpu.make_async_remote_copy(
                src_ref=b_refs[a], dst_ref=ball_refs[a].at[my_idx], send_sem=send_sems.at[base + n + a],
                recv_sem=recv_sems.at[base + n + a], device_id=(px, py, pc), device_id_type=MESH))

    def begin():
        for cp in mine + copies:
            cp.start()

    def end():
        for cp in copies:
            cp.wait_recv()
        for cp in copies:
            cp.wait_send()
        for cp in mine:
            cp.wait()

    return begin, end


def _pcall_hosting(body, args, exchange, bcast, *, name, grid, in_specs, out_specs, out_shape, compiler_params,
                   scratch_shapes=()):
    n_in, n_out, n_x, n_b = len(in_specs), len(out_specs), len(exchange), len(bcast)
    n_c, n_scr = n_x + n_b, len(scratch_shapes)

    def hosted(*refs):
        ins, c_in = refs[:n_in], refs[n_in:n_in + n_c]
        outs, c_out = refs[n_in + n_c:n_in + n_c + n_out], refs[n_in + n_c + n_out:n_in + 2 * n_c + n_out]
        scratch, sems = refs[n_in + 2 * n_c + n_out:][:n_scr], refs[n_in + 2 * n_c + n_out + n_scr:]
        begin, end = _exchange_parts(c_in[:n_x], c_out[:n_x], c_in[n_x:], c_out[n_x:], *sems)
        first = functools.reduce(jnp.logical_and, [pl.program_id(k) == 0 for k in range(len(grid))])
        last = functools.reduce(jnp.logical_and, [pl.program_id(k) == grid[k] - 1 for k in range(len(grid))])
        pl.when(first)(begin)
        body(*ins, *outs, *scratch)
        pl.when(last)(end)

    landed_shape = [jax.ShapeDtypeStruct(e.shape, e.dtype) for e in exchange]
    landed_shape += [jax.ShapeDtypeStruct((N_DEV,) + b.shape, b.dtype) for b in bcast]
    res = _pcall(hosted, name=name, grid=grid, in_specs=list(in_specs) + [_HBM] * n_c,
                 out_specs=list(out_specs) + [_HBM] * n_c, out_shape=list(out_shape) + landed_shape,
                 scratch_shapes=list(scratch_shapes) + _sem_shapes(7 * n_c, n_c),
                 compiler_params=compiler_params)(*args, *exchange, *bcast)
    return res[:n_out], res[n_out:]


def _all_gather(blks, name):
    n = len(blks)

    def body(*refs):
        begin, end = _gather_parts(refs[:n], refs[n:2 * n], *refs[2 * n:])
        begin()
        end()

    return _pcall(body, name=name, out_shape=[jax.ShapeDtypeStruct((N_DEV,) + b.shape, b.dtype) for b in blks],
                  in_specs=[_HBM] * n, out_specs=[_HBM] * n, scratch_shapes=_sem_shapes(7 * n, n))(*blks)


def _sum_and_adamw(p_ref, w, m, v):
    g = p_ref[0].astype(F32)
    for s in range(1, N_DEV):
        g = g + p_ref[s].astype(F32)
    nm = ADAM_B1 * m + (1.0 - ADAM_B1) * g
    nv = ADAM_B2 * v + (1.0 - ADAM_B2) * (g * g)
    m_hat = nm / (1.0 - ADAM_B1 ** ADAM_STEP)
    v_hat = nv / (1.0 - ADAM_B2 ** ADAM_STEP)
    return g, -ADAM_LR * (m_hat / (jnp.sqrt(v_hat) + ADAM_EPS) + ADAM_WD * w), nm, nv


def _reduce_adamw_vectors(parts, w, m, v, shapes, name):
    table = _row_table(shapes)

    def body(p_ref, w_ref, m_ref, v_ref, *out_refs):
        results = _sum_and_adamw(p_ref, w_ref[...], m_ref[...], v_ref[...])
        for kind, val in enumerate(results):
            for t, (r0, nr, nc) in enumerate(table):
                out_refs[kind * len(table) + t][...] = val[r0:r0 + nr, 0:nc]

    outs = _pcall(body, name=name, out_shape=[jax.ShapeDtypeStruct((nr, nc), F32) for _, nr, nc in table] * 4)(parts, w, m, v)
    return [[outs[kind * len(table) + t].reshape(s) for t, s in enumerate(shapes)] for kind in range(4)]


def _reduce_adamw(parts, w, m, v, name):
    lead = w.shape[:-2]
    r, c = w.shape[-2:]
    tr = 128 if r % 128 == 0 else r
    zeros = (0,) * len(lead)
    at = zeros + (slice(None), slice(None))

    def body(p_ref, w_ref, m_ref, v_ref, g_ref, d_ref, nm_ref, nv_ref):
        g_ref[at], d_ref[at], nm_ref[at], nv_ref[at] = _sum_and_adamw(p_ref, w_ref[at], m_ref[at], v_ref[at])

    row = pl.BlockSpec((1,) * len(lead) + (tr, c), lambda i: zeros + (i, 0))
    out = jax.ShapeDtypeStruct(w.shape, F32)
    return _pcall(
        body, name=name, grid=(r // tr,),
        in_specs=[pl.BlockSpec((N_DEV, tr, c), lambda i: (0, i, 0)), row, row, row],
        out_specs=[row, row, row, row], out_shape=[out, out, out, out], compiler_params=_cp(1),
    )(parts, w, m, v)


def _proj_fwd(x, nw, w, wab, name, gather=()):
    t, npj = x.shape[0], w.shape[1]
    tm, tn = min(512, t), 1024
    has_ab = wab is not None
    n_in, n_out, n_g = 3 + has_ab, 2 + has_ab, len(gather)
    n_i = t // tm

    def body(*refs):
        ins, g_ins = refs[:n_in], refs[n_in:n_in + n_g]
        outs = refs[n_in + n_g:n_in + n_g + n_out]
        g_outs = refs[n_in + n_g + n_out:n_in + 2 * n_g + n_out]
        if has_ab:
            (x_ref, nw_ref, w_ref, wab_ref), (p_ref, h_ref, ab_ref) = ins, outs
        else:
            (x_ref, nw_ref, w_ref), (p_ref, h_ref) = ins, outs
        if n_g:
            begin, end = _gather_parts(g_ins, g_outs, *refs[n_in + 2 * n_g + n_out:])
            pl.when(pl.program_id(0) == 0)(begin)

        xv = x_ref[...]
        hv = (xv * lax.rsqrt(jnp.mean(xv * xv, axis=-1, keepdims=True) + EPS) * nw_ref[...]).astype(MXU)
        h_ref[...] = hv.astype(ACT)
        if has_ab:
            ab_ref[...] = _dot(hv, wab_ref[...], NN)
        for j in range(npj // tn):
            p_ref[:, j * tn:(j + 1) * tn] = _dot(hv, w_ref[:, j * tn:(j + 1) * tn], NN).astype(ACT)
        if n_g:
            pl.when(pl.program_id(0) == n_i - 1)(end)

    in_specs = [pl.BlockSpec((tm, D), lambda i: (i, 0)), _full((1, D)), _full((D, npj))]
    out_specs = [pl.BlockSpec((tm, npj), lambda i: (i, 0)), pl.BlockSpec((tm, D), lambda i: (i, 0))]
    out_shape = [jax.ShapeDtypeStruct((t, npj), ACT), jax.ShapeDtypeStruct((t, D), ACT)]
    args = [x, nw, w]
    if has_ab:
        in_specs.append(_full((D, HEAD)))
        out_specs.append(pl.BlockSpec((tm, HEAD), lambda i: (i, 0)))
        out_shape.append(jax.ShapeDtypeStruct((t, HEAD), F32))
        args.append(wab)
    scratch = []
    if n_g:
        in_specs += [_HBM] * n_g
        out_specs += [_HBM] * n_g
        out_shape += [jax.ShapeDtypeStruct((N_DEV,) + b.shape, b.dtype) for b in gather]
        args += list(gather)
        scratch += _sem_shapes(7 * n_g, n_g)
    return _pcall(body, name=name, grid=(n_i,), in_specs=in_specs, out_specs=out_specs,
                  out_shape=out_shape, scratch_shapes=scratch, compiler_params=_cp(1))(*args)


def _proj_bwd_x(dps, ws, x, dxn, nw, name, exchange=(), bcast=()):
    t = x.shape[0]
    tm = min(512, t)
    n, n_x, n_b = len(dps), len(exchange), len(bcast)
    n_c, n_steps = n_x + n_b, t // tm

    def body(*refs):
        dp_refs, w_refs = refs[:n], refs[n:2 * n]
        x_ref, dxn_ref, nw_ref = refs[2 * n:2 * n + 3]
        c_in = refs[2 * n + 3:2 * n + 3 + n_c]
        dx_ref, dnw_ref = refs[2 * n + 3 + n_c:2 * n + 5 + n_c]
        c_out = refs[2 * n + 5 + n_c:2 * n + 5 + 2 * n_c]
        if n_c:
            begin, end = _exchange_parts(c_in[:n_x], c_out[:n_x], c_in[n_x:], c_out[n_x:], *refs[2 * n + 5 + 2 * n_c:])
            pl.when(pl.program_id(0) == 0)(begin)
        @pl.when(pl.program_id(0) == 0)
        def _():
            dnw_ref[...] = jnp.zeros_like(dnw_ref)

        halves = [slice(r0, r0 + tm // 2) for r0 in (0, tm // 2)]
        dhs = []
        for rows in halves:
            dh = _mdot(dp_refs[0][rows, :], w_refs[0][...], NT)
            for k in range(1, n):
                dh = dh + _mdot(dp_refs[k][rows, :], w_refs[k][...], NT)
            dhs.append(dh)
        for rows, dh in zip(halves, dhs):
            xv = x_ref[rows, :]
            r = lax.rsqrt(jnp.mean(xv * xv, axis=-1, keepdims=True) + EPS)
            dnw_ref[...] += jnp.sum(dh * xv * r, axis=0, keepdims=True)
            dhw = dh * nw_ref[...]
            dx_ref[rows, :] = dxn_ref[rows, :] + r * (dhw - xv * (r * r) * jnp.mean(dhw * xv, axis=-1, keepdims=True))
        if n_c:
            pl.when(pl.program_id(0) == n_steps - 1)(end)

    row = pl.BlockSpec((tm, D), lambda i: (i, 0))
    in_specs = [pl.BlockSpec((tm, dp.shape[1]), lambda i: (i, 0)) for dp in dps]
    in_specs += [_full(w.shape) for w in ws]
    in_specs += [row, row, _full((1, D))] + [_HBM] * n_c
    out_shape = [jax.ShapeDtypeStruct((t, D), F32), jax.ShapeDtypeStruct((1, D), F32)]
    out_shape += [jax.ShapeDtypeStruct(e.shape, e.dtype) for e in exchange]
    out_shape += [jax.ShapeDtypeStruct((N_DEV,) + b.shape, b.dtype) for b in bcast]
    return _pcall(body, name=name, grid=(n_steps,), in_specs=in_specs, out_specs=[row, _full((1, D))] + [_HBM] * n_c,
                  out_shape=out_shape, scratch_shapes=_sem_shapes(7 * n_c, n_c) if n_c else [],
                  compiler_params=_cp(1))(*dps, *ws, x, dxn, nw, *exchange, *bcast)


def _matmul_tn(a, b, name, out_dtype=XCH, exchange=(), bcast=()):
    t, m = a.shape
    n = b.shape[1]
    tm, tn = min(1024, t), min(1024, n)
    n_t = t // tm

    def body(a_ref, b_ref, o_ref, acc):
        @pl.when(pl.program_id(1) == 0)
        def _():
            acc[...] = jnp.zeros_like(acc)

        acc[...] += _mdot(a_ref[...], b_ref[...], TN)

        @pl.when(pl.program_id(1) == n_t - 1)
        def _():
            o_ref[...] = acc[...].astype(out_dtype)

    call = dict(name=name, grid=(n // tn, n_t),
                in_specs=[pl.BlockSpec((tm, m), lambda j, i: (i, 0)), pl.BlockSpec((tm, tn), lambda j, i: (i, j))],
                out_specs=[pl.BlockSpec((m, tn), lambda j, i: (0, j))],
                out_shape=[jax.ShapeDtypeStruct((m, n), out_dtype)], scratch_shapes=[pltpu.VMEM((m, tn), F32)],
                compiler_params=_cp(2))
    if not exchange and not bcast:
        return _pcall(body, **call)(a, b)[0]
    (out,), landed = _pcall_hosting(body, (a, b), exchange, bcast, **call)
    return out, landed


def _memkv_fwd(mem, mw, wkv):
    def body(mem_ref, mw_ref, w_ref, kv_ref):
        mv = mem_ref[...]
        mn = mv * lax.rsqrt(jnp.mean(mv * mv, axis=-1, keepdims=True) + EPS) * mw_ref[...]
        kv_ref[...] = _mdot(mn, w_ref[...], NN)

    return _pcall(body, name="memkv_fwd", out_shape=jax.ShapeDtypeStruct((N_MEM, 2 * D_XA), F32),
                  compiler_params=pltpu.CompilerParams(vmem_limit_bytes=VMEM_LIMIT))(mem, mw, wkv)


def _memkv_bwd(mem, mw, wkv, dkv):
    def body(mem_ref, mw_ref, w_ref, dkv_ref, dw_ref, dmw_ref):
        mv = mem_ref[...]
        r = lax.rsqrt(jnp.mean(mv * mv, axis=-1, keepdims=True) + EPS)
        mn = mv * r * mw_ref[...]
        dkvv = dkv_ref[...]
        dw_ref[...] = _mdot(mn, dkvv, TN)
        dmn = _mdot(dkvv, w_ref[...], NT)
        dmw_ref[...] = jnp.sum(dmn * mv * r, axis=0, keepdims=True)

    return _pcall(body, name="memkv_bwd",
                  out_shape=[jax.ShapeDtypeStruct((D, 2 * D_XA), F32), jax.ShapeDtypeStruct((1, D), F32)],
                  compiler_params=pltpu.CompilerParams(vmem_limit_bytes=VMEM_LIMIT))(mem, mw, wkv, dkv)


def _attend(q, kv):
    heads = range(XA_HEADS)
    qs = [q[:, h * HEAD:(h + 1) * HEAD] for h in heads]
    ks = [kv[:, h * HEAD:(h + 1) * HEAD] for h in heads]
    vs = [kv[:, D_XA + h * HEAD:D_XA + (h + 1) * HEAD] for h in heads]
    ss = [_mdot(qs[h], ks[h], NT) * (HEAD ** -0.5) for h in heads]
    es = [jnp.exp(s - jnp.max(s, axis=-1, keepdims=True)) for s in ss]
    ps = [e / jnp.sum(e, axis=-1, keepdims=True) for e in es]
    return ps, [_mdot(ps[h], vs[h], NN) for h in heads]


def _tail_fwd(p, ymix, x, kv, wout, npost, name):
    t = x.shape[0]
    tm = min(512, t)

    def body(z_ref, q_ref, y_ref, x_ref, kv_ref, w_ref, np_ref, xn_ref, o_ref):
        _, outs = _attend(q_ref[...], kv_ref[...])
        cat = jnp.concatenate([y_ref[...]] + [a.astype(ACT) for a in outs], axis=1)
        g = cat * _silu(z_ref[...])
        o = _mdot(g, w_ref[...], NN)
        o_ref[...] = o
        xn_ref[...] = x_ref[...] + o * lax.rsqrt(jnp.mean(o * o, axis=-1, keepdims=True) + EPS) * np_ref[...]

    row = pl.BlockSpec((tm, D), lambda i: (i, 0))
    return _pcall(
        body, name=name, grid=(t // tm,),
        in_specs=[pl.BlockSpec((tm, D_CAT), lambda i: (i, 0)), pl.BlockSpec((tm, D_XA), lambda i: (i, D_CAT // D_XA)),
                  row, row, _full((N_MEM, 2 * D_XA)), _full((D_CAT, D)), _full((1, D))],
        out_specs=[row, row],
        out_shape=[jax.ShapeDtypeStruct((t, D), F32), jax.ShapeDtypeStruct((t, D), F32)], compiler_params=_cp(1),
    )(p, p, ymix, x, kv, wout, npost)


def _tail_bwd(dxn, o, p, ymix, kv, wout, npost, name, exchange=()):
    t = dxn.shape[0]
    tm = min(512, t)
    n_x, n_steps = len(exchange), t // tm

    def body(*refs):
        dxn_ref, o_ref, z_ref, q_ref, y_ref, kv_ref, w_ref, np_ref = refs[:8]
        dzq_ref, dy_ref, dw_ref, dnp_ref, dkv_ref = refs[8 + n_x:13 + n_x]
        if n_x:
            begin, end = _exchange_parts(refs[8:8 + n_x], refs[13 + n_x:13 + 2 * n_x], (), (), *refs[13 + 2 * n_x:])
            pl.when(pl.program_id(0) == 0)(begin)

        @pl.when(pl.program_id(0) == 0)
        def _():
            dw_ref[...] = jnp.zeros_like(dw_ref)
            dnp_ref[...] = jnp.zeros_like(dnp_ref)
            dkv_ref[...] = jnp.zeros_like(dkv_ref)

        q = q_ref[...]
        kvv = kv_ref[...]
        z = z_ref[...]
        ps, outs = _attend(q, kvv)
        cat = jnp.concatenate([y_ref[...]] + [a.astype(ACT) for a in outs], axis=1)
        sz = _silu(z)
        g = cat * sz
        ov = o_ref[...]
        dr = dxn_ref[...]
        rr = lax.rsqrt(jnp.mean(ov * ov, axis=-1, keepdims=True) + EPS)
        dnp_ref[...] += jnp.sum(dr * ov * rr, axis=0, keepdims=True)
        dow = dr * np_ref[...]
        do = rr * (dow - ov * (rr * rr) * jnp.mean(dow * ov, axis=-1, keepdims=True))
        dg = _mdot(do, w_ref[...], NT).astype(ACT)
        dw_ref[...] += _mdot(g, do, TN)
        dcat = dg * sz
        dzq_ref[:, 0:D_CAT] = dg * cat * _dsilu(z)
        dy_ref[...] = dcat[:, 0:D]
        heads = range(XA_HEADS)
        dohs = [dcat[:, D + h * HEAD:D + (h + 1) * HEAD] for h in heads]
        dps = [_mdot(dohs[h], kvv[:, D_XA + h * HEAD:D_XA + (h + 1) * HEAD], NT) for h in heads]
        dss = [ps[h] * (dps[h] - jnp.sum(dps[h] * ps[h], axis=-1, keepdims=True)) for h in heads]
        dqs = [_mdot(dss[h], kvv[:, h * HEAD:(h + 1) * HEAD], NN) * (HEAD ** -0.5) for h in heads]
        dks = [_mdot(dss[h], q[:, h * HEAD:(h + 1) * HEAD], TN) * (HEAD ** -0.5) for h in heads]
        dvs = [_mdot(ps[h], dohs[h], TN) for h in heads]
        for h in heads:
            lo = h * HEAD
            dzq_ref[:, D_CAT + lo:D_CAT + lo + HEAD] = dqs[h].astype(ACT)
            dkv_ref[:, lo:lo + HEAD] += dks[h]
            dkv_ref[:, D_XA + lo:D_XA + lo + HEAD] += dvs[h]
        if n_x:
            pl.when(pl.program_id(0) == n_steps - 1)(end)

    row = pl.BlockSpec((tm, D), lambda i: (i, 0))
    return _pcall(
        body, name=name, grid=(n_steps,),
        in_specs=[row, row, pl.BlockSpec((tm, D_CAT), lambda i: (i, 0)),
                  pl.BlockSpec((tm, D_XA), lambda i: (i, D_CAT // D_XA)), row,
                  _full((N_MEM, 2 * D_XA)), _full((D_CAT, D)), _full((1, D))] + [_HBM] * n_x,
        out_specs=[pl.BlockSpec((tm, ZQ), lambda i: (i, 0)), row, _full((D_CAT, D)), _full((1, D)),
                   _full((N_MEM, 2 * D_XA))] + [_HBM] * n_x,
        out_shape=[jax.ShapeDtypeStruct((t, ZQ), ACT), jax.ShapeDtypeStruct((t, D), ACT),
                   jax.ShapeDtypeStruct((D_CAT, D), F32), jax.ShapeDtypeStruct((1, D), F32),
                   jax.ShapeDtypeStruct((N_MEM, 2 * D_XA), F32)]
        + [jax.ShapeDtypeStruct(e.shape, e.dtype) for e in exchange],
        scratch_shapes=_sem_shapes(7 * n_x, n_x) if n_x else [], compiler_params=_cp(1),
    )(dxn, o, p, p, ymix, kv, wout, npost, *exchange)


def _gmlp_chunk(us, vs, lnws, lnbs, wss, bss):
    gv = [jax.nn.gelu(v) for v in vs]
    mean = sum(jnp.sum(v, axis=-1, keepdims=True) for v in gv) / D
    cen = [v - mean for v in gv]
    var = sum(jnp.sum(c * c, axis=-1, keepdims=True) for c in cen) / D
    rstd = lax.rsqrt(var + EPS)
    row = lax.broadcasted_iota(jnp.int32, (GM_CHUNK, GM_CHUNK), 0)
    col = lax.broadcasted_iota(jnp.int32, (GM_CHUNK, GM_CHUNK), 1)
    ys = []
    for g in range(GM_GROUPS):
        vn = cen[g] * rstd * lnws[g] + lnbs[g]
        sp = _nn(jnp.where(row >= col, wss[g], 0.0), vn) + bss[g]
        ys.append(jax.nn.gelu(us[g]) * sp)
    return ys


def _split_cols(v, n, width=HEAD):
    return [v[:, k * width:(k + 1) * width] for k in range(n)]


def _gmlp_operands(u_ref, v_ref, lnw_ref, lnb_ref, ws_ref, bst_ref, r0):
    us = _split_cols(u_ref[r0:r0 + GM_CHUNK, :].astype(F32), GM_GROUPS)
    vs = _split_cols(v_ref[r0:r0 + GM_CHUNK, :].astype(F32), GM_GROUPS)
    lnws = _split_cols(lnw_ref[...], GM_GROUPS)
    lnbs = _split_cols(lnb_ref[...], GM_GROUPS)
    wss = [ws_ref[g] for g in range(GM_GROUPS)]
    bst = bst_ref[...]
    bss = [jnp.sum(bst * _onehot_row(HEAD, g), axis=1, keepdims=True) for g in range(GM_GROUPS)]
    return us, vs, lnws, lnbs, wss, bss


def _gmlp_rows(t):
    return min(GM_CHUNKS_PER_STEP, t // GM_CHUNK) * GM_CHUNK


def _gmlp_specs(rows):
    return [pl.BlockSpec((rows, D), lambda i: (i, ZQ // D)), pl.BlockSpec((rows, D), lambda i: (i, ZQ // D + 1)),
            _full((1, D)), _full((1, D)), _full((GM_GROUPS, GM_CHUNK, GM_CHUNK)), _full((GM_CHUNK, HEAD))]


def _gmlp_fwd(p, lnw, lnb, ws, bst, name):
    t = p.shape[0]
    rows = _gmlp_rows(t)

    def body(u_ref, v_ref, lnw_ref, lnb_ref, ws_ref, bst_ref, y_ref):
        for r0 in range(0, rows, GM_CHUNK):
            ys = _gmlp_chunk(*_gmlp_operands(u_ref, v_ref, lnw_ref, lnb_ref, ws_ref, bst_ref, r0))
            for g in range(GM_GROUPS):
                y_ref[r0:r0 + GM_CHUNK, g * HEAD:(g + 1) * HEAD] = ys[g].astype(ACT)

    return _pcall(body, name=name, grid=(t // rows,), in_specs=_gmlp_specs(rows),
                  out_specs=pl.BlockSpec((rows, D), lambda i: (i, 0)),
                  out_shape=jax.ShapeDtypeStruct((t, D), ACT), compiler_params=_cp(1))(p, p, lnw, lnb, ws, bst)


def _gmlp_bwd(p, dy, lnw, lnb, ws, bst, name, exchange=(), bcast=()):
    t = p.shape[0]
    rows = _gmlp_rows(t)

    def body(u_ref, v_ref, lnw_ref, lnb_ref, ws_ref, bst_ref, dy_ref, duv_ref, dlnw_ref, dlnb_ref, dws_ref, dbst_ref):
        @pl.when(pl.program_id(0) == 0)
        def _():
            dlnw_ref[...] = jnp.zeros_like(dlnw_ref)
            dlnb_ref[...] = jnp.zeros_like(dlnb_ref)
            dws_ref[...] = jnp.zeros_like(dws_ref)
            dbst_ref[...] = jnp.zeros_like(dbst_ref)

        for r0 in range(0, rows, GM_CHUNK):
            ops = _gmlp_operands(u_ref, v_ref, lnw_ref, lnb_ref, ws_ref, bst_ref, r0)
            _, vjp = jax.vjp(_gmlp_chunk, *ops)
            dus, dvs, dlnws, dlnbs, dwss, dbss = vjp(_split_cols(dy_ref[r0:r0 + GM_CHUNK, :].astype(F32), GM_GROUPS))
            dbst = jnp.zeros((GM_CHUNK, HEAD), F32)
            for g in range(GM_GROUPS):
                lo = g * HEAD
                duv_ref[r0:r0 + GM_CHUNK, lo:lo + HEAD] = dus[g].astype(ACT)
                duv_ref[r0:r0 + GM_CHUNK, D + lo:D + lo + HEAD] = dvs[g].astype(ACT)
                dlnw_ref[:, lo:lo + HEAD] += dlnws[g]
                dlnb_ref[:, lo:lo + HEAD] += dlnbs[g]
                dws_ref[g] += dwss[g]
                dbst = dbst + dbss[g] * _onehot_row(HEAD, g)
            dbst_ref[...] += dbst

    call = dict(
        name=name, grid=(t // rows,),
        in_specs=_gmlp_specs(rows) + [pl.BlockSpec((rows, D), lambda i: (i, 0))],
        out_specs=[pl.BlockSpec((rows, 2 * D), lambda i: (i, 0)), _full((1, D)), _full((1, D)),
                   _full((GM_GROUPS, GM_CHUNK, GM_CHUNK)), _full((GM_CHUNK, HEAD))],
        out_shape=[jax.ShapeDtypeStruct((t, 2 * D), ACT), jax.ShapeDtypeStruct((1, D), F32),
                   jax.ShapeDtypeStruct((1, D), F32), jax.ShapeDtypeStruct((GM_GROUPS, GM_CHUNK, GM_CHUNK), F32),
                   jax.ShapeDtypeStruct((GM_CHUNK, HEAD), F32)],
        compiler_params=_cp(1))
    args = (p, p, lnw, lnb, ws, bst, dy)
    if not exchange and not bcast:
        return _pcall(body, **call)(*args), ()
    return _pcall_hosting(body, args, exchange, bcast, **call)


def _prev_halo(tm, col):
    return pl.BlockSpec((HALO, D), lambda i: (jnp.maximum(i * (tm // HALO) - 1, 0), col))


def _next_halo(tm, col, n_tiles):
    return pl.BlockSpec((HALO, D), lambda i: (jnp.minimum(i + 1, n_tiles - 1) * (tm // HALO), col))


def _taps_back(ext, w, width):
    acc = None
    for k in range(width):
        s = width - 1 - k
        term = w[k:k + 1, :] * (pltpu.roll(ext, s, 0) if s else ext)[HALO:, :]
        acc = term if acc is None else acc + term
    return acc


def _taps_fwd(ext, w, width, n):
    rows = ext.shape[0]
    acc = None
    for k in range(width):
        s = width - 1 - k
        term = w[k:k + 1, :] * (pltpu.roll(ext, rows - s, 0) if s else ext)[0:n, :]
        acc = term if acc is None else acc + term
    return acc


def _sconv_fwd(p, cw, name):
    t = p.shape[0]
    tm = min(256, t)

    def body(b_ref, c_ref, h_ref, cp_ref, hp_ref, w_ref, y_ref):
        first = pl.program_id(0) == 0
        prev = jnp.where(first, 0.0, cp_ref[...].astype(F32) * hp_ref[...].astype(F32))
        ext = jnp.concatenate([prev, c_ref[...].astype(F32) * h_ref[...].astype(F32)], axis=0)
        y_ref[...] = (b_ref[...].astype(F32) * _taps_back(ext, w_ref[...], 3)).astype(ACT)

    c0 = ZQ // D
    tile = [pl.BlockSpec((tm, D), lambda i, c=c: (i, c)) for c in (c0, c0 + 1, c0 + 2)]
    return _pcall(body, name=name, grid=(t // tm,),
                  in_specs=tile + [_prev_halo(tm, c0 + 1), _prev_halo(tm, c0 + 2), _full((HALO, D))],
                  out_specs=pl.BlockSpec((tm, D), lambda i: (i, 0)),
                  out_shape=jax.ShapeDtypeStruct((t, D), ACT), compiler_params=_cp(1))(p, p, p, p, p, cw)


def _sconv_bwd(p, dy, cw, name):
    t = p.shape[0]
    tm = min(256, t)
    n_tiles = t // tm

    def body(b_ref, c_ref, h_ref, cp_ref, hp_ref, bn_ref, dy_ref, dyn_ref, w_ref, d_ref, dw_ref):
        i = pl.program_id(0)

        @pl.when(i == 0)
        def _():
            dw_ref[...] = jnp.zeros_like(dw_ref)

        w = w_ref[...]
        bv, cv, hv = b_ref[...].astype(F32), c_ref[...].astype(F32), h_ref[...].astype(F32)
        dyv = dy_ref[...].astype(F32)
        prev = jnp.where(i == 0, 0.0, cp_ref[...].astype(F32) * hp_ref[...].astype(F32))
        ext = jnp.concatenate([prev, cv * hv], axis=0)
        conv = _taps_back(ext, w, 3)
        dconv = dyv * bv
        nxt = jnp.where(i == n_tiles - 1, 0.0, dyn_ref[...].astype(F32) * bn_ref[...].astype(F32))
        dc = _taps_fwd(jnp.concatenate([dconv, nxt], axis=0), w, 3, tm)
        d_ref[:, 0:D] = (dyv * conv).astype(ACT)
        d_ref[:, D:2 * D] = (dc * hv).astype(ACT)
        d_ref[:, 2 * D:3 * D] = (dc * cv).astype(ACT)
        for k in range(3):
            s = 2 - k
            shifted = (pltpu.roll(ext, s, 0) if s else ext)[HALO:, :]
            dw_ref[k:k + 1, :] += jnp.sum(dconv * shifted, axis=0, keepdims=True)

    c0 = ZQ // D
    tile = [pl.BlockSpec((tm, D), lambda i, c=c: (i, c)) for c in (c0, c0 + 1, c0 + 2)]
    return _pcall(
        body, name=name, grid=(n_tiles,),
        in_specs=tile + [_prev_halo(tm, c0 + 1), _prev_halo(tm, c0 + 2), _next_halo(tm, c0, n_tiles),
                         pl.BlockSpec((tm, D), lambda i: (i, 0)), _next_halo(tm, 0, n_tiles), _full((HALO, D))],
        out_specs=[pl.BlockSpec((tm, 3 * D), lambda i: (i, 0)), _full((HALO, D))],
        out_shape=[jax.ShapeDtypeStruct((t, 3 * D), ACT), jax.ShapeDtypeStruct((HALO, D), F32)],
        compiler_params=_cp(1),
    )(p, p, p, p, p, p, dy, dy, cw)


def _l2_heads(s, scale):
    outs, rs = [], []
    for hh in range(DN_HEADS):
        blk = s[:, hh * HEAD:(hh + 1) * HEAD]
        r = lax.rsqrt(jnp.sum(blk * blk, axis=-1, keepdims=True) + EPS)
        outs.append(blk * (r * scale))
        rs.append(r)
    return outs, rs


_QKV_SCALE = (HEAD ** -0.5, 1.0, None)


def _qkv_fwd(p, cw, name):
    t = p.shape[0]
    tm = min(256, t)

    def body(q_ref, k_ref, v_ref, qp_ref, kp_ref, vp_ref, w_ref, o_ref):
        first = pl.program_id(0) == 0
        for part, (ref, pref) in enumerate(((q_ref, qp_ref), (k_ref, kp_ref), (v_ref, vp_ref))):
            prev = jnp.where(first, 0.0, pref[...].astype(F32))
            ext = jnp.concatenate([prev, ref[...].astype(F32)], axis=0)
            s = _silu(_taps_back(ext, w_ref[:, part * D:(part + 1) * D], 4))
            if _QKV_SCALE[part] is None:
                o_ref[:, part * D:(part + 1) * D] = s.astype(ACT)
            else:
                outs, _ = _l2_heads(s, _QKV_SCALE[part])
                for hh in range(DN_HEADS):
                    o_ref[:, part * D + hh * HEAD:part * D + (hh + 1) * HEAD] = outs[hh].astype(ACT)

    c0 = ZQ // D
    tile = [pl.BlockSpec((tm, D), lambda i, c=c: (i, c)) for c in (c0, c0 + 1, c0 + 2)]
    halo = [_prev_halo(tm, c) for c in (c0, c0 + 1, c0 + 2)]
    return _pcall(body, name=name, grid=(t // tm,), in_specs=tile + halo + [_full((HALO, 3 * D))],
                  out_specs=pl.BlockSpec((tm, 3 * D), lambda i: (i, 0)),
                  out_shape=jax.ShapeDtypeStruct((t, 3 * D), ACT), compiler_params=_cp(1))(p, p, p, p, p, p, cw)


def _qkv_bwd(p, dqkv, cw, name):
    t = p.shape[0]
    tm = min(256, t)
    n_tiles = t // tm

    def body(*refs):
        tiles, prevs, nexts = refs[0:3], refs[3:6], refs[6:9]
        d_tiles, d_nexts = refs[9:12], refs[12:15]
        w_ref, o_ref, dw_ref = refs[15:]
        i = pl.program_id(0)

        @pl.when(i == 0)
        def _():
            dw_ref[...] = jnp.zeros_like(dw_ref)

        for part in range(3):
            w = w_ref[:, part * D:(part + 1) * D]
            prev = jnp.where(i == 0, 0.0, prevs[part][...].astype(F32))
            ext = jnp.concatenate([prev, tiles[part][...].astype(F32), nexts[part][...].astype(F32)], axis=0)
            xc = _taps_back(ext, w, 4)
            dout = jnp.concatenate([d_tiles[part][...].astype(F32), d_nexts[part][...].astype(F32)], axis=0)
            s = _silu(xc)
            if _QKV_SCALE[part] is None:
                ds = dout
            else:
                scale = _QKV_SCALE[part]
                pieces = []
                for hh in range(DN_HEADS):
                    blk = s[:, hh * HEAD:(hh + 1) * HEAD]
                    dblk = dout[:, hh * HEAD:(hh + 1) * HEAD]
                    r = lax.rsqrt(jnp.sum(blk * blk, axis=-1, keepdims=True) + EPS)
                    pieces.append(scale * r * (dblk - blk * (r * r) * jnp.sum(dblk * blk, axis=-1, keepdims=True)))
                ds = jnp.concatenate(pieces, axis=1)
            dxc = ds * _dsilu(xc)
            row = lax.broadcasted_iota(jnp.int32, (tm + HALO, 1), 0)
            dxc = jnp.where(jnp.logical_and(i == n_tiles - 1, row >= tm), 0.0, dxc)
            o_ref[:, part * D:(part + 1) * D] = _taps_fwd(dxc, w, 4, tm).astype(ACT)
            for k in range(4):
                sh = 3 - k
                shifted = (pltpu.roll(ext, sh, 0) if sh else ext)[HALO:HALO + tm, :]
                dw_ref[k:k + 1, part * D:(part + 1) * D] += jnp.sum(dxc[0:tm, :] * shifted, axis=0, keepdims=True)

    c0 = ZQ // D
    cols = (c0, c0 + 1, c0 + 2)
    tile = [pl.BlockSpec((tm, D), lambda i, c=c: (i, c)) for c in cols]
    dtile = [pl.BlockSpec((tm, D), lambda i, c=c: (i, c)) for c in range(3)]
    in_specs = (tile + [_prev_halo(tm, c) for c in cols] + [_next_halo(tm, c, n_tiles) for c in cols]
                + dtile + [_next_halo(tm, c, n_tiles) for c in range(3)] + [_full((HALO, 3 * D))])
    return _pcall(
        body, name=name, grid=(n_tiles,), in_specs=in_specs,
        out_specs=[pl.BlockSpec((tm, 3 * D), lambda i: (i, 0)), _full((HALO, 3 * D))],
        out_shape=[jax.ShapeDtypeStruct((t, 3 * D), ACT), jax.ShapeDtypeStruct((HALO, 3 * D), F32)],
        compiler_params=_cp(1),
    )(*([p] * 9), *([dqkv] * 6), cw)


def _tri_masks(n):
    row = lax.broadcasted_iota(jnp.int32, (n, n), 0)
    col = lax.broadcasted_iota(jnp.int32, (n, n), 1)
    return row, col


@jax.custom_vjp
def _unit_lower_inverses(mats):
    n = DN_CHUNK
    row, col = _tri_masks(n)
    eye = (row == col).astype(F32)
    same16 = (row // 16) == (col // 16)
    same32 = (row // 32) == (col // 32)
    pw = [jnp.where(same16, a, 0.0) for a in mats]
    x = [eye - p for p in pw]
    for _ in range(3):
        pw = [_hnn(p, p) for p in pw]
        x = [_hnn(xi, eye + p) for xi, p in zip(x, pw)]
    for keep in (jnp.logical_and(same32, jnp.logical_not(same16)), jnp.logical_not(same32)):
        inner = [_hnn(jnp.where(keep, a, 0.0), xi) for a, xi in zip(mats, x)]
        x = [xi - _hnn(xi, y) for xi, y in zip(x, inner)]
    return tuple(x)


def _uli_fwd(mats):
    t = _unit_lower_inverses(mats)
    return t, t


def _uli_bwd(ts, gs):
    inner = [_nt(g, t) for g, t in zip(gs, ts)]
    return (tuple(-_tn(t, y) for t, y in zip(ts, inner)),)


_unit_lower_inverses.defvjp(_uli_fwd, _uli_bwd)


@jax.custom_vjp
def _known_inverses(mats, ts):
    return ts


_known_inverses.defvjp(lambda mats, ts: (ts, ts),
                       lambda ts, gs: (_uli_bwd(ts, gs)[0], tuple(jnp.zeros_like(t) for t in ts)))


def _pick_col(m, k):
    return jnp.sum(m * _onehot_row(m.shape[1], k), axis=1, keepdims=True)


def _pick_row(m, k):
    hot = (lax.broadcasted_iota(jnp.int32, (m.shape[0], 1), 0) == k).astype(F32)
    return jnp.sum(m * hot, axis=0, keepdims=True)


def _delta_chunk(states, qs, ks, vs, ab, alog, dtb, onw, known_inverses=None):
    n = DN_CHUNK
    heads = range(DN_HEADS)
    row, col = _tri_masks(n)
    incl = row >= col
    lane = lax.broadcasted_iota(jnp.int32, (1, HEAD), 1)
    g_all = jnp.where(lane < DN_HEADS, -jnp.exp(alog) * jax.nn.softplus(ab + dtb), 0.0)
    c_cols = _hnn(incl.astype(F32), g_all)
    c_rows = _htn(g_all, (row <= col).astype(F32))
    g_tot = jnp.sum(g_all, axis=0, keepdims=True)
    beta_all = jax.nn.sigmoid(ab)
    ccol = [_pick_col(c_cols, h) for h in heads]
    crow = [_pick_row(c_rows, h) for h in heads]
    gl = [_pick_col(g_tot, h) for h in heads]
    beta = [_pick_col(beta_all, DN_HEADS + h) for h in heads]
    decay = [jnp.exp(jnp.where(incl, ccol[h] - crow[h], -1e30)) for h in heads]
    eg = [jnp.exp(ccol[h]) for h in heads]
    kb = [ks[h] * beta[h] for h in heads]
    amat = [jnp.where(row > col, _nt(kb[h], ks[h]) * decay[h], 0.0) for h in heads]
    if known_inverses is None:
        tmat = _unit_lower_inverses(tuple(amat))
    else:
        tmat = _known_inverses(tuple(amat), tuple(known_inverses))
    u = [_nn(tmat[h], vs[h] * beta[h]) for h in heads]
    w = [_nn(tmat[h], kb[h] * eg[h]) for h in heads]
    qk = [_nt(qs[h], ks[h]) * decay[h] for h in heads]
    v_new = [u[h] - _nn(w[h], states[h]) for h in heads]
    o = [_nn(qs[h] * eg[h], states[h]) + _nn(qk[h], v_new[h]) for h in heads]
    new_states = [states[h] * jnp.exp(gl[h]) + _tn(ks[h] * jnp.exp(gl[h] - ccol[h]), v_new[h]) for h in heads]
    ys = [o[h] * lax.rsqrt(jnp.mean(o[h] * o[h], axis=-1, keepdims=True) + EPS) * onw for h in heads]
    return (ys, new_states), tmat


def _head_cols(ref, r0):
    return [ref[r0:r0 + DN_CHUNK, h * HEAD:(h + 1) * HEAD].astype(F32) for h in range(DN_HEADS)]


def _delta_rows(t):
    return min(DN_CHUNKS_PER_STEP, t // DN_CHUNK) * DN_CHUNK


def _delta_fwd(qkv, ab, alog, dtb, onw, name):
    t = qkv.shape[0]
    rows = _delta_rows(t)
    per_step = rows // DN_CHUNK

    def body(q_ref, k_ref, v_ref, ab_ref, alog_ref, dtb_ref, onw_ref, y_ref, keep_ref, inv_ref, state):
        @pl.when(pl.program_id(0) == 0)
        def _():
            state[...] = jnp.zeros_like(state)

        s = [state[hh] for hh in range(DN_HEADS)]
        for c in range(per_step):
            r0 = c * DN_CHUNK
            (ys, s1), tmat = _delta_chunk(s, _head_cols(q_ref, r0), _head_cols(k_ref, r0), _head_cols(v_ref, r0),
                                          ab_ref[r0:r0 + DN_CHUNK, :], alog_ref[...], dtb_ref[...], onw_ref[...])
            for hh in range(DN_HEADS):
                keep_ref[c, hh] = s[hh]
                inv_ref[c, hh] = tmat[hh]
                y_ref[r0:r0 + DN_CHUNK, hh * HEAD:(hh + 1) * HEAD] = ys[hh].astype(ACT)
            s = s1
        for hh in range(DN_HEADS):
            state[hh] = s[hh]

    block = [pl.BlockSpec((rows, D), lambda i, c=c: (i, c)) for c in range(3)]
    nc = t // DN_CHUNK
    return _pcall(
        body, name=name, grid=(t // rows,),
        in_specs=block + [pl.BlockSpec((rows, HEAD), lambda i: (i, 0)), _full((1, HEAD)), _full((1, HEAD)),
                          _full((1, HEAD))],
        out_specs=[pl.BlockSpec((rows, D), lambda i: (i, 0)),
                   pl.BlockSpec((per_step, DN_HEADS, HEAD, HEAD), lambda i: (i, 0, 0, 0)),
                   pl.BlockSpec((per_step, DN_HEADS, DN_CHUNK, DN_CHUNK), lambda i: (i, 0, 0, 0))],
        out_shape=[jax.ShapeDtypeStruct((t, D), ACT), jax.ShapeDtypeStruct((nc, DN_HEADS, HEAD, HEAD), F32),
                   jax.ShapeDtypeStruct((nc, DN_HEADS, DN_CHUNK, DN_CHUNK), F32)],
        scratch_shapes=[pltpu.VMEM((DN_HEADS, HEAD, HEAD), F32)], compiler_params=_cp(1),
    )(qkv, qkv, qkv, ab, alog, dtb, onw)


def _delta_bwd(qkv, ab, alog, dtb, onw, keep, inv, dy, name):
    t = qkv.shape[0]
    rows = _delta_rows(t)
    per_step, n_steps = rows // DN_CHUNK, t // rows

    def body(q_ref, k_ref, v_ref, ab_ref, alog_ref, dtb_ref, onw_ref, keep_ref, inv_ref, dy_ref,
             dqkv_ref, dab_ref, dalog_ref, ddtb_ref, donw_ref, dstate):
        @pl.when(pl.program_id(0) == 0)
        def _():
            dstate[...] = jnp.zeros_like(dstate)
            dalog_ref[...] = jnp.zeros_like(dalog_ref)
            ddtb_ref[...] = jnp.zeros_like(ddtb_ref)
            donw_ref[...] = jnp.zeros_like(donw_ref)

        ds = [dstate[hh] for hh in range(DN_HEADS)]
        for c in reversed(range(per_step)):
            r0 = c * DN_CHUNK
            s0 = [keep_ref[c, hh] for hh in range(DN_HEADS)]
            known = [inv_ref[c, hh] for hh in range(DN_HEADS)]
            _, vjp, _ = jax.vjp(functools.partial(_delta_chunk, known_inverses=known), s0, _head_cols(q_ref, r0),
                                _head_cols(k_ref, r0), _head_cols(v_ref, r0), ab_ref[r0:r0 + DN_CHUNK, :], alog_ref[...],
                                dtb_ref[...], onw_ref[...], has_aux=True)
            ds, dq, dk, dv, dab, dal, ddt, don = vjp((_head_cols(dy_ref, r0), ds))
            for hh in range(DN_HEADS):
                lo = hh * HEAD
                dqkv_ref[r0:r0 + DN_CHUNK, lo:lo + HEAD] = dq[hh].astype(ACT)
                dqkv_ref[r0:r0 + DN_CHUNK, D + lo:D + lo + HEAD] = dk[hh].astype(ACT)
                dqkv_ref[r0:r0 + DN_CHUNK, 2 * D + lo:2 * D + lo + HEAD] = dv[hh].astype(ACT)
            dab_ref[r0:r0 + DN_CHUNK, :] = dab
            dalog_ref[...] += dal
            ddtb_ref[...] += ddt
            donw_ref[...] += don
        for hh in range(DN_HEADS):
            dstate[hh] = ds[hh]

    rev = lambda i: n_steps - 1 - i
    block = [pl.BlockSpec((rows, D), lambda i, c=c: (rev(i), c)) for c in range(3)]
    small = jax.ShapeDtypeStruct((1, HEAD), F32)
    return _pcall(
        body, name=name, grid=(n_steps,),
        in_specs=block + [pl.BlockSpec((rows, HEAD), lambda i: (rev(i), 0)), _full((1, HEAD)), _full((1, HEAD)),
                          _full((1, HEAD)), pl.BlockSpec((per_step, DN_HEADS, HEAD, HEAD), lambda i: (rev(i), 0, 0, 0)),
                          pl.BlockSpec((per_step, DN_HEADS, DN_CHUNK, DN_CHUNK), lambda i: (rev(i), 0, 0, 0)),
                          pl.BlockSpec((rows, D), lambda i: (rev(i), 0))],
        out_specs=[pl.BlockSpec((rows, 3 * D), lambda i: (rev(i), 0)),
                   pl.BlockSpec((rows, HEAD), lambda i: (rev(i), 0)), _full((1, HEAD)), _full((1, HEAD)),
                   _full((1, HEAD))],
        out_shape=[jax.ShapeDtypeStruct((t, 3 * D), ACT), jax.ShapeDtypeStruct((t, HEAD), F32), small, small, small],
        scratch_shapes=[pltpu.VMEM((DN_HEADS, HEAD, HEAD), F32)], compiler_params=_cp(1),
    )(qkv, qkv, qkv, ab, alog, dtb, onw, keep, inv, dy)


def _loss_head(y, target):
    t = y.shape[0]
    tm = min(512, t)

    def body(y_ref, t_ref, l_ref, dy_ref):
        @pl.when(pl.program_id(0) == 0)
        def _():
            l_ref[...] = jnp.zeros_like(l_ref)

        diff = y_ref[...] - t_ref[...]
        dy_ref[...] = diff * (1.0 / D)
        l_ref[...] += 0.5 * jnp.sum(jnp.sum(diff * diff, axis=-1, keepdims=True) * (1.0 / D), axis=0, keepdims=True)

    row = pl.BlockSpec((tm, D), lambda i: (i, 0))
    return _pcall(body, name="loss_head", grid=(t // tm,), in_specs=[row, row], out_specs=[_full((8, 128)), row],
                  out_shape=[jax.ShapeDtypeStruct((8, 128), F32), jax.ShapeDtypeStruct((t, D), F32)],
                  compiler_params=_cp(1))(y, target)


_SMALL_SHARDED = (("a_ln_w", (2, 128), 1), ("a_ln_b", (2, 128), 1), ("b_conv_w", (1, 3, 128), 2),
                  ("c_conv_w", (1, 4, 384), 2))
_REPLICATED = (("mem_norm_w", (1024,)), ("norm_pre", (4, 1024)), ("norm_post", (4, 1024)),
               ("a_b_s", (2, 8, 128)), ("c_a_log", (1, 8)), ("c_dt_bias", (1, 8)), ("c_o_norm_w", (1, 128)))


def _layer_shards(given, prefix):
    w_out = given[prefix + "w_out"]
    w_ins = [given[prefix + "a_w_in"][0], given[prefix + "b_w_in"], given[prefix + "c_w_in"], given[prefix + "a_w_in"][1]]
    return [[w_out[i], w_ins[i]] for i in range(4)]


def _as_matmul_operand(w):
    return w.reshape(w.shape[-2:]).astype(MXU)


def _as_rows(shape):
    return (math.prod(shape[:-1]) if len(shape) > 1 else 1, shape[-1])


def _row_table(shapes):
    table, at = [], 0
    for shape in shapes:
        rows, cols = _as_rows(shape)
        table.append((at, rows, cols))
        at += rows
    return table


def _pack_rows(arrays, shapes, rows, lead=()):
    n_lead = len(lead)
    parts = [jnp.pad(a.reshape(lead + _as_rows(s)), [(0, 0)] * (n_lead + 1) + [(0, D - s[-1])])
             for a, s in zip(arrays, shapes)]
    block = jnp.concatenate(parts, axis=n_lead)
    return jnp.pad(block, [(0, 0)] * n_lead + [(0, rows - block.shape[n_lead]), (0, 0)])


def _unpack_rows(packed, shapes, lead=()):
    n_lead = len(lead)
    return [packed[(slice(None),) * n_lead + (slice(r0, r0 + nr), slice(0, nc))].reshape(lead + s)
            for (r0, nr, nc), s in zip(_row_table(shapes), shapes)]


def _join_shards(blocks, axis):
    moved = jnp.moveaxis(blocks, 0, axis)
    shape = moved.shape
    return moved.reshape(shape[:axis] + (shape[axis] * shape[axis + 1],) + shape[axis + 2:])


def _split_shards(full, axis):
    shape = full.shape
    split = full.reshape(shape[:axis] + (N_DEV, shape[axis] // N_DEV) + shape[axis + 1:])
    return jnp.moveaxis(split, axis, 0)


_A_COLS = ((2560, 1536), (2048, 512), (0, 2048))
_BC_COLS = ((3584, 1536), (3072, 512), (0, 3072))
_C_COLS = ((3600, 1536), (3088, 512), (0, 3072))


def _reorder_cols(w, cols):
    return jnp.concatenate([w[:, s:s + n] for s, n in cols], axis=1)


def _restore_cols(pieces_in_my_order, cols, extra=()):
    placed = sorted(list(zip([s for s, _ in cols], pieces_in_my_order)) + list(extra), key=lambda sp: sp[0])
    return jnp.concatenate([piece for _, piece in placed], axis=1)


def kernel(x, mem, mem_norm_w, w_mem_kv, norm_pre, norm_post, w_out, a_w_in, a_ln_w, a_ln_b, a_w_s, a_b_s, b_w_in, b_conv_w, c_w_in, c_conv_w, c_a_log, c_dt_bias, c_o_norm_w, loss_target, m_mem_norm_w, m_w_mem_kv, m_norm_pre, m_norm_post, m_w_out, m_a_w_in, m_a_ln_w, m_a_ln_b, m_a_w_s, m_a_b_s, m_b_w_in, m_b_conv_w, m_c_w_in, m_c_conv_w, m_c_a_log, m_c_dt_bias, m_c_o_norm_w, v_mem_norm_w, v_w_mem_kv, v_norm_pre, v_norm_post, v_w_out, v_a_w_in, v_a_ln_w, v_a_ln_b, v_a_w_s, v_a_b_s, v_b_w_in, v_b_conv_w, v_c_w_in, v_c_conv_w, v_c_a_log, v_c_dt_bias, v_c_o_norm_w):
    given = dict(locals())
    x0 = x[0]
    mem0 = mem[0]
    target = loss_target[0]

    w_sh, m_sh, v_sh = (_layer_shards(given, pre) for pre in ("", "m_", "v_"))
    small_names = [n for n, _, _ in _SMALL_SHARDED]
    small_shapes = [s for _, s, _ in _SMALL_SHARDED]
    repl_names = [n for n, _ in _REPLICATED]
    repl_shapes = [s for _, s in _REPLICATED]
    small = _pack_rows([given[n] for n in small_names], small_shapes, R_SMALL)
    g_in0, = _all_gather([_as_matmul_operand(w_sh[0][1])], "gather_weights")
    w_in = [_reorder_cols(_join_shards(g_in0, 1), _A_COLS)]
    wouts, w_cab = [], None
    ws = [a_w_s[j] for j in range(2)]
    bst = [jnp.pad(a_b_s[j].T, ((0, 0), (0, HEAD - GM_GROUPS))) for j in range(2)]
    alog = jnp.pad(c_a_log, ((0, 0), (0, HEAD - DN_HEADS)))
    dtb = jnp.pad(c_dt_bias, ((0, 0), (0, HEAD - DN_HEADS)))
    onw = c_o_norm_w
    mw = mem_norm_w[None, :]

    xs, saved = [x0], []
    for i in range(4):
        kind = i % 3
        npre, npost = norm_pre[i][None, :], norm_post[i][None, :]
        ahead = [_as_matmul_operand(w) for w in w_sh[i + 1]] if i < 3 else []
        if i == 0:
            ahead = [w_mem_kv.astype(MXU), _as_matmul_operand(w_sh[0][0]), small] + ahead
        res = _proj_fwd(xs[i], npre, w_in[i], w_cab if kind == 2 else None, f"proj_fwd_{i}", gather=ahead)
        if i == 0:
            g_wkv, g_wout0, g_small = res[2:5]
            small_full = _unpack_rows(g_small, small_shapes, lead=(N_DEV,))
            full = {n: _join_shards(blocks, ax) for (n, _, ax), blocks in zip(_SMALL_SHARDED, small_full)}
            wkv = g_wkv.reshape(D, 2 * D_XA)
            wouts.append(g_wout0.reshape(D_CAT, D))
            lnw = [full["a_ln_w"][j][None, :] for j in range(2)]
            lnb = [full["a_ln_b"][j][None, :] for j in range(2)]
            cw_b = jnp.pad(full["b_conv_w"][0], ((0, HALO - 3), (0, 0)))
            cw_c = jnp.pad(full["c_conv_w"][0], ((0, HALO - 4), (0, 0)))
            kv = _memkv_fwd(mem0, mw, wkv).astype(MXU)
        if ahead:
            g_wout, g_in = res[-2:]
            wouts.append(g_wout.reshape(D_CAT, D))
            if i + 1 == 2:
                c_full = jnp.concatenate([g_in[d] for d in range(N_DEV)], axis=1)
                w_in.append(_reorder_cols(c_full, _C_COLS))
                w_cab = jnp.pad(c_full[:, 3072:3088], ((0, 0), (0, HEAD - 16)))
            else:
                w_in.append(_reorder_cols(_join_shards(g_in, 1), _BC_COLS if i + 1 == 1 else _A_COLS))
        if kind == 2:
            p, h, ab = res[:3]
            qkv = _qkv_fwd(p, cw_c, f"qkv_fwd_{i}")
            ymix, keep, inv = _delta_fwd(qkv, ab, alog, dtb, onw, f"delta_fwd_{i}")
            extra = (qkv, ab, keep, inv)
        else:
            p, h = res[:2]
            if kind == 0:
                ymix = _gmlp_fwd(p, lnw[i // 3], lnb[i // 3], ws[i // 3], bst[i // 3], f"gmlp_fwd_{i}")
            else:
                ymix = _sconv_fwd(p, cw_b, f"sconv_fwd_{i}")
            extra = ()
        xn, o = _tail_fwd(p, ymix, xs[i], kv, wouts[i], npost, f"tail_fwd_{i}")
        xs.append(xn)
        saved.append((p, h, ymix, o, extra))

    loss_tile, dx = _loss_head(xs[4], target)
    loss = lax.psum(loss_tile[0, 0], ("x", "y", "c"))

    g = {}
    d_npre, d_npost = [None] * 4, [None] * 4
    d_ws, d_bs, d_lnw, d_lnb = [None] * 2, [None] * 2, [None] * 2, [None] * 2
    dkv = None
    pend_win, pend_wout = None, None
    landed_win, landed_wout = [None] * 4, [None] * 4
    for i in reversed(range(4)):
        kind = i % 3
        p, h, ymix, o, extra = saved[i]
        npre, npost = norm_pre[i][None, :], norm_post[i][None, :]
        res = _tail_bwd(dx, o, p, ymix, kv, wouts[i], npost, f"tail_bwd_{i}", exchange=[pend_win] if i < 3 else [])
        dzq, dymix, d_wout, d_npost[i], dkv_i = res[:5]
        if i < 3:
            landed_win[i + 1] = res[5]
        d_wout = d_wout.reshape(N_DEV, D_CAT // N_DEV, D).astype(XCH)
        dkv = dkv_i if dkv is None else dkv + dkv_i
        w_zq, w_mix = w_in[i][:, :ZQ], w_in[i][:, ZQ:]
        dw_zq = _matmul_tn(h, dzq, f"dw_zq_{i}")
        if kind == 0:
            j = i // 3
            early_x, early_b = [], []
            if i == 0:
                d_wkv, d_mw = _memkv_bwd(mem0, mw, wkv, dkv)
                early_x = [d_wkv.reshape(N_DEV, 128, D).astype(XCH), d_wout, pend_wout]
                early_b = [d_ws[1].reshape(GM_CHUNK, D).astype(XCH)]
            (dmix, d_lnw[j], d_lnb[j], d_ws[j], dbst), early_landed = _gmlp_bwd(
                p, dymix, lnw[j], lnb[j], ws[j], bst[j], f"gmlp_bwd_{i}", exchange=early_x, bcast=early_b)
            d_bs[j] = dbst[:, :GM_GROUPS].T
            if i == 0:
                g["mem_norm_w"] = d_mw[0]
                g["norm_pre"] = jnp.concatenate([jnp.zeros((1, D), F32)] + d_npre[1:], axis=0)
                g["norm_post"] = jnp.concatenate(d_npost, axis=0)
                g["a_ln_w"] = jnp.concatenate(d_lnw, axis=0)
                g["a_ln_b"] = jnp.concatenate(d_lnb, axis=0)
                g["a_b_s"] = jnp.stack(d_bs)
                e_small = _pack_rows([_split_shards(g[n], ax) for n, _, ax in _SMALL_SHARDED], small_shapes, R_SMALL,
                                     lead=(N_DEV,))
                r_pack = _pack_rows([g[n] for n in repl_names], repl_shapes, R_REPL)
                dw_mix, (l_small, ws0_all, r_all) = _matmul_tn(
                    h, dmix, f"dw_mix_{i}", exchange=[e_small], bcast=[d_ws[0].reshape(GM_CHUNK, D).astype(XCH), r_pack])
            else:
                dw_mix = _matmul_tn(h, dmix, f"dw_mix_{i}")
            d_win = _restore_cols([dw_zq[:, :D_CAT], dw_zq[:, D_CAT:], dw_mix], _A_COLS)
            dps, wparts = [dzq, dmix], [w_zq, w_mix]
        elif kind == 1:
            dmix, dcw = _sconv_bwd(p, dymix, cw_b, f"sconv_bwd_{i}")
            g["b_conv_w"] = dcw[None, :3]
            dw_mix = _matmul_tn(h, dmix, f"dw_mix_{i}")
            d_win = _restore_cols([dw_zq[:, :D_CAT], dw_zq[:, D_CAT:], dw_mix], _BC_COLS)
            dps, wparts = [dzq, dmix], [w_zq, w_mix]
        else:
            qkv, ab, keep, inv = extra
            dqkv, dab, dalog, ddtb, donw = _delta_bwd(qkv, ab, alog, dtb, onw, keep, inv, dymix, f"delta_bwd_{i}")
            dmix, dcw = _qkv_bwd(p, dqkv, cw_c, f"qkv_bwd_{i}")
            g["c_conv_w"] = dcw[None, :4]
            g["c_a_log"], g["c_dt_bias"], g["c_o_norm_w"] = dalog[:, :DN_HEADS], ddtb[:, :DN_HEADS], donw
            dw_mix = _matmul_tn(h, dmix, f"dw_mix_{i}")
            dw_ab = _matmul_tn(h, dab, f"dw_ab_{i}")
            d_win = _restore_cols([dw_zq[:, :D_CAT], dw_zq[:, D_CAT:], dw_mix], _C_COLS, extra=[(3072, dw_ab[:, :16])])
            dps, wparts = [dzq, dmix, dab], [w_zq, w_mix, w_cab]
        width = d_win.shape[1] // N_DEV
        pend_win = jnp.stack([d_win[:, d * width:(d + 1) * width] for d in range(N_DEV)]).astype(XCH)
        if i > 0:
            res = _proj_bwd_x(dps, wparts, xs[i], dx, npre, f"proj_bwd_x_{i}", exchange=[pend_wout] if i < 3 else [])
            dx, d_npre[i] = res[:2]
            if i < 3:
                landed_wout[i + 1] = res[2]
            pend_wout = d_wout

    dx, d_npre0, l_in = _proj_bwd_x(dps, wparts, xs[0], dx, norm_pre[0][None, :], "proj_bwd_x_0", exchange=[pend_win])
    l_wkv, landed_wout[0], landed_wout[1], ws3_all = early_landed
    landed_win[0] = l_in
    npre0_all, = _all_gather([jnp.pad(d_npre0, ((0, HALO - 1), (0, 0)))], "gather_norm_pre0")
    r_all = r_all.at[:, 1, :].set(npre0_all[:, 0, :])

    res = [[_reduce_adamw(parts, w_sh[i][a], m_sh[i][a], v_sh[i][a], f"adamw_{i}_{a}")
            for a, parts in enumerate((landed_wout[i], landed_win[i]))] for i in range(4)]
    res_wkv = _reduce_adamw(l_wkv, w_mem_kv, m_w_mem_kv, v_w_mem_kv, "adamw_w_mem_kv")
    res_small = _reduce_adamw_vectors(
        l_small, small, *(_pack_rows([given[pre + n] for n in small_names], small_shapes, R_SMALL) for pre in ("m_", "v_")),
        small_shapes, "adamw_small")
    res_ws = [_reduce_adamw(parts, *(given[pre + "a_w_s"][j].reshape(GM_CHUNK, D) for pre in ("", "m_", "v_")),
                            f"adamw_a_w_s_{j}") for j, parts in enumerate((ws0_all, ws3_all))]
    res_repl = _reduce_adamw_vectors(
        r_all, *(_pack_rows([given[pre + n] for n in repl_names], repl_shapes, R_REPL) for pre in ("", "m_", "v_")),
        repl_shapes, "adamw_replicated")

    order = ["mem_norm_w", "w_mem_kv", "norm_pre", "norm_post", "w_out", "a_w_in", "a_ln_w", "a_ln_b", "a_w_s", "a_b_s",
             "b_w_in", "b_conv_w", "c_w_in", "c_conv_w", "c_a_log", "c_dt_bias", "c_o_norm_w"]
    outs = [loss, dx[None]]
    for kind in range(4):
        got = dict(zip(repl_names, res_repl[kind]))
        got["a_w_s"] = jnp.stack([r[kind] for r in res_ws]).reshape(a_w_s.shape)
        got.update(zip(small_names, res_small[kind]))
        got["w_mem_kv"] = res_wkv[kind]
        got["w_out"] = jnp.stack([res[i][0][kind] for i in range(4)])
        got["a_w_in"] = jnp.stack([res[0][1][kind], res[3][1][kind]])
        got["b_w_in"] = res[1][1][kind]
        got["c_w_in"] = res[2][1][kind]
        outs += [got[n] for n in order]
    return tuple(outs)
```

```python
import functools
import math

import jax
import jax.numpy as jnp
from jax import lax
from jax.experimental import pallas as pl
from jax.experimental.pallas import tpu as pltpu

F32 = jnp.float32
MXU = jnp.bfloat16
ACT = jnp.bfloat16

D = 1024
D_XA = 512
D_CAT = 1536
N_MEM = 256
XA_HEADS = 4
HEAD = 128
ZQ = D_CAT + D_XA
EPS = 1e-6
GM_CHUNK = 128
GM_CHUNKS_PER_STEP = 4
GM_GROUPS = 8
DN_HEADS = 8
DN_CHUNK = 64
DN_CHUNKS_PER_STEP = 4
N_DEV = 8
HALO = 8
VMEM_LIMIT = 56 * 1024 * 1024
XCH = jnp.bfloat16
R_REPL = 32
R_SMALL = 16

ADAM_LR = 0.001
ADAM_B1 = 0.9
ADAM_B2 = 0.999
ADAM_EPS = 1e-08
ADAM_WD = 0.01
ADAM_STEP = 10

NN = ((1,), (0,))
NT = ((1,), (1,))
TN = ((0,), (0,))
MESH = pl.DeviceIdType.MESH


def _pcall(body, **kw):
    return pl.pallas_call(body, **kw)


def _cp(n_axes):
    return pltpu.CompilerParams(dimension_semantics=("arbitrary",) * n_axes, vmem_limit_bytes=VMEM_LIMIT)


def _dot(a, b, dims, prec=None):
    return lax.dot_general(a, b, (dims, ((), ())), preferred_element_type=F32, precision=prec)


def _mdot(a, b, dims):
    return _dot(a.astype(MXU), b.astype(MXU), dims)


def _make_mms(raw):
    @jax.custom_vjp
    def nn(a, b):
        return raw(a, b, NN)

    @jax.custom_vjp
    def nt(a, b):
        return raw(a, b, NT)

    @jax.custom_vjp
    def tn(a, b):
        return raw(a, b, TN)

    nn.defvjp(lambda a, b: (nn(a, b), (a, b)), lambda r, g: (nt(g, r[1]), tn(r[0], g)))
    nt.defvjp(lambda a, b: (nt(a, b), (a, b)), lambda r, g: (nn(g, r[1]), tn(g, r[0])))
    tn.defvjp(lambda a, b: (tn(a, b), (a, b)), lambda r, g: (nt(r[1], g), nn(r[0], g)))
    return nn, nt, tn


_nn, _nt, _tn = _make_mms(_mdot)


def _split_dot(a, b, dims):
    ah = a.astype(jnp.bfloat16)
    bh = b.astype(jnp.bfloat16)
    al = (a - ah.astype(F32)).astype(jnp.bfloat16)
    bl = (b - bh.astype(F32)).astype(jnp.bfloat16)
    return _dot(ah, bh, dims) + (_dot(ah, bl, dims) + _dot(al, bh, dims))


_hnn, _hnt, _htn = _make_mms(_split_dot)


def _full(shape):
    return pl.BlockSpec(shape, lambda *_: (0,) * len(shape))


def _silu(z):
    return z * jax.nn.sigmoid(z)


def _dsilu(z):
    s = jax.nn.sigmoid(z)
    return s * (1.0 + z * (1.0 - s))


def _onehot_row(n, k):
    return (lax.broadcasted_iota(jnp.int32, (1, n), 1) == k).astype(F32)


_HBM = pl.BlockSpec(memory_space=pl.ANY)


def _sem_shapes(n_remote, n_local):
    return [pltpu.SemaphoreType.DMA((n_remote,)), pltpu.SemaphoreType.DMA((n_remote,)),
            pltpu.SemaphoreType.DMA((n_local,))]


def _gather_parts(x_refs, out_refs, send_sems, recv_sems, local_sems):
    n = len(x_refs)
    x, y, cc = lax.axis_index("x"), lax.axis_index("y"), lax.axis_index("c")
    me, sibling = (x, y, cc), (x, y, 1 - cc)
    chips = [(1 - x, y), (x, 1 - y), (1 - x, 1 - y)]

    def slot(a, px, py, pc):
        return out_refs[a].at[4 * px + 2 * py + pc]

    def copy(k, a, block, to, src=None):
        return pltpu.make_async_remote_copy(
            src_ref=slot(a, *block) if src is None else src, dst_ref=slot(a, *block),
            send_sem=send_sems.at[k * n + a], recv_sem=recv_sems.at[k * n + a], device_id=to, device_id_type=MESH)

    mine = [pltpu.make_async_copy(x_refs[a], slot(a, *me), local_sems.at[a]) for a in range(n)]
    first = [copy(0, a, me, sibling, src=x_refs[a]) for a in range(n)]
    first += [copy(1 + j, a, me, (*chip, cc), src=x_refs[a]) for j, chip in enumerate(chips) for a in range(n)]

    def begin():
        for cp in mine + first:
            cp.start()

    def end():
        passed = []
        for j, chip in enumerate(chips):
            for a in range(n):
                copy(1 + j, a, (*chip, cc), me).wait_recv()
                passed.append(copy(4 + j, a, (*chip, cc), sibling))
                passed[-1].start()
        for a in range(n):
            copy(0, a, sibling, me).wait_recv()
        for j, chip in enumerate(chips):
            for a in range(n):
                copy(4 + j, a, (*chip, 1 - cc), me).wait_recv()
        for cp in first + passed:
            cp.wait_send()
        for cp in mine:
            cp.wait()

    return begin, end


def _exchange_parts(g_refs, out_refs, b_refs, ball_refs, send_sems, recv_sems, local_sems):
    n, nb = len(g_refs), len(b_refs)
    per_peer = n + nb
    x, y, cc = lax.axis_index("x"), lax.axis_index("y"), lax.axis_index("c")
    my_idx = 4 * x + 2 * y + cc
    mine = [pltpu.make_async_copy(g_refs[a].at[my_idx], out_refs[a].at[my_idx], local_sems.at[a]) for a in range(n)]
    mine += [pltpu.make_async_copy(b_refs[a], ball_refs[a].at[my_idx], local_sems.at[n + a]) for a in range(nb)]
    copies = []
    for k in range(1, N_DEV):
        px = 1 - x if k & 4 else x
        py = 1 - y if k & 2 else y
        pc = 1 - cc if k & 1 else cc
        base = (k - 1) * per_peer
        for a in range(n):
            copies.append(pltpu.make_async_remote_copy(
                src_ref=g_refs[a].at[4 * px + 2 * py + pc], dst_ref=out_refs[a].at[my_idx],
                send_sem=send_sems.at[base + a], recv_sem=recv_sems.at[base + a],
                device_id=(px, py, pc), device_id_type=MESH))
        for a in range(nb):
            copies.append(pltpu.make_async_remote_copy(
                src_ref=b_refs[a], dst_ref=ball_refs[a].at[my_idx], send_sem=send_sems.at[base + n + a],
                recv_sem=recv_sems.at[base + n + a], device_id=(px, py, pc), device_id_type=MESH))

    def begin():
        for cp in mine + copies:
            cp.start()

    def end():
        for cp in copies:
            cp.wait_recv()
        for cp in copies:
            cp.wait_send()
        for cp in mine:
            cp.wait()

    return begin, end


def _pcall_hosting(body, args, exchange, bcast, *, name, grid, in_specs, out_specs, out_shape, compiler_params,
                   scratch_shapes=(), gather=()):
    n_in, n_out, n_x, n_b, n_g = len(in_specs), len(out_specs), len(exchange), len(bcast), len(gather)
    n_d = n_x + n_b
    n_c, n_scr = n_d + n_g, len(scratch_shapes)

    def hosted(*refs):
        ins, c_in = refs[:n_in], refs[n_in:n_in + n_c]
        outs, c_out = refs[n_in + n_c:n_in + n_c + n_out], refs[n_in + n_c + n_out:n_in + 2 * n_c + n_out]
        scratch, sems = refs[n_in + 2 * n_c + n_out:][:n_scr], refs[n_in + 2 * n_c + n_out + n_scr:]
        parts = []
        if n_d:
            parts.append(_exchange_parts(c_in[:n_x], c_out[:n_x], c_in[n_x:n_d], c_out[n_x:n_d], *sems[:3]))
        if n_g:
            parts.append(_gather_parts(c_in[n_d:], c_out[n_d:], *sems[-3:]))
        first = functools.reduce(jnp.logical_and, [pl.program_id(k) == 0 for k in range(len(grid))])
        last = functools.reduce(jnp.logical_and, [pl.program_id(k) == grid[k] - 1 for k in range(len(grid))])
        for begin, _ in parts:
            pl.when(first)(begin)
        body(*ins, *outs, *scratch)
        for _, end in parts:
            pl.when(last)(end)

    landed_shape = [jax.ShapeDtypeStruct(e.shape, e.dtype) for e in exchange]
    landed_shape += [jax.ShapeDtypeStruct((N_DEV,) + b.shape, b.dtype) for b in list(bcast) + list(gather)]
    sems = (_sem_shapes(7 * n_d, n_d) if n_d else []) + (_sem_shapes(7 * n_g, n_g) if n_g else [])
    res = _pcall(hosted, name=name, grid=grid, in_specs=list(in_specs) + [_HBM] * n_c,
                 out_specs=list(out_specs) + [_HBM] * n_c, out_shape=list(out_shape) + landed_shape,
                 scratch_shapes=list(scratch_shapes) + sems,
                 compiler_params=compiler_params)(*args, *exchange, *bcast, *gather)
    return res[:n_out], res[n_out:]


def _all_gather(blks, name):
    n = len(blks)

    def body(*refs):
        begin, end = _gather_parts(refs[:n], refs[n:2 * n], *refs[2 * n:])
        begin()
        end()

    return _pcall(body, name=name, out_shape=[jax.ShapeDtypeStruct((N_DEV,) + b.shape, b.dtype) for b in blks],
                  in_specs=[_HBM] * n, out_specs=[_HBM] * n, scratch_shapes=_sem_shapes(7 * n, n))(*blks)


def _sum_and_adamw(p_ref, w, m, v):
    g = p_ref[0].astype(F32)
    for s in range(1, N_DEV):
        g = g + p_ref[s].astype(F32)
    nm = ADAM_B1 * m + (1.0 - ADAM_B1) * g
    nv = ADAM_B2 * v + (1.0 - ADAM_B2) * (g * g)
    m_hat = nm / (1.0 - ADAM_B1 ** ADAM_STEP)
    v_hat = nv / (1.0 - ADAM_B2 ** ADAM_STEP)
    return g, -ADAM_LR * (m_hat / (jnp.sqrt(v_hat) + ADAM_EPS) + ADAM_WD * w), nm, nv


def _reduce_adamw_vectors(parts, w, m, v, shapes, name):
    table = _row_table(shapes)

    def body(p_ref, w_ref, m_ref, v_ref, *out_refs):
        results = _sum_and_adamw(p_ref, w_ref[...], m_ref[...], v_ref[...])
        for kind, val in enumerate(results):
            for t, (r0, nr, nc) in enumerate(table):
                out_refs[kind * len(table) + t][...] = val[r0:r0 + nr, 0:nc]

    outs = _pcall(body, name=name, out_shape=[jax.ShapeDtypeStruct((nr, nc), F32) for _, nr, nc in table] * 4)(parts, w, m, v)
    return [[outs[kind * len(table) + t].reshape(s) for t, s in enumerate(shapes)] for kind in range(4)]


def _reduce_adamw(parts, w, m, v, name):
    lead = w.shape[:-2]
    r, c = w.shape[-2:]
    tr = 128 if r % 128 == 0 else r
    zeros = (0,) * len(lead)
    at = zeros + (slice(None), slice(None))

    def body(p_ref, w_ref, m_ref, v_ref, g_ref, d_ref, nm_ref, nv_ref):
        g_ref[at], d_ref[at], nm_ref[at], nv_ref[at] = _sum_and_adamw(p_ref, w_ref[at], m_ref[at], v_ref[at])

    row = pl.BlockSpec((1,) * len(lead) + (tr, c), lambda i: zeros + (i, 0))
    out = jax.ShapeDtypeStruct(w.shape, F32)
    return _pcall(
        body, name=name, grid=(r // tr,),
        in_specs=[pl.BlockSpec((N_DEV, tr, c), lambda i: (0, i, 0)), row, row, row],
        out_specs=[row, row, row, row], out_shape=[out, out, out, out], compiler_params=_cp(1),
    )(parts, w, m, v)


def _proj_fwd(x, nw, w, wab, name, gather=()):
    t, npj = x.shape[0], w.shape[1]
    tm, tn = min(512, t), 1024
    has_ab = wab is not None
    n_in, n_out, n_g = 3 + has_ab, 2 + has_ab, len(gather)
    n_i = t // tm

    def body(*refs):
        ins, g_ins = refs[:n_in], refs[n_in:n_in + n_g]
        outs = refs[n_in + n_g:n_in + n_g + n_out]
        g_outs = refs[n_in + n_g + n_out:n_in + 2 * n_g + n_out]
        if has_ab:
            (x_ref, nw_ref, w_ref, wab_ref), (p_ref, h_ref, ab_ref) = ins, outs
        else:
            (x_ref, nw_ref, w_ref), (p_ref, h_ref) = ins, outs
        if n_g:
            begin, end = _gather_parts(g_ins, g_outs, *refs[n_in + 2 * n_g + n_out:])
            pl.when(pl.program_id(0) == 0)(begin)

        xv = x_ref[...]
        hv = (xv * lax.rsqrt(jnp.mean(xv * xv, axis=-1, keepdims=True) + EPS) * nw_ref[...]).astype(MXU)
        h_ref[...] = hv.astype(ACT)
        if has_ab:
            ab_ref[...] = _dot(hv, wab_ref[...], NN)
        for j in range(npj // tn):
            p_ref[:, j * tn:(j + 1) * tn] = _dot(hv, w_ref[:, j * tn:(j + 1) * tn], NN).astype(ACT)
        if n_g:
            pl.when(pl.program_id(0) == n_i - 1)(end)

    in_specs = [pl.BlockSpec((tm, D), lambda i: (i, 0)), _full((1, D)), _full((D, npj))]
    out_specs = [pl.BlockSpec((tm, npj), lambda i: (i, 0)), pl.BlockSpec((tm, D), lambda i: (i, 0))]
    out_shape = [jax.ShapeDtypeStruct((t, npj), ACT), jax.ShapeDtypeStruct((t, D), ACT)]
    args = [x, nw, w]
    if has_ab:
        in_specs.append(_full((D, HEAD)))
        out_specs.append(pl.BlockSpec((tm, HEAD), lambda i: (i, 0)))
        out_shape.append(jax.ShapeDtypeStruct((t, HEAD), F32))
        args.append(wab)
    scratch = []
    if n_g:
        in_specs += [_HBM] * n_g
        out_specs += [_HBM] * n_g
        out_shape += [jax.ShapeDtypeStruct((N_DEV,) + b.shape, b.dtype) for b in gather]
        args += list(gather)
        scratch += _sem_shapes(7 * n_g, n_g)
    return _pcall(body, name=name, grid=(n_i,), in_specs=in_specs, out_specs=out_specs,
                  out_shape=out_shape, scratch_shapes=scratch, compiler_params=_cp(1))(*args)


def _proj_bwd_x(dps, ws, x, dxn, nw, name, exchange=(), bcast=()):
    t = x.shape[0]
    tm = min(512, t)
    n, n_x, n_b = len(dps), len(exchange), len(bcast)
    n_c, n_steps = n_x + n_b, t // tm

    def body(*refs):
        dp_refs, w_refs = refs[:n], refs[n:2 * n]
        x_ref, dxn_ref, nw_ref = refs[2 * n:2 * n + 3]
        c_in = refs[2 * n + 3:2 * n + 3 + n_c]
        dx_ref, dnw_ref = refs[2 * n + 3 + n_c:2 * n + 5 + n_c]
        c_out = refs[2 * n + 5 + n_c:2 * n + 5 + 2 * n_c]
        if n_c:
            begin, end = _exchange_parts(c_in[:n_x], c_out[:n_x], c_in[n_x:], c_out[n_x:], *refs[2 * n + 5 + 2 * n_c:])
            pl.when(pl.program_id(0) == 0)(begin)
        @pl.when(pl.program_id(0) == 0)
        def _():
            dnw_ref[...] = jnp.zeros_like(dnw_ref)

        halves = [slice(r0, r0 + tm // 2) for r0 in (0, tm // 2)]
        dhs = []
        for rows in halves:
            dh = _mdot(dp_refs[0][rows, :], w_refs[0][...], NT)
            for k in range(1, n):
                dh = dh + _mdot(dp_refs[k][rows, :], w_refs[k][...], NT)
            dhs.append(dh)
        for rows, dh in zip(halves, dhs):
            xv = x_ref[rows, :]
            r = lax.rsqrt(jnp.mean(xv * xv, axis=-1, keepdims=True) + EPS)
            dnw_ref[...] += jnp.sum(dh * xv * r, axis=0, keepdims=True)
            dhw = dh * nw_ref[...]
            dx_ref[rows, :] = dxn_ref[rows, :] + r * (dhw - xv * (r * r) * jnp.mean(dhw * xv, axis=-1, keepdims=True))
        if n_c:
            pl.when(pl.program_id(0) == n_steps - 1)(end)

    row = pl.BlockSpec((tm, D), lambda i: (i, 0))
    in_specs = [pl.BlockSpec((tm, dp.shape[1]), lambda i: (i, 0)) for dp in dps]
    in_specs += [_full(w.shape) for w in ws]
    in_specs += [row, row, _full((1, D))] + [_HBM] * n_c
    out_shape = [jax.ShapeDtypeStruct((t, D), F32), jax.ShapeDtypeStruct((1, D), F32)]
    out_shape += [jax.ShapeDtypeStruct(e.shape, e.dtype) for e in exchange]
    out_shape += [jax.ShapeDtypeStruct((N_DEV,) + b.shape, b.dtype) for b in bcast]
    return _pcall(body, name=name, grid=(n_steps,), in_specs=in_specs, out_specs=[row, _full((1, D))] + [_HBM] * n_c,
                  out_shape=out_shape, scratch_shapes=_sem_shapes(7 * n_c, n_c) if n_c else [],
                  compiler_params=_cp(1))(*dps, *ws, x, dxn, nw, *exchange, *bcast)


def _matmul_tn(a, b, name, out_dtype=XCH, exchange=(), bcast=()):
    t, m = a.shape
    n = b.shape[1]
    tm, tn = min(1024, t), min(1024, n)
    n_t = t // tm

    def body(a_ref, b_ref, o_ref, acc):
        @pl.when(pl.program_id(1) == 0)
        def _():
            acc[...] = jnp.zeros_like(acc)

        acc[...] += _mdot(a_ref[...], b_ref[...], TN)

        @pl.when(pl.program_id(1) == n_t - 1)
        def _():
            o_ref[...] = acc[...].astype(out_dtype)

    call = dict(name=name, grid=(n // tn, n_t),
                in_specs=[pl.BlockSpec((tm, m), lambda j, i: (i, 0)), pl.BlockSpec((tm, tn), lambda j, i: (i, j))],
                out_specs=[pl.BlockSpec((m, tn), lambda j, i: (0, j))],
                out_shape=[jax.ShapeDtypeStruct((m, n), out_dtype)], scratch_shapes=[pltpu.VMEM((m, tn), F32)],
                compiler_params=_cp(2))
    if not exchange and not bcast:
        return _pcall(body, **call)(a, b)[0]
    (out,), landed = _pcall_hosting(body, (a, b), exchange, bcast, **call)
    return out, landed


def _memkv_fwd(mem, mw, wkv):
    def body(mem_ref, mw_ref, w_ref, kv_ref):
        mv = mem_ref[...]
        mn = mv * lax.rsqrt(jnp.mean(mv * mv, axis=-1, keepdims=True) + EPS) * mw_ref[...]
        kv_ref[...] = _mdot(mn, w_ref[...], NN)

    return _pcall(body, name="memkv_fwd", out_shape=jax.ShapeDtypeStruct((N_MEM, 2 * D_XA), F32),
                  compiler_params=pltpu.CompilerParams(vmem_limit_bytes=VMEM_LIMIT))(mem, mw, wkv)


def _memkv_bwd(mem, mw, wkv, dkv):
    def body(mem_ref, mw_ref, w_ref, dkv_ref, dw_ref, dmw_ref):
        mv = mem_ref[...]
        r = lax.rsqrt(jnp.mean(mv * mv, axis=-1, keepdims=True) + EPS)
        mn = mv * r * mw_ref[...]
        dkvv = dkv_ref[...]
        dw_ref[...] = _mdot(mn, dkvv, TN)
        dmn = _mdot(dkvv, w_ref[...], NT)
        dmw_ref[...] = jnp.sum(dmn * mv * r, axis=0, keepdims=True)

    return _pcall(body, name="memkv_bwd",
                  out_shape=[jax.ShapeDtypeStruct((D, 2 * D_XA), F32), jax.ShapeDtypeStruct((1, D), F32)],
                  compiler_params=pltpu.CompilerParams(vmem_limit_bytes=VMEM_LIMIT))(mem, mw, wkv, dkv)


def _attend(q, kv):
    heads = range(XA_HEADS)
    qs = [q[:, h * HEAD:(h + 1) * HEAD] for h in heads]
    ks = [kv[:, h * HEAD:(h + 1) * HEAD] for h in heads]
    vs = [kv[:, D_XA + h * HEAD:D_XA + (h + 1) * HEAD] for h in heads]
    ss = [_mdot(qs[h], ks[h], NT) * (HEAD ** -0.5) for h in heads]
    es = [jnp.exp(s - jnp.max(s, axis=-1, keepdims=True)) for s in ss]
    ps = [e / jnp.sum(e, axis=-1, keepdims=True) for e in es]
    return ps, [_mdot(ps[h], vs[h], NN) for h in heads]


def _tail_fwd(p, ymix, x, kv, wout, npost, name, gather=()):
    t = x.shape[0]
    tm = min(512, t)

    def body(z_ref, q_ref, y_ref, x_ref, kv_ref, w_ref, np_ref, xn_ref, o_ref):
        _, outs = _attend(q_ref[...], kv_ref[...])
        cat = jnp.concatenate([y_ref[...]] + [a.astype(ACT) for a in outs], axis=1)
        g = cat * _silu(z_ref[...])
        o = _mdot(g, w_ref[...], NN)
        o_ref[...] = o
        xn_ref[...] = x_ref[...] + o * lax.rsqrt(jnp.mean(o * o, axis=-1, keepdims=True) + EPS) * np_ref[...]

    row = pl.BlockSpec((tm, D), lambda i: (i, 0))
    call = dict(
        name=name, grid=(t // tm,),
        in_specs=[pl.BlockSpec((tm, D_CAT), lambda i: (i, 0)), pl.BlockSpec((tm, D_XA), lambda i: (i, D_CAT // D_XA)),
                  row, row, _full((N_MEM, 2 * D_XA)), _full((D_CAT, D)), _full((1, D))],
        out_specs=[row, row],
        out_shape=[jax.ShapeDtypeStruct((t, D), F32), jax.ShapeDtypeStruct((t, D), F32)], compiler_params=_cp(1))
    args = (p, p, ymix, x, kv, wout, npost)
    if not gather:
        return tuple(_pcall(body, **call)(*args))
    outs, landed = _pcall_hosting(body, args, (), (), gather=gather, **call)
    return tuple(outs) + tuple(landed)


def _tail_bwd(dxn, o, p, ymix, kv, wout, npost, name, exchange=()):
    t = dxn.shape[0]
    tm = min(512, t)
    n_x, n_steps = len(exchange), t // tm

    def body(*refs):
        dxn_ref, o_ref, z_ref, q_ref, y_ref, kv_ref, w_ref, np_ref = refs[:8]
        dzq_ref, dy_ref, dw_ref, dnp_ref, dkv_ref = refs[8 + n_x:13 + n_x]
        if n_x:
            begin, end = _exchange_parts(refs[8:8 + n_x], refs[13 + n_x:13 + 2 * n_x], (), (), *refs[13 + 2 * n_x:])
            pl.when(pl.program_id(0) == 0)(begin)

        @pl.when(pl.program_id(0) == 0)
        def _():
            dw_ref[...] = jnp.zeros_like(dw_ref)
            dnp_ref[...] = jnp.zeros_like(dnp_ref)
            dkv_ref[...] = jnp.zeros_like(dkv_ref)

        q = q_ref[...]
        kvv = kv_ref[...]
        z = z_ref[...]
        ps, outs = _attend(q, kvv)
        cat = jnp.concatenate([y_ref[...]] + [a.astype(ACT) for a in outs], axis=1)
        sz = _silu(z)
        g = cat * sz
        ov = o_ref[...]
        dr = dxn_ref[...]
        rr = lax.rsqrt(jnp.mean(ov * ov, axis=-1, keepdims=True) + EPS)
        dnp_ref[...] += jnp.sum(dr * ov * rr, axis=0, keepdims=True)
        dow = dr * np_ref[...]
        do = rr * (dow - ov * (rr * rr) * jnp.mean(dow * ov, axis=-1, keepdims=True))
        dg = _mdot(do, w_ref[...], NT).astype(ACT)
        dw_ref[...] += _mdot(g, do, TN)
        dcat = dg * sz
        dzq_ref[:, 0:D_CAT] = dg * cat * _dsilu(z)
        dy_ref[...] = dcat[:, 0:D]
        heads = range(XA_HEADS)
        dohs = [dcat[:, D + h * HEAD:D + (h + 1) * HEAD] for h in heads]
        dps = [_mdot(dohs[h], kvv[:, D_XA + h * HEAD:D_XA + (h + 1) * HEAD], NT) for h in heads]
        dss = [ps[h] * (dps[h] - jnp.sum(dps[h] * ps[h], axis=-1, keepdims=True)) for h in heads]
        dqs = [_mdot(dss[h], kvv[:, h * HEAD:(h + 1) * HEAD], NN) * (HEAD ** -0.5) for h in heads]
        dks = [_mdot(dss[h], q[:, h * HEAD:(h + 1) * HEAD], TN) * (HEAD ** -0.5) for h in heads]
        dvs = [_mdot(ps[h], dohs[h], TN) for h in heads]
        for h in heads:
            lo = h * HEAD
            dzq_ref[:, D_CAT + lo:D_CAT + lo + HEAD] = dqs[h].astype(ACT)
            dkv_ref[:, lo:lo + HEAD] += dks[h]
            dkv_ref[:, D_XA + lo:D_XA + lo + HEAD] += dvs[h]
        if n_x:
            pl.when(pl.program_id(0) == n_steps - 1)(end)

    row = pl.BlockSpec((tm, D), lambda i: (i, 0))
    return _pcall(
        body, name=name, grid=(n_steps,),
        in_specs=[row, row, pl.BlockSpec((tm, D_CAT), lambda i: (i, 0)),
                  pl.BlockSpec((tm, D_XA), lambda i: (i, D_CAT // D_XA)), row,
                  _full((N_MEM, 2 * D_XA)), _full((D_CAT, D)), _full((1, D))] + [_HBM] * n_x,
        out_specs=[pl.BlockSpec((tm, ZQ), lambda i: (i, 0)), row, _full((D_CAT, D)), _full((1, D)),
                   _full((N_MEM, 2 * D_XA))] + [_HBM] * n_x,
        out_shape=[jax.ShapeDtypeStruct((t, ZQ), ACT), jax.ShapeDtypeStruct((t, D), ACT),
                   jax.ShapeDtypeStruct((D_CAT, D), F32), jax.ShapeDtypeStruct((1, D), F32),
                   jax.ShapeDtypeStruct((N_MEM, 2 * D_XA), F32)]
        + [jax.ShapeDtypeStruct(e.shape, e.dtype) for e in exchange],
        scratch_shapes=_sem_shapes(7 * n_x, n_x) if n_x else [], compiler_params=_cp(1),
    )(dxn, o, p, p, ymix, kv, wout, npost, *exchange)


def _gmlp_chunk(us, vs, lnws, lnbs, wss, bss):
    gv = [jax.nn.gelu(v) for v in vs]
    mean = sum(jnp.sum(v, axis=-1, keepdims=True) for v in gv) / D
    cen = [v - mean for v in gv]
    var = sum(jnp.sum(c * c, axis=-1, keepdims=True) for c in cen) / D
    rstd = lax.rsqrt(var + EPS)
    row = lax.broadcasted_iota(jnp.int32, (GM_CHUNK, GM_CHUNK), 0)
    col = lax.broadcasted_iota(jnp.int32, (GM_CHUNK, GM_CHUNK), 1)
    ys = []
    for g in range(GM_GROUPS):
        vn = cen[g] * rstd * lnws[g] + lnbs[g]
        sp = _nn(jnp.where(row >= col, wss[g], 0.0), vn) + bss[g]
        ys.append(jax.nn.gelu(us[g]) * sp)
    return ys


def _split_cols(v, n, width=HEAD):
    return [v[:, k * width:(k + 1) * width] for k in range(n)]


def _gmlp_operands(u_ref, v_ref, lnw_ref, lnb_ref, ws_ref, bst_ref, r0):
    us = _split_cols(u_ref[r0:r0 + GM_CHUNK, :].astype(F32), GM_GROUPS)
    vs = _split_cols(v_ref[r0:r0 + GM_CHUNK, :].astype(F32), GM_GROUPS)
    lnws = _split_cols(lnw_ref[...], GM_GROUPS)
    lnbs = _split_cols(lnb_ref[...], GM_GROUPS)
    wss = [ws_ref[g] for g in range(GM_GROUPS)]
    bst = bst_ref[...]
    bss = [jnp.sum(bst * _onehot_row(HEAD, g), axis=1, keepdims=True) for g in range(GM_GROUPS)]
    return us, vs, lnws, lnbs, wss, bss


def _gmlp_rows(t):
    return min(GM_CHUNKS_PER_STEP, t // GM_CHUNK) * GM_CHUNK


def _gmlp_specs(rows):
    return [pl.BlockSpec((rows, D), lambda i: (i, ZQ // D)), pl.BlockSpec((rows, D), lambda i: (i, ZQ // D + 1)),
            _full((1, D)), _full((1, D)), _full((GM_GROUPS, GM_CHUNK, GM_CHUNK)), _full((GM_CHUNK, HEAD))]


def _gmlp_fwd(p, lnw, lnb, ws, bst, name):
    t = p.shape[0]
    rows = _gmlp_rows(t)

    def body(u_ref, v_ref, lnw_ref, lnb_ref, ws_ref, bst_ref, y_ref):
        for r0 in range(0, rows, GM_CHUNK):
            ys = _gmlp_chunk(*_gmlp_operands(u_ref, v_ref, lnw_ref, lnb_ref, ws_ref, bst_ref, r0))
            for g in range(GM_GROUPS):
                y_ref[r0:r0 + GM_CHUNK, g * HEAD:(g + 1) * HEAD] = ys[g].astype(ACT)

    return _pcall(body, name=name, grid=(t // rows,), in_specs=_gmlp_specs(rows),
                  out_specs=pl.BlockSpec((rows, D), lambda i: (i, 0)),
                  out_shape=jax.ShapeDtypeStruct((t, D), ACT), compiler_params=_cp(1))(p, p, lnw, lnb, ws, bst)


def _gmlp_bwd(p, dy, lnw, lnb, ws, bst, name, exchange=(), bcast=()):
    t = p.shape[0]
    rows = _gmlp_rows(t)

    def body(u_ref, v_ref, lnw_ref, lnb_ref, ws_ref, bst_ref, dy_ref, duv_ref, dlnw_ref, dlnb_ref, dws_ref, dbst_ref):
        @pl.when(pl.program_id(0) == 0)
        def _():
            dlnw_ref[...] = jnp.zeros_like(dlnw_ref)
            dlnb_ref[...] = jnp.zeros_like(dlnb_ref)
            dws_ref[...] = jnp.zeros_like(dws_ref)
            dbst_ref[...] = jnp.zeros_like(dbst_ref)

        for r0 in range(0, rows, GM_CHUNK):
            ops = _gmlp_operands(u_ref, v_ref, lnw_ref, lnb_ref, ws_ref, bst_ref, r0)
            _, vjp = jax.vjp(_gmlp_chunk, *ops)
            dus, dvs, dlnws, dlnbs, dwss, dbss = vjp(_split_cols(dy_ref[r0:r0 + GM_CHUNK, :].astype(F32), GM_GROUPS))
            dbst = jnp.zeros((GM_CHUNK, HEAD), F32)
            for g in range(GM_GROUPS):
                lo = g * HEAD
                duv_ref[r0:r0 + GM_CHUNK, lo:lo + HEAD] = dus[g].astype(ACT)
                duv_ref[r0:r0 + GM_CHUNK, D + lo:D + lo + HEAD] = dvs[g].astype(ACT)
                dlnw_ref[:, lo:lo + HEAD] += dlnws[g]
                dlnb_ref[:, lo:lo + HEAD] += dlnbs[g]
                dws_ref[g] += dwss[g]
                dbst = dbst + dbss[g] * _onehot_row(HEAD, g)
            dbst_ref[...] += dbst

    call = dict(
        name=name, grid=(t // rows,),
        in_specs=_gmlp_specs(rows) + [pl.BlockSpec((rows, D), lambda i: (i, 0))],
        out_specs=[pl.BlockSpec((rows, 2 * D), lambda i: (i, 0)), _full((1, D)), _full((1, D)),
                   _full((GM_GROUPS, GM_CHUNK, GM_CHUNK)), _full((GM_CHUNK, HEAD))],
        out_shape=[jax.ShapeDtypeStruct((t, 2 * D), ACT), jax.ShapeDtypeStruct((1, D), F32),
                   jax.ShapeDtypeStruct((1, D), F32), jax.ShapeDtypeStruct((GM_GROUPS, GM_CHUNK, GM_CHUNK), F32),
                   jax.ShapeDtypeStruct((GM_CHUNK, HEAD), F32)],
        compiler_params=_cp(1))
    args = (p, p, lnw, lnb, ws, bst, dy)
    if not exchange and not bcast:
        return _pcall(body, **call)(*args), ()
    return _pcall_hosting(body, args, exchange, bcast, **call)


def _prev_halo(tm, col):
    return pl.BlockSpec((HALO, D), lambda i: (jnp.maximum(i * (tm // HALO) - 1, 0), col))


def _next_halo(tm, col, n_tiles):
    return pl.BlockSpec((HALO, D), lambda i: (jnp.minimum(i + 1, n_tiles - 1) * (tm // HALO), col))


def _taps_back(ext, w, width):
    acc = None
    for k in range(width):
        s = width - 1 - k
        term = w[k:k + 1, :] * (pltpu.roll(ext, s, 0) if s else ext)[HALO:, :]
        acc = term if acc is None else acc + term
    return acc


def _taps_fwd(ext, w, width, n):
    rows = ext.shape[0]
    acc = None
    for k in range(width):
        s = width - 1 - k
        term = w[k:k + 1, :] * (pltpu.roll(ext, rows - s, 0) if s else ext)[0:n, :]
        acc = term if acc is None else acc + term
    return acc


def _sconv_fwd(p, cw, name):
    t = p.shape[0]
    tm = min(256, t)

    def body(b_ref, c_ref, h_ref, cp_ref, hp_ref, w_ref, y_ref):
        first = pl.program_id(0) == 0
        prev = jnp.where(first, 0.0, cp_ref[...].astype(F32) * hp_ref[...].astype(F32))
        ext = jnp.concatenate([prev, c_ref[...].astype(F32) * h_ref[...].astype(F32)], axis=0)
        y_ref[...] = (b_ref[...].astype(F32) * _taps_back(ext, w_ref[...], 3)).astype(ACT)

    c0 = ZQ // D
    tile = [pl.BlockSpec((tm, D), lambda i, c=c: (i, c)) for c in (c0, c0 + 1, c0 + 2)]
    return _pcall(body, name=name, grid=(t // tm,),
                  in_specs=tile + [_prev_halo(tm, c0 + 1), _prev_halo(tm, c0 + 2), _full((HALO, D))],
                  out_specs=pl.BlockSpec((tm, D), lambda i: (i, 0)),
                  out_shape=jax.ShapeDtypeStruct((t, D), ACT), compiler_params=_cp(1))(p, p, p, p, p, cw)


def _sconv_bwd(p, dy, cw, name):
    t = p.shape[0]
    tm = min(256, t)
    n_tiles = t // tm

    def body(b_ref, c_ref, h_ref, cp_ref, hp_ref, bn_ref, dy_ref, dyn_ref, w_ref, d_ref, dw_ref):
        i = pl.program_id(0)

        @pl.when(i == 0)
        def _():
            dw_ref[...] = jnp.zeros_like(dw_ref)

        w = w_ref[...]
        bv, cv, hv = b_ref[...].astype(F32), c_ref[...].astype(F32), h_ref[...].astype(F32)
        dyv = dy_ref[...].astype(F32)
        prev = jnp.where(i == 0, 0.0, cp_ref[...].astype(F32) * hp_ref[...].astype(F32))
        ext = jnp.concatenate([prev, cv * hv], axis=0)
        conv = _taps_back(ext, w, 3)
        dconv = dyv * bv
        nxt = jnp.where(i == n_tiles - 1, 0.0, dyn_ref[...].astype(F32) * bn_ref[...].astype(F32))
        dc = _taps_fwd(jnp.concatenate([dconv, nxt], axis=0), w, 3, tm)
        d_ref[:, 0:D] = (dyv * conv).astype(ACT)
        d_ref[:, D:2 * D] = (dc * hv).astype(ACT)
        d_ref[:, 2 * D:3 * D] = (dc * cv).astype(ACT)
        for k in range(3):
            s = 2 - k
            shifted = (pltpu.roll(ext, s, 0) if s else ext)[HALO:, :]
            dw_ref[k:k + 1, :] += jnp.sum(dconv * shifted, axis=0, keepdims=True)

    c0 = ZQ // D
    tile = [pl.BlockSpec((tm, D), lambda i, c=c: (i, c)) for c in (c0, c0 + 1, c0 + 2)]
    return _pcall(
        body, name=name, grid=(n_tiles,),
        in_specs=tile + [_prev_halo(tm, c0 + 1), _prev_halo(tm, c0 + 2), _next_halo(tm, c0, n_tiles),
                         pl.BlockSpec((tm, D), lambda i: (i, 0)), _next_halo(tm, 0, n_tiles), _full((HALO, D))],
        out_specs=[pl.BlockSpec((tm, 3 * D), lambda i: (i, 0)), _full((HALO, D))],
        out_shape=[jax.ShapeDtypeStruct((t, 3 * D), ACT), jax.ShapeDtypeStruct((HALO, D), F32)],
        compiler_params=_cp(1),
    )(p, p, p, p, p, p, dy, dy, cw)


def _l2_heads(s, scale):
    outs, rs = [], []
    for hh in range(DN_HEADS):
        blk = s[:, hh * HEAD:(hh + 1) * HEAD]
        r = lax.rsqrt(jnp.sum(blk * blk, axis=-1, keepdims=True) + EPS)
        outs.append(blk * (r * scale))
        rs.append(r)
    return outs, rs


_QKV_SCALE = (HEAD ** -0.5, 1.0, None)


def _qkv_fwd(p, cw, name):
    t = p.shape[0]
    tm = min(256, t)

    def body(q_ref, k_ref, v_ref, qp_ref, kp_ref, vp_ref, w_ref, o_ref):
        first = pl.program_id(0) == 0
        for part, (ref, pref) in enumerate(((q_ref, qp_ref), (k_ref, kp_ref), (v_ref, vp_ref))):
            prev = jnp.where(first, 0.0, pref[...].astype(F32))
            ext = jnp.concatenate([prev, ref[...].astype(F32)], axis=0)
            s = _silu(_taps_back(ext, w_ref[:, part * D:(part + 1) * D], 4))
            if _QKV_SCALE[part] is None:
                o_ref[:, part * D:(part + 1) * D] = s.astype(ACT)
            else:
                outs, _ = _l2_heads(s, _QKV_SCALE[part])
                for hh in range(DN_HEADS):
                    o_ref[:, part * D + hh * HEAD:part * D + (hh + 1) * HEAD] = outs[hh].astype(ACT)

    c0 = ZQ // D
    tile = [pl.BlockSpec((tm, D), lambda i, c=c: (i, c)) for c in (c0, c0 + 1, c0 + 2)]
    halo = [_prev_halo(tm, c) for c in (c0, c0 + 1, c0 + 2)]
    return _pcall(body, name=name, grid=(t // tm,), in_specs=tile + halo + [_full((HALO, 3 * D))],
                  out_specs=pl.BlockSpec((tm, 3 * D), lambda i: (i, 0)),
                  out_shape=jax.ShapeDtypeStruct((t, 3 * D), ACT), compiler_params=_cp(1))(p, p, p, p, p, p, cw)


def _qkv_bwd(p, dqkv, cw, name):
    t = p.shape[0]
    tm = min(256, t)
    n_tiles = t // tm

    def body(*refs):
        tiles, prevs, nexts = refs[0:3], refs[3:6], refs[6:9]
        d_tiles, d_nexts = refs[9:12], refs[12:15]
        w_ref, o_ref, dw_ref = refs[15:]
        i = pl.program_id(0)

        @pl.when(i == 0)
        def _():
            dw_ref[...] = jnp.zeros_like(dw_ref)

        for part in range(3):
            w = w_ref[:, part * D:(part + 1) * D]
            prev = jnp.where(i == 0, 0.0, prevs[part][...].astype(F32))
            ext = jnp.concatenate([prev, tiles[part][...].astype(F32), nexts[part][...].astype(F32)], axis=0)
            xc = _taps_back(ext, w, 4)
            dout = jnp.concatenate([d_tiles[part][...].astype(F32), d_nexts[part][...].astype(F32)], axis=0)
            s = _silu(xc)
            if _QKV_SCALE[part] is None:
                ds = dout
            else:
                scale = _QKV_SCALE[part]
                pieces = []
                for hh in range(DN_HEADS):
                    blk = s[:, hh * HEAD:(hh + 1) * HEAD]
                    dblk = dout[:, hh * HEAD:(hh + 1) * HEAD]
                    r = lax.rsqrt(jnp.sum(blk * blk, axis=-1, keepdims=True) + EPS)
                    pieces.append(scale * r * (dblk - blk * (r * r) * jnp.sum(dblk * blk, axis=-1, keepdims=True)))
                ds = jnp.concatenate(pieces, axis=1)
            dxc = ds * _dsilu(xc)
            row = lax.broadcasted_iota(jnp.int32, (tm + HALO, 1), 0)
            dxc = jnp.where(jnp.logical_and(i == n_tiles - 1, row >= tm), 0.0, dxc)
            o_ref[:, part * D:(part + 1) * D] = _taps_fwd(dxc, w, 4, tm).astype(ACT)
            for k in range(4):
                sh = 3 - k
                shifted = (pltpu.roll(ext, sh, 0) if sh else ext)[HALO:HALO + tm, :]
                dw_ref[k:k + 1, part * D:(part + 1) * D] += jnp.sum(dxc[0:tm, :] * shifted, axis=0, keepdims=True)

    c0 = ZQ // D
    cols = (c0, c0 + 1, c0 + 2)
    tile = [pl.BlockSpec((tm, D), lambda i, c=c: (i, c)) for c in cols]
    dtile = [pl.BlockSpec((tm, D), lambda i, c=c: (i, c)) for c in range(3)]
    in_specs = (tile + [_prev_halo(tm, c) for c in cols] + [_next_halo(tm, c, n_tiles) for c in cols]
                + dtile + [_next_halo(tm, c, n_tiles) for c in range(3)] + [_full((HALO, 3 * D))])
    return _pcall(
        body, name=name, grid=(n_tiles,), in_specs=in_specs,
        out_specs=[pl.BlockSpec((tm, 3 * D), lambda i: (i, 0)), _full((HALO, 3 * D))],
        out_shape=[jax.ShapeDtypeStruct((t, 3 * D), ACT), jax.ShapeDtypeStruct((HALO, 3 * D), F32)],
        compiler_params=_cp(1),
    )(*([p] * 9), *([dqkv] * 6), cw)


def _tri_masks(n):
    row = lax.broadcasted_iota(jnp.int32, (n, n), 0)
    col = lax.broadcasted_iota(jnp.int32, (n, n), 1)
    return row, col


@jax.custom_vjp
def _unit_lower_inverses(mats):
    n = DN_CHUNK
    row, col = _tri_masks(n)
    eye = (row == col).astype(F32)
    same16 = (row // 16) == (col // 16)
    same32 = (row // 32) == (col // 32)
    pw = [jnp.where(same16, a, 0.0) for a in mats]
    x = [eye - p for p in pw]
    for _ in range(3):
        pw = [_hnn(p, p) for p in pw]
        x = [_hnn(xi, eye + p) for xi, p in zip(x, pw)]
    for keep in (jnp.logical_and(same32, jnp.logical_not(same16)), jnp.logical_not(same32)):
        inner = [_hnn(jnp.where(keep, a, 0.0), xi) for a, xi in zip(mats, x)]
        x = [xi - _hnn(xi, y) for xi, y in zip(x, inner)]
    return tuple(x)


def _uli_fwd(mats):
    t = _unit_lower_inverses(mats)
    return t, t


def _uli_bwd(ts, gs):
    inner = [_nt(g, t) for g, t in zip(gs, ts)]
    return (tuple(-_tn(t, y) for t, y in zip(ts, inner)),)


_unit_lower_inverses.defvjp(_uli_fwd, _uli_bwd)


@jax.custom_vjp
def _known_inverses(mats, ts):
    return ts


_known_inverses.defvjp(lambda mats, ts: (ts, ts),
                       lambda ts, gs: (_uli_bwd(ts, gs)[0], tuple(jnp.zeros_like(t) for t in ts)))


def _pick_col(m, k):
    return jnp.sum(m * _onehot_row(m.shape[1], k), axis=1, keepdims=True)


def _pick_row(m, k):
    hot = (lax.broadcasted_iota(jnp.int32, (m.shape[0], 1), 0) == k).astype(F32)
    return jnp.sum(m * hot, axis=0, keepdims=True)


def _delta_chunk(states, qs, ks, vs, ab, alog, dtb, onw, known_inverses=None):
    n = DN_CHUNK
    heads = range(DN_HEADS)
    row, col = _tri_masks(n)
    incl = row >= col
    lane = lax.broadcasted_iota(jnp.int32, (1, HEAD), 1)
    g_all = jnp.where(lane < DN_HEADS, -jnp.exp(alog) * jax.nn.softplus(ab + dtb), 0.0)
    c_cols = _hnn(incl.astype(F32), g_all)
    c_rows = _htn(g_all, (row <= col).astype(F32))
    g_tot = jnp.sum(g_all, axis=0, keepdims=True)
    beta_all = jax.nn.sigmoid(ab)
    ccol = [_pick_col(c_cols, h) for h in heads]
    crow = [_pick_row(c_rows, h) for h in heads]
    gl = [_pick_col(g_tot, h) for h in heads]
    beta = [_pick_col(beta_all, DN_HEADS + h) for h in heads]
    decay = [jnp.exp(jnp.where(incl, ccol[h] - crow[h], -1e30)) for h in heads]
    eg = [jnp.exp(ccol[h]) for h in heads]
    kb = [ks[h] * beta[h] for h in heads]
    amat = [jnp.where(row > col, _nt(kb[h], ks[h]) * decay[h], 0.0) for h in heads]
    if known_inverses is None:
        tmat = _unit_lower_inverses(tuple(amat))
    else:
        tmat = _known_inverses(tuple(amat), tuple(known_inverses))
    u = [_nn(tmat[h], vs[h] * beta[h]) for h in heads]
    w = [_nn(tmat[h], kb[h] * eg[h]) for h in heads]
    qk = [_nt(qs[h], ks[h]) * decay[h] for h in heads]
    v_new = [u[h] - _nn(w[h], states[h]) for h in heads]
    o = [_nn(qs[h] * eg[h], states[h]) + _nn(qk[h], v_new[h]) for h in heads]
    new_states = [states[h] * jnp.exp(gl[h]) + _tn(ks[h] * jnp.exp(gl[h] - ccol[h]), v_new[h]) for h in heads]
    ys = [o[h] * lax.rsqrt(jnp.mean(o[h] * o[h], axis=-1, keepdims=True) + EPS) * onw for h in heads]
    return (ys, new_states), tmat


def _head_cols(ref, r0):
    return [ref[r0:r0 + DN_CHUNK, h * HEAD:(h + 1) * HEAD].astype(F32) for h in range(DN_HEADS)]


def _delta_rows(t):
    return min(DN_CHUNKS_PER_STEP, t // DN_CHUNK) * DN_CHUNK


def _delta_fwd(qkv, ab, alog, dtb, onw, name):
    t = qkv.shape[0]
    rows = _delta_rows(t)
    per_step = rows // DN_CHUNK

    def body(q_ref, k_ref, v_ref, ab_ref, alog_ref, dtb_ref, onw_ref, y_ref, keep_ref, inv_ref, state):
        @pl.when(pl.program_id(0) == 0)
        def _():
            state[...] = jnp.zeros_like(state)

        s = [state[hh] for hh in range(DN_HEADS)]
        for c in range(per_step):
            r0 = c * DN_CHUNK
            (ys, s1), tmat = _delta_chunk(s, _head_cols(q_ref, r0), _head_cols(k_ref, r0), _head_cols(v_ref, r0),
                                          ab_ref[r0:r0 + DN_CHUNK, :], alog_ref[...], dtb_ref[...], onw_ref[...])
            for hh in range(DN_HEADS):
                keep_ref[c, hh] = s[hh]
                inv_ref[c, hh] = tmat[hh]
                y_ref[r0:r0 + DN_CHUNK, hh * HEAD:(hh + 1) * HEAD] = ys[hh].astype(ACT)
            s = s1
        for hh in range(DN_HEADS):
            state[hh] = s[hh]

    block = [pl.BlockSpec((rows, D), lambda i, c=c: (i, c)) for c in range(3)]
    nc = t // DN_CHUNK
    return _pcall(
        body, name=name, grid=(t // rows,),
        in_specs=block + [pl.BlockSpec((rows, HEAD), lambda i: (i, 0)), _full((1, HEAD)), _full((1, HEAD)),
                          _full((1, HEAD))],
        out_specs=[pl.BlockSpec((rows, D), lambda i: (i, 0)),
                   pl.BlockSpec((per_step, DN_HEADS, HEAD, HEAD), lambda i: (i, 0, 0, 0)),
                   pl.BlockSpec((per_step, DN_HEADS, DN_CHUNK, DN_CHUNK), lambda i: (i, 0, 0, 0))],
        out_shape=[jax.ShapeDtypeStruct((t, D), ACT), jax.ShapeDtypeStruct((nc, DN_HEADS, HEAD, HEAD), F32),
                   jax.ShapeDtypeStruct((nc, DN_HEADS, DN_CHUNK, DN_CHUNK), F32)],
        scratch_shapes=[pltpu.VMEM((DN_HEADS, HEAD, HEAD), F32)], compiler_params=_cp(1),
    )(qkv, qkv, qkv, ab, alog, dtb, onw)


def _delta_bwd(qkv, ab, alog, dtb, onw, keep, inv, dy, name):
    t = qkv.shape[0]
    rows = _delta_rows(t)
    per_step, n_steps = rows // DN_CHUNK, t // rows

    def body(q_ref, k_ref, v_ref, ab_ref, alog_ref, dtb_ref, onw_ref, keep_ref, inv_ref, dy_ref,
             dqkv_ref, dab_ref, dalog_ref, ddtb_ref, donw_ref, dstate):
        @pl.when(pl.program_id(0) == 0)
        def _():
            dstate[...] = jnp.zeros_like(dstate)
            dalog_ref[...] = jnp.zeros_like(dalog_ref)
            ddtb_ref[...] = jnp.zeros_like(ddtb_ref)
            donw_ref[...] = jnp.zeros_like(donw_ref)

        ds = [dstate[hh] for hh in range(DN_HEADS)]
        for c in reversed(range(per_step)):
            r0 = c * DN_CHUNK
            s0 = [keep_ref[c, hh] for hh in range(DN_HEADS)]
            known = [inv_ref[c, hh] for hh in range(DN_HEADS)]
            _, vjp, _ = jax.vjp(functools.partial(_delta_chunk, known_inverses=known), s0, _head_cols(q_ref, r0),
                                _head_cols(k_ref, r0), _head_cols(v_ref, r0), ab_ref[r0:r0 + DN_CHUNK, :], alog_ref[...],
                                dtb_ref[...], onw_ref[...], has_aux=True)
            ds, dq, dk, dv, dab, dal, ddt, don = vjp((_head_cols(dy_ref, r0), ds))
            for hh in range(DN_HEADS):
                lo = hh * HEAD
                dqkv_ref[r0:r0 + DN_CHUNK, lo:lo + HEAD] = dq[hh].astype(ACT)
                dqkv_ref[r0:r0 + DN_CHUNK, D + lo:D + lo + HEAD] = dk[hh].astype(ACT)
                dqkv_ref[r0:r0 + DN_CHUNK, 2 * D + lo:2 * D + lo + HEAD] = dv[hh].astype(ACT)
            dab_ref[r0:r0 + DN_CHUNK, :] = dab
            dalog_ref[...] += dal
            ddtb_ref[...] += ddt
            donw_ref[...] += don
        for hh in range(DN_HEADS):
            dstate[hh] = ds[hh]

    rev = lambda i: n_steps - 1 - i
    block = [pl.BlockSpec((rows, D), lambda i, c=c: (rev(i), c)) for c in range(3)]
    small = jax.ShapeDtypeStruct((1, HEAD), F32)
    return _pcall(
        body, name=name, grid=(n_steps,),
        in_specs=block + [pl.BlockSpec((rows, HEAD), lambda i: (rev(i), 0)), _full((1, HEAD)), _full((1, HEAD)),
                          _full((1, HEAD)), pl.BlockSpec((per_step, DN_HEADS, HEAD, HEAD), lambda i: (rev(i), 0, 0, 0)),
                          pl.BlockSpec((per_step, DN_HEADS, DN_CHUNK, DN_CHUNK), lambda i: (rev(i), 0, 0, 0)),
                          pl.BlockSpec((rows, D), lambda i: (rev(i), 0))],
        out_specs=[pl.BlockSpec((rows, 3 * D), lambda i: (rev(i), 0)),
                   pl.BlockSpec((rows, HEAD), lambda i: (rev(i), 0)), _full((1, HEAD)), _full((1, HEAD)),
                   _full((1, HEAD))],
        out_shape=[jax.ShapeDtypeStruct((t, 3 * D), ACT), jax.ShapeDtypeStruct((t, HEAD), F32), small, small, small],
        scratch_shapes=[pltpu.VMEM((DN_HEADS, HEAD, HEAD), F32)], compiler_params=_cp(1),
    )(qkv, qkv, qkv, ab, alog, dtb, onw, keep, inv, dy)


def _loss_head(y, target):
    t = y.shape[0]
    tm = min(512, t)

    def body(y_ref, t_ref, l_ref, dy_ref):
        @pl.when(pl.program_id(0) == 0)
        def _():
            l_ref[...] = jnp.zeros_like(l_ref)

        diff = y_ref[...] - t_ref[...]
        dy_ref[...] = diff * (1.0 / D)
        l_ref[...] += 0.5 * jnp.sum(jnp.sum(diff * diff, axis=-1, keepdims=True) * (1.0 / D), axis=0, keepdims=True)

    row = pl.BlockSpec((tm, D), lambda i: (i, 0))
    return _pcall(body, name="loss_head", grid=(t // tm,), in_specs=[row, row], out_specs=[_full((8, 128)), row],
                  out_shape=[jax.ShapeDtypeStruct((8, 128), F32), jax.ShapeDtypeStruct((t, D), F32)],
                  compiler_params=_cp(1))(y, target)


_SMALL_SHARDED = (("a_ln_w", (2, 128), 1), ("a_ln_b", (2, 128), 1), ("b_conv_w", (1, 3, 128), 2),
                  ("c_conv_w", (1, 4, 384), 2))
_REPLICATED = (("mem_norm_w", (1024,)), ("norm_pre", (4, 1024)), ("norm_post", (4, 1024)),
               ("a_b_s", (2, 8, 128)), ("c_a_log", (1, 8)), ("c_dt_bias", (1, 8)), ("c_o_norm_w", (1, 128)))


def _layer_shards(given, prefix):
    w_out = given[prefix + "w_out"]
    w_ins = [given[prefix + "a_w_in"][0], given[prefix + "b_w_in"], given[prefix + "c_w_in"], given[prefix + "a_w_in"][1]]
    return [[w_out[i], w_ins[i]] for i in range(4)]


def _as_matmul_operand(w):
    return w.reshape(w.shape[-2:]).astype(MXU)


def _as_rows(shape):
    return (math.prod(shape[:-1]) if len(shape) > 1 else 1, shape[-1])


def _row_table(shapes):
    table, at = [], 0
    for shape in shapes:
        rows, cols = _as_rows(shape)
        table.append((at, rows, cols))
        at += rows
    return table


def _pack_rows(arrays, shapes, rows, lead=()):
    n_lead = len(lead)
    parts = [jnp.pad(a.reshape(lead + _as_rows(s)), [(0, 0)] * (n_lead + 1) + [(0, D - s[-1])])
             for a, s in zip(arrays, shapes)]
    block = jnp.concatenate(parts, axis=n_lead)
    return jnp.pad(block, [(0, 0)] * n_lead + [(0, rows - block.shape[n_lead]), (0, 0)])


def _unpack_rows(packed, shapes, lead=()):
    n_lead = len(lead)
    return [packed[(slice(None),) * n_lead + (slice(r0, r0 + nr), slice(0, nc))].reshape(lead + s)
            for (r0, nr, nc), s in zip(_row_table(shapes), shapes)]


def _join_shards(blocks, axis):
    moved = jnp.moveaxis(blocks, 0, axis)
    shape = moved.shape
    return moved.reshape(shape[:axis] + (shape[axis] * shape[axis + 1],) + shape[axis + 2:])


def _split_shards(full, axis):
    shape = full.shape
    split = full.reshape(shape[:axis] + (N_DEV, shape[axis] // N_DEV) + shape[axis + 1:])
    return jnp.moveaxis(split, axis, 0)


_A_COLS = ((2560, 1536), (2048, 512), (0, 2048))
_BC_COLS = ((3584, 1536), (3072, 512), (0, 3072))
_C_COLS = ((3600, 1536), (3088, 512), (0, 3072))


def _reorder_cols(w, cols):
    return jnp.concatenate([w[:, s:s + n] for s, n in cols], axis=1)


def _restore_cols(pieces_in_my_order, cols, extra=()):
    placed = sorted(list(zip([s for s, _ in cols], pieces_in_my_order)) + list(extra), key=lambda sp: sp[0])
    return jnp.concatenate([piece for _, piece in placed], axis=1)


def kernel(x, mem, mem_norm_w, w_mem_kv, norm_pre, norm_post, w_out, a_w_in, a_ln_w, a_ln_b, a_w_s, a_b_s, b_w_in, b_conv_w, c_w_in, c_conv_w, c_a_log, c_dt_bias, c_o_norm_w, loss_target, m_mem_norm_w, m_w_mem_kv, m_norm_pre, m_norm_post, m_w_out, m_a_w_in, m_a_ln_w, m_a_ln_b, m_a_w_s, m_a_b_s, m_b_w_in, m_b_conv_w, m_c_w_in, m_c_conv_w, m_c_a_log, m_c_dt_bias, m_c_o_norm_w, v_mem_norm_w, v_w_mem_kv, v_norm_pre, v_norm_post, v_w_out, v_a_w_in, v_a_ln_w, v_a_ln_b, v_a_w_s, v_a_b_s, v_b_w_in, v_b_conv_w, v_c_w_in, v_c_conv_w, v_c_a_log, v_c_dt_bias, v_c_o_norm_w):
    given = dict(locals())
    x0 = x[0]
    mem0 = mem[0]
    target = loss_target[0]

    w_sh, m_sh, v_sh = (_layer_shards(given, pre) for pre in ("", "m_", "v_"))
    small_names = [n for n, _, _ in _SMALL_SHARDED]
    small_shapes = [s for _, s, _ in _SMALL_SHARDED]
    repl_names = [n for n, _ in _REPLICATED]
    repl_shapes = [s for _, s in _REPLICATED]
    small = _pack_rows([given[n] for n in small_names], small_shapes, R_SMALL)
    g_in0, = _all_gather([_as_matmul_operand(w_sh[0][1])], "gather_weights")
    w_in = [_reorder_cols(_join_shards(g_in0, 1), _A_COLS)]
    wouts, w_cab = [], None
    ws = [a_w_s[j] for j in range(2)]
    bst = [jnp.pad(a_b_s[j].T, ((0, 0), (0, HEAD - GM_GROUPS))) for j in range(2)]
    alog = jnp.pad(c_a_log, ((0, 0), (0, HEAD - DN_HEADS)))
    dtb = jnp.pad(c_dt_bias, ((0, 0), (0, HEAD - DN_HEADS)))
    onw = c_o_norm_w
    mw = mem_norm_w[None, :]

    xs, saved = [x0], []
    for i in range(4):
        kind = i % 3
        npre, npost = norm_pre[i][None, :], norm_post[i][None, :]
        ahead = [_as_matmul_operand(w_sh[i + 1][0])] if i < 3 else []
        if i == 0:
            ahead = [w_mem_kv.astype(MXU), _as_matmul_operand(w_sh[0][0]), small] + ahead
        res = _proj_fwd(xs[i], npre, w_in[i], w_cab if kind == 2 else None, f"proj_fwd_{i}", gather=ahead)
        if i == 0:
            g_wkv, g_wout0, g_small = res[2:5]
            small_full = _unpack_rows(g_small, small_shapes, lead=(N_DEV,))
            full = {n: _join_shards(blocks, ax) for (n, _, ax), blocks in zip(_SMALL_SHARDED, small_full)}
            wkv = g_wkv.reshape(D, 2 * D_XA)
            wouts.append(g_wout0.reshape(D_CAT, D))
            lnw = [full["a_ln_w"][j][None, :] for j in range(2)]
            lnb = [full["a_ln_b"][j][None, :] for j in range(2)]
            cw_b = jnp.pad(full["b_conv_w"][0], ((0, HALO - 3), (0, 0)))
            cw_c = jnp.pad(full["c_conv_w"][0], ((0, HALO - 4), (0, 0)))
            kv = _memkv_fwd(mem0, mw, wkv).astype(MXU)
        if ahead:
            wouts.append(res[-1].reshape(D_CAT, D))
        if kind == 2:
            p, h, ab = res[:3]
            qkv = _qkv_fwd(p, cw_c, f"qkv_fwd_{i}")
            ymix, keep, inv = _delta_fwd(qkv, ab, alog, dtb, onw, f"delta_fwd_{i}")
            extra = (qkv, ab, keep, inv)
        else:
            p, h = res[:2]
            if kind == 0:
                ymix = _gmlp_fwd(p, lnw[i // 3], lnb[i // 3], ws[i // 3], bst[i // 3], f"gmlp_fwd_{i}")
            else:
                ymix = _sconv_fwd(p, cw_b, f"sconv_fwd_{i}")
            extra = ()
        res = _tail_fwd(p, ymix, xs[i], kv, wouts[i], npost, f"tail_fwd_{i}",
                        gather=[_as_matmul_operand(w_sh[i + 1][1])] if i < 3 else [])
        xn, o = res[:2]
        if i + 1 == 2:
            c_full = jnp.concatenate([res[2][d] for d in range(N_DEV)], axis=1)
            w_in.append(_reorder_cols(c_full, _C_COLS))
            w_cab = jnp.pad(c_full[:, 3072:3088], ((0, 0), (0, HEAD - 16)))
        elif i < 3:
            w_in.append(_reorder_cols(_join_shards(res[2], 1), _BC_COLS if i + 1 == 1 else _A_COLS))
        xs.append(xn)
        saved.append((p, h, ymix, o, extra))

    loss_tile, dx = _loss_head(xs[4], target)
    loss = lax.psum(loss_tile[0, 0], ("x", "y", "c"))

    g = {}
    d_npre, d_npost = [None] * 4, [None] * 4
    d_ws, d_bs, d_lnw, d_lnb = [None] * 2, [None] * 2, [None] * 2, [None] * 2
    dkv = None
    pend_win, pend_wout = None, None
    landed_win, landed_wout = [None] * 4, [None] * 4
    for i in reversed(range(4)):
        kind = i % 3
        p, h, ymix, o, extra = saved[i]
        npre, npost = norm_pre[i][None, :], norm_post[i][None, :]
        res = _tail_bwd(dx, o, p, ymix, kv, wouts[i], npost, f"tail_bwd_{i}", exchange=[pend_win] if i < 3 else [])
        dzq, dymix, d_wout, d_npost[i], dkv_i = res[:5]
        if i < 3:
            landed_win[i + 1] = res[5]
        d_wout = d_wout.reshape(N_DEV, D_CAT // N_DEV, D).astype(XCH)
        dkv = dkv_i if dkv is None else dkv + dkv_i
        w_zq, w_mix = w_in[i][:, :ZQ], w_in[i][:, ZQ:]
        dw_zq = _matmul_tn(h, dzq, f"dw_zq_{i}")
        if kind == 0:
            j = i // 3
            early_x, early_b = [], []
            if i == 0:
                d_wkv, d_mw = _memkv_bwd(mem0, mw, wkv, dkv)
                early_x = [d_wkv.reshape(N_DEV, 128, D).astype(XCH), d_wout, pend_wout]
                early_b = [d_ws[1].reshape(GM_CHUNK, D).astype(XCH)]
            (dmix, d_lnw[j], d_lnb[j], d_ws[j], dbst), early_landed = _gmlp_bwd(
                p, dymix, lnw[j], lnb[j], ws[j], bst[j], f"gmlp_bwd_{i}", exchange=early_x, bcast=early_b)
            d_bs[j] = dbst[:, :GM_GROUPS].T
            if i == 0:
                g["mem_norm_w"] = d_mw[0]
                g["norm_pre"] = jnp.concatenate([jnp.zeros((1, D), F32)] + d_npre[1:], axis=0)
                g["norm_post"] = jnp.concatenate(d_npost, axis=0)
                g["a_ln_w"] = jnp.concatenate(d_lnw, axis=0)
                g["a_ln_b"] = jnp.concatenate(d_lnb, axis=0)
                g["a_b_s"] = jnp.stack(d_bs)
                e_small = _pack_rows([_split_shards(g[n], ax) for n, _, ax in _SMALL_SHARDED], small_shapes, R_SMALL,
                                     lead=(N_DEV,))
                r_pack = _pack_rows([g[n] for n in repl_names], repl_shapes, R_REPL)
                dw_mix, (l_small, ws0_all, r_all) = _matmul_tn(
                    h, dmix, f"dw_mix_{i}", exchange=[e_small], bcast=[d_ws[0].reshape(GM_CHUNK, D).astype(XCH), r_pack])
            else:
                dw_mix = _matmul_tn(h, dmix, f"dw_mix_{i}")
            d_win = _restore_cols([dw_zq[:, :D_CAT], dw_zq[:, D_CAT:], dw_mix], _A_COLS)
            dps, wparts = [dzq, dmix], [w_zq, w_mix]
        elif kind == 1:
            dmix, dcw = _sconv_bwd(p, dymix, cw_b, f"sconv_bwd_{i}")
            g["b_conv_w"] = dcw[None, :3]
            dw_mix = _matmul_tn(h, dmix, f"dw_mix_{i}")
            d_win = _restore_cols([dw_zq[:, :D_CAT], dw_zq[:, D_CAT:], dw_mix], _BC_COLS)
            dps, wparts = [dzq, dmix], [w_zq, w_mix]
        else:
            qkv, ab, keep, inv = extra
            dqkv, dab, dalog, ddtb, donw = _delta_bwd(qkv, ab, alog, dtb, onw, keep, inv, dymix, f"delta_bwd_{i}")
            dmix, dcw = _qkv_bwd(p, dqkv, cw_c, f"qkv_bwd_{i}")
            g["c_conv_w"] = dcw[None, :4]
            g["c_a_log"], g["c_dt_bias"], g["c_o_norm_w"] = dalog[:, :DN_HEADS], ddtb[:, :DN_HEADS], donw
            dw_mix = _matmul_tn(h, dmix, f"dw_mix_{i}")
            dw_ab = _matmul_tn(h, dab, f"dw_ab_{i}")
            d_win = _restore_cols([dw_zq[:, :D_CAT], dw_zq[:, D_CAT:], dw_mix], _C_COLS, extra=[(3072, dw_ab[:, :16])])
            dps, wparts = [dzq, dmix, dab], [w_zq, w_mix, w_cab]
        width = d_win.shape[1] // N_DEV
        pend_win = jnp.stack([d_win[:, d * width:(d + 1) * width] for d in range(N_DEV)]).astype(XCH)
        if i > 0:
            res = _proj_bwd_x(dps, wparts, xs[i], dx, npre, f"proj_bwd_x_{i}", exchange=[pend_wout] if i < 3 else [])
            dx, d_npre[i] = res[:2]
            if i < 3:
                landed_wout[i + 1] = res[2]
            pend_wout = d_wout

    dx, d_npre0, l_in = _proj_bwd_x(dps, wparts, xs[0], dx, norm_pre[0][None, :], "proj_bwd_x_0", exchange=[pend_win])
    l_wkv, landed_wout[0], landed_wout[1], ws3_all = early_landed
    landed_win[0] = l_in
    npre0_all, = _all_gather([jnp.pad(d_npre0, ((0, HALO - 1), (0, 0)))], "gather_norm_pre0")
    r_all = r_all.at[:, 1, :].set(npre0_all[:, 0, :])

    res = [[_reduce_adamw(parts, w_sh[i][a], m_sh[i][a], v_sh[i][a], f"adamw_{i}_{a}")
            for a, parts in enumerate((landed_wout[i], landed_win[i]))] for i in range(4)]
    res_wkv = _reduce_adamw(l_wkv, w_mem_kv, m_w_mem_kv, v_w_mem_kv, "adamw_w_mem_kv")
    res_small = _reduce_adamw_vectors(
        l_small, small, *(_pack_rows([given[pre + n] for n in small_names], small_shapes, R_SMALL) for pre in ("m_", "v_")),
        small_shapes, "adamw_small")
    res_ws = [_reduce_adamw(parts, *(given[pre + "a_w_s"][j].reshape(GM_CHUNK, D) for pre in ("", "m_", "v_")),
                            f"adamw_a_w_s_{j}") for j, parts in enumerate((ws0_all, ws3_all))]
    res_repl = _reduce_adamw_vectors(
        r_all, *(_pack_rows([given[pre + n] for n in repl_names], repl_shapes, R_REPL) for pre in ("", "m_", "v_")),
        repl_shapes, "adamw_replicated")

    order = ["mem_norm_w", "w_mem_kv", "norm_pre", "norm_post", "w_out", "a_w_in", "a_ln_w", "a_ln_b", "a_w_s", "a_b_s",
             "b_w_in", "b_conv_w", "c_w_in", "c_conv_w", "c_a_log", "c_dt_bias", "c_o_norm_w"]
    outs = [loss, dx[None]]
    for kind in range(4):
        got = dict(zip(repl_names, res_repl[kind]))
        got["a_w_s"] = jnp.stack([r[kind] for r in res_ws]).reshape(a_w_s.shape)
        got.update(zip(small_names, res_small[kind]))
        got["w_mem_kv"] = res_wkv[kind]
        got["w_out"] = jnp.stack([res[i][0][kind] for i in range(4)])
        got["a_w_in"] = jnp.stack([res[0][1][kind], res[3][1][kind]])
        got["b_w_in"] = res[1][1][kind]
        got["c_w_in"] = res[2][1][kind]
        outs += [got[n] for n in order]
    return tuple(outs)
```

```python
import functools
import math

import jax
import jax.numpy as jnp
from jax import lax
from jax.experimental import pallas as pl
from jax.experimental.pallas import tpu as pltpu

F32 = jnp.float32
MXU = jnp.bfloat16
ACT = jnp.bfloat16

D = 1024
D_XA = 512
D_CAT = 1536
N_MEM = 256
XA_HEADS = 4
HEAD = 128
ZQ = D_CAT + D_XA
EPS = 1e-6
GM_CHUNK = 128
GM_CHUNKS_PER_STEP = 4
GM_GROUPS = 8
DN_HEADS = 8
DN_CHUNK = 64
DN_CHUNKS_PER_STEP = 4
N_DEV = 8
HALO = 8
VMEM_LIMIT = 56 * 1024 * 1024
XCH = jnp.bfloat16
R_REPL = 32
R_SMALL = 16

ADAM_LR = 0.001
ADAM_B1 = 0.9
ADAM_B2 = 0.999
ADAM_EPS = 1e-08
ADAM_WD = 0.01
ADAM_STEP = 10

NN = ((1,), (0,))
NT = ((1,), (1,))
TN = ((0,), (0,))
MESH = pl.DeviceIdType.MESH


def _pcall(body, **kw):
    return pl.pallas_call(body, **kw)


def _cp(n_axes):
    return pltpu.CompilerParams(dimension_semantics=("arbitrary",) * n_axes, vmem_limit_bytes=VMEM_LIMIT)


def _dot(a, b, dims, prec=None):
    return lax.dot_general(a, b, (dims, ((), ())), preferred_element_type=F32, precision=prec)


def _mdot(a, b, dims):
    return _dot(a.astype(MXU), b.astype(MXU), dims)


def _make_mms(raw):
    @jax.custom_vjp
    def nn(a, b):
        return raw(a, b, NN)

    @jax.custom_vjp
    def nt(a, b):
        return raw(a, b, NT)

    @jax.custom_vjp
    def tn(a, b):
        return raw(a, b, TN)

    nn.defvjp(lambda a, b: (nn(a, b), (a, b)), lambda r, g: (nt(g, r[1]), tn(r[0], g)))
    nt.defvjp(lambda a, b: (nt(a, b), (a, b)), lambda r, g: (nn(g, r[1]), tn(g, r[0])))
    tn.defvjp(lambda a, b: (tn(a, b), (a, b)), lambda r, g: (nt(r[1], g), nn(r[0], g)))
    return nn, nt, tn


_nn, _nt, _tn = _make_mms(_mdot)


def _split_dot(a, b, dims):
    ah = a.astype(jnp.bfloat16)
    bh = b.astype(jnp.bfloat16)
    al = (a - ah.astype(F32)).astype(jnp.bfloat16)
    bl = (b - bh.astype(F32)).astype(jnp.bfloat16)
    return _dot(ah, bh, dims) + (_dot(ah, bl, dims) + _dot(al, bh, dims))


_hnn, _hnt, _htn = _make_mms(_split_dot)


def _full(shape):
    return pl.BlockSpec(shape, lambda *_: (0,) * len(shape))


def _silu(z):
    return z * jax.nn.sigmoid(z)


def _dsilu(z):
    s = jax.nn.sigmoid(z)
    return s * (1.0 + z * (1.0 - s))


def _onehot_row(n, k):
    return (lax.broadcasted_iota(jnp.int32, (1, n), 1) == k).astype(F32)


_HBM = pl.BlockSpec(memory_space=pl.ANY)


def _sem_shapes(n_remote, n_local):
    return [pltpu.SemaphoreType.DMA((n_remote,)), pltpu.SemaphoreType.DMA((n_remote,)),
            pltpu.SemaphoreType.DMA((n_local,))]


def _gather_parts(x_refs, out_refs, send_sems, recv_sems, local_sems):
    n = len(x_refs)
    x, y, cc = lax.axis_index("x"), lax.axis_index("y"), lax.axis_index("c")
    me, sibling = (x, y, cc), (x, y, 1 - cc)
    chips = [(1 - x, y), (x, 1 - y), (1 - x, 1 - y)]

    def slot(a, px, py, pc):
        return out_refs[a].at[4 * px + 2 * py + pc]

    def copy(k, a, block, to, src=None):
        return pltpu.make_async_remote_copy(
            src_ref=slot(a, *block) if src is None else src, dst_ref=slot(a, *block),
            send_sem=send_sems.at[k * n + a], recv_sem=recv_sems.at[k * n + a], device_id=to, device_id_type=MESH)

    mine = [pltpu.make_async_copy(x_refs[a], slot(a, *me), local_sems.at[a]) for a in range(n)]
    first = [copy(0, a, me, sibling, src=x_refs[a]) for a in range(n)]
    first += [copy(1 + j, a, me, (*chip, cc), src=x_refs[a]) for j, chip in enumerate(chips) for a in range(n)]

    def begin():
        for cp in mine + first:
            cp.start()

    def end():
        passed = []
        for j, chip in enumerate(chips):
            for a in range(n):
                copy(1 + j, a, (*chip, cc), me).wait_recv()
                passed.append(copy(4 + j, a, (*chip, cc), sibling))
                passed[-1].start()
        for a in range(n):
            copy(0, a, sibling, me).wait_recv()
        for j, chip in enumerate(chips):
            for a in range(n):
                copy(4 + j, a, (*chip, 1 - cc), me).wait_recv()
        for cp in first + passed:
            cp.wait_send()
        for cp in mine:
            cp.wait()

    return begin, end


def _exchange_parts(g_refs, out_refs, b_refs, ball_refs, send_sems, recv_sems, local_sems):
    n, nb = len(g_refs), len(b_refs)
    per_peer = n + nb
    x, y, cc = lax.axis_index("x"), lax.axis_index("y"), lax.axis_index("c")
    my_idx = 4 * x + 2 * y + cc
    mine = [pltpu.make_async_copy(g_refs[a].at[my_idx], out_refs[a].at[my_idx], local_sems.at[a]) for a in range(n)]
    mine += [pltpu.make_async_copy(b_refs[a], ball_refs[a].at[my_idx], local_sems.at[n + a]) for a in range(nb)]
    copies = []
    for k in range(1, N_DEV):
        px = 1 - x if k & 4 else x
        py = 1 - y if k & 2 else y
        pc = 1 - cc if k & 1 else cc
        base = (k - 1) * per_peer
        for a in range(n):
            copies.append(pltpu.make_async_remote_copy(
                src_ref=g_refs[a].at[4 * px + 2 * py + pc], dst_ref=out_refs[a].at[my_idx],
                send_sem=send_sems.at[base + a], recv_sem=recv_sems.at[base + a],
                device_id=(px, py, pc), device_id_type=MESH))
        for a in range(nb):
            copies.append(pltpu.make_async_remote_copy(
                src_ref=b_refs[a], dst_ref=ball_refs[a].at[my_idx], send_sem=send_sems.at[base + n + a],
                recv_sem=recv_sems.at[base + n + a], device_id=(px, py, pc), device_id_type=MESH))

    def begin():
        for cp in mine + copies:
            cp.start()

    def end():
        for cp in copies:
            cp.wait_recv()
        for cp in copies:
            cp.wait_send()
        for cp in mine:
            cp.wait()

    return begin, end


def _pcall_hosting(body, args, exchange, bcast, *, name, grid, in_specs, out_specs, out_shape, compiler_params,
                   scratch_shapes=(), gather=()):
    n_in, n_out, n_x, n_b, n_g = len(in_specs), len(out_specs), len(exchange), len(bcast), len(gather)
    n_d = n_x + n_b
    n_c, n_scr = n_d + n_g, len(scratch_shapes)

    def hosted(*refs):
        ins, c_in = refs[:n_in], refs[n_in:n_in + n_c]
        outs, c_out = refs[n_in + n_c:n_in + n_c + n_out], refs[n_in + n_c + n_out:n_in + 2 * n_c + n_out]
        scratch, sems = refs[n_in + 2 * n_c + n_out:][:n_scr], refs[n_in + 2 * n_c + n_out + n_scr:]
        parts = []
        if n_d:
            parts.append(_exchange_parts(c_in[:n_x], c_out[:n_x], c_in[n_x:n_d], c_out[n_x:n_d], *sems[:3]))
        if n_g:
            parts.append(_gather_parts(c_in[n_d:], c_out[n_d:], *sems[-3:]))
        first = functools.reduce(jnp.logical_and, [pl.program_id(k) == 0 for k in range(len(grid))])
        last = functools.reduce(jnp.logical_and, [pl.program_id(k) == grid[k] - 1 for k in range(len(grid))])
        for begin, _ in parts:
            pl.when(first)(begin)
        body(*ins, *outs, *scratch)
        for _, end in parts:
            pl.when(last)(end)

    landed_shape = [jax.ShapeDtypeStruct(e.shape, e.dtype) for e in exchange]
    landed_shape += [jax.ShapeDtypeStruct((N_DEV,) + b.shape, b.dtype) for b in list(bcast) + list(gather)]
    sems = (_sem_shapes(7 * n_d, n_d) if n_d else []) + (_sem_shapes(7 * n_g, n_g) if n_g else [])
    res = _pcall(hosted, name=name, grid=grid, in_specs=list(in_specs) + [_HBM] * n_c,
                 out_specs=list(out_specs) + [_HBM] * n_c, out_shape=list(out_shape) + landed_shape,
                 scratch_shapes=list(scratch_shapes) + sems,
                 compiler_params=compiler_params)(*args, *exchange, *bcast, *gather)
    return res[:n_out], res[n_out:]


def _all_gather(blks, name):
    n = len(blks)

    def body(*refs):
        begin, end = _gather_parts(refs[:n], refs[n:2 * n], *refs[2 * n:])
        begin()
        end()

    return _pcall(body, name=name, out_shape=[jax.ShapeDtypeStruct((N_DEV,) + b.shape, b.dtype) for b in blks],
                  in_specs=[_HBM] * n, out_specs=[_HBM] * n, scratch_shapes=_sem_shapes(7 * n, n))(*blks)


def _sum_and_adamw(p_ref, w, m, v):
    g = p_ref[0].astype(F32)
    for s in range(1, N_DEV):
        g = g + p_ref[s].astype(F32)
    nm = ADAM_B1 * m + (1.0 - ADAM_B1) * g
    nv = ADAM_B2 * v + (1.0 - ADAM_B2) * (g * g)
    m_hat = nm / (1.0 - ADAM_B1 ** ADAM_STEP)
    v_hat = nv / (1.0 - ADAM_B2 ** ADAM_STEP)
    return g, -ADAM_LR * (m_hat / (jnp.sqrt(v_hat) + ADAM_EPS) + ADAM_WD * w), nm, nv


def _reduce_adamw_vectors(parts, w, m, v, shapes, name):
    table = _row_table(shapes)

    def body(p_ref, w_ref, m_ref, v_ref, *out_refs):
        results = _sum_and_adamw(p_ref, w_ref[...], m_ref[...], v_ref[...])
        for kind, val in enumerate(results):
            for t, (r0, nr, nc) in enumerate(table):
                out_refs[kind * len(table) + t][...] = val[r0:r0 + nr, 0:nc]

    outs = _pcall(body, name=name, out_shape=[jax.ShapeDtypeStruct((nr, nc), F32) for _, nr, nc in table] * 4)(parts, w, m, v)
    return [[outs[kind * len(table) + t].reshape(s) for t, s in enumerate(shapes)] for kind in range(4)]


def _reduce_adamw(parts, w, m, v, name):
    lead = w.shape[:-2]
    r, c = w.shape[-2:]
    tr = 128 if r % 128 == 0 else r
    zeros = (0,) * len(lead)
    at = zeros + (slice(None), slice(None))

    def body(p_ref, w_ref, m_ref, v_ref, g_ref, d_ref, nm_ref, nv_ref):
        g_ref[at], d_ref[at], nm_ref[at], nv_ref[at] = _sum_and_adamw(p_ref, w_ref[at], m_ref[at], v_ref[at])

    row = pl.BlockSpec((1,) * len(lead) + (tr, c), lambda i: zeros + (i, 0))
    out = jax.ShapeDtypeStruct(w.shape, F32)
    return _pcall(
        body, name=name, grid=(r // tr,),
        in_specs=[pl.BlockSpec((N_DEV, tr, c), lambda i: (0, i, 0)), row, row, row],
        out_specs=[row, row, row, row], out_shape=[out, out, out, out], compiler_params=_cp(1),
    )(parts, w, m, v)


def _proj_fwd(x, nw, w, wab, name, gather=()):
    t, npj = x.shape[0], w.shape[1]
    tm, tn = min(512, t), 1024
    has_ab = wab is not None
    n_in, n_out, n_g = 3 + has_ab, 2 + has_ab, len(gather)
    n_i = t // tm

    def body(*refs):
        ins, g_ins = refs[:n_in], refs[n_in:n_in + n_g]
        outs = refs[n_in + n_g:n_in + n_g + n_out]
        g_outs = refs[n_in + n_g + n_out:n_in + 2 * n_g + n_out]
        if has_ab:
            (x_ref, nw_ref, w_ref, wab_ref), (p_ref, h_ref, ab_ref) = ins, outs
        else:
            (x_ref, nw_ref, w_ref), (p_ref, h_ref) = ins, outs
        if n_g:
            begin, end = _gather_parts(g_ins, g_outs, *refs[n_in + 2 * n_g + n_out:])
            pl.when(pl.program_id(0) == 0)(begin)

        for rows in (slice(0, tm // 2), slice(tm // 2, tm)):
            xv = x_ref[rows, :]
            hv = (xv * lax.rsqrt(jnp.mean(xv * xv, axis=-1, keepdims=True) + EPS) * nw_ref[...]).astype(MXU)
            h_ref[rows, :] = hv.astype(ACT)
            if has_ab:
                ab_ref[rows, :] = _dot(hv, wab_ref[...], NN)
            for j in range(npj // tn):
                p_ref[rows, j * tn:(j + 1) * tn] = _dot(hv, w_ref[:, j * tn:(j + 1) * tn], NN).astype(ACT)
        if n_g:
            pl.when(pl.program_id(0) == n_i - 1)(end)

    in_specs = [pl.BlockSpec((tm, D), lambda i: (i, 0)), _full((1, D)), _full((D, npj))]
    out_specs = [pl.BlockSpec((tm, npj), lambda i: (i, 0)), pl.BlockSpec((tm, D), lambda i: (i, 0))]
    out_shape = [jax.ShapeDtypeStruct((t, npj), ACT), jax.ShapeDtypeStruct((t, D), ACT)]
    args = [x, nw, w]
    if has_ab:
        in_specs.append(_full((D, HEAD)))
        out_specs.append(pl.BlockSpec((tm, HEAD), lambda i: (i, 0)))
        out_shape.append(jax.ShapeDtypeStruct((t, HEAD), F32))
        args.append(wab)
    scratch = []
    if n_g:
        in_specs += [_HBM] * n_g
        out_specs += [_HBM] * n_g
        out_shape += [jax.ShapeDtypeStruct((N_DEV,) + b.shape, b.dtype) for b in gather]
        args += list(gather)
        scratch += _sem_shapes(7 * n_g, n_g)
    return _pcall(body, name=name, grid=(n_i,), in_specs=in_specs, out_specs=out_specs,
                  out_shape=out_shape, scratch_shapes=scratch, compiler_params=_cp(1))(*args)


def _proj_bwd_x(dps, ws, x, dxn, nw, name, exchange=(), bcast=()):
    t = x.shape[0]
    tm = min(512, t)
    n, n_x, n_b = len(dps), len(exchange), len(bcast)
    n_c, n_steps = n_x + n_b, t // tm

    def body(*refs):
        dp_refs, w_refs = refs[:n], refs[n:2 * n]
        x_ref, dxn_ref, nw_ref = refs[2 * n:2 * n + 3]
        c_in = refs[2 * n + 3:2 * n + 3 + n_c]
        dx_ref, dnw_ref = refs[2 * n + 3 + n_c:2 * n + 5 + n_c]
        c_out = refs[2 * n + 5 + n_c:2 * n + 5 + 2 * n_c]
        if n_c:
            begin, end = _exchange_parts(c_in[:n_x], c_out[:n_x], c_in[n_x:], c_out[n_x:], *refs[2 * n + 5 + 2 * n_c:])
            pl.when(pl.program_id(0) == 0)(begin)
        @pl.when(pl.program_id(0) == 0)
        def _():
            dnw_ref[...] = jnp.zeros_like(dnw_ref)

        halves = [slice(r0, r0 + tm // 2) for r0 in (0, tm // 2)]
        dhs = []
        for rows in halves:
            dh = _mdot(dp_refs[0][rows, :], w_refs[0][...], NT)
            for k in range(1, n):
                dh = dh + _mdot(dp_refs[k][rows, :], w_refs[k][...], NT)
            dhs.append(dh)
        for rows, dh in zip(halves, dhs):
            xv = x_ref[rows, :]
            r = lax.rsqrt(jnp.mean(xv * xv, axis=-1, keepdims=True) + EPS)
            dnw_ref[...] += jnp.sum(dh * xv * r, axis=0, keepdims=True)
            dhw = dh * nw_ref[...]
            dx_ref[rows, :] = dxn_ref[rows, :] + r * (dhw - xv * (r * r) * jnp.mean(dhw * xv, axis=-1, keepdims=True))
        if n_c:
            pl.when(pl.program_id(0) == n_steps - 1)(end)

    row = pl.BlockSpec((tm, D), lambda i: (i, 0))
    in_specs = [pl.BlockSpec((tm, dp.shape[1]), lambda i: (i, 0)) for dp in dps]
    in_specs += [_full(w.shape) for w in ws]
    in_specs += [row, row, _full((1, D))] + [_HBM] * n_c
    out_shape = [jax.ShapeDtypeStruct((t, D), F32), jax.ShapeDtypeStruct((1, D), F32)]
    out_shape += [jax.ShapeDtypeStruct(e.shape, e.dtype) for e in exchange]
    out_shape += [jax.ShapeDtypeStruct((N_DEV,) + b.shape, b.dtype) for b in bcast]
    return _pcall(body, name=name, grid=(n_steps,), in_specs=in_specs, out_specs=[row, _full((1, D))] + [_HBM] * n_c,
                  out_shape=out_shape, scratch_shapes=_sem_shapes(7 * n_c, n_c) if n_c else [],
                  compiler_params=_cp(1))(*dps, *ws, x, dxn, nw, *exchange, *bcast)


def _matmul_tn(a, b, name, out_dtype=XCH, exchange=(), bcast=()):
    t, m = a.shape
    n = b.shape[1]
    tm, tn = min(1024, t), min(1024, n)
    n_t = t // tm

    def body(a_ref, b_ref, o_ref, acc):
        @pl.when(pl.program_id(1) == 0)
        def _():
            acc[...] = jnp.zeros_like(acc)

        acc[...] += _mdot(a_ref[...], b_ref[...], TN)

        @pl.when(pl.program_id(1) == n_t - 1)
        def _():
            o_ref[...] = acc[...].astype(out_dtype)

    call = dict(name=name, grid=(n // tn, n_t),
                in_specs=[pl.BlockSpec((tm, m), lambda j, i: (i, 0)), pl.BlockSpec((tm, tn), lambda j, i: (i, j))],
                out_specs=[pl.BlockSpec((m, tn), lambda j, i: (0, j))],
                out_shape=[jax.ShapeDtypeStruct((m, n), out_dtype)], scratch_shapes=[pltpu.VMEM((m, tn), F32)],
                compiler_params=_cp(2))
    if not exchange and not bcast:
        return _pcall(body, **call)(a, b)[0]
    (out,), landed = _pcall_hosting(body, (a, b), exchange, bcast, **call)
    return out, landed


def _memkv_fwd(mem, mw, wkv):
    def body(mem_ref, mw_ref, w_ref, kv_ref):
        mv = mem_ref[...]
        mn = mv * lax.rsqrt(jnp.mean(mv * mv, axis=-1, keepdims=True) + EPS) * mw_ref[...]
        kv_ref[...] = _mdot(mn, w_ref[...], NN)

    return _pcall(body, name="memkv_fwd", out_shape=jax.ShapeDtypeStruct((N_MEM, 2 * D_XA), F32),
                  compiler_params=pltpu.CompilerParams(vmem_limit_bytes=VMEM_LIMIT))(mem, mw, wkv)


def _memkv_bwd(mem, mw, wkv, dkv):
    def body(mem_ref, mw_ref, w_ref, dkv_ref, dw_ref, dmw_ref):
        mv = mem_ref[...]
        r = lax.rsqrt(jnp.mean(mv * mv, axis=-1, keepdims=True) + EPS)
        mn = mv * r * mw_ref[...]
        dkvv = dkv_ref[...]
        dw_ref[...] = _mdot(mn, dkvv, TN)
        dmn = _mdot(dkvv, w_ref[...], NT)
        dmw_ref[...] = jnp.sum(dmn * mv * r, axis=0, keepdims=True)

    return _pcall(body, name="memkv_bwd",
                  out_shape=[jax.ShapeDtypeStruct((D, 2 * D_XA), F32), jax.ShapeDtypeStruct((1, D), F32)],
                  compiler_params=pltpu.CompilerParams(vmem_limit_bytes=VMEM_LIMIT))(mem, mw, wkv, dkv)


def _attend(q, kv):
    heads = range(XA_HEADS)
    qs = [q[:, h * HEAD:(h + 1) * HEAD] for h in heads]
    ks = [kv[:, h * HEAD:(h + 1) * HEAD] for h in heads]
    vs = [kv[:, D_XA + h * HEAD:D_XA + (h + 1) * HEAD] for h in heads]
    ss = [_mdot(qs[h], ks[h], NT) * (HEAD ** -0.5) for h in heads]
    es = [jnp.exp(s - jnp.max(s, axis=-1, keepdims=True)) for s in ss]
    ps = [e / jnp.sum(e, axis=-1, keepdims=True) for e in es]
    return ps, [_mdot(ps[h], vs[h], NN) for h in heads]


def _tail_fwd(p, ymix, x, kv, wout, npost, name, gather=()):
    t = x.shape[0]
    tm = min(512, t)

    def body(z_ref, q_ref, y_ref, x_ref, kv_ref, w_ref, np_ref, xn_ref, o_ref):
        _, outs = _attend(q_ref[...], kv_ref[...])
        cat = jnp.concatenate([y_ref[...]] + [a.astype(ACT) for a in outs], axis=1)
        g = cat * _silu(z_ref[...])
        o = _mdot(g, w_ref[...], NN)
        o_ref[...] = o
        xn_ref[...] = x_ref[...] + o * lax.rsqrt(jnp.mean(o * o, axis=-1, keepdims=True) + EPS) * np_ref[...]

    row = pl.BlockSpec((tm, D), lambda i: (i, 0))
    call = dict(
        name=name, grid=(t // tm,),
        in_specs=[pl.BlockSpec((tm, D_CAT), lambda i: (i, 0)), pl.BlockSpec((tm, D_XA), lambda i: (i, D_CAT // D_XA)),
                  row, row, _full((N_MEM, 2 * D_XA)), _full((D_CAT, D)), _full((1, D))],
        out_specs=[row, row],
        out_shape=[jax.ShapeDtypeStruct((t, D), F32), jax.ShapeDtypeStruct((t, D), F32)], compiler_params=_cp(1))
    args = (p, p, ymix, x, kv, wout, npost)
    if not gather:
        return tuple(_pcall(body, **call)(*args))
    outs, landed = _pcall_hosting(body, args, (), (), gather=gather, **call)
    return tuple(outs) + tuple(landed)


def _tail_bwd(dxn, o, p, ymix, kv, wout, npost, name, exchange=()):
    t = dxn.shape[0]
    tm = min(512, t)
    n_x, n_steps = len(exchange), t // tm

    def body(*refs):
        dxn_ref, o_ref, z_ref, q_ref, y_ref, kv_ref, w_ref, np_ref = refs[:8]
        dzq_ref, dy_ref, dw_ref, dnp_ref, dkv_ref = refs[8 + n_x:13 + n_x]
        dw_acc = refs[13 + 2 * n_x]
        if n_x:
            begin, end = _exchange_parts(refs[8:8 + n_x], refs[13 + n_x:13 + 2 * n_x], (), (), *refs[14 + 2 * n_x:])
            pl.when(pl.program_id(0) == 0)(begin)

        @pl.when(pl.program_id(0) == 0)
        def _():
            dw_acc[...] = jnp.zeros_like(dw_acc)
            dnp_ref[...] = jnp.zeros_like(dnp_ref)
            dkv_ref[...] = jnp.zeros_like(dkv_ref)

        q = q_ref[...]
        kvv = kv_ref[...]
        z = z_ref[...]
        ps, outs = _attend(q, kvv)
        cat = jnp.concatenate([y_ref[...]] + [a.astype(ACT) for a in outs], axis=1)
        sz = _silu(z)
        g = cat * sz
        ov = o_ref[...]
        dr = dxn_ref[...]
        rr = lax.rsqrt(jnp.mean(ov * ov, axis=-1, keepdims=True) + EPS)
        dnp_ref[...] += jnp.sum(dr * ov * rr, axis=0, keepdims=True)
        dow = dr * np_ref[...]
        do = rr * (dow - ov * (rr * rr) * jnp.mean(dow * ov, axis=-1, keepdims=True))
        dg = _mdot(do, w_ref[...], NT).astype(ACT)
        dw_acc[...] += _mdot(g, do, TN)
        dcat = dg * sz
        dzq_ref[:, 0:D_CAT] = dg * cat * _dsilu(z)
        dy_ref[...] = dcat[:, 0:D]
        heads = range(XA_HEADS)
        dohs = [dcat[:, D + h * HEAD:D + (h + 1) * HEAD] for h in heads]
        dps = [_mdot(dohs[h], kvv[:, D_XA + h * HEAD:D_XA + (h + 1) * HEAD], NT) for h in heads]
        dss = [ps[h] * (dps[h] - jnp.sum(dps[h] * ps[h], axis=-1, keepdims=True)) for h in heads]
        dqs = [_mdot(dss[h], kvv[:, h * HEAD:(h + 1) * HEAD], NN) * (HEAD ** -0.5) for h in heads]
        dks = [_mdot(dss[h], q[:, h * HEAD:(h + 1) * HEAD], TN) * (HEAD ** -0.5) for h in heads]
        dvs = [_mdot(ps[h], dohs[h], TN) for h in heads]
        for h in heads:
            lo = h * HEAD
            dzq_ref[:, D_CAT + lo:D_CAT + lo + HEAD] = dqs[h].astype(ACT)
            dkv_ref[:, lo:lo + HEAD] += dks[h]
            dkv_ref[:, D_XA + lo:D_XA + lo + HEAD] += dvs[h]

        @pl.when(pl.program_id(0) == n_steps - 1)
        def _():
            dw_ref[...] = dw_acc[...].astype(XCH)

        if n_x:
            pl.when(pl.program_id(0) == n_steps - 1)(end)

    row = pl.BlockSpec((tm, D), lambda i: (i, 0))
    return _pcall(
        body, name=name, grid=(n_steps,),
        in_specs=[row, row, pl.BlockSpec((tm, D_CAT), lambda i: (i, 0)),
                  pl.BlockSpec((tm, D_XA), lambda i: (i, D_CAT // D_XA)), row,
                  _full((N_MEM, 2 * D_XA)), _full((D_CAT, D)), _full((1, D))] + [_HBM] * n_x,
        out_specs=[pl.BlockSpec((tm, ZQ), lambda i: (i, 0)), row, _full((D_CAT, D)), _full((1, D)),
                   _full((N_MEM, 2 * D_XA))] + [_HBM] * n_x,
        out_shape=[jax.ShapeDtypeStruct((t, ZQ), ACT), jax.ShapeDtypeStruct((t, D), ACT),
                   jax.ShapeDtypeStruct((D_CAT, D), XCH), jax.ShapeDtypeStruct((1, D), F32),
                   jax.ShapeDtypeStruct((N_MEM, 2 * D_XA), F32)]
        + [jax.ShapeDtypeStruct(e.shape, e.dtype) for e in exchange],
        scratch_shapes=[pltpu.VMEM((D_CAT, D), F32)] + (_sem_shapes(7 * n_x, n_x) if n_x else []),
        compiler_params=_cp(1),
    )(dxn, o, p, p, ymix, kv, wout, npost, *exchange)


def _gmlp_chunk(us, vs, lnws, lnbs, wss, bss):
    gv = [jax.nn.gelu(v) for v in vs]
    mean = sum(jnp.sum(v, axis=-1, keepdims=True) for v in gv) / D
    cen = [v - mean for v in gv]
    var = sum(jnp.sum(c * c, axis=-1, keepdims=True) for c in cen) / D
    rstd = lax.rsqrt(var + EPS)
    row = lax.broadcasted_iota(jnp.int32, (GM_CHUNK, GM_CHUNK), 0)
    col = lax.broadcasted_iota(jnp.int32, (GM_CHUNK, GM_CHUNK), 1)
    ys = []
    for g in range(GM_GROUPS):
        vn = cen[g] * rstd * lnws[g] + lnbs[g]
        sp = _nn(jnp.where(row >= col, wss[g], 0.0), vn) + bss[g]
        ys.append(jax.nn.gelu(us[g]) * sp)
    return ys


def _split_cols(v, n, width=HEAD):
    return [v[:, k * width:(k + 1) * width] for k in range(n)]


def _gmlp_operands(u_ref, v_ref, lnw_ref, lnb_ref, ws_ref, bst_ref, r0):
    us = _split_cols(u_ref[r0:r0 + GM_CHUNK, :].astype(F32), GM_GROUPS)
    vs = _split_cols(v_ref[r0:r0 + GM_CHUNK, :].astype(F32), GM_GROUPS)
    lnws = _split_cols(lnw_ref[...], GM_GROUPS)
    lnbs = _split_cols(lnb_ref[...], GM_GROUPS)
    wss = [ws_ref[g] for g in range(GM_GROUPS)]
    bst = bst_ref[...]
    bss = [jnp.sum(bst * _onehot_row(HEAD, g), axis=1, keepdims=True) for g in range(GM_GROUPS)]
    return us, vs, lnws, lnbs, wss, bss


def _gmlp_rows(t):
    return min(GM_CHUNKS_PER_STEP, t // GM_CHUNK) * GM_CHUNK


def _gmlp_specs(rows):
    return [pl.BlockSpec((rows, D), lambda i: (i, ZQ // D)), pl.BlockSpec((rows, D), lambda i: (i, ZQ // D + 1)),
            _full((1, D)), _full((1, D)), _full((GM_GROUPS, GM_CHUNK, GM_CHUNK)), _full((GM_CHUNK, HEAD))]


def _gmlp_fwd(p, lnw, lnb, ws, bst, name):
    t = p.shape[0]
    rows = _gmlp_rows(t)

    def body(u_ref, v_ref, lnw_ref, lnb_ref, ws_ref, bst_ref, y_ref):
        for r0 in range(0, rows, GM_CHUNK):
            ys = _gmlp_chunk(*_gmlp_operands(u_ref, v_ref, lnw_ref, lnb_ref, ws_ref, bst_ref, r0))
            for g in range(GM_GROUPS):
                y_ref[r0:r0 + GM_CHUNK, g * HEAD:(g + 1) * HEAD] = ys[g].astype(ACT)

    return _pcall(body, name=name, grid=(t // rows,), in_specs=_gmlp_specs(rows),
                  out_specs=pl.BlockSpec((rows, D), lambda i: (i, 0)),
                  out_shape=jax.ShapeDtypeStruct((t, D), ACT), compiler_params=_cp(1))(p, p, lnw, lnb, ws, bst)


def _gmlp_bwd(p, dy, lnw, lnb, ws, bst, name, exchange=(), bcast=()):
    t = p.shape[0]
    rows = _gmlp_rows(t)

    def body(u_ref, v_ref, lnw_ref, lnb_ref, ws_ref, bst_ref, dy_ref, duv_ref, dlnw_ref, dlnb_ref, dws_ref, dbst_ref):
        @pl.when(pl.program_id(0) == 0)
        def _():
            dlnw_ref[...] = jnp.zeros_like(dlnw_ref)
            dlnb_ref[...] = jnp.zeros_like(dlnb_ref)
            dws_ref[...] = jnp.zeros_like(dws_ref)
            dbst_ref[...] = jnp.zeros_like(dbst_ref)

        for r0 in range(0, rows, GM_CHUNK):
            ops = _gmlp_operands(u_ref, v_ref, lnw_ref, lnb_ref, ws_ref, bst_ref, r0)
            _, vjp = jax.vjp(_gmlp_chunk, *ops)
            dus, dvs, dlnws, dlnbs, dwss, dbss = vjp(_split_cols(dy_ref[r0:r0 + GM_CHUNK, :].astype(F32), GM_GROUPS))
            dbst = jnp.zeros((GM_CHUNK, HEAD), F32)
            for g in range(GM_GROUPS):
                lo = g * HEAD
                duv_ref[r0:r0 + GM_CHUNK, lo:lo + HEAD] = dus[g].astype(ACT)
                duv_ref[r0:r0 + GM_CHUNK, D + lo:D + lo + HEAD] = dvs[g].astype(ACT)
                dlnw_ref[:, lo:lo + HEAD] += dlnws[g]
                dlnb_ref[:, lo:lo + HEAD] += dlnbs[g]
                dws_ref[g] += dwss[g]
                dbst = dbst + dbss[g] * _onehot_row(HEAD, g)
            dbst_ref[...] += dbst

    call = dict(
        name=name, grid=(t // rows,),
        in_specs=_gmlp_specs(rows) + [pl.BlockSpec((rows, D), lambda i: (i, 0))],
        out_specs=[pl.BlockSpec((rows, 2 * D), lambda i: (i, 0)), _full((1, D)), _full((1, D)),
                   _full((GM_GROUPS, GM_CHUNK, GM_CHUNK)), _full((GM_CHUNK, HEAD))],
        out_shape=[jax.ShapeDtypeStruct((t, 2 * D), ACT), jax.ShapeDtypeStruct((1, D), F32),
                   jax.ShapeDtypeStruct((1, D), F32), jax.ShapeDtypeStruct((GM_GROUPS, GM_CHUNK, GM_CHUNK), F32),
                   jax.ShapeDtypeStruct((GM_CHUNK, HEAD), F32)],
        compiler_params=_cp(1))
    args = (p, p, lnw, lnb, ws, bst, dy)
    if not exchange and not bcast:
        return _pcall(body, **call)(*args), ()
    return _pcall_hosting(body, args, exchange, bcast, **call)


def _prev_halo(tm, col):
    return pl.BlockSpec((HALO, D), lambda i: (jnp.maximum(i * (tm // HALO) - 1, 0), col))


def _next_halo(tm, col, n_tiles):
    return pl.BlockSpec((HALO, D), lambda i: (jnp.minimum(i + 1, n_tiles - 1) * (tm // HALO), col))


def _taps_back(ext, w, width):
    acc = None
    for k in range(width):
        s = width - 1 - k
        term = w[k:k + 1, :] * (pltpu.roll(ext, s, 0) if s else ext)[HALO:, :]
        acc = term if acc is None else acc + term
    return acc


def _taps_fwd(ext, w, width, n):
    rows = ext.shape[0]
    acc = None
    for k in range(width):
        s = width - 1 - k
        term = w[k:k + 1, :] * (pltpu.roll(ext, rows - s, 0) if s else ext)[0:n, :]
        acc = term if acc is None else acc + term
    return acc


def _sconv_fwd(p, cw, name):
    t = p.shape[0]
    tm = min(256, t)

    def body(b_ref, c_ref, h_ref, cp_ref, hp_ref, w_ref, y_ref):
        first = pl.program_id(0) == 0
        prev = jnp.where(first, 0.0, cp_ref[...].astype(F32) * hp_ref[...].astype(F32))
        ext = jnp.concatenate([prev, c_ref[...].astype(F32) * h_ref[...].astype(F32)], axis=0)
        y_ref[...] = (b_ref[...].astype(F32) * _taps_back(ext, w_ref[...], 3)).astype(ACT)

    c0 = ZQ // D
    tile = [pl.BlockSpec((tm, D), lambda i, c=c: (i, c)) for c in (c0, c0 + 1, c0 + 2)]
    return _pcall(body, name=name, grid=(t // tm,),
                  in_specs=tile + [_prev_halo(tm, c0 + 1), _prev_halo(tm, c0 + 2), _full((HALO, D))],
                  out_specs=pl.BlockSpec((tm, D), lambda i: (i, 0)),
                  out_shape=jax.ShapeDtypeStruct((t, D), ACT), compiler_params=_cp(1))(p, p, p, p, p, cw)


def _sconv_bwd(p, dy, cw, name):
    t = p.shape[0]
    tm = min(256, t)
    n_tiles = t // tm

    def body(b_ref, c_ref, h_ref, cp_ref, hp_ref, bn_ref, dy_ref, dyn_ref, w_ref, d_ref, dw_ref):
        i = pl.program_id(0)

        @pl.when(i == 0)
        def _():
            dw_ref[...] = jnp.zeros_like(dw_ref)

        w = w_ref[...]
        bv, cv, hv = b_ref[...].astype(F32), c_ref[...].astype(F32), h_ref[...].astype(F32)
        dyv = dy_ref[...].astype(F32)
        prev = jnp.where(i == 0, 0.0, cp_ref[...].astype(F32) * hp_ref[...].astype(F32))
        ext = jnp.concatenate([prev, cv * hv], axis=0)
        conv = _taps_back(ext, w, 3)
        dconv = dyv * bv
        nxt = jnp.where(i == n_tiles - 1, 0.0, dyn_ref[...].astype(F32) * bn_ref[...].astype(F32))
        dc = _taps_fwd(jnp.concatenate([dconv, nxt], axis=0), w, 3, tm)
        d_ref[:, 0:D] = (dyv * conv).astype(ACT)
        d_ref[:, D:2 * D] = (dc * hv).astype(ACT)
        d_ref[:, 2 * D:3 * D] = (dc * cv).astype(ACT)
        for k in range(3):
            s = 2 - k
            shifted = (pltpu.roll(ext, s, 0) if s else ext)[HALO:, :]
            dw_ref[k:k + 1, :] += jnp.sum(dconv * shifted, axis=0, keepdims=True)

    c0 = ZQ // D
    tile = [pl.BlockSpec((tm, D), lambda i, c=c: (i, c)) for c in (c0, c0 + 1, c0 + 2)]
    return _pcall(
        body, name=name, grid=(n_tiles,),
        in_specs=tile + [_prev_halo(tm, c0 + 1), _prev_halo(tm, c0 + 2), _next_halo(tm, c0, n_tiles),
                         pl.BlockSpec((tm, D), lambda i: (i, 0)), _next_halo(tm, 0, n_tiles), _full((HALO, D))],
        out_specs=[pl.BlockSpec((tm, 3 * D), lambda i: (i, 0)), _full((HALO, D))],
        out_shape=[jax.ShapeDtypeStruct((t, 3 * D), ACT), jax.ShapeDtypeStruct((HALO, D), F32)],
        compiler_params=_cp(1),
    )(p, p, p, p, p, p, dy, dy, cw)


def _l2_heads(s, scale):
    outs, rs = [], []
    for hh in range(DN_HEADS):
        blk = s[:, hh * HEAD:(hh + 1) * HEAD]
        r = lax.rsqrt(jnp.sum(blk * blk, axis=-1, keepdims=True) + EPS)
        outs.append(blk * (r * scale))
        rs.append(r)
    return outs, rs


_QKV_SCALE = (HEAD ** -0.5, 1.0, None)


def _qkv_fwd(p, cw, name):
    t = p.shape[0]
    tm = min(256, t)

    def body(q_ref, k_ref, v_ref, qp_ref, kp_ref, vp_ref, w_ref, o_ref):
        first = pl.program_id(0) == 0
        for part, (ref, pref) in enumerate(((q_ref, qp_ref), (k_ref, kp_ref), (v_ref, vp_ref))):
            prev = jnp.where(first, 0.0, pref[...].astype(F32))
            ext = jnp.concatenate([prev, ref[...].astype(F32)], axis=0)
            s = _silu(_taps_back(ext, w_ref[:, part * D:(part + 1) * D], 4))
            if _QKV_SCALE[part] is None:
                o_ref[:, part * D:(part + 1) * D] = s.astype(ACT)
            else:
                outs, _ = _l2_heads(s, _QKV_SCALE[part])
                for hh in range(DN_HEADS):
                    o_ref[:, part * D + hh * HEAD:part * D + (hh + 1) * HEAD] = outs[hh].astype(ACT)

    c0 = ZQ // D
    tile = [pl.BlockSpec((tm, D), lambda i, c=c: (i, c)) for c in (c0, c0 + 1, c0 + 2)]
    halo = [_prev_halo(tm, c) for c in (c0, c0 + 1, c0 + 2)]
    return _pcall(body, name=name, grid=(t // tm,), in_specs=tile + halo + [_full((HALO, 3 * D))],
                  out_specs=pl.BlockSpec((tm, 3 * D), lambda i: (i, 0)),
                  out_shape=jax.ShapeDtypeStruct((t, 3 * D), ACT), compiler_params=_cp(1))(p, p, p, p, p, p, cw)


def _qkv_bwd(p, dqkv, cw, name):
    t = p.shape[0]
    tm = min(256, t)
    n_tiles = t // tm

    def body(*refs):
        tiles, prevs, nexts = refs[0:3], refs[3:6], refs[6:9]
        d_tiles, d_nexts = refs[9:12], refs[12:15]
        w_ref, o_ref, dw_ref = refs[15:]
        i = pl.program_id(0)

        @pl.when(i == 0)
        def _():
            dw_ref[...] = jnp.zeros_like(dw_ref)

        for part in range(3):
            w = w_ref[:, part * D:(part + 1) * D]
            prev = jnp.where(i == 0, 0.0, prevs[part][...].astype(F32))
            ext = jnp.concatenate([prev, tiles[part][...].astype(F32), nexts[part][...].astype(F32)], axis=0)
            xc = _taps_back(ext, w, 4)
            dout = jnp.concatenate([d_tiles[part][...].astype(F32), d_nexts[part][...].astype(F32)], axis=0)
            s = _silu(xc)
            if _QKV_SCALE[part] is None:
                ds = dout
            else:
                scale = _QKV_SCALE[part]
                pieces = []
                for hh in range(DN_HEADS):
                    blk = s[:, hh * HEAD:(hh + 1) * HEAD]
                    dblk = dout[:, hh * HEAD:(hh + 1) * HEAD]
                    r = lax.rsqrt(jnp.sum(blk * blk, axis=-1, keepdims=True) + EPS)
                    pieces.append(scale * r * (dblk - blk * (r * r) * jnp.sum(dblk * blk, axis=-1, keepdims=True)))
                ds = jnp.concatenate(pieces, axis=1)
            dxc = ds * _dsilu(xc)
            row = lax.broadcasted_iota(jnp.int32, (tm + HALO, 1), 0)
            dxc = jnp.where(jnp.logical_and(i == n_tiles - 1, row >= tm), 0.0, dxc)
            o_ref[:, part * D:(part + 1) * D] = _taps_fwd(dxc, w, 4, tm).astype(ACT)
            for k in range(4):
                sh = 3 - k
                shifted = (pltpu.roll(ext, sh, 0) if sh else ext)[HALO:HALO + tm, :]
                dw_ref[k:k + 1, part * D:(part + 1) * D] += jnp.sum(dxc[0:tm, :] * shifted, axis=0, keepdims=True)

    c0 = ZQ // D
    cols = (c0, c0 + 1, c0 + 2)
    tile = [pl.BlockSpec((tm, D), lambda i, c=c: (i, c)) for c in cols]
    dtile = [pl.BlockSpec((tm, D), lambda i, c=c: (i, c)) for c in range(3)]
    in_specs = (tile + [_prev_halo(tm, c) for c in cols] + [_next_halo(tm, c, n_tiles) for c in cols]
                + dtile + [_next_halo(tm, c, n_tiles) for c in range(3)] + [_full((HALO, 3 * D))])
    return _pcall(
        body, name=name, grid=(n_tiles,), in_specs=in_specs,
        out_specs=[pl.BlockSpec((tm, 3 * D), lambda i: (i, 0)), _full((HALO, 3 * D))],
        out_shape=[jax.ShapeDtypeStruct((t, 3 * D), ACT), jax.ShapeDtypeStruct((HALO, 3 * D), F32)],
        compiler_params=_cp(1),
    )(*([p] * 9), *([dqkv] * 6), cw)


def _tri_masks(n):
    row = lax.broadcasted_iota(jnp.int32, (n, n), 0)
    col = lax.broadcasted_iota(jnp.int32, (n, n), 1)
    return row, col


@jax.custom_vjp
def _unit_lower_inverses(mats):
    n = DN_CHUNK
    row, col = _tri_masks(n)
    eye = (row == col).astype(F32)
    same16 = (row // 16) == (col // 16)
    same32 = (row // 32) == (col // 32)
    pw = [jnp.where(same16, a, 0.0) for a in mats]
    x = [eye - p for p in pw]
    for _ in range(3):
        pw = [_hnn(p, p) for p in pw]
        x = [_hnn(xi, eye + p) for xi, p in zip(x, pw)]
    for keep in (jnp.logical_and(same32, jnp.logical_not(same16)), jnp.logical_not(same32)):
        inner = [_hnn(jnp.where(keep, a, 0.0), xi) for a, xi in zip(mats, x)]
        x = [xi - _hnn(xi, y) for xi, y in zip(x, inner)]
    return tuple(x)


def _uli_fwd(mats):
    t = _unit_lower_inverses(mats)
    return t, t


def _uli_bwd(ts, gs):
    inner = [_nt(g, t) for g, t in zip(gs, ts)]
    return (tuple(-_tn(t, y) for t, y in zip(ts, inner)),)


_unit_lower_inverses.defvjp(_uli_fwd, _uli_bwd)


@jax.custom_vjp
def _known_inverses(mats, ts):
    return ts


_known_inverses.defvjp(lambda mats, ts: (ts, ts),
                       lambda ts, gs: (_uli_bwd(ts, gs)[0], tuple(jnp.zeros_like(t) for t in ts)))


def _pick_col(m, k):
    return jnp.sum(m * _onehot_row(m.shape[1], k), axis=1, keepdims=True)


def _pick_row(m, k):
    hot = (lax.broadcasted_iota(jnp.int32, (m.shape[0], 1), 0) == k).astype(F32)
    return jnp.sum(m * hot, axis=0, keepdims=True)


def _delta_chunk(states, qs, ks, vs, ab, alog, dtb, onw, known_inverses=None):
    n = DN_CHUNK
    heads = range(DN_HEADS)
    row, col = _tri_masks(n)
    incl = row >= col
    lane = lax.broadcasted_iota(jnp.int32, (1, HEAD), 1)
    g_all = jnp.where(lane < DN_HEADS, -jnp.exp(alog) * jax.nn.softplus(ab + dtb), 0.0)
    c_cols = _hnn(incl.astype(F32), g_all)
    c_rows = _htn(g_all, (row <= col).astype(F32))
    g_tot = jnp.sum(g_all, axis=0, keepdims=True)
    beta_all = jax.nn.sigmoid(ab)
    ccol = [_pick_col(c_cols, h) for h in heads]
    crow = [_pick_row(c_rows, h) for h in heads]
    gl = [_pick_col(g_tot, h) for h in heads]
    beta = [_pick_col(beta_all, DN_HEADS + h) for h in heads]
    decay = [jnp.exp(jnp.where(incl, ccol[h] - crow[h], -1e30)) for h in heads]
    eg = [jnp.exp(ccol[h]) for h in heads]
    kb = [ks[h] * beta[h] for h in heads]
    amat = [jnp.where(row > col, _nt(kb[h], ks[h]) * decay[h], 0.0) for h in heads]
    if known_inverses is None:
        tmat = _unit_lower_inverses(tuple(amat))
    else:
        tmat = _known_inverses(tuple(amat), tuple(known_inverses))
    u = [_nn(tmat[h], vs[h] * beta[h]) for h in heads]
    w = [_nn(tmat[h], kb[h] * eg[h]) for h in heads]
    qk = [_nt(qs[h], ks[h]) * decay[h] for h in heads]
    v_new = [u[h] - _nn(w[h], states[h]) for h in heads]
    o = [_nn(qs[h] * eg[h], states[h]) + _nn(qk[h], v_new[h]) for h in heads]
    new_states = [states[h] * jnp.exp(gl[h]) + _tn(ks[h] * jnp.exp(gl[h] - ccol[h]), v_new[h]) for h in heads]
    ys = [o[h] * lax.rsqrt(jnp.mean(o[h] * o[h], axis=-1, keepdims=True) + EPS) * onw for h in heads]
    return (ys, new_states), tmat


def _head_cols(ref, r0):
    return [ref[r0:r0 + DN_CHUNK, h * HEAD:(h + 1) * HEAD].astype(F32) for h in range(DN_HEADS)]


def _delta_rows(t):
    return min(DN_CHUNKS_PER_STEP, t // DN_CHUNK) * DN_CHUNK


def _delta_fwd(qkv, ab, alog, dtb, onw, name):
    t = qkv.shape[0]
    rows = _delta_rows(t)
    per_step = rows // DN_CHUNK

    def body(q_ref, k_ref, v_ref, ab_ref, alog_ref, dtb_ref, onw_ref, y_ref, keep_ref, inv_ref, state):
        @pl.when(pl.program_id(0) == 0)
        def _():
            state[...] = jnp.zeros_like(state)

        s = [state[hh] for hh in range(DN_HEADS)]
        for c in range(per_step):
            r0 = c * DN_CHUNK
            (ys, s1), tmat = _delta_chunk(s, _head_cols(q_ref, r0), _head_cols(k_ref, r0), _head_cols(v_ref, r0),
                                          ab_ref[r0:r0 + DN_CHUNK, :], alog_ref[...], dtb_ref[...], onw_ref[...])
            for hh in range(DN_HEADS):
                keep_ref[c, hh] = s[hh]
                inv_ref[c, hh] = tmat[hh]
                y_ref[r0:r0 + DN_CHUNK, hh * HEAD:(hh + 1) * HEAD] = ys[hh].astype(ACT)
            s = s1
        for hh in range(DN_HEADS):
            state[hh] = s[hh]

    block = [pl.BlockSpec((rows, D), lambda i, c=c: (i, c)) for c in range(3)]
    nc = t // DN_CHUNK
    return _pcall(
        body, name=name, grid=(t // rows,),
        in_specs=block + [pl.BlockSpec((rows, HEAD), lambda i: (i, 0)), _full((1, HEAD)), _full((1, HEAD)),
                          _full((1, HEAD))],
        out_specs=[pl.BlockSpec((rows, D), lambda i: (i, 0)),
                   pl.BlockSpec((per_step, DN_HEADS, HEAD, HEAD), lambda i: (i, 0, 0, 0)),
                   pl.BlockSpec((per_step, DN_HEADS, DN_CHUNK, DN_CHUNK), lambda i: (i, 0, 0, 0))],
        out_shape=[jax.ShapeDtypeStruct((t, D), ACT), jax.ShapeDtypeStruct((nc, DN_HEADS, HEAD, HEAD), F32),
                   jax.ShapeDtypeStruct((nc, DN_HEADS, DN_CHUNK, DN_CHUNK), F32)],
        scratch_shapes=[pltpu.VMEM((DN_HEADS, HEAD, HEAD), F32)], compiler_params=_cp(1),
    )(qkv, qkv, qkv, ab, alog, dtb, onw)


def _delta_bwd(qkv, ab, alog, dtb, onw, keep, inv, dy, name):
    t = qkv.shape[0]
    rows = _delta_rows(t)
    per_step, n_steps = rows // DN_CHUNK, t // rows

    def body(q_ref, k_ref, v_ref, ab_ref, alog_ref, dtb_ref, onw_ref, keep_ref, inv_ref, dy_ref,
             dqkv_ref, dab_ref, dalog_ref, ddtb_ref, donw_ref, dstate):
        @pl.when(pl.program_id(0) == 0)
        def _():
            dstate[...] = jnp.zeros_like(dstate)
            dalog_ref[...] = jnp.zeros_like(dalog_ref)
            ddtb_ref[...] = jnp.zeros_like(ddtb_ref)
            donw_ref[...] = jnp.zeros_like(donw_ref)

        ds = [dstate[hh] for hh in range(DN_HEADS)]
        for c in reversed(range(per_step)):
            r0 = c * DN_CHUNK
            s0 = [keep_ref[c, hh] for hh in range(DN_HEADS)]
            known = [inv_ref[c, hh] for hh in range(DN_HEADS)]
            _, vjp, _ = jax.vjp(functools.partial(_delta_chunk, known_inverses=known), s0, _head_cols(q_ref, r0),
                                _head_cols(k_ref, r0), _head_cols(v_ref, r0), ab_ref[r0:r0 + DN_CHUNK, :], alog_ref[...],
                                dtb_ref[...], onw_ref[...], has_aux=True)
            ds, dq, dk, dv, dab, dal, ddt, don = vjp((_head_cols(dy_ref, r0), ds))
            for hh in range(DN_HEADS):
                lo = hh * HEAD
                dqkv_ref[r0:r0 + DN_CHUNK, lo:lo + HEAD] = dq[hh].astype(ACT)
                dqkv_ref[r0:r0 + DN_CHUNK, D + lo:D + lo + HEAD] = dk[hh].astype(ACT)
                dqkv_ref[r0:r0 + DN_CHUNK, 2 * D + lo:2 * D + lo + HEAD] = dv[hh].astype(ACT)
            dab_ref[r0:r0 + DN_CHUNK, :] = dab
            dalog_ref[...] += dal
            ddtb_ref[...] += ddt
            donw_ref[...] += don
        for hh in range(DN_HEADS):
            dstate[hh] = ds[hh]

    rev = lambda i: n_steps - 1 - i
    block = [pl.BlockSpec((rows, D), lambda i, c=c: (rev(i), c)) for c in range(3)]
    small = jax.ShapeDtypeStruct((1, HEAD), F32)
    return _pcall(
        body, name=name, grid=(n_steps,),
        in_specs=block + [pl.BlockSpec((rows, HEAD), lambda i: (rev(i), 0)), _full((1, HEAD)), _full((1, HEAD)),
                          _full((1, HEAD)), pl.BlockSpec((per_step, DN_HEADS, HEAD, HEAD), lambda i: (rev(i), 0, 0, 0)),
                          pl.BlockSpec((per_step, DN_HEADS, DN_CHUNK, DN_CHUNK), lambda i: (rev(i), 0, 0, 0)),
                          pl.BlockSpec((rows, D), lambda i: (rev(i), 0))],
        out_specs=[pl.BlockSpec((rows, 3 * D), lambda i: (rev(i), 0)),
                   pl.BlockSpec((rows, HEAD), lambda i: (rev(i), 0)), _full((1, HEAD)), _full((1, HEAD)),
                   _full((1, HEAD))],
        out_shape=[jax.ShapeDtypeStruct((t, 3 * D), ACT), jax.ShapeDtypeStruct((t, HEAD), F32), small, small, small],
        scratch_shapes=[pltpu.VMEM((DN_HEADS, HEAD, HEAD), F32)], compiler_params=_cp(1),
    )(qkv, qkv, qkv, ab, alog, dtb, onw, keep, inv, dy)


def _loss_head(y, target):
    t = y.shape[0]
    tm = min(512, t)

    def body(y_ref, t_ref, l_ref, dy_ref):
        @pl.when(pl.program_id(0) == 0)
        def _():
            l_ref[...] = jnp.zeros_like(l_ref)

        diff = y_ref[...] - t_ref[...]
        dy_ref[...] = diff * (1.0 / D)
        l_ref[...] += 0.5 * jnp.sum(jnp.sum(diff * diff, axis=-1, keepdims=True) * (1.0 / D), axis=0, keepdims=True)

    row = pl.BlockSpec((tm, D), lambda i: (i, 0))
    return _pcall(body, name="loss_head", grid=(t // tm,), in_specs=[row, row], out_specs=[_full((8, 128)), row],
                  out_shape=[jax.ShapeDtypeStruct((8, 128), F32), jax.ShapeDtypeStruct((t, D), F32)],
                  compiler_params=_cp(1))(y, target)


_SMALL_SHARDED = (("a_ln_w", (2, 128), 1), ("a_ln_b", (2, 128), 1), ("b_conv_w", (1, 3, 128), 2),
                  ("c_conv_w", (1, 4, 384), 2))
_REPLICATED = (("mem_norm_w", (1024,)), ("norm_pre", (4, 1024)), ("norm_post", (4, 1024)),
               ("a_b_s", (2, 8, 128)), ("c_a_log", (1, 8)), ("c_dt_bias", (1, 8)), ("c_o_norm_w", (1, 128)))


def _layer_shards(given, prefix):
    w_out = given[prefix + "w_out"]
    w_ins = [given[prefix + "a_w_in"][0], given[prefix + "b_w_in"], given[prefix + "c_w_in"], given[prefix + "a_w_in"][1]]
    return [[w_out[i], w_ins[i]] for i in range(4)]


def _as_matmul_operand(w):
    return w.reshape(w.shape[-2:]).astype(MXU)


def _as_rows(shape):
    return (math.prod(shape[:-1]) if len(shape) > 1 else 1, shape[-1])


def _row_table(shapes):
    table, at = [], 0
    for shape in shapes:
        rows, cols = _as_rows(shape)
        table.append((at, rows, cols))
        at += rows
    return table


def _pack_rows(arrays, shapes, rows, lead=()):
    n_lead = len(lead)
    parts = [jnp.pad(a.reshape(lead + _as_rows(s)), [(0, 0)] * (n_lead + 1) + [(0, D - s[-1])])
             for a, s in zip(arrays, shapes)]
    block = jnp.concatenate(parts, axis=n_lead)
    return jnp.pad(block, [(0, 0)] * n_lead + [(0, rows - block.shape[n_lead]), (0, 0)])


def _unpack_rows(packed, shapes, lead=()):
    n_lead = len(lead)
    return [packed[(slice(None),) * n_lead + (slice(r0, r0 + nr), slice(0, nc))].reshape(lead + s)
            for (r0, nr, nc), s in zip(_row_table(shapes), shapes)]


def _join_shards(blocks, axis):
    moved = jnp.moveaxis(blocks, 0, axis)
    shape = moved.shape
    return moved.reshape(shape[:axis] + (shape[axis] * shape[axis + 1],) + shape[axis + 2:])


def _split_shards(full, axis):
    shape = full.shape
    split = full.reshape(shape[:axis] + (N_DEV, shape[axis] // N_DEV) + shape[axis + 1:])
    return jnp.moveaxis(split, axis, 0)


_A_COLS = ((2560, 1536), (2048, 512), (0, 2048))
_BC_COLS = ((3584, 1536), (3072, 512), (0, 3072))
_C_COLS = ((3600, 1536), (3088, 512), (0, 3072))


def _reorder_cols(w, cols):
    return jnp.concatenate([w[:, s:s + n] for s, n in cols], axis=1)


def _restore_cols(pieces_in_my_order, cols, extra=()):
    placed = sorted(list(zip([s for s, _ in cols], pieces_in_my_order)) + list(extra), key=lambda sp: sp[0])
    return jnp.concatenate([piece for _, piece in placed], axis=1)


def kernel(x, mem, mem_norm_w, w_mem_kv, norm_pre, norm_post, w_out, a_w_in, a_ln_w, a_ln_b, a_w_s, a_b_s, b_w_in, b_conv_w, c_w_in, c_conv_w, c_a_log, c_dt_bias, c_o_norm_w, loss_target, m_mem_norm_w, m_w_mem_kv, m_norm_pre, m_norm_post, m_w_out, m_a_w_in, m_a_ln_w, m_a_ln_b, m_a_w_s, m_a_b_s, m_b_w_in, m_b_conv_w, m_c_w_in, m_c_conv_w, m_c_a_log, m_c_dt_bias, m_c_o_norm_w, v_mem_norm_w, v_w_mem_kv, v_norm_pre, v_norm_post, v_w_out, v_a_w_in, v_a_ln_w, v_a_ln_b, v_a_w_s, v_a_b_s, v_b_w_in, v_b_conv_w, v_c_w_in, v_c_conv_w, v_c_a_log, v_c_dt_bias, v_c_o_norm_w):
    given = dict(locals())
    x0 = x[0]
    mem0 = mem[0]
    target = loss_target[0]

    w_sh, m_sh, v_sh = (_layer_shards(given, pre) for pre in ("", "m_", "v_"))
    small_names = [n for n, _, _ in _SMALL_SHARDED]
    small_shapes = [s for _, s, _ in _SMALL_SHARDED]
    repl_names = [n for n, _ in _REPLICATED]
    repl_shapes = [s for _, s in _REPLICATED]
    small = _pack_rows([given[n] for n in small_names], small_shapes, R_SMALL)
    g_in0, = _all_gather([_as_matmul_operand(w_sh[0][1])], "gather_weights")
    w_in = [_reorder_cols(_join_shards(g_in0, 1), _A_COLS)]
    wouts, w_cab = [], None
    ws = [a_w_s[j] for j in range(2)]
    bst = [jnp.pad(a_b_s[j].T, ((0, 0), (0, HEAD - GM_GROUPS))) for j in range(2)]
    alog = jnp.pad(c_a_log, ((0, 0), (0, HEAD - DN_HEADS)))
    dtb = jnp.pad(c_dt_bias, ((0, 0), (0, HEAD - DN_HEADS)))
    onw = c_o_norm_w
    mw = mem_norm_w[None, :]

    xs, saved = [x0], []
    for i in range(4):
        kind = i % 3
        npre, npost = norm_pre[i][None, :], norm_post[i][None, :]
        ahead = [_as_matmul_operand(w) for w in w_sh[i + 1][:1 if i == 0 else 2]] if i < 3 else []
        if i == 0:
            ahead = [w_mem_kv.astype(MXU), _as_matmul_operand(w_sh[0][0]), small] + ahead
        res = _proj_fwd(xs[i], npre, w_in[i], w_cab if kind == 2 else None, f"proj_fwd_{i}", gather=ahead)
        if i == 0:
            g_wkv, g_wout0, g_small = res[2:5]
            small_full = _unpack_rows(g_small, small_shapes, lead=(N_DEV,))
            full = {n: _join_shards(blocks, ax) for (n, _, ax), blocks in zip(_SMALL_SHARDED, small_full)}
            wkv = g_wkv.reshape(D, 2 * D_XA)
            wouts.append(g_wout0.reshape(D_CAT, D))
            lnw = [full["a_ln_w"][j][None, :] for j in range(2)]
            lnb = [full["a_ln_b"][j][None, :] for j in range(2)]
            cw_b = jnp.pad(full["b_conv_w"][0], ((0, HALO - 3), (0, 0)))
            cw_c = jnp.pad(full["c_conv_w"][0], ((0, HALO - 4), (0, 0)))
            kv = _memkv_fwd(mem0, mw, wkv).astype(MXU)
        if ahead:
            wouts.append(res[-1 if i == 0 else -2].reshape(D_CAT, D))
            g_in = res[-1] if i > 0 else None
        if kind == 2:
            p, h, ab = res[:3]
            qkv = _qkv_fwd(p, cw_c, f"qkv_fwd_{i}")
            ymix, keep, inv = _delta_fwd(qkv, ab, alog, dtb, onw, f"delta_fwd_{i}")
            extra = (qkv, ab, keep, inv)
        else:
            p, h = res[:2]
            if kind == 0:
                ymix = _gmlp_fwd(p, lnw[i // 3], lnb[i // 3], ws[i // 3], bst[i // 3], f"gmlp_fwd_{i}")
            else:
                ymix = _sconv_fwd(p, cw_b, f"sconv_fwd_{i}")
            extra = ()
        res = _tail_fwd(p, ymix, xs[i], kv, wouts[i], npost, f"tail_fwd_{i}",
                        gather=[_as_matmul_operand(w_sh[1][1])] if i == 0 else [])
        xn, o = res[:2]
        if i == 0:
            g_in = res[2]
        if i + 1 == 2:
            c_full = jnp.concatenate([g_in[d] for d in range(N_DEV)], axis=1)
            w_in.append(_reorder_cols(c_full, _C_COLS))
            w_cab = jnp.pad(c_full[:, 3072:3088], ((0, 0), (0, HEAD - 16)))
        elif i < 3:
            w_in.append(_reorder_cols(_join_shards(g_in, 1), _BC_COLS if i + 1 == 1 else _A_COLS))
        xs.append(xn)
        saved.append((p, h, ymix, o, extra))

    loss_tile, dx = _loss_head(xs[4], target)
    loss = lax.psum(loss_tile[0, 0], ("x", "y", "c"))

    g = {}
    d_npre, d_npost = [None] * 4, [None] * 4
    d_ws, d_bs, d_lnw, d_lnb = [None] * 2, [None] * 2, [None] * 2, [None] * 2
    dkv = None
    pend_win, pend_wout = None, None
    landed_win, landed_wout = [None] * 4, [None] * 4
    for i in reversed(range(4)):
        kind = i % 3
        p, h, ymix, o, extra = saved[i]
        npre, npost = norm_pre[i][None, :], norm_post[i][None, :]
        res = _tail_bwd(dx, o, p, ymix, kv, wouts[i], npost, f"tail_bwd_{i}", exchange=[pend_win] if i < 3 else [])
        dzq, dymix, d_wout, d_npost[i], dkv_i = res[:5]
        if i < 3:
            landed_win[i + 1] = res[5]
        d_wout = d_wout.reshape(N_DEV, D_CAT // N_DEV, D)
        dkv = dkv_i if dkv is None else dkv + dkv_i
        w_zq, w_mix = w_in[i][:, :ZQ], w_in[i][:, ZQ:]
        dw_zq = _matmul_tn(h, dzq, f"dw_zq_{i}")
        if kind == 0:
            j = i // 3
            early_x, early_b = [], []
            if i == 0:
                d_wkv, d_mw = _memkv_bwd(mem0, mw, wkv, dkv)
                early_x = [d_wkv.reshape(N_DEV, 128, D).astype(XCH), d_wout, pend_wout]
                early_b = [d_ws[1].reshape(GM_CHUNK, D).astype(XCH)]
            (dmix, d_lnw[j], d_lnb[j], d_ws[j], dbst), early_landed = _gmlp_bwd(
                p, dymix, lnw[j], lnb[j], ws[j], bst[j], f"gmlp_bwd_{i}", exchange=early_x, bcast=early_b)
            d_bs[j] = dbst[:, :GM_GROUPS].T
            if i == 0:
                g["mem_norm_w"] = d_mw[0]
                g["norm_pre"] = jnp.concatenate([jnp.zeros((1, D), F32)] + d_npre[1:], axis=0)
                g["norm_post"] = jnp.concatenate(d_npost, axis=0)
                g["a_ln_w"] = jnp.concatenate(d_lnw, axis=0)
                g["a_ln_b"] = jnp.concatenate(d_lnb, axis=0)
                g["a_b_s"] = jnp.stack(d_bs)
                e_small = _pack_rows([_split_shards(g[n], ax) for n, _, ax in _SMALL_SHARDED], small_shapes, R_SMALL,
                                     lead=(N_DEV,))
                r_pack = _pack_rows([g[n] for n in repl_names], repl_shapes, R_REPL)
                dw_mix, (l_small, ws0_all, r_all) = _matmul_tn(
                    h, dmix, f"dw_mix_{i}", exchange=[e_small], bcast=[d_ws[0].reshape(GM_CHUNK, D).astype(XCH), r_pack])
            else:
                dw_mix = _matmul_tn(h, dmix, f"dw_mix_{i}")
            d_win = _restore_cols([dw_zq[:, :D_CAT], dw_zq[:, D_CAT:], dw_mix], _A_COLS)
            dps, wparts = [dzq, dmix], [w_zq, w_mix]
        elif kind == 1:
            dmix, dcw = _sconv_bwd(p, dymix, cw_b, f"sconv_bwd_{i}")
            g["b_conv_w"] = dcw[None, :3]
            dw_mix = _matmul_tn(h, dmix, f"dw_mix_{i}")
            d_win = _restore_cols([dw_zq[:, :D_CAT], dw_zq[:, D_CAT:], dw_mix], _BC_COLS)
            dps, wparts = [dzq, dmix], [w_zq, w_mix]
        else:
            qkv, ab, keep, inv = extra
            dqkv, dab, dalog, ddtb, donw = _delta_bwd(qkv, ab, alog, dtb, onw, keep, inv, dymix, f"delta_bwd_{i}")
            dmix, dcw = _qkv_bwd(p, dqkv, cw_c, f"qkv_bwd_{i}")
            g["c_conv_w"] = dcw[None, :4]
            g["c_a_log"], g["c_dt_bias"], g["c_o_norm_w"] = dalog[:, :DN_HEADS], ddtb[:, :DN_HEADS], donw
            dw_mix = _matmul_tn(h, dmix, f"dw_mix_{i}")
            dw_ab = _matmul_tn(h, dab, f"dw_ab_{i}")
            d_win = _restore_cols([dw_zq[:, :D_CAT], dw_zq[:, D_CAT:], dw_mix], _C_COLS, extra=[(3072, dw_ab[:, :16])])
            dps, wparts = [dzq, dmix, dab], [w_zq, w_mix, w_cab]
        width = d_win.shape[1] // N_DEV
        pend_win = jnp.stack([d_win[:, d * width:(d + 1) * width] for d in range(N_DEV)]).astype(XCH)
        if i > 0:
            res = _proj_bwd_x(dps, wparts, xs[i], dx, npre, f"proj_bwd_x_{i}", exchange=[pend_wout] if i < 3 else [])
            dx, d_npre[i] = res[:2]
            if i < 3:
                landed_wout[i + 1] = res[2]
            pend_wout = d_wout

    dx, d_npre0, l_in = _proj_bwd_x(dps, wparts, xs[0], dx, norm_pre[0][None, :], "proj_bwd_x_0", exchange=[pend_win])
    l_wkv, landed_wout[0], landed_wout[1], ws3_all = early_landed
    landed_win[0] = l_in
    npre0_all, = _all_gather([jnp.pad(d_npre0, ((0, HALO - 1), (0, 0)))], "gather_norm_pre0")
    r_all = r_all.at[:, 1, :].set(npre0_all[:, 0, :])

    res = [[_reduce_adamw(parts, w_sh[i][a], m_sh[i][a], v_sh[i][a], f"adamw_{i}_{a}")
            for a, parts in enumerate((landed_wout[i], landed_win[i]))] for i in range(4)]
    res_wkv = _reduce_adamw(l_wkv, w_mem_kv, m_w_mem_kv, v_w_mem_kv, "adamw_w_mem_kv")
    res_small = _reduce_adamw_vectors(
        l_small, small, *(_pack_rows([given[pre + n] for n in small_names], small_shapes, R_SMALL) for pre in ("m_", "v_")),
        small_shapes, "adamw_small")
    res_ws = [_reduce_adamw(parts, *(given[pre + "a_w_s"][j].reshape(GM_CHUNK, D) for pre in ("", "m_", "v_")),
                            f"adamw_a_w_s_{j}") for j, parts in enumerate((ws0_all, ws3_all))]
    res_repl = _reduce_adamw_vectors(
        r_all, *(_pack_rows([given[pre + n] for n in repl_names], repl_shapes, R_REPL) for pre in ("", "m_", "v_")),
        repl_shapes, "adamw_replicated")

    order = ["mem_norm_w", "w_mem_kv", "norm_pre", "norm_post", "w_out", "a_w_in", "a_ln_w", "a_ln_b", "a_w_s", "a_b_s",
             "b_w_in", "b_conv_w", "c_w_in", "c_conv_w", "c_a_log", "c_dt_bias", "c_o_norm_w"]
    outs = [loss, dx[None]]
    for kind in range(4):
        got = dict(zip(repl_names, res_repl[kind]))
        got["a_w_s"] = jnp.stack([r[kind] for r in res_ws]).reshape(a_w_s.shape)
        got.update(zip(small_names, res_small[kind]))
        got["w_mem_kv"] = res_wkv[kind]
        got["w_out"] = jnp.stack([res[i][0][kind] for i in range(4)])
        got["a_w_in"] = jnp.stack([res[0][1][kind], res[3][1][kind]])
        got["b_w_in"] = res[1][1][kind]
        got["c_w_in"] = res[2][1][kind]
        outs += [got[n] for n in order]
    return tuple(outs)
```

```python
import functools
import math

import jax
import jax.numpy as jnp
from jax import lax
from jax.experimental import pallas as pl
from jax.experimental.pallas import tpu as pltpu

F32 = jnp.float32
MXU = jnp.bfloat16
ACT = jnp.bfloat16

D = 1024
D_XA = 512
D_CAT = 1536
N_MEM = 256
XA_HEADS = 4
HEAD = 128
ZQ = D_CAT + D_XA
EPS = 1e-6
GM_CHUNK = 128
GM_CHUNKS_PER_STEP = 8
GM_GROUPS = 8
DN_HEADS = 8
DN_CHUNK = 64
DN_CHUNKS_PER_STEP = 8
N_DEV = 8
HALO = 8
VMEM_LIMIT = 56 * 1024 * 1024
XCH = jnp.bfloat16
R_REPL = 32
R_SMALL = 16

ADAM_LR = 0.001
ADAM_B1 = 0.9
ADAM_B2 = 0.999
ADAM_EPS = 1e-08
ADAM_WD = 0.01
ADAM_STEP = 10

NN = ((1,), (0,))
NT = ((1,), (1,))
TN = ((0,), (0,))
MESH = pl.DeviceIdType.MESH


def _pcall(body, **kw):
    return pl.pallas_call(body, **kw)


def _cp(n_axes):
    return pltpu.CompilerParams(dimension_semantics=("arbitrary",) * n_axes, vmem_limit_bytes=VMEM_LIMIT)


def _dot(a, b, dims, prec=None):
    return lax.dot_general(a, b, (dims, ((), ())), preferred_element_type=F32, precision=prec)


def _mdot(a, b, dims):
    return _dot(a.astype(MXU), b.astype(MXU), dims)


def _make_mms(raw):
    @jax.custom_vjp
    def nn(a, b):
        return raw(a, b, NN)

    @jax.custom_vjp
    def nt(a, b):
        return raw(a, b, NT)

    @jax.custom_vjp
    def tn(a, b):
        return raw(a, b, TN)

    nn.defvjp(lambda a, b: (nn(a, b), (a, b)), lambda r, g: (nt(g, r[1]), tn(r[0], g)))
    nt.defvjp(lambda a, b: (nt(a, b), (a, b)), lambda r, g: (nn(g, r[1]), tn(g, r[0])))
    tn.defvjp(lambda a, b: (tn(a, b), (a, b)), lambda r, g: (nt(r[1], g), nn(r[0], g)))
    return nn, nt, tn


_nn, _nt, _tn = _make_mms(_mdot)


def _split_dot(a, b, dims):
    ah = a.astype(jnp.bfloat16)
    bh = b.astype(jnp.bfloat16)
    al = (a - ah.astype(F32)).astype(jnp.bfloat16)
    bl = (b - bh.astype(F32)).astype(jnp.bfloat16)
    return _dot(ah, bh, dims) + (_dot(ah, bl, dims) + _dot(al, bh, dims))


_hnn, _hnt, _htn = _make_mms(_split_dot)


def _full(shape):
    return pl.BlockSpec(shape, lambda *_: (0,) * len(shape))


def _silu(z):
    return z * jax.nn.sigmoid(z)


def _dsilu(z):
    s = jax.nn.sigmoid(z)
    return s * (1.0 + z * (1.0 - s))


def _onehot_row(n, k):
    return (lax.broadcasted_iota(jnp.int32, (1, n), 1) == k).astype(F32)


_HBM = pl.BlockSpec(memory_space=pl.ANY)


def _sem_shapes(n_remote, n_local):
    return [pltpu.SemaphoreType.DMA((n_remote,)), pltpu.SemaphoreType.DMA((n_remote,)),
            pltpu.SemaphoreType.DMA((n_local,))]


def _gather_parts(x_refs, out_refs, send_sems, recv_sems, local_sems):
    n = len(x_refs)
    x, y, cc = lax.axis_index("x"), lax.axis_index("y"), lax.axis_index("c")
    me, sibling = (x, y, cc), (x, y, 1 - cc)
    chips = [(1 - x, y), (x, 1 - y), (1 - x, 1 - y)]

    def slot(a, px, py, pc):
        return out_refs[a].at[4 * px + 2 * py + pc]

    def copy(k, a, block, to, src=None):
        return pltpu.make_async_remote_copy(
            src_ref=slot(a, *block) if src is None else src, dst_ref=slot(a, *block),
            send_sem=send_sems.at[k * n + a], recv_sem=recv_sems.at[k * n + a], device_id=to, device_id_type=MESH)

    mine = [pltpu.make_async_copy(x_refs[a], slot(a, *me), local_sems.at[a]) for a in range(n)]
    first = [copy(0, a, me, sibling, src=x_refs[a]) for a in range(n)]
    first += [copy(1 + j, a, me, (*chip, cc), src=x_refs[a]) for j, chip in enumerate(chips) for a in range(n)]

    def begin():
        for cp in mine + first:
            cp.start()

    def end():
        passed = []
        for j, chip in enumerate(chips):
            for a in range(n):
                copy(1 + j, a, (*chip, cc), me).wait_recv()
                passed.append(copy(4 + j, a, (*chip, cc), sibling))
                passed[-1].start()
        for a in range(n):
            copy(0, a, sibling, me).wait_recv()
        for j, chip in enumerate(chips):
            for a in range(n):
                copy(4 + j, a, (*chip, 1 - cc), me).wait_recv()
        for cp in first + passed:
            cp.wait_send()
        for cp in mine:
            cp.wait()

    return begin, end


def _exchange_parts(g_refs, out_refs, b_refs, ball_refs, send_sems, recv_sems, local_sems):
    n, nb = len(g_refs), len(b_refs)
    per_peer = n + nb
    x, y, cc = lax.axis_index("x"), lax.axis_index("y"), lax.axis_index("c")
    my_idx = 4 * x + 2 * y + cc
    mine = [pltpu.make_async_copy(g_refs[a].at[my_idx], out_refs[a].at[my_idx], local_sems.at[a]) for a in range(n)]
    mine += [pltpu.make_async_copy(b_refs[a], ball_refs[a].at[my_idx], local_sems.at[n + a]) for a in range(nb)]
    copies = []
    for k in range(1, N_DEV):
        px = 1 - x if k & 4 else x
        py = 1 - y if k & 2 else y
        pc = 1 - cc if k & 1 else cc
        base = (k - 1) * per_peer
        for a in range(n):
            copies.append(pltpu.make_async_remote_copy(
                src_ref=g_refs[a].at[4 * px + 2 * py + pc], dst_ref=out_refs[a].at[my_idx],
                send_sem=send_sems.at[base + a], recv_sem=recv_sems.at[base + a],
                device_id=(px, py, pc), device_id_type=MESH))
        for a in range(nb):
            copies.append(pltpu.make_async_remote_copy(
                src_ref=b_refs[a], dst_ref=ball_refs[a].at[my_idx], send_sem=send_sems.at[base + n + a],
                recv_sem=recv_sems.at[base + n + a], device_id=(px, py, pc), device_id_type=MESH))

    def begin():
        for cp in mine + copies:
            cp.start()

    def end():
        for cp in copies:
            cp.wait_recv()
        for cp in copies:
            cp.wait_send()
        for cp in mine:
            cp.wait()

    return begin, end


def _pcall_hosting(body, args, exchange, bcast, *, name, grid, in_specs, out_specs, out_shape, compiler_params,
                   scratch_shapes=(), gather=()):
    n_in, n_out, n_x, n_b, n_g = len(in_specs), len(out_specs), len(exchange), len(bcast), len(gather)
    n_d = n_x + n_b
    n_c, n_scr = n_d + n_g, len(scratch_shapes)

    def hosted(*refs):
        ins, c_in = refs[:n_in], refs[n_in:n_in + n_c]
        outs, c_out = refs[n_in + n_c:n_in + n_c + n_out], refs[n_in + n_c + n_out:n_in + 2 * n_c + n_out]
        scratch, sems = refs[n_in + 2 * n_c + n_out:][:n_scr], refs[n_in + 2 * n_c + n_out + n_scr:]
        parts = []
        if n_d:
            parts.append(_exchange_parts(c_in[:n_x], c_out[:n_x], c_in[n_x:n_d], c_out[n_x:n_d], *sems[:3]))
        if n_g:
            parts.append(_gather_parts(c_in[n_d:], c_out[n_d:], *sems[-3:]))
        first = functools.reduce(jnp.logical_and, [pl.program_id(k) == 0 for k in range(len(grid))])
        last = functools.reduce(jnp.logical_and, [pl.program_id(k) == grid[k] - 1 for k in range(len(grid))])
        for begin, _ in parts:
            pl.when(first)(begin)
        body(*ins, *outs, *scratch)
        for _, end in parts:
            pl.when(last)(end)

    landed_shape = [jax.ShapeDtypeStruct(e.shape, e.dtype) for e in exchange]
    landed_shape += [jax.ShapeDtypeStruct((N_DEV,) + b.shape, b.dtype) for b in list(bcast) + list(gather)]
    sems = (_sem_shapes(7 * n_d, n_d) if n_d else []) + (_sem_shapes(7 * n_g, n_g) if n_g else [])
    res = _pcall(hosted, name=name, grid=grid, in_specs=list(in_specs) + [_HBM] * n_c,
                 out_specs=list(out_specs) + [_HBM] * n_c, out_shape=list(out_shape) + landed_shape,
                 scratch_shapes=list(scratch_shapes) + sems,
                 compiler_params=compiler_params)(*args, *exchange, *bcast, *gather)
    return res[:n_out], res[n_out:]


def _all_gather(blks, name):
    n = len(blks)

    def body(*refs):
        begin, end = _gather_parts(refs[:n], refs[n:2 * n], *refs[2 * n:])
        begin()
        end()

    return _pcall(body, name=name, out_shape=[jax.ShapeDtypeStruct((N_DEV,) + b.shape, b.dtype) for b in blks],
                  in_specs=[_HBM] * n, out_specs=[_HBM] * n, scratch_shapes=_sem_shapes(7 * n, n))(*blks)


def _sum_and_adamw(p_ref, w, m, v):
    g = p_ref[0].astype(F32)
    for s in range(1, N_DEV):
        g = g + p_ref[s].astype(F32)
    nm = ADAM_B1 * m + (1.0 - ADAM_B1) * g
    nv = ADAM_B2 * v + (1.0 - ADAM_B2) * (g * g)
    m_hat = nm / (1.0 - ADAM_B1 ** ADAM_STEP)
    v_hat = nv / (1.0 - ADAM_B2 ** ADAM_STEP)
    return g, -ADAM_LR * (m_hat / (jnp.sqrt(v_hat) + ADAM_EPS) + ADAM_WD * w), nm, nv


def _reduce_adamw_vectors(parts, w, m, v, shapes, name):
    table = _row_table(shapes)

    def body(p_ref, w_ref, m_ref, v_ref, *out_refs):
        results = _sum_and_adamw(p_ref, w_ref[...], m_ref[...], v_ref[...])
        for kind, val in enumerate(results):
            for t, (r0, nr, nc) in enumerate(table):
                out_refs[kind * len(table) + t][...] = val[r0:r0 + nr, 0:nc]

    outs = _pcall(body, name=name, out_shape=[jax.ShapeDtypeStruct((nr, nc), F32) for _, nr, nc in table] * 4)(parts, w, m, v)
    return [[outs[kind * len(table) + t].reshape(s) for t, s in enumerate(shapes)] for kind in range(4)]


def _reduce_adamw(parts, w, m, v, name):
    lead = w.shape[:-2]
    r, c = w.shape[-2:]
    tr = 128 if r % 128 == 0 else r
    zeros = (0,) * len(lead)
    at = zeros + (slice(None), slice(None))

    def body(p_ref, w_ref, m_ref, v_ref, g_ref, d_ref, nm_ref, nv_ref):
        g_ref[at], d_ref[at], nm_ref[at], nv_ref[at] = _sum_and_adamw(p_ref, w_ref[at], m_ref[at], v_ref[at])

    row = pl.BlockSpec((1,) * len(lead) + (tr, c), lambda i: zeros + (i, 0))
    out = jax.ShapeDtypeStruct(w.shape, F32)
    return _pcall(
        body, name=name, grid=(r // tr,),
        in_specs=[pl.BlockSpec((N_DEV, tr, c), lambda i: (0, i, 0)), row, row, row],
        out_specs=[row, row, row, row], out_shape=[out, out, out, out], compiler_params=_cp(1),
    )(parts, w, m, v)


def _proj_fwd(x, nw, w, wab, name, gather=()):
    t, npj = x.shape[0], w.shape[1]
    tm, tn = min(512, t), 1024
    has_ab = wab is not None
    n_in, n_out, n_g = 3 + has_ab, 2 + has_ab, len(gather)
    n_i = t // tm

    def body(*refs):
        ins, g_ins = refs[:n_in], refs[n_in:n_in + n_g]
        outs = refs[n_in + n_g:n_in + n_g + n_out]
        g_outs = refs[n_in + n_g + n_out:n_in + 2 * n_g + n_out]
        if has_ab:
            (x_ref, nw_ref, w_ref, wab_ref), (p_ref, h_ref, ab_ref) = ins, outs
        else:
            (x_ref, nw_ref, w_ref), (p_ref, h_ref) = ins, outs
        if n_g:
            begin, end = _gather_parts(g_ins, g_outs, *refs[n_in + 2 * n_g + n_out:])
            pl.when(pl.program_id(0) == 0)(begin)

        for rows in (slice(0, tm // 2), slice(tm // 2, tm)):
            xv = x_ref[rows, :]
            hv = (xv * lax.rsqrt(jnp.mean(xv * xv, axis=-1, keepdims=True) + EPS) * nw_ref[...]).astype(MXU)
            h_ref[rows, :] = hv.astype(ACT)
            if has_ab:
                ab_ref[rows, :] = _dot(hv, wab_ref[...], NN)
            for j in range(npj // tn):
                p_ref[rows, j * tn:(j + 1) * tn] = _dot(hv, w_ref[:, j * tn:(j + 1) * tn], NN).astype(ACT)
        if n_g:
            pl.when(pl.program_id(0) == n_i - 1)(end)

    in_specs = [pl.BlockSpec((tm, D), lambda i: (i, 0)), _full((1, D)), _full((D, npj))]
    out_specs = [pl.BlockSpec((tm, npj), lambda i: (i, 0)), pl.BlockSpec((tm, D), lambda i: (i, 0))]
    out_shape = [jax.ShapeDtypeStruct((t, npj), ACT), jax.ShapeDtypeStruct((t, D), ACT)]
    args = [x, nw, w]
    if has_ab:
        in_specs.append(_full((D, HEAD)))
        out_specs.append(pl.BlockSpec((tm, HEAD), lambda i: (i, 0)))
        out_shape.append(jax.ShapeDtypeStruct((t, HEAD), F32))
        args.append(wab)
    scratch = []
    if n_g:
        in_specs += [_HBM] * n_g
        out_specs += [_HBM] * n_g
        out_shape += [jax.ShapeDtypeStruct((N_DEV,) + b.shape, b.dtype) for b in gather]
        args += list(gather)
        scratch += _sem_shapes(7 * n_g, n_g)
    return _pcall(body, name=name, grid=(n_i,), in_specs=in_specs, out_specs=out_specs,
                  out_shape=out_shape, scratch_shapes=scratch, compiler_params=_cp(1))(*args)


def _proj_bwd_x(dps, ws, x, dxn, nw, name, exchange=(), bcast=()):
    t = x.shape[0]
    tm = min(512, t)
    n, n_x, n_b = len(dps), len(exchange), len(bcast)
    n_c, n_steps = n_x + n_b, t // tm

    def body(*refs):
        dp_refs, w_refs = refs[:n], refs[n:2 * n]
        x_ref, dxn_ref, nw_ref = refs[2 * n:2 * n + 3]
        c_in = refs[2 * n + 3:2 * n + 3 + n_c]
        dx_ref, dnw_ref = refs[2 * n + 3 + n_c:2 * n + 5 + n_c]
        c_out = refs[2 * n + 5 + n_c:2 * n + 5 + 2 * n_c]
        if n_c:
            begin, end = _exchange_parts(c_in[:n_x], c_out[:n_x], c_in[n_x:], c_out[n_x:], *refs[2 * n + 5 + 2 * n_c:])
            pl.when(pl.program_id(0) == 0)(begin)
        @pl.when(pl.program_id(0) == 0)
        def _():
            dnw_ref[...] = jnp.zeros_like(dnw_ref)

        halves = [slice(r0, r0 + tm // 2) for r0 in (0, tm // 2)]
        dhs = []
        for rows in halves:
            dh = _mdot(dp_refs[0][rows, :], w_refs[0][...], NT)
            for k in range(1, n):
                dh = dh + _mdot(dp_refs[k][rows, :], w_refs[k][...], NT)
            dhs.append(dh)
        for rows, dh in zip(halves, dhs):
            xv = x_ref[rows, :]
            r = lax.rsqrt(jnp.mean(xv * xv, axis=-1, keepdims=True) + EPS)
            dnw_ref[...] += jnp.sum(dh * xv * r, axis=0, keepdims=True)
            dhw = dh * nw_ref[...]
            dx_ref[rows, :] = dxn_ref[rows, :] + r * (dhw - xv * (r * r) * jnp.mean(dhw * xv, axis=-1, keepdims=True))
        if n_c:
            pl.when(pl.program_id(0) == n_steps - 1)(end)

    row = pl.BlockSpec((tm, D), lambda i: (i, 0))
    in_specs = [pl.BlockSpec((tm, dp.shape[1]), lambda i: (i, 0)) for dp in dps]
    in_specs += [_full(w.shape) for w in ws]
    in_specs += [row, row, _full((1, D))] + [_HBM] * n_c
    out_shape = [jax.ShapeDtypeStruct((t, D), F32), jax.ShapeDtypeStruct((1, D), F32)]
    out_shape += [jax.ShapeDtypeStruct(e.shape, e.dtype) for e in exchange]
    out_shape += [jax.ShapeDtypeStruct((N_DEV,) + b.shape, b.dtype) for b in bcast]
    return _pcall(body, name=name, grid=(n_steps,), in_specs=in_specs, out_specs=[row, _full((1, D))] + [_HBM] * n_c,
                  out_shape=out_shape, scratch_shapes=_sem_shapes(7 * n_c, n_c) if n_c else [],
                  compiler_params=_cp(1))(*dps, *ws, x, dxn, nw, *exchange, *bcast)


def _matmul_tn(a, b, name, out_dtype=XCH, exchange=(), bcast=()):
    t, m = a.shape
    n = b.shape[1]
    tm, tn = min(1024, t), min(1024, n)
    n_t = t // tm

    def body(a_ref, b_ref, o_ref, acc):
        @pl.when(pl.program_id(1) == 0)
        def _():
            acc[...] = jnp.zeros_like(acc)

        acc[...] += _mdot(a_ref[...], b_ref[...], TN)

        @pl.when(pl.program_id(1) == n_t - 1)
        def _():
            o_ref[...] = acc[...].astype(out_dtype)

    call = dict(name=name, grid=(n // tn, n_t),
                in_specs=[pl.BlockSpec((tm, m), lambda j, i: (i, 0)), pl.BlockSpec((tm, tn), lambda j, i: (i, j))],
                out_specs=[pl.BlockSpec((m, tn), lambda j, i: (0, j))],
                out_shape=[jax.ShapeDtypeStruct((m, n), out_dtype)], scratch_shapes=[pltpu.VMEM((m, tn), F32)],
                compiler_params=_cp(2))
    if not exchange and not bcast:
        return _pcall(body, **call)(a, b)[0]
    (out,), landed = _pcall_hosting(body, (a, b), exchange, bcast, **call)
    return out, landed


def _memkv_fwd(mem, mw, wkv):
    def body(mem_ref, mw_ref, w_ref, kv_ref):
        mv = mem_ref[...]
        mn = mv * lax.rsqrt(jnp.mean(mv * mv, axis=-1, keepdims=True) + EPS) * mw_ref[...]
        kv_ref[...] = _mdot(mn, w_ref[...], NN)

    return _pcall(body, name="memkv_fwd", out_shape=jax.ShapeDtypeStruct((N_MEM, 2 * D_XA), F32),
                  compiler_params=pltpu.CompilerParams(vmem_limit_bytes=VMEM_LIMIT))(mem, mw, wkv)


def _memkv_bwd(mem, mw, wkv, dkv):
    def body(mem_ref, mw_ref, w_ref, dkv_ref, dw_ref, dmw_ref):
        mv = mem_ref[...]
        r = lax.rsqrt(jnp.mean(mv * mv, axis=-1, keepdims=True) + EPS)
        mn = mv * r * mw_ref[...]
        dkvv = dkv_ref[...]
        dw_ref[...] = _mdot(mn, dkvv, TN)
        dmn = _mdot(dkvv, w_ref[...], NT)
        dmw_ref[...] = jnp.sum(dmn * mv * r, axis=0, keepdims=True)

    return _pcall(body, name="memkv_bwd",
                  out_shape=[jax.ShapeDtypeStruct((D, 2 * D_XA), F32), jax.ShapeDtypeStruct((1, D), F32)],
                  compiler_params=pltpu.CompilerParams(vmem_limit_bytes=VMEM_LIMIT))(mem, mw, wkv, dkv)


def _attend(q, kv):
    heads = range(XA_HEADS)
    qs = [q[:, h * HEAD:(h + 1) * HEAD] for h in heads]
    ks = [kv[:, h * HEAD:(h + 1) * HEAD] for h in heads]
    vs = [kv[:, D_XA + h * HEAD:D_XA + (h + 1) * HEAD] for h in heads]
    ss = [_mdot(qs[h], ks[h], NT) * (HEAD ** -0.5) for h in heads]
    es = [jnp.exp(s - jnp.max(s, axis=-1, keepdims=True)) for s in ss]
    ps = [e / jnp.sum(e, axis=-1, keepdims=True) for e in es]
    return ps, [_mdot(ps[h], vs[h], NN) for h in heads]


def _tail_fwd(p, ymix, x, kv, wout, npost, name, gather=()):
    t = x.shape[0]
    tm = min(512, t)

    def body(z_ref, q_ref, y_ref, x_ref, kv_ref, w_ref, np_ref, xn_ref, o_ref):
        _, outs = _attend(q_ref[...], kv_ref[...])
        cat = jnp.concatenate([y_ref[...]] + [a.astype(ACT) for a in outs], axis=1)
        g = cat * _silu(z_ref[...])
        o = _mdot(g, w_ref[...], NN)
        o_ref[...] = o
        xn_ref[...] = x_ref[...] + o * lax.rsqrt(jnp.mean(o * o, axis=-1, keepdims=True) + EPS) * np_ref[...]

    row = pl.BlockSpec((tm, D), lambda i: (i, 0))
    call = dict(
        name=name, grid=(t // tm,),
        in_specs=[pl.BlockSpec((tm, D_CAT), lambda i: (i, 0)), pl.BlockSpec((tm, D_XA), lambda i: (i, D_CAT // D_XA)),
                  row, row, _full((N_MEM, 2 * D_XA)), _full((D_CAT, D)), _full((1, D))],
        out_specs=[row, row],
        out_shape=[jax.ShapeDtypeStruct((t, D), F32), jax.ShapeDtypeStruct((t, D), F32)], compiler_params=_cp(1))
    args = (p, p, ymix, x, kv, wout, npost)
    if not gather:
        return tuple(_pcall(body, **call)(*args))
    outs, landed = _pcall_hosting(body, args, (), (), gather=gather, **call)
    return tuple(outs) + tuple(landed)


def _tail_bwd(dxn, o, p, ymix, kv, wout, npost, name, exchange=()):
    t = dxn.shape[0]
    tm = min(512, t)
    n_x, n_steps = len(exchange), t // tm

    def body(*refs):
        dxn_ref, o_ref, z_ref, q_ref, y_ref, kv_ref, w_ref, np_ref = refs[:8]
        dzq_ref, dy_ref, dw_ref, dnp_ref, dkv_ref = refs[8 + n_x:13 + n_x]
        dw_acc = refs[13 + 2 * n_x]
        if n_x:
            begin, end = _exchange_parts(refs[8:8 + n_x], refs[13 + n_x:13 + 2 * n_x], (), (), *refs[14 + 2 * n_x:])
            pl.when(pl.program_id(0) == 0)(begin)

        @pl.when(pl.program_id(0) == 0)
        def _():
            dw_acc[...] = jnp.zeros_like(dw_acc)
            dnp_ref[...] = jnp.zeros_like(dnp_ref)
            dkv_ref[...] = jnp.zeros_like(dkv_ref)

        q = q_ref[...]
        kvv = kv_ref[...]
        z = z_ref[...]
        ps, outs = _attend(q, kvv)
        cat = jnp.concatenate([y_ref[...]] + [a.astype(ACT) for a in outs], axis=1)
        sz = _silu(z)
        g = cat * sz
        ov = o_ref[...]
        dr = dxn_ref[...]
        rr = lax.rsqrt(jnp.mean(ov * ov, axis=-1, keepdims=True) + EPS)
        dnp_ref[...] += jnp.sum(dr * ov * rr, axis=0, keepdims=True)
        dow = dr * np_ref[...]
        do = rr * (dow - ov * (rr * rr) * jnp.mean(dow * ov, axis=-1, keepdims=True))
        dg = _mdot(do, w_ref[...], NT).astype(ACT)
        dw_acc[...] += _mdot(g, do, TN)
        dcat = dg * sz
        dzq_ref[:, 0:D_CAT] = dg * cat * _dsilu(z)
        dy_ref[...] = dcat[:, 0:D]
        heads = range(XA_HEADS)
        dohs = [dcat[:, D + h * HEAD:D + (h + 1) * HEAD] for h in heads]
        dps = [_mdot(dohs[h], kvv[:, D_XA + h * HEAD:D_XA + (h + 1) * HEAD], NT) for h in heads]
        dss = [ps[h] * (dps[h] - jnp.sum(dps[h] * ps[h], axis=-1, keepdims=True)) for h in heads]
        dqs = [_mdot(dss[h], kvv[:, h * HEAD:(h + 1) * HEAD], NN) * (HEAD ** -0.5) for h in heads]
        dks = [_mdot(dss[h], q[:, h * HEAD:(h + 1) * HEAD], TN) * (HEAD ** -0.5) for h in heads]
        dvs = [_mdot(ps[h], dohs[h], TN) for h in heads]
        for h in heads:
            lo = h * HEAD
            dzq_ref[:, D_CAT + lo:D_CAT + lo + HEAD] = dqs[h].astype(ACT)
            dkv_ref[:, lo:lo + HEAD] += dks[h]
            dkv_ref[:, D_XA + lo:D_XA + lo + HEAD] += dvs[h]

        @pl.when(pl.program_id(0) == n_steps - 1)
        def _():
            dw_ref[...] = dw_acc[...].astype(XCH)

        if n_x:
            pl.when(pl.program_id(0) == n_steps - 1)(end)

    row = pl.BlockSpec((tm, D), lambda i: (i, 0))
    return _pcall(
        body, name=name, grid=(n_steps,),
        in_specs=[row, row, pl.BlockSpec((tm, D_CAT), lambda i: (i, 0)),
                  pl.BlockSpec((tm, D_XA), lambda i: (i, D_CAT // D_XA)), row,
                  _full((N_MEM, 2 * D_XA)), _full((D_CAT, D)), _full((1, D))] + [_HBM] * n_x,
        out_specs=[pl.BlockSpec((tm, ZQ), lambda i: (i, 0)), row, _full((D_CAT, D)), _full((1, D)),
                   _full((N_MEM, 2 * D_XA))] + [_HBM] * n_x,
        out_shape=[jax.ShapeDtypeStruct((t, ZQ), ACT), jax.ShapeDtypeStruct((t, D), ACT),
                   jax.ShapeDtypeStruct((D_CAT, D), XCH), jax.ShapeDtypeStruct((1, D), F32),
                   jax.ShapeDtypeStruct((N_MEM, 2 * D_XA), F32)]
        + [jax.ShapeDtypeStruct(e.shape, e.dtype) for e in exchange],
        scratch_shapes=[pltpu.VMEM((D_CAT, D), F32)] + (_sem_shapes(7 * n_x, n_x) if n_x else []),
        compiler_params=_cp(1),
    )(dxn, o, p, p, ymix, kv, wout, npost, *exchange)


def _gmlp_chunk(us, vs, lnws, lnbs, wss, bss):
    gv = [jax.nn.gelu(v) for v in vs]
    mean = sum(jnp.sum(v, axis=-1, keepdims=True) for v in gv) / D
    cen = [v - mean for v in gv]
    var = sum(jnp.sum(c * c, axis=-1, keepdims=True) for c in cen) / D
    rstd = lax.rsqrt(var + EPS)
    row = lax.broadcasted_iota(jnp.int32, (GM_CHUNK, GM_CHUNK), 0)
    col = lax.broadcasted_iota(jnp.int32, (GM_CHUNK, GM_CHUNK), 1)
    ys = []
    for g in range(GM_GROUPS):
        vn = cen[g] * rstd * lnws[g] + lnbs[g]
        sp = _nn(jnp.where(row >= col, wss[g], 0.0), vn) + bss[g]
        ys.append(jax.nn.gelu(us[g]) * sp)
    return ys


def _split_cols(v, n, width=HEAD):
    return [v[:, k * width:(k + 1) * width] for k in range(n)]


def _gmlp_operands(u_ref, v_ref, lnw_ref, lnb_ref, ws_ref, bst_ref, r0):
    us = _split_cols(u_ref[r0:r0 + GM_CHUNK, :].astype(F32), GM_GROUPS)
    vs = _split_cols(v_ref[r0:r0 + GM_CHUNK, :].astype(F32), GM_GROUPS)
    lnws = _split_cols(lnw_ref[...], GM_GROUPS)
    lnbs = _split_cols(lnb_ref[...], GM_GROUPS)
    wss = [ws_ref[g] for g in range(GM_GROUPS)]
    bst = bst_ref[...]
    bss = [jnp.sum(bst * _onehot_row(HEAD, g), axis=1, keepdims=True) for g in range(GM_GROUPS)]
    return us, vs, lnws, lnbs, wss, bss


def _gmlp_rows(t):
    return min(GM_CHUNKS_PER_STEP, t // GM_CHUNK) * GM_CHUNK


def _gmlp_specs(rows):
    return [pl.BlockSpec((rows, D), lambda i: (i, ZQ // D)), pl.BlockSpec((rows, D), lambda i: (i, ZQ // D + 1)),
            _full((1, D)), _full((1, D)), _full((GM_GROUPS, GM_CHUNK, GM_CHUNK)), _full((GM_CHUNK, HEAD))]


def _gmlp_fwd(p, lnw, lnb, ws, bst, name):
    t = p.shape[0]
    rows = _gmlp_rows(t)

    def body(u_ref, v_ref, lnw_ref, lnb_ref, ws_ref, bst_ref, y_ref):
        for r0 in range(0, rows, GM_CHUNK):
            ys = _gmlp_chunk(*_gmlp_operands(u_ref, v_ref, lnw_ref, lnb_ref, ws_ref, bst_ref, r0))
            for g in range(GM_GROUPS):
                y_ref[r0:r0 + GM_CHUNK, g * HEAD:(g + 1) * HEAD] = ys[g].astype(ACT)

    return _pcall(body, name=name, grid=(t // rows,), in_specs=_gmlp_specs(rows),
                  out_specs=pl.BlockSpec((rows, D), lambda i: (i, 0)),
                  out_shape=jax.ShapeDtypeStruct((t, D), ACT), compiler_params=_cp(1))(p, p, lnw, lnb, ws, bst)


def _gmlp_bwd(p, dy, lnw, lnb, ws, bst, name, exchange=(), bcast=()):
    t = p.shape[0]
    rows = _gmlp_rows(t)

    def body(u_ref, v_ref, lnw_ref, lnb_ref, ws_ref, bst_ref, dy_ref, duv_ref, dlnw_ref, dlnb_ref, dws_ref, dbst_ref):
        @pl.when(pl.program_id(0) == 0)
        def _():
            dlnw_ref[...] = jnp.zeros_like(dlnw_ref)
            dlnb_ref[...] = jnp.zeros_like(dlnb_ref)
            dws_ref[...] = jnp.zeros_like(dws_ref)
            dbst_ref[...] = jnp.zeros_like(dbst_ref)

        for r0 in range(0, rows, GM_CHUNK):
            ops = _gmlp_operands(u_ref, v_ref, lnw_ref, lnb_ref, ws_ref, bst_ref, r0)
            _, vjp = jax.vjp(_gmlp_chunk, *ops)
            dus, dvs, dlnws, dlnbs, dwss, dbss = vjp(_split_cols(dy_ref[r0:r0 + GM_CHUNK, :].astype(F32), GM_GROUPS))
            dbst = jnp.zeros((GM_CHUNK, HEAD), F32)
            for g in range(GM_GROUPS):
                lo = g * HEAD
                duv_ref[r0:r0 + GM_CHUNK, lo:lo + HEAD] = dus[g].astype(ACT)
                duv_ref[r0:r0 + GM_CHUNK, D + lo:D + lo + HEAD] = dvs[g].astype(ACT)
                dlnw_ref[:, lo:lo + HEAD] += dlnws[g]
                dlnb_ref[:, lo:lo + HEAD] += dlnbs[g]
                dws_ref[g] += dwss[g]
                dbst = dbst + dbss[g] * _onehot_row(HEAD, g)
            dbst_ref[...] += dbst

    call = dict(
        name=name, grid=(t // rows,),
        in_specs=_gmlp_specs(rows) + [pl.BlockSpec((rows, D), lambda i: (i, 0))],
        out_specs=[pl.BlockSpec((rows, 2 * D), lambda i: (i, 0)), _full((1, D)), _full((1, D)),
                   _full((GM_GROUPS, GM_CHUNK, GM_CHUNK)), _full((GM_CHUNK, HEAD))],
        out_shape=[jax.ShapeDtypeStruct((t, 2 * D), ACT), jax.ShapeDtypeStruct((1, D), F32),
                   jax.ShapeDtypeStruct((1, D), F32), jax.ShapeDtypeStruct((GM_GROUPS, GM_CHUNK, GM_CHUNK), F32),
                   jax.ShapeDtypeStruct((GM_CHUNK, HEAD), F32)],
        compiler_params=_cp(1))
    args = (p, p, lnw, lnb, ws, bst, dy)
    if not exchange and not bcast:
        return _pcall(body, **call)(*args), ()
    return _pcall_hosting(body, args, exchange, bcast, **call)


def _prev_halo(tm, col):
    return pl.BlockSpec((HALO, D), lambda i: (jnp.maximum(i * (tm // HALO) - 1, 0), col))


def _next_halo(tm, col, n_tiles):
    return pl.BlockSpec((HALO, D), lambda i: (jnp.minimum(i + 1, n_tiles - 1) * (tm // HALO), col))


def _taps_back(ext, w, width):
    acc = None
    for k in range(width):
        s = width - 1 - k
        term = w[k:k + 1, :] * (pltpu.roll(ext, s, 0) if s else ext)[HALO:, :]
        acc = term if acc is None else acc + term
    return acc


def _taps_fwd(ext, w, width, n):
    rows = ext.shape[0]
    acc = None
    for k in range(width):
        s = width - 1 - k
        term = w[k:k + 1, :] * (pltpu.roll(ext, rows - s, 0) if s else ext)[0:n, :]
        acc = term if acc is None else acc + term
    return acc


def _sconv_fwd(p, cw, name):
    t = p.shape[0]
    tm = min(256, t)

    def body(b_ref, c_ref, h_ref, cp_ref, hp_ref, w_ref, y_ref):
        first = pl.program_id(0) == 0
        prev = jnp.where(first, 0.0, cp_ref[...].astype(F32) * hp_ref[...].astype(F32))
        ext = jnp.concatenate([prev, c_ref[...].astype(F32) * h_ref[...].astype(F32)], axis=0)
        y_ref[...] = (b_ref[...].astype(F32) * _taps_back(ext, w_ref[...], 3)).astype(ACT)

    c0 = ZQ // D
    tile = [pl.BlockSpec((tm, D), lambda i, c=c: (i, c)) for c in (c0, c0 + 1, c0 + 2)]
    return _pcall(body, name=name, grid=(t // tm,),
                  in_specs=tile + [_prev_halo(tm, c0 + 1), _prev_halo(tm, c0 + 2), _full((HALO, D))],
                  out_specs=pl.BlockSpec((tm, D), lambda i: (i, 0)),
                  out_shape=jax.ShapeDtypeStruct((t, D), ACT), compiler_params=_cp(1))(p, p, p, p, p, cw)


def _sconv_bwd(p, dy, cw, name):
    t = p.shape[0]
    tm = min(256, t)
    n_tiles = t // tm

    def body(b_ref, c_ref, h_ref, cp_ref, hp_ref, bn_ref, dy_ref, dyn_ref, w_ref, d_ref, dw_ref):
        i = pl.program_id(0)

        @pl.when(i == 0)
        def _():
            dw_ref[...] = jnp.zeros_like(dw_ref)

        w = w_ref[...]
        bv, cv, hv = b_ref[...].astype(F32), c_ref[...].astype(F32), h_ref[...].astype(F32)
        dyv = dy_ref[...].astype(F32)
        prev = jnp.where(i == 0, 0.0, cp_ref[...].astype(F32) * hp_ref[...].astype(F32))
        ext = jnp.concatenate([prev, cv * hv], axis=0)
        conv = _taps_back(ext, w, 3)
        dconv = dyv * bv
        nxt = jnp.where(i == n_tiles - 1, 0.0, dyn_ref[...].astype(F32) * bn_ref[...].astype(F32))
        dc = _taps_fwd(jnp.concatenate([dconv, nxt], axis=0), w, 3, tm)
        d_ref[:, 0:D] = (dyv * conv).astype(ACT)
        d_ref[:, D:2 * D] = (dc * hv).astype(ACT)
        d_ref[:, 2 * D:3 * D] = (dc * cv).astype(ACT)
        for k in range(3):
            s = 2 - k
            shifted = (pltpu.roll(ext, s, 0) if s else ext)[HALO:, :]
            dw_ref[k:k + 1, :] += jnp.sum(dconv * shifted, axis=0, keepdims=True)

    c0 = ZQ // D
    tile = [pl.BlockSpec((tm, D), lambda i, c=c: (i, c)) for c in (c0, c0 + 1, c0 + 2)]
    return _pcall(
        body, name=name, grid=(n_tiles,),
        in_specs=tile + [_prev_halo(tm, c0 + 1), _prev_halo(tm, c0 + 2), _next_halo(tm, c0, n_tiles),
                         pl.BlockSpec((tm, D), lambda i: (i, 0)), _next_halo(tm, 0, n_tiles), _full((HALO, D))],
        out_specs=[pl.BlockSpec((tm, 3 * D), lambda i: (i, 0)), _full((HALO, D))],
        out_shape=[jax.ShapeDtypeStruct((t, 3 * D), ACT), jax.ShapeDtypeStruct((HALO, D), F32)],
        compiler_params=_cp(1),
    )(p, p, p, p, p, p, dy, dy, cw)


def _l2_heads(s, scale):
    outs, rs = [], []
    for hh in range(DN_HEADS):
        blk = s[:, hh * HEAD:(hh + 1) * HEAD]
        r = lax.rsqrt(jnp.sum(blk * blk, axis=-1, keepdims=True) + EPS)
        outs.append(blk * (r * scale))
        rs.append(r)
    return outs, rs


_QKV_SCALE = (HEAD ** -0.5, 1.0, None)


def _qkv_fwd(p, cw, name):
    t = p.shape[0]
    tm = min(256, t)

    def body(q_ref, k_ref, v_ref, qp_ref, kp_ref, vp_ref, w_ref, o_ref):
        first = pl.program_id(0) == 0
        for part, (ref, pref) in enumerate(((q_ref, qp_ref), (k_ref, kp_ref), (v_ref, vp_ref))):
            prev = jnp.where(first, 0.0, pref[...].astype(F32))
            ext = jnp.concatenate([prev, ref[...].astype(F32)], axis=0)
            s = _silu(_taps_back(ext, w_ref[:, part * D:(part + 1) * D], 4))
            if _QKV_SCALE[part] is None:
                o_ref[:, part * D:(part + 1) * D] = s.astype(ACT)
            else:
                outs, _ = _l2_heads(s, _QKV_SCALE[part])
                for hh in range(DN_HEADS):
                    o_ref[:, part * D + hh * HEAD:part * D + (hh + 1) * HEAD] = outs[hh].astype(ACT)

    c0 = ZQ // D
    tile = [pl.BlockSpec((tm, D), lambda i, c=c: (i, c)) for c in (c0, c0 + 1, c0 + 2)]
    halo = [_prev_halo(tm, c) for c in (c0, c0 + 1, c0 + 2)]
    return _pcall(body, name=name, grid=(t // tm,), in_specs=tile + halo + [_full((HALO, 3 * D))],
                  out_specs=pl.BlockSpec((tm, 3 * D), lambda i: (i, 0)),
                  out_shape=jax.ShapeDtypeStruct((t, 3 * D), ACT), compiler_params=_cp(1))(p, p, p, p, p, p, cw)


def _qkv_bwd(p, dqkv, cw, name):
    t = p.shape[0]
    tm = min(256, t)
    n_tiles = t // tm

    def body(*refs):
        tiles, prevs, nexts = refs[0:3], refs[3:6], refs[6:9]
        d_tiles, d_nexts = refs[9:12], refs[12:15]
        w_ref, o_ref, dw_ref = refs[15:]
        i = pl.program_id(0)

        @pl.when(i == 0)
        def _():
            dw_ref[...] = jnp.zeros_like(dw_ref)

        for part in range(3):
            w = w_ref[:, part * D:(part + 1) * D]
            prev = jnp.where(i == 0, 0.0, prevs[part][...].astype(F32))
            ext = jnp.concatenate([prev, tiles[part][...].astype(F32), nexts[part][...].astype(F32)], axis=0)
            xc = _taps_back(ext, w, 4)
            dout = jnp.concatenate([d_tiles[part][...].astype(F32), d_nexts[part][...].astype(F32)], axis=0)
            s = _silu(xc)
            if _QKV_SCALE[part] is None:
                ds = dout
            else:
                scale = _QKV_SCALE[part]
                pieces = []
                for hh in range(DN_HEADS):
                    blk = s[:, hh * HEAD:(hh + 1) * HEAD]
                    dblk = dout[:, hh * HEAD:(hh + 1) * HEAD]
                    r = lax.rsqrt(jnp.sum(blk * blk, axis=-1, keepdims=True) + EPS)
                    pieces.append(scale * r * (dblk - blk * (r * r) * jnp.sum(dblk * blk, axis=-1, keepdims=True)))
                ds = jnp.concatenate(pieces, axis=1)
            dxc = ds * _dsilu(xc)
            row = lax.broadcasted_iota(jnp.int32, (tm + HALO, 1), 0)
            dxc = jnp.where(jnp.logical_and(i == n_tiles - 1, row >= tm), 0.0, dxc)
            o_ref[:, part * D:(part + 1) * D] = _taps_fwd(dxc, w, 4, tm).astype(ACT)
            for k in range(4):
                sh = 3 - k
                shifted = (pltpu.roll(ext, sh, 0) if sh else ext)[HALO:HALO + tm, :]
                dw_ref[k:k + 1, part * D:(part + 1) * D] += jnp.sum(dxc[0:tm, :] * shifted, axis=0, keepdims=True)

    c0 = ZQ // D
    cols = (c0, c0 + 1, c0 + 2)
    tile = [pl.BlockSpec((tm, D), lambda i, c=c: (i, c)) for c in cols]
    dtile = [pl.BlockSpec((tm, D), lambda i, c=c: (i, c)) for c in range(3)]
    in_specs = (tile + [_prev_halo(tm, c) for c in cols] + [_next_halo(tm, c, n_tiles) for c in cols]
                + dtile + [_next_halo(tm, c, n_tiles) for c in range(3)] + [_full((HALO, 3 * D))])
    return _pcall(
        body, name=name, grid=(n_tiles,), in_specs=in_specs,
        out_specs=[pl.BlockSpec((tm, 3 * D), lambda i: (i, 0)), _full((HALO, 3 * D))],
        out_shape=[jax.ShapeDtypeStruct((t, 3 * D), ACT), jax.ShapeDtypeStruct((HALO, 3 * D), F32)],
        compiler_params=_cp(1),
    )(*([p] * 9), *([dqkv] * 6), cw)


def _tri_masks(n):
    row = lax.broadcasted_iota(jnp.int32, (n, n), 0)
    col = lax.broadcasted_iota(jnp.int32, (n, n), 1)
    return row, col


@jax.custom_vjp
def _unit_lower_inverses(mats):
    n = DN_CHUNK
    row, col = _tri_masks(n)
    eye = (row == col).astype(F32)
    same16 = (row // 16) == (col // 16)
    same32 = (row // 32) == (col // 32)
    pw = [jnp.where(same16, a, 0.0) for a in mats]
    x = [eye - p for p in pw]
    for _ in range(3):
        pw = [_hnn(p, p) for p in pw]
        x = [_hnn(xi, eye + p) for xi, p in zip(x, pw)]
    for keep in (jnp.logical_and(same32, jnp.logical_not(same16)), jnp.logical_not(same32)):
        inner = [_hnn(jnp.where(keep, a, 0.0), xi) for a, xi in zip(mats, x)]
        x = [xi - _hnn(xi, y) for xi, y in zip(x, inner)]
    return tuple(x)


def _uli_fwd(mats):
    t = _unit_lower_inverses(mats)
    return t, t


def _uli_bwd(ts, gs):
    inner = [_nt(g, t) for g, t in zip(gs, ts)]
    return (tuple(-_tn(t, y) for t, y in zip(ts, inner)),)


_unit_lower_inverses.defvjp(_uli_fwd, _uli_bwd)


@jax.custom_vjp
def _known_inverses(mats, ts):
    return ts


_known_inverses.defvjp(lambda mats, ts: (ts, ts),
                       lambda ts, gs: (_uli_bwd(ts, gs)[0], tuple(jnp.zeros_like(t) for t in ts)))


def _pick_col(m, k):
    return jnp.sum(m * _onehot_row(m.shape[1], k), axis=1, keepdims=True)


def _pick_row(m, k):
    hot = (lax.broadcasted_iota(jnp.int32, (m.shape[0], 1), 0) == k).astype(F32)
    return jnp.sum(m * hot, axis=0, keepdims=True)


def _delta_chunk(states, qs, ks, vs, ab, alog, dtb, onw, known_inverses=None):
    n = DN_CHUNK
    heads = range(DN_HEADS)
    row, col = _tri_masks(n)
    incl = row >= col
    lane = lax.broadcasted_iota(jnp.int32, (1, HEAD), 1)
    g_all = jnp.where(lane < DN_HEADS, -jnp.exp(alog) * jax.nn.softplus(ab + dtb), 0.0)
    c_cols = _hnn(incl.astype(F32), g_all)
    c_rows = _htn(g_all, (row <= col).astype(F32))
    g_tot = jnp.sum(g_all, axis=0, keepdims=True)
    beta_all = jax.nn.sigmoid(ab)
    ccol = [_pick_col(c_cols, h) for h in heads]
    crow = [_pick_row(c_rows, h) for h in heads]
    gl = [_pick_col(g_tot, h) for h in heads]
    beta = [_pick_col(beta_all, DN_HEADS + h) for h in heads]
    decay = [jnp.exp(jnp.where(incl, ccol[h] - crow[h], -1e30)) for h in heads]
    eg = [jnp.exp(ccol[h]) for h in heads]
    kb = [ks[h] * beta[h] for h in heads]
    amat = [jnp.where(row > col, _nt(kb[h], ks[h]) * decay[h], 0.0) for h in heads]
    if known_inverses is None:
        tmat = _unit_lower_inverses(tuple(amat))
    else:
        tmat = _known_inverses(tuple(amat), tuple(known_inverses))
    u = [_nn(tmat[h], vs[h] * beta[h]) for h in heads]
    w = [_nn(tmat[h], kb[h] * eg[h]) for h in heads]
    qk = [_nt(qs[h], ks[h]) * decay[h] for h in heads]
    v_new = [u[h] - _nn(w[h], states[h]) for h in heads]
    o = [_nn(qs[h] * eg[h], states[h]) + _nn(qk[h], v_new[h]) for h in heads]
    new_states = [states[h] * jnp.exp(gl[h]) + _tn(ks[h] * jnp.exp(gl[h] - ccol[h]), v_new[h]) for h in heads]
    ys = [o[h] * lax.rsqrt(jnp.mean(o[h] * o[h], axis=-1, keepdims=True) + EPS) * onw for h in heads]
    return (ys, new_states), tmat


def _head_cols(ref, r0):
    return [ref[r0:r0 + DN_CHUNK, h * HEAD:(h + 1) * HEAD].astype(F32) for h in range(DN_HEADS)]


def _delta_rows(t):
    return min(DN_CHUNKS_PER_STEP, t // DN_CHUNK) * DN_CHUNK


def _delta_fwd(qkv, ab, alog, dtb, onw, name):
    t = qkv.shape[0]
    rows = _delta_rows(t)
    per_step = rows // DN_CHUNK

    def body(q_ref, k_ref, v_ref, ab_ref, alog_ref, dtb_ref, onw_ref, y_ref, keep_ref, inv_ref, state):
        @pl.when(pl.program_id(0) == 0)
        def _():
            state[...] = jnp.zeros_like(state)

        s = [state[hh] for hh in range(DN_HEADS)]
        for c in range(per_step):
            r0 = c * DN_CHUNK
            (ys, s1), tmat = _delta_chunk(s, _head_cols(q_ref, r0), _head_cols(k_ref, r0), _head_cols(v_ref, r0),
                                          ab_ref[r0:r0 + DN_CHUNK, :], alog_ref[...], dtb_ref[...], onw_ref[...])
            for hh in range(DN_HEADS):
                keep_ref[c, hh] = s[hh]
                inv_ref[c, hh] = tmat[hh]
                y_ref[r0:r0 + DN_CHUNK, hh * HEAD:(hh + 1) * HEAD] = ys[hh].astype(ACT)
            s = s1
        for hh in range(DN_HEADS):
            state[hh] = s[hh]

    block = [pl.BlockSpec((rows, D), lambda i, c=c: (i, c)) for c in range(3)]
    nc = t // DN_CHUNK
    return _pcall(
        body, name=name, grid=(t // rows,),
        in_specs=block + [pl.BlockSpec((rows, HEAD), lambda i: (i, 0)), _full((1, HEAD)), _full((1, HEAD)),
                          _full((1, HEAD))],
        out_specs=[pl.BlockSpec((rows, D), lambda i: (i, 0)),
                   pl.BlockSpec((per_step, DN_HEADS, HEAD, HEAD), lambda i: (i, 0, 0, 0)),
                   pl.BlockSpec((per_step, DN_HEADS, DN_CHUNK, DN_CHUNK), lambda i: (i, 0, 0, 0))],
        out_shape=[jax.ShapeDtypeStruct((t, D), ACT), jax.ShapeDtypeStruct((nc, DN_HEADS, HEAD, HEAD), F32),
                   jax.ShapeDtypeStruct((nc, DN_HEADS, DN_CHUNK, DN_CHUNK), F32)],
        scratch_shapes=[pltpu.VMEM((DN_HEADS, HEAD, HEAD), F32)], compiler_params=_cp(1),
    )(qkv, qkv, qkv, ab, alog, dtb, onw)


def _delta_bwd(qkv, ab, alog, dtb, onw, keep, inv, dy, name):
    t = qkv.shape[0]
    rows = _delta_rows(t)
    per_step, n_steps = rows // DN_CHUNK, t // rows

    def body(q_ref, k_ref, v_ref, ab_ref, alog_ref, dtb_ref, onw_ref, keep_ref, inv_ref, dy_ref,
             dqkv_ref, dab_ref, dalog_ref, ddtb_ref, donw_ref, dstate):
        @pl.when(pl.program_id(0) == 0)
        def _():
            dstate[...] = jnp.zeros_like(dstate)
            dalog_ref[...] = jnp.zeros_like(dalog_ref)
            ddtb_ref[...] = jnp.zeros_like(ddtb_ref)
            donw_ref[...] = jnp.zeros_like(donw_ref)

        ds = [dstate[hh] for hh in range(DN_HEADS)]
        for c in reversed(range(per_step)):
            r0 = c * DN_CHUNK
            s0 = [keep_ref[c, hh] for hh in range(DN_HEADS)]
            known = [inv_ref[c, hh] for hh in range(DN_HEADS)]
            _, vjp, _ = jax.vjp(functools.partial(_delta_chunk, known_inverses=known), s0, _head_cols(q_ref, r0),
                                _head_cols(k_ref, r0), _head_cols(v_ref, r0), ab_ref[r0:r0 + DN_CHUNK, :], alog_ref[...],
                                dtb_ref[...], onw_ref[...], has_aux=True)
            ds, dq, dk, dv, dab, dal, ddt, don = vjp((_head_cols(dy_ref, r0), ds))
            for hh in range(DN_HEADS):
                lo = hh * HEAD
                dqkv_ref[r0:r0 + DN_CHUNK, lo:lo + HEAD] = dq[hh].astype(ACT)
                dqkv_ref[r0:r0 + DN_CHUNK, D + lo:D + lo + HEAD] = dk[hh].astype(ACT)
                dqkv_ref[r0:r0 + DN_CHUNK, 2 * D + lo:2 * D + lo + HEAD] = dv[hh].astype(ACT)
            dab_ref[r0:r0 + DN_CHUNK, :] = dab
            dalog_ref[...] += dal
            ddtb_ref[...] += ddt
            donw_ref[...] += don
        for hh in range(DN_HEADS):
            dstate[hh] = ds[hh]

    rev = lambda i: n_steps - 1 - i
    block = [pl.BlockSpec((rows, D), lambda i, c=c: (rev(i), c)) for c in range(3)]
    small = jax.ShapeDtypeStruct((1, HEAD), F32)
    return _pcall(
        body, name=name, grid=(n_steps,),
        in_specs=block + [pl.BlockSpec((rows, HEAD), lambda i: (rev(i), 0)), _full((1, HEAD)), _full((1, HEAD)),
                          _full((1, HEAD)), pl.BlockSpec((per_step, DN_HEADS, HEAD, HEAD), lambda i: (rev(i), 0, 0, 0)),
                          pl.BlockSpec((per_step, DN_HEADS, DN_CHUNK, DN_CHUNK), lambda i: (rev(i), 0, 0, 0)),
                          pl.BlockSpec((rows, D), lambda i: (rev(i), 0))],
        out_specs=[pl.BlockSpec((rows, 3 * D), lambda i: (rev(i), 0)),
                   pl.BlockSpec((rows, HEAD), lambda i: (rev(i), 0)), _full((1, HEAD)), _full((1, HEAD)),
                   _full((1, HEAD))],
        out_shape=[jax.ShapeDtypeStruct((t, 3 * D), ACT), jax.ShapeDtypeStruct((t, HEAD), F32), small, small, small],
        scratch_shapes=[pltpu.VMEM((DN_HEADS, HEAD, HEAD), F32)], compiler_params=_cp(1),
    )(qkv, qkv, qkv, ab, alog, dtb, onw, keep, inv, dy)


def _loss_head(y, target):
    t = y.shape[0]
    tm = min(512, t)

    def body(y_ref, t_ref, l_ref, dy_ref):
        @pl.when(pl.program_id(0) == 0)
        def _():
            l_ref[...] = jnp.zeros_like(l_ref)

        diff = y_ref[...] - t_ref[...]
        dy_ref[...] = diff * (1.0 / D)
        l_ref[...] += 0.5 * jnp.sum(jnp.sum(diff * diff, axis=-1, keepdims=True) * (1.0 / D), axis=0, keepdims=True)

    row = pl.BlockSpec((tm, D), lambda i: (i, 0))
    return _pcall(body, name="loss_head", grid=(t // tm,), in_specs=[row, row], out_specs=[_full((8, 128)), row],
                  out_shape=[jax.ShapeDtypeStruct((8, 128), F32), jax.ShapeDtypeStruct((t, D), F32)],
                  compiler_params=_cp(1))(y, target)


_SMALL_SHARDED = (("a_ln_w", (2, 128), 1), ("a_ln_b", (2, 128), 1), ("b_conv_w", (1, 3, 128), 2),
                  ("c_conv_w", (1, 4, 384), 2))
_REPLICATED = (("mem_norm_w", (1024,)), ("norm_pre", (4, 1024)), ("norm_post", (4, 1024)),
               ("a_b_s", (2, 8, 128)), ("c_a_log", (1, 8)), ("c_dt_bias", (1, 8)), ("c_o_norm_w", (1, 128)))


def _layer_shards(given, prefix):
    w_out = given[prefix + "w_out"]
    w_ins = [given[prefix + "a_w_in"][0], given[prefix + "b_w_in"], given[prefix + "c_w_in"], given[prefix + "a_w_in"][1]]
    return [[w_out[i], w_ins[i]] for i in range(4)]


def _as_matmul_operand(w):
    return w.reshape(w.shape[-2:]).astype(MXU)


def _as_rows(shape):
    return (math.prod(shape[:-1]) if len(shape) > 1 else 1, shape[-1])


def _row_table(shapes):
    table, at = [], 0
    for shape in shapes:
        rows, cols = _as_rows(shape)
        table.append((at, rows, cols))
        at += rows
    return table


def _pack_rows(arrays, shapes, rows, lead=()):
    n_lead = len(lead)
    parts = [jnp.pad(a.reshape(lead + _as_rows(s)), [(0, 0)] * (n_lead + 1) + [(0, D - s[-1])])
             for a, s in zip(arrays, shapes)]
    block = jnp.concatenate(parts, axis=n_lead)
    return jnp.pad(block, [(0, 0)] * n_lead + [(0, rows - block.shape[n_lead]), (0, 0)])


def _unpack_rows(packed, shapes, lead=()):
    n_lead = len(lead)
    return [packed[(slice(None),) * n_lead + (slice(r0, r0 + nr), slice(0, nc))].reshape(lead + s)
            for (r0, nr, nc), s in zip(_row_table(shapes), shapes)]


def _join_shards(blocks, axis):
    moved = jnp.moveaxis(blocks, 0, axis)
    shape = moved.shape
    return moved.reshape(shape[:axis] + (shape[axis] * shape[axis + 1],) + shape[axis + 2:])


def _split_shards(full, axis):
    shape = full.shape
    split = full.reshape(shape[:axis] + (N_DEV, shape[axis] // N_DEV) + shape[axis + 1:])
    return jnp.moveaxis(split, axis, 0)


_A_COLS = ((2560, 1536), (2048, 512), (0, 2048))
_BC_COLS = ((3584, 1536), (3072, 512), (0, 3072))
_C_COLS = ((3600, 1536), (3088, 512), (0, 3072))


def _reorder_cols(w, cols):
    return jnp.concatenate([w[:, s:s + n] for s, n in cols], axis=1)


def _restore_cols(pieces_in_my_order, cols, extra=()):
    placed = sorted(list(zip([s for s, _ in cols], pieces_in_my_order)) + list(extra), key=lambda sp: sp[0])
    return jnp.concatenate([piece for _, piece in placed], axis=1)


def kernel(x, mem, mem_norm_w, w_mem_kv, norm_pre, norm_post, w_out, a_w_in, a_ln_w, a_ln_b, a_w_s, a_b_s, b_w_in, b_conv_w, c_w_in, c_conv_w, c_a_log, c_dt_bias, c_o_norm_w, loss_target, m_mem_norm_w, m_w_mem_kv, m_norm_pre, m_norm_post, m_w_out, m_a_w_in, m_a_ln_w, m_a_ln_b, m_a_w_s, m_a_b_s, m_b_w_in, m_b_conv_w, m_c_w_in, m_c_conv_w, m_c_a_log, m_c_dt_bias, m_c_o_norm_w, v_mem_norm_w, v_w_mem_kv, v_norm_pre, v_norm_post, v_w_out, v_a_w_in, v_a_ln_w, v_a_ln_b, v_a_w_s, v_a_b_s, v_b_w_in, v_b_conv_w, v_c_w_in, v_c_conv_w, v_c_a_log, v_c_dt_bias, v_c_o_norm_w):
    given = dict(locals())
    x0 = x[0]
    mem0 = mem[0]
    target = loss_target[0]

    w_sh, m_sh, v_sh = (_layer_shards(given, pre) for pre in ("", "m_", "v_"))
    small_names = [n for n, _, _ in _SMALL_SHARDED]
    small_shapes = [s for _, s, _ in _SMALL_SHARDED]
    repl_names = [n for n, _ in _REPLICATED]
    repl_shapes = [s for _, s in _REPLICATED]
    small = _pack_rows([given[n] for n in small_names], small_shapes, R_SMALL)
    g_in0, = _all_gather([_as_matmul_operand(w_sh[0][1])], "gather_weights")
    w_in = [_reorder_cols(_join_shards(g_in0, 1), _A_COLS)]
    wouts, w_cab = [], None
    ws = [a_w_s[j] for j in range(2)]
    bst = [jnp.pad(a_b_s[j].T, ((0, 0), (0, HEAD - GM_GROUPS))) for j in range(2)]
    alog = jnp.pad(c_a_log, ((0, 0), (0, HEAD - DN_HEADS)))
    dtb = jnp.pad(c_dt_bias, ((0, 0), (0, HEAD - DN_HEADS)))
    onw = c_o_norm_w
    mw = mem_norm_w[None, :]

    xs, saved = [x0], []
    for i in range(4):
        kind = i % 3
        npre, npost = norm_pre[i][None, :], norm_post[i][None, :]
        ahead = [_as_matmul_operand(w) for w in w_sh[i + 1][:1 if i == 0 else 2]] if i < 3 else []
        if i == 0:
            ahead = [w_mem_kv.astype(MXU), _as_matmul_operand(w_sh[0][0]), small] + ahead
        res = _proj_fwd(xs[i], npre, w_in[i], w_cab if kind == 2 else None, f"proj_fwd_{i}", gather=ahead)
        if i == 0:
            g_wkv, g_wout0, g_small = res[2:5]
            small_full = _unpack_rows(g_small, small_shapes, lead=(N_DEV,))
            full = {n: _join_shards(blocks, ax) for (n, _, ax), blocks in zip(_SMALL_SHARDED, small_full)}
            wkv = g_wkv.reshape(D, 2 * D_XA)
            wouts.append(g_wout0.reshape(D_CAT, D))
            lnw = [full["a_ln_w"][j][None, :] for j in range(2)]
            lnb = [full["a_ln_b"][j][None, :] for j in range(2)]
            cw_b = jnp.pad(full["b_conv_w"][0], ((0, HALO - 3), (0, 0)))
            cw_c = jnp.pad(full["c_conv_w"][0], ((0, HALO - 4), (0, 0)))
            kv = _memkv_fwd(mem0, mw, wkv).astype(MXU)
        if ahead:
            wouts.append(res[-1 if i == 0 else -2].reshape(D_CAT, D))
            g_in = res[-1] if i > 0 else None
        if kind == 2:
            p, h, ab = res[:3]
            qkv = _qkv_fwd(p, cw_c, f"qkv_fwd_{i}")
            ymix, keep, inv = _delta_fwd(qkv, ab, alog, dtb, onw, f"delta_fwd_{i}")
            extra = (qkv, ab, keep, inv)
        else:
            p, h = res[:2]
            if kind == 0:
                ymix = _gmlp_fwd(p, lnw[i // 3], lnb[i // 3], ws[i // 3], bst[i // 3], f"gmlp_fwd_{i}")
            else:
                ymix = _sconv_fwd(p, cw_b, f"sconv_fwd_{i}")
            extra = ()
        res = _tail_fwd(p, ymix, xs[i], kv, wouts[i], npost, f"tail_fwd_{i}",
                        gather=[_as_matmul_operand(w_sh[1][1])] if i == 0 else [])
        xn, o = res[:2]
        if i == 0:
            g_in = res[2]
        if i + 1 == 2:
            c_full = jnp.concatenate([g_in[d] for d in range(N_DEV)], axis=1)
            w_in.append(_reorder_cols(c_full, _C_COLS))
            w_cab = jnp.pad(c_full[:, 3072:3088], ((0, 0), (0, HEAD - 16)))
        elif i < 3:
            w_in.append(_reorder_cols(_join_shards(g_in, 1), _BC_COLS if i + 1 == 1 else _A_COLS))
        xs.append(xn)
        saved.append((p, h, ymix, o, extra))

    loss_tile, dx = _loss_head(xs[4], target)
    loss = lax.psum(loss_tile[0, 0], ("x", "y", "c"))

    g = {}
    d_npre, d_npost = [None] * 4, [None] * 4
    d_ws, d_bs, d_lnw, d_lnb = [None] * 2, [None] * 2, [None] * 2, [None] * 2
    dkv = None
    pend_win, pend_wout = None, None
    landed_win, landed_wout = [None] * 4, [None] * 4
    for i in reversed(range(4)):
        kind = i % 3
        p, h, ymix, o, extra = saved[i]
        npre, npost = norm_pre[i][None, :], norm_post[i][None, :]
        res = _tail_bwd(dx, o, p, ymix, kv, wouts[i], npost, f"tail_bwd_{i}", exchange=[pend_win] if i < 3 else [])
        dzq, dymix, d_wout, d_npost[i], dkv_i = res[:5]
        if i < 3:
            landed_win[i + 1] = res[5]
        d_wout = d_wout.reshape(N_DEV, D_CAT // N_DEV, D)
        dkv = dkv_i if dkv is None else dkv + dkv_i
        w_zq, w_mix = w_in[i][:, :ZQ], w_in[i][:, ZQ:]
        dw_zq = _matmul_tn(h, dzq, f"dw_zq_{i}")
        if kind == 0:
            j = i // 3
            early_x, early_b = [], []
            if i == 0:
                d_wkv, d_mw = _memkv_bwd(mem0, mw, wkv, dkv)
                early_x = [d_wkv.reshape(N_DEV, 128, D).astype(XCH), d_wout, pend_wout]
                early_b = [d_ws[1].reshape(GM_CHUNK, D).astype(XCH)]
            (dmix, d_lnw[j], d_lnb[j], d_ws[j], dbst), early_landed = _gmlp_bwd(
                p, dymix, lnw[j], lnb[j], ws[j], bst[j], f"gmlp_bwd_{i}", exchange=early_x, bcast=early_b)
            d_bs[j] = dbst[:, :GM_GROUPS].T
            if i == 0:
                g["mem_norm_w"] = d_mw[0]
                g["norm_pre"] = jnp.concatenate([jnp.zeros((1, D), F32)] + d_npre[1:], axis=0)
                g["norm_post"] = jnp.concatenate(d_npost, axis=0)
                g["a_ln_w"] = jnp.concatenate(d_lnw, axis=0)
                g["a_ln_b"] = jnp.concatenate(d_lnb, axis=0)
                g["a_b_s"] = jnp.stack(d_bs)
                e_small = _pack_rows([_split_shards(g[n], ax) for n, _, ax in _SMALL_SHARDED], small_shapes, R_SMALL,
                                     lead=(N_DEV,))
                r_pack = _pack_rows([g[n] for n in repl_names], repl_shapes, R_REPL)
                dw_mix, (l_small, ws0_all, r_all) = _matmul_tn(
                    h, dmix, f"dw_mix_{i}", exchange=[e_small], bcast=[d_ws[0].reshape(GM_CHUNK, D).astype(XCH), r_pack])
            else:
                dw_mix = _matmul_tn(h, dmix, f"dw_mix_{i}")
            d_win = _restore_cols([dw_zq[:, :D_CAT], dw_zq[:, D_CAT:], dw_mix], _A_COLS)
            dps, wparts = [dzq, dmix], [w_zq, w_mix]
        elif kind == 1:
            dmix, dcw = _sconv_bwd(p, dymix, cw_b, f"sconv_bwd_{i}")
            g["b_conv_w"] = dcw[None, :3]
            dw_mix = _matmul_tn(h, dmix, f"dw_mix_{i}")
            d_win = _restore_cols([dw_zq[:, :D_CAT], dw_zq[:, D_CAT:], dw_mix], _BC_COLS)
            dps, wparts = [dzq, dmix], [w_zq, w_mix]
        else:
            qkv, ab, keep, inv = extra
            dqkv, dab, dalog, ddtb, donw = _delta_bwd(qkv, ab, alog, dtb, onw, keep, inv, dymix, f"delta_bwd_{i}")
            dmix, dcw = _qkv_bwd(p, dqkv, cw_c, f"qkv_bwd_{i}")
            g["c_conv_w"] = dcw[None, :4]
            g["c_a_log"], g["c_dt_bias"], g["c_o_norm_w"] = dalog[:, :DN_HEADS], ddtb[:, :DN_HEADS], donw
            dw_mix = _matmul_tn(h, dmix, f"dw_mix_{i}")
            dw_ab = _matmul_tn(h, dab, f"dw_ab_{i}")
            d_win = _restore_cols([dw_zq[:, :D_CAT], dw_zq[:, D_CAT:], dw_mix], _C_COLS, extra=[(3072, dw_ab[:, :16])])
            dps, wparts = [dzq, dmix, dab], [w_zq, w_mix, w_cab]
        width = d_win.shape[1] // N_DEV
        pend_win = jnp.stack([d_win[:, d * width:(d + 1) * width] for d in range(N_DEV)]).astype(XCH)
        if i > 0:
            res = _proj_bwd_x(dps, wparts, xs[i], dx, npre, f"proj_bwd_x_{i}", exchange=[pend_wout] if i < 3 else [])
            dx, d_npre[i] = res[:2]
            if i < 3:
                landed_wout[i + 1] = res[2]
            pend_wout = d_wout

    dx, d_npre0, l_in = _proj_bwd_x(dps, wparts, xs[0], dx, norm_pre[0][None, :], "proj_bwd_x_0", exchange=[pend_win])
    l_wkv, landed_wout[0], landed_wout[1], ws3_all = early_landed
    landed_win[0] = l_in
    npre0_all, = _all_gather([jnp.pad(d_npre0, ((0, HALO - 1), (0, 0)))], "gather_norm_pre0")
    r_all = r_all.at[:, 1, :].set(npre0_all[:, 0, :])

    res = [[_reduce_adamw(parts, w_sh[i][a], m_sh[i][a], v_sh[i][a], f"adamw_{i}_{a}")
            for a, parts in enumerate((landed_wout[i], landed_win[i]))] for i in range(4)]
    res_wkv = _reduce_adamw(l_wkv, w_mem_kv, m_w_mem_kv, v_w_mem_kv, "adamw_w_mem_kv")
    res_small = _reduce_adamw_vectors(
        l_small, small, *(_pack_rows([given[pre + n] for n in small_names], small_shapes, R_SMALL) for pre in ("m_", "v_")),
        small_shapes, "adamw_small")
    res_ws = [_reduce_adamw(parts, *(given[pre + "a_w_s"][j].reshape(GM_CHUNK, D) for pre in ("", "m_", "v_")),
                            f"adamw_a_w_s_{j}") for j, parts in enumerate((ws0_all, ws3_all))]
    res_repl = _reduce_adamw_vectors(
        r_all, *(_pack_rows([given[pre + n] for n in repl_names], repl_shapes, R_REPL) for pre in ("", "m_", "v_")),
        repl_shapes, "adamw_replicated")

    order = ["mem_norm_w", "w_mem_kv", "norm_pre", "norm_post", "w_out", "a_w_in", "a_ln_w", "a_ln_b", "a_w_s", "a_b_s",
             "b_w_in", "b_conv_w", "c_w_in", "c_conv_w", "c_a_log", "c_dt_bias", "c_o_norm_w"]
    outs = [loss, dx[None]]
    for kind in range(4):
        got = dict(zip(repl_names, res_repl[kind]))
        got["a_w_s"] = jnp.stack([r[kind] for r in res_ws]).reshape(a_w_s.shape)
        got.update(zip(small_names, res_small[kind]))
        got["w_mem_kv"] = res_wkv[kind]
        got["w_out"] = jnp.stack([res[i][0][kind] for i in range(4)])
        got["a_w_in"] = jnp.stack([res[0][1][kind], res[3][1][kind]])
        got["b_w_in"] = res[1][1][kind]
        got["c_w_in"] = res[2][1][kind]
        outs += [got[n] for n in order]
    return tuple(outs)
```

```python
import functools
import math

import jax
import jax.numpy as jnp
from jax import lax
from jax.experimental import pallas as pl
from jax.experimental.pallas import tpu as pltpu

F32 = jnp.float32
MXU = jnp.bfloat16
ACT = jnp.bfloat16

D = 1024
D_XA = 512
D_CAT = 1536
N_MEM = 256
XA_HEADS = 4
HEAD = 128
ZQ = D_CAT + D_XA
EPS = 1e-6
GM_CHUNK = 128
GM_CHUNKS_PER_STEP = 8
GM_GROUPS = 8
DN_HEADS = 8
DN_CHUNK = 64
DN_CHUNKS_PER_STEP = 8
N_DEV = 8
HALO = 8
VMEM_LIMIT = 56 * 1024 * 1024
XCH = jnp.bfloat16
R_REPL = 32
R_SMALL = 16

ADAM_LR = 0.001
ADAM_B1 = 0.9
ADAM_B2 = 0.999
ADAM_EPS = 1e-08
ADAM_WD = 0.01
ADAM_STEP = 10

NN = ((1,), (0,))
NT = ((1,), (1,))
TN = ((0,), (0,))
MESH = pl.DeviceIdType.MESH


def _pcall(body, **kw):
    return pl.pallas_call(body, **kw)


def _cp(n_axes):
    return pltpu.CompilerParams(dimension_semantics=("arbitrary",) * n_axes, vmem_limit_bytes=VMEM_LIMIT)


def _dot(a, b, dims, prec=None):
    return lax.dot_general(a, b, (dims, ((), ())), preferred_element_type=F32, precision=prec)


def _mdot(a, b, dims):
    return _dot(a.astype(MXU), b.astype(MXU), dims)


def _make_mms(raw):
    @jax.custom_vjp
    def nn(a, b):
        return raw(a, b, NN)

    @jax.custom_vjp
    def nt(a, b):
        return raw(a, b, NT)

    @jax.custom_vjp
    def tn(a, b):
        return raw(a, b, TN)

    nn.defvjp(lambda a, b: (nn(a, b), (a, b)), lambda r, g: (nt(g, r[1]), tn(r[0], g)))
    nt.defvjp(lambda a, b: (nt(a, b), (a, b)), lambda r, g: (nn(g, r[1]), tn(g, r[0])))
    tn.defvjp(lambda a, b: (tn(a, b), (a, b)), lambda r, g: (nt(r[1], g), nn(r[0], g)))
    return nn, nt, tn


_nn, _nt, _tn = _make_mms(_mdot)


def _split_dot(a, b, dims):
    ah = a.astype(jnp.bfloat16)
    bh = b.astype(jnp.bfloat16)
    al = (a - ah.astype(F32)).astype(jnp.bfloat16)
    bl = (b - bh.astype(F32)).astype(jnp.bfloat16)
    return _dot(ah, bh, dims) + (_dot(ah, bl, dims) + _dot(al, bh, dims))


_hnn, _hnt, _htn = _make_mms(_split_dot)


def _full(shape):
    return pl.BlockSpec(shape, lambda *_: (0,) * len(shape))


def _silu(z):
    return z * jax.nn.sigmoid(z)


def _dsilu(z):
    s = jax.nn.sigmoid(z)
    return s * (1.0 + z * (1.0 - s))


def _onehot_row(n, k):
    return (lax.broadcasted_iota(jnp.int32, (1, n), 1) == k).astype(F32)


_HBM = pl.BlockSpec(memory_space=pl.ANY)


def _sem_shapes(n_remote, n_local):
    return [pltpu.SemaphoreType.DMA((n_remote,)), pltpu.SemaphoreType.DMA((n_remote,)),
            pltpu.SemaphoreType.DMA((n_local,))]


def _gather_parts(x_refs, out_refs, send_sems, recv_sems, local_sems):
    n = len(x_refs)
    x, y, cc = lax.axis_index("x"), lax.axis_index("y"), lax.axis_index("c")
    me, sibling = (x, y, cc), (x, y, 1 - cc)
    chips = [(1 - x, y), (x, 1 - y), (1 - x, 1 - y)]

    def slot(a, px, py, pc):
        return out_refs[a].at[4 * px + 2 * py + pc]

    def copy(k, a, block, to, src=None):
        return pltpu.make_async_remote_copy(
            src_ref=slot(a, *block) if src is None else src, dst_ref=slot(a, *block),
            send_sem=send_sems.at[k * n + a], recv_sem=recv_sems.at[k * n + a], device_id=to, device_id_type=MESH)

    mine = [pltpu.make_async_copy(x_refs[a], slot(a, *me), local_sems.at[a]) for a in range(n)]
    first = [copy(0, a, me, sibling, src=x_refs[a]) for a in range(n)]
    first += [copy(1 + j, a, me, (*chip, cc), src=x_refs[a]) for j, chip in enumerate(chips) for a in range(n)]

    def begin():
        for cp in mine + first:
            cp.start()

    def end():
        passed = []
        for j, chip in enumerate(chips):
            for a in range(n):
                copy(1 + j, a, (*chip, cc), me).wait_recv()
                passed.append(copy(4 + j, a, (*chip, cc), sibling))
                passed[-1].start()
        for a in range(n):
            copy(0, a, sibling, me).wait_recv()
        for j, chip in enumerate(chips):
            for a in range(n):
                copy(4 + j, a, (*chip, 1 - cc), me).wait_recv()
        for cp in first + passed:
            cp.wait_send()
        for cp in mine:
            cp.wait()

    return begin, end


def _exchange_parts(g_refs, out_refs, b_refs, ball_refs, send_sems, recv_sems, local_sems):
    n, nb = len(g_refs), len(b_refs)
    per_peer = n + nb
    x, y, cc = lax.axis_index("x"), lax.axis_index("y"), lax.axis_index("c")
    my_idx = 4 * x + 2 * y + cc
    mine = [pltpu.make_async_copy(g_refs[a].at[my_idx], out_refs[a].at[my_idx], local_sems.at[a]) for a in range(n)]
    mine += [pltpu.make_async_copy(b_refs[a], ball_refs[a].at[my_idx], local_sems.at[n + a]) for a in range(nb)]
    copies = []
    for k in range(1, N_DEV):
        px = 1 - x if k & 4 else x
        py = 1 - y if k & 2 else y
        pc = 1 - cc if k & 1 else cc
        base = (k - 1) * per_peer
        for a in range(n):
            copies.append(pltpu.make_async_remote_copy(
                src_ref=g_refs[a].at[4 * px + 2 * py + pc], dst_ref=out_refs[a].at[my_idx],
                send_sem=send_sems.at[base + a], recv_sem=recv_sems.at[base + a],
                device_id=(px, py, pc), device_id_type=MESH))
        for a in range(nb):
            copies.append(pltpu.make_async_remote_copy(
                src_ref=b_refs[a], dst_ref=ball_refs[a].at[my_idx], send_sem=send_sems.at[base + n + a],
                recv_sem=recv_sems.at[base + n + a], device_id=(px, py, pc), device_id_type=MESH))

    def begin():
        for cp in mine + copies:
            cp.start()

    def end():
        for cp in copies:
            cp.wait_recv()
        for cp in copies:
            cp.wait_send()
        for cp in mine:
            cp.wait()

    return begin, end


def _pcall_hosting(body, args, exchange, bcast, *, name, grid, in_specs, out_specs, out_shape, compiler_params,
                   scratch_shapes=(), gather=()):
    n_in, n_out, n_x, n_b, n_g = len(in_specs), len(out_specs), len(exchange), len(bcast), len(gather)
    n_d = n_x + n_b
    n_c, n_scr = n_d + n_g, len(scratch_shapes)

    def hosted(*refs):
        ins, c_in = refs[:n_in], refs[n_in:n_in + n_c]
        outs, c_out = refs[n_in + n_c:n_in + n_c + n_out], refs[n_in + n_c + n_out:n_in + 2 * n_c + n_out]
        scratch, sems = refs[n_in + 2 * n_c + n_out:][:n_scr], refs[n_in + 2 * n_c + n_out + n_scr:]
        parts = []
        if n_d:
            parts.append(_exchange_parts(c_in[:n_x], c_out[:n_x], c_in[n_x:n_d], c_out[n_x:n_d], *sems[:3]))
        if n_g:
            parts.append(_gather_parts(c_in[n_d:], c_out[n_d:], *sems[-3:]))
        first = functools.reduce(jnp.logical_and, [pl.program_id(k) == 0 for k in range(len(grid))])
        last = functools.reduce(jnp.logical_and, [pl.program_id(k) == grid[k] - 1 for k in range(len(grid))])
        for begin, _ in parts:
            pl.when(first)(begin)
        body(*ins, *outs, *scratch)
        for _, end in parts:
            pl.when(last)(end)

    landed_shape = [jax.ShapeDtypeStruct(e.shape, e.dtype) for e in exchange]
    landed_shape += [jax.ShapeDtypeStruct((N_DEV,) + b.shape, b.dtype) for b in list(bcast) + list(gather)]
    sems = (_sem_shapes(7 * n_d, n_d) if n_d else []) + (_sem_shapes(7 * n_g, n_g) if n_g else [])
    res = _pcall(hosted, name=name, grid=grid, in_specs=list(in_specs) + [_HBM] * n_c,
                 out_specs=list(out_specs) + [_HBM] * n_c, out_shape=list(out_shape) + landed_shape,
                 scratch_shapes=list(scratch_shapes) + sems,
                 compiler_params=compiler_params)(*args, *exchange, *bcast, *gather)
    return res[:n_out], res[n_out:]


def _all_gather(blks, name):
    n = len(blks)

    def body(*refs):
        begin, end = _gather_parts(refs[:n], refs[n:2 * n], *refs[2 * n:])
        begin()
        end()

    return _pcall(body, name=name, out_shape=[jax.ShapeDtypeStruct((N_DEV,) + b.shape, b.dtype) for b in blks],
                  in_specs=[_HBM] * n, out_specs=[_HBM] * n, scratch_shapes=_sem_shapes(7 * n, n))(*blks)


def _sum_and_adamw(p_ref, w, m, v):
    g = p_ref[0].astype(F32)
    for s in range(1, N_DEV):
        g = g + p_ref[s].astype(F32)
    nm = ADAM_B1 * m + (1.0 - ADAM_B1) * g
    nv = ADAM_B2 * v + (1.0 - ADAM_B2) * (g * g)
    m_hat = nm / (1.0 - ADAM_B1 ** ADAM_STEP)
    v_hat = nv / (1.0 - ADAM_B2 ** ADAM_STEP)
    return g, -ADAM_LR * (m_hat / (jnp.sqrt(v_hat) + ADAM_EPS) + ADAM_WD * w), nm, nv


def _reduce_adamw_vectors(parts, w, m, v, shapes, name):
    table = _row_table(shapes)

    def body(p_ref, w_ref, m_ref, v_ref, *out_refs):
        results = _sum_and_adamw(p_ref, w_ref[...], m_ref[...], v_ref[...])
        for kind, val in enumerate(results):
            for t, (r0, nr, nc) in enumerate(table):
                out_refs[kind * len(table) + t][...] = val[r0:r0 + nr, 0:nc]

    outs = _pcall(body, name=name, out_shape=[jax.ShapeDtypeStruct((nr, nc), F32) for _, nr, nc in table] * 4)(parts, w, m, v)
    return [[outs[kind * len(table) + t].reshape(s) for t, s in enumerate(shapes)] for kind in range(4)]


def _reduce_adamw(parts, w, m, v, name):
    lead = w.shape[:-2]
    r, c = w.shape[-2:]
    tr = 128 if r % 128 == 0 else r
    zeros = (0,) * len(lead)
    at = zeros + (slice(None), slice(None))

    def body(p_ref, w_ref, m_ref, v_ref, g_ref, d_ref, nm_ref, nv_ref):
        g_ref[at], d_ref[at], nm_ref[at], nv_ref[at] = _sum_and_adamw(p_ref, w_ref[at], m_ref[at], v_ref[at])

    row = pl.BlockSpec((1,) * len(lead) + (tr, c), lambda i: zeros + (i, 0))
    out = jax.ShapeDtypeStruct(w.shape, F32)
    return _pcall(
        body, name=name, grid=(r // tr,),
        in_specs=[pl.BlockSpec((N_DEV, tr, c), lambda i: (0, i, 0)), row, row, row],
        out_specs=[row, row, row, row], out_shape=[out, out, out, out], compiler_params=_cp(1),
    )(parts, w, m, v)


def _proj_fwd(x, nw, w, wab, name, gather=()):
    t, npj = x.shape[0], w.shape[1]
    tm, tn = min(512, t), 1024
    has_ab = wab is not None
    n_in, n_out, n_g = 3 + has_ab, 2 + has_ab, len(gather)
    n_i = t // tm

    def body(*refs):
        ins, g_ins = refs[:n_in], refs[n_in:n_in + n_g]
        outs = refs[n_in + n_g:n_in + n_g + n_out]
        g_outs = refs[n_in + n_g + n_out:n_in + 2 * n_g + n_out]
        if has_ab:
            (x_ref, nw_ref, w_ref, wab_ref), (p_ref, h_ref, ab_ref) = ins, outs
        else:
            (x_ref, nw_ref, w_ref), (p_ref, h_ref) = ins, outs
        if n_g:
            begin, end = _gather_parts(g_ins, g_outs, *refs[n_in + 2 * n_g + n_out:])
            pl.when(pl.program_id(0) == 0)(begin)

        for rows in (slice(0, tm // 2), slice(tm // 2, tm)):
            xv = x_ref[rows, :]
            hv = (xv * lax.rsqrt(jnp.mean(xv * xv, axis=-1, keepdims=True) + EPS) * nw_ref[...]).astype(MXU)
            h_ref[rows, :] = hv.astype(ACT)
            if has_ab:
                ab_ref[rows, :] = _dot(hv, wab_ref[...], NN)
            for j in range(npj // tn):
                p_ref[rows, j * tn:(j + 1) * tn] = _dot(hv, w_ref[:, j * tn:(j + 1) * tn], NN).astype(ACT)
        if n_g:
            pl.when(pl.program_id(0) == n_i - 1)(end)

    in_specs = [pl.BlockSpec((tm, D), lambda i: (i, 0)), _full((1, D)), _full((D, npj))]
    out_specs = [pl.BlockSpec((tm, npj), lambda i: (i, 0)), pl.BlockSpec((tm, D), lambda i: (i, 0))]
    out_shape = [jax.ShapeDtypeStruct((t, npj), ACT), jax.ShapeDtypeStruct((t, D), ACT)]
    args = [x, nw, w]
    if has_ab:
        in_specs.append(_full((D, HEAD)))
        out_specs.append(pl.BlockSpec((tm, HEAD), lambda i: (i, 0)))
        out_shape.append(jax.ShapeDtypeStruct((t, HEAD), F32))
        args.append(wab)
    scratch = []
    if n_g:
        in_specs += [_HBM] * n_g
        out_specs += [_HBM] * n_g
        out_shape += [jax.ShapeDtypeStruct((N_DEV,) + b.shape, b.dtype) for b in gather]
        args += list(gather)
        scratch += _sem_shapes(7 * n_g, n_g)
    return _pcall(body, name=name, grid=(n_i,), in_specs=in_specs, out_specs=out_specs,
                  out_shape=out_shape, scratch_shapes=scratch, compiler_params=_cp(1))(*args)


def _proj_bwd_x(dps, ws, x, dxn, nw, name, exchange=(), bcast=()):
    t = x.shape[0]
    tm = min(512, t)
    n, n_x, n_b = len(dps), len(exchange), len(bcast)
    n_c, n_steps = n_x + n_b, t // tm

    def body(*refs):
        dp_refs, w_refs = refs[:n], refs[n:2 * n]
        x_ref, dxn_ref, nw_ref = refs[2 * n:2 * n + 3]
        c_in = refs[2 * n + 3:2 * n + 3 + n_c]
        dx_ref, dnw_ref = refs[2 * n + 3 + n_c:2 * n + 5 + n_c]
        c_out = refs[2 * n + 5 + n_c:2 * n + 5 + 2 * n_c]
        if n_c:
            begin, end = _exchange_parts(c_in[:n_x], c_out[:n_x], c_in[n_x:], c_out[n_x:], *refs[2 * n + 5 + 2 * n_c:])
            pl.when(pl.program_id(0) == 0)(begin)
        @pl.when(pl.program_id(0) == 0)
        def _():
            dnw_ref[...] = jnp.zeros_like(dnw_ref)

        halves = [slice(r0, r0 + tm // 2) for r0 in (0, tm // 2)]
        dhs = []
        for rows in halves:
            dh = _mdot(dp_refs[0][rows, :], w_refs[0][...], NT)
            for k in range(1, n):
                dh = dh + _mdot(dp_refs[k][rows, :], w_refs[k][...], NT)
            dhs.append(dh)
        for rows, dh in zip(halves, dhs):
            xv = x_ref[rows, :]
            r = lax.rsqrt(jnp.mean(xv * xv, axis=-1, keepdims=True) + EPS)
            dnw_ref[...] += jnp.sum(dh * xv * r, axis=0, keepdims=True)
            dhw = dh * nw_ref[...]
            dx_ref[rows, :] = dxn_ref[rows, :] + r * (dhw - xv * (r * r) * jnp.mean(dhw * xv, axis=-1, keepdims=True))
        if n_c:
            pl.when(pl.program_id(0) == n_steps - 1)(end)

    row = pl.BlockSpec((tm, D), lambda i: (i, 0))
    in_specs = [pl.BlockSpec((tm, dp.shape[1]), lambda i: (i, 0)) for dp in dps]
    in_specs += [_full(w.shape) for w in ws]
    in_specs += [row, row, _full((1, D))] + [_HBM] * n_c
    out_shape = [jax.ShapeDtypeStruct((t, D), F32), jax.ShapeDtypeStruct((1, D), F32)]
    out_shape += [jax.ShapeDtypeStruct(e.shape, e.dtype) for e in exchange]
    out_shape += [jax.ShapeDtypeStruct((N_DEV,) + b.shape, b.dtype) for b in bcast]
    return _pcall(body, name=name, grid=(n_steps,), in_specs=in_specs, out_specs=[row, _full((1, D))] + [_HBM] * n_c,
                  out_shape=out_shape, scratch_shapes=_sem_shapes(7 * n_c, n_c) if n_c else [],
                  compiler_params=_cp(1))(*dps, *ws, x, dxn, nw, *exchange, *bcast)


def _matmul_tn(a, b, name, out_dtype=XCH, exchange=(), bcast=()):
    t, m = a.shape
    n = b.shape[1]
    tm, tn = min(1024, t), min(1024, n)
    n_t = t // tm

    def body(a_ref, b_ref, o_ref, acc):
        @pl.when(pl.program_id(1) == 0)
        def _():
            acc[...] = jnp.zeros_like(acc)

        acc[...] += _mdot(a_ref[...], b_ref[...], TN)

        @pl.when(pl.program_id(1) == n_t - 1)
        def _():
            o_ref[...] = acc[...].astype(out_dtype)

    call = dict(name=name, grid=(n // tn, n_t),
                in_specs=[pl.BlockSpec((tm, m), lambda j, i: (i, 0)), pl.BlockSpec((tm, tn), lambda j, i: (i, j))],
                out_specs=[pl.BlockSpec((m, tn), lambda j, i: (0, j))],
                out_shape=[jax.ShapeDtypeStruct((m, n), out_dtype)], scratch_shapes=[pltpu.VMEM((m, tn), F32)],
                compiler_params=_cp(2))
    if not exchange and not bcast:
        return _pcall(body, **call)(a, b)[0]
    (out,), landed = _pcall_hosting(body, (a, b), exchange, bcast, **call)
    return out, landed


def _memkv_fwd(mem, mw, wkv):
    def body(mem_ref, mw_ref, w_ref, kv_ref):
        mv = mem_ref[...]
        mn = mv * lax.rsqrt(jnp.mean(mv * mv, axis=-1, keepdims=True) + EPS) * mw_ref[...]
        kv_ref[...] = _mdot(mn, w_ref[...], NN)

    return _pcall(body, name="memkv_fwd", out_shape=jax.ShapeDtypeStruct((N_MEM, 2 * D_XA), F32),
                  compiler_params=pltpu.CompilerParams(vmem_limit_bytes=VMEM_LIMIT))(mem, mw, wkv)


def _memkv_bwd(mem, mw, wkv, dkv):
    def body(mem_ref, mw_ref, w_ref, dkv_ref, dw_ref, dmw_ref):
        mv = mem_ref[...]
        r = lax.rsqrt(jnp.mean(mv * mv, axis=-1, keepdims=True) + EPS)
        mn = mv * r * mw_ref[...]
        dkvv = dkv_ref[...]
        dw_ref[...] = _mdot(mn, dkvv, TN)
        dmn = _mdot(dkvv, w_ref[...], NT)
        dmw_ref[...] = jnp.sum(dmn * mv * r, axis=0, keepdims=True)

    return _pcall(body, name="memkv_bwd",
                  out_shape=[jax.ShapeDtypeStruct((D, 2 * D_XA), F32), jax.ShapeDtypeStruct((1, D), F32)],
                  compiler_params=pltpu.CompilerParams(vmem_limit_bytes=VMEM_LIMIT))(mem, mw, wkv, dkv)


def _attend(q, kv):
    heads = range(XA_HEADS)
    qs = [q[:, h * HEAD:(h + 1) * HEAD] for h in heads]
    ks = [kv[:, h * HEAD:(h + 1) * HEAD] for h in heads]
    vs = [kv[:, D_XA + h * HEAD:D_XA + (h + 1) * HEAD] for h in heads]
    ss = [_mdot(qs[h], ks[h], NT) * (HEAD ** -0.5) for h in heads]
    es = [jnp.exp(s - jnp.max(s, axis=-1, keepdims=True)) for s in ss]
    ps = [e / jnp.sum(e, axis=-1, keepdims=True) for e in es]
    return ps, [_mdot(ps[h], vs[h], NN) for h in heads]


def _tail_fwd(p, ymix, x, kv, wout, npost, name, gather=()):
    t = x.shape[0]
    tm = min(512, t)

    def body(z_ref, q_ref, y_ref, x_ref, kv_ref, w_ref, np_ref, xn_ref, o_ref):
        _, outs = _attend(q_ref[...], kv_ref[...])
        cat = jnp.concatenate([y_ref[...]] + [a.astype(ACT) for a in outs], axis=1)
        g = cat * _silu(z_ref[...])
        o = _mdot(g, w_ref[...], NN)
        o_ref[...] = o.astype(ACT)
        xn_ref[...] = x_ref[...] + o * lax.rsqrt(jnp.mean(o * o, axis=-1, keepdims=True) + EPS) * np_ref[...]

    row = pl.BlockSpec((tm, D), lambda i: (i, 0))
    call = dict(
        name=name, grid=(t // tm,),
        in_specs=[pl.BlockSpec((tm, D_CAT), lambda i: (i, 0)), pl.BlockSpec((tm, D_XA), lambda i: (i, D_CAT // D_XA)),
                  row, row, _full((N_MEM, 2 * D_XA)), _full((D_CAT, D)), _full((1, D))],
        out_specs=[row, row],
        out_shape=[jax.ShapeDtypeStruct((t, D), F32), jax.ShapeDtypeStruct((t, D), ACT)], compiler_params=_cp(1))
    args = (p, p, ymix, x, kv, wout, npost)
    if not gather:
        return tuple(_pcall(body, **call)(*args))
    outs, landed = _pcall_hosting(body, args, (), (), gather=gather, **call)
    return tuple(outs) + tuple(landed)


def _tail_bwd(dxn, o, p, ymix, kv, wout, npost, name, exchange=()):
    t = dxn.shape[0]
    tm = min(512, t)
    n_x, n_steps = len(exchange), t // tm

    def body(*refs):
        dxn_ref, o_ref, z_ref, q_ref, y_ref, kv_ref, w_ref, np_ref = refs[:8]
        dzq_ref, dy_ref, dw_ref, dnp_ref, dkv_ref = refs[8 + n_x:13 + n_x]
        dw_acc = refs[13 + 2 * n_x]
        if n_x:
            begin, end = _exchange_parts(refs[8:8 + n_x], refs[13 + n_x:13 + 2 * n_x], (), (), *refs[14 + 2 * n_x:])
            pl.when(pl.program_id(0) == 0)(begin)

        @pl.when(pl.program_id(0) == 0)
        def _():
            dw_acc[...] = jnp.zeros_like(dw_acc)
            dnp_ref[...] = jnp.zeros_like(dnp_ref)
            dkv_ref[...] = jnp.zeros_like(dkv_ref)

        q = q_ref[...]
        kvv = kv_ref[...]
        z = z_ref[...]
        ps, outs = _attend(q, kvv)
        cat = jnp.concatenate([y_ref[...]] + [a.astype(ACT) for a in outs], axis=1)
        sz = _silu(z)
        g = cat * sz
        ov = o_ref[...].astype(F32)
        dr = dxn_ref[...]
        rr = lax.rsqrt(jnp.mean(ov * ov, axis=-1, keepdims=True) + EPS)
        dnp_ref[...] += jnp.sum(dr * ov * rr, axis=0, keepdims=True)
        dow = dr * np_ref[...]
        do = rr * (dow - ov * (rr * rr) * jnp.mean(dow * ov, axis=-1, keepdims=True))
        dg = _mdot(do, w_ref[...], NT).astype(ACT)
        dw_acc[...] += _mdot(g, do, TN)
        dcat = dg * sz
        dzq_ref[:, 0:D_CAT] = dg * cat * _dsilu(z)
        dy_ref[...] = dcat[:, 0:D]
        heads = range(XA_HEADS)
        dohs = [dcat[:, D + h * HEAD:D + (h + 1) * HEAD] for h in heads]
        dps = [_mdot(dohs[h], kvv[:, D_XA + h * HEAD:D_XA + (h + 1) * HEAD], NT) for h in heads]
        dss = [ps[h] * (dps[h] - jnp.sum(dps[h] * ps[h], axis=-1, keepdims=True)) for h in heads]
        dqs = [_mdot(dss[h], kvv[:, h * HEAD:(h + 1) * HEAD], NN) * (HEAD ** -0.5) for h in heads]
        dks = [_mdot(dss[h], q[:, h * HEAD:(h + 1) * HEAD], TN) * (HEAD ** -0.5) for h in heads]
        dvs = [_mdot(ps[h], dohs[h], TN) for h in heads]
        for h in heads:
            lo = h * HEAD
            dzq_ref[:, D_CAT + lo:D_CAT + lo + HEAD] = dqs[h].astype(ACT)
            dkv_ref[:, lo:lo + HEAD] += dks[h]
            dkv_ref[:, D_XA + lo:D_XA + lo + HEAD] += dvs[h]

        @pl.when(pl.program_id(0) == n_steps - 1)
        def _():
            dw_ref[...] = dw_acc[...].astype(XCH)

        if n_x:
            pl.when(pl.program_id(0) == n_steps - 1)(end)

    row = pl.BlockSpec((tm, D), lambda i: (i, 0))
    return _pcall(
        body, name=name, grid=(n_steps,),
        in_specs=[row, row, pl.BlockSpec((tm, D_CAT), lambda i: (i, 0)),
                  pl.BlockSpec((tm, D_XA), lambda i: (i, D_CAT // D_XA)), row,
                  _full((N_MEM, 2 * D_XA)), _full((D_CAT, D)), _full((1, D))] + [_HBM] * n_x,
        out_specs=[pl.BlockSpec((tm, ZQ), lambda i: (i, 0)), row, _full((D_CAT, D)), _full((1, D)),
                   _full((N_MEM, 2 * D_XA))] + [_HBM] * n_x,
        out_shape=[jax.ShapeDtypeStruct((t, ZQ), ACT), jax.ShapeDtypeStruct((t, D), ACT),
                   jax.ShapeDtypeStruct((D_CAT, D), XCH), jax.ShapeDtypeStruct((1, D), F32),
                   jax.ShapeDtypeStruct((N_MEM, 2 * D_XA), F32)]
        + [jax.ShapeDtypeStruct(e.shape, e.dtype) for e in exchange],
        scratch_shapes=[pltpu.VMEM((D_CAT, D), F32)] + (_sem_shapes(7 * n_x, n_x) if n_x else []),
        compiler_params=_cp(1),
    )(dxn, o, p, p, ymix, kv, wout, npost, *exchange)


def _gmlp_chunk(us, vs, lnws, lnbs, wss, bss):
    gv = [jax.nn.gelu(v) for v in vs]
    mean = sum(jnp.sum(v, axis=-1, keepdims=True) for v in gv) / D
    cen = [v - mean for v in gv]
    var = sum(jnp.sum(c * c, axis=-1, keepdims=True) for c in cen) / D
    rstd = lax.rsqrt(var + EPS)
    row = lax.broadcasted_iota(jnp.int32, (GM_CHUNK, GM_CHUNK), 0)
    col = lax.broadcasted_iota(jnp.int32, (GM_CHUNK, GM_CHUNK), 1)
    ys = []
    for g in range(GM_GROUPS):
        vn = cen[g] * rstd * lnws[g] + lnbs[g]
        sp = _nn(jnp.where(row >= col, wss[g], 0.0), vn) + bss[g]
        ys.append(jax.nn.gelu(us[g]) * sp)
    return ys


def _split_cols(v, n, width=HEAD):
    return [v[:, k * width:(k + 1) * width] for k in range(n)]


def _gmlp_operands(u_ref, v_ref, lnw_ref, lnb_ref, ws_ref, bst_ref, r0):
    us = _split_cols(u_ref[r0:r0 + GM_CHUNK, :].astype(F32), GM_GROUPS)
    vs = _split_cols(v_ref[r0:r0 + GM_CHUNK, :].astype(F32), GM_GROUPS)
    lnws = _split_cols(lnw_ref[...], GM_GROUPS)
    lnbs = _split_cols(lnb_ref[...], GM_GROUPS)
    wss = [ws_ref[g] for g in range(GM_GROUPS)]
    bst = bst_ref[...]
    bss = [jnp.sum(bst * _onehot_row(HEAD, g), axis=1, keepdims=True) for g in range(GM_GROUPS)]
    return us, vs, lnws, lnbs, wss, bss


def _gmlp_rows(t):
    return min(GM_CHUNKS_PER_STEP, t // GM_CHUNK) * GM_CHUNK


def _gmlp_specs(rows):
    return [pl.BlockSpec((rows, D), lambda i: (i, ZQ // D)), pl.BlockSpec((rows, D), lambda i: (i, ZQ // D + 1)),
            _full((1, D)), _full((1, D)), _full((GM_GROUPS, GM_CHUNK, GM_CHUNK)), _full((GM_CHUNK, HEAD))]


def _gmlp_fwd(p, lnw, lnb, ws, bst, name):
    t = p.shape[0]
    rows = _gmlp_rows(t)

    def body(u_ref, v_ref, lnw_ref, lnb_ref, ws_ref, bst_ref, y_ref):
        for r0 in range(0, rows, GM_CHUNK):
            ys = _gmlp_chunk(*_gmlp_operands(u_ref, v_ref, lnw_ref, lnb_ref, ws_ref, bst_ref, r0))
            for g in range(GM_GROUPS):
                y_ref[r0:r0 + GM_CHUNK, g * HEAD:(g + 1) * HEAD] = ys[g].astype(ACT)

    return _pcall(body, name=name, grid=(t // rows,), in_specs=_gmlp_specs(rows),
                  out_specs=pl.BlockSpec((rows, D), lambda i: (i, 0)),
                  out_shape=jax.ShapeDtypeStruct((t, D), ACT), compiler_params=_cp(1))(p, p, lnw, lnb, ws, bst)


def _gmlp_bwd(p, dy, lnw, lnb, ws, bst, name, exchange=(), bcast=()):
    t = p.shape[0]
    rows = _gmlp_rows(t)

    def body(u_ref, v_ref, lnw_ref, lnb_ref, ws_ref, bst_ref, dy_ref, duv_ref, dlnw_ref, dlnb_ref, dws_ref, dbst_ref):
        @pl.when(pl.program_id(0) == 0)
        def _():
            dlnw_ref[...] = jnp.zeros_like(dlnw_ref)
            dlnb_ref[...] = jnp.zeros_like(dlnb_ref)
            dws_ref[...] = jnp.zeros_like(dws_ref)
            dbst_ref[...] = jnp.zeros_like(dbst_ref)

        for r0 in range(0, rows, GM_CHUNK):
            ops = _gmlp_operands(u_ref, v_ref, lnw_ref, lnb_ref, ws_ref, bst_ref, r0)
            _, vjp = jax.vjp(_gmlp_chunk, *ops)
            dus, dvs, dlnws, dlnbs, dwss, dbss = vjp(_split_cols(dy_ref[r0:r0 + GM_CHUNK, :].astype(F32), GM_GROUPS))
            dbst = jnp.zeros((GM_CHUNK, HEAD), F32)
            for g in range(GM_GROUPS):
                lo = g * HEAD
                duv_ref[r0:r0 + GM_CHUNK, lo:lo + HEAD] = dus[g].astype(ACT)
                duv_ref[r0:r0 + GM_CHUNK, D + lo:D + lo + HEAD] = dvs[g].astype(ACT)
                dlnw_ref[:, lo:lo + HEAD] += dlnws[g]
                dlnb_ref[:, lo:lo + HEAD] += dlnbs[g]
                dws_ref[g] += dwss[g]
                dbst = dbst + dbss[g] * _onehot_row(HEAD, g)
            dbst_ref[...] += dbst

    call = dict(
        name=name, grid=(t // rows,),
        in_specs=_gmlp_specs(rows) + [pl.BlockSpec((rows, D), lambda i: (i, 0))],
        out_specs=[pl.BlockSpec((rows, 2 * D), lambda i: (i, 0)), _full((1, D)), _full((1, D)),
                   _full((GM_GROUPS, GM_CHUNK, GM_CHUNK)), _full((GM_CHUNK, HEAD))],
        out_shape=[jax.ShapeDtypeStruct((t, 2 * D), ACT), jax.ShapeDtypeStruct((1, D), F32),
                   jax.ShapeDtypeStruct((1, D), F32), jax.ShapeDtypeStruct((GM_GROUPS, GM_CHUNK, GM_CHUNK), F32),
                   jax.ShapeDtypeStruct((GM_CHUNK, HEAD), F32)],
        compiler_params=_cp(1))
    args = (p, p, lnw, lnb, ws, bst, dy)
    if not exchange and not bcast:
        return _pcall(body, **call)(*args), ()
    return _pcall_hosting(body, args, exchange, bcast, **call)


def _prev_halo(tm, col):
    return pl.BlockSpec((HALO, D), lambda i: (jnp.maximum(i * (tm // HALO) - 1, 0), col))


def _next_halo(tm, col, n_tiles):
    return pl.BlockSpec((HALO, D), lambda i: (jnp.minimum(i + 1, n_tiles - 1) * (tm // HALO), col))


def _taps_back(ext, w, width):
    acc = None
    for k in range(width):
        s = width - 1 - k
        term = w[k:k + 1, :] * (pltpu.roll(ext, s, 0) if s else ext)[HALO:, :]
        acc = term if acc is None else acc + term
    return acc


def _taps_fwd(ext, w, width, n):
    rows = ext.shape[0]
    acc = None
    for k in range(width):
        s = width - 1 - k
        term = w[k:k + 1, :] * (pltpu.roll(ext, rows - s, 0) if s else ext)[0:n, :]
        acc = term if acc is None else acc + term
    return acc


def _sconv_fwd(p, cw, name):
    t = p.shape[0]
    tm = min(256, t)

    def body(b_ref, c_ref, h_ref, cp_ref, hp_ref, w_ref, y_ref):
        first = pl.program_id(0) == 0
        prev = jnp.where(first, 0.0, cp_ref[...].astype(F32) * hp_ref[...].astype(F32))
        ext = jnp.concatenate([prev, c_ref[...].astype(F32) * h_ref[...].astype(F32)], axis=0)
        y_ref[...] = (b_ref[...].astype(F32) * _taps_back(ext, w_ref[...], 3)).astype(ACT)

    c0 = ZQ // D
    tile = [pl.BlockSpec((tm, D), lambda i, c=c: (i, c)) for c in (c0, c0 + 1, c0 + 2)]
    return _pcall(body, name=name, grid=(t // tm,),
                  in_specs=tile + [_prev_halo(tm, c0 + 1), _prev_halo(tm, c0 + 2), _full((HALO, D))],
                  out_specs=pl.BlockSpec((tm, D), lambda i: (i, 0)),
                  out_shape=jax.ShapeDtypeStruct((t, D), ACT), compiler_params=_cp(1))(p, p, p, p, p, cw)


def _sconv_bwd(p, dy, cw, name):
    t = p.shape[0]
    tm = min(256, t)
    n_tiles = t // tm

    def body(b_ref, c_ref, h_ref, cp_ref, hp_ref, bn_ref, dy_ref, dyn_ref, w_ref, d_ref, dw_ref):
        i = pl.program_id(0)

        @pl.when(i == 0)
        def _():
            dw_ref[...] = jnp.zeros_like(dw_ref)

        w = w_ref[...]
        bv, cv, hv = b_ref[...].astype(F32), c_ref[...].astype(F32), h_ref[...].astype(F32)
        dyv = dy_ref[...].astype(F32)
        prev = jnp.where(i == 0, 0.0, cp_ref[...].astype(F32) * hp_ref[...].astype(F32))
        ext = jnp.concatenate([prev, cv * hv], axis=0)
        conv = _taps_back(ext, w, 3)
        dconv = dyv * bv
        nxt = jnp.where(i == n_tiles - 1, 0.0, dyn_ref[...].astype(F32) * bn_ref[...].astype(F32))
        dc = _taps_fwd(jnp.concatenate([dconv, nxt], axis=0), w, 3, tm)
        d_ref[:, 0:D] = (dyv * conv).astype(ACT)
        d_ref[:, D:2 * D] = (dc * hv).astype(ACT)
        d_ref[:, 2 * D:3 * D] = (dc * cv).astype(ACT)
        for k in range(3):
            s = 2 - k
            shifted = (pltpu.roll(ext, s, 0) if s else ext)[HALO:, :]
            dw_ref[k:k + 1, :] += jnp.sum(dconv * shifted, axis=0, keepdims=True)

    c0 = ZQ // D
    tile = [pl.BlockSpec((tm, D), lambda i, c=c: (i, c)) for c in (c0, c0 + 1, c0 + 2)]
    return _pcall(
        body, name=name, grid=(n_tiles,),
        in_specs=tile + [_prev_halo(tm, c0 + 1), _prev_halo(tm, c0 + 2), _next_halo(tm, c0, n_tiles),
                         pl.BlockSpec((tm, D), lambda i: (i, 0)), _next_halo(tm, 0, n_tiles), _full((HALO, D))],
        out_specs=[pl.BlockSpec((tm, 3 * D), lambda i: (i, 0)), _full((HALO, D))],
        out_shape=[jax.ShapeDtypeStruct((t, 3 * D), ACT), jax.ShapeDtypeStruct((HALO, D), F32)],
        compiler_params=_cp(1),
    )(p, p, p, p, p, p, dy, dy, cw)


def _l2_heads(s, scale):
    outs, rs = [], []
    for hh in range(DN_HEADS):
        blk = s[:, hh * HEAD:(hh + 1) * HEAD]
        r = lax.rsqrt(jnp.sum(blk * blk, axis=-1, keepdims=True) + EPS)
        outs.append(blk * (r * scale))
        rs.append(r)
    return outs, rs


_QKV_SCALE = (HEAD ** -0.5, 1.0, None)


def _qkv_fwd(p, cw, name):
    t = p.shape[0]
    tm = min(256, t)

    def body(q_ref, k_ref, v_ref, qp_ref, kp_ref, vp_ref, w_ref, o_ref):
        first = pl.program_id(0) == 0
        for part, (ref, pref) in enumerate(((q_ref, qp_ref), (k_ref, kp_ref), (v_ref, vp_ref))):
            prev = jnp.where(first, 0.0, pref[...].astype(F32))
            ext = jnp.concatenate([prev, ref[...].astype(F32)], axis=0)
            s = _silu(_taps_back(ext, w_ref[:, part * D:(part + 1) * D], 4))
            if _QKV_SCALE[part] is None:
                o_ref[:, part * D:(part + 1) * D] = s.astype(ACT)
            else:
                outs, _ = _l2_heads(s, _QKV_SCALE[part])
                for hh in range(DN_HEADS):
                    o_ref[:, part * D + hh * HEAD:part * D + (hh + 1) * HEAD] = outs[hh].astype(ACT)

    c0 = ZQ // D
    tile = [pl.BlockSpec((tm, D), lambda i, c=c: (i, c)) for c in (c0, c0 + 1, c0 + 2)]
    halo = [_prev_halo(tm, c) for c in (c0, c0 + 1, c0 + 2)]
    return _pcall(body, name=name, grid=(t // tm,), in_specs=tile + halo + [_full((HALO, 3 * D))],
                  out_specs=pl.BlockSpec((tm, 3 * D), lambda i: (i, 0)),
                  out_shape=jax.ShapeDtypeStruct((t, 3 * D), ACT), compiler_params=_cp(1))(p, p, p, p, p, p, cw)


def _qkv_bwd(p, dqkv, cw, name):
    t = p.shape[0]
    tm = min(256, t)
    n_tiles = t // tm

    def body(*refs):
        tiles, prevs, nexts = refs[0:3], refs[3:6], refs[6:9]
        d_tiles, d_nexts = refs[9:12], refs[12:15]
        w_ref, o_ref, dw_ref = refs[15:]
        i = pl.program_id(0)

        @pl.when(i == 0)
        def _():
            dw_ref[...] = jnp.zeros_like(dw_ref)

        for part in range(3):
            w = w_ref[:, part * D:(part + 1) * D]
            prev = jnp.where(i == 0, 0.0, prevs[part][...].astype(F32))
            ext = jnp.concatenate([prev, tiles[part][...].astype(F32), nexts[part][...].astype(F32)], axis=0)
            xc = _taps_back(ext, w, 4)
            dout = jnp.concatenate([d_tiles[part][...].astype(F32), d_nexts[part][...].astype(F32)], axis=0)
            s = _silu(xc)
            if _QKV_SCALE[part] is None:
                ds = dout
            else:
                scale = _QKV_SCALE[part]
                pieces = []
                for hh in range(DN_HEADS):
                    blk = s[:, hh * HEAD:(hh + 1) * HEAD]
                    dblk = dout[:, hh * HEAD:(hh + 1) * HEAD]
                    r = lax.rsqrt(jnp.sum(blk * blk, axis=-1, keepdims=True) + EPS)
                    pieces.append(scale * r * (dblk - blk * (r * r) * jnp.sum(dblk * blk, axis=-1, keepdims=True)))
                ds = jnp.concatenate(pieces, axis=1)
            dxc = ds * _dsilu(xc)
            row = lax.broadcasted_iota(jnp.int32, (tm + HALO, 1), 0)
            dxc = jnp.where(jnp.logical_and(i == n_tiles - 1, row >= tm), 0.0, dxc)
            o_ref[:, part * D:(part + 1) * D] = _taps_fwd(dxc, w, 4, tm).astype(ACT)
            for k in range(4):
                sh = 3 - k
                shifted = (pltpu.roll(ext, sh, 0) if sh else ext)[HALO:HALO + tm, :]
                dw_ref[k:k + 1, part * D:(part + 1) * D] += jnp.sum(dxc[0:tm, :] * shifted, axis=0, keepdims=True)

    c0 = ZQ // D
    cols = (c0, c0 + 1, c0 + 2)
    tile = [pl.BlockSpec((tm, D), lambda i, c=c: (i, c)) for c in cols]
    dtile = [pl.BlockSpec((tm, D), lambda i, c=c: (i, c)) for c in range(3)]
    in_specs = (tile + [_prev_halo(tm, c) for c in cols] + [_next_halo(tm, c, n_tiles) for c in cols]
                + dtile + [_next_halo(tm, c, n_tiles) for c in range(3)] + [_full((HALO, 3 * D))])
    return _pcall(
        body, name=name, grid=(n_tiles,), in_specs=in_specs,
        out_specs=[pl.BlockSpec((tm, 3 * D), lambda i: (i, 0)), _full((HALO, 3 * D))],
        out_shape=[jax.ShapeDtypeStruct((t, 3 * D), ACT), jax.ShapeDtypeStruct((HALO, 3 * D), F32)],
        compiler_params=_cp(1),
    )(*([p] * 9), *([dqkv] * 6), cw)


def _tri_masks(n):
    row = lax.broadcasted_iota(jnp.int32, (n, n), 0)
    col = lax.broadcasted_iota(jnp.int32, (n, n), 1)
    return row, col


@jax.custom_vjp
def _unit_lower_inverses(mats):
    n = DN_CHUNK
    row, col = _tri_masks(n)
    eye = (row == col).astype(F32)
    same16 = (row // 16) == (col // 16)
    same32 = (row // 32) == (col // 32)
    pw = [jnp.where(same16, a, 0.0) for a in mats]
    x = [eye - p for p in pw]
    for _ in range(3):
        pw = [_hnn(p, p) for p in pw]
        x = [_hnn(xi, eye + p) for xi, p in zip(x, pw)]
    for keep in (jnp.logical_and(same32, jnp.logical_not(same16)), jnp.logical_not(same32)):
        inner = [_hnn(jnp.where(keep, a, 0.0), xi) for a, xi in zip(mats, x)]
        x = [xi - _hnn(xi, y) for xi, y in zip(x, inner)]
    return tuple(x)


def _uli_fwd(mats):
    t = _unit_lower_inverses(mats)
    return t, t


def _uli_bwd(ts, gs):
    inner = [_nt(g, t) for g, t in zip(gs, ts)]
    return (tuple(-_tn(t, y) for t, y in zip(ts, inner)),)


_unit_lower_inverses.defvjp(_uli_fwd, _uli_bwd)


@jax.custom_vjp
def _known_inverses(mats, ts):
    return ts


_known_inverses.defvjp(lambda mats, ts: (ts, ts),
                       lambda ts, gs: (_uli_bwd(ts, gs)[0], tuple(jnp.zeros_like(t) for t in ts)))


def _pick_col(m, k):
    return jnp.sum(m * _onehot_row(m.shape[1], k), axis=1, keepdims=True)


def _pick_row(m, k):
    hot = (lax.broadcasted_iota(jnp.int32, (m.shape[0], 1), 0) == k).astype(F32)
    return jnp.sum(m * hot, axis=0, keepdims=True)


def _delta_chunk(states, qs, ks, vs, ab, alog, dtb, onw, known_inverses=None):
    n = DN_CHUNK
    heads = range(DN_HEADS)
    row, col = _tri_masks(n)
    incl = row >= col
    lane = lax.broadcasted_iota(jnp.int32, (1, HEAD), 1)
    g_all = jnp.where(lane < DN_HEADS, -jnp.exp(alog) * jax.nn.softplus(ab + dtb), 0.0)
    c_cols = _hnn(incl.astype(F32), g_all)
    c_rows = _htn(g_all, (row <= col).astype(F32))
    g_tot = jnp.sum(g_all, axis=0, keepdims=True)
    beta_all = jax.nn.sigmoid(ab)
    ccol = [_pick_col(c_cols, h) for h in heads]
    crow = [_pick_row(c_rows, h) for h in heads]
    gl = [_pick_col(g_tot, h) for h in heads]
    beta = [_pick_col(beta_all, DN_HEADS + h) for h in heads]
    decay = [jnp.exp(jnp.where(incl, ccol[h] - crow[h], -1e30)) for h in heads]
    eg = [jnp.exp(ccol[h]) for h in heads]
    kb = [ks[h] * beta[h] for h in heads]
    amat = [jnp.where(row > col, _nt(kb[h], ks[h]) * decay[h], 0.0) for h in heads]
    if known_inverses is None:
        tmat = _unit_lower_inverses(tuple(amat))
    else:
        tmat = _known_inverses(tuple(amat), tuple(known_inverses))
    u = [_nn(tmat[h], vs[h] * beta[h]) for h in heads]
    w = [_nn(tmat[h], kb[h] * eg[h]) for h in heads]
    qk = [_nt(qs[h], ks[h]) * decay[h] for h in heads]
    v_new = [u[h] - _nn(w[h], states[h]) for h in heads]
    o = [_nn(qs[h] * eg[h], states[h]) + _nn(qk[h], v_new[h]) for h in heads]
    new_states = [states[h] * jnp.exp(gl[h]) + _tn(ks[h] * jnp.exp(gl[h] - ccol[h]), v_new[h]) for h in heads]
    ys = [o[h] * lax.rsqrt(jnp.mean(o[h] * o[h], axis=-1, keepdims=True) + EPS) * onw for h in heads]
    return (ys, new_states), tmat


def _head_cols(ref, r0):
    return [ref[r0:r0 + DN_CHUNK, h * HEAD:(h + 1) * HEAD].astype(F32) for h in range(DN_HEADS)]


def _delta_rows(t):
    return min(DN_CHUNKS_PER_STEP, t // DN_CHUNK) * DN_CHUNK


def _delta_fwd(qkv, ab, alog, dtb, onw, name):
    t = qkv.shape[0]
    rows = _delta_rows(t)
    per_step = rows // DN_CHUNK

    def body(q_ref, k_ref, v_ref, ab_ref, alog_ref, dtb_ref, onw_ref, y_ref, keep_ref, inv_ref, state):
        @pl.when(pl.program_id(0) == 0)
        def _():
            state[...] = jnp.zeros_like(state)

        s = [state[hh] for hh in range(DN_HEADS)]
        for c in range(per_step):
            r0 = c * DN_CHUNK
            (ys, s1), tmat = _delta_chunk(s, _head_cols(q_ref, r0), _head_cols(k_ref, r0), _head_cols(v_ref, r0),
                                          ab_ref[r0:r0 + DN_CHUNK, :], alog_ref[...], dtb_ref[...], onw_ref[...])
            for hh in range(DN_HEADS):
                keep_ref[c, hh] = s[hh]
                inv_ref[c, hh] = tmat[hh]
                y_ref[r0:r0 + DN_CHUNK, hh * HEAD:(hh + 1) * HEAD] = ys[hh].astype(ACT)
            s = s1
        for hh in range(DN_HEADS):
            state[hh] = s[hh]

    block = [pl.BlockSpec((rows, D), lambda i, c=c: (i, c)) for c in range(3)]
    nc = t // DN_CHUNK
    return _pcall(
        body, name=name, grid=(t // rows,),
        in_specs=block + [pl.BlockSpec((rows, HEAD), lambda i: (i, 0)), _full((1, HEAD)), _full((1, HEAD)),
                          _full((1, HEAD))],
        out_specs=[pl.BlockSpec((rows, D), lambda i: (i, 0)),
                   pl.BlockSpec((per_step, DN_HEADS, HEAD, HEAD), lambda i: (i, 0, 0, 0)),
                   pl.BlockSpec((per_step, DN_HEADS, DN_CHUNK, DN_CHUNK), lambda i: (i, 0, 0, 0))],
        out_shape=[jax.ShapeDtypeStruct((t, D), ACT), jax.ShapeDtypeStruct((nc, DN_HEADS, HEAD, HEAD), F32),
                   jax.ShapeDtypeStruct((nc, DN_HEADS, DN_CHUNK, DN_CHUNK), F32)],
        scratch_shapes=[pltpu.VMEM((DN_HEADS, HEAD, HEAD), F32)], compiler_params=_cp(1),
    )(qkv, qkv, qkv, ab, alog, dtb, onw)


def _delta_bwd(qkv, ab, alog, dtb, onw, keep, inv, dy, name):
    t = qkv.shape[0]
    rows = _delta_rows(t)
    per_step, n_steps = rows // DN_CHUNK, t // rows

    def body(q_ref, k_ref, v_ref, ab_ref, alog_ref, dtb_ref, onw_ref, keep_ref, inv_ref, dy_ref,
             dqkv_ref, dab_ref, dalog_ref, ddtb_ref, donw_ref, dstate):
        @pl.when(pl.program_id(0) == 0)
        def _():
            dstate[...] = jnp.zeros_like(dstate)
            dalog_ref[...] = jnp.zeros_like(dalog_ref)
            ddtb_ref[...] = jnp.zeros_like(ddtb_ref)
            donw_ref[...] = jnp.zeros_like(donw_ref)

        ds = [dstate[hh] for hh in range(DN_HEADS)]
        for c in reversed(range(per_step)):
            r0 = c * DN_CHUNK
            s0 = [keep_ref[c, hh] for hh in range(DN_HEADS)]
            known = [inv_ref[c, hh] for hh in range(DN_HEADS)]
            _, vjp, _ = jax.vjp(functools.partial(_delta_chunk, known_inverses=known), s0, _head_cols(q_ref, r0),
                                _head_cols(k_ref, r0), _head_cols(v_ref, r0), ab_ref[r0:r0 + DN_CHUNK, :], alog_ref[...],
                                dtb_ref[...], onw_ref[...], has_aux=True)
            ds, dq, dk, dv, dab, dal, ddt, don = vjp((_head_cols(dy_ref, r0), ds))
            for hh in range(DN_HEADS):
                lo = hh * HEAD
                dqkv_ref[r0:r0 + DN_CHUNK, lo:lo + HEAD] = dq[hh].astype(ACT)
                dqkv_ref[r0:r0 + DN_CHUNK, D + lo:D + lo + HEAD] = dk[hh].astype(ACT)
                dqkv_ref[r0:r0 + DN_CHUNK, 2 * D + lo:2 * D + lo + HEAD] = dv[hh].astype(ACT)
            dab_ref[r0:r0 + DN_CHUNK, :] = dab
            dalog_ref[...] += dal
            ddtb_ref[...] += ddt
            donw_ref[...] += don
        for hh in range(DN_HEADS):
            dstate[hh] = ds[hh]

    rev = lambda i: n_steps - 1 - i
    block = [pl.BlockSpec((rows, D), lambda i, c=c: (rev(i), c)) for c in range(3)]
    small = jax.ShapeDtypeStruct((1, HEAD), F32)
    return _pcall(
        body, name=name, grid=(n_steps,),
        in_specs=block + [pl.BlockSpec((rows, HEAD), lambda i: (rev(i), 0)), _full((1, HEAD)), _full((1, HEAD)),
                          _full((1, HEAD)), pl.BlockSpec((per_step, DN_HEADS, HEAD, HEAD), lambda i: (rev(i), 0, 0, 0)),
                          pl.BlockSpec((per_step, DN_HEADS, DN_CHUNK, DN_CHUNK), lambda i: (rev(i), 0, 0, 0)),
                          pl.BlockSpec((rows, D), lambda i: (rev(i), 0))],
        out_specs=[pl.BlockSpec((rows, 3 * D), lambda i: (rev(i), 0)),
                   pl.BlockSpec((rows, HEAD), lambda i: (rev(i), 0)), _full((1, HEAD)), _full((1, HEAD)),
                   _full((1, HEAD))],
        out_shape=[jax.ShapeDtypeStruct((t, 3 * D), ACT), jax.ShapeDtypeStruct((t, HEAD), F32), small, small, small],
        scratch_shapes=[pltpu.VMEM((DN_HEADS, HEAD, HEAD), F32)], compiler_params=_cp(1),
    )(qkv, qkv, qkv, ab, alog, dtb, onw, keep, inv, dy)


def _loss_head(y, target):
    t = y.shape[0]
    tm = min(512, t)

    def body(y_ref, t_ref, l_ref, dy_ref):
        @pl.when(pl.program_id(0) == 0)
        def _():
            l_ref[...] = jnp.zeros_like(l_ref)

        diff = y_ref[...] - t_ref[...]
        dy_ref[...] = diff * (1.0 / D)
        l_ref[...] += 0.5 * jnp.sum(jnp.sum(diff * diff, axis=-1, keepdims=True) * (1.0 / D), axis=0, keepdims=True)

    row = pl.BlockSpec((tm, D), lambda i: (i, 0))
    return _pcall(body, name="loss_head", grid=(t // tm,), in_specs=[row, row], out_specs=[_full((8, 128)), row],
                  out_shape=[jax.ShapeDtypeStruct((8, 128), F32), jax.ShapeDtypeStruct((t, D), F32)],
                  compiler_params=_cp(1))(y, target)


_SMALL_SHARDED = (("a_ln_w", (2, 128), 1), ("a_ln_b", (2, 128), 1), ("b_conv_w", (1, 3, 128), 2),
                  ("c_conv_w", (1, 4, 384), 2))
_REPLICATED = (("mem_norm_w", (1024,)), ("norm_pre", (4, 1024)), ("norm_post", (4, 1024)),
               ("a_b_s", (2, 8, 128)), ("c_a_log", (1, 8)), ("c_dt_bias", (1, 8)), ("c_o_norm_w", (1, 128)))


def _layer_shards(given, prefix):
    w_out = given[prefix + "w_out"]
    w_ins = [given[prefix + "a_w_in"][0], given[prefix + "b_w_in"], given[prefix + "c_w_in"], given[prefix + "a_w_in"][1]]
    return [[w_out[i], w_ins[i]] for i in range(4)]


def _as_matmul_operand(w):
    return w.reshape(w.shape[-2:]).astype(MXU)


def _as_rows(shape):
    return (math.prod(shape[:-1]) if len(shape) > 1 else 1, shape[-1])


def _row_table(shapes):
    table, at = [], 0
    for shape in shapes:
        rows, cols = _as_rows(shape)
        table.append((at, rows, cols))
        at += rows
    return table


def _pack_rows(arrays, shapes, rows, lead=()):
    n_lead = len(lead)
    parts = [jnp.pad(a.reshape(lead + _as_rows(s)), [(0, 0)] * (n_lead + 1) + [(0, D - s[-1])])
             for a, s in zip(arrays, shapes)]
    block = jnp.concatenate(parts, axis=n_lead)
    return jnp.pad(block, [(0, 0)] * n_lead + [(0, rows - block.shape[n_lead]), (0, 0)])


def _unpack_rows(packed, shapes, lead=()):
    n_lead = len(lead)
    return [packed[(slice(None),) * n_lead + (slice(r0, r0 + nr), slice(0, nc))].reshape(lead + s)
            for (r0, nr, nc), s in zip(_row_table(shapes), shapes)]


def _join_shards(blocks, axis):
    moved = jnp.moveaxis(blocks, 0, axis)
    shape = moved.shape
    return moved.reshape(shape[:axis] + (shape[axis] * shape[axis + 1],) + shape[axis + 2:])


def _split_shards(full, axis):
    shape = full.shape
    split = full.reshape(shape[:axis] + (N_DEV, shape[axis] // N_DEV) + shape[axis + 1:])
    return jnp.moveaxis(split, axis, 0)


_A_COLS = ((2560, 1536), (2048, 512), (0, 2048))
_BC_COLS = ((3584, 1536), (3072, 512), (0, 3072))
_C_COLS = ((3600, 1536), (3088, 512), (0, 3072))


def _reorder_cols(w, cols):
    return jnp.concatenate([w[:, s:s + n] for s, n in cols], axis=1)


def _restore_cols(pieces_in_my_order, cols, extra=()):
    placed = sorted(list(zip([s for s, _ in cols], pieces_in_my_order)) + list(extra), key=lambda sp: sp[0])
    return jnp.concatenate([piece for _, piece in placed], axis=1)


def kernel(x, mem, mem_norm_w, w_mem_kv, norm_pre, norm_post, w_out, a_w_in, a_ln_w, a_ln_b, a_w_s, a_b_s, b_w_in, b_conv_w, c_w_in, c_conv_w, c_a_log, c_dt_bias, c_o_norm_w, loss_target, m_mem_norm_w, m_w_mem_kv, m_norm_pre, m_norm_post, m_w_out, m_a_w_in, m_a_ln_w, m_a_ln_b, m_a_w_s, m_a_b_s, m_b_w_in, m_b_conv_w, m_c_w_in, m_c_conv_w, m_c_a_log, m_c_dt_bias, m_c_o_norm_w, v_mem_norm_w, v_w_mem_kv, v_norm_pre, v_norm_post, v_w_out, v_a_w_in, v_a_ln_w, v_a_ln_b, v_a_w_s, v_a_b_s, v_b_w_in, v_b_conv_w, v_c_w_in, v_c_conv_w, v_c_a_log, v_c_dt_bias, v_c_o_norm_w):
    given = dict(locals())
    x0 = x[0]
    mem0 = mem[0]
    target = loss_target[0]

    w_sh, m_sh, v_sh = (_layer_shards(given, pre) for pre in ("", "m_", "v_"))
    small_names = [n for n, _, _ in _SMALL_SHARDED]
    small_shapes = [s for _, s, _ in _SMALL_SHARDED]
    repl_names = [n for n, _ in _REPLICATED]
    repl_shapes = [s for _, s in _REPLICATED]
    small = _pack_rows([given[n] for n in small_names], small_shapes, R_SMALL)
    g_in0, = _all_gather([_as_matmul_operand(w_sh[0][1])], "gather_weights")
    w_in = [_reorder_cols(_join_shards(g_in0, 1), _A_COLS)]
    wouts, w_cab = [], None
    ws = [a_w_s[j] for j in range(2)]
    bst = [jnp.pad(a_b_s[j].T, ((0, 0), (0, HEAD - GM_GROUPS))) for j in range(2)]
    alog = jnp.pad(c_a_log, ((0, 0), (0, HEAD - DN_HEADS)))
    dtb = jnp.pad(c_dt_bias, ((0, 0), (0, HEAD - DN_HEADS)))
    onw = c_o_norm_w
    mw = mem_norm_w[None, :]

    xs, saved = [x0], []
    for i in range(4):
        kind = i % 3
        npre, npost = norm_pre[i][None, :], norm_post[i][None, :]
        ahead = [_as_matmul_operand(w) for w in w_sh[i + 1][:1 if i == 0 else 2]] if i < 3 else []
        if i == 0:
            ahead = [w_mem_kv.astype(MXU), _as_matmul_operand(w_sh[0][0]), small] + ahead
        res = _proj_fwd(xs[i], npre, w_in[i], w_cab if kind == 2 else None, f"proj_fwd_{i}", gather=ahead)
        if i == 0:
            g_wkv, g_wout0, g_small = res[2:5]
            small_full = _unpack_rows(g_small, small_shapes, lead=(N_DEV,))
            full = {n: _join_shards(blocks, ax) for (n, _, ax), blocks in zip(_SMALL_SHARDED, small_full)}
            wkv = g_wkv.reshape(D, 2 * D_XA)
            wouts.append(g_wout0.reshape(D_CAT, D))
            lnw = [full["a_ln_w"][j][None, :] for j in range(2)]
            lnb = [full["a_ln_b"][j][None, :] for j in range(2)]
            cw_b = jnp.pad(full["b_conv_w"][0], ((0, HALO - 3), (0, 0)))
            cw_c = jnp.pad(full["c_conv_w"][0], ((0, HALO - 4), (0, 0)))
            kv = _memkv_fwd(mem0, mw, wkv).astype(MXU)
        if ahead:
            wouts.append(res[-1 if i == 0 else -2].reshape(D_CAT, D))
            g_in = res[-1] if i > 0 else None
        if kind == 2:
            p, h, ab = res[:3]
            qkv = _qkv_fwd(p, cw_c, f"qkv_fwd_{i}")
            ymix, keep, inv = _delta_fwd(qkv, ab, alog, dtb, onw, f"delta_fwd_{i}")
            extra = (qkv, ab, keep, inv)
        else:
            p, h = res[:2]
            if kind == 0:
                ymix = _gmlp_fwd(p, lnw[i // 3], lnb[i // 3], ws[i // 3], bst[i // 3], f"gmlp_fwd_{i}")
            else:
                ymix = _sconv_fwd(p, cw_b, f"sconv_fwd_{i}")
            extra = ()
        res = _tail_fwd(p, ymix, xs[i], kv, wouts[i], npost, f"tail_fwd_{i}",
                        gather=[_as_matmul_operand(w_sh[1][1])] if i == 0 else [])
        xn, o = res[:2]
        if i == 0:
            g_in = res[2]
        if i + 1 == 2:
            c_full = jnp.concatenate([g_in[d] for d in range(N_DEV)], axis=1)
            w_in.append(_reorder_cols(c_full, _C_COLS))
            w_cab = jnp.pad(c_full[:, 3072:3088], ((0, 0), (0, HEAD - 16)))
        elif i < 3:
            w_in.append(_reorder_cols(_join_shards(g_in, 1), _BC_COLS if i + 1 == 1 else _A_COLS))
        xs.append(xn)
        saved.append((p, h, ymix, o, extra))

    loss_tile, dx = _loss_head(xs[4], target)
    loss = lax.psum(loss_tile[0, 0], ("x", "y", "c"))

    g = {}
    d_npre, d_npost = [None] * 4, [None] * 4
    d_ws, d_bs, d_lnw, d_lnb = [None] * 2, [None] * 2, [None] * 2, [None] * 2
    dkv = None
    pend_win, pend_wout = None, None
    landed_win, landed_wout = [None] * 4, [None] * 4
    for i in reversed(range(4)):
        kind = i % 3
        p, h, ymix, o, extra = saved[i]
        npre, npost = norm_pre[i][None, :], norm_post[i][None, :]
        res = _tail_bwd(dx, o, p, ymix, kv, wouts[i], npost, f"tail_bwd_{i}", exchange=[pend_win] if i < 3 else [])
        dzq, dymix, d_wout, d_npost[i], dkv_i = res[:5]
        if i < 3:
            landed_win[i + 1] = res[5]
        d_wout = d_wout.reshape(N_DEV, D_CAT // N_DEV, D)
        dkv = dkv_i if dkv is None else dkv + dkv_i
        w_zq, w_mix = w_in[i][:, :ZQ], w_in[i][:, ZQ:]
        dw_zq = _matmul_tn(h, dzq, f"dw_zq_{i}")
        if kind == 0:
            j = i // 3
            early_x, early_b = [], []
            if i == 0:
                d_wkv, d_mw = _memkv_bwd(mem0, mw, wkv, dkv)
                early_x = [d_wkv.reshape(N_DEV, 128, D).astype(XCH), d_wout, pend_wout]
                early_b = [d_ws[1].reshape(GM_CHUNK, D).astype(XCH)]
            (dmix, d_lnw[j], d_lnb[j], d_ws[j], dbst), early_landed = _gmlp_bwd(
                p, dymix, lnw[j], lnb[j], ws[j], bst[j], f"gmlp_bwd_{i}", exchange=early_x, bcast=early_b)
            d_bs[j] = dbst[:, :GM_GROUPS].T
            if i == 0:
                g["mem_norm_w"] = d_mw[0]
                g["norm_pre"] = jnp.concatenate([jnp.zeros((1, D), F32)] + d_npre[1:], axis=0)
                g["norm_post"] = jnp.concatenate(d_npost, axis=0)
                g["a_ln_w"] = jnp.concatenate(d_lnw, axis=0)
                g["a_ln_b"] = jnp.concatenate(d_lnb, axis=0)
                g["a_b_s"] = jnp.stack(d_bs)
                e_small = _pack_rows([_split_shards(g[n], ax) for n, _, ax in _SMALL_SHARDED], small_shapes, R_SMALL,
                                     lead=(N_DEV,))
                r_pack = _pack_rows([g[n] for n in repl_names], repl_shapes, R_REPL)
                dw_mix, (l_small, ws0_all, r_all) = _matmul_tn(
                    h, dmix, f"dw_mix_{i}", exchange=[e_small], bcast=[d_ws[0].reshape(GM_CHUNK, D).astype(XCH), r_pack])
            else:
                dw_mix = _matmul_tn(h, dmix, f"dw_mix_{i}")
            d_win = _restore_cols([dw_zq[:, :D_CAT], dw_zq[:, D_CAT:], dw_mix], _A_COLS)
            dps, wparts = [dzq, dmix], [w_zq, w_mix]
        elif kind == 1:
            dmix, dcw = _sconv_bwd(p, dymix, cw_b, f"sconv_bwd_{i}")
            g["b_conv_w"] = dcw[None, :3]
            dw_mix = _matmul_tn(h, dmix, f"dw_mix_{i}")
            d_win = _restore_cols([dw_zq[:, :D_CAT], dw_zq[:, D_CAT:], dw_mix], _BC_COLS)
            dps, wparts = [dzq, dmix], [w_zq, w_mix]
        else:
            qkv, ab, keep, inv = extra
            dqkv, dab, dalog, ddtb, donw = _delta_bwd(qkv, ab, alog, dtb, onw, keep, inv, dymix, f"delta_bwd_{i}")
            dmix, dcw = _qkv_bwd(p, dqkv, cw_c, f"qkv_bwd_{i}")
            g["c_conv_w"] = dcw[None, :4]
            g["c_a_log"], g["c_dt_bias"], g["c_o_norm_w"] = dalog[:, :DN_HEADS], ddtb[:, :DN_HEADS], donw
            dw_mix = _matmul_tn(h, dmix, f"dw_mix_{i}")
            dw_ab = _matmul_tn(h, dab, f"dw_ab_{i}")
            d_win = _restore_cols([dw_zq[:, :D_CAT], dw_zq[:, D_CAT:], dw_mix], _C_COLS, extra=[(3072, dw_ab[:, :16])])
            dps, wparts = [dzq, dmix, dab], [w_zq, w_mix, w_cab]
        width = d_win.shape[1] // N_DEV
        pend_win = jnp.stack([d_win[:, d * width:(d + 1) * width] for d in range(N_DEV)]).astype(XCH)
        if i > 0:
            res = _proj_bwd_x(dps, wparts, xs[i], dx, npre, f"proj_bwd_x_{i}", exchange=[pend_wout] if i < 3 else [])
            dx, d_npre[i] = res[:2]
            if i < 3:
                landed_wout[i + 1] = res[2]
            pend_wout = d_wout

    dx, d_npre0, l_in = _proj_bwd_x(dps, wparts, xs[0], dx, norm_pre[0][None, :], "proj_bwd_x_0", exchange=[pend_win])
    l_wkv, landed_wout[0], landed_wout[1], ws3_all = early_landed
    landed_win[0] = l_in
    npre0_all, = _all_gather([jnp.pad(d_npre0, ((0, HALO - 1), (0, 0)))], "gather_norm_pre0")
    r_all = r_all.at[:, 1, :].set(npre0_all[:, 0, :])

    res = [[_reduce_adamw(parts, w_sh[i][a], m_sh[i][a], v_sh[i][a], f"adamw_{i}_{a}")
            for a, parts in enumerate((landed_wout[i], landed_win[i]))] for i in range(4)]
    res_wkv = _reduce_adamw(l_wkv, w_mem_kv, m_w_mem_kv, v_w_mem_kv, "adamw_w_mem_kv")
    res_small = _reduce_adamw_vectors(
        l_small, small, *(_pack_rows([given[pre + n] for n in small_names], small_shapes, R_SMALL) for pre in ("m_", "v_")),
        small_shapes, "adamw_small")
    res_ws = [_reduce_adamw(parts, *(given[pre + "a_w_s"][j].reshape(GM_CHUNK, D) for pre in ("", "m_", "v_")),
                            f"adamw_a_w_s_{j}") for j, parts in enumerate((ws0_all, ws3_all))]
    res_repl = _reduce_adamw_vectors(
        r_all, *(_pack_rows([given[pre + n] for n in repl_names], repl_shapes, R_REPL) for pre in ("", "m_", "v_")),
        repl_shapes, "adamw_replicated")

    order = ["mem_norm_w", "w_mem_kv", "norm_pre", "norm_post", "w_out", "a_w_in", "a_ln_w", "a_ln_b", "a_w_s", "a_b_s",
             "b_w_in", "b_conv_w", "c_w_in", "c_conv_w", "c_a_log", "c_dt_bias", "c_o_norm_w"]
    outs = [loss, dx[None]]
    for kind in range(4):
        got = dict(zip(repl_names, res_repl[kind]))
        got["a_w_s"] = jnp.stack([r[kind] for r in res_ws]).reshape(a_w_s.shape)
        got.update(zip(small_names, res_small[kind]))
        got["w_mem_kv"] = res_wkv[kind]
        got["w_out"] = jnp.stack([res[i][0][kind] for i in range(4)])
        got["a_w_in"] = jnp.stack([res[0][1][kind], res[3][1][kind]])
        got["b_w_in"] = res[1][1][kind]
        got["c_w_in"] = res[2][1][kind]
        outs += [got[n] for n in order]
    return tuple(outs)
```

```python
import functools
import math

import jax
import jax.numpy as jnp
from jax import lax
from jax.experimental import pallas as pl
from jax.experimental.pallas import tpu as pltpu

F32 = jnp.float32
MXU = jnp.bfloat16
ACT = jnp.bfloat16

D = 1024
D_XA = 512
D_CAT = 1536
N_MEM = 256
XA_HEADS = 4
HEAD = 128
ZQ = D_CAT + D_XA
EPS = 1e-6
GM_CHUNK = 128
GM_CHUNKS_PER_STEP = 8
GM_GROUPS = 8
DN_HEADS = 8
DN_CHUNK = 64
DN_CHUNKS_PER_STEP = 8
N_DEV = 8
HALO = 8
VMEM_LIMIT = 56 * 1024 * 1024
XCH = jnp.bfloat16
R_REPL = 32
R_SMALL = 16

ADAM_LR = 0.001
ADAM_B1 = 0.9
ADAM_B2 = 0.999
ADAM_EPS = 1e-08
ADAM_WD = 0.01
ADAM_STEP = 10

NN = ((1,), (0,))
NT = ((1,), (1,))
TN = ((0,), (0,))
MESH = pl.DeviceIdType.MESH


def _pcall(body, **kw):
    return pl.pallas_call(body, **kw)


def _cp(n_axes):
    return pltpu.CompilerParams(dimension_semantics=("arbitrary",) * n_axes, vmem_limit_bytes=VMEM_LIMIT)


def _dot(a, b, dims, prec=None):
    return lax.dot_general(a, b, (dims, ((), ())), preferred_element_type=F32, precision=prec)


def _mdot(a, b, dims):
    return _dot(a.astype(MXU), b.astype(MXU), dims)


def _make_mms(raw):
    @jax.custom_vjp
    def nn(a, b):
        return raw(a, b, NN)

    @jax.custom_vjp
    def nt(a, b):
        return raw(a, b, NT)

    @jax.custom_vjp
    def tn(a, b):
        return raw(a, b, TN)

    nn.defvjp(lambda a, b: (nn(a, b), (a, b)), lambda r, g: (nt(g, r[1]), tn(r[0], g)))
    nt.defvjp(lambda a, b: (nt(a, b), (a, b)), lambda r, g: (nn(g, r[1]), tn(g, r[0])))
    tn.defvjp(lambda a, b: (tn(a, b), (a, b)), lambda r, g: (nt(r[1], g), nn(r[0], g)))
    return nn, nt, tn


_nn, _nt, _tn = _make_mms(_mdot)


def _split_dot(a, b, dims):
    ah = a.astype(jnp.bfloat16)
    bh = b.astype(jnp.bfloat16)
    al = (a - ah.astype(F32)).astype(jnp.bfloat16)
    bl = (b - bh.astype(F32)).astype(jnp.bfloat16)
    return _dot(ah, bh, dims) + (_dot(ah, bl, dims) + _dot(al, bh, dims))


_hnn, _hnt, _htn = _make_mms(_split_dot)


def _full(shape):
    return pl.BlockSpec(shape, lambda *_: (0,) * len(shape))


def _silu(z):
    return z * jax.nn.sigmoid(z)


def _dsilu(z):
    s = jax.nn.sigmoid(z)
    return s * (1.0 + z * (1.0 - s))


def _onehot_row(n, k):
    return (lax.broadcasted_iota(jnp.int32, (1, n), 1) == k).astype(F32)


_HBM = pl.BlockSpec(memory_space=pl.ANY)


def _sem_shapes(n_remote, n_local):
    return [pltpu.SemaphoreType.DMA((n_remote,)), pltpu.SemaphoreType.DMA((n_remote,)),
            pltpu.SemaphoreType.DMA((n_local,))]


def _gather_parts(x_refs, out_refs, send_sems, recv_sems, local_sems):
    n = len(x_refs)
    x, y, cc = lax.axis_index("x"), lax.axis_index("y"), lax.axis_index("c")
    me, sibling = (x, y, cc), (x, y, 1 - cc)
    chips = [(1 - x, y), (x, 1 - y), (1 - x, 1 - y)]

    def slot(a, px, py, pc):
        return out_refs[a].at[4 * px + 2 * py + pc]

    def copy(k, a, block, to, src=None):
        return pltpu.make_async_remote_copy(
            src_ref=slot(a, *block) if src is None else src, dst_ref=slot(a, *block),
            send_sem=send_sems.at[k * n + a], recv_sem=recv_sems.at[k * n + a], device_id=to, device_id_type=MESH)

    mine = [pltpu.make_async_copy(x_refs[a], slot(a, *me), local_sems.at[a]) for a in range(n)]
    first = [copy(0, a, me, sibling, src=x_refs[a]) for a in range(n)]
    first += [copy(1 + j, a, me, (*chip, cc), src=x_refs[a]) for j, chip in enumerate(chips) for a in range(n)]

    def begin():
        for cp in mine + first:
            cp.start()

    def end():
        passed = []
        for j, chip in enumerate(chips):
            for a in range(n):
                copy(1 + j, a, (*chip, cc), me).wait_recv()
                passed.append(copy(4 + j, a, (*chip, cc), sibling))
                passed[-1].start()
        for a in range(n):
            copy(0, a, sibling, me).wait_recv()
        for j, chip in enumerate(chips):
            for a in range(n):
                copy(4 + j, a, (*chip, 1 - cc), me).wait_recv()
        for cp in first + passed:
            cp.wait_send()
        for cp in mine:
            cp.wait()

    return begin, end


def _exchange_parts(g_refs, out_refs, b_refs, ball_refs, send_sems, recv_sems, local_sems):
    n, nb = len(g_refs), len(b_refs)
    per_peer = n + nb
    x, y, cc = lax.axis_index("x"), lax.axis_index("y"), lax.axis_index("c")
    my_idx = 4 * x + 2 * y + cc
    mine = [pltpu.make_async_copy(g_refs[a].at[my_idx], out_refs[a].at[my_idx], local_sems.at[a]) for a in range(n)]
    mine += [pltpu.make_async_copy(b_refs[a], ball_refs[a].at[my_idx], local_sems.at[n + a]) for a in range(nb)]
    copies = []
    for k in range(1, N_DEV):
        px = 1 - x if k & 4 else x
        py = 1 - y if k & 2 else y
        pc = 1 - cc if k & 1 else cc
        base = (k - 1) * per_peer
        for a in range(n):
            copies.append(pltpu.make_async_remote_copy(
                src_ref=g_refs[a].at[4 * px + 2 * py + pc], dst_ref=out_refs[a].at[my_idx],
                send_sem=send_sems.at[base + a], recv_sem=recv_sems.at[base + a],
                device_id=(px, py, pc), device_id_type=MESH))
        for a in range(nb):
            copies.append(pltpu.make_async_remote_copy(
                src_ref=b_refs[a], dst_ref=ball_refs[a].at[my_idx], send_sem=send_sems.at[base + n + a],
                recv_sem=recv_sems.at[base + n + a], device_id=(px, py, pc), device_id_type=MESH))

    def begin():
        for cp in mine + copies:
            cp.start()

    def end():
        for cp in copies:
            cp.wait_recv()
        for cp in copies:
            cp.wait_send()
        for cp in mine:
            cp.wait()

    return begin, end


def _pcall_hosting(body, args, exchange, bcast, *, name, grid, in_specs, out_specs, out_shape, compiler_params,
                   scratch_shapes=(), gather=()):
    n_in, n_out, n_x, n_b, n_g = len(in_specs), len(out_specs), len(exchange), len(bcast), len(gather)
    n_d = n_x + n_b
    n_c, n_scr = n_d + n_g, len(scratch_shapes)

    def hosted(*refs):
        ins, c_in = refs[:n_in], refs[n_in:n_in + n_c]
        outs, c_out = refs[n_in + n_c:n_in + n_c + n_out], refs[n_in + n_c + n_out:n_in + 2 * n_c + n_out]
        scratch, sems = refs[n_in + 2 * n_c + n_out:][:n_scr], refs[n_in + 2 * n_c + n_out + n_scr:]
        parts = []
        if n_d:
            parts.append(_exchange_parts(c_in[:n_x], c_out[:n_x], c_in[n_x:n_d], c_out[n_x:n_d], *sems[:3]))
        if n_g:
            parts.append(_gather_parts(c_in[n_d:], c_out[n_d:], *sems[-3:]))
        first = functools.reduce(jnp.logical_and, [pl.program_id(k) == 0 for k in range(len(grid))])
        last = functools.reduce(jnp.logical_and, [pl.program_id(k) == grid[k] - 1 for k in range(len(grid))])
        for begin, _ in parts:
            pl.when(first)(begin)
        body(*ins, *outs, *scratch)
        for _, end in parts:
            pl.when(last)(end)

    landed_shape = [jax.ShapeDtypeStruct(e.shape, e.dtype) for e in exchange]
    landed_shape += [jax.ShapeDtypeStruct((N_DEV,) + b.shape, b.dtype) for b in list(bcast) + list(gather)]
    sems = (_sem_shapes(7 * n_d, n_d) if n_d else []) + (_sem_shapes(7 * n_g, n_g) if n_g else [])
    res = _pcall(hosted, name=name, grid=grid, in_specs=list(in_specs) + [_HBM] * n_c,
                 out_specs=list(out_specs) + [_HBM] * n_c, out_shape=list(out_shape) + landed_shape,
                 scratch_shapes=list(scratch_shapes) + sems,
                 compiler_params=compiler_params)(*args, *exchange, *bcast, *gather)
    return res[:n_out], res[n_out:]


def _all_gather(blks, name):
    n = len(blks)

    def body(*refs):
        begin, end = _gather_parts(refs[:n], refs[n:2 * n], *refs[2 * n:])
        begin()
        end()

    return _pcall(body, name=name, out_shape=[jax.ShapeDtypeStruct((N_DEV,) + b.shape, b.dtype) for b in blks],
                  in_specs=[_HBM] * n, out_specs=[_HBM] * n, scratch_shapes=_sem_shapes(7 * n, n))(*blks)


def _sum_and_adamw(p_ref, w, m, v):
    g = p_ref[0].astype(F32)
    for s in range(1, N_DEV):
        g = g + p_ref[s].astype(F32)
    nm = ADAM_B1 * m + (1.0 - ADAM_B1) * g
    nv = ADAM_B2 * v + (1.0 - ADAM_B2) * (g * g)
    m_hat = nm / (1.0 - ADAM_B1 ** ADAM_STEP)
    v_hat = nv / (1.0 - ADAM_B2 ** ADAM_STEP)
    return g, -ADAM_LR * (m_hat / (jnp.sqrt(v_hat) + ADAM_EPS) + ADAM_WD * w), nm, nv


def _reduce_adamw_vectors(parts, w, m, v, shapes, name):
    table = _row_table(shapes)

    def body(p_ref, w_ref, m_ref, v_ref, *out_refs):
        results = _sum_and_adamw(p_ref, w_ref[...], m_ref[...], v_ref[...])
        for kind, val in enumerate(results):
            for t, (r0, nr, nc) in enumerate(table):
                out_refs[kind * len(table) + t][...] = val[r0:r0 + nr, 0:nc]

    outs = _pcall(body, name=name, out_shape=[jax.ShapeDtypeStruct((nr, nc), F32) for _, nr, nc in table] * 4)(parts, w, m, v)
    return [[outs[kind * len(table) + t].reshape(s) for t, s in enumerate(shapes)] for kind in range(4)]


def _reduce_adamw(parts, w, m, v, name):
    lead = w.shape[:-2]
    r, c = w.shape[-2:]
    tr = 128 if r % 128 == 0 else r
    zeros = (0,) * len(lead)
    at = zeros + (slice(None), slice(None))

    def body(p_ref, w_ref, m_ref, v_ref, g_ref, d_ref, nm_ref, nv_ref):
        g_ref[at], d_ref[at], nm_ref[at], nv_ref[at] = _sum_and_adamw(p_ref, w_ref[at], m_ref[at], v_ref[at])

    row = pl.BlockSpec((1,) * len(lead) + (tr, c), lambda i: zeros + (i, 0))
    out = jax.ShapeDtypeStruct(w.shape, F32)
    return _pcall(
        body, name=name, grid=(r // tr,),
        in_specs=[pl.BlockSpec((N_DEV, tr, c), lambda i: (0, i, 0)), row, row, row],
        out_specs=[row, row, row, row], out_shape=[out, out, out, out], compiler_params=_cp(1),
    )(parts, w, m, v)


def _proj_fwd(x, nw, w, wab, name, gather=()):
    t, npj = x.shape[0], w.shape[1]
    tm, tn = min(512, t), 1024
    has_ab = wab is not None
    n_in, n_out, n_g = 3 + has_ab, 2 + has_ab, len(gather)
    n_i = t // tm

    def body(*refs):
        ins, g_ins = refs[:n_in], refs[n_in:n_in + n_g]
        outs = refs[n_in + n_g:n_in + n_g + n_out]
        g_outs = refs[n_in + n_g + n_out:n_in + 2 * n_g + n_out]
        if has_ab:
            (x_ref, nw_ref, w_ref, wab_ref), (p_ref, h_ref, ab_ref) = ins, outs
        else:
            (x_ref, nw_ref, w_ref), (p_ref, h_ref) = ins, outs
        if n_g:
            begin, end = _gather_parts(g_ins, g_outs, *refs[n_in + 2 * n_g + n_out:])
            pl.when(pl.program_id(0) == 0)(begin)

        for rows in (slice(0, tm // 2), slice(tm // 2, tm)):
            xv = x_ref[rows, :]
            hv = (xv * lax.rsqrt(jnp.mean(xv * xv, axis=-1, keepdims=True) + EPS) * nw_ref[...]).astype(MXU)
            h_ref[rows, :] = hv.astype(ACT)
            if has_ab:
                ab_ref[rows, :] = _dot(hv, wab_ref[...], NN)
            for j in range(npj // tn):
                p_ref[rows, j * tn:(j + 1) * tn] = _dot(hv, w_ref[:, j * tn:(j + 1) * tn], NN).astype(ACT)
        if n_g:
            pl.when(pl.program_id(0) == n_i - 1)(end)

    in_specs = [pl.BlockSpec((tm, D), lambda i: (i, 0)), _full((1, D)), _full((D, npj))]
    out_specs = [pl.BlockSpec((tm, npj), lambda i: (i, 0)), pl.BlockSpec((tm, D), lambda i: (i, 0))]
    out_shape = [jax.ShapeDtypeStruct((t, npj), ACT), jax.ShapeDtypeStruct((t, D), ACT)]
    args = [x, nw, w]
    if has_ab:
        in_specs.append(_full((D, HEAD)))
        out_specs.append(pl.BlockSpec((tm, HEAD), lambda i: (i, 0)))
        out_shape.append(jax.ShapeDtypeStruct((t, HEAD), F32))
        args.append(wab)
    scratch = []
    if n_g:
        in_specs += [_HBM] * n_g
        out_specs += [_HBM] * n_g
        out_shape += [jax.ShapeDtypeStruct((N_DEV,) + b.shape, b.dtype) for b in gather]
        args += list(gather)
        scratch += _sem_shapes(7 * n_g, n_g)
    return _pcall(body, name=name, grid=(n_i,), in_specs=in_specs, out_specs=out_specs,
                  out_shape=out_shape, scratch_shapes=scratch, compiler_params=_cp(1))(*args)


def _proj_bwd_x(dps, ws, x, dxn, nw, name, exchange=(), bcast=()):
    t = x.shape[0]
    tm = min(512, t)
    n, n_x, n_b = len(dps), len(exchange), len(bcast)
    n_c, n_steps = n_x + n_b, t // tm

    def body(*refs):
        dp_refs, w_refs = refs[:n], refs[n:2 * n]
        x_ref, dxn_ref, nw_ref = refs[2 * n:2 * n + 3]
        c_in = refs[2 * n + 3:2 * n + 3 + n_c]
        dx_ref, dnw_ref = refs[2 * n + 3 + n_c:2 * n + 5 + n_c]
        c_out = refs[2 * n + 5 + n_c:2 * n + 5 + 2 * n_c]
        if n_c:
            begin, end = _exchange_parts(c_in[:n_x], c_out[:n_x], c_in[n_x:], c_out[n_x:], *refs[2 * n + 5 + 2 * n_c:])
            pl.when(pl.program_id(0) == 0)(begin)
        @pl.when(pl.program_id(0) == 0)
        def _():
            dnw_ref[...] = jnp.zeros_like(dnw_ref)

        halves = [slice(r0, r0 + tm // 2) for r0 in (0, tm // 2)]
        dhs = []
        for rows in halves:
            dh = _mdot(dp_refs[0][rows, :], w_refs[0][...], NT)
            for k in range(1, n):
                dh = dh + _mdot(dp_refs[k][rows, :], w_refs[k][...], NT)
            dhs.append(dh)
        for rows, dh in zip(halves, dhs):
            xv = x_ref[rows, :]
            r = lax.rsqrt(jnp.mean(xv * xv, axis=-1, keepdims=True) + EPS)
            dnw_ref[...] += jnp.sum(dh * xv * r, axis=0, keepdims=True)
            dhw = dh * nw_ref[...]
            dx_ref[rows, :] = dxn_ref[rows, :] + r * (dhw - xv * (r * r) * jnp.mean(dhw * xv, axis=-1, keepdims=True))
        if n_c:
            pl.when(pl.program_id(0) == n_steps - 1)(end)

    row = pl.BlockSpec((tm, D), lambda i: (i, 0))
    in_specs = [pl.BlockSpec((tm, dp.shape[1]), lambda i: (i, 0)) for dp in dps]
    in_specs += [_full(w.shape) for w in ws]
    in_specs += [row, row, _full((1, D))] + [_HBM] * n_c
    out_shape = [jax.ShapeDtypeStruct((t, D), F32), jax.ShapeDtypeStruct((1, D), F32)]
    out_shape += [jax.ShapeDtypeStruct(e.shape, e.dtype) for e in exchange]
    out_shape += [jax.ShapeDtypeStruct((N_DEV,) + b.shape, b.dtype) for b in bcast]
    return _pcall(body, name=name, grid=(n_steps,), in_specs=in_specs, out_specs=[row, _full((1, D))] + [_HBM] * n_c,
                  out_shape=out_shape, scratch_shapes=_sem_shapes(7 * n_c, n_c) if n_c else [],
                  compiler_params=_cp(1))(*dps, *ws, x, dxn, nw, *exchange, *bcast)


def _matmul_tn(a, b, name, out_dtype=XCH, exchange=(), bcast=()):
    t, m = a.shape
    n = b.shape[1]
    tm, tn = min(1024, t), min(1024, n)
    n_t = t // tm

    def body(a_ref, b_ref, o_ref, acc):
        @pl.when(pl.program_id(1) == 0)
        def _():
            acc[...] = jnp.zeros_like(acc)

        acc[...] += _mdot(a_ref[...], b_ref[...], TN)

        @pl.when(pl.program_id(1) == n_t - 1)
        def _():
            o_ref[...] = acc[...].astype(out_dtype)

    call = dict(name=name, grid=(n // tn, n_t),
                in_specs=[pl.BlockSpec((tm, m), lambda j, i: (i, 0)), pl.BlockSpec((tm, tn), lambda j, i: (i, j))],
                out_specs=[pl.BlockSpec((m, tn), lambda j, i: (0, j))],
                out_shape=[jax.ShapeDtypeStruct((m, n), out_dtype)], scratch_shapes=[pltpu.VMEM((m, tn), F32)],
                compiler_params=_cp(2))
    if not exchange and not bcast:
        return _pcall(body, **call)(a, b)[0]
    (out,), landed = _pcall_hosting(body, (a, b), exchange, bcast, **call)
    return out, landed


def _memkv_fwd(mem, mw, wkv):
    def body(mem_ref, mw_ref, w_ref, kv_ref):
        mv = mem_ref[...]
        mn = mv * lax.rsqrt(jnp.mean(mv * mv, axis=-1, keepdims=True) + EPS) * mw_ref[...]
        kv_ref[...] = _mdot(mn, w_ref[...], NN)

    return _pcall(body, name="memkv_fwd", out_shape=jax.ShapeDtypeStruct((N_MEM, 2 * D_XA), F32),
                  compiler_params=pltpu.CompilerParams(vmem_limit_bytes=VMEM_LIMIT))(mem, mw, wkv)


def _memkv_bwd(mem, mw, wkv, dkv):
    def body(mem_ref, mw_ref, w_ref, dkv_ref, dw_ref, dmw_ref):
        mv = mem_ref[...]
        r = lax.rsqrt(jnp.mean(mv * mv, axis=-1, keepdims=True) + EPS)
        mn = mv * r * mw_ref[...]
        dkvv = dkv_ref[...]
        dw_ref[...] = _mdot(mn, dkvv, TN)
        dmn = _mdot(dkvv, w_ref[...], NT)
        dmw_ref[...] = jnp.sum(dmn * mv * r, axis=0, keepdims=True)

    return _pcall(body, name="memkv_bwd",
                  out_shape=[jax.ShapeDtypeStruct((D, 2 * D_XA), F32), jax.ShapeDtypeStruct((1, D), F32)],
                  compiler_params=pltpu.CompilerParams(vmem_limit_bytes=VMEM_LIMIT))(mem, mw, wkv, dkv)


def _attend(q, kv):
    heads = range(XA_HEADS)
    qs = [q[:, h * HEAD:(h + 1) * HEAD] for h in heads]
    ks = [kv[:, h * HEAD:(h + 1) * HEAD] for h in heads]
    vs = [kv[:, D_XA + h * HEAD:D_XA + (h + 1) * HEAD] for h in heads]
    ss = [_mdot(qs[h], ks[h], NT) * (HEAD ** -0.5) for h in heads]
    es = [jnp.exp(s - jnp.max(s, axis=-1, keepdims=True)) for s in ss]
    ps = [e / jnp.sum(e, axis=-1, keepdims=True) for e in es]
    return ps, [_mdot(ps[h], vs[h], NN) for h in heads]


def _tail_fwd(p, ymix, x, kv, wout, npost, name, gather=()):
    t = x.shape[0]
    tm = min(512, t)

    def body(z_ref, q_ref, y_ref, x_ref, kv_ref, w_ref, np_ref, xn_ref, o_ref):
        _, outs = _attend(q_ref[...], kv_ref[...])
        cat = jnp.concatenate([y_ref[...]] + [a.astype(ACT) for a in outs], axis=1)
        g = cat * _silu(z_ref[...])
        o = _mdot(g, w_ref[...], NN)
        o_ref[...] = o.astype(ACT)
        xn_ref[...] = x_ref[...] + o * lax.rsqrt(jnp.mean(o * o, axis=-1, keepdims=True) + EPS) * np_ref[...]

    row = pl.BlockSpec((tm, D), lambda i: (i, 0))
    call = dict(
        name=name, grid=(t // tm,),
        in_specs=[pl.BlockSpec((tm, D_CAT), lambda i: (i, 0)), pl.BlockSpec((tm, D_XA), lambda i: (i, D_CAT // D_XA)),
                  row, row, _full((N_MEM, 2 * D_XA)), _full((D_CAT, D)), _full((1, D))],
        out_specs=[row, row],
        out_shape=[jax.ShapeDtypeStruct((t, D), F32), jax.ShapeDtypeStruct((t, D), ACT)], compiler_params=_cp(1))
    args = (p, p, ymix, x, kv, wout, npost)
    if not gather:
        return tuple(_pcall(body, **call)(*args))
    outs, landed = _pcall_hosting(body, args, (), (), gather=gather, **call)
    return tuple(outs) + tuple(landed)


def _tail_bwd(dxn, o, p, ymix, kv, wout, npost, name, exchange=()):
    t = dxn.shape[0]
    tm = min(512, t)
    n_x, n_steps = len(exchange), t // tm

    def body(*refs):
        dxn_ref, o_ref, z_ref, q_ref, y_ref, kv_ref, w_ref, np_ref = refs[:8]
        dzq_ref, dy_ref, dw_ref, dnp_ref, dkv_ref = refs[8 + n_x:13 + n_x]
        dw_acc = refs[13 + 2 * n_x]
        if n_x:
            begin, end = _exchange_parts(refs[8:8 + n_x], refs[13 + n_x:13 + 2 * n_x], (), (), *refs[14 + 2 * n_x:])
            pl.when(pl.program_id(0) == 0)(begin)

        @pl.when(pl.program_id(0) == 0)
        def _():
            dw_acc[...] = jnp.zeros_like(dw_acc)
            dnp_ref[...] = jnp.zeros_like(dnp_ref)
            dkv_ref[...] = jnp.zeros_like(dkv_ref)

        q = q_ref[...]
        kvv = kv_ref[...]
        z = z_ref[...]
        ps, outs = _attend(q, kvv)
        cat = jnp.concatenate([y_ref[...]] + [a.astype(ACT) for a in outs], axis=1)
        sz = _silu(z)
        g = cat * sz
        ov = o_ref[...].astype(F32)
        dr = dxn_ref[...]
        rr = lax.rsqrt(jnp.mean(ov * ov, axis=-1, keepdims=True) + EPS)
        dnp_ref[...] += jnp.sum(dr * ov * rr, axis=0, keepdims=True)
        dow = dr * np_ref[...]
        do = rr * (dow - ov * (rr * rr) * jnp.mean(dow * ov, axis=-1, keepdims=True))
        dg = _mdot(do, w_ref[...], NT).astype(ACT)
        dw_acc[...] += _mdot(g, do, TN)
        dcat = dg * sz
        dzq_ref[:, 0:D_CAT] = dg * cat * _dsilu(z)
        dy_ref[...] = dcat[:, 0:D]
        heads = range(XA_HEADS)
        dohs = [dcat[:, D + h * HEAD:D + (h + 1) * HEAD] for h in heads]
        dps = [_mdot(dohs[h], kvv[:, D_XA + h * HEAD:D_XA + (h + 1) * HEAD], NT) for h in heads]
        dss = [ps[h] * (dps[h] - jnp.sum(dps[h] * ps[h], axis=-1, keepdims=True)) for h in heads]
        dqs = [_mdot(dss[h], kvv[:, h * HEAD:(h + 1) * HEAD], NN) * (HEAD ** -0.5) for h in heads]
        dks = [_mdot(dss[h], q[:, h * HEAD:(h + 1) * HEAD], TN) * (HEAD ** -0.5) for h in heads]
        dvs = [_mdot(ps[h], dohs[h], TN) for h in heads]
        for h in heads:
            lo = h * HEAD
            dzq_ref[:, D_CAT + lo:D_CAT + lo + HEAD] = dqs[h].astype(ACT)
            dkv_ref[:, lo:lo + HEAD] += dks[h]
            dkv_ref[:, D_XA + lo:D_XA + lo + HEAD] += dvs[h]

        @pl.when(pl.program_id(0) == n_steps - 1)
        def _():
            dw_ref[...] = dw_acc[...].astype(XCH)

        if n_x:
            pl.when(pl.program_id(0) == n_steps - 1)(end)

    row = pl.BlockSpec((tm, D), lambda i: (i, 0))
    return _pcall(
        body, name=name, grid=(n_steps,),
        in_specs=[row, row, pl.BlockSpec((tm, D_CAT), lambda i: (i, 0)),
                  pl.BlockSpec((tm, D_XA), lambda i: (i, D_CAT // D_XA)), row,
                  _full((N_MEM, 2 * D_XA)), _full((D_CAT, D)), _full((1, D))] + [_HBM] * n_x,
        out_specs=[pl.BlockSpec((tm, ZQ), lambda i: (i, 0)), row, _full((D_CAT, D)), _full((1, D)),
                   _full((N_MEM, 2 * D_XA))] + [_HBM] * n_x,
        out_shape=[jax.ShapeDtypeStruct((t, ZQ), ACT), jax.ShapeDtypeStruct((t, D), ACT),
                   jax.ShapeDtypeStruct((D_CAT, D), XCH), jax.ShapeDtypeStruct((1, D), F32),
                   jax.ShapeDtypeStruct((N_MEM, 2 * D_XA), F32)]
        + [jax.ShapeDtypeStruct(e.shape, e.dtype) for e in exchange],
        scratch_shapes=[pltpu.VMEM((D_CAT, D), F32)] + (_sem_shapes(7 * n_x, n_x) if n_x else []),
        compiler_params=_cp(1),
    )(dxn, o, p, p, ymix, kv, wout, npost, *exchange)


def _gmlp_chunk(us, vs, lnws, lnbs, wss, bss):
    gv = [jax.nn.gelu(v) for v in vs]
    mean = sum(jnp.sum(v, axis=-1, keepdims=True) for v in gv) / D
    cen = [v - mean for v in gv]
    var = sum(jnp.sum(c * c, axis=-1, keepdims=True) for c in cen) / D
    rstd = lax.rsqrt(var + EPS)
    row = lax.broadcasted_iota(jnp.int32, (GM_CHUNK, GM_CHUNK), 0)
    col = lax.broadcasted_iota(jnp.int32, (GM_CHUNK, GM_CHUNK), 1)
    ys = []
    for g in range(GM_GROUPS):
        vn = cen[g] * rstd * lnws[g] + lnbs[g]
        sp = _nn(jnp.where(row >= col, wss[g], 0.0), vn) + bss[g]
        ys.append(jax.nn.gelu(us[g]) * sp)
    return ys


def _split_cols(v, n, width=HEAD):
    return [v[:, k * width:(k + 1) * width] for k in range(n)]


def _gmlp_operands(u_ref, v_ref, lnw_ref, lnb_ref, ws_ref, bst_ref, r0):
    us = _split_cols(u_ref[r0:r0 + GM_CHUNK, :].astype(F32), GM_GROUPS)
    vs = _split_cols(v_ref[r0:r0 + GM_CHUNK, :].astype(F32), GM_GROUPS)
    lnws = _split_cols(lnw_ref[...], GM_GROUPS)
    lnbs = _split_cols(lnb_ref[...], GM_GROUPS)
    wss = [ws_ref[g] for g in range(GM_GROUPS)]
    bst = bst_ref[...]
    bss = [jnp.sum(bst * _onehot_row(HEAD, g), axis=1, keepdims=True) for g in range(GM_GROUPS)]
    return us, vs, lnws, lnbs, wss, bss


def _gmlp_rows(t):
    return min(GM_CHUNKS_PER_STEP, t // GM_CHUNK) * GM_CHUNK


def _gmlp_specs(rows):
    return [pl.BlockSpec((rows, D), lambda i: (i, ZQ // D)), pl.BlockSpec((rows, D), lambda i: (i, ZQ // D + 1)),
            _full((1, D)), _full((1, D)), _full((GM_GROUPS, GM_CHUNK, GM_CHUNK)), _full((GM_CHUNK, HEAD))]


def _gmlp_fwd(p, lnw, lnb, ws, bst, name):
    t = p.shape[0]
    rows = _gmlp_rows(t)

    def body(u_ref, v_ref, lnw_ref, lnb_ref, ws_ref, bst_ref, y_ref):
        for r0 in range(0, rows, GM_CHUNK):
            ys = _gmlp_chunk(*_gmlp_operands(u_ref, v_ref, lnw_ref, lnb_ref, ws_ref, bst_ref, r0))
            for g in range(GM_GROUPS):
                y_ref[r0:r0 + GM_CHUNK, g * HEAD:(g + 1) * HEAD] = ys[g].astype(ACT)

    return _pcall(body, name=name, grid=(t // rows,), in_specs=_gmlp_specs(rows),
                  out_specs=pl.BlockSpec((rows, D), lambda i: (i, 0)),
                  out_shape=jax.ShapeDtypeStruct((t, D), ACT), compiler_params=_cp(1))(p, p, lnw, lnb, ws, bst)


def _gmlp_bwd(p, dy, lnw, lnb, ws, bst, name, exchange=(), bcast=()):
    t = p.shape[0]
    rows = _gmlp_rows(t)

    def body(u_ref, v_ref, lnw_ref, lnb_ref, ws_ref, bst_ref, dy_ref, duv_ref, dlnw_ref, dlnb_ref, dws_ref, dbst_ref):
        @pl.when(pl.program_id(0) == 0)
        def _():
            dlnw_ref[...] = jnp.zeros_like(dlnw_ref)
            dlnb_ref[...] = jnp.zeros_like(dlnb_ref)
            dws_ref[...] = jnp.zeros_like(dws_ref)
            dbst_ref[...] = jnp.zeros_like(dbst_ref)

        for r0 in range(0, rows, GM_CHUNK):
            ops = _gmlp_operands(u_ref, v_ref, lnw_ref, lnb_ref, ws_ref, bst_ref, r0)
            _, vjp = jax.vjp(_gmlp_chunk, *ops)
            dus, dvs, dlnws, dlnbs, dwss, dbss = vjp(_split_cols(dy_ref[r0:r0 + GM_CHUNK, :].astype(F32), GM_GROUPS))
            dbst = jnp.zeros((GM_CHUNK, HEAD), F32)
            for g in range(GM_GROUPS):
                lo = g * HEAD
                duv_ref[r0:r0 + GM_CHUNK, lo:lo + HEAD] = dus[g].astype(ACT)
                duv_ref[r0:r0 + GM_CHUNK, D + lo:D + lo + HEAD] = dvs[g].astype(ACT)
                dlnw_ref[:, lo:lo + HEAD] += dlnws[g]
                dlnb_ref[:, lo:lo + HEAD] += dlnbs[g]
                dws_ref[g] += dwss[g]
                dbst = dbst + dbss[g] * _onehot_row(HEAD, g)
            dbst_ref[...] += dbst

    call = dict(
        name=name, grid=(t // rows,),
        in_specs=_gmlp_specs(rows) + [pl.BlockSpec((rows, D), lambda i: (i, 0))],
        out_specs=[pl.BlockSpec((rows, 2 * D), lambda i: (i, 0)), _full((1, D)), _full((1, D)),
                   _full((GM_GROUPS, GM_CHUNK, GM_CHUNK)), _full((GM_CHUNK, HEAD))],
        out_shape=[jax.ShapeDtypeStruct((t, 2 * D), ACT), jax.ShapeDtypeStruct((1, D), F32),
                   jax.ShapeDtypeStruct((1, D), F32), jax.ShapeDtypeStruct((GM_GROUPS, GM_CHUNK, GM_CHUNK), F32),
                   jax.ShapeDtypeStruct((GM_CHUNK, HEAD), F32)],
        compiler_params=_cp(1))
    args = (p, p, lnw, lnb, ws, bst, dy)
    if not exchange and not bcast:
        return _pcall(body, **call)(*args), ()
    return _pcall_hosting(body, args, exchange, bcast, **call)


def _prev_halo(tm, col):
    return pl.BlockSpec((HALO, D), lambda i: (jnp.maximum(i * (tm // HALO) - 1, 0), col))


def _next_halo(tm, col, n_tiles):
    return pl.BlockSpec((HALO, D), lambda i: (jnp.minimum(i + 1, n_tiles - 1) * (tm // HALO), col))


def _taps_back(ext, w, width):
    acc = None
    for k in range(width):
        s = width - 1 - k
        term = w[k:k + 1, :] * (pltpu.roll(ext, s, 0) if s else ext)[HALO:, :]
        acc = term if acc is None else acc + term
    return acc


def _taps_fwd(ext, w, width, n):
    rows = ext.shape[0]
    acc = None
    for k in range(width):
        s = width - 1 - k
        term = w[k:k + 1, :] * (pltpu.roll(ext, rows - s, 0) if s else ext)[0:n, :]
        acc = term if acc is None else acc + term
    return acc


def _sconv_fwd(p, cw, name):
    t = p.shape[0]
    tm = min(512, t)

    def body(b_ref, c_ref, h_ref, cp_ref, hp_ref, w_ref, y_ref):
        first = pl.program_id(0) == 0
        prev = jnp.where(first, 0.0, cp_ref[...].astype(F32) * hp_ref[...].astype(F32))
        ext = jnp.concatenate([prev, c_ref[...].astype(F32) * h_ref[...].astype(F32)], axis=0)
        y_ref[...] = (b_ref[...].astype(F32) * _taps_back(ext, w_ref[...], 3)).astype(ACT)

    c0 = ZQ // D
    tile = [pl.BlockSpec((tm, D), lambda i, c=c: (i, c)) for c in (c0, c0 + 1, c0 + 2)]
    return _pcall(body, name=name, grid=(t // tm,),
                  in_specs=tile + [_prev_halo(tm, c0 + 1), _prev_halo(tm, c0 + 2), _full((HALO, D))],
                  out_specs=pl.BlockSpec((tm, D), lambda i: (i, 0)),
                  out_shape=jax.ShapeDtypeStruct((t, D), ACT), compiler_params=_cp(1))(p, p, p, p, p, cw)


def _sconv_bwd(p, dy, cw, name):
    t = p.shape[0]
    tm = min(512, t)
    n_tiles = t // tm

    def body(b_ref, c_ref, h_ref, cp_ref, hp_ref, bn_ref, dy_ref, dyn_ref, w_ref, d_ref, dw_ref):
        i = pl.program_id(0)

        @pl.when(i == 0)
        def _():
            dw_ref[...] = jnp.zeros_like(dw_ref)

        w = w_ref[...]
        bv, cv, hv = b_ref[...].astype(F32), c_ref[...].astype(F32), h_ref[...].astype(F32)
        dyv = dy_ref[...].astype(F32)
        prev = jnp.where(i == 0, 0.0, cp_ref[...].astype(F32) * hp_ref[...].astype(F32))
        ext = jnp.concatenate([prev, cv * hv], axis=0)
        conv = _taps_back(ext, w, 3)
        dconv = dyv * bv
        nxt = jnp.where(i == n_tiles - 1, 0.0, dyn_ref[...].astype(F32) * bn_ref[...].astype(F32))
        dc = _taps_fwd(jnp.concatenate([dconv, nxt], axis=0), w, 3, tm)
        d_ref[:, 0:D] = (dyv * conv).astype(ACT)
        d_ref[:, D:2 * D] = (dc * hv).astype(ACT)
        d_ref[:, 2 * D:3 * D] = (dc * cv).astype(ACT)
        for k in range(3):
            s = 2 - k
            shifted = (pltpu.roll(ext, s, 0) if s else ext)[HALO:, :]
            dw_ref[k:k + 1, :] += jnp.sum(dconv * shifted, axis=0, keepdims=True)

    c0 = ZQ // D
    tile = [pl.BlockSpec((tm, D), lambda i, c=c: (i, c)) for c in (c0, c0 + 1, c0 + 2)]
    return _pcall(
        body, name=name, grid=(n_tiles,),
        in_specs=tile + [_prev_halo(tm, c0 + 1), _prev_halo(tm, c0 + 2), _next_halo(tm, c0, n_tiles),
                         pl.BlockSpec((tm, D), lambda i: (i, 0)), _next_halo(tm, 0, n_tiles), _full((HALO, D))],
        out_specs=[pl.BlockSpec((tm, 3 * D), lambda i: (i, 0)), _full((HALO, D))],
        out_shape=[jax.ShapeDtypeStruct((t, 3 * D), ACT), jax.ShapeDtypeStruct((HALO, D), F32)],
        compiler_params=_cp(1),
    )(p, p, p, p, p, p, dy, dy, cw)


def _l2_heads(s, scale):
    outs, rs = [], []
    for hh in range(DN_HEADS):
        blk = s[:, hh * HEAD:(hh + 1) * HEAD]
        r = lax.rsqrt(jnp.sum(blk * blk, axis=-1, keepdims=True) + EPS)
        outs.append(blk * (r * scale))
        rs.append(r)
    return outs, rs


_QKV_SCALE = (HEAD ** -0.5, 1.0, None)


def _qkv_fwd(p, cw, name):
    t = p.shape[0]
    tm = min(256, t)

    def body(q_ref, k_ref, v_ref, qp_ref, kp_ref, vp_ref, w_ref, o_ref):
        first = pl.program_id(0) == 0
        for part, (ref, pref) in enumerate(((q_ref, qp_ref), (k_ref, kp_ref), (v_ref, vp_ref))):
            prev = jnp.where(first, 0.0, pref[...].astype(F32))
            ext = jnp.concatenate([prev, ref[...].astype(F32)], axis=0)
            s = _silu(_taps_back(ext, w_ref[:, part * D:(part + 1) * D], 4))
            if _QKV_SCALE[part] is None:
                o_ref[:, part * D:(part + 1) * D] = s.astype(ACT)
            else:
                outs, _ = _l2_heads(s, _QKV_SCALE[part])
                for hh in range(DN_HEADS):
                    o_ref[:, part * D + hh * HEAD:part * D + (hh + 1) * HEAD] = outs[hh].astype(ACT)

    c0 = ZQ // D
    tile = [pl.BlockSpec((tm, D), lambda i, c=c: (i, c)) for c in (c0, c0 + 1, c0 + 2)]
    halo = [_prev_halo(tm, c) for c in (c0, c0 + 1, c0 + 2)]
    return _pcall(body, name=name, grid=(t // tm,), in_specs=tile + halo + [_full((HALO, 3 * D))],
                  out_specs=pl.BlockSpec((tm, 3 * D), lambda i: (i, 0)),
                  out_shape=jax.ShapeDtypeStruct((t, 3 * D), ACT), compiler_params=_cp(1))(p, p, p, p, p, p, cw)


def _qkv_bwd(p, dqkv, cw, name):
    t = p.shape[0]
    tm = min(256, t)
    n_tiles = t // tm

    def body(*refs):
        tiles, prevs, nexts = refs[0:3], refs[3:6], refs[6:9]
        d_tiles, d_nexts = refs[9:12], refs[12:15]
        w_ref, o_ref, dw_ref = refs[15:]
        i = pl.program_id(0)

        @pl.when(i == 0)
        def _():
            dw_ref[...] = jnp.zeros_like(dw_ref)

        for part in range(3):
            w = w_ref[:, part * D:(part + 1) * D]
            prev = jnp.where(i == 0, 0.0, prevs[part][...].astype(F32))
            ext = jnp.concatenate([prev, tiles[part][...].astype(F32), nexts[part][...].astype(F32)], axis=0)
            xc = _taps_back(ext, w, 4)
            dout = jnp.concatenate([d_tiles[part][...].astype(F32), d_nexts[part][...].astype(F32)], axis=0)
            s = _silu(xc)
            if _QKV_SCALE[part] is None:
                ds = dout
            else:
                scale = _QKV_SCALE[part]
                pieces = []
                for hh in range(DN_HEADS):
                    blk = s[:, hh * HEAD:(hh + 1) * HEAD]
                    dblk = dout[:, hh * HEAD:(hh + 1) * HEAD]
                    r = lax.rsqrt(jnp.sum(blk * blk, axis=-1, keepdims=True) + EPS)
                    pieces.append(scale * r * (dblk - blk * (r * r) * jnp.sum(dblk * blk, axis=-1, keepdims=True)))
                ds = jnp.concatenate(pieces, axis=1)
            dxc = ds * _dsilu(xc)
            row = lax.broadcasted_iota(jnp.int32, (tm + HALO, 1), 0)
            dxc = jnp.where(jnp.logical_and(i == n_tiles - 1, row >= tm), 0.0, dxc)
            o_ref[:, part * D:(part + 1) * D] = _taps_fwd(dxc, w, 4, tm).astype(ACT)
            for k in range(4):
                sh = 3 - k
                shifted = (pltpu.roll(ext, sh, 0) if sh else ext)[HALO:HALO + tm, :]
                dw_ref[k:k + 1, part * D:(part + 1) * D] += jnp.sum(dxc[0:tm, :] * shifted, axis=0, keepdims=True)

    c0 = ZQ // D
    cols = (c0, c0 + 1, c0 + 2)
    tile = [pl.BlockSpec((tm, D), lambda i, c=c: (i, c)) for c in cols]
    dtile = [pl.BlockSpec((tm, D), lambda i, c=c: (i, c)) for c in range(3)]
    in_specs = (tile + [_prev_halo(tm, c) for c in cols] + [_next_halo(tm, c, n_tiles) for c in cols]
                + dtile + [_next_halo(tm, c, n_tiles) for c in range(3)] + [_full((HALO, 3 * D))])
    return _pcall(
        body, name=name, grid=(n_tiles,), in_specs=in_specs,
        out_specs=[pl.BlockSpec((tm, 3 * D), lambda i: (i, 0)), _full((HALO, 3 * D))],
        out_shape=[jax.ShapeDtypeStruct((t, 3 * D), ACT), jax.ShapeDtypeStruct((HALO, 3 * D), F32)],
        compiler_params=_cp(1),
    )(*([p] * 9), *([dqkv] * 6), cw)


def _tri_masks(n):
    row = lax.broadcasted_iota(jnp.int32, (n, n), 0)
    col = lax.broadcasted_iota(jnp.int32, (n, n), 1)
    return row, col


@jax.custom_vjp
def _unit_lower_inverses(mats):
    n = DN_CHUNK
    row, col = _tri_masks(n)
    eye = (row == col).astype(F32)
    same16 = (row // 16) == (col // 16)
    same32 = (row // 32) == (col // 32)
    pw = [jnp.where(same16, a, 0.0) for a in mats]
    x = [eye - p for p in pw]
    for _ in range(3):
        pw = [_hnn(p, p) for p in pw]
        x = [_hnn(xi, eye + p) for xi, p in zip(x, pw)]
    for keep in (jnp.logical_and(same32, jnp.logical_not(same16)), jnp.logical_not(same32)):
        inner = [_hnn(jnp.where(keep, a, 0.0), xi) for a, xi in zip(mats, x)]
        x = [xi - _hnn(xi, y) for xi, y in zip(x, inner)]
    return tuple(x)


def _uli_fwd(mats):
    t = _unit_lower_inverses(mats)
    return t, t


def _uli_bwd(ts, gs):
    inner = [_nt(g, t) for g, t in zip(gs, ts)]
    return (tuple(-_tn(t, y) for t, y in zip(ts, inner)),)


_unit_lower_inverses.defvjp(_uli_fwd, _uli_bwd)


@jax.custom_vjp
def _known_inverses(mats, ts):
    return ts


_known_inverses.defvjp(lambda mats, ts: (ts, ts),
                       lambda ts, gs: (_uli_bwd(ts, gs)[0], tuple(jnp.zeros_like(t) for t in ts)))


def _pick_col(m, k):
    return jnp.sum(m * _onehot_row(m.shape[1], k), axis=1, keepdims=True)


def _pick_row(m, k):
    hot = (lax.broadcasted_iota(jnp.int32, (m.shape[0], 1), 0) == k).astype(F32)
    return jnp.sum(m * hot, axis=0, keepdims=True)


def _delta_chunk(states, qs, ks, vs, ab, alog, dtb, onw, known_inverses=None):
    n = DN_CHUNK
    heads = range(DN_HEADS)
    row, col = _tri_masks(n)
    incl = row >= col
    lane = lax.broadcasted_iota(jnp.int32, (1, HEAD), 1)
    g_all = jnp.where(lane < DN_HEADS, -jnp.exp(alog) * jax.nn.softplus(ab + dtb), 0.0)
    c_cols = _hnn(incl.astype(F32), g_all)
    c_rows = _htn(g_all, (row <= col).astype(F32))
    g_tot = jnp.sum(g_all, axis=0, keepdims=True)
    beta_all = jax.nn.sigmoid(ab)
    ccol = [_pick_col(c_cols, h) for h in heads]
    crow = [_pick_row(c_rows, h) for h in heads]
    gl = [_pick_col(g_tot, h) for h in heads]
    beta = [_pick_col(beta_all, DN_HEADS + h) for h in heads]
    decay = [jnp.exp(jnp.where(incl, ccol[h] - crow[h], -1e30)) for h in heads]
    eg = [jnp.exp(ccol[h]) for h in heads]
    kb = [ks[h] * beta[h] for h in heads]
    amat = [jnp.where(row > col, _nt(kb[h], ks[h]) * decay[h], 0.0) for h in heads]
    if known_inverses is None:
        tmat = _unit_lower_inverses(tuple(amat))
    else:
        tmat = _known_inverses(tuple(amat), tuple(known_inverses))
    u = [_nn(tmat[h], vs[h] * beta[h]) for h in heads]
    w = [_nn(tmat[h], kb[h] * eg[h]) for h in heads]
    qk = [_nt(qs[h], ks[h]) * decay[h] for h in heads]
    v_new = [u[h] - _nn(w[h], states[h]) for h in heads]
    o = [_nn(qs[h] * eg[h], states[h]) + _nn(qk[h], v_new[h]) for h in heads]
    new_states = [states[h] * jnp.exp(gl[h]) + _tn(ks[h] * jnp.exp(gl[h] - ccol[h]), v_new[h]) for h in heads]
    ys = [o[h] * lax.rsqrt(jnp.mean(o[h] * o[h], axis=-1, keepdims=True) + EPS) * onw for h in heads]
    return (ys, new_states), tmat


def _head_cols(ref, r0):
    return [ref[r0:r0 + DN_CHUNK, h * HEAD:(h + 1) * HEAD].astype(F32) for h in range(DN_HEADS)]


def _delta_rows(t):
    return min(DN_CHUNKS_PER_STEP, t // DN_CHUNK) * DN_CHUNK


def _delta_fwd(qkv, ab, alog, dtb, onw, name):
    t = qkv.shape[0]
    rows = _delta_rows(t)
    per_step = rows // DN_CHUNK

    def body(q_ref, k_ref, v_ref, ab_ref, alog_ref, dtb_ref, onw_ref, y_ref, keep_ref, inv_ref, state):
        @pl.when(pl.program_id(0) == 0)
        def _():
            state[...] = jnp.zeros_like(state)

        s = [state[hh] for hh in range(DN_HEADS)]
        for c in range(per_step):
            r0 = c * DN_CHUNK
            (ys, s1), tmat = _delta_chunk(s, _head_cols(q_ref, r0), _head_cols(k_ref, r0), _head_cols(v_ref, r0),
                                          ab_ref[r0:r0 + DN_CHUNK, :], alog_ref[...], dtb_ref[...], onw_ref[...])
            for hh in range(DN_HEADS):
                keep_ref[c, hh] = s[hh]
                inv_ref[c, hh] = tmat[hh]
                y_ref[r0:r0 + DN_CHUNK, hh * HEAD:(hh + 1) * HEAD] = ys[hh].astype(ACT)
            s = s1
        for hh in range(DN_HEADS):
            state[hh] = s[hh]

    block = [pl.BlockSpec((rows, D), lambda i, c=c: (i, c)) for c in range(3)]
    nc = t // DN_CHUNK
    return _pcall(
        body, name=name, grid=(t // rows,),
        in_specs=block + [pl.BlockSpec((rows, HEAD), lambda i: (i, 0)), _full((1, HEAD)), _full((1, HEAD)),
                          _full((1, HEAD))],
        out_specs=[pl.BlockSpec((rows, D), lambda i: (i, 0)),
                   pl.BlockSpec((per_step, DN_HEADS, HEAD, HEAD), lambda i: (i, 0, 0, 0)),
                   pl.BlockSpec((per_step, DN_HEADS, DN_CHUNK, DN_CHUNK), lambda i: (i, 0, 0, 0))],
        out_shape=[jax.ShapeDtypeStruct((t, D), ACT), jax.ShapeDtypeStruct((nc, DN_HEADS, HEAD, HEAD), F32),
                   jax.ShapeDtypeStruct((nc, DN_HEADS, DN_CHUNK, DN_CHUNK), F32)],
        scratch_shapes=[pltpu.VMEM((DN_HEADS, HEAD, HEAD), F32)], compiler_params=_cp(1),
    )(qkv, qkv, qkv, ab, alog, dtb, onw)


def _delta_bwd(qkv, ab, alog, dtb, onw, keep, inv, dy, name):
    t = qkv.shape[0]
    rows = _delta_rows(t)
    per_step, n_steps = rows // DN_CHUNK, t // rows

    def body(q_ref, k_ref, v_ref, ab_ref, alog_ref, dtb_ref, onw_ref, keep_ref, inv_ref, dy_ref,
             dqkv_ref, dab_ref, dalog_ref, ddtb_ref, donw_ref, dstate):
        @pl.when(pl.program_id(0) == 0)
        def _():
            dstate[...] = jnp.zeros_like(dstate)
            dalog_ref[...] = jnp.zeros_like(dalog_ref)
            ddtb_ref[...] = jnp.zeros_like(ddtb_ref)
            donw_ref[...] = jnp.zeros_like(donw_ref)

        ds = [dstate[hh] for hh in range(DN_HEADS)]
        for c in reversed(range(per_step)):
            r0 = c * DN_CHUNK
            s0 = [keep_ref[c, hh] for hh in range(DN_HEADS)]
            known = [inv_ref[c, hh] for hh in range(DN_HEADS)]
            _, vjp, _ = jax.vjp(functools.partial(_delta_chunk, known_inverses=known), s0, _head_cols(q_ref, r0),
                                _head_cols(k_ref, r0), _head_cols(v_ref, r0), ab_ref[r0:r0 + DN_CHUNK, :], alog_ref[...],
                                dtb_ref[...], onw_ref[...], has_aux=True)
            ds, dq, dk, dv, dab, dal, ddt, don = vjp((_head_cols(dy_ref, r0), ds))
            for hh in range(DN_HEADS):
                lo = hh * HEAD
                dqkv_ref[r0:r0 + DN_CHUNK, lo:lo + HEAD] = dq[hh].astype(ACT)
                dqkv_ref[r0:r0 + DN_CHUNK, D + lo:D + lo + HEAD] = dk[hh].astype(ACT)
                dqkv_ref[r0:r0 + DN_CHUNK, 2 * D + lo:2 * D + lo + HEAD] = dv[hh].astype(ACT)
            dab_ref[r0:r0 + DN_CHUNK, :] = dab
            dalog_ref[...] += dal
            ddtb_ref[...] += ddt
            donw_ref[...] += don
        for hh in range(DN_HEADS):
            dstate[hh] = ds[hh]

    rev = lambda i: n_steps - 1 - i
    block = [pl.BlockSpec((rows, D), lambda i, c=c: (rev(i), c)) for c in range(3)]
    small = jax.ShapeDtypeStruct((1, HEAD), F32)
    return _pcall(
        body, name=name, grid=(n_steps,),
        in_specs=block + [pl.BlockSpec((rows, HEAD), lambda i: (rev(i), 0)), _full((1, HEAD)), _full((1, HEAD)),
                          _full((1, HEAD)), pl.BlockSpec((per_step, DN_HEADS, HEAD, HEAD), lambda i: (rev(i), 0, 0, 0)),
                          pl.BlockSpec((per_step, DN_HEADS, DN_CHUNK, DN_CHUNK), lambda i: (rev(i), 0, 0, 0)),
                          pl.BlockSpec((rows, D), lambda i: (rev(i), 0))],
        out_specs=[pl.BlockSpec((rows, 3 * D), lambda i: (rev(i), 0)),
                   pl.BlockSpec((rows, HEAD), lambda i: (rev(i), 0)), _full((1, HEAD)), _full((1, HEAD)),
                   _full((1, HEAD))],
        out_shape=[jax.ShapeDtypeStruct((t, 3 * D), ACT), jax.ShapeDtypeStruct((t, HEAD), F32), small, small, small],
        scratch_shapes=[pltpu.VMEM((DN_HEADS, HEAD, HEAD), F32)], compiler_params=_cp(1),
    )(qkv, qkv, qkv, ab, alog, dtb, onw, keep, inv, dy)


def _loss_head(y, target):
    t = y.shape[0]
    tm = min(512, t)

    def body(y_ref, t_ref, l_ref, dy_ref):
        @pl.when(pl.program_id(0) == 0)
        def _():
            l_ref[...] = jnp.zeros_like(l_ref)

        diff = y_ref[...] - t_ref[...]
        dy_ref[...] = diff * (1.0 / D)
        l_ref[...] += 0.5 * jnp.sum(jnp.sum(diff * diff, axis=-1, keepdims=True) * (1.0 / D), axis=0, keepdims=True)

    row = pl.BlockSpec((tm, D), lambda i: (i, 0))
    return _pcall(body, name="loss_head", grid=(t // tm,), in_specs=[row, row], out_specs=[_full((8, 128)), row],
                  out_shape=[jax.ShapeDtypeStruct((8, 128), F32), jax.ShapeDtypeStruct((t, D), F32)],
                  compiler_params=_cp(1))(y, target)


_SMALL_SHARDED = (("a_ln_w", (2, 128), 1), ("a_ln_b", (2, 128), 1), ("b_conv_w", (1, 3, 128), 2),
                  ("c_conv_w", (1, 4, 384), 2))
_REPLICATED = (("mem_norm_w", (1024,)), ("norm_pre", (4, 1024)), ("norm_post", (4, 1024)),
               ("a_b_s", (2, 8, 128)), ("c_a_log", (1, 8)), ("c_dt_bias", (1, 8)), ("c_o_norm_w", (1, 128)))


def _layer_shards(given, prefix):
    w_out = given[prefix + "w_out"]
    w_ins = [given[prefix + "a_w_in"][0], given[prefix + "b_w_in"], given[prefix + "c_w_in"], given[prefix + "a_w_in"][1]]
    return [[w_out[i], w_ins[i]] for i in range(4)]


def _as_matmul_operand(w):
    return w.reshape(w.shape[-2:]).astype(MXU)


def _as_rows(shape):
    return (math.prod(shape[:-1]) if len(shape) > 1 else 1, shape[-1])


def _row_table(shapes):
    table, at = [], 0
    for shape in shapes:
        rows, cols = _as_rows(shape)
        table.append((at, rows, cols))
        at += rows
    return table


def _pack_rows(arrays, shapes, rows, lead=()):
    n_lead = len(lead)
    parts = [jnp.pad(a.reshape(lead + _as_rows(s)), [(0, 0)] * (n_lead + 1) + [(0, D - s[-1])])
             for a, s in zip(arrays, shapes)]
    block = jnp.concatenate(parts, axis=n_lead)
    return jnp.pad(block, [(0, 0)] * n_lead + [(0, rows - block.shape[n_lead]), (0, 0)])


def _unpack_rows(packed, shapes, lead=()):
    n_lead = len(lead)
    return [packed[(slice(None),) * n_lead + (slice(r0, r0 + nr), slice(0, nc))].reshape(lead + s)
            for (r0, nr, nc), s in zip(_row_table(shapes), shapes)]


def _join_shards(blocks, axis):
    moved = jnp.moveaxis(blocks, 0, axis)
    shape = moved.shape
    return moved.reshape(shape[:axis] + (shape[axis] * shape[axis + 1],) + shape[axis + 2:])


def _split_shards(full, axis):
    shape = full.shape
    split = full.reshape(shape[:axis] + (N_DEV, shape[axis] // N_DEV) + shape[axis + 1:])
    return jnp.moveaxis(split, axis, 0)


_A_COLS = ((2560, 1536), (2048, 512), (0, 2048))
_BC_COLS = ((3584, 1536), (3072, 512), (0, 3072))
_C_COLS = ((3600, 1536), (3088, 512), (0, 3072))


def _reorder_cols(w, cols):
    return jnp.concatenate([w[:, s:s + n] for s, n in cols], axis=1)


def _restore_cols(pieces_in_my_order, cols, extra=()):
    placed = sorted(list(zip([s for s, _ in cols], pieces_in_my_order)) + list(extra), key=lambda sp: sp[0])
    return jnp.concatenate([piece for _, piece in placed], axis=1)


def kernel(x, mem, mem_norm_w, w_mem_kv, norm_pre, norm_post, w_out, a_w_in, a_ln_w, a_ln_b, a_w_s, a_b_s, b_w_in, b_conv_w, c_w_in, c_conv_w, c_a_log, c_dt_bias, c_o_norm_w, loss_target, m_mem_norm_w, m_w_mem_kv, m_norm_pre, m_norm_post, m_w_out, m_a_w_in, m_a_ln_w, m_a_ln_b, m_a_w_s, m_a_b_s, m_b_w_in, m_b_conv_w, m_c_w_in, m_c_conv_w, m_c_a_log, m_c_dt_bias, m_c_o_norm_w, v_mem_norm_w, v_w_mem_kv, v_norm_pre, v_norm_post, v_w_out, v_a_w_in, v_a_ln_w, v_a_ln_b, v_a_w_s, v_a_b_s, v_b_w_in, v_b_conv_w, v_c_w_in, v_c_conv_w, v_c_a_log, v_c_dt_bias, v_c_o_norm_w):
    given = dict(locals())
    x0 = x[0]
    mem0 = mem[0]
    target = loss_target[0]

    w_sh, m_sh, v_sh = (_layer_shards(given, pre) for pre in ("", "m_", "v_"))
    small_names = [n for n, _, _ in _SMALL_SHARDED]
    small_shapes = [s for _, s, _ in _SMALL_SHARDED]
    repl_names = [n for n, _ in _REPLICATED]
    repl_shapes = [s for _, s in _REPLICATED]
    small = _pack_rows([given[n] for n in small_names], small_shapes, R_SMALL)
    g_in0, = _all_gather([_as_matmul_operand(w_sh[0][1])], "gather_weights")
    w_in = [_reorder_cols(_join_shards(g_in0, 1), _A_COLS)]
    wouts, w_cab = [], None
    ws = [a_w_s[j] for j in range(2)]
    bst = [jnp.pad(a_b_s[j].T, ((0, 0), (0, HEAD - GM_GROUPS))) for j in range(2)]
    alog = jnp.pad(c_a_log, ((0, 0), (0, HEAD - DN_HEADS)))
    dtb = jnp.pad(c_dt_bias, ((0, 0), (0, HEAD - DN_HEADS)))
    onw = c_o_norm_w
    mw = mem_norm_w[None, :]

    xs, saved = [x0], []
    for i in range(4):
        kind = i % 3
        npre, npost = norm_pre[i][None, :], norm_post[i][None, :]
        ahead = [_as_matmul_operand(w) for w in w_sh[i + 1][:1 if i == 0 else 2]] if i < 3 else []
        if i == 0:
            ahead = [w_mem_kv.astype(MXU), _as_matmul_operand(w_sh[0][0]), small] + ahead
        res = _proj_fwd(xs[i], npre, w_in[i], w_cab if kind == 2 else None, f"proj_fwd_{i}", gather=ahead)
        if i == 0:
            g_wkv, g_wout0, g_small = res[2:5]
            small_full = _unpack_rows(g_small, small_shapes, lead=(N_DEV,))
            full = {n: _join_shards(blocks, ax) for (n, _, ax), blocks in zip(_SMALL_SHARDED, small_full)}
            wkv = g_wkv.reshape(D, 2 * D_XA)
            wouts.append(g_wout0.reshape(D_CAT, D))
            lnw = [full["a_ln_w"][j][None, :] for j in range(2)]
            lnb = [full["a_ln_b"][j][None, :] for j in range(2)]
            cw_b = jnp.pad(full["b_conv_w"][0], ((0, HALO - 3), (0, 0)))
            cw_c = jnp.pad(full["c_conv_w"][0], ((0, HALO - 4), (0, 0)))
            kv = _memkv_fwd(mem0, mw, wkv).astype(MXU)
        if ahead:
            wouts.append(res[-1 if i == 0 else -2].reshape(D_CAT, D))
            g_in = res[-1] if i > 0 else None
        if kind == 2:
            p, h, ab = res[:3]
            qkv = _qkv_fwd(p, cw_c, f"qkv_fwd_{i}")
            ymix, keep, inv = _delta_fwd(qkv, ab, alog, dtb, onw, f"delta_fwd_{i}")
            extra = (qkv, ab, keep, inv)
        else:
            p, h = res[:2]
            if kind == 0:
                ymix = _gmlp_fwd(p, lnw[i // 3], lnb[i // 3], ws[i // 3], bst[i // 3], f"gmlp_fwd_{i}")
            else:
                ymix = _sconv_fwd(p, cw_b, f"sconv_fwd_{i}")
            extra = ()
        res = _tail_fwd(p, ymix, xs[i], kv, wouts[i], npost, f"tail_fwd_{i}",
                        gather=[_as_matmul_operand(w_sh[1][1])] if i == 0 else [])
        xn, o = res[:2]
        if i == 0:
            g_in = res[2]
        if i + 1 == 2:
            c_full = jnp.concatenate([g_in[d] for d in range(N_DEV)], axis=1)
            w_in.append(_reorder_cols(c_full, _C_COLS))
            w_cab = jnp.pad(c_full[:, 3072:3088], ((0, 0), (0, HEAD - 16)))
        elif i < 3:
            w_in.append(_reorder_cols(_join_shards(g_in, 1), _BC_COLS if i + 1 == 1 else _A_COLS))
        xs.append(xn)
        saved.append((p, h, ymix, o, extra))

    loss_tile, dx = _loss_head(xs[4], target)
    loss = lax.psum(loss_tile[0, 0], ("x", "y", "c"))

    g = {}
    d_npre, d_npost = [None] * 4, [None] * 4
    d_ws, d_bs, d_lnw, d_lnb = [None] * 2, [None] * 2, [None] * 2, [None] * 2
    dkv = None
    pend_win, pend_wout = None, None
    landed_win, landed_wout = [None] * 4, [None] * 4
    for i in reversed(range(4)):
        kind = i % 3
        p, h, ymix, o, extra = saved[i]
        npre, npost = norm_pre[i][None, :], norm_post[i][None, :]
        res = _tail_bwd(dx, o, p, ymix, kv, wouts[i], npost, f"tail_bwd_{i}", exchange=[pend_win] if i < 3 else [])
        dzq, dymix, d_wout, d_npost[i], dkv_i = res[:5]
        if i < 3:
            landed_win[i + 1] = res[5]
        d_wout = d_wout.reshape(N_DEV, D_CAT // N_DEV, D)
        dkv = dkv_i if dkv is None else dkv + dkv_i
        w_zq, w_mix = w_in[i][:, :ZQ], w_in[i][:, ZQ:]
        dw_zq = _matmul_tn(h, dzq, f"dw_zq_{i}")
        if kind == 0:
            j = i // 3
            early_x, early_b = [], []
            if i == 0:
                d_wkv, d_mw = _memkv_bwd(mem0, mw, wkv, dkv)
                early_x = [d_wkv.reshape(N_DEV, 128, D).astype(XCH), d_wout, pend_wout]
                early_b = [d_ws[1].reshape(GM_CHUNK, D).astype(XCH)]
            (dmix, d_lnw[j], d_lnb[j], d_ws[j], dbst), early_landed = _gmlp_bwd(
                p, dymix, lnw[j], lnb[j], ws[j], bst[j], f"gmlp_bwd_{i}", exchange=early_x, bcast=early_b)
            d_bs[j] = dbst[:, :GM_GROUPS].T
            if i == 0:
                g["mem_norm_w"] = d_mw[0]
                g["norm_pre"] = jnp.concatenate([jnp.zeros((1, D), F32)] + d_npre[1:], axis=0)
                g["norm_post"] = jnp.concatenate(d_npost, axis=0)
                g["a_ln_w"] = jnp.concatenate(d_lnw, axis=0)
                g["a_ln_b"] = jnp.concatenate(d_lnb, axis=0)
                g["a_b_s"] = jnp.stack(d_bs)
                e_small = _pack_rows([_split_shards(g[n], ax) for n, _, ax in _SMALL_SHARDED], small_shapes, R_SMALL,
                                     lead=(N_DEV,))
                r_pack = _pack_rows([g[n] for n in repl_names], repl_shapes, R_REPL)
                dw_mix, (l_small, ws0_all, r_all) = _matmul_tn(
                    h, dmix, f"dw_mix_{i}", exchange=[e_small], bcast=[d_ws[0].reshape(GM_CHUNK, D).astype(XCH), r_pack])
            else:
                dw_mix = _matmul_tn(h, dmix, f"dw_mix_{i}")
            d_win = _restore_cols([dw_zq[:, :D_CAT], dw_zq[:, D_CAT:], dw_mix], _A_COLS)
            dps, wparts = [dzq, dmix], [w_zq, w_mix]
        elif kind == 1:
            dmix, dcw = _sconv_bwd(p, dymix, cw_b, f"sconv_bwd_{i}")
            g["b_conv_w"] = dcw[None, :3]
            dw_mix = _matmul_tn(h, dmix, f"dw_mix_{i}")
            d_win = _restore_cols([dw_zq[:, :D_CAT], dw_zq[:, D_CAT:], dw_mix], _BC_COLS)
            dps, wparts = [dzq, dmix], [w_zq, w_mix]
        else:
            qkv, ab, keep, inv = extra
            dqkv, dab, dalog, ddtb, donw = _delta_bwd(qkv, ab, alog, dtb, onw, keep, inv, dymix, f"delta_bwd_{i}")
            dmix, dcw = _qkv_bwd(p, dqkv, cw_c, f"qkv_bwd_{i}")
            g["c_conv_w"] = dcw[None, :4]
            g["c_a_log"], g["c_dt_bias"], g["c_o_norm_w"] = dalog[:, :DN_HEADS], ddtb[:, :DN_HEADS], donw
            dw_mix = _matmul_tn(h, dmix, f"dw_mix_{i}")
            dw_ab = _matmul_tn(h, dab, f"dw_ab_{i}")
            d_win = _restore_cols([dw_zq[:, :D_CAT], dw_zq[:, D_CAT:], dw_mix], _C_COLS, extra=[(3072, dw_ab[:, :16])])
            dps, wparts = [dzq, dmix, dab], [w_zq, w_mix, w_cab]
        width = d_win.shape[1] // N_DEV
        pend_win = jnp.stack([d_win[:, d * width:(d + 1) * width] for d in range(N_DEV)]).astype(XCH)
        if i > 0:
            res = _proj_bwd_x(dps, wparts, xs[i], dx, npre, f"proj_bwd_x_{i}", exchange=[pend_wout] if i < 3 else [])
            dx, d_npre[i] = res[:2]
            if i < 3:
                landed_wout[i + 1] = res[2]
            pend_wout = d_wout

    dx, d_npre0, l_in = _proj_bwd_x(dps, wparts, xs[0], dx, norm_pre[0][None, :], "proj_bwd_x_0", exchange=[pend_win])
    l_wkv, landed_wout[0], landed_wout[1], ws3_all = early_landed
    landed_win[0] = l_in
    npre0_all, = _all_gather([jnp.pad(d_npre0, ((0, HALO - 1), (0, 0)))], "gather_norm_pre0")
    r_all = r_all.at[:, 1, :].set(npre0_all[:, 0, :])

    res = [[_reduce_adamw(parts, w_sh[i][a], m_sh[i][a], v_sh[i][a], f"adamw_{i}_{a}")
            for a, parts in enumerate((landed_wout[i], landed_win[i]))] for i in range(4)]
    res_wkv = _reduce_adamw(l_wkv, w_mem_kv, m_w_mem_kv, v_w_mem_kv, "adamw_w_mem_kv")
    res_small = _reduce_adamw_vectors(
        l_small, small, *(_pack_rows([given[pre + n] for n in small_names], small_shapes, R_SMALL) for pre in ("m_", "v_")),
        small_shapes, "adamw_small")
    res_ws = [_reduce_adamw(parts, *(given[pre + "a_w_s"][j].reshape(GM_CHUNK, D) for pre in ("", "m_", "v_")),
                            f"adamw_a_w_s_{j}") for j, parts in enumerate((ws0_all, ws3_all))]
    res_repl = _reduce_adamw_vectors(
        r_all, *(_pack_rows([given[pre + n] for n in repl_names], repl_shapes, R_REPL) for pre in ("", "m_", "v_")),
        repl_shapes, "adamw_replicated")

    order = ["mem_norm_w", "w_mem_kv", "norm_pre", "norm_post", "w_out", "a_w_in", "a_ln_w", "a_ln_b", "a_w_s", "a_b_s",
             "b_w_in", "b_conv_w", "c_w_in", "c_conv_w", "c_a_log", "c_dt_bias", "c_o_norm_w"]
    outs = [loss, dx[None]]
    for kind in range(4):
        got = dict(zip(repl_names, res_repl[kind]))
        got["a_w_s"] = jnp.stack([r[kind] for r in res_ws]).reshape(a_w_s.shape)
        got.update(zip(small_names, res_small[kind]))
        got["w_mem_kv"] = res_wkv[kind]
        got["w_out"] = jnp.stack([res[i][0][kind] for i in range(4)])
        got["a_w_in"] = jnp.stack([res[0][1][kind], res[3][1][kind]])
        got["b_w_in"] = res[1][1][kind]
        got["c_w_in"] = res[2][1][kind]
        outs += [got[n] for n in order]
    return tuple(outs)
```
